```python
import jax, jax.numpy as jnp
from jax import lax
import numpy as np

D_MODEL = 1024
BATCH = 8
SEQ = 4096
DEPTH = 1

CHUNK = 64
EPS = 1e-6
D_FF = 2816
S5_WIDTH = D_MODEL
S5_GROUP = 16
S5_GROUPS = S5_WIDTH // S5_GROUP
S5_STATE = 64
D_INNER = 2 * D_MODEL
SSD_HEADDIM = 64
SSD_HEADS = D_INNER // SSD_HEADDIM
SSD_GROUPS = 8
SSD_HPG = SSD_HEADS // SSD_GROUPS
SSD_STATE = 128
CONV_K = 4
CONV_DIM = D_INNER + 2 * SSD_GROUPS * SSD_STATE
N_BRANCH = 2
IN_SPLITS = (S5_WIDTH, D_INNER, CONV_DIM, SSD_HEADS, N_BRANCH * D_MODEL)
D_IN = S5_WIDTH + D_INNER + CONV_DIM + SSD_HEADS + N_BRANCH * D_MODEL

kernel_name = "hybrid_s5_ssd_gated_macaron"


def rmsnorm(x, g):
    xf = x.astype(jnp.float32)
    y = xf * lax.rsqrt(jnp.mean(xf * xf, axis=-1, keepdims=True) + EPS)
    return (y * g.astype(jnp.float32)).astype(x.dtype)


def swiglu(h, w_gate, w_up, w_down):
    return (jax.nn.silu(h @ w_gate) * (h @ w_up)) @ w_down


def s5_branch(u, A_re, A_im, log_dt, B_re, B_im, C_re, C_im, d_skip, w_glu, b_glu):
    f32 = jnp.float32
    bsz, L, _ = u.shape
    uf = u.astype(f32).reshape(bsz, L, S5_GROUPS, S5_GROUP)
    dt = jnp.exp(log_dt.astype(f32))[:, None]
    lr, li = A_re.astype(f32), A_im.astype(f32)
    mag = jnp.exp(lr * dt)
    ar, ai = mag * jnp.cos(li * dt), mag * jnp.sin(li * dt)
    den = lr * lr + li * li
    cr = ((ar - 1.0) * lr + ai * li) / den
    ci = (ai * lr - (ar - 1.0) * li) / den
    Br, Bi = B_re.astype(f32), B_im.astype(f32)
    bbr = cr[..., None] * Br - ci[..., None] * Bi
    bbi = cr[..., None] * Bi + ci[..., None] * Br
    bu_r = jnp.einsum('blgm,gnm->blgn', uf, bbr)
    bu_i = jnp.einsum('blgm,gnm->blgn', uf, bbi)
    a_r = jnp.broadcast_to(ar, bu_r.shape)
    a_i = jnp.broadcast_to(ai, bu_i.shape)

    def combine(e1, e2):
        a1r, a1i, b1r, b1i = e1
        a2r, a2i, b2r, b2i = e2
        return (a2r * a1r - a2i * a1i,
                a2r * a1i + a2i * a1r,
                a2r * b1r - a2i * b1i + b2r,
                a2r * b1i + a2i * b1r + b2i)

    _, _, s_r, s_i = lax.associative_scan(combine, (a_r, a_i, bu_r, bu_i), axis=1)
    y = (jnp.einsum('blgn,gmn->blgm', s_r, C_re.astype(f32))
         - jnp.einsum('blgn,gmn->blgm', s_i, C_im.astype(f32)))
    y = y.reshape(bsz, L, S5_WIDTH) + d_skip.astype(f32) * u.astype(f32)
    g = jax.nn.gelu(y)
    out = g * jax.nn.sigmoid(g @ w_glu.astype(f32) + b_glu.astype(f32))
    return out.astype(u.dtype)


def causal_depthwise_conv(x, w, b):
    y = lax.conv_general_dilated(
        x, w[:, None, :].astype(x.dtype), window_strides=(1,),
        padding=[(CONV_K - 1, 0)], dimension_numbers=('NWC', 'WIO', 'NWC'),
        feature_group_count=CONV_DIM)
    return y + b.astype(x.dtype)


def ssd_branch(z, xbc, dt_raw, conv_w, conv_b, A_log, dt_bias, d_head, norm_w):
    f32 = jnp.float32
    bsz, L, _ = z.shape
    nc = L // CHUNK
    xbc = jax.nn.silu(causal_depthwise_conv(xbc, conv_w, conv_b))
    xs, Bm, Cm = jnp.split(xbc, [D_INNER, D_INNER + SSD_GROUPS * SSD_STATE], axis=-1)
    x = xs.reshape(bsz, nc, CHUNK, SSD_GROUPS, SSD_HPG, SSD_HEADDIM).astype(f32)
    Bm = Bm.reshape(bsz, nc, CHUNK, SSD_GROUPS, SSD_STATE).astype(f32)
    Cm = Cm.reshape(bsz, nc, CHUNK, SSD_GROUPS, SSD_STATE).astype(f32)
    dt = jax.nn.softplus(dt_raw.astype(f32) + dt_bias.astype(f32))
    dt = dt.reshape(bsz, nc, CHUNK, SSD_GROUPS, SSD_HPG)
    A = -jnp.exp(A_log.astype(f32)).reshape(SSD_GROUPS, SSD_HPG)
    a_cum = jnp.cumsum(dt * A, axis=2)
    seg = a_cum[:, :, :, None] - a_cum[:, :, None]
    mask = jnp.tril(jnp.ones((CHUNK, CHUNK), dtype=bool))[:, :, None, None]
    Lmat = jnp.exp(jnp.where(mask, seg, -jnp.inf))
    cb = jnp.einsum('bcign,bcjgn->bcijg', Cm, Bm)
    w = cb[..., None] * Lmat * dt[:, :, None]
    y_diag = jnp.einsum('bcijgk,bcjgkp->bcigkp', w, x)
    decay_to_end = jnp.exp(a_cum[:, :, -1:] - a_cum)
    xw = x * (decay_to_end * dt)[..., None]
    states = jnp.einsum('bclgn,bclgkp->bcgkpn', Bm, xw)
    chunk_decay = jnp.exp(a_cum[:, :, -1])

    def step(h, inp):
        dec, s = inp
        return dec[..., None, None] * h + s, h

    h0 = jnp.zeros((bsz, SSD_GROUPS, SSD_HPG, SSD_HEADDIM, SSD_STATE), f32)
    _, prev = lax.scan(step, h0, (jnp.moveaxis(chunk_decay, 1, 0), jnp.moveaxis(states, 1, 0)))
    prev = jnp.moveaxis(prev, 0, 1)
    y_off = jnp.einsum('bcign,bcgkpn->bcigkp', Cm, prev) * jnp.exp(a_cum)[..., None]
    y = y_diag + y_off + d_head.astype(f32).reshape(SSD_GROUPS, SSD_HPG)[:, :, None] * x
    y = y.reshape(bsz, L, D_INNER).astype(z.dtype)
    return rmsnorm(y * jax.nn.silu(z), norm_w)


def _fwd_setup_inputs(seed: int = 0) -> dict:
    key = jax.random.key(seed)
    ks = iter(jax.random.split(key, 40))
    f32 = jnp.float32

    def nrm(shape, scale):
        return jax.random.normal(next(ks), shape, f32) * scale

    def gain(shape):
        return 1.0 + nrm(shape, 0.02)

    Dp = DEPTH
    x = jax.random.normal(next(ks), (BATCH, SEQ, D_MODEL), f32)
    inp = {"x": x}
    inp["ffn1_norm"] = gain((Dp, D_MODEL))
    inp["ffn1_w_gate"] = nrm((Dp, D_MODEL, D_FF), D_MODEL ** -0.5)
    inp["ffn1_w_up"] = nrm((Dp, D_MODEL, D_FF), D_MODEL ** -0.5)
    inp["ffn1_w_down"] = nrm((Dp, D_FF, D_MODEL), D_FF ** -0.5)
    inp["mix_norm"] = gain((Dp, D_MODEL))
    inp["w_in"] = nrm((Dp, D_MODEL, D_IN), D_MODEL ** -0.5)
    inp["conv_w"] = nrm((Dp, CONV_K, CONV_DIM), CONV_K ** -0.5)
    inp["conv_b"] = nrm((Dp, CONV_DIM), 0.02)
    inp["s5_A_re"] = -0.5 + nrm((Dp, S5_GROUPS, S5_STATE), 0.01)
    inp["s5_A_im"] = jnp.pi * jnp.arange(S5_STATE, dtype=f32) + nrm((Dp, S5_GROUPS, S5_STATE), 0.01)
    inp["s5_log_dt"] = jax.random.uniform(next(ks), (Dp, S5_GROUPS), f32,
                                          np.log(0.001), np.log(0.1))
    inp["s5_B_re"] = nrm((Dp, S5_GROUPS, S5_STATE, S5_GROUP), (2 * S5_GROUP) ** -0.5)
    inp["s5_B_im"] = nrm((Dp, S5_GROUPS, S5_STATE, S5_GROUP), (2 * S5_GROUP) ** -0.5)
    inp["s5_C_re"] = nrm((Dp, S5_GROUPS, S5_GROUP, S5_STATE), S5_STATE ** -0.5)
    inp["s5_C_im"] = nrm((Dp, S5_GROUPS, S5_GROUP, S5_STATE), S5_STATE ** -0.5)
    inp["s5_D"] = nrm((Dp, S5_WIDTH), 1.0)
    inp["s5_w_glu"] = nrm((Dp, S5_WIDTH, S5_WIDTH), S5_WIDTH ** -0.5)
    inp["s5_b_glu"] = nrm((Dp, S5_WIDTH), 0.02)
    inp["ssd_A_log"] = jnp.log(jax.random.uniform(next(ks), (Dp, SSD_HEADS), f32, 1.0, 16.0))
    dt0 = jnp.exp(jax.random.uniform(next(ks), (Dp, SSD_HEADS), f32, np.log(0.001), np.log(0.1)))
    inp["ssd_dt_bias"] = dt0 + jnp.log(-jnp.expm1(-dt0))
    inp["ssd_D"] = gain((Dp, SSD_HEADS))
    inp["ssd_norm"] = gain((Dp, D_INNER))
    inp["w_proj_s5"] = nrm((Dp, S5_WIDTH, D_MODEL), S5_WIDTH ** -0.5)
    inp["w_proj_ssd"] = nrm((Dp, D_INNER, D_MODEL), D_INNER ** -0.5)
    inp["b_gate"] = nrm((Dp, N_BRANCH * D_MODEL), 0.02)
    inp["w_out"] = nrm((Dp, D_MODEL, D_MODEL), D_MODEL ** -0.5)
    inp["ffn2_norm"] = gain((Dp, D_MODEL))
    inp["ffn2_w_gate"] = nrm((Dp, D_MODEL, D_FF), D_MODEL ** -0.5)
    inp["ffn2_w_up"] = nrm((Dp, D_MODEL, D_FF), D_MODEL ** -0.5)
    inp["ffn2_w_down"] = nrm((Dp, D_FF, D_MODEL), D_FF ** -0.5)
    inp["final_norm"] = gain((D_MODEL,))
    return inp


def _fwd_reference(x, ffn1_norm, ffn1_w_gate, ffn1_w_up, ffn1_w_down, mix_norm, w_in,
              conv_w, conv_b, s5_A_re, s5_A_im, s5_log_dt, s5_B_re, s5_B_im,
              s5_C_re, s5_C_im, s5_D, s5_w_glu, s5_b_glu, ssd_A_log, ssd_dt_bias,
              ssd_D, ssd_norm, w_proj_s5, w_proj_ssd, b_gate, w_out,
              ffn2_norm, ffn2_w_gate, ffn2_w_up, ffn2_w_down, final_norm):
    split_at = [sum(IN_SPLITS[:i + 1]) for i in range(len(IN_SPLITS) - 1)]
    for i in range(DEPTH):
        x = x + 0.5 * swiglu(rmsnorm(x, ffn1_norm[i]), ffn1_w_gate[i], ffn1_w_up[i], ffn1_w_down[i])
        h = rmsnorm(x, mix_norm[i])
        proj = h @ w_in[i]
        u_s5, z, xbc, dt_raw, gate_logits = jnp.split(proj, split_at, axis=-1)
        y_s5 = s5_branch(u_s5, s5_A_re[i], s5_A_im[i], s5_log_dt[i], s5_B_re[i], s5_B_im[i],
                         s5_C_re[i], s5_C_im[i], s5_D[i], s5_w_glu[i], s5_b_glu[i])
        y_ssd = ssd_branch(z, xbc, dt_raw, conv_w[i], conv_b[i], ssd_A_log[i], ssd_dt_bias[i],
                           ssd_D[i], ssd_norm[i])
        gates = jax.nn.sigmoid(gate_logits + b_gate[i])
        g_s5, g_ssd = jnp.split(gates, 2, axis=-1)
        merged = g_s5 * (y_s5 @ w_proj_s5[i]) + g_ssd * (y_ssd @ w_proj_ssd[i])
        x = x + merged @ w_out[i]
        x = x + 0.5 * swiglu(rmsnorm(x, ffn2_norm[i]), ffn2_w_gate[i], ffn2_w_up[i], ffn2_w_down[i])
    return rmsnorm(x, final_norm)


import jax as _jax
import jax.numpy as _jnp

TWIN_FORMAT = 'train_step'
FWD_PARAMS = ['x', 'ffn1_norm', 'ffn1_w_gate', 'ffn1_w_up', 'ffn1_w_down', 'mix_norm', 'w_in', 'conv_w', 'conv_b', 's5_A_re', 's5_A_im', 's5_log_dt', 's5_B_re', 's5_B_im', 's5_C_re', 's5_C_im', 's5_D', 's5_w_glu', 's5_b_glu', 'ssd_A_log', 'ssd_dt_bias', 'ssd_D', 'ssd_norm', 'w_proj_s5', 'w_proj_ssd', 'b_gate', 'w_out', 'ffn2_norm', 'ffn2_w_gate', 'ffn2_w_up', 'ffn2_w_down', 'final_norm']
TWIN_WEIGHTS = ['ffn1_norm', 'ffn1_w_gate', 'ffn1_w_up', 'ffn1_w_down', 'mix_norm', 'w_in', 'conv_w', 'conv_b', 's5_A_re', 's5_A_im', 's5_log_dt', 's5_B_re', 's5_B_im', 's5_C_re', 's5_C_im', 's5_D', 's5_w_glu', 's5_b_glu', 'ssd_A_log', 'ssd_dt_bias', 'ssd_D', 'ssd_norm', 'w_proj_s5', 'w_proj_ssd', 'b_gate', 'w_out', 'ffn2_norm', 'ffn2_w_gate', 'ffn2_w_up', 'ffn2_w_down', 'final_norm']
TWIN_DIFF_INPUT = 'x'
TWIN_INPUTS = ['x', 'ffn1_norm', 'ffn1_w_gate', 'ffn1_w_up', 'ffn1_w_down', 'mix_norm', 'w_in', 'conv_w', 'conv_b', 's5_A_re', 's5_A_im', 's5_log_dt', 's5_B_re', 's5_B_im', 's5_C_re', 's5_C_im', 's5_D', 's5_w_glu', 's5_b_glu', 'ssd_A_log', 'ssd_dt_bias', 'ssd_D', 'ssd_norm', 'w_proj_s5', 'w_proj_ssd', 'b_gate', 'w_out', 'ffn2_norm', 'ffn2_w_gate', 'ffn2_w_up', 'ffn2_w_down', 'final_norm', 'loss_target', 'm_ffn1_norm', 'm_ffn1_w_gate', 'm_ffn1_w_up', 'm_ffn1_w_down', 'm_mix_norm', 'm_w_in', 'm_conv_w', 'm_conv_b', 'm_s5_A_re', 'm_s5_A_im', 'm_s5_log_dt', 'm_s5_B_re', 'm_s5_B_im', 'm_s5_C_re', 'm_s5_C_im', 'm_s5_D', 'm_s5_w_glu', 'm_s5_b_glu', 'm_ssd_A_log', 'm_ssd_dt_bias', 'm_ssd_D', 'm_ssd_norm', 'm_w_proj_s5', 'm_w_proj_ssd', 'm_b_gate', 'm_w_out', 'm_ffn2_norm', 'm_ffn2_w_gate', 'm_ffn2_w_up', 'm_ffn2_w_down', 'm_final_norm', 'v_ffn1_norm', 'v_ffn1_w_gate', 'v_ffn1_w_up', 'v_ffn1_w_down', 'v_mix_norm', 'v_w_in', 'v_conv_w', 'v_conv_b', 'v_s5_A_re', 'v_s5_A_im', 'v_s5_log_dt', 'v_s5_B_re', 'v_s5_B_im', 'v_s5_C_re', 'v_s5_C_im', 'v_s5_D', 'v_s5_w_glu', 'v_s5_b_glu', 'v_ssd_A_log', 'v_ssd_dt_bias', 'v_ssd_D', 'v_ssd_norm', 'v_w_proj_s5', 'v_w_proj_ssd', 'v_b_gate', 'v_w_out', 'v_ffn2_norm', 'v_ffn2_w_gate', 'v_ffn2_w_up', 'v_ffn2_w_down', 'v_final_norm']
TWIN_OUTPUTS = ['loss', 'grad_x', 'grad_ffn1_norm', 'grad_ffn1_w_gate', 'grad_ffn1_w_up', 'grad_ffn1_w_down', 'grad_mix_norm', 'grad_w_in', 'grad_conv_w', 'grad_conv_b', 'grad_s5_A_re', 'grad_s5_A_im', 'grad_s5_log_dt', 'grad_s5_B_re', 'grad_s5_B_im', 'grad_s5_C_re', 'grad_s5_C_im', 'grad_s5_D', 'grad_s5_w_glu', 'grad_s5_b_glu', 'grad_ssd_A_log', 'grad_ssd_dt_bias', 'grad_ssd_D', 'grad_ssd_norm', 'grad_w_proj_s5', 'grad_w_proj_ssd', 'grad_b_gate', 'grad_w_out', 'grad_ffn2_norm', 'grad_ffn2_w_gate', 'grad_ffn2_w_up', 'grad_ffn2_w_down', 'grad_final_norm', 'delta_ffn1_norm', 'delta_ffn1_w_gate', 'delta_ffn1_w_up', 'delta_ffn1_w_down', 'delta_mix_norm', 'delta_w_in', 'delta_conv_w', 'delta_conv_b', 'delta_s5_A_re', 'delta_s5_A_im', 'delta_s5_log_dt', 'delta_s5_B_re', 'delta_s5_B_im', 'delta_s5_C_re', 'delta_s5_C_im', 'delta_s5_D', 'delta_s5_w_glu', 'delta_s5_b_glu', 'delta_ssd_A_log', 'delta_ssd_dt_bias', 'delta_ssd_D', 'delta_ssd_norm', 'delta_w_proj_s5', 'delta_w_proj_ssd', 'delta_b_gate', 'delta_w_out', 'delta_ffn2_norm', 'delta_ffn2_w_gate', 'delta_ffn2_w_up', 'delta_ffn2_w_down', 'delta_final_norm', 'new_m_ffn1_norm', 'new_m_ffn1_w_gate', 'new_m_ffn1_w_up', 'new_m_ffn1_w_down', 'new_m_mix_norm', 'new_m_w_in', 'new_m_conv_w', 'new_m_conv_b', 'new_m_s5_A_re', 'new_m_s5_A_im', 'new_m_s5_log_dt', 'new_m_s5_B_re', 'new_m_s5_B_im', 'new_m_s5_C_re', 'new_m_s5_C_im', 'new_m_s5_D', 'new_m_s5_w_glu', 'new_m_s5_b_glu', 'new_m_ssd_A_log', 'new_m_ssd_dt_bias', 'new_m_ssd_D', 'new_m_ssd_norm', 'new_m_w_proj_s5', 'new_m_w_proj_ssd', 'new_m_b_gate', 'new_m_w_out', 'new_m_ffn2_norm', 'new_m_ffn2_w_gate', 'new_m_ffn2_w_up', 'new_m_ffn2_w_down', 'new_m_final_norm', 'new_v_ffn1_norm', 'new_v_ffn1_w_gate', 'new_v_ffn1_w_up', 'new_v_ffn1_w_down', 'new_v_mix_norm', 'new_v_w_in', 'new_v_conv_w', 'new_v_conv_b', 'new_v_s5_A_re', 'new_v_s5_A_im', 'new_v_s5_log_dt', 'new_v_s5_B_re', 'new_v_s5_B_im', 'new_v_s5_C_re', 'new_v_s5_C_im', 'new_v_s5_D', 'new_v_s5_w_glu', 'new_v_s5_b_glu', 'new_v_ssd_A_log', 'new_v_ssd_dt_bias', 'new_v_ssd_D', 'new_v_ssd_norm', 'new_v_w_proj_s5', 'new_v_w_proj_ssd', 'new_v_b_gate', 'new_v_w_out', 'new_v_ffn2_norm', 'new_v_ffn2_w_gate', 'new_v_ffn2_w_up', 'new_v_ffn2_w_down', 'new_v_final_norm']
TWIN_LEAF_KINDS = {'loss': 'loss', 'grad_x': 'grad_x', 'grad_ffn1_norm': 'grad_w', 'grad_ffn1_w_gate': 'grad_w', 'grad_ffn1_w_up': 'grad_w', 'grad_ffn1_w_down': 'grad_w', 'grad_mix_norm': 'grad_w', 'grad_w_in': 'grad_w', 'grad_conv_w': 'grad_w', 'grad_conv_b': 'grad_w', 'grad_s5_A_re': 'grad_w', 'grad_s5_A_im': 'grad_w', 'grad_s5_log_dt': 'grad_w', 'grad_s5_B_re': 'grad_w', 'grad_s5_B_im': 'grad_w', 'grad_s5_C_re': 'grad_w', 'grad_s5_C_im': 'grad_w', 'grad_s5_D': 'grad_w', 'grad_s5_w_glu': 'grad_w', 'grad_s5_b_glu': 'grad_w', 'grad_ssd_A_log': 'grad_w', 'grad_ssd_dt_bias': 'grad_w', 'grad_ssd_D': 'grad_w', 'grad_ssd_norm': 'grad_w', 'grad_w_proj_s5': 'grad_w', 'grad_w_proj_ssd': 'grad_w', 'grad_b_gate': 'grad_w', 'grad_w_out': 'grad_w', 'grad_ffn2_norm': 'grad_w', 'grad_ffn2_w_gate': 'grad_w', 'grad_ffn2_w_up': 'grad_w', 'grad_ffn2_w_down': 'grad_w', 'grad_final_norm': 'grad_w', 'delta_ffn1_norm': 'delta_w', 'delta_ffn1_w_gate': 'delta_w', 'delta_ffn1_w_up': 'delta_w', 'delta_ffn1_w_down': 'delta_w', 'delta_mix_norm': 'delta_w', 'delta_w_in': 'delta_w', 'delta_conv_w': 'delta_w', 'delta_conv_b': 'delta_w', 'delta_s5_A_re': 'delta_w', 'delta_s5_A_im': 'delta_w', 'delta_s5_log_dt': 'delta_w', 'delta_s5_B_re': 'delta_w', 'delta_s5_B_im': 'delta_w', 'delta_s5_C_re': 'delta_w', 'delta_s5_C_im': 'delta_w', 'delta_s5_D': 'delta_w', 'delta_s5_w_glu': 'delta_w', 'delta_s5_b_glu': 'delta_w', 'delta_ssd_A_log': 'delta_w', 'delta_ssd_dt_bias': 'delta_w', 'delta_ssd_D': 'delta_w', 'delta_ssd_norm': 'delta_w', 'delta_w_proj_s5': 'delta_w', 'delta_w_proj_ssd': 'delta_w', 'delta_b_gate': 'delta_w', 'delta_w_out': 'delta_w', 'delta_ffn2_norm': 'delta_w', 'delta_ffn2_w_gate': 'delta_w', 'delta_ffn2_w_up': 'delta_w', 'delta_ffn2_w_down': 'delta_w', 'delta_final_norm': 'delta_w', 'new_m_ffn1_norm': 'new_m', 'new_m_ffn1_w_gate': 'new_m', 'new_m_ffn1_w_up': 'new_m', 'new_m_ffn1_w_down': 'new_m', 'new_m_mix_norm': 'new_m', 'new_m_w_in': 'new_m', 'new_m_conv_w': 'new_m', 'new_m_conv_b': 'new_m', 'new_m_s5_A_re': 'new_m', 'new_m_s5_A_im': 'new_m', 'new_m_s5_log_dt': 'new_m', 'new_m_s5_B_re': 'new_m', 'new_m_s5_B_im': 'new_m', 'new_m_s5_C_re': 'new_m', 'new_m_s5_C_im': 'new_m', 'new_m_s5_D': 'new_m', 'new_m_s5_w_glu': 'new_m', 'new_m_s5_b_glu': 'new_m', 'new_m_ssd_A_log': 'new_m', 'new_m_ssd_dt_bias': 'new_m', 'new_m_ssd_D': 'new_m', 'new_m_ssd_norm': 'new_m', 'new_m_w_proj_s5': 'new_m', 'new_m_w_proj_ssd': 'new_m', 'new_m_b_gate': 'new_m', 'new_m_w_out': 'new_m', 'new_m_ffn2_norm': 'new_m', 'new_m_ffn2_w_gate': 'new_m', 'new_m_ffn2_w_up': 'new_m', 'new_m_ffn2_w_down': 'new_m', 'new_m_final_norm': 'new_m', 'new_v_ffn1_norm': 'new_v', 'new_v_ffn1_w_gate': 'new_v', 'new_v_ffn1_w_up': 'new_v', 'new_v_ffn1_w_down': 'new_v', 'new_v_mix_norm': 'new_v', 'new_v_w_in': 'new_v', 'new_v_conv_w': 'new_v', 'new_v_conv_b': 'new_v', 'new_v_s5_A_re': 'new_v', 'new_v_s5_A_im': 'new_v', 'new_v_s5_log_dt': 'new_v', 'new_v_s5_B_re': 'new_v', 'new_v_s5_B_im': 'new_v', 'new_v_s5_C_re': 'new_v', 'new_v_s5_C_im': 'new_v', 'new_v_s5_D': 'new_v', 'new_v_s5_w_glu': 'new_v', 'new_v_s5_b_glu': 'new_v', 'new_v_ssd_A_log': 'new_v', 'new_v_ssd_dt_bias': 'new_v', 'new_v_ssd_D': 'new_v', 'new_v_ssd_norm': 'new_v', 'new_v_w_proj_s5': 'new_v', 'new_v_w_proj_ssd': 'new_v', 'new_v_b_gate': 'new_v', 'new_v_w_out': 'new_v', 'new_v_ffn2_norm': 'new_v', 'new_v_ffn2_w_gate': 'new_v', 'new_v_ffn2_w_up': 'new_v', 'new_v_ffn2_w_down': 'new_v', 'new_v_final_norm': 'new_v'}


def _forward(args):
    return _fwd_reference(*[args[k] for k in FWD_PARAMS])


def _output_shape():
    def fwd():
        inp = _fwd_setup_inputs(0)
        return _fwd_reference(*[inp[k] for k in FWD_PARAMS])
    out = _jax.eval_shape(fwd)
    return out.shape, out.dtype

N_MICROBATCH = 1
ADAM_LR = 0.001
ADAM_B1 = 0.9
ADAM_B2 = 0.999
ADAM_EPS = 1e-08
ADAM_WD = 0.01
ADAM_STEP = 10
PER_EXAMPLE_BATCH_AXIS = {'x': 0, 'loss_target': 0}
SHARED_INPUTS = []
_WEIGHT_DTYPES = {'ffn1_norm': _jnp.float32, 'ffn1_w_gate': _jnp.float32, 'ffn1_w_up': _jnp.float32, 'ffn1_w_down': _jnp.float32, 'mix_norm': _jnp.float32, 'w_in': _jnp.float32, 'conv_w': _jnp.float32, 'conv_b': _jnp.float32, 's5_A_re': _jnp.float32, 's5_A_im': _jnp.float32, 's5_log_dt': _jnp.float32, 's5_B_re': _jnp.float32, 's5_B_im': _jnp.float32, 's5_C_re': _jnp.float32, 's5_C_im': _jnp.float32, 's5_D': _jnp.float32, 's5_w_glu': _jnp.float32, 's5_b_glu': _jnp.float32, 'ssd_A_log': _jnp.float32, 'ssd_dt_bias': _jnp.float32, 'ssd_D': _jnp.float32, 'ssd_norm': _jnp.float32, 'w_proj_s5': _jnp.float32, 'w_proj_ssd': _jnp.float32, 'b_gate': _jnp.float32, 'w_out': _jnp.float32, 'ffn2_norm': _jnp.float32, 'ffn2_w_gate': _jnp.float32, 'ffn2_w_up': _jnp.float32, 'ffn2_w_down': _jnp.float32, 'final_norm': _jnp.float32}
MOMENT_SCALE = {'ffn1_norm': 9.335075e-02, 'ffn1_w_gate': 3.713882e-02, 'ffn1_w_up': 3.598759e-02, 'ffn1_w_down': 5.967293e-02, 'mix_norm': 1.352651e-01, 'w_in': 4.470862e-02, 'conv_w': 4.488300e-02, 'conv_b': 5.914754e-02, 's5_A_re': 2.087879e-03, 's5_A_im': 2.159538e-03, 's5_log_dt': 1.817382e+00, 's5_B_re': 1.374863e-03, 's5_B_im': 1.375568e-03, 's5_C_re': 1.924918e-03, 's5_C_im': 2.005260e-03, 's5_D': 3.481238e-02, 's5_w_glu': 8.718242e-03, 's5_b_glu': 1.398298e-02, 'ssd_A_log': 2.029029e-01, 'ssd_dt_bias': 2.437674e-01, 'ssd_D': 3.140190e-01, 'ssd_norm': 6.210587e-02, 'w_proj_s5': 2.954326e-02, 'w_proj_ssd': 8.342748e-02, 'b_gate': 2.481877e-02, 'w_out': 8.864872e-02, 'ffn2_norm': 6.079060e-02, 'ffn2_w_gate': 2.678153e-02, 'ffn2_w_up': 2.593236e-02, 'ffn2_w_down': 4.318641e-02, 'final_norm': 3.199971e+01}


def _to_microbatches(a, axis):
    t = _jnp.moveaxis(a, axis, 0)
    t = t.reshape((N_MICROBATCH, t.shape[0] // N_MICROBATCH) + t.shape[1:])
    return _jnp.moveaxis(t, 1, axis + 1)


def setup_inputs(seed: int = 0) -> dict:
    inp = _fwd_setup_inputs(seed)
    key = _jax.random.fold_in(_jax.random.key(seed), 7919)
    shape, _ = _output_shape()
    out = dict(inp)
    out["loss_target"] = _jax.random.normal(_jax.random.fold_in(key, 0), shape, _jnp.float32)
    for i, name in enumerate(TWIN_WEIGHTS):
        w = inp[name].astype(_jnp.float32)
        if MOMENT_SCALE is None:
            s = _jnp.sqrt(_jnp.mean(_jnp.square(w)) + 1e-30)
        else:
            s = MOMENT_SCALE[name]
        km, kv = _jax.random.split(_jax.random.fold_in(key, i + 1))
        out[name] = w
        out["m_" + name] = s * _jax.random.normal(km, w.shape, _jnp.float32)
        out["v_" + name] = (s * s) * _jax.random.uniform(kv, w.shape, _jnp.float32, 0.5, 1.5)
    if N_MICROBATCH > 1:
        for name, axis in PER_EXAMPLE_BATCH_AXIS.items():
            out[name] = _to_microbatches(out[name], axis)
    return {'x': out['x'], 'ffn1_norm': out['ffn1_norm'], 'ffn1_w_gate': out['ffn1_w_gate'], 'ffn1_w_up': out['ffn1_w_up'], 'ffn1_w_down': out['ffn1_w_down'], 'mix_norm': out['mix_norm'], 'w_in': out['w_in'], 'conv_w': out['conv_w'], 'conv_b': out['conv_b'], 's5_A_re': out['s5_A_re'], 's5_A_im': out['s5_A_im'], 's5_log_dt': out['s5_log_dt'], 's5_B_re': out['s5_B_re'], 's5_B_im': out['s5_B_im'], 's5_C_re': out['s5_C_re'], 's5_C_im': out['s5_C_im'], 's5_D': out['s5_D'], 's5_w_glu': out['s5_w_glu'], 's5_b_glu': out['s5_b_glu'], 'ssd_A_log': out['ssd_A_log'], 'ssd_dt_bias': out['ssd_dt_bias'], 'ssd_D': out['ssd_D'], 'ssd_norm': out['ssd_norm'], 'w_proj_s5': out['w_proj_s5'], 'w_proj_ssd': out['w_proj_ssd'], 'b_gate': out['b_gate'], 'w_out': out['w_out'], 'ffn2_norm': out['ffn2_norm'], 'ffn2_w_gate': out['ffn2_w_gate'], 'ffn2_w_up': out['ffn2_w_up'], 'ffn2_w_down': out['ffn2_w_down'], 'final_norm': out['final_norm'], 'loss_target': out['loss_target'], 'm_ffn1_norm': out['m_ffn1_norm'], 'm_ffn1_w_gate': out['m_ffn1_w_gate'], 'm_ffn1_w_up': out['m_ffn1_w_up'], 'm_ffn1_w_down': out['m_ffn1_w_down'], 'm_mix_norm': out['m_mix_norm'], 'm_w_in': out['m_w_in'], 'm_conv_w': out['m_conv_w'], 'm_conv_b': out['m_conv_b'], 'm_s5_A_re': out['m_s5_A_re'], 'm_s5_A_im': out['m_s5_A_im'], 'm_s5_log_dt': out['m_s5_log_dt'], 'm_s5_B_re': out['m_s5_B_re'], 'm_s5_B_im': out['m_s5_B_im'], 'm_s5_C_re': out['m_s5_C_re'], 'm_s5_C_im': out['m_s5_C_im'], 'm_s5_D': out['m_s5_D'], 'm_s5_w_glu': out['m_s5_w_glu'], 'm_s5_b_glu': out['m_s5_b_glu'], 'm_ssd_A_log': out['m_ssd_A_log'], 'm_ssd_dt_bias': out['m_ssd_dt_bias'], 'm_ssd_D': out['m_ssd_D'], 'm_ssd_norm': out['m_ssd_norm'], 'm_w_proj_s5': out['m_w_proj_s5'], 'm_w_proj_ssd': out['m_w_proj_ssd'], 'm_b_gate': out['m_b_gate'], 'm_w_out': out['m_w_out'], 'm_ffn2_norm': out['m_ffn2_norm'], 'm_ffn2_w_gate': out['m_ffn2_w_gate'], 'm_ffn2_w_up': out['m_ffn2_w_up'], 'm_ffn2_w_down': out['m_ffn2_w_down'], 'm_final_norm': out['m_final_norm'], 'v_ffn1_norm': out['v_ffn1_norm'], 'v_ffn1_w_gate': out['v_ffn1_w_gate'], 'v_ffn1_w_up': out['v_ffn1_w_up'], 'v_ffn1_w_down': out['v_ffn1_w_down'], 'v_mix_norm': out['v_mix_norm'], 'v_w_in': out['v_w_in'], 'v_conv_w': out['v_conv_w'], 'v_conv_b': out['v_conv_b'], 'v_s5_A_re': out['v_s5_A_re'], 'v_s5_A_im': out['v_s5_A_im'], 'v_s5_log_dt': out['v_s5_log_dt'], 'v_s5_B_re': out['v_s5_B_re'], 'v_s5_B_im': out['v_s5_B_im'], 'v_s5_C_re': out['v_s5_C_re'], 'v_s5_C_im': out['v_s5_C_im'], 'v_s5_D': out['v_s5_D'], 'v_s5_w_glu': out['v_s5_w_glu'], 'v_s5_b_glu': out['v_s5_b_glu'], 'v_ssd_A_log': out['v_ssd_A_log'], 'v_ssd_dt_bias': out['v_ssd_dt_bias'], 'v_ssd_D': out['v_ssd_D'], 'v_ssd_norm': out['v_ssd_norm'], 'v_w_proj_s5': out['v_w_proj_s5'], 'v_w_proj_ssd': out['v_w_proj_ssd'], 'v_b_gate': out['v_b_gate'], 'v_w_out': out['v_w_out'], 'v_ffn2_norm': out['v_ffn2_norm'], 'v_ffn2_w_gate': out['v_ffn2_w_gate'], 'v_ffn2_w_up': out['v_ffn2_w_up'], 'v_ffn2_w_down': out['v_ffn2_w_down'], 'v_final_norm': out['v_final_norm']}


def _loss(weights, diff, rest, loss_target):
    with _jax.named_scope("forward"):
        args = {**rest, TWIN_DIFF_INPUT: diff, **{k: w.astype(_WEIGHT_DTYPES[k]) for k, w in weights.items()}}
        y = _forward(args)
    with _jax.named_scope("loss_head"):
        err = _jnp.square(y.astype(_jnp.float32) - loss_target)
        return 0.5 * _jnp.sum(_jnp.mean(err, axis=-1)) if err.ndim else 0.5 * err


def _adamw(w, g, m, v):
    m = ADAM_B1 * m + (1.0 - ADAM_B1) * g
    v = ADAM_B2 * v + (1.0 - ADAM_B2) * _jnp.square(g)
    m_hat = m / (1.0 - ADAM_B1 ** ADAM_STEP)
    v_hat = v / (1.0 - ADAM_B2 ** ADAM_STEP)
    delta = -ADAM_LR * (m_hat / (_jnp.sqrt(v_hat) + ADAM_EPS) + ADAM_WD * w)
    return delta, m, v


def reference(x, ffn1_norm, ffn1_w_gate, ffn1_w_up, ffn1_w_down, mix_norm, w_in, conv_w, conv_b, s5_A_re, s5_A_im, s5_log_dt, s5_B_re, s5_B_im, s5_C_re, s5_C_im, s5_D, s5_w_glu, s5_b_glu, ssd_A_log, ssd_dt_bias, ssd_D, ssd_norm, w_proj_s5, w_proj_ssd, b_gate, w_out, ffn2_norm, ffn2_w_gate, ffn2_w_up, ffn2_w_down, final_norm, loss_target, m_ffn1_norm, m_ffn1_w_gate, m_ffn1_w_up, m_ffn1_w_down, m_mix_norm, m_w_in, m_conv_w, m_conv_b, m_s5_A_re, m_s5_A_im, m_s5_log_dt, m_s5_B_re, m_s5_B_im, m_s5_C_re, m_s5_C_im, m_s5_D, m_s5_w_glu, m_s5_b_glu, m_ssd_A_log, m_ssd_dt_bias, m_ssd_D, m_ssd_norm, m_w_proj_s5, m_w_proj_ssd, m_b_gate, m_w_out, m_ffn2_norm, m_ffn2_w_gate, m_ffn2_w_up, m_ffn2_w_down, m_final_norm, v_ffn1_norm, v_ffn1_w_gate, v_ffn1_w_up, v_ffn1_w_down, v_mix_norm, v_w_in, v_conv_w, v_conv_b, v_s5_A_re, v_s5_A_im, v_s5_log_dt, v_s5_B_re, v_s5_B_im, v_s5_C_re, v_s5_C_im, v_s5_D, v_s5_w_glu, v_s5_b_glu, v_ssd_A_log, v_ssd_dt_bias, v_ssd_D, v_ssd_norm, v_w_proj_s5, v_w_proj_ssd, v_b_gate, v_w_out, v_ffn2_norm, v_ffn2_w_gate, v_ffn2_w_up, v_ffn2_w_down, v_final_norm):
    given = dict(x=x, ffn1_norm=ffn1_norm, ffn1_w_gate=ffn1_w_gate, ffn1_w_up=ffn1_w_up, ffn1_w_down=ffn1_w_down, mix_norm=mix_norm, w_in=w_in, conv_w=conv_w, conv_b=conv_b, s5_A_re=s5_A_re, s5_A_im=s5_A_im, s5_log_dt=s5_log_dt, s5_B_re=s5_B_re, s5_B_im=s5_B_im, s5_C_re=s5_C_re, s5_C_im=s5_C_im, s5_D=s5_D, s5_w_glu=s5_w_glu, s5_b_glu=s5_b_glu, ssd_A_log=ssd_A_log, ssd_dt_bias=ssd_dt_bias, ssd_D=ssd_D, ssd_norm=ssd_norm, w_proj_s5=w_proj_s5, w_proj_ssd=w_proj_ssd, b_gate=b_gate, w_out=w_out, ffn2_norm=ffn2_norm, ffn2_w_gate=ffn2_w_gate, ffn2_w_up=ffn2_w_up, ffn2_w_down=ffn2_w_down, final_norm=final_norm, loss_target=loss_target, m_ffn1_norm=m_ffn1_norm, m_ffn1_w_gate=m_ffn1_w_gate, m_ffn1_w_up=m_ffn1_w_up, m_ffn1_w_down=m_ffn1_w_down, m_mix_norm=m_mix_norm, m_w_in=m_w_in, m_conv_w=m_conv_w, m_conv_b=m_conv_b, m_s5_A_re=m_s5_A_re, m_s5_A_im=m_s5_A_im, m_s5_log_dt=m_s5_log_dt, m_s5_B_re=m_s5_B_re, m_s5_B_im=m_s5_B_im, m_s5_C_re=m_s5_C_re, m_s5_C_im=m_s5_C_im, m_s5_D=m_s5_D, m_s5_w_glu=m_s5_w_glu, m_s5_b_glu=m_s5_b_glu, m_ssd_A_log=m_ssd_A_log, m_ssd_dt_bias=m_ssd_dt_bias, m_ssd_D=m_ssd_D, m_ssd_norm=m_ssd_norm, m_w_proj_s5=m_w_proj_s5, m_w_proj_ssd=m_w_proj_ssd, m_b_gate=m_b_gate, m_w_out=m_w_out, m_ffn2_norm=m_ffn2_norm, m_ffn2_w_gate=m_ffn2_w_gate, m_ffn2_w_up=m_ffn2_w_up, m_ffn2_w_down=m_ffn2_w_down, m_final_norm=m_final_norm, v_ffn1_norm=v_ffn1_norm, v_ffn1_w_gate=v_ffn1_w_gate, v_ffn1_w_up=v_ffn1_w_up, v_ffn1_w_down=v_ffn1_w_down, v_mix_norm=v_mix_norm, v_w_in=v_w_in, v_conv_w=v_conv_w, v_conv_b=v_conv_b, v_s5_A_re=v_s5_A_re, v_s5_A_im=v_s5_A_im, v_s5_log_dt=v_s5_log_dt, v_s5_B_re=v_s5_B_re, v_s5_B_im=v_s5_B_im, v_s5_C_re=v_s5_C_re, v_s5_C_im=v_s5_C_im, v_s5_D=v_s5_D, v_s5_w_glu=v_s5_w_glu, v_s5_b_glu=v_s5_b_glu, v_ssd_A_log=v_ssd_A_log, v_ssd_dt_bias=v_ssd_dt_bias, v_ssd_D=v_ssd_D, v_ssd_norm=v_ssd_norm, v_w_proj_s5=v_w_proj_s5, v_w_proj_ssd=v_w_proj_ssd, v_b_gate=v_b_gate, v_w_out=v_w_out, v_ffn2_norm=v_ffn2_norm, v_ffn2_w_gate=v_ffn2_w_gate, v_ffn2_w_up=v_ffn2_w_up, v_ffn2_w_down=v_ffn2_w_down, v_final_norm=v_final_norm)
    weights = {n: given[n] for n in TWIN_WEIGHTS}
    shared = {n: given[n] for n in SHARED_INPUTS}
    per_example = {n: given[n] for n in ['x']}
    grad_fn = _jax.value_and_grad(_loss, argnums=(0, 1))

    def one_microbatch(ex, loss_target):
        ex = dict(ex)
        diff = ex.pop(TWIN_DIFF_INPUT)
        return grad_fn(weights, diff, {**shared, **ex}, loss_target)

    if N_MICROBATCH == 1:
        loss, (grad_w, grad_x) = one_microbatch(per_example, given["loss_target"])
    else:
        def body(carry, xs):
            loss_sum, grad_sum = carry
            l_k, (gw_k, gx_k) = one_microbatch(xs[0], xs[1])
            with _jax.named_scope("update"):
                return (loss_sum + l_k, _jax.tree.map(_jnp.add, grad_sum, gw_k)), gx_k

        init = (_jnp.zeros((), _jnp.float32), _jax.tree.map(_jnp.zeros_like, weights))
        (loss, grad_w), grad_x = _jax.lax.scan(body, init, (per_example, given["loss_target"]))
    with _jax.named_scope("update"):
        delta_w, new_m, new_v = {}, {}, {}
        for n in TWIN_WEIGHTS:
            delta_w[n], new_m[n], new_v[n] = _adamw(weights[n], grad_w[n], given["m_" + n], given["v_" + n])
    return (loss, grad_x, *[grad_w[n] for n in TWIN_WEIGHTS], *[delta_w[n] for n in TWIN_WEIGHTS],
            *[new_m[n] for n in TWIN_WEIGHTS], *[new_v[n] for n in TWIN_WEIGHTS])
```

```python
import functools
import math

import jax
import jax.numpy as jnp
from jax import lax
from jax.experimental import pallas as pl
from jax.experimental.pallas import tpu as pltpu

F32 = jnp.float32
BF16 = jnp.bfloat16
MXU = BF16
HI = lax.Precision.HIGHEST

D_MODEL = 1024
D_FF = 2816
EPS = 1e-6
S5_GROUPS, S5_GROUP, S5_STATE = 64, 16, 64
D_INNER = 2048
SSD_HEADDIM, SSD_HEADS, SSD_GROUPS, SSD_HPG, SSD_STATE = 64, 32, 8, 4, 128
CONV_K, CONV_DIM = 4, 4096
IN_SPLITS = (1024, 2048, 4096, 32, 2048)
IN_OFFS = (0, 1024, 3072, 7168, 7200, 9248)
SSD_Q = 256
ADAM_LR, ADAM_B1, ADAM_B2, ADAM_EPS, ADAM_WD, ADAM_STEP = 0.001, 0.9, 0.999, 1e-08, 0.01, 10

VMEM_LIMIT = 56 * 1024 * 1024
MESH = pl.DeviceIdType.MESH

NN = ((1,), (0,))
NT = ((1,), (1,))
TN = ((0,), (0,))


def _dot(a, b, dims, precision=None):
    if precision is None:
        a, b = a.astype(MXU), b.astype(MXU)
    return lax.dot_general(a, b, (dims, ((), ())), precision=precision, preferred_element_type=F32)


def _tile(n, pref):
    if n <= pref:
        return n
    best = None
    for t in range(128, pref + 1, 128):
        if n % t == 0:
            best = t
    assert best is not None, (n, pref)
    return best


def _params(sem):
    return pltpu.CompilerParams(dimension_semantics=sem, vmem_limit_bytes=VMEM_LIMIT)


def _matmul(a, b, mode, *, name, out=((F32),), epilogue=None, extras=(), tm=512, tn=512):
    if mode == "nn":
        (M, K), (_, N) = a.shape, b.shape
    elif mode == "nt":
        (M, K), (N, _) = a.shape, b.shape
    else:
        (K, M), (_, N) = a.shape, b.shape
    tm, tn = _tile(M, tm), _tile(N, tn)
    a_spec = pl.BlockSpec((K, tm), lambda i, j: (0, i)) if mode == "tn" else pl.BlockSpec((tm, K), lambda i, j: (i, 0))
    b_spec = pl.BlockSpec((tn, K), lambda i, j: (j, 0)) if mode == "nt" else pl.BlockSpec((K, tn), lambda i, j: (0, j))
    dims = {"nn": NN, "nt": NT, "tn": TN}[mode]
    e_specs = []
    for e in extras:
        if e.shape[0] == 1:
            e_specs.append(pl.BlockSpec((1, tn), lambda i, j: (0, j)))
        else:
            assert e.shape == (M, N), (e.shape, M, N)
            e_specs.append(pl.BlockSpec((tm, tn), lambda i, j: (i, j)))
    n_e, n_o = len(extras), len(out)

    def body(a_ref, b_ref, *refs):
        acc = _dot(a_ref[...], b_ref[...], dims)
        res = (acc,) if epilogue is None else epilogue(acc, *[r[...] for r in refs[:n_e]])
        for r, v in zip(refs[n_e:], res):
            r[...] = v.astype(r.dtype)

    res = pl.pallas_call(
        body, name=name, grid=(M // tm, N // tn),
        in_specs=[a_spec, b_spec] + e_specs,
        out_specs=[pl.BlockSpec((tm, tn), lambda i, j: (i, j)) for _ in range(n_o)],
        out_shape=[jax.ShapeDtypeStruct((M, N), dt) for dt in out],
        compiler_params=_params(("parallel", "parallel")),
    )(a, b, *extras)
    return res[0] if n_o == 1 else res


def _rowwise(fn, rows, fulls, outs, reds=(), *, name, tm=256):
    T = rows[0].shape[0]
    tm = min(tm, T)
    n_r, n_f, n_o, n_d = len(rows), len(fulls), len(outs), len(reds)

    def body(*refs):
        ins = [r[...] for r in refs[:n_r + n_f]]
        o_refs = refs[n_r + n_f:n_r + n_f + n_o]
        d_refs = refs[n_r + n_f + n_o:]
        res = fn(*ins)
        for r, v in zip(o_refs, res[:n_o]):
            r[...] = v.astype(r.dtype)
        if n_d:
            @pl.when(pl.program_id(0) == 0)
            def _():
                for r in d_refs:
                    r[...] = jnp.zeros_like(r)
            for r, v in zip(d_refs, res[n_o:]):
                r[...] += v.astype(r.dtype)

    res = pl.pallas_call(
        body, name=name, grid=(T // tm,),
        in_specs=[pl.BlockSpec((tm, r.shape[1]), lambda i: (i, 0)) for r in rows]
        + [pl.BlockSpec(f.shape, lambda i, nd=f.ndim: (0,) * nd) for f in fulls],
        out_specs=[pl.BlockSpec((tm, c), lambda i: (i, 0)) for c, _ in outs]
        + [pl.BlockSpec(s, lambda i, nd=len(s): (0,) * nd) for s, _ in reds],
        out_shape=[jax.ShapeDtypeStruct((T, c), dt) for c, dt in outs]
        + [jax.ShapeDtypeStruct(s, dt) for s, dt in reds],
        compiler_params=_params(("arbitrary",)),
    )(*rows, *fulls)
    return res


def _rms(x, g):
    return x * lax.rsqrt(jnp.mean(x * x, axis=-1, keepdims=True) + EPS) * g


def _colsum(v):
    return jnp.sum(v, axis=0, keepdims=True)


def _softplus(x):
    return jnp.maximum(x, 0.0) + jnp.log1p(jnp.exp(-jnp.abs(x)))


def _ffn_fwd(x, norm, wgT, wuT, wd, tag):
    D = x.shape[1]
    (hn,) = _rowwise(lambda xt, g: (_rms(xt, g),), [x], [norm], [(D, BF16)], name=f"{tag}_norm")
    a = _matmul(hn, wgT, "nt", name=f"{tag}_gate")
    b = _matmul(hn, wuT, "nt", name=f"{tag}_up")
    (hid,) = _rowwise(lambda at, bt: (jax.nn.silu(at) * bt,), [a, b], [], [(a.shape[1], BF16)], name=f"{tag}_act")
    y = _matmul(hid, wd, "nn", name=f"{tag}_down", epilogue=lambda acc, xt: (xt + 0.5 * acc,), extras=[x])
    return y, (hn, a, b, hid)


def _ffn_bwd(dy, x, norm, wgT, wuT, wd, res, tag):
    hn, a, b, hid = res
    D, F = x.shape[1], a.shape[1]
    dhid = _matmul(dy, wd, "nt", name=f"{tag}_d_hid", epilogue=lambda acc: (0.5 * acc,))
    d_wd = _matmul(hid, dy, "tn", name=f"{tag}_d_wd", epilogue=lambda acc: (0.5 * acc,), tm=256)

    def act_bwd(at, bt, dh):
        _, vjp = jax.vjp(lambda p, q: jax.nn.silu(p) * q, at, bt)
        return vjp(dh)

    da, db = _rowwise(act_bwd, [a, b, dhid], [], [(F, BF16), (F, BF16)], name=f"{tag}_d_act")
    d_wgT = _matmul(da, hn, "tn", name=f"{tag}_d_wg", tm=256)
    d_wuT = _matmul(db, hn, "tn", name=f"{tag}_d_wu", tm=256)
    dhn = _matmul(da, wgT, "nn", name=f"{tag}_d_hn1")
    dhn = _matmul(db, wuT, "nn", name=f"{tag}_d_hn2", epilogue=lambda acc, e: (acc + e,), extras=[dhn])

    def norm_bwd(xt, dh, dyt, g):
        _, vjp = jax.vjp(_rms, xt, g)
        dx, dg = vjp(dh)
        return dyt + dx, dg

    dx, d_norm = _rowwise(norm_bwd, [x, dhn, dy], [norm], [(D, F32)], [((1, D), F32)], name=f"{tag}_d_norm")
    return dx, d_norm, d_wgT, d_wuT, d_wd


S5_TILES = 8
S5_HALF = 256


def _s5_derive(A_re, A_im, log_dt, B_re, B_im, C_re, C_im):
    G, N, M = S5_GROUPS, S5_STATE, S5_GROUP
    dt = jnp.exp(log_dt)[:, None]
    mag = jnp.exp(A_re * dt)
    ar, ai = mag * jnp.cos(A_im * dt), mag * jnp.sin(A_im * dt)
    den = A_re * A_re + A_im * A_im
    cr = ((ar - 1.0) * A_re + ai * A_im) / den
    ci = (ai * A_re - (ar - 1.0) * A_im) / den
    bbr = cr[..., None] * B_re - ci[..., None] * B_im
    bbi = cr[..., None] * B_im + ci[..., None] * B_re
    eye = jnp.eye(8, dtype=F32)

    def tile_in(bb):
        t = bb.reshape(S5_TILES, 8, N, M).transpose(0, 1, 3, 2)
        return jnp.einsum("jamn,ab->jambn", t, eye).reshape(S5_TILES, 8 * M, 8 * N)

    def tile_out(c):
        t = c.reshape(S5_TILES, 8, M, N).transpose(0, 1, 3, 2)
        return jnp.einsum("janm,ab->janbm", t, eye).reshape(S5_TILES, 8 * N, 8 * M)

    return (tile_in(bbr), tile_in(bbi), tile_out(C_re), tile_out(C_im),
            ar.reshape(S5_TILES, 1, 8 * N), ai.reshape(S5_TILES, 1, 8 * N))


def _s5_scan(sr_ref, si_ref, ar, ai, T, reverse):
    sgn = -1.0 if reverse else 1.0

    def step(i, carry):
        pr, pi = carry
        t = (T - 1 - i) if reverse else i
        nr = ar * pr - sgn * ai * pi + sr_ref[pl.ds(t, 1), :]
        ni = ar * pi + sgn * ai * pr + si_ref[pl.ds(t, 1), :]
        sr_ref[pl.ds(t, 1), :] = nr
        si_ref[pl.ds(t, 1), :] = ni
        return nr, ni

    z = jnp.zeros((1, sr_ref.shape[1]), F32)
    lax.fori_loop(0, T, step, (z, z), unroll=8)


def _s5_specs(T):
    u_spec = pl.BlockSpec((T, 128), lambda j, h: (0, j))
    b_spec = pl.BlockSpec((None, 128, S5_HALF), lambda j, h: (j, 0, h))
    c_spec = pl.BlockSpec((None, S5_HALF, 128), lambda j, h: (j, h, 0))
    a_spec = pl.BlockSpec((None, 1, S5_HALF), lambda j, h: (j, 0, h))
    return u_spec, b_spec, c_spec, a_spec


def _s5_fwd(u, tiles):
    T = u.shape[0]
    u_spec, b_spec, c_spec, a_spec = _s5_specs(T)

    def body(u_ref, br_ref, bi_ref, cr_ref, ci_ref, ar_ref, ai_ref, y_ref, sr_ref, si_ref):
        ut = u_ref[...]
        sr_ref[...] = _dot(ut, br_ref[...], NN)
        si_ref[...] = _dot(ut, bi_ref[...], NN)
        _s5_scan(sr_ref, si_ref, ar_ref[...], ai_ref[...], T, False)
        y = _dot(sr_ref[...], cr_ref[...], NN) - _dot(si_ref[...], ci_ref[...], NN)

        @pl.when(pl.program_id(1) == 0)
        def _():
            y_ref[...] = y

        @pl.when(pl.program_id(1) == 1)
        def _():
            y_ref[...] += y

    return pl.pallas_call(
        body, name="s5_fwd", grid=(S5_TILES, 2),
        in_specs=[u_spec, b_spec, b_spec, c_spec, c_spec, a_spec, a_spec],
        out_specs=u_spec, out_shape=jax.ShapeDtypeStruct(u.shape, F32),
        scratch_shapes=[pltpu.VMEM((T, S5_HALF), F32), pltpu.VMEM((T, S5_HALF), F32)],
        compiler_params=_params(("parallel", "arbitrary")),
    )(u, *tiles)


def _s5_bwd(u, dys, du_skip, tiles):
    T = u.shape[0]
    u_spec, b_spec, c_spec, a_spec = _s5_specs(T)

    def body(u_ref, dy_ref, sk_ref, br_ref, bi_ref, cr_ref, ci_ref, ar_ref, ai_ref,
             du_ref, dbr_ref, dbi_ref, dcr_ref, dci_ref, dar_ref, dai_ref, sr_ref, si_ref, lr_ref, li_ref):
        ut, dy = u_ref[...], dy_ref[...]
        ar, ai = ar_ref[...], ai_ref[...]
        sr_ref[...] = _dot(ut, br_ref[...], NN)
        si_ref[...] = _dot(ut, bi_ref[...], NN)
        _s5_scan(sr_ref, si_ref, ar, ai, T, False)
        dcr_ref[...] = _dot(sr_ref[...], dy, TN)
        dci_ref[...] = -_dot(si_ref[...], dy, TN)
        lr_ref[...] = _dot(dy, cr_ref[...], NT)
        li_ref[...] = -_dot(dy, ci_ref[...], NT)
        _s5_scan(lr_ref, li_ref, ar, ai, T, True)
        lr, li = lr_ref[...], li_ref[...]
        first = lax.broadcasted_iota(jnp.int32, lr.shape, 0) == 0
        pr = jnp.where(first, 0.0, pltpu.roll(sr_ref[...], 1, 0))
        pi = jnp.where(first, 0.0, pltpu.roll(si_ref[...], 1, 0))
        dar_ref[...] = _colsum(lr * pr + li * pi)
        dai_ref[...] = _colsum(li * pr - lr * pi)
        dbr_ref[...] = _dot(ut, lr, TN)
        dbi_ref[...] = _dot(ut, li, TN)
        du = _dot(lr, br_ref[...], NT) + _dot(li, bi_ref[...], NT)

        @pl.when(pl.program_id(1) == 0)
        def _():
            du_ref[...] = du + sk_ref[...]

        @pl.when(pl.program_id(1) == 1)
        def _():
            du_ref[...] += du

    scr = pltpu.VMEM((T, S5_HALF), F32)
    return pl.pallas_call(
        body, name="s5_bwd", grid=(S5_TILES, 2),
        in_specs=[u_spec, u_spec, u_spec, b_spec, b_spec, c_spec, c_spec, a_spec, a_spec],
        out_specs=[u_spec, b_spec, b_spec, c_spec, c_spec, a_spec, a_spec],
        out_shape=[jax.ShapeDtypeStruct(u.shape, F32)] + [jax.ShapeDtypeStruct(t.shape, F32) for t in tiles],
        scratch_shapes=[scr, scr, scr, scr],
        compiler_params=_params(("parallel", "arbitrary")),
    )(u, dys, du_skip, *tiles)


CONV_TILE = 256


def _conv_pre(x, w, b):
    T = x.shape[0]
    row = lax.broadcasted_iota(jnp.int32, x.shape, 0)
    acc = x * w[CONV_K - 1:CONV_K, :] + b
    for lag in range(1, CONV_K):
        acc = acc + jnp.where(row >= lag, pltpu.roll(x, lag, 0), 0.0) * w[CONV_K - 1 - lag:CONV_K - lag, :]
    return acc


def _conv_fwd(x, w, b):
    T, C = x.shape
    col = pl.BlockSpec((T, CONV_TILE), lambda j: (0, j))

    def body(x_ref, w_ref, b_ref, o_ref):
        o_ref[...] = jax.nn.silu(_conv_pre(x_ref[...], w_ref[...], b_ref[...]))

    return pl.pallas_call(
        body, name="conv_fwd", grid=(C // CONV_TILE,),
        in_specs=[col, pl.BlockSpec((CONV_K, CONV_TILE), lambda j: (0, j)), pl.BlockSpec((1, CONV_TILE), lambda j: (0, j))],
        out_specs=col, out_shape=jax.ShapeDtypeStruct((T, C), F32),
        compiler_params=_params(("parallel",)),
    )(x, w, b)


def _conv_bwd(x, w, b, dout, col0, name):
    T, C = dout.shape
    off = col0 // CONV_TILE
    xcol = pl.BlockSpec((T, CONV_TILE), lambda j: (0, j + off))
    dcol = pl.BlockSpec((T, CONV_TILE), lambda j: (0, j))

    def body(x_ref, w_ref, b_ref, d_ref, dx_ref, dw_ref, db_ref):
        x, w = x_ref[...], w_ref[...]
        _, vjp = jax.vjp(jax.nn.silu, _conv_pre(x, w, b_ref[...]))
        (dy,) = vjp(d_ref[...])
        row = lax.broadcasted_iota(jnp.int32, x.shape, 0)
        dx = dy * w[CONV_K - 1:CONV_K, :]
        dw_ref[CONV_K - 1:CONV_K, :] = _colsum(dy * x)
        for lag in range(1, CONV_K):
            dx = dx + jnp.where(row < T - lag, pltpu.roll(dy, T - lag, 0), 0.0) * w[CONV_K - 1 - lag:CONV_K - lag, :]
            xs = jnp.where(row >= lag, pltpu.roll(x, lag, 0), 0.0)
            dw_ref[CONV_K - 1 - lag:CONV_K - lag, :] = _colsum(dy * xs)
        dx_ref[...] = dx.astype(dx_ref.dtype)
        db_ref[...] = _colsum(dy)

    return pl.pallas_call(
        body, name=name, grid=(C // CONV_TILE,),
        in_specs=[xcol, pl.BlockSpec((CONV_K, CONV_TILE), lambda j: (0, j + off)),
                  pl.BlockSpec((1, CONV_TILE), lambda j: (0, j + off)), dcol],
        out_specs=[dcol, pl.BlockSpec((CONV_K, CONV_TILE), lambda j: (0, j)), pl.BlockSpec((1, CONV_TILE), lambda j: (0, j))],
        out_shape=[jax.ShapeDtypeStruct((T, C), BF16), jax.ShapeDtypeStruct((CONV_K, C), F32), jax.ShapeDtypeStruct((1, C), F32)],
        compiler_params=_params(("parallel",)),
    )(x, w, b, dout)


def _ssd_common(x_ref, b_ref, c_ref, dtr_ref, bias_ref, alog_ref, tri_ref, triu_ref):
    Q = x_ref.shape[0]
    x, Bm, Cm = x_ref[...], b_ref[...], c_ref[...]
    pre = dtr_ref[...] + bias_ref[...]
    dt = _softplus(pre)
    A = -jnp.exp(alog_ref[...])
    adt = dt * A
    ac4 = _dot(tri_ref[...], adt, NN, HI)
    ar4 = _dot(adt, triu_ref[...], TN, HI)
    cb = _dot(Cm, Bm, NT)
    ii = lax.broadcasted_iota(jnp.int32, (Q, Q), 0)
    jj = lax.broadcasted_iota(jnp.int32, (Q, Q), 1)
    causal = ii >= jj
    return x, Bm, Cm, pre, dt, A, ac4, ar4, cb, causal


def _ssd_head(k, x, dt, ac4, ar4, cb, causal):
    P = SSD_HEADDIM
    Q = x.shape[0]
    xk = x[:, P * k:P * (k + 1)]
    dtk = dt[:, k:k + 1]
    ac = ac4[:, k:k + 1]
    L = jnp.where(causal, jnp.exp(jnp.where(causal, ac - ar4[k:k + 1, :], 0.0)), 0.0)
    G = cb * L
    xdt = xk * dtk
    atot = ac[Q - 1:Q, :]
    return xk, dtk, ac, L, G, xdt, atot


def _ssd_specs(T, Q, rev):
    NC = T // Q
    cc = (lambda c: NC - 1 - c) if rev else (lambda c: c)
    x_spec = pl.BlockSpec((Q, SSD_HPG * SSD_HEADDIM), lambda g, c: (cc(c), g))
    b_spec = pl.BlockSpec((Q, SSD_STATE), lambda g, c: (cc(c), D_INNER // SSD_STATE + g))
    c_spec = pl.BlockSpec((Q, SSD_STATE), lambda g, c: (cc(c), D_INNER // SSD_STATE + SSD_GROUPS + g))
    dt_spec = pl.BlockSpec((None, Q, SSD_HPG), lambda g, c: (g, cc(c), 0))
    p_spec = pl.BlockSpec((None, 1, SSD_HPG), lambda g, c: (g, 0, 0))
    tri_spec = pl.BlockSpec((Q, Q), lambda g, c: (0, 0))
    h_spec = pl.BlockSpec((None, None, SSD_HPG, SSD_HEADDIM, SSD_STATE), lambda g, c: (g, cc(c), 0, 0, 0))
    return x_spec, b_spec, c_spec, dt_spec, p_spec, tri_spec, h_spec


def _tri(Q):
    tri = jnp.tril(jnp.ones((Q, Q), F32))
    return tri, tri.T


def _ssd_fwd(xc, dtr8, bias8, alog8, dsk8):
    T, Q = xc.shape[0], SSD_Q
    NC = T // Q
    x_spec, b_spec, c_spec, dt_spec, p_spec, tri_spec, h_spec = _ssd_specs(T, Q, False)
    y_spec = pl.BlockSpec((Q, SSD_HPG * SSD_HEADDIM), lambda g, c: (c, g))

    def body(x_ref, b_ref, c_ref, dtr_ref, bias_ref, alog_ref, dsk_ref, tri_ref, triu_ref, y_ref, hs_ref, h_scr):
        @pl.when(pl.program_id(1) == 0)
        def _():
            h_scr[...] = jnp.zeros_like(h_scr)

        hs_ref[...] = h_scr[...]
        x, Bm, Cm, pre, dt, A, ac4, ar4, cb, causal = _ssd_common(
            x_ref, b_ref, c_ref, dtr_ref, bias_ref, alog_ref, tri_ref, triu_ref)
        dsk = dsk_ref[...]
        for k in range(SSD_HPG):
            xk, dtk, ac, L, G, xdt, atot = _ssd_head(k, x, dt, ac4, ar4, cb, causal)
            hin = h_scr[k]
            y = _dot(G, xdt, NN) + _dot(Cm, hin, NT) * jnp.exp(ac) + dsk[:, k:k + 1] * xk
            y_ref[:, SSD_HEADDIM * k:SSD_HEADDIM * (k + 1)] = y
            xw = xdt * jnp.exp(atot - ac)
            h_scr[k] = jnp.exp(atot) * hin + _dot(xw, Bm, TN)

    tri, triu = _tri(Q)
    return pl.pallas_call(
        body, name="ssd_fwd", grid=(SSD_GROUPS, NC),
        in_specs=[x_spec, b_spec, c_spec, dt_spec, p_spec, p_spec, p_spec, tri_spec, tri_spec],
        out_specs=[y_spec, h_spec],
        out_shape=[jax.ShapeDtypeStruct((T, D_INNER), F32),
                   jax.ShapeDtypeStruct((SSD_GROUPS, NC, SSD_HPG, SSD_HEADDIM, SSD_STATE), F32)],
        scratch_shapes=[pltpu.VMEM((SSD_HPG, SSD_HEADDIM, SSD_STATE), F32)],
        compiler_params=_params(("parallel", "arbitrary")),
    )(xc, xc, xc, dtr8, bias8, alog8, dsk8, tri, triu)


def _ssd_bwd(xc, dtr8, bias8, alog8, dsk8, hs, dy):
    T, Q = xc.shape[0], SSD_Q
    NC = T // Q
    P = SSD_HEADDIM
    x_spec, b_spec, c_spec, dt_spec, p_spec, tri_spec, h_spec = _ssd_specs(T, Q, True)
    dy_spec = pl.BlockSpec((Q, SSD_HPG * P), lambda g, c: (NC - 1 - c, g))
    dbc_spec = pl.BlockSpec((Q, SSD_STATE), lambda g, c: (NC - 1 - c, g))

    def body(x_ref, b_ref, c_ref, dtr_ref, bias_ref, alog_ref, dsk_ref, tri_ref, triu_ref, hs_ref, dy_ref,
             dx_ref, db_ref, dc_ref, ddtr_ref, dbias_ref, dalog_ref, ddsk_ref, dh_scr):
        @pl.when(pl.program_id(1) == 0)
        def _():
            dh_scr[...] = jnp.zeros_like(dh_scr)
            dbias_ref[...] = jnp.zeros_like(dbias_ref)
            dalog_ref[...] = jnp.zeros_like(dalog_ref)
            ddsk_ref[...] = jnp.zeros_like(ddsk_ref)

        x, Bm, Cm, pre, dt, A, ac4, ar4, cb, causal = _ssd_common(
            x_ref, b_ref, c_ref, dtr_ref, bias_ref, alog_ref, tri_ref, triu_ref)
        dsk = dsk_ref[...]
        dyv = dy_ref[...]
        triu = triu_ref[...]
        ones = jnp.ones((Q, 1), F32)
        last = lax.broadcasted_iota(jnp.int32, (Q, 1), 0) == Q - 1
        dcb = jnp.zeros((Q, Q), F32)
        dB = jnp.zeros((Q, SSD_STATE), F32)
        dC = jnp.zeros((Q, SSD_STATE), F32)
        for k in range(SSD_HPG):
            xk, dtk, ac, L, G, xdt, atot = _ssd_head(k, x, dt, ac4, ar4, cb, causal)
            hin = hs_ref[k]
            dho = dh_scr[k]
            dyk = dyv[:, P * k:P * (k + 1)]
            E = jnp.exp(ac)
            Fd = jnp.exp(atot - ac)
            etot = jnp.exp(atot)
            ddsk_ref[:, k:k + 1] += jnp.sum(dyk * xk, keepdims=True)
            dxk = dsk[:, k:k + 1] * dyk
            Z = _dot(Cm, hin, NT)
            dZ = dyk * E
            dac = jnp.sum(dyk * Z, axis=1, keepdims=True) * E
            dC = dC + _dot(dZ, hin, NN)
            dhin = _dot(dZ, Cm, TN) + etot * dho
            datot = jnp.sum(dho * hin, keepdims=True) * etot
            xw = xdt * Fd
            dxw = _dot(Bm, dho, NT)
            dB = dB + _dot(xw, dho, NN)
            dxdt = dxw * Fd
            dFa = jnp.sum(dxw * xdt, axis=1, keepdims=True) * Fd
            datot = datot + jnp.sum(dFa, keepdims=True)
            dac = dac - dFa
            dG = _dot(dyk, xdt, NT)
            dxdt = dxdt + _dot(G, dyk, TN)
            dcb = dcb + dG * L
            Mseg = dG * G
            dac = dac + jnp.sum(Mseg, axis=1, keepdims=True) - _dot(Mseg, ones, TN, HI)
            dac = dac + jnp.where(last, datot, 0.0)
            dxk = dxk + dxdt * dtk
            ddtk = jnp.sum(dxdt * xk, axis=1, keepdims=True)
            dadt = _dot(triu, dac, NN, HI)
            Ak = A[:, k:k + 1]
            ddtk = ddtk + dadt * Ak
            dalog_ref[:, k:k + 1] += jnp.sum(dadt * dtk, keepdims=True) * Ak
            ddtr = ddtk * jax.nn.sigmoid(pre[:, k:k + 1])
            dbias_ref[:, k:k + 1] += jnp.sum(ddtr, keepdims=True)
            ddtr_ref[:, k:k + 1] = ddtr
            dx_ref[:, P * k:P * (k + 1)] = dxk.astype(dx_ref.dtype)
            dh_scr[k] = dhin
        dC = dC + _dot(dcb, Bm, NN)
        dB = dB + _dot(dcb, Cm, TN)
        db_ref[...] = dB
        dc_ref[...] = dC

    tri, triu = _tri(Q)
    return pl.pallas_call(
        body, name="ssd_bwd", grid=(SSD_GROUPS, NC),
        in_specs=[x_spec, b_spec, c_spec, dt_spec, p_spec, p_spec, p_spec, tri_spec, tri_spec, h_spec, dy_spec],
        out_specs=[dy_spec, dbc_spec, dbc_spec, dt_spec, p_spec, p_spec, p_spec],
        out_shape=[jax.ShapeDtypeStruct((T, D_INNER), F32),
                   jax.ShapeDtypeStruct((T, SSD_GROUPS * SSD_STATE), F32),
                   jax.ShapeDtypeStruct((T, SSD_GROUPS * SSD_STATE), F32),
                   jax.ShapeDtypeStruct(dtr8.shape, F32)] + [jax.ShapeDtypeStruct(bias8.shape, F32)] * 3,
        scratch_shapes=[pltpu.VMEM((SSD_HPG, P, SSD_STATE), F32)],
        compiler_params=_params(("parallel", "arbitrary")),
    )(xc, xc, xc, dtr8, bias8, alog8, dsk8, tri, triu, hs, dy)


def _to_groups(v):
    return v.reshape(SSD_GROUPS, 1, SSD_HPG)


def _mixer_fwd(x, p, w):
    T, D = x.shape
    (hn,) = _rowwise(lambda xt, g: (_rms(xt, g),), [x], [p["mix_norm"]], [(D, BF16)], name="mix_norm")
    winT = w["w_in"]
    u, z, xbc, dtr, gl = [
        _matmul(hn, winT[IN_OFFS[i]:IN_OFFS[i + 1]], "nt", name=f"mix_in{i}") for i in range(5)]
    tiles = _s5_derive(*[p[k][0] for k in ("s5_A_re", "s5_A_im", "s5_log_dt", "s5_B_re", "s5_B_im", "s5_C_re", "s5_C_im")])
    ys = _s5_fwd(u, tiles)
    (g,) = _rowwise(lambda yt, ut, d: (jax.nn.gelu(yt + d * ut),), [ys, u], [p["s5_D"]], [(D, F32)], name="s5_gelu")
    y5 = _matmul(g, w["s5_w_glu"], "nn", name="s5_glu", out=(BF16,),
                 epilogue=lambda acc, gt, b: (gt * jax.nn.sigmoid(acc + b),), extras=[g, p["s5_b_glu"]])
    xc = _conv_fwd(xbc, w["conv_w"], p["conv_b"])
    dtr8 = dtr.reshape(T, SSD_GROUPS, SSD_HPG).transpose(1, 0, 2)
    ssd_p = [_to_groups(p[k]) for k in ("ssd_dt_bias", "ssd_A_log", "ssd_D")]
    yssd_raw, hs = _ssd_fwd(xc, dtr8, *ssd_p)
    (yssd,) = _rowwise(lambda yt, zt, nw: (_rms(yt * jax.nn.silu(zt), nw),), [yssd_raw, z], [p["ssd_norm"]],
                       [(D_INNER, BF16)], name="ssd_gate")
    p5 = _matmul(y5, w["w_proj_s5"], "nn", name="mix_p5")
    pssd = _matmul(yssd, w["w_proj_ssd"], "nn", name="mix_pssd")

    def merge(glt, at, bt, bg):
        gates = jax.nn.sigmoid(glt + bg)
        return (gates[:, :D] * at + gates[:, D:] * bt,)

    (merged,) = _rowwise(merge, [gl, p5, pssd], [p["b_gate"]], [(D, BF16)], name="mix_merge")
    y = _matmul(merged, w["w_out"], "nn", name="mix_out", epilogue=lambda acc, xt: (xt + acc,), extras=[x])
    res = dict(hn=hn, u=u, z=z, xbc=xbc, gl=gl, tiles=tiles, ys=ys, g=g, y5=y5, xc=xc, dtr8=dtr8, ssd_p=ssd_p,
               yssd_raw=yssd_raw, hs=hs, yssd=yssd, p5=p5, pssd=pssd, merged=merged)
    return y, res


def _mixer_bwd(dy, x, p, w, r):
    T, D = x.shape
    gp = {}
    dmerged = _matmul(dy, w["w_out"], "nt", name="mix_d_merged")
    gp["w_out"] = _matmul(r["merged"], dy, "tn", name="mix_d_wout", tm=256)

    def merge_bwd(glt, at, bt, dm, bg):
        def f(q, a_, b_):
            gates = jax.nn.sigmoid(q + bg)
            return gates[:, :D] * a_ + gates[:, D:] * b_
        _, vjp = jax.vjp(f, glt, at, bt)
        dq, da_, db_ = vjp(dm)
        return da_, db_, dq, _colsum(dq)

    dp5, dpssd, dgl, gp["b_gate"] = _rowwise(
        merge_bwd, [r["gl"], r["p5"], r["pssd"], dmerged], [p["b_gate"]],
        [(D, BF16), (D, BF16), (2 * D, BF16)], [((1, 2 * D), F32)], name="mix_d_merge")
    dy5 = _matmul(dp5, w["w_proj_s5"], "nt", name="mix_d_y5")
    gp["w_proj_s5"] = _matmul(r["y5"], dp5, "tn", name="mix_d_wp5", tm=256)
    dyssd = _matmul(dpssd, w["w_proj_ssd"], "nt", name="mix_d_yssd")
    gp["w_proj_ssd"] = _matmul(r["yssd"], dpssd, "tn", name="mix_d_wpssd", tm=256)

    def gate_bwd(yt, zt, dyt, nw):
        _, vjp = jax.vjp(lambda a_, b_, c_: _rms(a_ * jax.nn.silu(b_), c_), yt, zt, nw)
        return vjp(dyt)

    dyraw, dz, gp["ssd_norm"] = _rowwise(
        gate_bwd, [r["yssd_raw"], r["z"], dyssd], [p["ssd_norm"]],
        [(D_INNER, F32), (D_INNER, BF16)], [((1, D_INNER), F32)], name="ssd_d_gate")
    dxs, dBm, dCm, ddtr8, dbias8, dalog8, ddsk8 = _ssd_bwd(r["xc"], r["dtr8"], *r["ssd_p"], r["hs"], dyraw)
    gp["ssd_dt_bias"], gp["ssd_A_log"], gp["ssd_D"] = [v.reshape(1, SSD_HEADS) for v in (dbias8, dalog8, ddsk8)]
    ddtr = ddtr8.transpose(1, 0, 2).reshape(T, SSD_HEADS)
    conv = [_conv_bwd(r["xbc"], w["conv_w"], p["conv_b"], d, c0, f"conv_bwd{i}")
            for i, (d, c0) in enumerate(((dxs, 0), (dBm, D_INNER), (dCm, D_INNER + SSD_GROUPS * SSD_STATE)))]
    dxbc = [c[0] for c in conv]
    gp["conv_w"] = jnp.concatenate([c[1] for c in conv], axis=1)
    gp["conv_b"] = jnp.concatenate([c[2] for c in conv], axis=1)

    g = r["g"]

    def glu_ep(acc, gt, b, dyt):
        _, vjp = jax.vjp(lambda g_, t_: g_ * jax.nn.sigmoid(t_ + b), gt, acc)
        return vjp(dyt)

    dg1, dt_ = _matmul(g, w["s5_w_glu"], "nn", name="s5_d_glu", out=(F32, BF16), epilogue=glu_ep,
                       extras=[g, p["s5_b_glu"], dy5])
    gp["s5_w_glu"] = _matmul(g, dt_, "tn", name="s5_d_wglu", tm=256)
    dg = _matmul(dt_, w["s5_w_glu"], "nt", name="s5_d_g", epilogue=lambda acc, e: (acc + e,), extras=[dg1])

    def gelu_bwd(yt, ut, dgt, dtt, d):
        _, vjp = jax.vjp(lambda y_, d_: jax.nn.gelu(y_ + d_ * ut), yt, d)
        dys, dd = vjp(dgt)
        return dys, dys * d, dd, _colsum(dtt.astype(F32))

    dys, dusk, gp["s5_D"], gp["s5_b_glu"] = _rowwise(
        gelu_bwd, [r["ys"], r["u"], dg, dt_], [p["s5_D"]], [(D, F32), (D, F32)], [((1, D), F32), ((1, D), F32)],
        name="s5_d_gelu")
    du, *dtiles = _s5_bwd(r["u"], dys, dusk, r["tiles"])

    pieces = [du, dz, None, ddtr, dgl]
    winT = w["w_in"]
    hn = r["hn"]
    d_rows, dhn = [], None
    cols = [(du, 0, 1024), (dz, 1024, 3072), (dxbc[0], 3072, 5120), (dxbc[1], 5120, 6144), (dxbc[2], 6144, 7168),
            (ddtr, 7168, 7200), (dgl, 7200, 9248)]
    for i, (d, lo, hi) in enumerate(cols):
        d_rows.append(_matmul(d, hn, "tn", name=f"mix_d_win{i}", tm=256))
        if dhn is None:
            dhn = _matmul(d, winT[lo:hi], "nn", name=f"mix_d_hn{i}")
        else:
            dhn = _matmul(d, winT[lo:hi], "nn", name=f"mix_d_hn{i}", epilogue=lambda acc, e: (acc + e,), extras=[dhn])
    gp["w_in"] = jnp.concatenate(d_rows, axis=0)
    del pieces

    def norm_bwd(xt, dh, dyt, gn):
        _, vjp = jax.vjp(_rms, xt, gn)
        dx, dgn = vjp(dh)
        return dyt + dx, dgn

    dx, gp["mix_norm"] = _rowwise(norm_bwd, [x, dhn, dy], [p["mix_norm"]], [(D, F32)], [((1, D), F32)], name="mix_d_norm")
    return dx, gp, dtiles


def _s5_param_grads(p, dtiles):
    keys = ("s5_A_re", "s5_A_im", "s5_log_dt", "s5_B_re", "s5_B_im", "s5_C_re", "s5_C_im")
    _, vjp = jax.vjp(_s5_derive, *[p[k][0] for k in keys])
    return {k: v[None] for k, v in zip(keys, vjp(tuple(dtiles)))}


def _local_step(x, target, p, w):
    T, D = x.shape
    x1, r1 = _ffn_fwd(x, p["ffn1_norm"], w["ffn1_w_gate"], w["ffn1_w_up"], w["ffn1_w_down"], "ffn1")
    x2, rm = _mixer_fwd(x1, p, w)
    x3, r2 = _ffn_fwd(x2, p["ffn2_norm"], w["ffn2_w_gate"], w["ffn2_w_up"], w["ffn2_w_down"], "ffn2")

    def head(xt, tt, g):
        def f(x_, g_):
            e = _rms(x_, g_) - tt
            return 0.5 * jnp.sum(jnp.mean(e * e, axis=-1))
        l, (dx_, dg_) = jax.value_and_grad(f, argnums=(0, 1))(xt, g)
        return dx_, l.reshape(1, 1), dg_

    dx3, loss, d_final = _rowwise(head, [x3, target], [p["final_norm"]], [(D, F32)],
                                  [((1, 1), F32), ((1, D), F32)], name="loss_head")
    gp, gw = {"final_norm": d_final}, {}
    dx2, gp["ffn2_norm"], gw["ffn2_w_gate"], gw["ffn2_w_up"], gw["ffn2_w_down"] = _ffn_bwd(
        dx3, x2, p["ffn2_norm"], w["ffn2_w_gate"], w["ffn2_w_up"], w["ffn2_w_down"], r2, "ffn2")
    dx1, gm, dtiles = _mixer_bwd(dx2, x1, p, w, rm)
    for k in ("w_out", "w_proj_s5", "w_proj_ssd", "s5_w_glu", "w_in", "conv_w"):
        gw[k] = gm.pop(k)
    gp.update(gm)
    gp.update(_s5_param_grads(p, dtiles))
    dx0, gp["ffn1_norm"], gw["ffn1_w_gate"], gw["ffn1_w_up"], gw["ffn1_w_down"] = _ffn_bwd(
        dx1, x, p["ffn1_norm"], w["ffn1_w_gate"], w["ffn1_w_up"], w["ffn1_w_down"], r1, "ffn1")
    return loss, dx0, gp, gw


def _adamw(w, m, v, gs, *, name, emit_g, tm=128):
    n_g = len(gs)
    C = w.shape[1]

    def fn(wt, mt, vt, *gt):
        g = gt[0]
        for e in gt[1:]:
            g = g + e
        m2 = ADAM_B1 * mt + (1.0 - ADAM_B1) * g
        v2 = ADAM_B2 * vt + (1.0 - ADAM_B2) * (g * g)
        m_hat = m2 / (1.0 - ADAM_B1 ** ADAM_STEP)
        v_hat = v2 / (1.0 - ADAM_B2 ** ADAM_STEP)
        delta = -ADAM_LR * (m_hat / (jnp.sqrt(v_hat) + ADAM_EPS) + ADAM_WD * wt)
        return ((g,) if emit_g else ()) + (delta, m2, v2)

    assert n_g >= 1
    return _rowwise(fn, [w, m, v, *gs], [], [(C, F32)] * (4 if emit_g else 3), name=name, tm=_rtile(w.shape[0], tm))


def _rtile(r, cap=512):
    if r <= cap:
        return r
    best = None
    for t in range(8, cap + 1, 8):
        if r % t == 0:
            best = t
    assert best is not None, r
    return best


ANY = pl.BlockSpec(memory_space=pl.ANY)


def _where_am_i():
    x, y, c = lax.axis_index("x"), lax.axis_index("y"), lax.axis_index("c")
    ks = (2 * x + y, 2 * (1 - x) + y, 2 * x + (1 - y), 2 * (1 - x) + (1 - y))
    return x, y, c, ks


def _rc(src, dst, ssem, rsem, to):
    return pltpu.make_async_remote_copy(src_ref=src, dst_ref=dst, send_sem=ssem, recv_sem=rsem,
                                        device_id=to, device_id_type=MESH)


def _gather_shards(srcs):
    n = len(srcs)

    def body(*refs):
        src, out = refs[:n], refs[n:2 * n]
        ssem, rsem, lsem = refs[2 * n:]
        x, y, c, (k_me, k_x, k_y, k_d) = _where_am_i()
        sib = (x, y, 1 - c)
        local = [pltpu.make_async_copy(src[i], out[i].at[k_me], lsem.at[i]) for i in range(n)]
        for cp in local:
            cp.start()

        def own(i, q, to):
            return _rc(src[i], out[i].at[k_me], ssem.at[i, q], rsem.at[i, q], to)

        def slab(i, q, k, to):
            return _rc(out[i].at[k], out[i].at[k], ssem.at[i, q], rsem.at[i, q], to)

        @pl.when(c == 1)
        def _():
            sends = [own(i, 0, (1 - x, y, 1)) for i in range(n)]
            for cp in sends:
                cp.start()
            fwd = [slab(i, 3, k_x, sib) for i in range(n)]
            for i in range(n):
                slab(i, 0, k_x, sib).wait_recv()
                fwd[i].start()
            for i in range(n):
                slab(i, 4, k_y, sib).wait_recv()
                slab(i, 5, k_d, sib).wait_recv()
            for cp in sends + fwd:
                cp.wait_send()

        @pl.when(c == 0)
        def _():
            sends = [own(i, 1, (x, 1 - y, 0)) for i in range(n)] + [own(i, 2, (1 - x, 1 - y, 0)) for i in range(n)]
            for cp in sends:
                cp.start()
            fwd = [slab(i, 4, k_y, sib) for i in range(n)] + [slab(i, 5, k_d, sib) for i in range(n)]
            for i in range(n):
                slab(i, 1, k_y, sib).wait_recv()
                fwd[i].start()
            for i in range(n):
                slab(i, 2, k_d, sib).wait_recv()
                fwd[n + i].start()
            for i in range(n):
                slab(i, 3, k_x, sib).wait_recv()
            for cp in sends + fwd:
                cp.wait_send()

        for cp in local:
            cp.wait()

    return pl.pallas_call(
        body, name="gather_shards",
        in_specs=[ANY] * n, out_specs=[ANY] * n,
        out_shape=[jax.ShapeDtypeStruct((4,) + s.shape, s.dtype) for s in srcs],
        scratch_shapes=[pltpu.SemaphoreType.DMA((n, 6)), pltpu.SemaphoreType.DMA((n, 6)), pltpu.SemaphoreType.DMA((n,))],
        compiler_params=pltpu.CompilerParams(has_side_effects=True),
    )(*srcs)


def _swap_slabs(arrs, n_slabs, name):
    n = len(arrs)
    J = max(n_slabs, 1)

    def body(*refs):
        src, out = refs[:n], refs[n:2 * n]
        ssem, rsem = refs[2 * n:]
        x, y, c, (k_me, k_x, k_y, k_d) = _where_am_i()
        sib = (x, y, 1 - c)
        sel = (jnp.where(c == 1, k_y, k_me), jnp.where(c == 1, k_d, k_x))
        cps = []
        for i in range(n):
            for j in range(J):
                s = src[i].at[sel[j]] if n_slabs else src[i]
                d = out[i].at[j] if n_slabs else out[i]
                cps.append(_rc(s, d, ssem.at[i, j], rsem.at[i, j], sib))
        for cp in cps:
            cp.start()
        for cp in cps:
            cp.wait()

    return pl.pallas_call(
        body, name=name, in_specs=[ANY] * n, out_specs=[ANY] * n,
        out_shape=[jax.ShapeDtypeStruct(((n_slabs,) if n_slabs else ()) + a.shape[-2:], a.dtype) for a in arrs],
        scratch_shapes=[pltpu.SemaphoreType.DMA((n, J)), pltpu.SemaphoreType.DMA((n, J))],
        compiler_params=pltpu.CompilerParams(has_side_effects=True),
    )(*arrs)


def _send_chip_sums(arrs):
    n = len(arrs)

    def body(*refs):
        src, out = refs[:n], refs[n:2 * n]
        ssem, rsem = refs[2 * n:]
        x, y, c, _ = _where_am_i()

        @pl.when(c == 1)
        def _():
            cps = [_rc(src[i].at[1], out[i].at[0], ssem.at[i, 0], rsem.at[i, 0], (1 - x, y, 1)) for i in range(n)]
            for cp in cps:
                cp.start()
            for cp in cps:
                cp.wait()

        @pl.when(c == 0)
        def _():
            cps = [_rc(src[i].at[0], out[i].at[0], ssem.at[i, 0], rsem.at[i, 0], (x, 1 - y, 0)) for i in range(n)]
            cps += [_rc(src[i].at[1], out[i].at[1], ssem.at[i, 1], rsem.at[i, 1], (1 - x, 1 - y, 0)) for i in range(n)]
            for cp in cps:
                cp.start()
            for cp in cps:
                cp.wait()

    return pl.pallas_call(
        body, name="send_chip_sums", in_specs=[ANY] * n, out_specs=[ANY] * n,
        out_shape=[jax.ShapeDtypeStruct(a.shape, a.dtype) for a in arrs],
        scratch_shapes=[pltpu.SemaphoreType.DMA((n, 2)), pltpu.SemaphoreType.DMA((n, 2))],
        compiler_params=pltpu.CompilerParams(has_side_effects=True),
    )(*arrs)


def _chip_sum(g, recv, sel, name):
    _, r, C = g.shape
    tr = _rtile(r)

    def body(sel_ref, g_ref, r_ref, o32_ref, o16_ref):
        s = g_ref[...] + r_ref[...]
        o32_ref[...] = s
        o16_ref[...] = s.astype(BF16)

    blk = pl.BlockSpec((None, tr, C), lambda j, t, sel_ref: (j, t, 0))
    return pl.pallas_call(
        body, name=name,
        grid_spec=pltpu.PrefetchScalarGridSpec(
            num_scalar_prefetch=1, grid=(2, r // tr),
            in_specs=[pl.BlockSpec((None, tr, C), lambda j, t, sel_ref: (sel_ref[j], t, 0)), blk],
            out_specs=[blk, blk]),
        out_shape=[jax.ShapeDtypeStruct((2, r, C), F32), jax.ShapeDtypeStruct((2, r, C), BF16)],
        compiler_params=_params(("arbitrary", "arbitrary")),
    )(sel, g, recv)


def _cross_sum(s32, got, north, name):
    _, r, C = s32.shape
    tr = _rtile(r)

    def body(f_ref, p_ref, a_ref, b_ref, o_ref):
        a = a_ref[...].astype(F32)

        @pl.when(f_ref[0] == 1)
        def _():
            o_ref[...] = p_ref[...] + a

        @pl.when(f_ref[0] == 0)
        def _():
            o_ref[...] = a + b_ref[...].astype(F32)

    return pl.pallas_call(
        body, name=name,
        grid_spec=pltpu.PrefetchScalarGridSpec(
            num_scalar_prefetch=1, grid=(r // tr,),
            in_specs=[pl.BlockSpec((None, tr, C), lambda t, f: (0, t, 0)),
                      pl.BlockSpec((None, tr, C), lambda t, f: (0, t, 0)),
                      pl.BlockSpec((None, tr, C), lambda t, f: (1 - f[0], t, 0))],
            out_specs=pl.BlockSpec((tr, C), lambda t, f: (t, 0))),
        out_shape=jax.ShapeDtypeStruct((r, C), F32),
        compiler_params=_params(("arbitrary",)),
    )(north, s32, got, got)


def _reduce_scatter(grads):
    x, y, c, (k_me, k_x, k_y, k_d) = _where_am_i()
    sel = jnp.stack([jnp.where(c == 1, k_me, k_y), jnp.where(c == 1, k_x, k_d)]).astype(jnp.int32)
    north = jnp.reshape(c, (1,)).astype(jnp.int32)
    from_sib = _swap_slabs(grads, 2, "swap_grad_slabs")
    sums = [_chip_sum(g, r, sel, f"chip_sum{i}") for i, (g, r) in enumerate(zip(grads, from_sib))]
    got = _send_chip_sums([s16 for _, s16 in sums])
    parts = [_cross_sum(s32, b, north, f"cross_sum{i}") for i, ((s32, _), b) in enumerate(zip(sums, got))]
    theirs = _swap_slabs(parts, 0, "swap_parts")
    return list(zip(parts, theirs))


def _all_reduce_small(v):
    R, C = v.shape

    def body(v_ref, o_ref, buf, ssem, rsem, lsem):
        x, y, c = lax.axis_index("x"), lax.axis_index("y"), lax.axis_index("c")
        me, sib = (x, y, c), (x, y, 1 - c)
        chips = [(1 - x, y), (x, 1 - y), (1 - x, 1 - y)]

        def slot(px, py, pc):
            return buf.at[4 * px + 2 * py + pc]

        def copy(k, block, to, src=None):
            return _rc(slot(*block) if src is None else src, slot(*block), ssem.at[k], rsem.at[k], to)

        mine = pltpu.make_async_copy(v_ref, slot(*me), lsem)
        mine.start()
        first = [copy(0, me, sib, src=v_ref)] + [copy(1 + j, me, (*chip, c), src=v_ref) for j, chip in enumerate(chips)]
        for cp in first:
            cp.start()
        passed = [copy(4 + j, (*chip, c), sib) for j, chip in enumerate(chips)]
        for j, chip in enumerate(chips):
            copy(1 + j, (*chip, c), me).wait_recv()
            passed[j].start()
        copy(0, sib, me).wait_recv()
        for j, chip in enumerate(chips):
            copy(4 + j, (*chip, 1 - c), me).wait_recv()
        for cp in first + passed:
            cp.wait_send()
        mine.wait()
        acc = buf[0]
        for k in range(1, 8):
            acc = acc + buf[k]
        o_ref[...] = acc

    return pl.pallas_call(
        body, name="all_reduce_small",
        in_specs=[pl.BlockSpec(memory_space=pltpu.VMEM)], out_specs=pl.BlockSpec(memory_space=pltpu.VMEM),
        out_shape=jax.ShapeDtypeStruct((R, C), F32),
        scratch_shapes=[pltpu.VMEM((8, R, C), F32), pltpu.SemaphoreType.DMA((7,)), pltpu.SemaphoreType.DMA((7,)),
                        pltpu.SemaphoreType.DMA],
        compiler_params=pltpu.CompilerParams(has_side_effects=True, vmem_limit_bytes=VMEM_LIMIT),
    )(v)


WEIGHTS = ['ffn1_norm', 'ffn1_w_gate', 'ffn1_w_up', 'ffn1_w_down', 'mix_norm', 'w_in', 'conv_w', 'conv_b', 's5_A_re',
           's5_A_im', 's5_log_dt', 's5_B_re', 's5_B_im', 's5_C_re', 's5_C_im', 's5_D', 's5_w_glu', 's5_b_glu',
           'ssd_A_log', 'ssd_dt_bias', 'ssd_D', 'ssd_norm', 'w_proj_s5', 'w_proj_ssd', 'b_gate', 'w_out', 'ffn2_norm',
           'ffn2_w_gate', 'ffn2_w_up', 'ffn2_w_down', 'final_norm']
ARGS = ['x'] + WEIGHTS + ['loss_target'] + ['m_' + n for n in WEIGHTS] + ['v_' + n for n in WEIGHTS]
COL_SHARDED = ('ffn1_w_gate', 'ffn1_w_up', 'w_in', 'ffn2_w_gate', 'ffn2_w_up')
ROW_SHARDED = ('ffn1_w_down', 's5_w_glu', 'w_proj_s5', 'w_proj_ssd', 'w_out', 'ffn2_w_down')
MATRICES = COL_SHARDED + ROW_SHARDED
SMALL = [n for n in WEIGHTS if n not in MATRICES]


def _pack(arrs, width=1024):
    flat = jnp.concatenate([a.reshape(-1).astype(F32) for a in arrs])
    rows = -(-flat.shape[0] // (8 * width)) * 8
    return jnp.pad(flat, (0, rows * width - flat.shape[0])).reshape(rows, width)


def _unpack(packed, shapes):
    flat, out, o = packed.reshape(-1), [], 0
    for s in shapes:
        n = math.prod(s)
        out.append(flat[o:o + n].reshape(s))
        o += n
    return out


def kernel(x, ffn1_norm, ffn1_w_gate, ffn1_w_up, ffn1_w_down, mix_norm, w_in, conv_w, conv_b, s5_A_re, s5_A_im, s5_log_dt, s5_B_re, s5_B_im, s5_C_re, s5_C_im, s5_D, s5_w_glu, s5_b_glu, ssd_A_log, ssd_dt_bias, ssd_D, ssd_norm, w_proj_s5, w_proj_ssd, b_gate, w_out, ffn2_norm, ffn2_w_gate, ffn2_w_up, ffn2_w_down, final_norm, loss_target, m_ffn1_norm, m_ffn1_w_gate, m_ffn1_w_up, m_ffn1_w_down, m_mix_norm, m_w_in, m_conv_w, m_conv_b, m_s5_A_re, m_s5_A_im, m_s5_log_dt, m_s5_B_re, m_s5_B_im, m_s5_C_re, m_s5_C_im, m_s5_D, m_s5_w_glu, m_s5_b_glu, m_ssd_A_log, m_ssd_dt_bias, m_ssd_D, m_ssd_norm, m_w_proj_s5, m_w_proj_ssd, m_b_gate, m_w_out, m_ffn2_norm, m_ffn2_w_gate, m_ffn2_w_up, m_ffn2_w_down, m_final_norm, v_ffn1_norm, v_ffn1_w_gate, v_ffn1_w_up, v_ffn1_w_down, v_mix_norm, v_w_in, v_conv_w, v_conv_b, v_s5_A_re, v_s5_A_im, v_s5_log_dt, v_s5_B_re, v_s5_B_im, v_s5_C_re, v_s5_C_im, v_s5_D, v_s5_w_glu, v_s5_b_glu, v_ssd_A_log, v_ssd_dt_bias, v_ssd_D, v_ssd_norm, v_w_proj_s5, v_w_proj_ssd, v_b_gate, v_w_out, v_ffn2_norm, v_ffn2_w_gate, v_ffn2_w_up, v_ffn2_w_down, v_final_norm):
    a = dict(locals())
    assert list(a) == ARGS
    x, target = a['x'][0], a['loss_target'][0]
    k_me = 2 * lax.axis_index("x") + lax.axis_index("y")

    srcs = [a[n][0].T.astype(BF16) for n in COL_SHARDED] + [a[n][0].astype(BF16) for n in ROW_SHARDED] + [a['conv_w'][0]]
    full = _gather_shards(srcs)
    w = {n: f.reshape(-1, f.shape[-1]) for n, f in zip(MATRICES, full)}
    w['conv_w'] = full[-1].transpose(1, 0, 2).reshape(CONV_K, CONV_DIM)
    p = {n: a[n] for n in SMALL if n != 'conv_w'}
    p['final_norm'] = a['final_norm'][None]

    loss, grad_x, gp, gw = _local_step(x, target, p, w)

    pairs = _reduce_scatter([gw[n].reshape(4, -1, D_MODEL) for n in MATRICES])
    gp['conv_w'] = gw['conv_w'][None]
    small_shapes = [(1, 1)] + [gp[n].shape if n != 'final_norm' else (1, D_MODEL) for n in SMALL]
    red = _unpack(_all_reduce_small(_pack([loss] + [gp[n] for n in SMALL])), small_shapes)
    loss_all = red[0].reshape(())
    gsmall = dict(zip(SMALL, red[1:]))
    gsmall['conv_w'] = lax.dynamic_slice_in_dim(gsmall['conv_w'], k_me * 1024, 1024, axis=2)
    gsmall = {n: g.reshape(a[n].shape) for n, g in gsmall.items()}

    grads, delta, new_m, new_v = {}, {}, {}, {}
    for n, (mine, theirs) in zip(MATRICES, pairs):
        wn, mn, vn = a[n][0], a['m_' + n][0], a['v_' + n][0]
        if n in COL_SHARDED:
            (gt,) = _rowwise(lambda p_, q_: (p_ + q_,), [mine, theirs], [], [(D_MODEL, F32)], name=f"sum_{n}",
                             tm=_rtile(mine.shape[0]))
            g = gt.T
            d, m2, v2 = _adamw(wn, mn, vn, [g], name=f"adamw_{n}", emit_g=False)
        else:
            g, d, m2, v2 = _adamw(wn, mn, vn, [mine, theirs], name=f"adamw_{n}", emit_g=True)
        grads[n], delta[n], new_m[n], new_v[n] = g[None], d[None], m2[None], v2[None]
    sw, sm, sv, sg = [_pack([t[n] for n in SMALL]) for t in (a, {n: a['m_' + n] for n in SMALL},
                                                             {n: a['v_' + n] for n in SMALL}, gsmall)]
    d, m2, v2 = _adamw(sw, sm, sv, [sg], name="adamw_small", emit_g=False)
    shapes = [a[n].shape for n in SMALL]
    for n, dd, mm, vv in zip(SMALL, _unpack(d, shapes), _unpack(m2, shapes), _unpack(v2, shapes)):
        grads[n], delta[n], new_m[n], new_v[n] = gsmall[n], dd, mm, vv
    return (loss_all, grad_x[None], *[grads[n] for n in WEIGHTS], *[delta[n] for n in WEIGHTS],
            *[new_m[n] for n in WEIGHTS], *[new_v[n] for n in WEIGHTS])
```

```python
import functools
import math

import jax
import jax.numpy as jnp
from jax import lax
from jax.experimental import pallas as pl
from jax.experimental.pallas import tpu as pltpu

F32 = jnp.float32
BF16 = jnp.bfloat16
MXU = BF16
HI = lax.Precision.HIGHEST

D_MODEL = 1024
D_FF = 2816
EPS = 1e-6
S5_GROUPS, S5_GROUP, S5_STATE = 64, 16, 64
D_INNER = 2048
SSD_HEADDIM, SSD_HEADS, SSD_GROUPS, SSD_HPG, SSD_STATE = 64, 32, 8, 4, 128
CONV_K, CONV_DIM = 4, 4096
IN_SPLITS = (1024, 2048, 4096, 32, 2048)
IN_OFFS = (0, 1024, 3072, 7168, 7200, 9248)
SSD_Q = 256
ADAM_LR, ADAM_B1, ADAM_B2, ADAM_EPS, ADAM_WD, ADAM_STEP = 0.001, 0.9, 0.999, 1e-08, 0.01, 10

VMEM_LIMIT = 56 * 1024 * 1024
MATMUL_VMEM = 32 * 1024 * 1024
MESH = pl.DeviceIdType.MESH

NN = ((1,), (0,))
NT = ((1,), (1,))
TN = ((0,), (0,))


def _dot(a, b, dims, precision=None):
    if precision is None:
        a, b = a.astype(MXU), b.astype(MXU)
    return lax.dot_general(a, b, (dims, ((), ())), precision=precision, preferred_element_type=F32)


def _tile(n, pref):
    if n <= pref:
        return n
    best = None
    for t in range(128, pref + 1, 128):
        if n % t == 0:
            best = t
    assert best is not None, (n, pref)
    return best


def _params(sem):
    return pltpu.CompilerParams(dimension_semantics=sem, vmem_limit_bytes=VMEM_LIMIT)


def _matmul(a, b, mode, *, name, out=((F32),), epilogue=None, extras=(), tm=512, tn=1536):
    if mode == "nn":
        (M, K), (_, N) = a.shape, b.shape
    elif mode == "nt":
        (M, K), (N, _) = a.shape, b.shape
    else:
        (K, M), (_, N) = a.shape, b.shape
    tm, tn = _tile(M, tm), _tile(N, tn)

    def vmem(tm_, tn_):
        per = K * tm_ * a.dtype.itemsize + K * tn_ * b.dtype.itemsize
        per += sum(tm_ * tn_ * jnp.dtype(dt).itemsize for dt in out)
        per += sum((1 if e.shape[0] == 1 else tm_) * tn_ * e.dtype.itemsize for e in extras)
        return 2 * per

    while vmem(tm, tn) > MATMUL_VMEM and (tn > 128 or tm > 128):
        if tn >= tm and tn > 128:
            tn = _tile(N, tn - 128)
        else:
            tm = _tile(M, tm - 128)
    a_spec = pl.BlockSpec((K, tm), lambda i, j: (0, i)) if mode == "tn" else pl.BlockSpec((tm, K), lambda i, j: (i, 0))
    b_spec = pl.BlockSpec((tn, K), lambda i, j: (j, 0)) if mode == "nt" else pl.BlockSpec((K, tn), lambda i, j: (0, j))
    dims = {"nn": NN, "nt": NT, "tn": TN}[mode]
    e_specs = []
    for e in extras:
        if e.shape[0] == 1:
            e_specs.append(pl.BlockSpec((1, tn), lambda i, j: (0, j)))
        else:
            assert e.shape == (M, N), (e.shape, M, N)
            e_specs.append(pl.BlockSpec((tm, tn), lambda i, j: (i, j)))
    n_e, n_o = len(extras), len(out)

    def body(a_ref, b_ref, *refs):
        acc = _dot(a_ref[...], b_ref[...], dims)
        res = (acc,) if epilogue is None else epilogue(acc, *[r[...] for r in refs[:n_e]])
        for r, v in zip(refs[n_e:], res):
            r[...] = v.astype(r.dtype)

    res = pl.pallas_call(
        body, name=name, grid=(M // tm, N // tn),
        in_specs=[a_spec, b_spec] + e_specs,
        out_specs=[pl.BlockSpec((tm, tn), lambda i, j: (i, j)) for _ in range(n_o)],
        out_shape=[jax.ShapeDtypeStruct((M, N), dt) for dt in out],
        compiler_params=_params(("parallel", "parallel")),
    )(a, b, *extras)
    return res[0] if n_o == 1 else res


def _rowwise(fn, rows, fulls, outs, reds=(), *, name, tm=256):
    T = rows[0].shape[0]
    tm = min(tm, T)
    n_r, n_f, n_o, n_d = len(rows), len(fulls), len(outs), len(reds)

    def body(*refs):
        ins = [r[...] for r in refs[:n_r + n_f]]
        o_refs = refs[n_r + n_f:n_r + n_f + n_o]
        d_refs = refs[n_r + n_f + n_o:]
        res = fn(*ins)
        for r, v in zip(o_refs, res[:n_o]):
            r[...] = v.astype(r.dtype)
        if n_d:
            @pl.when(pl.program_id(0) == 0)
            def _():
                for r in d_refs:
                    r[...] = jnp.zeros_like(r)
            for r, v in zip(d_refs, res[n_o:]):
                r[...] += v.astype(r.dtype)

    res = pl.pallas_call(
        body, name=name, grid=(T // tm,),
        in_specs=[pl.BlockSpec((tm, r.shape[1]), lambda i: (i, 0)) for r in rows]
        + [pl.BlockSpec(f.shape, lambda i, nd=f.ndim: (0,) * nd) for f in fulls],
        out_specs=[pl.BlockSpec((tm, c), lambda i: (i, 0)) for c, _ in outs]
        + [pl.BlockSpec(s, lambda i, nd=len(s): (0,) * nd) for s, _ in reds],
        out_shape=[jax.ShapeDtypeStruct((T, c), dt) for c, dt in outs]
        + [jax.ShapeDtypeStruct(s, dt) for s, dt in reds],
        compiler_params=_params(("arbitrary",)),
    )(*rows, *fulls)
    return res


def _rms(x, g):
    return x * lax.rsqrt(jnp.mean(x * x, axis=-1, keepdims=True) + EPS) * g


def _colsum(v):
    return jnp.sum(v, axis=0, keepdims=True)


def _softplus(x):
    return jnp.maximum(x, 0.0) + jnp.log1p(jnp.exp(-jnp.abs(x)))


def _ffn_fwd(x, norm, wgT, wuT, wd, tag):
    D = x.shape[1]
    (hn,) = _rowwise(lambda xt, g: (_rms(xt, g),), [x], [norm], [(D, BF16)], name=f"{tag}_norm")
    a = _matmul(hn, wgT, "nt", name=f"{tag}_gate")
    b = _matmul(hn, wuT, "nt", name=f"{tag}_up")
    (hid,) = _rowwise(lambda at, bt: (jax.nn.silu(at) * bt,), [a, b], [], [(a.shape[1], BF16)], name=f"{tag}_act")
    y = _matmul(hid, wd, "nn", name=f"{tag}_down", epilogue=lambda acc, xt: (xt + 0.5 * acc,), extras=[x])
    return y, (hn, a, b, hid)


def _ffn_bwd(dy, x, norm, wgT, wuT, wd, res, tag):
    hn, a, b, hid = res
    D, F = x.shape[1], a.shape[1]
    dhid = _matmul(dy, wd, "nt", name=f"{tag}_d_hid", epilogue=lambda acc: (0.5 * acc,))
    d_wd = _matmul(hid, dy, "tn", name=f"{tag}_d_wd", epilogue=lambda acc: (0.5 * acc,))

    def act_bwd(at, bt, dh):
        _, vjp = jax.vjp(lambda p, q: jax.nn.silu(p) * q, at, bt)
        return vjp(dh)

    da, db = _rowwise(act_bwd, [a, b, dhid], [], [(F, BF16), (F, BF16)], name=f"{tag}_d_act")
    d_wgT = _matmul(da, hn, "tn", name=f"{tag}_d_wg")
    d_wuT = _matmul(db, hn, "tn", name=f"{tag}_d_wu")
    dhn = _matmul(da, wgT, "nn", name=f"{tag}_d_hn1")
    dhn = _matmul(db, wuT, "nn", name=f"{tag}_d_hn2", epilogue=lambda acc, e: (acc + e,), extras=[dhn])

    def norm_bwd(xt, dh, dyt, g):
        _, vjp = jax.vjp(_rms, xt, g)
        dx, dg = vjp(dh)
        return dyt + dx, dg

    dx, d_norm = _rowwise(norm_bwd, [x, dhn, dy], [norm], [(D, F32)], [((1, D), F32)], name=f"{tag}_d_norm")
    return dx, d_norm, d_wgT, d_wuT, d_wd


S5_TILES = 8
S5_HALF = 256


def _s5_derive(A_re, A_im, log_dt, B_re, B_im, C_re, C_im):
    G, N, M = S5_GROUPS, S5_STATE, S5_GROUP
    dt = jnp.exp(log_dt)[:, None]
    mag = jnp.exp(A_re * dt)
    ar, ai = mag * jnp.cos(A_im * dt), mag * jnp.sin(A_im * dt)
    den = A_re * A_re + A_im * A_im
    cr = ((ar - 1.0) * A_re + ai * A_im) / den
    ci = (ai * A_re - (ar - 1.0) * A_im) / den
    bbr = cr[..., None] * B_re - ci[..., None] * B_im
    bbi = cr[..., None] * B_im + ci[..., None] * B_re
    eye = jnp.eye(8, dtype=F32)

    def tile_in(bb):
        t = bb.reshape(S5_TILES, 8, N, M).transpose(0, 1, 3, 2)
        return jnp.einsum("jamn,ab->jambn", t, eye).reshape(S5_TILES, 8 * M, 8 * N)

    def tile_out(c):
        t = c.reshape(S5_TILES, 8, M, N).transpose(0, 1, 3, 2)
        return jnp.einsum("janm,ab->janbm", t, eye).reshape(S5_TILES, 8 * N, 8 * M)

    return (tile_in(bbr), tile_in(bbi), tile_out(C_re), tile_out(C_im),
            ar.reshape(S5_TILES, 1, 8 * N), ai.reshape(S5_TILES, 1, 8 * N))


S5_NB = S5_HALF // 128
S5_SEG = 32


def _cmul(ar, ai, br, bi):
    return ar * br - ai * bi, ar * bi + ai * br


def _s5_scan(sr_ref, si_ref, ar, ai, T, reverse):
    L, V = T // S5_SEG, S5_SEG // 8
    assert L * S5_SEG == T and L & (L - 1) == 0, T
    sg = -1.0 if reverse else 1.0
    a_r = [jnp.broadcast_to(ar[:, 128 * b:128 * (b + 1)], (8, 128)) for b in range(S5_NB)]
    a_i = [jnp.broadcast_to(sg * ai[:, 128 * b:128 * (b + 1)], (8, 128)) for b in range(S5_NB)]

    def rows(k, v):
        return pl.ds(((L - 1 - k) if reverse else k) + 8 * L * v, 8, stride=L)

    def local(k, carry):
        out = []
        for b in range(S5_NB):
            for v in range(V):
                idx = rows(k, v)
                mr, mi = _cmul(a_r[b], a_i[b], *carry[b * V + v])
                nr, ni = mr + sr_ref[b, idx, :], mi + si_ref[b, idx, :]
                sr_ref[b, idx, :] = nr
                si_ref[b, idx, :] = ni
                out.append((nr, ni))
        return tuple(out)

    z = jnp.zeros((8, 128), F32)
    ends = lax.fori_loop(0, L, local, tuple((z, z) for _ in range(S5_NB * V)))

    carries = []
    for b in range(S5_NB):
        pr, pi = a_r[b][0:1], a_i[b][0:1]
        n = L
        while n > 1:
            pr, pi = _cmul(pr, pi, pr, pi)
            n //= 2
        cr, ci = jnp.zeros((1, 128), F32), jnp.zeros((1, 128), F32)
        into = [None] * S5_SEG
        for j in (reversed(range(S5_SEG)) if reverse else range(S5_SEG)):
            into[j] = (cr, ci)
            er, ei = ends[b * V + j // 8]
            mr, mi = _cmul(pr, pi, cr, ci)
            cr, ci = mr + er[j % 8:j % 8 + 1], mi + ei[j % 8:j % 8 + 1]
        carries.append([(jnp.concatenate([into[8 * v + s][0] for s in range(8)], axis=0),
                         jnp.concatenate([into[8 * v + s][1] for s in range(8)], axis=0)) for v in range(V)])

    def fix(k, powers):
        out = []
        for b in range(S5_NB):
            pr, pi = powers[b]
            for v in range(V):
                idx = rows(k, v)
                dr, di = _cmul(pr, pi, *carries[b][v])
                sr_ref[b, idx, :] += dr
                si_ref[b, idx, :] += di
            out.append(_cmul(pr, pi, a_r[b], a_i[b]))
        return tuple(out)

    lax.fori_loop(0, L, fix, tuple((a_r[b], a_i[b]) for b in range(S5_NB)))


def _s5_in(ut, b_ref, s_ref):
    for b in range(S5_NB):
        s_ref[b] = _dot(ut, b_ref[:, 128 * b:128 * (b + 1)], NN)


def _s5_specs(T):
    u_spec = pl.BlockSpec((T, 128), lambda j, h: (0, j))
    b_spec = pl.BlockSpec((None, 128, S5_HALF), lambda j, h: (j, 0, h))
    c_spec = pl.BlockSpec((None, S5_HALF, 128), lambda j, h: (j, h, 0))
    a_spec = pl.BlockSpec((None, 1, S5_HALF), lambda j, h: (j, 0, h))
    return u_spec, b_spec, c_spec, a_spec


def _s5_fwd(u, tiles):
    T = u.shape[0]
    u_spec, b_spec, c_spec, a_spec = _s5_specs(T)

    def body(u_ref, br_ref, bi_ref, cr_ref, ci_ref, ar_ref, ai_ref, y_ref, sr_ref, si_ref):
        ut = u_ref[...].astype(MXU)
        _s5_in(ut, br_ref, sr_ref)
        _s5_in(ut, bi_ref, si_ref)
        _s5_scan(sr_ref, si_ref, ar_ref[...], ai_ref[...], T, False)
        y = None
        for b in range(S5_NB):
            blk = slice(128 * b, 128 * (b + 1))
            yb = _dot(sr_ref[b], cr_ref[blk, :], NN) - _dot(si_ref[b], ci_ref[blk, :], NN)
            y = yb if y is None else y + yb

        @pl.when(pl.program_id(1) == 0)
        def _():
            y_ref[...] = y

        @pl.when(pl.program_id(1) == 1)
        def _():
            y_ref[...] += y

    scr = pltpu.VMEM((S5_NB, T, 128), F32)
    return pl.pallas_call(
        body, name="s5_fwd", grid=(S5_TILES, 2),
        in_specs=[u_spec, b_spec, b_spec, c_spec, c_spec, a_spec, a_spec],
        out_specs=u_spec, out_shape=jax.ShapeDtypeStruct(u.shape, F32),
        scratch_shapes=[scr, scr],
        compiler_params=_params(("parallel", "arbitrary")),
    )(u, *tiles)


def _s5_bwd(u, dys, du_skip, tiles):
    T = u.shape[0]
    u_spec, b_spec, c_spec, a_spec = _s5_specs(T)

    def body(u_ref, dy_ref, sk_ref, br_ref, bi_ref, cr_ref, ci_ref, ar_ref, ai_ref,
             du_ref, dbr_ref, dbi_ref, dcr_ref, dci_ref, dar_ref, dai_ref, sr_ref, si_ref, lr_ref, li_ref):
        ut, dy = u_ref[...].astype(MXU), dy_ref[...].astype(MXU)
        ar, ai = ar_ref[...], ai_ref[...]
        _s5_in(ut, br_ref, sr_ref)
        _s5_in(ut, bi_ref, si_ref)
        _s5_scan(sr_ref, si_ref, ar, ai, T, False)
        for b in range(S5_NB):
            blk = slice(128 * b, 128 * (b + 1))
            dcr_ref[blk, :] = _dot(sr_ref[b], dy, TN)
            dci_ref[blk, :] = -_dot(si_ref[b], dy, TN)
            lr_ref[b] = _dot(dy, cr_ref[blk, :], NT)
            li_ref[b] = -_dot(dy, ci_ref[blk, :], NT)
        _s5_scan(lr_ref, li_ref, ar, ai, T, True)
        du = None
        first = lax.broadcasted_iota(jnp.int32, (T, 128), 0) == 0
        for b in range(S5_NB):
            blk = slice(128 * b, 128 * (b + 1))
            lr, li = lr_ref[b], li_ref[b]
            pr = jnp.where(first, 0.0, pltpu.roll(sr_ref[b], 1, 0))
            pi = jnp.where(first, 0.0, pltpu.roll(si_ref[b], 1, 0))
            dar_ref[:, blk] = _colsum(lr * pr + li * pi)
            dai_ref[:, blk] = _colsum(li * pr - lr * pi)
            dbr_ref[:, blk] = _dot(ut, lr, TN)
            dbi_ref[:, blk] = _dot(ut, li, TN)
            dub = _dot(lr, br_ref[:, blk], NT) + _dot(li, bi_ref[:, blk], NT)
            du = dub if du is None else du + dub

        @pl.when(pl.program_id(1) == 0)
        def _():
            du_ref[...] = du + sk_ref[...]

        @pl.when(pl.program_id(1) == 1)
        def _():
            du_ref[...] += du

    scr = pltpu.VMEM((S5_NB, T, 128), F32)
    return pl.pallas_call(
        body, name="s5_bwd", grid=(S5_TILES, 2),
        in_specs=[u_spec, u_spec, u_spec, b_spec, b_spec, c_spec, c_spec, a_spec, a_spec],
        out_specs=[u_spec, b_spec, b_spec, c_spec, c_spec, a_spec, a_spec],
        out_shape=[jax.ShapeDtypeStruct(u.shape, F32)] + [jax.ShapeDtypeStruct(t.shape, F32) for t in tiles],
        scratch_shapes=[scr, scr, scr, scr],
        compiler_params=_params(("parallel", "arbitrary")),
    )(u, dys, du_skip, *tiles)


CONV_TILE = 256


def _conv_pre(x, w, b):
    T = x.shape[0]
    row = lax.broadcasted_iota(jnp.int32, x.shape, 0)
    acc = x * w[CONV_K - 1:CONV_K, :] + b
    for lag in range(1, CONV_K):
        acc = acc + jnp.where(row >= lag, pltpu.roll(x, lag, 0), 0.0) * w[CONV_K - 1 - lag:CONV_K - lag, :]
    return acc


def _conv_fwd(x, w, b):
    T, C = x.shape
    col = pl.BlockSpec((T, CONV_TILE), lambda j: (0, j))

    def body(x_ref, w_ref, b_ref, o_ref):
        o_ref[...] = jax.nn.silu(_conv_pre(x_ref[...], w_ref[...], b_ref[...]))

    return pl.pallas_call(
        body, name="conv_fwd", grid=(C // CONV_TILE,),
        in_specs=[col, pl.BlockSpec((CONV_K, CONV_TILE), lambda j: (0, j)), pl.BlockSpec((1, CONV_TILE), lambda j: (0, j))],
        out_specs=col, out_shape=jax.ShapeDtypeStruct((T, C), F32),
        compiler_params=_params(("parallel",)),
    )(x, w, b)


def _conv_bwd(x, w, b, dout, col0, name):
    T, C = dout.shape
    off = col0 // CONV_TILE
    xcol = pl.BlockSpec((T, CONV_TILE), lambda j: (0, j + off))
    dcol = pl.BlockSpec((T, CONV_TILE), lambda j: (0, j))

    def body(x_ref, w_ref, b_ref, d_ref, dx_ref, dw_ref, db_ref):
        x, w = x_ref[...], w_ref[...]
        _, vjp = jax.vjp(jax.nn.silu, _conv_pre(x, w, b_ref[...]))
        (dy,) = vjp(d_ref[...])
        row = lax.broadcasted_iota(jnp.int32, x.shape, 0)
        dx = dy * w[CONV_K - 1:CONV_K, :]
        dw_ref[CONV_K - 1:CONV_K, :] = _colsum(dy * x)
        for lag in range(1, CONV_K):
            dx = dx + jnp.where(row < T - lag, pltpu.roll(dy, T - lag, 0), 0.0) * w[CONV_K - 1 - lag:CONV_K - lag, :]
            xs = jnp.where(row >= lag, pltpu.roll(x, lag, 0), 0.0)
            dw_ref[CONV_K - 1 - lag:CONV_K - lag, :] = _colsum(dy * xs)
        dx_ref[...] = dx.astype(dx_ref.dtype)
        db_ref[...] = _colsum(dy)

    return pl.pallas_call(
        body, name=name, grid=(C // CONV_TILE,),
        in_specs=[xcol, pl.BlockSpec((CONV_K, CONV_TILE), lambda j: (0, j + off)),
                  pl.BlockSpec((1, CONV_TILE), lambda j: (0, j + off)), dcol],
        out_specs=[dcol, pl.BlockSpec((CONV_K, CONV_TILE), lambda j: (0, j)), pl.BlockSpec((1, CONV_TILE), lambda j: (0, j))],
        out_shape=[jax.ShapeDtypeStruct((T, C), BF16), jax.ShapeDtypeStruct((CONV_K, C), F32), jax.ShapeDtypeStruct((1, C), F32)],
        compiler_params=_params(("parallel",)),
    )(x, w, b, dout)


def _ssd_common(x_ref, b_ref, c_ref, dtr_ref, bias_ref, alog_ref, tri_ref, triu_ref):
    Q = x_ref.shape[0]
    x, Bm, Cm = x_ref[...], b_ref[...], c_ref[...]
    pre = dtr_ref[...] + bias_ref[...]
    dt = _softplus(pre)
    A = -jnp.exp(alog_ref[...])
    adt = dt * A
    ac4 = _dot(tri_ref[...], adt, NN, HI)
    ar4 = _dot(adt, triu_ref[...], TN, HI)
    cb = _dot(Cm, Bm, NT)
    ii = lax.broadcasted_iota(jnp.int32, (Q, Q), 0)
    jj = lax.broadcasted_iota(jnp.int32, (Q, Q), 1)
    causal = ii >= jj
    return x, Bm, Cm, pre, dt, A, ac4, ar4, cb, causal


def _ssd_head(k, x, dt, ac4, ar4, cb, causal):
    P = SSD_HEADDIM
    Q = x.shape[0]
    xk = x[:, P * k:P * (k + 1)]
    dtk = dt[:, k:k + 1]
    ac = ac4[:, k:k + 1]
    L = jnp.where(causal, jnp.exp(jnp.where(causal, ac - ar4[k:k + 1, :], 0.0)), 0.0)
    G = cb * L
    xdt = xk * dtk
    atot = ac[Q - 1:Q, :]
    return xk, dtk, ac, L, G, xdt, atot


def _ssd_specs(T, Q, rev):
    NC = T // Q
    cc = (lambda c: NC - 1 - c) if rev else (lambda c: c)
    x_spec = pl.BlockSpec((Q, SSD_HPG * SSD_HEADDIM), lambda g, c: (cc(c), g))
    b_spec = pl.BlockSpec((Q, SSD_STATE), lambda g, c: (cc(c), D_INNER // SSD_STATE + g))
    c_spec = pl.BlockSpec((Q, SSD_STATE), lambda g, c: (cc(c), D_INNER // SSD_STATE + SSD_GROUPS + g))
    dt_spec = pl.BlockSpec((None, Q, SSD_HPG), lambda g, c: (g, cc(c), 0))
    p_spec = pl.BlockSpec((None, 1, SSD_HPG), lambda g, c: (g, 0, 0))
    tri_spec = pl.BlockSpec((Q, Q), lambda g, c: (0, 0))
    h_spec = pl.BlockSpec((None, None, SSD_HPG, SSD_HEADDIM, SSD_STATE), lambda g, c: (g, cc(c), 0, 0, 0))
    return x_spec, b_spec, c_spec, dt_spec, p_spec, tri_spec, h_spec


def _tri(Q):
    tri = jnp.tril(jnp.ones((Q, Q), F32))
    return tri, tri.T


def _ssd_fwd(xc, dtr8, bias8, alog8, dsk8):
    T, Q = xc.shape[0], SSD_Q
    NC = T // Q
    x_spec, b_spec, c_spec, dt_spec, p_spec, tri_spec, h_spec = _ssd_specs(T, Q, False)
    y_spec = pl.BlockSpec((Q, SSD_HPG * SSD_HEADDIM), lambda g, c: (c, g))

    def body(x_ref, b_ref, c_ref, dtr_ref, bias_ref, alog_ref, dsk_ref, tri_ref, triu_ref, y_ref, hs_ref, h_scr):
        @pl.when(pl.program_id(1) == 0)
        def _():
            h_scr[...] = jnp.zeros_like(h_scr)

        hs_ref[...] = h_scr[...]
        x, Bm, Cm, pre, dt, A, ac4, ar4, cb, causal = _ssd_common(
            x_ref, b_ref, c_ref, dtr_ref, bias_ref, alog_ref, tri_ref, triu_ref)
        dsk = dsk_ref[...]
        for k in range(SSD_HPG):
            xk, dtk, ac, L, G, xdt, atot = _ssd_head(k, x, dt, ac4, ar4, cb, causal)
            hin = h_scr[k]
            y = _dot(G, xdt, NN) + _dot(Cm, hin, NT) * jnp.exp(ac) + dsk[:, k:k + 1] * xk
            y_ref[:, SSD_HEADDIM * k:SSD_HEADDIM * (k + 1)] = y
            xw = xdt * jnp.exp(atot - ac)
            h_scr[k] = jnp.exp(atot) * hin + _dot(xw, Bm, TN)

    tri, triu = _tri(Q)
    return pl.pallas_call(
        body, name="ssd_fwd", grid=(SSD_GROUPS, NC),
        in_specs=[x_spec, b_spec, c_spec, dt_spec, p_spec, p_spec, p_spec, tri_spec, tri_spec],
        out_specs=[y_spec, h_spec],
        out_shape=[jax.ShapeDtypeStruct((T, D_INNER), F32),
                   jax.ShapeDtypeStruct((SSD_GROUPS, NC, SSD_HPG, SSD_HEADDIM, SSD_STATE), F32)],
        scratch_shapes=[pltpu.VMEM((SSD_HPG, SSD_HEADDIM, SSD_STATE), F32)],
        compiler_params=_params(("parallel", "arbitrary")),
    )(xc, xc, xc, dtr8, bias8, alog8, dsk8, tri, triu)


def _ssd_bwd(xc, dtr8, bias8, alog8, dsk8, hs, dy):
    T, Q = xc.shape[0], SSD_Q
    NC = T // Q
    P = SSD_HEADDIM
    x_spec, b_spec, c_spec, dt_spec, p_spec, tri_spec, h_spec = _ssd_specs(T, Q, True)
    dy_spec = pl.BlockSpec((Q, SSD_HPG * P), lambda g, c: (NC - 1 - c, g))
    dbc_spec = pl.BlockSpec((Q, SSD_STATE), lambda g, c: (NC - 1 - c, g))

    def body(x_ref, b_ref, c_ref, dtr_ref, bias_ref, alog_ref, dsk_ref, tri_ref, triu_ref, hs_ref, dy_ref,
             dx_ref, db_ref, dc_ref, ddtr_ref, dbias_ref, dalog_ref, ddsk_ref, dh_scr):
        @pl.when(pl.program_id(1) == 0)
        def _():
            dh_scr[...] = jnp.zeros_like(dh_scr)
            dbias_ref[...] = jnp.zeros_like(dbias_ref)
            dalog_ref[...] = jnp.zeros_like(dalog_ref)
            ddsk_ref[...] = jnp.zeros_like(ddsk_ref)

        x, Bm, Cm, pre, dt, A, ac4, ar4, cb, causal = _ssd_common(
            x_ref, b_ref, c_ref, dtr_ref, bias_ref, alog_ref, tri_ref, triu_ref)
        dsk = dsk_ref[...]
        dyv = dy_ref[...]
        triu = triu_ref[...]
        ones = jnp.ones((Q, 1), F32)
        last = lax.broadcasted_iota(jnp.int32, (Q, 1), 0) == Q - 1
        dcb = jnp.zeros((Q, Q), F32)
        dB = jnp.zeros((Q, SSD_STATE), F32)
        dC = jnp.zeros((Q, SSD_STATE), F32)
        for k in range(SSD_HPG):
            xk, dtk, ac, L, G, xdt, atot = _ssd_head(k, x, dt, ac4, ar4, cb, causal)
            hin = hs_ref[k]
            dho = dh_scr[k]
            dyk = dyv[:, P * k:P * (k + 1)]
            E = jnp.exp(ac)
            Fd = jnp.exp(atot - ac)
            etot = jnp.exp(atot)
            ddsk_ref[:, k:k + 1] += jnp.sum(dyk * xk, keepdims=True)
            dxk = dsk[:, k:k + 1] * dyk
            Z = _dot(Cm, hin, NT)
            dZ = dyk * E
            dac = jnp.sum(dyk * Z, axis=1, keepdims=True) * E
            dC = dC + _dot(dZ, hin, NN)
            dhin = _dot(dZ, Cm, TN) + etot * dho
            datot = jnp.sum(dho * hin, keepdims=True) * etot
            xw = xdt * Fd
            dxw = _dot(Bm, dho, NT)
            dB = dB + _dot(xw, dho, NN)
            dxdt = dxw * Fd
            dFa = jnp.sum(dxw * xdt, axis=1, keepdims=True) * Fd
            datot = datot + jnp.sum(dFa, keepdims=True)
            dac = dac - dFa
            dG = _dot(dyk, xdt, NT)
            dxdt = dxdt + _dot(G, dyk, TN)
            dcb = dcb + dG * L
            Mseg = dG * G
            dac = dac + jnp.sum(Mseg, axis=1, keepdims=True) - _dot(Mseg, ones, TN, HI)
            dac = dac + jnp.where(last, datot, 0.0)
            dxk = dxk + dxdt * dtk
            ddtk = jnp.sum(dxdt * xk, axis=1, keepdims=True)
            dadt = _dot(triu, dac, NN, HI)
            Ak = A[:, k:k + 1]
            ddtk = ddtk + dadt * Ak
            dalog_ref[:, k:k + 1] += jnp.sum(dadt * dtk, keepdims=True) * Ak
            ddtr = ddtk * jax.nn.sigmoid(pre[:, k:k + 1])
            dbias_ref[:, k:k + 1] += jnp.sum(ddtr, keepdims=True)
            ddtr_ref[:, k:k + 1] = ddtr
            dx_ref[:, P * k:P * (k + 1)] = dxk.astype(dx_ref.dtype)
            dh_scr[k] = dhin
        dC = dC + _dot(dcb, Bm, NN)
        dB = dB + _dot(dcb, Cm, TN)
        db_ref[...] = dB
        dc_ref[...] = dC

    tri, triu = _tri(Q)
    return pl.pallas_call(
        body, name="ssd_bwd", grid=(SSD_GROUPS, NC),
        in_specs=[x_spec, b_spec, c_spec, dt_spec, p_spec, p_spec, p_spec, tri_spec, tri_spec, h_spec, dy_spec],
        out_specs=[dy_spec, dbc_spec, dbc_spec, dt_spec, p_spec, p_spec, p_spec],
        out_shape=[jax.ShapeDtypeStruct((T, D_INNER), F32),
                   jax.ShapeDtypeStruct((T, SSD_GROUPS * SSD_STATE), F32),
                   jax.ShapeDtypeStruct((T, SSD_GROUPS * SSD_STATE), F32),
                   jax.ShapeDtypeStruct(dtr8.shape, F32)] + [jax.ShapeDtypeStruct(bias8.shape, F32)] * 3,
        scratch_shapes=[pltpu.VMEM((SSD_HPG, P, SSD_STATE), F32)],
        compiler_params=_params(("parallel", "arbitrary")),
    )(xc, xc, xc, dtr8, bias8, alog8, dsk8, tri, triu, hs, dy)


def _to_groups(v):
    return v.reshape(SSD_GROUPS, 1, SSD_HPG)


def _mixer_fwd(x, p, w):
    T, D = x.shape
    (hn,) = _rowwise(lambda xt, g: (_rms(xt, g),), [x], [p["mix_norm"]], [(D, BF16)], name="mix_norm")
    winT = w["w_in"]
    u, z, xbc, dtr, gl = [
        _matmul(hn, winT[IN_OFFS[i]:IN_OFFS[i + 1]], "nt", name=f"mix_in{i}") for i in range(5)]
    tiles = _s5_derive(*[p[k][0] for k in ("s5_A_re", "s5_A_im", "s5_log_dt", "s5_B_re", "s5_B_im", "s5_C_re", "s5_C_im")])
    ys = _s5_fwd(u, tiles)
    (g,) = _rowwise(lambda yt, ut, d: (jax.nn.gelu(yt + d * ut),), [ys, u], [p["s5_D"]], [(D, F32)], name="s5_gelu")
    y5 = _matmul(g, w["s5_w_glu"], "nn", name="s5_glu", out=(BF16,),
                 epilogue=lambda acc, gt, b: (gt * jax.nn.sigmoid(acc + b),), extras=[g, p["s5_b_glu"]])
    xc = _conv_fwd(xbc, w["conv_w"], p["conv_b"])
    dtr8 = dtr.reshape(T, SSD_GROUPS, SSD_HPG).transpose(1, 0, 2)
    ssd_p = [_to_groups(p[k]) for k in ("ssd_dt_bias", "ssd_A_log", "ssd_D")]
    yssd_raw, hs = _ssd_fwd(xc, dtr8, *ssd_p)
    (yssd,) = _rowwise(lambda yt, zt, nw: (_rms(yt * jax.nn.silu(zt), nw),), [yssd_raw, z], [p["ssd_norm"]],
                       [(D_INNER, BF16)], name="ssd_gate")
    p5 = _matmul(y5, w["w_proj_s5"], "nn", name="mix_p5")
    pssd = _matmul(yssd, w["w_proj_ssd"], "nn", name="mix_pssd")

    def merge(glt, at, bt, bg):
        gates = jax.nn.sigmoid(glt + bg)
        return (gates[:, :D] * at + gates[:, D:] * bt,)

    (merged,) = _rowwise(merge, [gl, p5, pssd], [p["b_gate"]], [(D, BF16)], name="mix_merge")
    y = _matmul(merged, w["w_out"], "nn", name="mix_out", epilogue=lambda acc, xt: (xt + acc,), extras=[x])
    res = dict(hn=hn, u=u, z=z, xbc=xbc, gl=gl, tiles=tiles, ys=ys, g=g, y5=y5, xc=xc, dtr8=dtr8, ssd_p=ssd_p,
               yssd_raw=yssd_raw, hs=hs, yssd=yssd, p5=p5, pssd=pssd, merged=merged)
    return y, res


def _mixer_bwd(dy, x, p, w, r):
    T, D = x.shape
    gp = {}
    dmerged = _matmul(dy, w["w_out"], "nt", name="mix_d_merged")
    gp["w_out"] = _matmul(r["merged"], dy, "tn", name="mix_d_wout")

    def merge_bwd(glt, at, bt, dm, bg):
        def f(q, a_, b_):
            gates = jax.nn.sigmoid(q + bg)
            return gates[:, :D] * a_ + gates[:, D:] * b_
        _, vjp = jax.vjp(f, glt, at, bt)
        dq, da_, db_ = vjp(dm)
        return da_, db_, dq, _colsum(dq)

    dp5, dpssd, dgl, gp["b_gate"] = _rowwise(
        merge_bwd, [r["gl"], r["p5"], r["pssd"], dmerged], [p["b_gate"]],
        [(D, BF16), (D, BF16), (2 * D, BF16)], [((1, 2 * D), F32)], name="mix_d_merge")
    dy5 = _matmul(dp5, w["w_proj_s5"], "nt", name="mix_d_y5")
    gp["w_proj_s5"] = _matmul(r["y5"], dp5, "tn", name="mix_d_wp5")
    dyssd = _matmul(dpssd, w["w_proj_ssd"], "nt", name="mix_d_yssd")
    gp["w_proj_ssd"] = _matmul(r["yssd"], dpssd, "tn", name="mix_d_wpssd")

    def gate_bwd(yt, zt, dyt, nw):
        _, vjp = jax.vjp(lambda a_, b_, c_: _rms(a_ * jax.nn.silu(b_), c_), yt, zt, nw)
        return vjp(dyt)

    dyraw, dz, gp["ssd_norm"] = _rowwise(
        gate_bwd, [r["yssd_raw"], r["z"], dyssd], [p["ssd_norm"]],
        [(D_INNER, F32), (D_INNER, BF16)], [((1, D_INNER), F32)], name="ssd_d_gate")
    dxs, dBm, dCm, ddtr8, dbias8, dalog8, ddsk8 = _ssd_bwd(r["xc"], r["dtr8"], *r["ssd_p"], r["hs"], dyraw)
    gp["ssd_dt_bias"], gp["ssd_A_log"], gp["ssd_D"] = [v.reshape(1, SSD_HEADS) for v in (dbias8, dalog8, ddsk8)]
    ddtr = ddtr8.transpose(1, 0, 2).reshape(T, SSD_HEADS)
    conv = [_conv_bwd(r["xbc"], w["conv_w"], p["conv_b"], d, c0, f"conv_bwd{i}")
            for i, (d, c0) in enumerate(((dxs, 0), (dBm, D_INNER), (dCm, D_INNER + SSD_GROUPS * SSD_STATE)))]
    dxbc = [c[0] for c in conv]
    gp["conv_w"] = jnp.concatenate([c[1] for c in conv], axis=1)
    gp["conv_b"] = jnp.concatenate([c[2] for c in conv], axis=1)

    g = r["g"]

    def glu_ep(acc, gt, b, dyt):
        _, vjp = jax.vjp(lambda g_, t_: g_ * jax.nn.sigmoid(t_ + b), gt, acc)
        return vjp(dyt)

    dg1, dt_ = _matmul(g, w["s5_w_glu"], "nn", name="s5_d_glu", out=(F32, BF16), epilogue=glu_ep,
                       extras=[g, p["s5_b_glu"], dy5])
    gp["s5_w_glu"] = _matmul(g, dt_, "tn", name="s5_d_wglu")
    dg = _matmul(dt_, w["s5_w_glu"], "nt", name="s5_d_g", epilogue=lambda acc, e: (acc + e,), extras=[dg1])

    def gelu_bwd(yt, ut, dgt, dtt, d):
        _, vjp = jax.vjp(lambda y_, d_: jax.nn.gelu(y_ + d_ * ut), yt, d)
        dys, dd = vjp(dgt)
        return dys, dys * d, dd, _colsum(dtt.astype(F32))

    dys, dusk, gp["s5_D"], gp["s5_b_glu"] = _rowwise(
        gelu_bwd, [r["ys"], r["u"], dg, dt_], [p["s5_D"]], [(D, F32), (D, F32)], [((1, D), F32), ((1, D), F32)],
        name="s5_d_gelu")
    du, *dtiles = _s5_bwd(r["u"], dys, dusk, r["tiles"])

    pieces = [du, dz, None, ddtr, dgl]
    winT = w["w_in"]
    hn = r["hn"]
    d_rows, dhn = [], None
    cols = [(du, 0, 1024), (dz, 1024, 3072), (dxbc[0], 3072, 5120), (dxbc[1], 5120, 6144), (dxbc[2], 6144, 7168),
            (ddtr, 7168, 7200), (dgl, 7200, 9248)]
    for i, (d, lo, hi) in enumerate(cols):
        d_rows.append(_matmul(d, hn, "tn", name=f"mix_d_win{i}"))
        if dhn is None:
            dhn = _matmul(d, winT[lo:hi], "nn", name=f"mix_d_hn{i}")
        else:
            dhn = _matmul(d, winT[lo:hi], "nn", name=f"mix_d_hn{i}", epilogue=lambda acc, e: (acc + e,), extras=[dhn])
    gp["w_in"] = jnp.concatenate(d_rows, axis=0)
    del pieces

    def norm_bwd(xt, dh, dyt, gn):
        _, vjp = jax.vjp(_rms, xt, gn)
        dx, dgn = vjp(dh)
        return dyt + dx, dgn

    dx, gp["mix_norm"] = _rowwise(norm_bwd, [x, dhn, dy], [p["mix_norm"]], [(D, F32)], [((1, D), F32)], name="mix_d_norm")
    return dx, gp, dtiles


def _s5_param_grads(p, dtiles):
    keys = ("s5_A_re", "s5_A_im", "s5_log_dt", "s5_B_re", "s5_B_im", "s5_C_re", "s5_C_im")
    _, vjp = jax.vjp(_s5_derive, *[p[k][0] for k in keys])
    return {k: v[None] for k, v in zip(keys, vjp(tuple(dtiles)))}


def _local_step(x, target, p, w):
    T, D = x.shape
    x1, r1 = _ffn_fwd(x, p["ffn1_norm"], w["ffn1_w_gate"], w["ffn1_w_up"], w["ffn1_w_down"], "ffn1")
    x2, rm = _mixer_fwd(x1, p, w)
    x3, r2 = _ffn_fwd(x2, p["ffn2_norm"], w["ffn2_w_gate"], w["ffn2_w_up"], w["ffn2_w_down"], "ffn2")

    def head(xt, tt, g):
        def f(x_, g_):
            e = _rms(x_, g_) - tt
            return 0.5 * jnp.sum(jnp.mean(e * e, axis=-1))
        l, (dx_, dg_) = jax.value_and_grad(f, argnums=(0, 1))(xt, g)
        return dx_, l.reshape(1, 1), dg_

    dx3, loss, d_final = _rowwise(head, [x3, target], [p["final_norm"]], [(D, F32)],
                                  [((1, 1), F32), ((1, D), F32)], name="loss_head")
    gp, gw = {"final_norm": d_final}, {}
    dx2, gp["ffn2_norm"], gw["ffn2_w_gate"], gw["ffn2_w_up"], gw["ffn2_w_down"] = _ffn_bwd(
        dx3, x2, p["ffn2_norm"], w["ffn2_w_gate"], w["ffn2_w_up"], w["ffn2_w_down"], r2, "ffn2")
    dx1, gm, dtiles = _mixer_bwd(dx2, x1, p, w, rm)
    for k in ("w_out", "w_proj_s5", "w_proj_ssd", "s5_w_glu", "w_in", "conv_w"):
        gw[k] = gm.pop(k)
    gp.update(gm)
    gp.update(_s5_param_grads(p, dtiles))
    dx0, gp["ffn1_norm"], gw["ffn1_w_gate"], gw["ffn1_w_up"], gw["ffn1_w_down"] = _ffn_bwd(
        dx1, x, p["ffn1_norm"], w["ffn1_w_gate"], w["ffn1_w_up"], w["ffn1_w_down"], r1, "ffn1")
    return loss, dx0, gp, gw


def _adamw(w, m, v, gs, *, name, emit_g, tm=128):
    n_g = len(gs)
    C = w.shape[1]

    def fn(wt, mt, vt, *gt):
        g = gt[0]
        for e in gt[1:]:
            g = g + e
        m2 = ADAM_B1 * mt + (1.0 - ADAM_B1) * g
        v2 = ADAM_B2 * vt + (1.0 - ADAM_B2) * (g * g)
        m_hat = m2 / (1.0 - ADAM_B1 ** ADAM_STEP)
        v_hat = v2 / (1.0 - ADAM_B2 ** ADAM_STEP)
        delta = -ADAM_LR * (m_hat / (jnp.sqrt(v_hat) + ADAM_EPS) + ADAM_WD * wt)
        return ((g,) if emit_g else ()) + (delta, m2, v2)

    assert n_g >= 1
    return _rowwise(fn, [w, m, v, *gs], [], [(C, F32)] * (4 if emit_g else 3), name=name, tm=_rtile(w.shape[0], tm))


def _rtile(r, cap=512):
    if r <= cap:
        return r
    best = None
    for t in range(8, cap + 1, 8):
        if r % t == 0:
            best = t
    assert best is not None, r
    return best


ANY = pl.BlockSpec(memory_space=pl.ANY)


def _where_am_i():
    x, y, c = lax.axis_index("x"), lax.axis_index("y"), lax.axis_index("c")
    ks = (2 * x + y, 2 * (1 - x) + y, 2 * x + (1 - y), 2 * (1 - x) + (1 - y))
    return x, y, c, ks


def _rc(src, dst, ssem, rsem, to):
    return pltpu.make_async_remote_copy(src_ref=src, dst_ref=dst, send_sem=ssem, recv_sem=rsem,
                                        device_id=to, device_id_type=MESH)


def _gather_shards(srcs):
    n = len(srcs)

    def body(*refs):
        src, out = refs[:n], refs[n:2 * n]
        ssem, rsem, lsem = refs[2 * n:]
        x, y, c, (k_me, k_x, k_y, k_d) = _where_am_i()
        sib = (x, y, 1 - c)
        local = [pltpu.make_async_copy(src[i], out[i].at[k_me], lsem.at[i]) for i in range(n)]
        for cp in local:
            cp.start()

        def own(i, q, to):
            return _rc(src[i], out[i].at[k_me], ssem.at[i, q], rsem.at[i, q], to)

        def slab(i, q, k, to):
            return _rc(out[i].at[k], out[i].at[k], ssem.at[i, q], rsem.at[i, q], to)

        @pl.when(c == 1)
        def _():
            sends = [own(i, 0, (1 - x, y, 1)) for i in range(n)]
            for cp in sends:
                cp.start()
            fwd = [slab(i, 3, k_x, sib) for i in range(n)]
            for i in range(n):
                slab(i, 0, k_x, sib).wait_recv()
                fwd[i].start()
            for i in range(n):
                slab(i, 4, k_y, sib).wait_recv()
                slab(i, 5, k_d, sib).wait_recv()
            for cp in sends + fwd:
                cp.wait_send()

        @pl.when(c == 0)
        def _():
            sends = [own(i, 1, (x, 1 - y, 0)) for i in range(n)] + [own(i, 2, (1 - x, 1 - y, 0)) for i in range(n)]
            for cp in sends:
                cp.start()
            fwd = [slab(i, 4, k_y, sib) for i in range(n)] + [slab(i, 5, k_d, sib) for i in range(n)]
            for i in range(n):
                slab(i, 1, k_y, sib).wait_recv()
                fwd[i].start()
            for i in range(n):
                slab(i, 2, k_d, sib).wait_recv()
                fwd[n + i].start()
            for i in range(n):
                slab(i, 3, k_x, sib).wait_recv()
            for cp in sends + fwd:
                cp.wait_send()

        for cp in local:
            cp.wait()

    return pl.pallas_call(
        body, name="gather_shards",
        in_specs=[ANY] * n, out_specs=[ANY] * n,
        out_shape=[jax.ShapeDtypeStruct((4,) + s.shape, s.dtype) for s in srcs],
        scratch_shapes=[pltpu.SemaphoreType.DMA((n, 6)), pltpu.SemaphoreType.DMA((n, 6)), pltpu.SemaphoreType.DMA((n,))],
        compiler_params=pltpu.CompilerParams(has_side_effects=True),
    )(*srcs)


def _swap_slabs(arrs, n_slabs, name):
    n = len(arrs)
    J = max(n_slabs, 1)

    def body(*refs):
        src, out = refs[:n], refs[n:2 * n]
        ssem, rsem = refs[2 * n:]
        x, y, c, (k_me, k_x, k_y, k_d) = _where_am_i()
        sib = (x, y, 1 - c)
        sel = (jnp.where(c == 1, k_y, k_me), jnp.where(c == 1, k_d, k_x))
        cps = []
        for i in range(n):
            for j in range(J):
                s = src[i].at[sel[j]] if n_slabs else src[i]
                d = out[i].at[j] if n_slabs else out[i]
                cps.append(_rc(s, d, ssem.at[i, j], rsem.at[i, j], sib))
        for cp in cps:
            cp.start()
        for cp in cps:
            cp.wait()

    return pl.pallas_call(
        body, name=name, in_specs=[ANY] * n, out_specs=[ANY] * n,
        out_shape=[jax.ShapeDtypeStruct(((n_slabs,) if n_slabs else ()) + a.shape[-2:], a.dtype) for a in arrs],
        scratch_shapes=[pltpu.SemaphoreType.DMA((n, J)), pltpu.SemaphoreType.DMA((n, J))],
        compiler_params=pltpu.CompilerParams(has_side_effects=True),
    )(*arrs)


def _send_chip_sums(arrs):
    n = len(arrs)

    def body(*refs):
        src, out = refs[:n], refs[n:2 * n]
        ssem, rsem = refs[2 * n:]
        x, y, c, _ = _where_am_i()

        @pl.when(c == 1)
        def _():
            cps = [_rc(src[i].at[1], out[i].at[0], ssem.at[i, 0], rsem.at[i, 0], (1 - x, y, 1)) for i in range(n)]
            for cp in cps:
                cp.start()
            for cp in cps:
                cp.wait()

        @pl.when(c == 0)
        def _():
            cps = [_rc(src[i].at[0], out[i].at[0], ssem.at[i, 0], rsem.at[i, 0], (x, 1 - y, 0)) for i in range(n)]
            cps += [_rc(src[i].at[1], out[i].at[1], ssem.at[i, 1], rsem.at[i, 1], (1 - x, 1 - y, 0)) for i in range(n)]
            for cp in cps:
                cp.start()
            for cp in cps:
                cp.wait()

    return pl.pallas_call(
        body, name="send_chip_sums", in_specs=[ANY] * n, out_specs=[ANY] * n,
        out_shape=[jax.ShapeDtypeStruct(a.shape, a.dtype) for a in arrs],
        scratch_shapes=[pltpu.SemaphoreType.DMA((n, 2)), pltpu.SemaphoreType.DMA((n, 2))],
        compiler_params=pltpu.CompilerParams(has_side_effects=True),
    )(*arrs)


def _chip_sum(g, recv, sel, name):
    _, r, C = g.shape
    tr = _rtile(r)

    def body(sel_ref, g_ref, r_ref, o32_ref, o16_ref):
        s = g_ref[...] + r_ref[...]
        o32_ref[...] = s
        o16_ref[...] = s.astype(BF16)

    blk = pl.BlockSpec((None, tr, C), lambda j, t, sel_ref: (j, t, 0))
    return pl.pallas_call(
        body, name=name,
        grid_spec=pltpu.PrefetchScalarGridSpec(
            num_scalar_prefetch=1, grid=(2, r // tr),
            in_specs=[pl.BlockSpec((None, tr, C), lambda j, t, sel_ref: (sel_ref[j], t, 0)), blk],
            out_specs=[blk, blk]),
        out_shape=[jax.ShapeDtypeStruct((2, r, C), F32), jax.ShapeDtypeStruct((2, r, C), BF16)],
        compiler_params=_params(("arbitrary", "arbitrary")),
    )(sel, g, recv)


def _cross_sum(s32, got, north, name):
    _, r, C = s32.shape
    tr = _rtile(r)

    def body(f_ref, p_ref, a_ref, b_ref, o_ref):
        a = a_ref[...].astype(F32)

        @pl.when(f_ref[0] == 1)
        def _():
            o_ref[...] = p_ref[...] + a

        @pl.when(f_ref[0] == 0)
        def _():
            o_ref[...] = a + b_ref[...].astype(F32)

    return pl.pallas_call(
        body, name=name,
        grid_spec=pltpu.PrefetchScalarGridSpec(
            num_scalar_prefetch=1, grid=(r // tr,),
            in_specs=[pl.BlockSpec((None, tr, C), lambda t, f: (0, t, 0)),
                      pl.BlockSpec((None, tr, C), lambda t, f: (0, t, 0)),
                      pl.BlockSpec((None, tr, C), lambda t, f: (1 - f[0], t, 0))],
            out_specs=pl.BlockSpec((tr, C), lambda t, f: (t, 0))),
        out_shape=jax.ShapeDtypeStruct((r, C), F32),
        compiler_params=_params(("arbitrary",)),
    )(north, s32, got, got)


def _reduce_scatter(grads):
    x, y, c, (k_me, k_x, k_y, k_d) = _where_am_i()
    sel = jnp.stack([jnp.where(c == 1, k_me, k_y), jnp.where(c == 1, k_x, k_d)]).astype(jnp.int32)
    north = jnp.reshape(c, (1,)).astype(jnp.int32)
    from_sib = _swap_slabs(grads, 2, "swap_grad_slabs")
    sums = [_chip_sum(g, r, sel, f"chip_sum{i}") for i, (g, r) in enumerate(zip(grads, from_sib))]
    got = _send_chip_sums([s16 for _, s16 in sums])
    parts = [_cross_sum(s32, b, north, f"cross_sum{i}") for i, ((s32, _), b) in enumerate(zip(sums, got))]
    theirs = _swap_slabs(parts, 0, "swap_parts")
    return list(zip(parts, theirs))


def _all_reduce_small(v):
    R, C = v.shape

    def body(v_ref, o_ref, buf, ssem, rsem, lsem):
        x, y, c = lax.axis_index("x"), lax.axis_index("y"), lax.axis_index("c")
        me, sib = (x, y, c), (x, y, 1 - c)
        chips = [(1 - x, y), (x, 1 - y), (1 - x, 1 - y)]

        def slot(px, py, pc):
            return buf.at[4 * px + 2 * py + pc]

        def copy(k, block, to, src=None):
            return _rc(slot(*block) if src is None else src, slot(*block), ssem.at[k], rsem.at[k], to)

        mine = pltpu.make_async_copy(v_ref, slot(*me), lsem)
        mine.start()
        first = [copy(0, me, sib, src=v_ref)] + [copy(1 + j, me, (*chip, c), src=v_ref) for j, chip in enumerate(chips)]
        for cp in first:
            cp.start()
        passed = [copy(4 + j, (*chip, c), sib) for j, chip in enumerate(chips)]
        for j, chip in enumerate(chips):
            copy(1 + j, (*chip, c), me).wait_recv()
            passed[j].start()
        copy(0, sib, me).wait_recv()
        for j, chip in enumerate(chips):
            copy(4 + j, (*chip, 1 - c), me).wait_recv()
        for cp in first + passed:
            cp.wait_send()
        mine.wait()
        acc = buf[0]
        for k in range(1, 8):
            acc = acc + buf[k]
        o_ref[...] = acc

    return pl.pallas_call(
        body, name="all_reduce_small",
        in_specs=[pl.BlockSpec(memory_space=pltpu.VMEM)], out_specs=pl.BlockSpec(memory_space=pltpu.VMEM),
        out_shape=jax.ShapeDtypeStruct((R, C), F32),
        scratch_shapes=[pltpu.VMEM((8, R, C), F32), pltpu.SemaphoreType.DMA((7,)), pltpu.SemaphoreType.DMA((7,)),
                        pltpu.SemaphoreType.DMA],
        compiler_params=pltpu.CompilerParams(has_side_effects=True, vmem_limit_bytes=VMEM_LIMIT),
    )(v)


WEIGHTS = ['ffn1_norm', 'ffn1_w_gate', 'ffn1_w_up', 'ffn1_w_down', 'mix_norm', 'w_in', 'conv_w', 'conv_b', 's5_A_re',
           's5_A_im', 's5_log_dt', 's5_B_re', 's5_B_im', 's5_C_re', 's5_C_im', 's5_D', 's5_w_glu', 's5_b_glu',
           'ssd_A_log', 'ssd_dt_bias', 'ssd_D', 'ssd_norm', 'w_proj_s5', 'w_proj_ssd', 'b_gate', 'w_out', 'ffn2_norm',
           'ffn2_w_gate', 'ffn2_w_up', 'ffn2_w_down', 'final_norm']
ARGS = ['x'] + WEIGHTS + ['loss_target'] + ['m_' + n for n in WEIGHTS] + ['v_' + n for n in WEIGHTS]
COL_SHARDED = ('ffn1_w_gate', 'ffn1_w_up', 'w_in', 'ffn2_w_gate', 'ffn2_w_up')
ROW_SHARDED = ('ffn1_w_down', 's5_w_glu', 'w_proj_s5', 'w_proj_ssd', 'w_out', 'ffn2_w_down')
MATRICES = COL_SHARDED + ROW_SHARDED
SMALL = [n for n in WEIGHTS if n not in MATRICES]


def _pack(arrs, width=1024):
    flat = jnp.concatenate([a.reshape(-1).astype(F32) for a in arrs])
    rows = -(-flat.shape[0] // (8 * width)) * 8
    return jnp.pad(flat, (0, rows * width - flat.shape[0])).reshape(rows, width)


def _unpack(packed, shapes):
    flat, out, o = packed.reshape(-1), [], 0
    for s in shapes:
        n = math.prod(s)
        out.append(flat[o:o + n].reshape(s))
        o += n
    return out


def kernel(x, ffn1_norm, ffn1_w_gate, ffn1_w_up, ffn1_w_down, mix_norm, w_in, conv_w, conv_b, s5_A_re, s5_A_im, s5_log_dt, s5_B_re, s5_B_im, s5_C_re, s5_C_im, s5_D, s5_w_glu, s5_b_glu, ssd_A_log, ssd_dt_bias, ssd_D, ssd_norm, w_proj_s5, w_proj_ssd, b_gate, w_out, ffn2_norm, ffn2_w_gate, ffn2_w_up, ffn2_w_down, final_norm, loss_target, m_ffn1_norm, m_ffn1_w_gate, m_ffn1_w_up, m_ffn1_w_down, m_mix_norm, m_w_in, m_conv_w, m_conv_b, m_s5_A_re, m_s5_A_im, m_s5_log_dt, m_s5_B_re, m_s5_B_im, m_s5_C_re, m_s5_C_im, m_s5_D, m_s5_w_glu, m_s5_b_glu, m_ssd_A_log, m_ssd_dt_bias, m_ssd_D, m_ssd_norm, m_w_proj_s5, m_w_proj_ssd, m_b_gate, m_w_out, m_ffn2_norm, m_ffn2_w_gate, m_ffn2_w_up, m_ffn2_w_down, m_final_norm, v_ffn1_norm, v_ffn1_w_gate, v_ffn1_w_up, v_ffn1_w_down, v_mix_norm, v_w_in, v_conv_w, v_conv_b, v_s5_A_re, v_s5_A_im, v_s5_log_dt, v_s5_B_re, v_s5_B_im, v_s5_C_re, v_s5_C_im, v_s5_D, v_s5_w_glu, v_s5_b_glu, v_ssd_A_log, v_ssd_dt_bias, v_ssd_D, v_ssd_norm, v_w_proj_s5, v_w_proj_ssd, v_b_gate, v_w_out, v_ffn2_norm, v_ffn2_w_gate, v_ffn2_w_up, v_ffn2_w_down, v_final_norm):
    a = dict(locals())
    assert list(a) == ARGS
    x, target = a['x'][0], a['loss_target'][0]
    k_me = 2 * lax.axis_index("x") + lax.axis_index("y")

    srcs = [a[n][0].T.astype(BF16) for n in COL_SHARDED] + [a[n][0].astype(BF16) for n in ROW_SHARDED] + [a['conv_w'][0]]
    full = _gather_shards(srcs)
    w = {n: f.reshape(-1, f.shape[-1]) for n, f in zip(MATRICES, full)}
    w['conv_w'] = full[-1].transpose(1, 0, 2).reshape(CONV_K, CONV_DIM)
    p = {n: a[n] for n in SMALL if n != 'conv_w'}
    p['final_norm'] = a['final_norm'][None]

    loss, grad_x, gp, gw = _local_step(x, target, p, w)

    pairs = _reduce_scatter([gw[n].reshape(4, -1, D_MODEL) for n in MATRICES])
    gp['conv_w'] = gw['conv_w'][None]
    small_shapes = [(1, 1)] + [gp[n].shape if n != 'final_norm' else (1, D_MODEL) for n in SMALL]
    red = _unpack(_all_reduce_small(_pack([loss] + [gp[n] for n in SMALL])), small_shapes)
    loss_all = red[0].reshape(())
    gsmall = dict(zip(SMALL, red[1:]))
    gsmall['conv_w'] = lax.dynamic_slice_in_dim(gsmall['conv_w'], k_me * 1024, 1024, axis=2)
    gsmall = {n: g.reshape(a[n].shape) for n, g in gsmall.items()}

    grads, delta, new_m, new_v = {}, {}, {}, {}
    for n, (mine, theirs) in zip(MATRICES, pairs):
        wn, mn, vn = a[n][0], a['m_' + n][0], a['v_' + n][0]
        if n in COL_SHARDED:
            (gt,) = _rowwise(lambda p_, q_: (p_ + q_,), [mine, theirs], [], [(D_MODEL, F32)], name=f"sum_{n}",
                             tm=_rtile(mine.shape[0]))
            g = gt.T
            d, m2, v2 = _adamw(wn, mn, vn, [g], name=f"adamw_{n}", emit_g=False)
        else:
            g, d, m2, v2 = _adamw(wn, mn, vn, [mine, theirs], name=f"adamw_{n}", emit_g=True)
        grads[n], delta[n], new_m[n], new_v[n] = g[None], d[None], m2[None], v2[None]
    sw, sm, sv, sg = [_pack([t[n] for n in SMALL]) for t in (a, {n: a['m_' + n] for n in SMALL},
                                                             {n: a['v_' + n] for n in SMALL}, gsmall)]
    d, m2, v2 = _adamw(sw, sm, sv, [sg], name="adamw_small", emit_g=False)
    shapes = [a[n].shape for n in SMALL]
    for n, dd, mm, vv in zip(SMALL, _unpack(d, shapes), _unpack(m2, shapes), _unpack(v2, shapes)):
        grads[n], delta[n], new_m[n], new_v[n] = gsmall[n], dd, mm, vv
    return (loss_all, grad_x[None], *[grads[n] for n in WEIGHTS], *[delta[n] for n in WEIGHTS],
            *[new_m[n] for n in WEIGHTS], *[new_v[n] for n in WEIGHTS])
```

```python
import functools
import math

import jax
import jax.numpy as jnp
from jax import lax
from jax.experimental import pallas as pl
from jax.experimental.pallas import tpu as pltpu

F32 = jnp.float32
BF16 = jnp.bfloat16
MXU = BF16
HI = lax.Precision.HIGHEST

D_MODEL = 1024
D_FF = 2816
EPS = 1e-6
S5_GROUPS, S5_GROUP, S5_STATE = 64, 16, 64
D_INNER = 2048
SSD_HEADDIM, SSD_HEADS, SSD_GROUPS, SSD_HPG, SSD_STATE = 64, 32, 8, 4, 128
CONV_K, CONV_DIM = 4, 4096
IN_SPLITS = (1024, 2048, 4096, 32, 2048)
IN_OFFS = (0, 1024, 3072, 7168, 7200, 9248)
SSD_Q = 256
ADAM_LR, ADAM_B1, ADAM_B2, ADAM_EPS, ADAM_WD, ADAM_STEP = 0.001, 0.9, 0.999, 1e-08, 0.01, 10

VMEM_LIMIT = 56 * 1024 * 1024
MATMUL_VMEM = 32 * 1024 * 1024
MESH = pl.DeviceIdType.MESH

NN = ((1,), (0,))
NT = ((1,), (1,))
TN = ((0,), (0,))


def _dot(a, b, dims, precision=None):
    if precision is None:
        a, b = a.astype(MXU), b.astype(MXU)
    return lax.dot_general(a, b, (dims, ((), ())), precision=precision, preferred_element_type=F32)


def _tile(n, pref):
    if n <= pref:
        return n
    best = None
    for t in range(128, pref + 1, 128):
        if n % t == 0:
            best = t
    assert best is not None, (n, pref)
    return best


def _params(sem):
    return pltpu.CompilerParams(dimension_semantics=sem, vmem_limit_bytes=VMEM_LIMIT)


def _matmul(a, b, mode, *, name, out=((F32),), epilogue=None, extras=(), tm=512, tn=1536):
    if mode == "nn":
        (M, K), (_, N) = a.shape, b.shape
    elif mode == "nt":
        (M, K), (N, _) = a.shape, b.shape
    else:
        (K, M), (_, N) = a.shape, b.shape
    tm, tn = _tile(M, tm), _tile(N, tn)

    def vmem(tm_, tn_):
        per = K * tm_ * a.dtype.itemsize + K * tn_ * b.dtype.itemsize
        per += sum(tm_ * tn_ * jnp.dtype(dt).itemsize for dt in out)
        per += sum((1 if e.shape[0] == 1 else tm_) * tn_ * e.dtype.itemsize for e in extras)
        return 2 * per

    while vmem(tm, tn) > MATMUL_VMEM and (tn > 128 or tm > 128):
        if tn >= tm and tn > 128:
            tn = _tile(N, tn - 128)
        else:
            tm = _tile(M, tm - 128)
    a_spec = pl.BlockSpec((K, tm), lambda i, j: (0, i)) if mode == "tn" else pl.BlockSpec((tm, K), lambda i, j: (i, 0))
    b_spec = pl.BlockSpec((tn, K), lambda i, j: (j, 0)) if mode == "nt" else pl.BlockSpec((K, tn), lambda i, j: (0, j))
    dims = {"nn": NN, "nt": NT, "tn": TN}[mode]
    e_specs = []
    for e in extras:
        if e.shape[0] == 1:
            e_specs.append(pl.BlockSpec((1, tn), lambda i, j: (0, j)))
        else:
            assert e.shape == (M, N), (e.shape, M, N)
            e_specs.append(pl.BlockSpec((tm, tn), lambda i, j: (i, j)))
    n_e, n_o = len(extras), len(out)

    def body(a_ref, b_ref, *refs):
        acc = _dot(a_ref[...], b_ref[...], dims)
        res = (acc,) if epilogue is None else epilogue(acc, *[r[...] for r in refs[:n_e]])
        for r, v in zip(refs[n_e:], res):
            r[...] = v.astype(r.dtype)

    res = pl.pallas_call(
        body, name=name, grid=(M // tm, N // tn),
        in_specs=[a_spec, b_spec] + e_specs,
        out_specs=[pl.BlockSpec((tm, tn), lambda i, j: (i, j)) for _ in range(n_o)],
        out_shape=[jax.ShapeDtypeStruct((M, N), dt) for dt in out],
        compiler_params=_params(("parallel", "parallel")),
    )(a, b, *extras)
    return res[0] if n_o == 1 else res


def _rowwise(fn, rows, fulls, outs, reds=(), *, name, tm=256):
    T = rows[0].shape[0]
    tm = min(tm, T)
    n_r, n_f, n_o, n_d = len(rows), len(fulls), len(outs), len(reds)

    def body(*refs):
        ins = [r[...] for r in refs[:n_r + n_f]]
        o_refs = refs[n_r + n_f:n_r + n_f + n_o]
        d_refs = refs[n_r + n_f + n_o:]
        res = fn(*ins)
        for r, v in zip(o_refs, res[:n_o]):
            r[...] = v.astype(r.dtype)
        if n_d:
            @pl.when(pl.program_id(0) == 0)
            def _():
                for r in d_refs:
                    r[...] = jnp.zeros_like(r)
            for r, v in zip(d_refs, res[n_o:]):
                r[...] += v.astype(r.dtype)

    res = pl.pallas_call(
        body, name=name, grid=(T // tm,),
        in_specs=[pl.BlockSpec((tm, r.shape[1]), lambda i: (i, 0)) for r in rows]
        + [pl.BlockSpec(f.shape, lambda i, nd=f.ndim: (0,) * nd) for f in fulls],
        out_specs=[pl.BlockSpec((tm, c), lambda i: (i, 0)) for c, _ in outs]
        + [pl.BlockSpec(s, lambda i, nd=len(s): (0,) * nd) for s, _ in reds],
        out_shape=[jax.ShapeDtypeStruct((T, c), dt) for c, dt in outs]
        + [jax.ShapeDtypeStruct(s, dt) for s, dt in reds],
        compiler_params=_params(("arbitrary",)),
    )(*rows, *fulls)
    return res


def _rms(x, g):
    return x * lax.rsqrt(jnp.mean(x * x, axis=-1, keepdims=True) + EPS) * g


def _colsum(v):
    return jnp.sum(v, axis=0, keepdims=True)


def _softplus(x):
    return jnp.maximum(x, 0.0) + jnp.log1p(jnp.exp(-jnp.abs(x)))


def _ffn_fwd(x, norm, wgT, wuT, wd, tag):
    D = x.shape[1]
    (hn,) = _rowwise(lambda xt, g: (_rms(xt, g),), [x], [norm], [(D, BF16)], name=f"{tag}_norm")
    a = _matmul(hn, wgT, "nt", name=f"{tag}_gate")
    b = _matmul(hn, wuT, "nt", name=f"{tag}_up")
    (hid,) = _rowwise(lambda at, bt: (jax.nn.silu(at) * bt,), [a, b], [], [(a.shape[1], BF16)], name=f"{tag}_act")
    y = _matmul(hid, wd, "nn", name=f"{tag}_down", epilogue=lambda acc, xt: (xt + 0.5 * acc,), extras=[x])
    return y, (hn, a, b, hid)


def _ffn_bwd(dy, x, norm, wgT, wuT, wd, res, tag):
    hn, a, b, hid = res
    D, F = x.shape[1], a.shape[1]
    dhid = _matmul(dy, wd, "nt", name=f"{tag}_d_hid", epilogue=lambda acc: (0.5 * acc,))
    d_wd = _matmul(hid, dy, "tn", name=f"{tag}_d_wd", epilogue=lambda acc: (0.5 * acc,))

    def act_bwd(at, bt, dh):
        _, vjp = jax.vjp(lambda p, q: jax.nn.silu(p) * q, at, bt)
        return vjp(dh)

    da, db = _rowwise(act_bwd, [a, b, dhid], [], [(F, BF16), (F, BF16)], name=f"{tag}_d_act")
    d_wgT = _matmul(da, hn, "tn", name=f"{tag}_d_wg")
    d_wuT = _matmul(db, hn, "tn", name=f"{tag}_d_wu")
    dhn = _matmul(da, wgT, "nn", name=f"{tag}_d_hn1")
    dhn = _matmul(db, wuT, "nn", name=f"{tag}_d_hn2", epilogue=lambda acc, e: (acc + e,), extras=[dhn])

    def norm_bwd(xt, dh, dyt, g):
        _, vjp = jax.vjp(_rms, xt, g)
        dx, dg = vjp(dh)
        return dyt + dx, dg

    dx, d_norm = _rowwise(norm_bwd, [x, dhn, dy], [norm], [(D, F32)], [((1, D), F32)], name=f"{tag}_d_norm")
    return dx, d_norm, d_wgT, d_wuT, d_wd


S5_TILES = 8
S5_HALF = 256


def _s5_derive(A_re, A_im, log_dt, B_re, B_im, C_re, C_im):
    G, N, M = S5_GROUPS, S5_STATE, S5_GROUP
    dt = jnp.exp(log_dt)[:, None]
    mag = jnp.exp(A_re * dt)
    ar, ai = mag * jnp.cos(A_im * dt), mag * jnp.sin(A_im * dt)
    den = A_re * A_re + A_im * A_im
    cr = ((ar - 1.0) * A_re + ai * A_im) / den
    ci = (ai * A_re - (ar - 1.0) * A_im) / den
    bbr = cr[..., None] * B_re - ci[..., None] * B_im
    bbi = cr[..., None] * B_im + ci[..., None] * B_re
    eye = jnp.eye(8, dtype=F32)

    def tile_in(bb):
        t = bb.reshape(S5_TILES, 8, N, M).transpose(0, 1, 3, 2)
        return jnp.einsum("jamn,ab->jambn", t, eye).reshape(S5_TILES, 8 * M, 8 * N)

    def tile_out(c):
        t = c.reshape(S5_TILES, 8, M, N).transpose(0, 1, 3, 2)
        return jnp.einsum("janm,ab->janbm", t, eye).reshape(S5_TILES, 8 * N, 8 * M)

    return (tile_in(bbr), tile_in(bbi), tile_out(C_re), tile_out(C_im),
            ar.reshape(S5_TILES, 1, 8 * N), ai.reshape(S5_TILES, 1, 8 * N))


S5_NB = S5_HALF // 128
S5_SEG = 32


def _cmul(ar, ai, br, bi):
    return ar * br - ai * bi, ar * bi + ai * br


def _s5_scan(sr_ref, si_ref, ar, ai, T, reverse):
    L, V = T // S5_SEG, S5_SEG // 8
    assert L * S5_SEG == T and L & (L - 1) == 0, T
    sg = -1.0 if reverse else 1.0
    a_r = [jnp.broadcast_to(ar[:, 128 * b:128 * (b + 1)], (8, 128)) for b in range(S5_NB)]
    a_i = [jnp.broadcast_to(sg * ai[:, 128 * b:128 * (b + 1)], (8, 128)) for b in range(S5_NB)]

    def rows(k, v):
        return pl.ds(pl.multiple_of(((L - 1 - k) if reverse else k) * S5_SEG + 8 * v, 8), 8)

    def local(k, carry):
        out = []
        for b in range(S5_NB):
            for v in range(V):
                idx = rows(k, v)
                mr, mi = _cmul(a_r[b], a_i[b], *carry[b * V + v])
                nr, ni = mr + sr_ref[b, idx, :], mi + si_ref[b, idx, :]
                sr_ref[b, idx, :] = nr
                si_ref[b, idx, :] = ni
                out.append((nr, ni))
        return tuple(out)

    z = jnp.zeros((8, 128), F32)
    ends = lax.fori_loop(0, L, local, tuple((z, z) for _ in range(S5_NB * V)))

    carries = []
    for b in range(S5_NB):
        pr, pi = a_r[b][0:1], a_i[b][0:1]
        n = L
        while n > 1:
            pr, pi = _cmul(pr, pi, pr, pi)
            n //= 2
        cr, ci = jnp.zeros((1, 128), F32), jnp.zeros((1, 128), F32)
        into = [None] * S5_SEG
        for j in (reversed(range(S5_SEG)) if reverse else range(S5_SEG)):
            into[j] = (cr, ci)
            er, ei = ends[b * V + j // 8]
            mr, mi = _cmul(pr, pi, cr, ci)
            cr, ci = mr + er[j % 8:j % 8 + 1], mi + ei[j % 8:j % 8 + 1]
        carries.append([(jnp.concatenate([into[8 * v + s][0] for s in range(8)], axis=0),
                         jnp.concatenate([into[8 * v + s][1] for s in range(8)], axis=0)) for v in range(V)])

    def fix(k, powers):
        out = []
        for b in range(S5_NB):
            pr, pi = powers[b]
            for v in range(V):
                idx = rows(k, v)
                dr, di = _cmul(pr, pi, *carries[b][v])
                sr_ref[b, idx, :] += dr
                si_ref[b, idx, :] += di
            out.append(_cmul(pr, pi, a_r[b], a_i[b]))
        return tuple(out)

    lax.fori_loop(0, L, fix, tuple((a_r[b], a_i[b]) for b in range(S5_NB)))


def _s5_in(ut, b_ref, s_ref):
    for b in range(S5_NB):
        s_ref[b] = _dot(ut, b_ref[:, 128 * b:128 * (b + 1)], NN)


def _to_seg(v):
    T, C = v.shape
    return v.reshape(S5_SEG, T // S5_SEG, C).transpose(1, 0, 2).reshape(T, C)


def _from_seg(v):
    T, C = v.shape
    return v.reshape(T // S5_SEG, S5_SEG, C).transpose(1, 0, 2).reshape(T, C)


def _s5_specs(T):
    u_spec = pl.BlockSpec((T, 128), lambda j, h: (0, j))
    b_spec = pl.BlockSpec((None, 128, S5_HALF), lambda j, h: (j, 0, h))
    c_spec = pl.BlockSpec((None, S5_HALF, 128), lambda j, h: (j, h, 0))
    a_spec = pl.BlockSpec((None, 1, S5_HALF), lambda j, h: (j, 0, h))
    return u_spec, b_spec, c_spec, a_spec


def _s5_fwd(u, tiles):
    T = u.shape[0]
    u_spec, b_spec, c_spec, a_spec = _s5_specs(T)

    def body(u_ref, br_ref, bi_ref, cr_ref, ci_ref, ar_ref, ai_ref, y_ref, sr_ref, si_ref):
        ut = u_ref[...].astype(MXU)
        _s5_in(ut, br_ref, sr_ref)
        _s5_in(ut, bi_ref, si_ref)
        _s5_scan(sr_ref, si_ref, ar_ref[...], ai_ref[...], T, False)
        y = None
        for b in range(S5_NB):
            blk = slice(128 * b, 128 * (b + 1))
            yb = _dot(sr_ref[b], cr_ref[blk, :], NN) - _dot(si_ref[b], ci_ref[blk, :], NN)
            y = yb if y is None else y + yb

        @pl.when(pl.program_id(1) == 0)
        def _():
            y_ref[...] = y

        @pl.when(pl.program_id(1) == 1)
        def _():
            y_ref[...] += y

    scr = pltpu.VMEM((S5_NB, T, 128), F32)
    return pl.pallas_call(
        body, name="s5_fwd", grid=(S5_TILES, 2),
        in_specs=[u_spec, b_spec, b_spec, c_spec, c_spec, a_spec, a_spec],
        out_specs=u_spec, out_shape=jax.ShapeDtypeStruct(u.shape, F32),
        scratch_shapes=[scr, scr],
        compiler_params=_params(("parallel", "arbitrary")),
    )(u, *tiles)


def _s5_bwd(u, dys, du_skip, tiles):
    T = u.shape[0]
    u_spec, b_spec, c_spec, a_spec = _s5_specs(T)

    def body(u_ref, dy_ref, sk_ref, br_ref, bi_ref, cr_ref, ci_ref, ar_ref, ai_ref,
             du_ref, dbr_ref, dbi_ref, dcr_ref, dci_ref, dar_ref, dai_ref, sr_ref, si_ref, lr_ref, li_ref):
        ut, dy = u_ref[...].astype(MXU), dy_ref[...].astype(MXU)
        ar, ai = ar_ref[...], ai_ref[...]
        _s5_in(ut, br_ref, sr_ref)
        _s5_in(ut, bi_ref, si_ref)
        _s5_scan(sr_ref, si_ref, ar, ai, T, False)
        for b in range(S5_NB):
            blk = slice(128 * b, 128 * (b + 1))
            dcr_ref[blk, :] = _dot(sr_ref[b], dy, TN)
            dci_ref[blk, :] = -_dot(si_ref[b], dy, TN)
            lr_ref[b] = _dot(dy, cr_ref[blk, :], NT)
            li_ref[b] = -_dot(dy, ci_ref[blk, :], NT)
        _s5_scan(lr_ref, li_ref, ar, ai, T, True)
        du = None
        head = lax.broadcasted_iota(jnp.int32, (T, 128), 0) < S5_SEG
        seg0 = lax.broadcasted_iota(jnp.int32, (S5_SEG, 128), 0) == 0
        for b in range(S5_NB):
            blk = slice(128 * b, 128 * (b + 1))
            lr, li = lr_ref[b], li_ref[b]
            pr = jnp.where(head, 0.0, pltpu.roll(sr_ref[b], S5_SEG, 0))
            pi = jnp.where(head, 0.0, pltpu.roll(si_ref[b], S5_SEG, 0))
            er = jnp.where(seg0, 0.0, pltpu.roll(sr_ref[b, T - S5_SEG:T, :], 1, 0))
            ei = jnp.where(seg0, 0.0, pltpu.roll(si_ref[b, T - S5_SEG:T, :], 1, 0))
            hr, hi = lr[0:S5_SEG], li[0:S5_SEG]
            dar_ref[:, blk] = _colsum(lr * pr + li * pi) + _colsum(hr * er + hi * ei)
            dai_ref[:, blk] = _colsum(li * pr - lr * pi) + _colsum(hi * er - hr * ei)
            dbr_ref[:, blk] = _dot(ut, lr, TN)
            dbi_ref[:, blk] = _dot(ut, li, TN)
            dub = _dot(lr, br_ref[:, blk], NT) + _dot(li, bi_ref[:, blk], NT)
            du = dub if du is None else du + dub

        @pl.when(pl.program_id(1) == 0)
        def _():
            du_ref[...] = du + sk_ref[...]

        @pl.when(pl.program_id(1) == 1)
        def _():
            du_ref[...] += du

    scr = pltpu.VMEM((S5_NB, T, 128), F32)
    return pl.pallas_call(
        body, name="s5_bwd", grid=(S5_TILES, 2),
        in_specs=[u_spec, u_spec, u_spec, b_spec, b_spec, c_spec, c_spec, a_spec, a_spec],
        out_specs=[u_spec, b_spec, b_spec, c_spec, c_spec, a_spec, a_spec],
        out_shape=[jax.ShapeDtypeStruct(u.shape, F32)] + [jax.ShapeDtypeStruct(t.shape, F32) for t in tiles],
        scratch_shapes=[scr, scr, scr, scr],
        compiler_params=_params(("parallel", "arbitrary")),
    )(u, dys, du_skip, *tiles)


CONV_TILE = 256


def _conv_pre(x, w, b):
    T = x.shape[0]
    row = lax.broadcasted_iota(jnp.int32, x.shape, 0)
    acc = x * w[CONV_K - 1:CONV_K, :] + b
    for lag in range(1, CONV_K):
        acc = acc + jnp.where(row >= lag, pltpu.roll(x, lag, 0), 0.0) * w[CONV_K - 1 - lag:CONV_K - lag, :]
    return acc


def _conv_fwd(x, w, b):
    T, C = x.shape
    col = pl.BlockSpec((T, CONV_TILE), lambda j: (0, j))

    def body(x_ref, w_ref, b_ref, o_ref):
        o_ref[...] = jax.nn.silu(_conv_pre(x_ref[...], w_ref[...], b_ref[...]))

    return pl.pallas_call(
        body, name="conv_fwd", grid=(C // CONV_TILE,),
        in_specs=[col, pl.BlockSpec((CONV_K, CONV_TILE), lambda j: (0, j)), pl.BlockSpec((1, CONV_TILE), lambda j: (0, j))],
        out_specs=col, out_shape=jax.ShapeDtypeStruct((T, C), F32),
        compiler_params=_params(("parallel",)),
    )(x, w, b)


def _conv_bwd(x, w, b, dout, col0, name):
    T, C = dout.shape
    off = col0 // CONV_TILE
    xcol = pl.BlockSpec((T, CONV_TILE), lambda j: (0, j + off))
    dcol = pl.BlockSpec((T, CONV_TILE), lambda j: (0, j))

    def body(x_ref, w_ref, b_ref, d_ref, dx_ref, dw_ref, db_ref):
        x, w = x_ref[...], w_ref[...]
        _, vjp = jax.vjp(jax.nn.silu, _conv_pre(x, w, b_ref[...]))
        (dy,) = vjp(d_ref[...])
        row = lax.broadcasted_iota(jnp.int32, x.shape, 0)
        dx = dy * w[CONV_K - 1:CONV_K, :]
        dw_ref[CONV_K - 1:CONV_K, :] = _colsum(dy * x)
        for lag in range(1, CONV_K):
            dx = dx + jnp.where(row < T - lag, pltpu.roll(dy, T - lag, 0), 0.0) * w[CONV_K - 1 - lag:CONV_K - lag, :]
            xs = jnp.where(row >= lag, pltpu.roll(x, lag, 0), 0.0)
            dw_ref[CONV_K - 1 - lag:CONV_K - lag, :] = _colsum(dy * xs)
        dx_ref[...] = dx.astype(dx_ref.dtype)
        db_ref[...] = _colsum(dy)

    return pl.pallas_call(
        body, name=name, grid=(C // CONV_TILE,),
        in_specs=[xcol, pl.BlockSpec((CONV_K, CONV_TILE), lambda j: (0, j + off)),
                  pl.BlockSpec((1, CONV_TILE), lambda j: (0, j + off)), dcol],
        out_specs=[dcol, pl.BlockSpec((CONV_K, CONV_TILE), lambda j: (0, j)), pl.BlockSpec((1, CONV_TILE), lambda j: (0, j))],
        out_shape=[jax.ShapeDtypeStruct((T, C), BF16), jax.ShapeDtypeStruct((CONV_K, C), F32), jax.ShapeDtypeStruct((1, C), F32)],
        compiler_params=_params(("parallel",)),
    )(x, w, b, dout)


def _ssd_common(x_ref, b_ref, c_ref, dtr_ref, bias_ref, alog_ref, tri_ref, triu_ref):
    Q = x_ref.shape[0]
    x, Bm, Cm = x_ref[...], b_ref[...], c_ref[...]
    pre = dtr_ref[...] + bias_ref[...]
    dt = _softplus(pre)
    A = -jnp.exp(alog_ref[...])
    adt = dt * A
    ac4 = _dot(tri_ref[...], adt, NN, HI)
    ar4 = _dot(adt, triu_ref[...], TN, HI)
    cb = _dot(Cm, Bm, NT)
    ii = lax.broadcasted_iota(jnp.int32, (Q, Q), 0)
    jj = lax.broadcasted_iota(jnp.int32, (Q, Q), 1)
    atot4 = ac4[Q - 1:Q, :]
    return x, Bm, Cm, pre, dt, A, ac4, ar4, cb, ii >= jj, atot4


def _ssd_decay(k, ac4, ar4, causal):
    seg = ac4[:, k:k + 1] - ar4[k:k + 1, :]
    return jnp.where(causal, jnp.exp(jnp.where(causal, seg, 0.0)), 0.0)


def _per_head(c4, n):
    lane = lax.broadcasted_iota(jnp.int32, (c4.shape[0], n), 1)
    out = jnp.broadcast_to(c4[:, 0:1], (c4.shape[0], n))
    for k in range(1, SSD_HPG):
        out = jnp.where(lane >= SSD_HEADDIM * k, c4[:, k:k + 1], out)
    return out


def _head_sums(v):
    lane = lax.broadcasted_iota(jnp.int32, (v.shape[0], SSD_HPG), 1)
    out = jnp.zeros((v.shape[0], SSD_HPG), F32)
    for k in range(SSD_HPG):
        s = jnp.sum(v[:, SSD_HEADDIM * k:SSD_HEADDIM * (k + 1)], axis=1, keepdims=True)
        out = jnp.where(lane == k, s, out)
    return out


def _head_rows(c4):
    n = SSD_HPG * SSD_HEADDIM
    row = lax.broadcasted_iota(jnp.int32, (n, 1), 0)
    out = jnp.broadcast_to(c4[:, 0:1], (n, 1))
    for k in range(1, SSD_HPG):
        out = jnp.where(row >= SSD_HEADDIM * k, c4[:, k:k + 1], out)
    return out


def _ssd_specs(T, Q, rev):
    NC = T // Q
    cc = (lambda c: NC - 1 - c) if rev else (lambda c: c)
    x_spec = pl.BlockSpec((Q, SSD_HPG * SSD_HEADDIM), lambda g, c: (cc(c), g))
    b_spec = pl.BlockSpec((Q, SSD_STATE), lambda g, c: (cc(c), D_INNER // SSD_STATE + g))
    c_spec = pl.BlockSpec((Q, SSD_STATE), lambda g, c: (cc(c), D_INNER // SSD_STATE + SSD_GROUPS + g))
    dt_spec = pl.BlockSpec((None, Q, SSD_HPG), lambda g, c: (g, cc(c), 0))
    p_spec = pl.BlockSpec((None, 1, SSD_HPG), lambda g, c: (g, 0, 0))
    tri_spec = pl.BlockSpec((Q, Q), lambda g, c: (0, 0))
    h_spec = pl.BlockSpec((None, None, SSD_HPG * SSD_HEADDIM, SSD_STATE), lambda g, c: (g, cc(c), 0, 0))
    return x_spec, b_spec, c_spec, dt_spec, p_spec, tri_spec, h_spec


def _tri(Q):
    tri = jnp.tril(jnp.ones((Q, Q), F32))
    return tri, tri.T


def _ssd_fwd(xc, dtr8, bias8, alog8, dsk8):
    T, Q = xc.shape[0], SSD_Q
    NC = T // Q
    P, W = SSD_HEADDIM, SSD_HPG * SSD_HEADDIM
    x_spec, b_spec, c_spec, dt_spec, p_spec, tri_spec, h_spec = _ssd_specs(T, Q, False)

    def body(x_ref, b_ref, c_ref, dtr_ref, bias_ref, alog_ref, dsk_ref, tri_ref, triu_ref, y_ref, hs_ref, h_scr):
        @pl.when(pl.program_id(1) == 0)
        def _():
            h_scr[...] = jnp.zeros_like(h_scr)

        hin = h_scr[...]
        hs_ref[...] = hin
        x, Bm, Cm, pre, dt, A, ac4, ar4, cb, causal, atot4 = _ssd_common(
            x_ref, b_ref, c_ref, dtr_ref, bias_ref, alog_ref, tri_ref, triu_ref)
        xdt = x * _per_head(dt, W)
        rest = _dot(Cm, hin, NT) * _per_head(jnp.exp(ac4), W) + _per_head(dsk_ref[...], W) * x
        for k in range(SSD_HPG):
            blk = slice(P * k, P * (k + 1))
            G = cb * _ssd_decay(k, ac4, ar4, causal)
            y_ref[:, blk] = _dot(G, xdt[:, blk], NN) + rest[:, blk]
        xw = xdt * _per_head(jnp.exp(atot4 - ac4), W)
        h_scr[...] = _head_rows(jnp.exp(atot4)) * hin + _dot(xw, Bm, TN)

    tri, triu = _tri(Q)
    return pl.pallas_call(
        body, name="ssd_fwd", grid=(SSD_GROUPS, NC),
        in_specs=[x_spec, b_spec, c_spec, dt_spec, p_spec, p_spec, p_spec, tri_spec, tri_spec],
        out_specs=[pl.BlockSpec((Q, W), lambda g, c: (c, g)), h_spec],
        out_shape=[jax.ShapeDtypeStruct((T, D_INNER), F32),
                   jax.ShapeDtypeStruct((SSD_GROUPS, NC, W, SSD_STATE), F32)],
        scratch_shapes=[pltpu.VMEM((W, SSD_STATE), F32)],
        compiler_params=_params(("parallel", "arbitrary")),
    )(xc, xc, xc, dtr8, bias8, alog8, dsk8, tri, triu)


def _ssd_bwd(xc, dtr8, bias8, alog8, dsk8, hs, dy):
    T, Q = xc.shape[0], SSD_Q
    NC = T // Q
    P, W = SSD_HEADDIM, SSD_HPG * SSD_HEADDIM
    x_spec, b_spec, c_spec, dt_spec, p_spec, tri_spec, h_spec = _ssd_specs(T, Q, True)
    dy_spec = pl.BlockSpec((Q, W), lambda g, c: (NC - 1 - c, g))
    dbc_spec = pl.BlockSpec((Q, SSD_STATE), lambda g, c: (NC - 1 - c, g))

    def body(x_ref, b_ref, c_ref, dtr_ref, bias_ref, alog_ref, dsk_ref, tri_ref, triu_ref, hs_ref, dy_ref,
             dx_ref, db_ref, dc_ref, ddtr_ref, dbias_ref, dalog_ref, ddsk_ref, dh_scr):
        @pl.when(pl.program_id(1) == 0)
        def _():
            dh_scr[...] = jnp.zeros_like(dh_scr)
            dbias_ref[...] = jnp.zeros_like(dbias_ref)
            dalog_ref[...] = jnp.zeros_like(dalog_ref)
            ddsk_ref[...] = jnp.zeros_like(ddsk_ref)

        x, Bm, Cm, pre, dt, A, ac4, ar4, cb, causal, atot4 = _ssd_common(
            x_ref, b_ref, c_ref, dtr_ref, bias_ref, alog_ref, tri_ref, triu_ref)
        dyv, hin, dho = dy_ref[...], hs_ref[...], dh_scr[...]
        dt_w = _per_head(dt, W)
        xdt = x * dt_w
        E4, F4, etot4 = jnp.exp(ac4), jnp.exp(atot4 - ac4), jnp.exp(atot4)
        F_w = _per_head(F4, W)
        ddsk_ref[...] += _head_sums(_colsum(dyv * x))
        Z = _dot(Cm, hin, NT)
        dZ = dyv * _per_head(E4, W)
        dac4 = _head_sums(dyv * Z) * E4
        dC = _dot(dZ, hin, NN)
        dh_scr[...] = _dot(dZ, Cm, TN) + _head_rows(etot4) * dho
        per_row = jnp.sum(dho * hin, axis=1, keepdims=True)
        lane4 = lax.broadcasted_iota(jnp.int32, (1, SSD_HPG), 1)
        datot4 = jnp.zeros((1, SSD_HPG), F32)
        for k in range(SSD_HPG):
            datot4 = jnp.where(lane4 == k, jnp.sum(per_row[P * k:P * (k + 1)], keepdims=True), datot4)
        datot4 = datot4 * etot4
        dxw = _dot(Bm, dho, NT)
        dB = _dot(xdt * F_w, dho, NN)
        dFa4 = _head_sums(dxw * xdt) * F4
        datot4 = datot4 + _colsum(dFa4)
        dac4 = dac4 - dFa4
        dcb = jnp.zeros((Q, Q), F32)
        lane_q4 = lax.broadcasted_iota(jnp.int32, (Q, SSD_HPG), 1)
        sub_4q = lax.broadcasted_iota(jnp.int32, (SSD_HPG, Q), 0)
        row_sums = jnp.zeros((Q, SSD_HPG), F32)
        col_sums = jnp.zeros((SSD_HPG, Q), F32)
        dxdt_heads = []
        for k in range(SSD_HPG):
            blk = slice(P * k, P * (k + 1))
            L = _ssd_decay(k, ac4, ar4, causal)
            G = cb * L
            dG = _dot(dyv[:, blk], xdt[:, blk], NT)
            dxdt_heads.append(_dot(G, dyv[:, blk], TN))
            dcb = dcb + dG * L
            Mseg = dG * G
            row_sums = jnp.where(lane_q4 == k, jnp.sum(Mseg, axis=1, keepdims=True), row_sums)
            col_sums = jnp.where(sub_4q == k, jnp.sum(Mseg, axis=0, keepdims=True), col_sums)
        dxdt = dxw * F_w + jnp.concatenate(dxdt_heads, axis=1)
        last = lax.broadcasted_iota(jnp.int32, (Q, SSD_HPG), 0) == Q - 1
        dac4 = dac4 + row_sums + jnp.where(last, datot4, 0.0)
        dadt4 = _dot(triu_ref[...], dac4, NN, HI) - _dot(triu_ref[...], col_sums, NT, HI)
        ddt4 = _head_sums(dxdt * x) + dadt4 * A
        dalog_ref[...] += _colsum(dadt4 * dt) * A
        ddtr4 = ddt4 * jax.nn.sigmoid(pre)
        dbias_ref[...] += _colsum(ddtr4)
        ddtr_ref[...] = ddtr4
        dx_ref[...] = (_per_head(dsk_ref[...], W) * dyv + dxdt * dt_w).astype(dx_ref.dtype)
        db_ref[...] = dB + _dot(dcb, Cm, TN)
        dc_ref[...] = dC + _dot(dcb, Bm, NN)

    tri, triu = _tri(Q)
    return pl.pallas_call(
        body, name="ssd_bwd", grid=(SSD_GROUPS, NC),
        in_specs=[x_spec, b_spec, c_spec, dt_spec, p_spec, p_spec, p_spec, tri_spec, tri_spec, h_spec, dy_spec],
        out_specs=[dy_spec, dbc_spec, dbc_spec, dt_spec, p_spec, p_spec, p_spec],
        out_shape=[jax.ShapeDtypeStruct((T, D_INNER), F32),
                   jax.ShapeDtypeStruct((T, SSD_GROUPS * SSD_STATE), F32),
                   jax.ShapeDtypeStruct((T, SSD_GROUPS * SSD_STATE), F32),
                   jax.ShapeDtypeStruct(dtr8.shape, F32)] + [jax.ShapeDtypeStruct(bias8.shape, F32)] * 3,
        scratch_shapes=[pltpu.VMEM((W, SSD_STATE), F32)],
        compiler_params=_params(("parallel", "arbitrary")),
    )(xc, xc, xc, dtr8, bias8, alog8, dsk8, tri, triu, hs, dy)


def _to_groups(v):
    return v.reshape(SSD_GROUPS, 1, SSD_HPG)


def _mixer_fwd(x, p, w):
    T, D = x.shape
    (hn,) = _rowwise(lambda xt, g: (_rms(xt, g),), [x], [p["mix_norm"]], [(D, BF16)], name="mix_norm")
    winT = w["w_in"]
    u, z, xbc, dtr, gl = [
        _matmul(hn, winT[IN_OFFS[i]:IN_OFFS[i + 1]], "nt", name=f"mix_in{i}") for i in range(5)]
    tiles = _s5_derive(*[p[k][0] for k in ("s5_A_re", "s5_A_im", "s5_log_dt", "s5_B_re", "s5_B_im", "s5_C_re", "s5_C_im")])
    u = _to_seg(u)
    ys = _s5_fwd(u, tiles)
    (g,) = _rowwise(lambda yt, ut, d: (jax.nn.gelu(yt + d * ut),), [ys, u], [p["s5_D"]], [(D, F32)], name="s5_gelu")
    y5 = _matmul(g, w["s5_w_glu"], "nn", name="s5_glu", out=(BF16,),
                 epilogue=lambda acc, gt, b: (gt * jax.nn.sigmoid(acc + b),), extras=[g, p["s5_b_glu"]])
    xc = _conv_fwd(xbc, w["conv_w"], p["conv_b"])
    dtr8 = dtr.reshape(T, SSD_GROUPS, SSD_HPG).transpose(1, 0, 2)
    ssd_p = [_to_groups(p[k]) for k in ("ssd_dt_bias", "ssd_A_log", "ssd_D")]
    yssd_raw, hs = _ssd_fwd(xc, dtr8, *ssd_p)
    (yssd,) = _rowwise(lambda yt, zt, nw: (_rms(yt * jax.nn.silu(zt), nw),), [yssd_raw, z], [p["ssd_norm"]],
                       [(D_INNER, BF16)], name="ssd_gate")
    p5 = _from_seg(_matmul(y5, w["w_proj_s5"], "nn", name="mix_p5"))
    pssd = _matmul(yssd, w["w_proj_ssd"], "nn", name="mix_pssd")

    def merge(glt, at, bt, bg):
        gates = jax.nn.sigmoid(glt + bg)
        return (gates[:, :D] * at + gates[:, D:] * bt,)

    (merged,) = _rowwise(merge, [gl, p5, pssd], [p["b_gate"]], [(D, BF16)], name="mix_merge")
    y = _matmul(merged, w["w_out"], "nn", name="mix_out", epilogue=lambda acc, xt: (xt + acc,), extras=[x])
    res = dict(hn=hn, u=u, z=z, xbc=xbc, gl=gl, tiles=tiles, ys=ys, g=g, y5=y5, xc=xc, dtr8=dtr8, ssd_p=ssd_p,
               yssd_raw=yssd_raw, hs=hs, yssd=yssd, p5=p5, pssd=pssd, merged=merged)
    return y, res


def _mixer_bwd(dy, x, p, w, r):
    T, D = x.shape
    gp = {}
    dmerged = _matmul(dy, w["w_out"], "nt", name="mix_d_merged")
    gp["w_out"] = _matmul(r["merged"], dy, "tn", name="mix_d_wout")

    def merge_bwd(glt, at, bt, dm, bg):
        def f(q, a_, b_):
            gates = jax.nn.sigmoid(q + bg)
            return gates[:, :D] * a_ + gates[:, D:] * b_
        _, vjp = jax.vjp(f, glt, at, bt)
        dq, da_, db_ = vjp(dm)
        return da_, db_, dq, _colsum(dq)

    dp5, dpssd, dgl, gp["b_gate"] = _rowwise(
        merge_bwd, [r["gl"], r["p5"], r["pssd"], dmerged], [p["b_gate"]],
        [(D, BF16), (D, BF16), (2 * D, BF16)], [((1, 2 * D), F32)], name="mix_d_merge")
    dp5 = _to_seg(dp5)
    dy5 = _matmul(dp5, w["w_proj_s5"], "nt", name="mix_d_y5")
    gp["w_proj_s5"] = _matmul(r["y5"], dp5, "tn", name="mix_d_wp5")
    dyssd = _matmul(dpssd, w["w_proj_ssd"], "nt", name="mix_d_yssd")
    gp["w_proj_ssd"] = _matmul(r["yssd"], dpssd, "tn", name="mix_d_wpssd")

    def gate_bwd(yt, zt, dyt, nw):
        _, vjp = jax.vjp(lambda a_, b_, c_: _rms(a_ * jax.nn.silu(b_), c_), yt, zt, nw)
        return vjp(dyt)

    dyraw, dz, gp["ssd_norm"] = _rowwise(
        gate_bwd, [r["yssd_raw"], r["z"], dyssd], [p["ssd_norm"]],
        [(D_INNER, F32), (D_INNER, BF16)], [((1, D_INNER), F32)], name="ssd_d_gate")
    dxs, dBm, dCm, ddtr8, dbias8, dalog8, ddsk8 = _ssd_bwd(r["xc"], r["dtr8"], *r["ssd_p"], r["hs"], dyraw)
    gp["ssd_dt_bias"], gp["ssd_A_log"], gp["ssd_D"] = [v.reshape(1, SSD_HEADS) for v in (dbias8, dalog8, ddsk8)]
    ddtr = ddtr8.transpose(1, 0, 2).reshape(T, SSD_HEADS)
    conv = [_conv_bwd(r["xbc"], w["conv_w"], p["conv_b"], d, c0, f"conv_bwd{i}")
            for i, (d, c0) in enumerate(((dxs, 0), (dBm, D_INNER), (dCm, D_INNER + SSD_GROUPS * SSD_STATE)))]
    dxbc = [c[0] for c in conv]
    gp["conv_w"] = jnp.concatenate([c[1] for c in conv], axis=1)
    gp["conv_b"] = jnp.concatenate([c[2] for c in conv], axis=1)

    g = r["g"]

    def glu_ep(acc, gt, b, dyt):
        _, vjp = jax.vjp(lambda g_, t_: g_ * jax.nn.sigmoid(t_ + b), gt, acc)
        return vjp(dyt)

    dg1, dt_ = _matmul(g, w["s5_w_glu"], "nn", name="s5_d_glu", out=(F32, BF16), epilogue=glu_ep,
                       extras=[g, p["s5_b_glu"], dy5])
    gp["s5_w_glu"] = _matmul(g, dt_, "tn", name="s5_d_wglu")
    dg = _matmul(dt_, w["s5_w_glu"], "nt", name="s5_d_g", epilogue=lambda acc, e: (acc + e,), extras=[dg1])

    def gelu_bwd(yt, ut, dgt, dtt, d):
        _, vjp = jax.vjp(lambda y_, d_: jax.nn.gelu(y_ + d_ * ut), yt, d)
        dys, dd = vjp(dgt)
        return dys, dys * d, dd, _colsum(dtt.astype(F32))

    dys, dusk, gp["s5_D"], gp["s5_b_glu"] = _rowwise(
        gelu_bwd, [r["ys"], r["u"], dg, dt_], [p["s5_D"]], [(D, F32), (D, F32)], [((1, D), F32), ((1, D), F32)],
        name="s5_d_gelu")
    du, *dtiles = _s5_bwd(r["u"], dys, dusk, r["tiles"])
    du = _from_seg(du)

    pieces = [du, dz, None, ddtr, dgl]
    winT = w["w_in"]
    hn = r["hn"]
    d_rows, dhn = [], None
    cols = [(du, 0, 1024), (dz, 1024, 3072), (dxbc[0], 3072, 5120), (dxbc[1], 5120, 6144), (dxbc[2], 6144, 7168),
            (ddtr, 7168, 7200), (dgl, 7200, 9248)]
    for i, (d, lo, hi) in enumerate(cols):
        d_rows.append(_matmul(d, hn, "tn", name=f"mix_d_win{i}"))
        if dhn is None:
            dhn = _matmul(d, winT[lo:hi], "nn", name=f"mix_d_hn{i}")
        else:
            dhn = _matmul(d, winT[lo:hi], "nn", name=f"mix_d_hn{i}", epilogue=lambda acc, e: (acc + e,), extras=[dhn])
    gp["w_in"] = jnp.concatenate(d_rows, axis=0)
    del pieces

    def norm_bwd(xt, dh, dyt, gn):
        _, vjp = jax.vjp(_rms, xt, gn)
        dx, dgn = vjp(dh)
        return dyt + dx, dgn

    dx, gp["mix_norm"] = _rowwise(norm_bwd, [x, dhn, dy], [p["mix_norm"]], [(D, F32)], [((1, D), F32)], name="mix_d_norm")
    return dx, gp, dtiles


def _s5_param_grads(p, dtiles):
    keys = ("s5_A_re", "s5_A_im", "s5_log_dt", "s5_B_re", "s5_B_im", "s5_C_re", "s5_C_im")
    _, vjp = jax.vjp(_s5_derive, *[p[k][0] for k in keys])
    return {k: v[None] for k, v in zip(keys, vjp(tuple(dtiles)))}


def _local_step(x, target, p, w):
    T, D = x.shape
    x1, r1 = _ffn_fwd(x, p["ffn1_norm"], w["ffn1_w_gate"], w["ffn1_w_up"], w["ffn1_w_down"], "ffn1")
    x2, rm = _mixer_fwd(x1, p, w)
    x3, r2 = _ffn_fwd(x2, p["ffn2_norm"], w["ffn2_w_gate"], w["ffn2_w_up"], w["ffn2_w_down"], "ffn2")

    def head(xt, tt, g):
        def f(x_, g_):
            e = _rms(x_, g_) - tt
            return 0.5 * jnp.sum(jnp.mean(e * e, axis=-1))
        l, (dx_, dg_) = jax.value_and_grad(f, argnums=(0, 1))(xt, g)
        return dx_, l.reshape(1, 1), dg_

    dx3, loss, d_final = _rowwise(head, [x3, target], [p["final_norm"]], [(D, F32)],
                                  [((1, 1), F32), ((1, D), F32)], name="loss_head")
    gp, gw = {"final_norm": d_final}, {}
    dx2, gp["ffn2_norm"], gw["ffn2_w_gate"], gw["ffn2_w_up"], gw["ffn2_w_down"] = _ffn_bwd(
        dx3, x2, p["ffn2_norm"], w["ffn2_w_gate"], w["ffn2_w_up"], w["ffn2_w_down"], r2, "ffn2")
    dx1, gm, dtiles = _mixer_bwd(dx2, x1, p, w, rm)
    for k in ("w_out", "w_proj_s5", "w_proj_ssd", "s5_w_glu", "w_in", "conv_w"):
        gw[k] = gm.pop(k)
    gp.update(gm)
    gp.update(_s5_param_grads(p, dtiles))
    dx0, gp["ffn1_norm"], gw["ffn1_w_gate"], gw["ffn1_w_up"], gw["ffn1_w_down"] = _ffn_bwd(
        dx1, x, p["ffn1_norm"], w["ffn1_w_gate"], w["ffn1_w_up"], w["ffn1_w_down"], r1, "ffn1")
    return loss, dx0, gp, gw


def _adamw(w, m, v, gs, *, name, emit_g, tm=128):
    n_g = len(gs)
    C = w.shape[1]

    def fn(wt, mt, vt, *gt):
        g = gt[0]
        for e in gt[1:]:
            g = g + e
        m2 = ADAM_B1 * mt + (1.0 - ADAM_B1) * g
        v2 = ADAM_B2 * vt + (1.0 - ADAM_B2) * (g * g)
        m_hat = m2 / (1.0 - ADAM_B1 ** ADAM_STEP)
        v_hat = v2 / (1.0 - ADAM_B2 ** ADAM_STEP)
        delta = -ADAM_LR * (m_hat / (jnp.sqrt(v_hat) + ADAM_EPS) + ADAM_WD * wt)
        return ((g,) if emit_g else ()) + (delta, m2, v2)

    assert n_g >= 1
    return _rowwise(fn, [w, m, v, *gs], [], [(C, F32)] * (4 if emit_g else 3), name=name, tm=_rtile(w.shape[0], tm))


def _rtile(r, cap=512):
    if r <= cap:
        return r
    best = None
    for t in range(8, cap + 1, 8):
        if r % t == 0:
            best = t
    assert best is not None, r
    return best


ANY = pl.BlockSpec(memory_space=pl.ANY)


def _where_am_i():
    x, y, c = lax.axis_index("x"), lax.axis_index("y"), lax.axis_index("c")
    ks = (2 * x + y, 2 * (1 - x) + y, 2 * x + (1 - y), 2 * (1 - x) + (1 - y))
    return x, y, c, ks


def _rc(src, dst, ssem, rsem, to):
    return pltpu.make_async_remote_copy(src_ref=src, dst_ref=dst, send_sem=ssem, recv_sem=rsem,
                                        device_id=to, device_id_type=MESH)


def _gather_shards(srcs):
    n = len(srcs)

    def body(*refs):
        src, out = refs[:n], refs[n:2 * n]
        ssem, rsem, lsem = refs[2 * n:]
        x, y, c, (k_me, k_x, k_y, k_d) = _where_am_i()
        sib = (x, y, 1 - c)
        local = [pltpu.make_async_copy(src[i], out[i].at[k_me], lsem.at[i]) for i in range(n)]
        for cp in local:
            cp.start()

        def own(i, q, to):
            return _rc(src[i], out[i].at[k_me], ssem.at[i, q], rsem.at[i, q], to)

        def slab(i, q, k, to):
            return _rc(out[i].at[k], out[i].at[k], ssem.at[i, q], rsem.at[i, q], to)

        @pl.when(c == 1)
        def _():
            sends = [own(i, 0, (1 - x, y, 1)) for i in range(n)]
            for cp in sends:
                cp.start()
            fwd = [slab(i, 3, k_x, sib) for i in range(n)]
            for i in range(n):
                slab(i, 0, k_x, sib).wait_recv()
                fwd[i].start()
            for i in range(n):
                slab(i, 4, k_y, sib).wait_recv()
                slab(i, 5, k_d, sib).wait_recv()
            for cp in sends + fwd:
                cp.wait_send()

        @pl.when(c == 0)
        def _():
            sends = [own(i, 1, (x, 1 - y, 0)) for i in range(n)] + [own(i, 2, (1 - x, 1 - y, 0)) for i in range(n)]
            for cp in sends:
                cp.start()
            fwd = [slab(i, 4, k_y, sib) for i in range(n)] + [slab(i, 5, k_d, sib) for i in range(n)]
            for i in range(n):
                slab(i, 1, k_y, sib).wait_recv()
                fwd[i].start()
            for i in range(n):
                slab(i, 2, k_d, sib).wait_recv()
                fwd[n + i].start()
            for i in range(n):
                slab(i, 3, k_x, sib).wait_recv()
            for cp in sends + fwd:
                cp.wait_send()

        for cp in local:
            cp.wait()

    return pl.pallas_call(
        body, name="gather_shards",
        in_specs=[ANY] * n, out_specs=[ANY] * n,
        out_shape=[jax.ShapeDtypeStruct((4,) + s.shape, s.dtype) for s in srcs],
        scratch_shapes=[pltpu.SemaphoreType.DMA((n, 6)), pltpu.SemaphoreType.DMA((n, 6)), pltpu.SemaphoreType.DMA((n,))],
        compiler_params=pltpu.CompilerParams(has_side_effects=True),
    )(*srcs)


def _swap_slabs(arrs, n_slabs, name):
    n = len(arrs)
    J = max(n_slabs, 1)

    def body(*refs):
        src, out = refs[:n], refs[n:2 * n]
        ssem, rsem = refs[2 * n:]
        x, y, c, (k_me, k_x, k_y, k_d) = _where_am_i()
        sib = (x, y, 1 - c)
        sel = (jnp.where(c == 1, k_y, k_me), jnp.where(c == 1, k_d, k_x))
        cps = []
        for i in range(n):
            for j in range(J):
                s = src[i].at[sel[j]] if n_slabs else src[i]
                d = out[i].at[j] if n_slabs else out[i]
                cps.append(_rc(s, d, ssem.at[i, j], rsem.at[i, j], sib))
        for cp in cps:
            cp.start()
        for cp in cps:
            cp.wait()

    return pl.pallas_call(
        body, name=name, in_specs=[ANY] * n, out_specs=[ANY] * n,
        out_shape=[jax.ShapeDtypeStruct(((n_slabs,) if n_slabs else ()) + a.shape[-2:], a.dtype) for a in arrs],
        scratch_shapes=[pltpu.SemaphoreType.DMA((n, J)), pltpu.SemaphoreType.DMA((n, J))],
        compiler_params=pltpu.CompilerParams(has_side_effects=True),
    )(*arrs)


def _send_chip_sums(arrs):
    n = len(arrs)

    def body(*refs):
        src, out = refs[:n], refs[n:2 * n]
        ssem, rsem = refs[2 * n:]
        x, y, c, _ = _where_am_i()

        @pl.when(c == 1)
        def _():
            cps = [_rc(src[i].at[1], out[i].at[0], ssem.at[i, 0], rsem.at[i, 0], (1 - x, y, 1)) for i in range(n)]
            for cp in cps:
                cp.start()
            for cp in cps:
                cp.wait()

        @pl.when(c == 0)
        def _():
            cps = [_rc(src[i].at[0], out[i].at[0], ssem.at[i, 0], rsem.at[i, 0], (x, 1 - y, 0)) for i in range(n)]
            cps += [_rc(src[i].at[1], out[i].at[1], ssem.at[i, 1], rsem.at[i, 1], (1 - x, 1 - y, 0)) for i in range(n)]
            for cp in cps:
                cp.start()
            for cp in cps:
                cp.wait()

    return pl.pallas_call(
        body, name="send_chip_sums", in_specs=[ANY] * n, out_specs=[ANY] * n,
        out_shape=[jax.ShapeDtypeStruct(a.shape, a.dtype) for a in arrs],
        scratch_shapes=[pltpu.SemaphoreType.DMA((n, 2)), pltpu.SemaphoreType.DMA((n, 2))],
        compiler_params=pltpu.CompilerParams(has_side_effects=True),
    )(*arrs)


def _chip_sum(g, recv, sel, name):
    _, r, C = g.shape
    tr = _rtile(r)

    def body(sel_ref, g_ref, r_ref, o32_ref, o16_ref):
        s = g_ref[...] + r_ref[...]
        o32_ref[...] = s
        o16_ref[...] = s.astype(BF16)

    blk = pl.BlockSpec((None, tr, C), lambda j, t, sel_ref: (j, t, 0))
    return pl.pallas_call(
        body, name=name,
        grid_spec=pltpu.PrefetchScalarGridSpec(
            num_scalar_prefetch=1, grid=(2, r // tr),
            in_specs=[pl.BlockSpec((None, tr, C), lambda j, t, sel_ref: (sel_ref[j], t, 0)), blk],
            out_specs=[blk, blk]),
        out_shape=[jax.ShapeDtypeStruct((2, r, C), F32), jax.ShapeDtypeStruct((2, r, C), BF16)],
        compiler_params=_params(("arbitrary", "arbitrary")),
    )(sel, g, recv)


def _cross_sum(s32, got, north, name):
    _, r, C = s32.shape
    tr = _rtile(r)

    def body(f_ref, p_ref, a_ref, b_ref, o_ref):
        a = a_ref[...].astype(F32)

        @pl.when(f_ref[0] == 1)
        def _():
            o_ref[...] = p_ref[...] + a

        @pl.when(f_ref[0] == 0)
        def _():
            o_ref[...] = a + b_ref[...].astype(F32)

    return pl.pallas_call(
        body, name=name,
        grid_spec=pltpu.PrefetchScalarGridSpec(
            num_scalar_prefetch=1, grid=(r // tr,),
            in_specs=[pl.BlockSpec((None, tr, C), lambda t, f: (0, t, 0)),
                      pl.BlockSpec((None, tr, C), lambda t, f: (0, t, 0)),
                      pl.BlockSpec((None, tr, C), lambda t, f: (1 - f[0], t, 0))],
            out_specs=pl.BlockSpec((tr, C), lambda t, f: (t, 0))),
        out_shape=jax.ShapeDtypeStruct((r, C), F32),
        compiler_params=_params(("arbitrary",)),
    )(north, s32, got, got)


def _reduce_scatter(grads):
    x, y, c, (k_me, k_x, k_y, k_d) = _where_am_i()
    sel = jnp.stack([jnp.where(c == 1, k_me, k_y), jnp.where(c == 1, k_x, k_d)]).astype(jnp.int32)
    north = jnp.reshape(c, (1,)).astype(jnp.int32)
    from_sib = _swap_slabs(grads, 2, "swap_grad_slabs")
    sums = [_chip_sum(g, r, sel, f"chip_sum{i}") for i, (g, r) in enumerate(zip(grads, from_sib))]
    got = _send_chip_sums([s16 for _, s16 in sums])
    parts = [_cross_sum(s32, b, north, f"cross_sum{i}") for i, ((s32, _), b) in enumerate(zip(sums, got))]
    theirs = _swap_slabs(parts, 0, "swap_parts")
    return list(zip(parts, theirs))


def _all_reduce_small(v):
    R, C = v.shape

    def body(v_ref, o_ref, buf, ssem, rsem, lsem):
        x, y, c = lax.axis_index("x"), lax.axis_index("y"), lax.axis_index("c")
        me, sib = (x, y, c), (x, y, 1 - c)
        chips = [(1 - x, y), (x, 1 - y), (1 - x, 1 - y)]

        def slot(px, py, pc):
            return buf.at[4 * px + 2 * py + pc]

        def copy(k, block, to, src=None):
            return _rc(slot(*block) if src is None else src, slot(*block), ssem.at[k], rsem.at[k], to)

        mine = pltpu.make_async_copy(v_ref, slot(*me), lsem)
        mine.start()
        first = [copy(0, me, sib, src=v_ref)] + [copy(1 + j, me, (*chip, c), src=v_ref) for j, chip in enumerate(chips)]
        for cp in first:
            cp.start()
        passed = [copy(4 + j, (*chip, c), sib) for j, chip in enumerate(chips)]
        for j, chip in enumerate(chips):
            copy(1 + j, (*chip, c), me).wait_recv()
            passed[j].start()
        copy(0, sib, me).wait_recv()
        for j, chip in enumerate(chips):
            copy(4 + j, (*chip, 1 - c), me).wait_recv()
        for cp in first + passed:
            cp.wait_send()
        mine.wait()
        acc = buf[0]
        for k in range(1, 8):
            acc = acc + buf[k]
        o_ref[...] = acc

    return pl.pallas_call(
        body, name="all_reduce_small",
        in_specs=[pl.BlockSpec(memory_space=pltpu.VMEM)], out_specs=pl.BlockSpec(memory_space=pltpu.VMEM),
        out_shape=jax.ShapeDtypeStruct((R, C), F32),
        scratch_shapes=[pltpu.VMEM((8, R, C), F32), pltpu.SemaphoreType.DMA((7,)), pltpu.SemaphoreType.DMA((7,)),
                        pltpu.SemaphoreType.DMA],
        compiler_params=pltpu.CompilerParams(has_side_effects=True, vmem_limit_bytes=VMEM_LIMIT),
    )(v)


WEIGHTS = ['ffn1_norm', 'ffn1_w_gate', 'ffn1_w_up', 'ffn1_w_down', 'mix_norm', 'w_in', 'conv_w', 'conv_b', 's5_A_re',
           's5_A_im', 's5_log_dt', 's5_B_re', 's5_B_im', 's5_C_re', 's5_C_im', 's5_D', 's5_w_glu', 's5_b_glu',
           'ssd_A_log', 'ssd_dt_bias', 'ssd_D', 'ssd_norm', 'w_proj_s5', 'w_proj_ssd', 'b_gate', 'w_out', 'ffn2_norm',
           'ffn2_w_gate', 'ffn2_w_up', 'ffn2_w_down', 'final_norm']
ARGS = ['x'] + WEIGHTS + ['loss_target'] + ['m_' + n for n in WEIGHTS] + ['v_' + n for n in WEIGHTS]
COL_SHARDED = ('ffn1_w_gate', 'ffn1_w_up', 'w_in', 'ffn2_w_gate', 'ffn2_w_up')
ROW_SHARDED = ('ffn1_w_down', 's5_w_glu', 'w_proj_s5', 'w_proj_ssd', 'w_out', 'ffn2_w_down')
MATRICES = COL_SHARDED + ROW_SHARDED
SMALL = [n for n in WEIGHTS if n not in MATRICES]


def _pack(arrs, width=1024):
    flat = jnp.concatenate([a.reshape(-1).astype(F32) for a in arrs])
    rows = -(-flat.shape[0] // (8 * width)) * 8
    return jnp.pad(flat, (0, rows * width - flat.shape[0])).reshape(rows, width)


def _unpack(packed, shapes):
    flat, out, o = packed.reshape(-1), [], 0
    for s in shapes:
        n = math.prod(s)
        out.append(flat[o:o + n].reshape(s))
        o += n
    return out


def kernel(x, ffn1_norm, ffn1_w_gate, ffn1_w_up, ffn1_w_down, mix_norm, w_in, conv_w, conv_b, s5_A_re, s5_A_im, s5_log_dt, s5_B_re, s5_B_im, s5_C_re, s5_C_im, s5_D, s5_w_glu, s5_b_glu, ssd_A_log, ssd_dt_bias, ssd_D, ssd_norm, w_proj_s5, w_proj_ssd, b_gate, w_out, ffn2_norm, ffn2_w_gate, ffn2_w_up, ffn2_w_down, final_norm, loss_target, m_ffn1_norm, m_ffn1_w_gate, m_ffn1_w_up, m_ffn1_w_down, m_mix_norm, m_w_in, m_conv_w, m_conv_b, m_s5_A_re, m_s5_A_im, m_s5_log_dt, m_s5_B_re, m_s5_B_im, m_s5_C_re, m_s5_C_im, m_s5_D, m_s5_w_glu, m_s5_b_glu, m_ssd_A_log, m_ssd_dt_bias, m_ssd_D, m_ssd_norm, m_w_proj_s5, m_w_proj_ssd, m_b_gate, m_w_out, m_ffn2_norm, m_ffn2_w_gate, m_ffn2_w_up, m_ffn2_w_down, m_final_norm, v_ffn1_norm, v_ffn1_w_gate, v_ffn1_w_up, v_ffn1_w_down, v_mix_norm, v_w_in, v_conv_w, v_conv_b, v_s5_A_re, v_s5_A_im, v_s5_log_dt, v_s5_B_re, v_s5_B_im, v_s5_C_re, v_s5_C_im, v_s5_D, v_s5_w_glu, v_s5_b_glu, v_ssd_A_log, v_ssd_dt_bias, v_ssd_D, v_ssd_norm, v_w_proj_s5, v_w_proj_ssd, v_b_gate, v_w_out, v_ffn2_norm, v_ffn2_w_gate, v_ffn2_w_up, v_ffn2_w_down, v_final_norm):
    a = dict(locals())
    assert list(a) == ARGS
    x, target = a['x'][0], a['loss_target'][0]
    k_me = 2 * lax.axis_index("x") + lax.axis_index("y")

    srcs = [a[n][0].T.astype(BF16) for n in COL_SHARDED] + [a[n][0].astype(BF16) for n in ROW_SHARDED] + [a['conv_w'][0]]
    full = _gather_shards(srcs)
    w = {n: f.reshape(-1, f.shape[-1]) for n, f in zip(MATRICES, full)}
    w['conv_w'] = full[-1].transpose(1, 0, 2).reshape(CONV_K, CONV_DIM)
    p = {n: a[n] for n in SMALL if n != 'conv_w'}
    p['final_norm'] = a['final_norm'][None]

    loss, grad_x, gp, gw = _local_step(x, target, p, w)

    pairs = _reduce_scatter([gw[n].reshape(4, -1, D_MODEL) for n in MATRICES])
    gp['conv_w'] = gw['conv_w'][None]
    small_shapes = [(1, 1)] + [gp[n].shape if n != 'final_norm' else (1, D_MODEL) for n in SMALL]
    red = _unpack(_all_reduce_small(_pack([loss] + [gp[n] for n in SMALL])), small_shapes)
    loss_all = red[0].reshape(())
    gsmall = dict(zip(SMALL, red[1:]))
    gsmall['conv_w'] = lax.dynamic_slice_in_dim(gsmall['conv_w'], k_me * 1024, 1024, axis=2)
    gsmall = {n: g.reshape(a[n].shape) for n, g in gsmall.items()}

    grads, delta, new_m, new_v = {}, {}, {}, {}
    for n, (mine, theirs) in zip(MATRICES, pairs):
        wn, mn, vn = a[n][0], a['m_' + n][0], a['v_' + n][0]
        if n in COL_SHARDED:
            (gt,) = _rowwise(lambda p_, q_: (p_ + q_,), [mine, theirs], [], [(D_MODEL, F32)], name=f"sum_{n}",
                             tm=_rtile(mine.shape[0]))
            g = gt.T
            d, m2, v2 = _adamw(wn, mn, vn, [g], name=f"adamw_{n}", emit_g=False)
        else:
            g, d, m2, v2 = _adamw(wn, mn, vn, [mine, theirs], name=f"adamw_{n}", emit_g=True)
        grads[n], delta[n], new_m[n], new_v[n] = g[None], d[None], m2[None], v2[None]
    sw, sm, sv, sg = [_pack([t[n] for n in SMALL]) for t in (a, {n: a['m_' + n] for n in SMALL},
                                                             {n: a['v_' + n] for n in SMALL}, gsmall)]
    d, m2, v2 = _adamw(sw, sm, sv, [sg], name="adamw_small", emit_g=False)
    shapes = [a[n].shape for n in SMALL]
    for n, dd, mm, vv in zip(SMALL, _unpack(d, shapes), _unpack(m2, shapes), _unpack(v2, shapes)):
        grads[n], delta[n], new_m[n], new_v[n] = gsmall[n], dd, mm, vv
    return (loss_all, grad_x[None], *[grads[n] for n in WEIGHTS], *[delta[n] for n in WEIGHTS],
            *[new_m[n] for n in WEIGHTS], *[new_v[n] for n in WEIGHTS])
```

```python
import itertools
import math

import jax
import jax.numpy as jnp
from jax import lax
from jax.experimental import pallas as pl
from jax.experimental.pallas import tpu as pltpu
from jax.experimental.pallas import tpu_sc as plsc

F32 = jnp.float32
BF16 = jnp.bfloat16
MXU = BF16
HI = lax.Precision.HIGHEST

D_MODEL = 1024
D_FF = 2816
EPS = 1e-6
S5_GROUPS, S5_GROUP, S5_STATE = 64, 16, 64
D_INNER = 2048
SSD_HEADDIM, SSD_HEADS, SSD_GROUPS, SSD_HPG, SSD_STATE = 64, 32, 8, 4, 128
CONV_K, CONV_DIM = 4, 4096
IN_SPLITS = (1024, 2048, 4096, 32, 2048)
IN_OFFS = (0, 1024, 3072, 7168, 7200, 9248)
SSD_Q = 256
ADAM_LR, ADAM_B1, ADAM_B2, ADAM_EPS, ADAM_WD, ADAM_STEP = 0.001, 0.9, 0.999, 1e-08, 0.01, 10

VMEM_LIMIT = 56 * 1024 * 1024
MATMUL_VMEM = 32 * 1024 * 1024
MESH = pl.DeviceIdType.MESH

NN = ((1,), (0,))
NT = ((1,), (1,))
TN = ((0,), (0,))


def _dot(a, b, dims, precision=None):
    if precision is None:
        a, b = a.astype(MXU), b.astype(MXU)
    return lax.dot_general(a, b, (dims, ((), ())), precision=precision, preferred_element_type=F32)


def _tile(n, pref):
    if n <= pref:
        return n
    best = None
    for t in range(128, pref + 1, 128):
        if n % t == 0:
            best = t
    assert best is not None, (n, pref)
    return best


def _params(sem):
    return pltpu.CompilerParams(dimension_semantics=sem, vmem_limit_bytes=VMEM_LIMIT)


def _matmul(a, b, mode, *, name, out=((F32),), epilogue=None, extras=(), tm=512, tn=1536):
    if mode == "nn":
        (M, K), (_, N) = a.shape, b.shape
    elif mode == "nt":
        (M, K), (N, _) = a.shape, b.shape
    else:
        (K, M), (_, N) = a.shape, b.shape
    tm, tn = _tile(M, tm), _tile(N, tn)

    def vmem(tm_, tn_):
        per = K * tm_ * a.dtype.itemsize + K * tn_ * b.dtype.itemsize
        per += sum(tm_ * tn_ * jnp.dtype(dt).itemsize for dt in out)
        per += sum((1 if e.shape[0] == 1 else tm_) * tn_ * e.dtype.itemsize for e in extras)
        return 2 * per

    while vmem(tm, tn) > MATMUL_VMEM and (tn > 128 or tm > 128):
        if tn >= tm and tn > 128:
            tn = _tile(N, tn - 128)
        else:
            tm = _tile(M, tm - 128)
    a_spec = pl.BlockSpec((K, tm), lambda i, j: (0, i)) if mode == "tn" else pl.BlockSpec((tm, K), lambda i, j: (i, 0))
    b_spec = pl.BlockSpec((tn, K), lambda i, j: (j, 0)) if mode == "nt" else pl.BlockSpec((K, tn), lambda i, j: (0, j))
    dims = {"nn": NN, "nt": NT, "tn": TN}[mode]
    e_specs = []
    for e in extras:
        if e.shape[0] == 1:
            e_specs.append(pl.BlockSpec((1, tn), lambda i, j: (0, j)))
        else:
            assert e.shape == (M, N), (e.shape, M, N)
            e_specs.append(pl.BlockSpec((tm, tn), lambda i, j: (i, j)))
    n_e, n_o = len(extras), len(out)

    def body(a_ref, b_ref, *refs):
        acc = _dot(a_ref[...], b_ref[...], dims)
        res = (acc,) if epilogue is None else epilogue(acc, *[r[...] for r in refs[:n_e]])
        for r, v in zip(refs[n_e:], res):
            r[...] = v.astype(r.dtype)

    res = pl.pallas_call(
        body, name=name, grid=(M // tm, N // tn),
        in_specs=[a_spec, b_spec] + e_specs,
        out_specs=[pl.BlockSpec((tm, tn), lambda i, j: (i, j)) for _ in range(n_o)],
        out_shape=[jax.ShapeDtypeStruct((M, N), dt) for dt in out],
        compiler_params=_params(("parallel", "parallel")),
    )(a, b, *extras)
    return res[0] if n_o == 1 else res


def _rowwise(fn, rows, fulls, outs, reds=(), *, name, tm=256):
    T = rows[0].shape[0]
    tm = min(tm, T)
    n_r, n_f, n_o, n_d = len(rows), len(fulls), len(outs), len(reds)

    def body(*refs):
        ins = [r[...] for r in refs[:n_r + n_f]]
        o_refs = refs[n_r + n_f:n_r + n_f + n_o]
        d_refs = refs[n_r + n_f + n_o:]
        res = fn(*ins)
        for r, v in zip(o_refs, res[:n_o]):
            r[...] = v.astype(r.dtype)
        if n_d:
            @pl.when(pl.program_id(0) == 0)
            def _():
                for r in d_refs:
                    r[...] = jnp.zeros_like(r)
            for r, v in zip(d_refs, res[n_o:]):
                r[...] += v.astype(r.dtype)

    res = pl.pallas_call(
        body, name=name, grid=(T // tm,),
        in_specs=[pl.BlockSpec((tm, r.shape[1]), lambda i: (i, 0)) for r in rows]
        + [pl.BlockSpec(f.shape, lambda i, nd=f.ndim: (0,) * nd) for f in fulls],
        out_specs=[pl.BlockSpec((tm, c), lambda i: (i, 0)) for c, _ in outs]
        + [pl.BlockSpec(s, lambda i, nd=len(s): (0,) * nd) for s, _ in reds],
        out_shape=[jax.ShapeDtypeStruct((T, c), dt) for c, dt in outs]
        + [jax.ShapeDtypeStruct(s, dt) for s, dt in reds],
        compiler_params=_params(("arbitrary",)),
    )(*rows, *fulls)
    return res


def _rms(x, g):
    return x * lax.rsqrt(jnp.mean(x * x, axis=-1, keepdims=True) + EPS) * g


def _colsum(v):
    return jnp.sum(v, axis=0, keepdims=True)


def _softplus(x):
    return jnp.maximum(x, 0.0) + jnp.log1p(jnp.exp(-jnp.abs(x)))


def _ffn_fwd(x, norm, wgT, wuT, wd, tag):
    D = x.shape[1]
    (hn,) = _rowwise(lambda xt, g: (_rms(xt, g),), [x], [norm], [(D, BF16)], name=f"{tag}_norm")
    a = _matmul(hn, wgT, "nt", name=f"{tag}_gate")
    b = _matmul(hn, wuT, "nt", name=f"{tag}_up")
    (hid,) = _rowwise(lambda at, bt: (jax.nn.silu(at) * bt,), [a, b], [], [(a.shape[1], BF16)], name=f"{tag}_act")
    y = _matmul(hid, wd, "nn", name=f"{tag}_down", epilogue=lambda acc, xt: (xt + 0.5 * acc,), extras=[x])
    return y, (hn, a, b, hid)


def _ffn_bwd(dy, x, norm, wgT, wuT, wd, res, tag):
    hn, a, b, hid = res
    D, F = x.shape[1], a.shape[1]
    dhid = _matmul(dy, wd, "nt", name=f"{tag}_d_hid", epilogue=lambda acc: (0.5 * acc,))
    d_wd = _matmul(hid, dy, "tn", name=f"{tag}_d_wd", epilogue=lambda acc: (0.5 * acc,))

    def act_bwd(at, bt, dh):
        _, vjp = jax.vjp(lambda p, q: jax.nn.silu(p) * q, at, bt)
        return vjp(dh)

    da, db = _rowwise(act_bwd, [a, b, dhid], [], [(F, BF16), (F, BF16)], name=f"{tag}_d_act")
    d_wgT = _matmul(da, hn, "tn", name=f"{tag}_d_wg")
    d_wuT = _matmul(db, hn, "tn", name=f"{tag}_d_wu")
    dhn = _matmul(da, wgT, "nn", name=f"{tag}_d_hn1")
    dhn = _matmul(db, wuT, "nn", name=f"{tag}_d_hn2", epilogue=lambda acc, e: (acc + e,), extras=[dhn])

    def norm_bwd(xt, dh, dyt, g):
        _, vjp = jax.vjp(_rms, xt, g)
        dx, dg = vjp(dh)
        return dyt + dx, dg

    dx, d_norm = _rowwise(norm_bwd, [x, dhn, dy], [norm], [(D, F32)], [((1, D), F32)], name=f"{tag}_d_norm")
    return dx, d_norm, d_wgT, d_wuT, d_wd


S5_TILES = 8
S5_HALF = 256


def _s5_derive(A_re, A_im, log_dt, B_re, B_im, C_re, C_im):
    G, N, M = S5_GROUPS, S5_STATE, S5_GROUP
    dt = jnp.exp(log_dt)[:, None]
    mag = jnp.exp(A_re * dt)
    ar, ai = mag * jnp.cos(A_im * dt), mag * jnp.sin(A_im * dt)
    den = A_re * A_re + A_im * A_im
    cr = ((ar - 1.0) * A_re + ai * A_im) / den
    ci = (ai * A_re - (ar - 1.0) * A_im) / den
    bbr = cr[..., None] * B_re - ci[..., None] * B_im
    bbi = cr[..., None] * B_im + ci[..., None] * B_re
    eye = jnp.eye(8, dtype=F32)

    def tile_in(bb):
        t = bb.reshape(S5_TILES, 8, N, M).transpose(0, 1, 3, 2)
        return jnp.einsum("jamn,ab->jambn", t, eye).reshape(S5_TILES, 8 * M, 8 * N)

    def tile_out(c):
        t = c.reshape(S5_TILES, 8, M, N).transpose(0, 1, 3, 2)
        return jnp.einsum("janm,ab->janbm", t, eye).reshape(S5_TILES, 8 * N, 8 * M)

    return (tile_in(bbr), tile_in(bbi), tile_out(C_re), tile_out(C_im),
            ar.reshape(S5_TILES, 1, 8 * N), ai.reshape(S5_TILES, 1, 8 * N))


S5_NB = S5_HALF // 128
S5_SEG = 32


def _cmul(ar, ai, br, bi):
    return ar * br - ai * bi, ar * bi + ai * br


def _s5_scan(sr_ref, si_ref, ar, ai, T, reverse):
    L, V = T // S5_SEG, S5_SEG // 8
    assert L * S5_SEG == T and L & (L - 1) == 0, T
    sg = -1.0 if reverse else 1.0
    a_r = [jnp.broadcast_to(ar[:, 128 * b:128 * (b + 1)], (8, 128)) for b in range(S5_NB)]
    a_i = [jnp.broadcast_to(sg * ai[:, 128 * b:128 * (b + 1)], (8, 128)) for b in range(S5_NB)]

    def rows(k, v):
        return pl.ds(pl.multiple_of(((L - 1 - k) if reverse else k) * S5_SEG + 8 * v, 8), 8)

    def local(k, carry):
        out = []
        for b in range(S5_NB):
            for v in range(V):
                idx = rows(k, v)
                mr, mi = _cmul(a_r[b], a_i[b], *carry[b * V + v])
                nr, ni = mr + sr_ref[b, idx, :], mi + si_ref[b, idx, :]
                sr_ref[b, idx, :] = nr
                si_ref[b, idx, :] = ni
                out.append((nr, ni))
        return tuple(out)

    z = jnp.zeros((8, 128), F32)
    ends = lax.fori_loop(0, L, local, tuple((z, z) for _ in range(S5_NB * V)))

    carries = []
    for b in range(S5_NB):
        pr, pi = a_r[b][0:1], a_i[b][0:1]
        n = L
        while n > 1:
            pr, pi = _cmul(pr, pi, pr, pi)
            n //= 2
        cr, ci = jnp.zeros((1, 128), F32), jnp.zeros((1, 128), F32)
        into = [None] * S5_SEG
        for j in (reversed(range(S5_SEG)) if reverse else range(S5_SEG)):
            into[j] = (cr, ci)
            er, ei = ends[b * V + j // 8]
            mr, mi = _cmul(pr, pi, cr, ci)
            cr, ci = mr + er[j % 8:j % 8 + 1], mi + ei[j % 8:j % 8 + 1]
        carries.append([(jnp.concatenate([into[8 * v + s][0] for s in range(8)], axis=0),
                         jnp.concatenate([into[8 * v + s][1] for s in range(8)], axis=0)) for v in range(V)])

    def fix(k, powers):
        out = []
        for b in range(S5_NB):
            pr, pi = powers[b]
            for v in range(V):
                idx = rows(k, v)
                dr, di = _cmul(pr, pi, *carries[b][v])
                sr_ref[b, idx, :] += dr
                si_ref[b, idx, :] += di
            out.append(_cmul(pr, pi, a_r[b], a_i[b]))
        return tuple(out)

    lax.fori_loop(0, L, fix, tuple((a_r[b], a_i[b]) for b in range(S5_NB)))


def _s5_in(ut, b_ref, s_ref):
    for b in range(S5_NB):
        s_ref[b] = _dot(ut, b_ref[:, 128 * b:128 * (b + 1)], NN)


def _permute_rows(v, to_seg, name):
    T, C = v.shape
    L = T // S5_SEG
    shape_in, shape_out = ((S5_SEG, L, C), (L, S5_SEG, C)) if to_seg else ((L, S5_SEG, C), (S5_SEG, L, C))

    def body(i_ref, o_ref, sems):
        cps = []
        for s in range(S5_SEG):
            src, dst = (i_ref.at[s], o_ref.at[:, s, :]) if to_seg else (i_ref.at[:, s, :], o_ref.at[s])
            cps.append(pltpu.make_async_copy(src, dst, sems.at[s]))
            cps[-1].start()
        for cp in cps:
            cp.wait()

    return pl.pallas_call(
        body, name=name, in_specs=[pl.BlockSpec(memory_space=pl.ANY)], out_specs=pl.BlockSpec(memory_space=pl.ANY),
        out_shape=jax.ShapeDtypeStruct(shape_out, v.dtype), scratch_shapes=[pltpu.SemaphoreType.DMA((S5_SEG,))],
    )(v.reshape(shape_in)).reshape(T, C)


def _to_seg(v, name):
    return _permute_rows(v, True, name)


def _from_seg(v, name):
    return _permute_rows(v, False, name)


def _s5_specs(T):
    u_spec = pl.BlockSpec((T, 128), lambda j, h: (0, j))
    b_spec = pl.BlockSpec((None, 128, S5_HALF), lambda j, h: (j, 0, h))
    c_spec = pl.BlockSpec((None, S5_HALF, 128), lambda j, h: (j, h, 0))
    a_spec = pl.BlockSpec((None, 1, S5_HALF), lambda j, h: (j, 0, h))
    return u_spec, b_spec, c_spec, a_spec


def _s5_fwd(u, tiles):
    T = u.shape[0]
    u_spec, b_spec, c_spec, a_spec = _s5_specs(T)

    def body(u_ref, br_ref, bi_ref, cr_ref, ci_ref, ar_ref, ai_ref, y_ref, sr_ref, si_ref):
        ut = u_ref[...].astype(MXU)
        _s5_in(ut, br_ref, sr_ref)
        _s5_in(ut, bi_ref, si_ref)
        _s5_scan(sr_ref, si_ref, ar_ref[...], ai_ref[...], T, False)
        y = None
        for b in range(S5_NB):
            blk = slice(128 * b, 128 * (b + 1))
            yb = _dot(sr_ref[b], cr_ref[blk, :], NN) - _dot(si_ref[b], ci_ref[blk, :], NN)
            y = yb if y is None else y + yb

        @pl.when(pl.program_id(1) == 0)
        def _():
            y_ref[...] = y

        @pl.when(pl.program_id(1) == 1)
        def _():
            y_ref[...] += y

    scr = pltpu.VMEM((S5_NB, T, 128), F32)
    return pl.pallas_call(
        body, name="s5_fwd", grid=(S5_TILES, 2),
        in_specs=[u_spec, b_spec, b_spec, c_spec, c_spec, a_spec, a_spec],
        out_specs=u_spec, out_shape=jax.ShapeDtypeStruct(u.shape, F32),
        scratch_shapes=[scr, scr],
        compiler_params=_params(("parallel", "arbitrary")),
    )(u, *tiles)


def _s5_bwd(u, dys, du_skip, tiles):
    T = u.shape[0]
    u_spec, b_spec, c_spec, a_spec = _s5_specs(T)

    def body(u_ref, dy_ref, sk_ref, br_ref, bi_ref, cr_ref, ci_ref, ar_ref, ai_ref,
             du_ref, dbr_ref, dbi_ref, dcr_ref, dci_ref, dar_ref, dai_ref, sr_ref, si_ref, lr_ref, li_ref):
        ut, dy = u_ref[...].astype(MXU), dy_ref[...].astype(MXU)
        ar, ai = ar_ref[...], ai_ref[...]
        _s5_in(ut, br_ref, sr_ref)
        _s5_in(ut, bi_ref, si_ref)
        _s5_scan(sr_ref, si_ref, ar, ai, T, False)
        for b in range(S5_NB):
            blk = slice(128 * b, 128 * (b + 1))
            dcr_ref[blk, :] = _dot(sr_ref[b], dy, TN)
            dci_ref[blk, :] = -_dot(si_ref[b], dy, TN)
            lr_ref[b] = _dot(dy, cr_ref[blk, :], NT)
            li_ref[b] = -_dot(dy, ci_ref[blk, :], NT)
        _s5_scan(lr_ref, li_ref, ar, ai, T, True)
        du = None
        head = lax.broadcasted_iota(jnp.int32, (T, 128), 0) < S5_SEG
        seg0 = lax.broadcasted_iota(jnp.int32, (S5_SEG, 128), 0) == 0
        for b in range(S5_NB):
            blk = slice(128 * b, 128 * (b + 1))
            lr, li = lr_ref[b], li_ref[b]
            pr = jnp.where(head, 0.0, pltpu.roll(sr_ref[b], S5_SEG, 0))
            pi = jnp.where(head, 0.0, pltpu.roll(si_ref[b], S5_SEG, 0))
            er = jnp.where(seg0, 0.0, pltpu.roll(sr_ref[b, T - S5_SEG:T, :], 1, 0))
            ei = jnp.where(seg0, 0.0, pltpu.roll(si_ref[b, T - S5_SEG:T, :], 1, 0))
            hr, hi = lr[0:S5_SEG], li[0:S5_SEG]
            dar_ref[:, blk] = _colsum(lr * pr + li * pi) + _colsum(hr * er + hi * ei)
            dai_ref[:, blk] = _colsum(li * pr - lr * pi) + _colsum(hi * er - hr * ei)
            dbr_ref[:, blk] = _dot(ut, lr, TN)
            dbi_ref[:, blk] = _dot(ut, li, TN)
            dub = _dot(lr, br_ref[:, blk], NT) + _dot(li, bi_ref[:, blk], NT)
            du = dub if du is None else du + dub

        @pl.when(pl.program_id(1) == 0)
        def _():
            du_ref[...] = du + sk_ref[...]

        @pl.when(pl.program_id(1) == 1)
        def _():
            du_ref[...] += du

    scr = pltpu.VMEM((S5_NB, T, 128), F32)
    return pl.pallas_call(
        body, name="s5_bwd", grid=(S5_TILES, 2),
        in_specs=[u_spec, u_spec, u_spec, b_spec, b_spec, c_spec, c_spec, a_spec, a_spec],
        out_specs=[u_spec, b_spec, b_spec, c_spec, c_spec, a_spec, a_spec],
        out_shape=[jax.ShapeDtypeStruct(u.shape, F32)] + [jax.ShapeDtypeStruct(t.shape, F32) for t in tiles],
        scratch_shapes=[scr, scr, scr, scr],
        compiler_params=_params(("parallel", "arbitrary")),
    )(u, dys, du_skip, *tiles)


CONV_TILE = 256


def _conv_pre(x, w, b):
    T = x.shape[0]
    row = lax.broadcasted_iota(jnp.int32, x.shape, 0)
    acc = x * w[CONV_K - 1:CONV_K, :] + b
    for lag in range(1, CONV_K):
        acc = acc + jnp.where(row >= lag, pltpu.roll(x, lag, 0), 0.0) * w[CONV_K - 1 - lag:CONV_K - lag, :]
    return acc


def _conv_fwd(x, w, b):
    T, C = x.shape
    col = pl.BlockSpec((T, CONV_TILE), lambda j: (0, j))

    def body(x_ref, w_ref, b_ref, o_ref):
        o_ref[...] = jax.nn.silu(_conv_pre(x_ref[...], w_ref[...], b_ref[...]))

    return pl.pallas_call(
        body, name="conv_fwd", grid=(C // CONV_TILE,),
        in_specs=[col, pl.BlockSpec((CONV_K, CONV_TILE), lambda j: (0, j)), pl.BlockSpec((1, CONV_TILE), lambda j: (0, j))],
        out_specs=col, out_shape=jax.ShapeDtypeStruct((T, C), F32),
        compiler_params=_params(("parallel",)),
    )(x, w, b)


def _conv_bwd(x, w, b, dout, col0, name):
    T, C = dout.shape
    off = col0 // CONV_TILE
    xcol = pl.BlockSpec((T, CONV_TILE), lambda j: (0, j + off))
    dcol = pl.BlockSpec((T, CONV_TILE), lambda j: (0, j))

    def body(x_ref, w_ref, b_ref, d_ref, dx_ref, dw_ref, db_ref):
        x, w = x_ref[...], w_ref[...]
        _, vjp = jax.vjp(jax.nn.silu, _conv_pre(x, w, b_ref[...]))
        (dy,) = vjp(d_ref[...])
        row = lax.broadcasted_iota(jnp.int32, x.shape, 0)
        dx = dy * w[CONV_K - 1:CONV_K, :]
        dw_ref[CONV_K - 1:CONV_K, :] = _colsum(dy * x)
        for lag in range(1, CONV_K):
            dx = dx + jnp.where(row < T - lag, pltpu.roll(dy, T - lag, 0), 0.0) * w[CONV_K - 1 - lag:CONV_K - lag, :]
            xs = jnp.where(row >= lag, pltpu.roll(x, lag, 0), 0.0)
            dw_ref[CONV_K - 1 - lag:CONV_K - lag, :] = _colsum(dy * xs)
        dx_ref[...] = dx.astype(dx_ref.dtype)
        db_ref[...] = _colsum(dy)

    return pl.pallas_call(
        body, name=name, grid=(C // CONV_TILE,),
        in_specs=[xcol, pl.BlockSpec((CONV_K, CONV_TILE), lambda j: (0, j + off)),
                  pl.BlockSpec((1, CONV_TILE), lambda j: (0, j + off)), dcol],
        out_specs=[dcol, pl.BlockSpec((CONV_K, CONV_TILE), lambda j: (0, j)), pl.BlockSpec((1, CONV_TILE), lambda j: (0, j))],
        out_shape=[jax.ShapeDtypeStruct((T, C), BF16), jax.ShapeDtypeStruct((CONV_K, C), F32), jax.ShapeDtypeStruct((1, C), F32)],
        compiler_params=_params(("parallel",)),
    )(x, w, b, dout)


def _ssd_common(x_ref, b_ref, c_ref, dtr_ref, bias_ref, alog_ref, tri_ref, triu_ref):
    Q = x_ref.shape[0]
    x, Bm, Cm = x_ref[...], b_ref[...], c_ref[...]
    pre = dtr_ref[...] + bias_ref[...]
    dt = _softplus(pre)
    A = -jnp.exp(alog_ref[...])
    adt = dt * A
    ac4 = _dot(tri_ref[...], adt, NN, HI)
    ar4 = _dot(adt, triu_ref[...], TN, HI)
    cb = _dot(Cm, Bm, NT)
    ii = lax.broadcasted_iota(jnp.int32, (Q, Q), 0)
    jj = lax.broadcasted_iota(jnp.int32, (Q, Q), 1)
    atot4 = ac4[Q - 1:Q, :]
    return x, Bm, Cm, pre, dt, A, ac4, ar4, cb, ii >= jj, atot4


def _ssd_decay(k, ac4, ar4, causal):
    seg = ac4[:, k:k + 1] - ar4[k:k + 1, :]
    return jnp.where(causal, jnp.exp(jnp.where(causal, seg, 0.0)), 0.0)


def _per_head(c4, n):
    lane = lax.broadcasted_iota(jnp.int32, (c4.shape[0], n), 1)
    out = jnp.broadcast_to(c4[:, 0:1], (c4.shape[0], n))
    for k in range(1, SSD_HPG):
        out = jnp.where(lane >= SSD_HEADDIM * k, c4[:, k:k + 1], out)
    return out


def _head_sums(v):
    lane = lax.broadcasted_iota(jnp.int32, (v.shape[0], SSD_HPG), 1)
    out = jnp.zeros((v.shape[0], SSD_HPG), F32)
    for k in range(SSD_HPG):
        s = jnp.sum(v[:, SSD_HEADDIM * k:SSD_HEADDIM * (k + 1)], axis=1, keepdims=True)
        out = jnp.where(lane == k, s, out)
    return out


def _head_rows(c4):
    n = SSD_HPG * SSD_HEADDIM
    row = lax.broadcasted_iota(jnp.int32, (n, 1), 0)
    out = jnp.broadcast_to(c4[:, 0:1], (n, 1))
    for k in range(1, SSD_HPG):
        out = jnp.where(row >= SSD_HEADDIM * k, c4[:, k:k + 1], out)
    return out


def _ssd_specs(T, Q, rev):
    NC = T // Q
    cc = (lambda c: NC - 1 - c) if rev else (lambda c: c)
    x_spec = pl.BlockSpec((Q, SSD_HPG * SSD_HEADDIM), lambda g, c: (cc(c), g))
    b_spec = pl.BlockSpec((Q, SSD_STATE), lambda g, c: (cc(c), D_INNER // SSD_STATE + g))
    c_spec = pl.BlockSpec((Q, SSD_STATE), lambda g, c: (cc(c), D_INNER // SSD_STATE + SSD_GROUPS + g))
    dt_spec = pl.BlockSpec((None, Q, SSD_HPG), lambda g, c: (g, cc(c), 0))
    p_spec = pl.BlockSpec((None, 1, SSD_HPG), lambda g, c: (g, 0, 0))
    tri_spec = pl.BlockSpec((Q, Q), lambda g, c: (0, 0))
    h_spec = pl.BlockSpec((None, None, SSD_HPG * SSD_HEADDIM, SSD_STATE), lambda g, c: (g, cc(c), 0, 0))
    return x_spec, b_spec, c_spec, dt_spec, p_spec, tri_spec, h_spec


def _tri(Q):
    tri = jnp.tril(jnp.ones((Q, Q), F32))
    return tri, tri.T


def _ssd_fwd(xc, dtr8, bias8, alog8, dsk8):
    T, Q = xc.shape[0], SSD_Q
    NC = T // Q
    P, W = SSD_HEADDIM, SSD_HPG * SSD_HEADDIM
    x_spec, b_spec, c_spec, dt_spec, p_spec, tri_spec, h_spec = _ssd_specs(T, Q, False)

    def body(x_ref, b_ref, c_ref, dtr_ref, bias_ref, alog_ref, dsk_ref, tri_ref, triu_ref, y_ref, hs_ref, h_scr):
        @pl.when(pl.program_id(1) == 0)
        def _():
            h_scr[...] = jnp.zeros_like(h_scr)

        hin = h_scr[...]
        hs_ref[...] = hin
        x, Bm, Cm, pre, dt, A, ac4, ar4, cb, causal, atot4 = _ssd_common(
            x_ref, b_ref, c_ref, dtr_ref, bias_ref, alog_ref, tri_ref, triu_ref)
        xdt = x * _per_head(dt, W)
        rest = _dot(Cm, hin, NT) * _per_head(jnp.exp(ac4), W) + _per_head(dsk_ref[...], W) * x
        for k in range(SSD_HPG):
            blk = slice(P * k, P * (k + 1))
            G = cb * _ssd_decay(k, ac4, ar4, causal)
            y_ref[:, blk] = _dot(G, xdt[:, blk], NN) + rest[:, blk]
        xw = xdt * _per_head(jnp.exp(atot4 - ac4), W)
        h_scr[...] = _head_rows(jnp.exp(atot4)) * hin + _dot(xw, Bm, TN)

    tri, triu = _tri(Q)
    return pl.pallas_call(
        body, name="ssd_fwd", grid=(SSD_GROUPS, NC),
        in_specs=[x_spec, b_spec, c_spec, dt_spec, p_spec, p_spec, p_spec, tri_spec, tri_spec],
        out_specs=[pl.BlockSpec((Q, W), lambda g, c: (c, g)), h_spec],
        out_shape=[jax.ShapeDtypeStruct((T, D_INNER), F32),
                   jax.ShapeDtypeStruct((SSD_GROUPS, NC, W, SSD_STATE), F32)],
        scratch_shapes=[pltpu.VMEM((W, SSD_STATE), F32)],
        compiler_params=_params(("parallel", "arbitrary")),
    )(xc, xc, xc, dtr8, bias8, alog8, dsk8, tri, triu)


def _ssd_bwd(xc, dtr8, bias8, alog8, dsk8, hs, dy):
    T, Q = xc.shape[0], SSD_Q
    NC = T // Q
    P, W = SSD_HEADDIM, SSD_HPG * SSD_HEADDIM
    x_spec, b_spec, c_spec, dt_spec, p_spec, tri_spec, h_spec = _ssd_specs(T, Q, True)
    dy_spec = pl.BlockSpec((Q, W), lambda g, c: (NC - 1 - c, g))
    dbc_spec = pl.BlockSpec((Q, SSD_STATE), lambda g, c: (NC - 1 - c, g))

    def body(x_ref, b_ref, c_ref, dtr_ref, bias_ref, alog_ref, dsk_ref, tri_ref, triu_ref, hs_ref, dy_ref,
             dx_ref, db_ref, dc_ref, ddtr_ref, dbias_ref, dalog_ref, ddsk_ref, dh_scr):
        @pl.when(pl.program_id(1) == 0)
        def _():
            dh_scr[...] = jnp.zeros_like(dh_scr)
            dbias_ref[...] = jnp.zeros_like(dbias_ref)
            dalog_ref[...] = jnp.zeros_like(dalog_ref)
            ddsk_ref[...] = jnp.zeros_like(ddsk_ref)

        x, Bm, Cm, pre, dt, A, ac4, ar4, cb, causal, atot4 = _ssd_common(
            x_ref, b_ref, c_ref, dtr_ref, bias_ref, alog_ref, tri_ref, triu_ref)
        dyv, hin, dho = dy_ref[...], hs_ref[...], dh_scr[...]
        dt_w = _per_head(dt, W)
        xdt = x * dt_w
        E4, F4, etot4 = jnp.exp(ac4), jnp.exp(atot4 - ac4), jnp.exp(atot4)
        F_w = _per_head(F4, W)
        ddsk_ref[...] += _head_sums(_colsum(dyv * x))
        Z = _dot(Cm, hin, NT)
        dZ = dyv * _per_head(E4, W)
        dac4 = _head_sums(dyv * Z) * E4
        dC = _dot(dZ, hin, NN)
        dh_scr[...] = _dot(dZ, Cm, TN) + _head_rows(etot4) * dho
        per_row = jnp.sum(dho * hin, axis=1, keepdims=True)
        lane4 = lax.broadcasted_iota(jnp.int32, (1, SSD_HPG), 1)
        datot4 = jnp.zeros((1, SSD_HPG), F32)
        for k in range(SSD_HPG):
            datot4 = jnp.where(lane4 == k, jnp.sum(per_row[P * k:P * (k + 1)], keepdims=True), datot4)
        datot4 = datot4 * etot4
        dxw = _dot(Bm, dho, NT)
        dB = _dot(xdt * F_w, dho, NN)
        dFa4 = _head_sums(dxw * xdt) * F4
        datot4 = datot4 + _colsum(dFa4)
        dac4 = dac4 - dFa4
        dcb = jnp.zeros((Q, Q), F32)
        lane_q4 = lax.broadcasted_iota(jnp.int32, (Q, SSD_HPG), 1)
        sub_4q = lax.broadcasted_iota(jnp.int32, (SSD_HPG, Q), 0)
        row_sums = jnp.zeros((Q, SSD_HPG), F32)
        col_sums = jnp.zeros((SSD_HPG, Q), F32)
        dxdt_heads = []
        for k in range(SSD_HPG):
            blk = slice(P * k, P * (k + 1))
            L = _ssd_decay(k, ac4, ar4, causal)
            G = cb * L
            dG = _dot(dyv[:, blk], xdt[:, blk], NT)
            dxdt_heads.append(_dot(G, dyv[:, blk], TN))
            dcb = dcb + dG * L
            Mseg = dG * G
            row_sums = jnp.where(lane_q4 == k, jnp.sum(Mseg, axis=1, keepdims=True), row_sums)
            col_sums = jnp.where(sub_4q == k, jnp.sum(Mseg, axis=0, keepdims=True), col_sums)
        dxdt = dxw * F_w + jnp.concatenate(dxdt_heads, axis=1)
        last = lax.broadcasted_iota(jnp.int32, (Q, SSD_HPG), 0) == Q - 1
        dac4 = dac4 + row_sums + jnp.where(last, datot4, 0.0)
        dadt4 = _dot(triu_ref[...], dac4, NN, HI) - _dot(triu_ref[...], col_sums, NT, HI)
        ddt4 = _head_sums(dxdt * x) + dadt4 * A
        dalog_ref[...] += _colsum(dadt4 * dt) * A
        ddtr4 = ddt4 * jax.nn.sigmoid(pre)
        dbias_ref[...] += _colsum(ddtr4)
        ddtr_ref[...] = ddtr4
        dx_ref[...] = (_per_head(dsk_ref[...], W) * dyv + dxdt * dt_w).astype(dx_ref.dtype)
        db_ref[...] = dB + _dot(dcb, Cm, TN)
        dc_ref[...] = dC + _dot(dcb, Bm, NN)

    tri, triu = _tri(Q)
    return pl.pallas_call(
        body, name="ssd_bwd", grid=(SSD_GROUPS, NC),
        in_specs=[x_spec, b_spec, c_spec, dt_spec, p_spec, p_spec, p_spec, tri_spec, tri_spec, h_spec, dy_spec],
        out_specs=[dy_spec, dbc_spec, dbc_spec, dt_spec, p_spec, p_spec, p_spec],
        out_shape=[jax.ShapeDtypeStruct((T, D_INNER), F32),
                   jax.ShapeDtypeStruct((T, SSD_GROUPS * SSD_STATE), F32),
                   jax.ShapeDtypeStruct((T, SSD_GROUPS * SSD_STATE), F32),
                   jax.ShapeDtypeStruct(dtr8.shape, F32)] + [jax.ShapeDtypeStruct(bias8.shape, F32)] * 3,
        scratch_shapes=[pltpu.VMEM((W, SSD_STATE), F32)],
        compiler_params=_params(("parallel", "arbitrary")),
    )(xc, xc, xc, dtr8, bias8, alog8, dsk8, tri, triu, hs, dy)


def _to_groups(v):
    return v.reshape(SSD_GROUPS, 1, SSD_HPG)


def _mixer_fwd(x, p, w):
    T, D = x.shape
    (hn,) = _rowwise(lambda xt, g: (_rms(xt, g),), [x], [p["mix_norm"]], [(D, BF16)], name="mix_norm")
    winT = w["w_in"]
    u, z, xbc, dtr, gl = [
        _matmul(hn, winT[IN_OFFS[i]:IN_OFFS[i + 1]], "nt", name=f"mix_in{i}") for i in range(5)]
    tiles = _s5_derive(*[p[k][0] for k in ("s5_A_re", "s5_A_im", "s5_log_dt", "s5_B_re", "s5_B_im", "s5_C_re", "s5_C_im")])
    u = _to_seg(u, "seg_u")
    ys = _s5_fwd(u, tiles)
    (g,) = _rowwise(lambda yt, ut, d: (jax.nn.gelu(yt + d * ut),), [ys, u], [p["s5_D"]], [(D, F32)], name="s5_gelu")
    y5 = _matmul(g, w["s5_w_glu"], "nn", name="s5_glu", out=(BF16,),
                 epilogue=lambda acc, gt, b: (gt * jax.nn.sigmoid(acc + b),), extras=[g, p["s5_b_glu"]])
    xc = _conv_fwd(xbc, w["conv_w"], p["conv_b"])
    dtr8 = dtr.reshape(T, SSD_GROUPS, SSD_HPG).transpose(1, 0, 2)
    ssd_p = [_to_groups(p[k]) for k in ("ssd_dt_bias", "ssd_A_log", "ssd_D")]
    yssd_raw, hs = _ssd_fwd(xc, dtr8, *ssd_p)
    (yssd,) = _rowwise(lambda yt, zt, nw: (_rms(yt * jax.nn.silu(zt), nw),), [yssd_raw, z], [p["ssd_norm"]],
                       [(D_INNER, BF16)], name="ssd_gate")
    p5 = _from_seg(_matmul(y5, w["w_proj_s5"], "nn", name="mix_p5"), "unseg_p5")
    pssd = _matmul(yssd, w["w_proj_ssd"], "nn", name="mix_pssd")

    def merge(glt, at, bt, bg):
        gates = jax.nn.sigmoid(glt + bg)
        return (gates[:, :D] * at + gates[:, D:] * bt,)

    (merged,) = _rowwise(merge, [gl, p5, pssd], [p["b_gate"]], [(D, BF16)], name="mix_merge")
    y = _matmul(merged, w["w_out"], "nn", name="mix_out", epilogue=lambda acc, xt: (xt + acc,), extras=[x])
    res = dict(hn=hn, u=u, z=z, xbc=xbc, gl=gl, tiles=tiles, ys=ys, g=g, y5=y5, xc=xc, dtr8=dtr8, ssd_p=ssd_p,
               yssd_raw=yssd_raw, hs=hs, yssd=yssd, p5=p5, pssd=pssd, merged=merged)
    return y, res


def _mixer_bwd(dy, x, p, w, r):
    T, D = x.shape
    gp = {}
    dmerged = _matmul(dy, w["w_out"], "nt", name="mix_d_merged")
    gp["w_out"] = _matmul(r["merged"], dy, "tn", name="mix_d_wout")

    def merge_bwd(glt, at, bt, dm, bg):
        def f(q, a_, b_):
            gates = jax.nn.sigmoid(q + bg)
            return gates[:, :D] * a_ + gates[:, D:] * b_
        _, vjp = jax.vjp(f, glt, at, bt)
        dq, da_, db_ = vjp(dm)
        return da_, db_, dq, _colsum(dq)

    dp5, dpssd, dgl, gp["b_gate"] = _rowwise(
        merge_bwd, [r["gl"], r["p5"], r["pssd"], dmerged], [p["b_gate"]],
        [(D, F32), (D, BF16), (2 * D, BF16)], [((1, 2 * D), F32)], name="mix_d_merge")
    dp5 = _to_seg(dp5, "seg_dp5")
    dy5 = _matmul(dp5, w["w_proj_s5"], "nt", name="mix_d_y5")
    gp["w_proj_s5"] = _matmul(r["y5"], dp5, "tn", name="mix_d_wp5")
    dyssd = _matmul(dpssd, w["w_proj_ssd"], "nt", name="mix_d_yssd")
    gp["w_proj_ssd"] = _matmul(r["yssd"], dpssd, "tn", name="mix_d_wpssd")

    def gate_bwd(yt, zt, dyt, nw):
        _, vjp = jax.vjp(lambda a_, b_, c_: _rms(a_ * jax.nn.silu(b_), c_), yt, zt, nw)
        return vjp(dyt)

    dyraw, dz, gp["ssd_norm"] = _rowwise(
        gate_bwd, [r["yssd_raw"], r["z"], dyssd], [p["ssd_norm"]],
        [(D_INNER, F32), (D_INNER, BF16)], [((1, D_INNER), F32)], name="ssd_d_gate")
    dxs, dBm, dCm, ddtr8, dbias8, dalog8, ddsk8 = _ssd_bwd(r["xc"], r["dtr8"], *r["ssd_p"], r["hs"], dyraw)
    gp["ssd_dt_bias"], gp["ssd_A_log"], gp["ssd_D"] = [v.reshape(1, SSD_HEADS) for v in (dbias8, dalog8, ddsk8)]
    ddtr = ddtr8.transpose(1, 0, 2).reshape(T, SSD_HEADS)
    conv = [_conv_bwd(r["xbc"], w["conv_w"], p["conv_b"], d, c0, f"conv_bwd{i}")
            for i, (d, c0) in enumerate(((dxs, 0), (dBm, D_INNER), (dCm, D_INNER + SSD_GROUPS * SSD_STATE)))]
    dxbc = [c[0] for c in conv]
    gp["conv_w"] = jnp.concatenate([c[1] for c in conv], axis=1)
    gp["conv_b"] = jnp.concatenate([c[2] for c in conv], axis=1)

    g = r["g"]

    def glu_ep(acc, gt, b, dyt):
        _, vjp = jax.vjp(lambda g_, t_: g_ * jax.nn.sigmoid(t_ + b), gt, acc)
        return vjp(dyt)

    dg1, dt_ = _matmul(g, w["s5_w_glu"], "nn", name="s5_d_glu", out=(F32, BF16), epilogue=glu_ep,
                       extras=[g, p["s5_b_glu"], dy5])
    gp["s5_w_glu"] = _matmul(g, dt_, "tn", name="s5_d_wglu")
    dg = _matmul(dt_, w["s5_w_glu"], "nt", name="s5_d_g", epilogue=lambda acc, e: (acc + e,), extras=[dg1])

    def gelu_bwd(yt, ut, dgt, dtt, d):
        _, vjp = jax.vjp(lambda y_, d_: jax.nn.gelu(y_ + d_ * ut), yt, d)
        dys, dd = vjp(dgt)
        return dys, dys * d, dd, _colsum(dtt.astype(F32))

    dys, dusk, gp["s5_D"], gp["s5_b_glu"] = _rowwise(
        gelu_bwd, [r["ys"], r["u"], dg, dt_], [p["s5_D"]], [(D, F32), (D, F32)], [((1, D), F32), ((1, D), F32)],
        name="s5_d_gelu")
    du, *dtiles = _s5_bwd(r["u"], dys, dusk, r["tiles"])
    du = _from_seg(du, "unseg_du")

    pieces = [du, dz, None, ddtr, dgl]
    winT = w["w_in"]
    hn = r["hn"]
    d_rows, dhn = [], None
    cols = [(du, 0, 1024), (dz, 1024, 3072), (dxbc[0], 3072, 5120), (dxbc[1], 5120, 6144), (dxbc[2], 6144, 7168),
            (ddtr, 7168, 7200), (dgl, 7200, 9248)]
    for i, (d, lo, hi) in enumerate(cols):
        d_rows.append(_matmul(d, hn, "tn", name=f"mix_d_win{i}"))
        if dhn is None:
            dhn = _matmul(d, winT[lo:hi], "nn", name=f"mix_d_hn{i}")
        else:
            dhn = _matmul(d, winT[lo:hi], "nn", name=f"mix_d_hn{i}", epilogue=lambda acc, e: (acc + e,), extras=[dhn])
    gp["w_in"] = jnp.concatenate(d_rows, axis=0)
    del pieces

    def norm_bwd(xt, dh, dyt, gn):
        _, vjp = jax.vjp(_rms, xt, gn)
        dx, dgn = vjp(dh)
        return dyt + dx, dgn

    dx, gp["mix_norm"] = _rowwise(norm_bwd, [x, dhn, dy], [p["mix_norm"]], [(D, F32)], [((1, D), F32)], name="mix_d_norm")
    return dx, gp, dtiles


def _s5_param_grads(p, dtiles):
    keys = ("s5_A_re", "s5_A_im", "s5_log_dt", "s5_B_re", "s5_B_im", "s5_C_re", "s5_C_im")
    _, vjp = jax.vjp(_s5_derive, *[p[k][0] for k in keys])
    return {k: v[None] for k, v in zip(keys, vjp(tuple(dtiles)))}


def _local_step(x, target, p, w):
    T, D = x.shape
    x1, r1 = _ffn_fwd(x, p["ffn1_norm"], w["ffn1_w_gate"], w["ffn1_w_up"], w["ffn1_w_down"], "ffn1")
    x2, rm = _mixer_fwd(x1, p, w)
    x3, r2 = _ffn_fwd(x2, p["ffn2_norm"], w["ffn2_w_gate"], w["ffn2_w_up"], w["ffn2_w_down"], "ffn2")

    def head(xt, tt, g):
        def f(x_, g_):
            e = _rms(x_, g_) - tt
            return 0.5 * jnp.sum(jnp.mean(e * e, axis=-1))
        l, (dx_, dg_) = jax.value_and_grad(f, argnums=(0, 1))(xt, g)
        return dx_, l.reshape(1, 1), dg_

    dx3, loss, d_final = _rowwise(head, [x3, target], [p["final_norm"]], [(D, F32)],
                                  [((1, 1), F32), ((1, D), F32)], name="loss_head")
    gp, gw = {"final_norm": d_final}, {}
    dx2, gp["ffn2_norm"], gw["ffn2_w_gate"], gw["ffn2_w_up"], gw["ffn2_w_down"] = _ffn_bwd(
        dx3, x2, p["ffn2_norm"], w["ffn2_w_gate"], w["ffn2_w_up"], w["ffn2_w_down"], r2, "ffn2")
    dx1, gm, dtiles = _mixer_bwd(dx2, x1, p, w, rm)
    for k in ("w_out", "w_proj_s5", "w_proj_ssd", "s5_w_glu", "w_in", "conv_w"):
        gw[k] = gm.pop(k)
    gp.update(gm)
    gp.update(_s5_param_grads(p, dtiles))
    dx0, gp["ffn1_norm"], gw["ffn1_w_gate"], gw["ffn1_w_up"], gw["ffn1_w_down"] = _ffn_bwd(
        dx1, x, p["ffn1_norm"], w["ffn1_w_gate"], w["ffn1_w_up"], w["ffn1_w_down"], r1, "ffn1")
    return loss, dx0, gp, gw


def _adamw(w, m, v, gs, *, name, emit_g, tm=128):
    n_g = len(gs)
    C = w.shape[1]

    def fn(wt, mt, vt, *gt):
        g = gt[0]
        for e in gt[1:]:
            g = g + e
        m2 = ADAM_B1 * mt + (1.0 - ADAM_B1) * g
        v2 = ADAM_B2 * vt + (1.0 - ADAM_B2) * (g * g)
        m_hat = m2 / (1.0 - ADAM_B1 ** ADAM_STEP)
        v_hat = v2 / (1.0 - ADAM_B2 ** ADAM_STEP)
        delta = -ADAM_LR * (m_hat / (jnp.sqrt(v_hat) + ADAM_EPS) + ADAM_WD * wt)
        return ((g,) if emit_g else ()) + (delta, m2, v2)

    assert n_g >= 1
    return _rowwise(fn, [w, m, v, *gs], [], [(C, F32)] * (4 if emit_g else 3), name=name, tm=_rtile(w.shape[0], tm))


def _rtile(r, cap=512):
    if r <= cap:
        return r
    best = None
    for t in range(8, cap + 1, 8):
        if r % t == 0:
            best = t
    assert best is not None, r
    return best


ANY = pl.BlockSpec(memory_space=pl.ANY)


def _where_am_i():
    x, y, c = lax.axis_index("x"), lax.axis_index("y"), lax.axis_index("c")
    ks = (2 * x + y, 2 * (1 - x) + y, 2 * x + (1 - y), 2 * (1 - x) + (1 - y))
    return x, y, c, ks


def _rc(src, dst, ssem, rsem, to):
    return pltpu.make_async_remote_copy(src_ref=src, dst_ref=dst, send_sem=ssem, recv_sem=rsem,
                                        device_id=to, device_id_type=MESH)


def _gather_shards(srcs):
    n = len(srcs)

    def body(*refs):
        _gather_body(refs[:n], refs[n:2 * n], *refs[2 * n:], extra=())

    return pl.pallas_call(
        body, name="gather_shards",
        in_specs=[ANY] * n, out_specs=[ANY] * n,
        out_shape=[jax.ShapeDtypeStruct((4,) + s.shape, s.dtype) for s in srcs],
        scratch_shapes=[pltpu.SemaphoreType.DMA((n, 6)), pltpu.SemaphoreType.DMA((n, 6)), pltpu.SemaphoreType.DMA((n,))],
        compiler_params=pltpu.CompilerParams(has_side_effects=True),
    )(*srcs)


def _handshake_all():
    x, y, c = lax.axis_index("x"), lax.axis_index("y"), lax.axis_index("c")
    barrier = pltpu.get_barrier_semaphore()
    for dx, dy, dc in itertools.product((0, 1), repeat=3):
        if (dx, dy, dc) != (0, 0, 0):
            to = (1 - x if dx else x, 1 - y if dy else y, 1 - c if dc else c)
            pl.semaphore_signal(barrier, inc=1, device_id=to, device_id_type=MESH)
    pl.semaphore_wait(barrier, 7)


SEQUENCER = dict(axis_name="seq", num_cores=1)


def _gather_shards_async(srcs, token, name, collective_id):
    n = len(srcs)

    def body(*refs):
        tok, tok_out = refs[n], refs[2 * n + 1]
        ssem, rsem, lsem = refs[2 * n + 2:]
        _handshake_all()
        _gather_body(refs[:n], refs[n + 1:2 * n + 1], ssem, rsem, lsem,
                     extra=(pltpu.make_async_copy(tok, tok_out, lsem.at[n]),))

    res = pl.kernel(
        body, name=name,
        out_type=[jax.ShapeDtypeStruct((4,) + s.shape, s.dtype) for s in srcs]
        + [jax.ShapeDtypeStruct(token.shape, token.dtype)],
        mesh=plsc.ScalarSubcoreMesh(**SEQUENCER),
        scratch_types=[pltpu.SemaphoreType.DMA((n, 6)), pltpu.SemaphoreType.DMA((n, 6)), pltpu.SemaphoreType.DMA((n + 1,))],
        compiler_params=pltpu.CompilerParams(collective_id=collective_id),
    )(*srcs, token)
    return res[:n], res[n]


def _gather_body(src, out, ssem, rsem, lsem, extra):
        n = len(src)
        x, y, c, (k_me, k_x, k_y, k_d) = _where_am_i()
        sib = (x, y, 1 - c)
        local = [pltpu.make_async_copy(src[i], out[i].at[k_me], lsem.at[i]) for i in range(n)] + list(extra)
        for cp in local:
            cp.start()

        def own(i, q, to):
            return _rc(src[i], out[i].at[k_me], ssem.at[i, q], rsem.at[i, q], to)

        def slab(i, q, k, to):
            return _rc(out[i].at[k], out[i].at[k], ssem.at[i, q], rsem.at[i, q], to)

        @pl.when(c == 1)
        def _():
            sends = [own(i, 0, (1 - x, y, 1)) for i in range(n)]
            for cp in sends:
                cp.start()
            fwd = [slab(i, 3, k_x, sib) for i in range(n)]
            for i in range(n):
                slab(i, 0, k_x, sib).wait_recv()
                fwd[i].start()
            for i in range(n):
                slab(i, 4, k_y, sib).wait_recv()
                slab(i, 5, k_d, sib).wait_recv()
            for cp in sends + fwd:
                cp.wait_send()

        @pl.when(c == 0)
        def _():
            sends = [own(i, 1, (x, 1 - y, 0)) for i in range(n)] + [own(i, 2, (1 - x, 1 - y, 0)) for i in range(n)]
            for cp in sends:
                cp.start()
            fwd = [slab(i, 4, k_y, sib) for i in range(n)] + [slab(i, 5, k_d, sib) for i in range(n)]
            for i in range(n):
                slab(i, 1, k_y, sib).wait_recv()
                fwd[i].start()
            for i in range(n):
                slab(i, 2, k_d, sib).wait_recv()
                fwd[n + i].start()
            for i in range(n):
                slab(i, 3, k_x, sib).wait_recv()
            for cp in sends + fwd:
                cp.wait_send()

        for cp in local:
            cp.wait()


def _swap_slabs(arrs, n_slabs, name):
    n = len(arrs)
    J = max(n_slabs, 1)

    def body(*refs):
        src, out = refs[:n], refs[n:2 * n]
        ssem, rsem = refs[2 * n:]
        x, y, c, (k_me, k_x, k_y, k_d) = _where_am_i()
        sib = (x, y, 1 - c)
        sel = (jnp.where(c == 1, k_y, k_me), jnp.where(c == 1, k_d, k_x))
        cps = []
        for i in range(n):
            for j in range(J):
                s = src[i].at[sel[j]] if n_slabs else src[i]
                d = out[i].at[j] if n_slabs else out[i]
                cps.append(_rc(s, d, ssem.at[i, j], rsem.at[i, j], sib))
        for cp in cps:
            cp.start()
        for cp in cps:
            cp.wait()

    return pl.pallas_call(
        body, name=name, in_specs=[ANY] * n, out_specs=[ANY] * n,
        out_shape=[jax.ShapeDtypeStruct(((n_slabs,) if n_slabs else ()) + a.shape[-2:], a.dtype) for a in arrs],
        scratch_shapes=[pltpu.SemaphoreType.DMA((n, J)), pltpu.SemaphoreType.DMA((n, J))],
        compiler_params=pltpu.CompilerParams(has_side_effects=True),
    )(*arrs)


def _send_chip_sums(arrs):
    n = len(arrs)

    def body(*refs):
        src, out = refs[:n], refs[n:2 * n]
        ssem, rsem = refs[2 * n:]
        x, y, c, _ = _where_am_i()

        @pl.when(c == 1)
        def _():
            cps = [_rc(src[i].at[1], out[i].at[0], ssem.at[i, 0], rsem.at[i, 0], (1 - x, y, 1)) for i in range(n)]
            for cp in cps:
                cp.start()
            for cp in cps:
                cp.wait()

        @pl.when(c == 0)
        def _():
            cps = [_rc(src[i].at[0], out[i].at[0], ssem.at[i, 0], rsem.at[i, 0], (x, 1 - y, 0)) for i in range(n)]
            cps += [_rc(src[i].at[1], out[i].at[1], ssem.at[i, 1], rsem.at[i, 1], (1 - x, 1 - y, 0)) for i in range(n)]
            for cp in cps:
                cp.start()
            for cp in cps:
                cp.wait()

    return pl.pallas_call(
        body, name="send_chip_sums", in_specs=[ANY] * n, out_specs=[ANY] * n,
        out_shape=[jax.ShapeDtypeStruct(a.shape, a.dtype) for a in arrs],
        scratch_shapes=[pltpu.SemaphoreType.DMA((n, 2)), pltpu.SemaphoreType.DMA((n, 2))],
        compiler_params=pltpu.CompilerParams(has_side_effects=True),
    )(*arrs)


def _chip_sum(g, recv, sel, name):
    _, r, C = g.shape
    tr = _rtile(r)

    def body(sel_ref, g_ref, r_ref, o32_ref, o16_ref):
        s = g_ref[...] + r_ref[...]
        o32_ref[...] = s
        o16_ref[...] = s.astype(BF16)

    blk = pl.BlockSpec((None, tr, C), lambda j, t, sel_ref: (j, t, 0))
    return pl.pallas_call(
        body, name=name,
        grid_spec=pltpu.PrefetchScalarGridSpec(
            num_scalar_prefetch=1, grid=(2, r // tr),
            in_specs=[pl.BlockSpec((None, tr, C), lambda j, t, sel_ref: (sel_ref[j], t, 0)), blk],
            out_specs=[blk, blk]),
        out_shape=[jax.ShapeDtypeStruct((2, r, C), F32), jax.ShapeDtypeStruct((2, r, C), BF16)],
        compiler_params=_params(("arbitrary", "arbitrary")),
    )(sel, g, recv)


def _cross_sum(s32, got, north, name):
    _, r, C = s32.shape
    tr = _rtile(r)

    def body(f_ref, p_ref, a_ref, b_ref, o_ref):
        a = a_ref[...].astype(F32)

        @pl.when(f_ref[0] == 1)
        def _():
            o_ref[...] = p_ref[...] + a

        @pl.when(f_ref[0] == 0)
        def _():
            o_ref[...] = a + b_ref[...].astype(F32)

    return pl.pallas_call(
        body, name=name,
        grid_spec=pltpu.PrefetchScalarGridSpec(
            num_scalar_prefetch=1, grid=(r // tr,),
            in_specs=[pl.BlockSpec((None, tr, C), lambda t, f: (0, t, 0)),
                      pl.BlockSpec((None, tr, C), lambda t, f: (0, t, 0)),
                      pl.BlockSpec((None, tr, C), lambda t, f: (1 - f[0], t, 0))],
            out_specs=pl.BlockSpec((tr, C), lambda t, f: (t, 0))),
        out_shape=jax.ShapeDtypeStruct((r, C), F32),
        compiler_params=_params(("arbitrary",)),
    )(north, s32, got, got)


def _reduce_scatter(grads):
    x, y, c, (k_me, k_x, k_y, k_d) = _where_am_i()
    sel = jnp.stack([jnp.where(c == 1, k_me, k_y), jnp.where(c == 1, k_x, k_d)]).astype(jnp.int32)
    north = jnp.reshape(c, (1,)).astype(jnp.int32)
    from_sib = _swap_slabs(grads, 2, "swap_grad_slabs")
    sums = [_chip_sum(g, r, sel, f"chip_sum{i}") for i, (g, r) in enumerate(zip(grads, from_sib))]
    got = _send_chip_sums([s16 for _, s16 in sums])
    parts = [_cross_sum(s32, b, north, f"cross_sum{i}") for i, ((s32, _), b) in enumerate(zip(sums, got))]
    theirs = _swap_slabs(parts, 0, "swap_parts")
    return list(zip(parts, theirs))


def _all_reduce_small(v):
    R, C = v.shape

    def body(v_ref, o_ref, buf, ssem, rsem, lsem):
        x, y, c = lax.axis_index("x"), lax.axis_index("y"), lax.axis_index("c")
        me, sib = (x, y, c), (x, y, 1 - c)
        chips = [(1 - x, y), (x, 1 - y), (1 - x, 1 - y)]

        def slot(px, py, pc):
            return buf.at[4 * px + 2 * py + pc]

        def copy(k, block, to, src=None):
            return _rc(slot(*block) if src is None else src, slot(*block), ssem.at[k], rsem.at[k], to)

        mine = pltpu.make_async_copy(v_ref, slot(*me), lsem)
        mine.start()
        first = [copy(0, me, sib, src=v_ref)] + [copy(1 + j, me, (*chip, c), src=v_ref) for j, chip in enumerate(chips)]
        for cp in first:
            cp.start()
        passed = [copy(4 + j, (*chip, c), sib) for j, chip in enumerate(chips)]
        for j, chip in enumerate(chips):
            copy(1 + j, (*chip, c), me).wait_recv()
            passed[j].start()
        copy(0, sib, me).wait_recv()
        for j, chip in enumerate(chips):
            copy(4 + j, (*chip, 1 - c), me).wait_recv()
        for cp in first + passed:
            cp.wait_send()
        mine.wait()
        acc = buf[0]
        for k in range(1, 8):
            acc = acc + buf[k]
        o_ref[...] = acc

    return pl.pallas_call(
        body, name="all_reduce_small",
        in_specs=[pl.BlockSpec(memory_space=pltpu.VMEM)], out_specs=pl.BlockSpec(memory_space=pltpu.VMEM),
        out_shape=jax.ShapeDtypeStruct((R, C), F32),
        scratch_shapes=[pltpu.VMEM((8, R, C), F32), pltpu.SemaphoreType.DMA((7,)), pltpu.SemaphoreType.DMA((7,)),
                        pltpu.SemaphoreType.DMA],
        compiler_params=pltpu.CompilerParams(has_side_effects=True, vmem_limit_bytes=VMEM_LIMIT),
    )(v)


WEIGHTS = ['ffn1_norm', 'ffn1_w_gate', 'ffn1_w_up', 'ffn1_w_down', 'mix_norm', 'w_in', 'conv_w', 'conv_b', 's5_A_re',
           's5_A_im', 's5_log_dt', 's5_B_re', 's5_B_im', 's5_C_re', 's5_C_im', 's5_D', 's5_w_glu', 's5_b_glu',
           'ssd_A_log', 'ssd_dt_bias', 'ssd_D', 'ssd_norm', 'w_proj_s5', 'w_proj_ssd', 'b_gate', 'w_out', 'ffn2_norm',
           'ffn2_w_gate', 'ffn2_w_up', 'ffn2_w_down', 'final_norm']
ARGS = ['x'] + WEIGHTS + ['loss_target'] + ['m_' + n for n in WEIGHTS] + ['v_' + n for n in WEIGHTS]
COL_SHARDED = ('ffn1_w_gate', 'ffn1_w_up', 'w_in', 'ffn2_w_gate', 'ffn2_w_up')
ROW_SHARDED = ('ffn1_w_down', 's5_w_glu', 'w_proj_s5', 'w_proj_ssd', 'w_out', 'ffn2_w_down')
MATRICES = COL_SHARDED + ROW_SHARDED
GATHER_FIRST = ('ffn1_w_gate', 'ffn1_w_up', 'ffn1_w_down')
GATHER_SECOND = ('w_in', 'conv_w', 's5_w_glu', 'w_proj_s5', 'w_proj_ssd', 'w_out')
GATHER_THIRD = ('ffn2_w_gate', 'ffn2_w_up', 'ffn2_w_down')
SMALL = [n for n in WEIGHTS if n not in MATRICES]


def _pack(arrs, width=1024):
    flat = jnp.concatenate([a.reshape(-1).astype(F32) for a in arrs])
    rows = -(-flat.shape[0] // (8 * width)) * 8
    return jnp.pad(flat, (0, rows * width - flat.shape[0])).reshape(rows, width)


def _unpack(packed, shapes):
    flat, out, o = packed.reshape(-1), [], 0
    for s in shapes:
        n = math.prod(s)
        out.append(flat[o:o + n].reshape(s))
        o += n
    return out


def kernel(x, ffn1_norm, ffn1_w_gate, ffn1_w_up, ffn1_w_down, mix_norm, w_in, conv_w, conv_b, s5_A_re, s5_A_im, s5_log_dt, s5_B_re, s5_B_im, s5_C_re, s5_C_im, s5_D, s5_w_glu, s5_b_glu, ssd_A_log, ssd_dt_bias, ssd_D, ssd_norm, w_proj_s5, w_proj_ssd, b_gate, w_out, ffn2_norm, ffn2_w_gate, ffn2_w_up, ffn2_w_down, final_norm, loss_target, m_ffn1_norm, m_ffn1_w_gate, m_ffn1_w_up, m_ffn1_w_down, m_mix_norm, m_w_in, m_conv_w, m_conv_b, m_s5_A_re, m_s5_A_im, m_s5_log_dt, m_s5_B_re, m_s5_B_im, m_s5_C_re, m_s5_C_im, m_s5_D, m_s5_w_glu, m_s5_b_glu, m_ssd_A_log, m_ssd_dt_bias, m_ssd_D, m_ssd_norm, m_w_proj_s5, m_w_proj_ssd, m_b_gate, m_w_out, m_ffn2_norm, m_ffn2_w_gate, m_ffn2_w_up, m_ffn2_w_down, m_final_norm, v_ffn1_norm, v_ffn1_w_gate, v_ffn1_w_up, v_ffn1_w_down, v_mix_norm, v_w_in, v_conv_w, v_conv_b, v_s5_A_re, v_s5_A_im, v_s5_log_dt, v_s5_B_re, v_s5_B_im, v_s5_C_re, v_s5_C_im, v_s5_D, v_s5_w_glu, v_s5_b_glu, v_ssd_A_log, v_ssd_dt_bias, v_ssd_D, v_ssd_norm, v_w_proj_s5, v_w_proj_ssd, v_b_gate, v_w_out, v_ffn2_norm, v_ffn2_w_gate, v_ffn2_w_up, v_ffn2_w_down, v_final_norm):
    a = dict(locals())
    assert list(a) == ARGS
    x, target = a['x'][0], a['loss_target'][0]
    k_me = 2 * lax.axis_index("x") + lax.axis_index("y")

    src = {n: a[n][0].T.astype(BF16) for n in COL_SHARDED}
    src.update({n: a[n][0].astype(BF16) for n in ROW_SHARDED})
    src['conv_w'] = a['conv_w'][0]
    full = dict(zip(GATHER_FIRST, _gather_shards([src[n] for n in GATHER_FIRST])))
    token = full[GATHER_FIRST[0]][0, :8, :128].astype(F32)
    got, token = _gather_shards_async([src[n] for n in GATHER_SECOND], token, "gather_mixer", 1)
    full.update(zip(GATHER_SECOND, got))
    got, token = _gather_shards_async([src[n] for n in GATHER_THIRD], token, "gather_ffn2", 2)
    full.update(zip(GATHER_THIRD, got))
    w = {n: full[n].reshape(-1, full[n].shape[-1]) for n in MATRICES}
    w['conv_w'] = full['conv_w'].transpose(1, 0, 2).reshape(CONV_K, CONV_DIM)
    p = {n: a[n] for n in SMALL if n != 'conv_w'}
    p['final_norm'] = a['final_norm'][None]

    loss, grad_x, gp, gw = _local_step(x, target, p, w)

    pairs = _reduce_scatter([gw[n].reshape(4, -1, D_MODEL) for n in MATRICES])
    gp['conv_w'] = gw['conv_w'][None]
    small_shapes = [(1, 1)] + [gp[n].shape if n != 'final_norm' else (1, D_MODEL) for n in SMALL]
    red = _unpack(_all_reduce_small(_pack([loss] + [gp[n] for n in SMALL])), small_shapes)
    loss_all = red[0].reshape(())
    gsmall = dict(zip(SMALL, red[1:]))
    gsmall['conv_w'] = lax.dynamic_slice_in_dim(gsmall['conv_w'], k_me * 1024, 1024, axis=2)
    gsmall = {n: g.reshape(a[n].shape) for n, g in gsmall.items()}

    grads, delta, new_m, new_v = {}, {}, {}, {}
    for n, (mine, theirs) in zip(MATRICES, pairs):
        wn, mn, vn = a[n][0], a['m_' + n][0], a['v_' + n][0]
        if n in COL_SHARDED:
            (gt,) = _rowwise(lambda p_, q_: (p_ + q_,), [mine, theirs], [], [(D_MODEL, F32)], name=f"sum_{n}",
                             tm=_rtile(mine.shape[0]))
            g = gt.T
            d, m2, v2 = _adamw(wn, mn, vn, [g], name=f"adamw_{n}", emit_g=False)
        else:
            g, d, m2, v2 = _adamw(wn, mn, vn, [mine, theirs], name=f"adamw_{n}", emit_g=True)
        grads[n], delta[n], new_m[n], new_v[n] = g[None], d[None], m2[None], v2[None]
    sw, sm, sv, sg = [_pack([t[n] for n in SMALL]) for t in (a, {n: a['m_' + n] for n in SMALL},
                                                             {n: a['v_' + n] for n in SMALL}, gsmall)]
    d, m2, v2 = _adamw(sw, sm, sv, [sg], name="adamw_small", emit_g=False)
    shapes = [a[n].shape for n in SMALL]
    for n, dd, mm, vv in zip(SMALL, _unpack(d, shapes), _unpack(m2, shapes), _unpack(v2, shapes)):
        grads[n], delta[n], new_m[n], new_v[n] = gsmall[n], dd, mm, vv
    return (loss_all, grad_x[None], *[grads[n] for n in WEIGHTS], *[delta[n] for n in WEIGHTS],
            *[new_m[n] for n in WEIGHTS], *[new_v[n] for n in WEIGHTS])
```

```python
import itertools
import math

import jax
import jax.numpy as jnp
from jax import lax
from jax.experimental import pallas as pl
from jax.experimental.pallas import tpu as pltpu
from jax.experimental.pallas import tpu_sc as plsc

F32 = jnp.float32
BF16 = jnp.bfloat16
MXU = BF16
HI = lax.Precision.HIGHEST

D_MODEL = 1024
D_FF = 2816
EPS = 1e-6
S5_GROUPS, S5_GROUP, S5_STATE = 64, 16, 64
D_INNER = 2048
SSD_HEADDIM, SSD_HEADS, SSD_GROUPS, SSD_HPG, SSD_STATE = 64, 32, 8, 4, 128
CONV_K, CONV_DIM = 4, 4096
IN_SPLITS = (1024, 2048, 4096, 32, 2048)
IN_OFFS = (0, 1024, 3072, 7168, 7200, 9248)
SSD_Q = 256
ADAM_LR, ADAM_B1, ADAM_B2, ADAM_EPS, ADAM_WD, ADAM_STEP = 0.001, 0.9, 0.999, 1e-08, 0.01, 10

VMEM_LIMIT = 56 * 1024 * 1024
MATMUL_VMEM = 32 * 1024 * 1024
MESH = pl.DeviceIdType.MESH

NN = ((1,), (0,))
NT = ((1,), (1,))
TN = ((0,), (0,))


def _dot(a, b, dims, precision=None):
    if precision is None:
        a, b = a.astype(MXU), b.astype(MXU)
    return lax.dot_general(a, b, (dims, ((), ())), precision=precision, preferred_element_type=F32)


def _tile(n, pref):
    if n <= pref:
        return n
    best = None
    for t in range(128, pref + 1, 128):
        if n % t == 0:
            best = t
    assert best is not None, (n, pref)
    return best


def _params(sem):
    return pltpu.CompilerParams(dimension_semantics=sem, vmem_limit_bytes=VMEM_LIMIT)


def _matmul(a, b, mode, *, name, out=((F32),), epilogue=None, extras=(), tm=512, tn=1536):
    if mode == "nn":
        (M, K), (_, N) = a.shape, b.shape
    elif mode == "nt":
        (M, K), (N, _) = a.shape, b.shape
    else:
        (K, M), (_, N) = a.shape, b.shape
    tm, tn = _tile(M, tm), _tile(N, tn)

    def vmem(tm_, tn_):
        per = K * tm_ * a.dtype.itemsize + K * tn_ * b.dtype.itemsize
        per += sum(tm_ * tn_ * jnp.dtype(dt).itemsize for dt in out)
        per += sum((1 if e.shape[0] == 1 else tm_) * tn_ * e.dtype.itemsize for e in extras)
        return 2 * per

    while vmem(tm, tn) > MATMUL_VMEM and (tn > 128 or tm > 128):
        if tn >= tm and tn > 128:
            tn = _tile(N, tn - 128)
        else:
            tm = _tile(M, tm - 128)
    a_spec = pl.BlockSpec((K, tm), lambda i, j: (0, i)) if mode == "tn" else pl.BlockSpec((tm, K), lambda i, j: (i, 0))
    b_spec = pl.BlockSpec((tn, K), lambda i, j: (j, 0)) if mode == "nt" else pl.BlockSpec((K, tn), lambda i, j: (0, j))
    dims = {"nn": NN, "nt": NT, "tn": TN}[mode]
    e_specs = []
    for e in extras:
        if e.shape[0] == 1:
            e_specs.append(pl.BlockSpec((1, tn), lambda i, j: (0, j)))
        else:
            assert e.shape == (M, N), (e.shape, M, N)
            e_specs.append(pl.BlockSpec((tm, tn), lambda i, j: (i, j)))
    n_e, n_o = len(extras), len(out)

    def body(a_ref, b_ref, *refs):
        acc = _dot(a_ref[...], b_ref[...], dims)
        res = (acc,) if epilogue is None else epilogue(acc, *[r[...] for r in refs[:n_e]])
        for r, v in zip(refs[n_e:], res):
            r[...] = v.astype(r.dtype)

    res = pl.pallas_call(
        body, name=name, grid=(M // tm, N // tn),
        in_specs=[a_spec, b_spec] + e_specs,
        out_specs=[pl.BlockSpec((tm, tn), lambda i, j: (i, j)) for _ in range(n_o)],
        out_shape=[jax.ShapeDtypeStruct((M, N), dt) for dt in out],
        compiler_params=_params(("parallel", "parallel")),
    )(a, b, *extras)
    return res[0] if n_o == 1 else res


def _rowwise(fn, rows, fulls, outs, reds=(), *, name, tm=256):
    T = rows[0].shape[0]
    tm = min(tm, T)
    n_r, n_f, n_o, n_d = len(rows), len(fulls), len(outs), len(reds)

    def body(*refs):
        ins = [r[...] for r in refs[:n_r + n_f]]
        o_refs = refs[n_r + n_f:n_r + n_f + n_o]
        d_refs = refs[n_r + n_f + n_o:]
        res = fn(*ins)
        for r, v in zip(o_refs, res[:n_o]):
            r[...] = v.astype(r.dtype)
        if n_d:
            @pl.when(pl.program_id(0) == 0)
            def _():
                for r in d_refs:
                    r[...] = jnp.zeros_like(r)
            for r, v in zip(d_refs, res[n_o:]):
                r[...] += v.astype(r.dtype)

    res = pl.pallas_call(
        body, name=name, grid=(T // tm,),
        in_specs=[pl.BlockSpec((tm, r.shape[1]), lambda i: (i, 0)) for r in rows]
        + [pl.BlockSpec(f.shape, lambda i, nd=f.ndim: (0,) * nd) for f in fulls],
        out_specs=[pl.BlockSpec((tm, c), lambda i: (i, 0)) for c, _ in outs]
        + [pl.BlockSpec(s, lambda i, nd=len(s): (0,) * nd) for s, _ in reds],
        out_shape=[jax.ShapeDtypeStruct((T, c), dt) for c, dt in outs]
        + [jax.ShapeDtypeStruct(s, dt) for s, dt in reds],
        compiler_params=_params(("arbitrary",)),
    )(*rows, *fulls)
    return res


def _rms(x, g):
    return x * lax.rsqrt(jnp.mean(x * x, axis=-1, keepdims=True) + EPS) * g


def _colsum(v):
    return jnp.sum(v, axis=0, keepdims=True)


def _softplus(x):
    return jnp.maximum(x, 0.0) + jnp.log1p(jnp.exp(-jnp.abs(x)))


def _ffn_fwd(x, norm, wgT, wuT, wd, tag):
    D = x.shape[1]
    (hn,) = _rowwise(lambda xt, g: (_rms(xt, g),), [x], [norm], [(D, BF16)], name=f"{tag}_norm")
    a = _matmul(hn, wgT, "nt", name=f"{tag}_gate")
    b = _matmul(hn, wuT, "nt", name=f"{tag}_up")
    (hid,) = _rowwise(lambda at, bt: (jax.nn.silu(at) * bt,), [a, b], [], [(a.shape[1], BF16)], name=f"{tag}_act")
    y = _matmul(hid, wd, "nn", name=f"{tag}_down", epilogue=lambda acc, xt: (xt + 0.5 * acc,), extras=[x])
    return y, (hn, a, b, hid)


def _ffn_bwd(dy, x, norm, wgT, wuT, wd, res, tag):
    hn, a, b, hid = res
    D, F = x.shape[1], a.shape[1]
    dhid = _matmul(dy, wd, "nt", name=f"{tag}_d_hid", epilogue=lambda acc: (0.5 * acc,))
    d_wd = _matmul(hid, dy, "tn", name=f"{tag}_d_wd", epilogue=lambda acc: (0.5 * acc,))

    def act_bwd(at, bt, dh):
        _, vjp = jax.vjp(lambda p, q: jax.nn.silu(p) * q, at, bt)
        return vjp(dh)

    da, db = _rowwise(act_bwd, [a, b, dhid], [], [(F, BF16), (F, BF16)], name=f"{tag}_d_act")
    d_wgT = _matmul(da, hn, "tn", name=f"{tag}_d_wg")
    d_wuT = _matmul(db, hn, "tn", name=f"{tag}_d_wu")
    dhn = _matmul(da, wgT, "nn", name=f"{tag}_d_hn1")
    dhn = _matmul(db, wuT, "nn", name=f"{tag}_d_hn2", epilogue=lambda acc, e: (acc + e,), extras=[dhn])

    def norm_bwd(xt, dh, dyt, g):
        _, vjp = jax.vjp(_rms, xt, g)
        dx, dg = vjp(dh)
        return dyt + dx, dg

    dx, d_norm = _rowwise(norm_bwd, [x, dhn, dy], [norm], [(D, F32)], [((1, D), F32)], name=f"{tag}_d_norm")
    return dx, d_norm, d_wgT, d_wuT, d_wd


S5_TILES = 8
S5_HALF = 256


def _s5_derive(A_re, A_im, log_dt, B_re, B_im, C_re, C_im):
    G, N, M = S5_GROUPS, S5_STATE, S5_GROUP
    dt = jnp.exp(log_dt)[:, None]
    mag = jnp.exp(A_re * dt)
    ar, ai = mag * jnp.cos(A_im * dt), mag * jnp.sin(A_im * dt)
    den = A_re * A_re + A_im * A_im
    cr = ((ar - 1.0) * A_re + ai * A_im) / den
    ci = (ai * A_re - (ar - 1.0) * A_im) / den
    bbr = cr[..., None] * B_re - ci[..., None] * B_im
    bbi = cr[..., None] * B_im + ci[..., None] * B_re
    eye = jnp.eye(8, dtype=F32)

    def tile_in(bb):
        t = bb.reshape(S5_TILES, 8, N, M).transpose(0, 1, 3, 2)
        return jnp.einsum("jamn,ab->jambn", t, eye).reshape(S5_TILES, 8 * M, 8 * N)

    def tile_out(c):
        t = c.reshape(S5_TILES, 8, M, N).transpose(0, 1, 3, 2)
        return jnp.einsum("janm,ab->janbm", t, eye).reshape(S5_TILES, 8 * N, 8 * M)

    return (tile_in(bbr), tile_in(bbi), tile_out(C_re), tile_out(C_im),
            ar.reshape(S5_TILES, 1, 8 * N), ai.reshape(S5_TILES, 1, 8 * N))


S5_NB = S5_HALF // 128
S5_SEG = 32


def _cmul(ar, ai, br, bi):
    return ar * br - ai * bi, ar * bi + ai * br


def _s5_scan(sr_ref, si_ref, ar, ai, T, reverse):
    L, V = T // S5_SEG, S5_SEG // 8
    assert L * S5_SEG == T and L & (L - 1) == 0, T
    sg = -1.0 if reverse else 1.0
    a_r = [jnp.broadcast_to(ar[:, 128 * b:128 * (b + 1)], (8, 128)) for b in range(S5_NB)]
    a_i = [jnp.broadcast_to(sg * ai[:, 128 * b:128 * (b + 1)], (8, 128)) for b in range(S5_NB)]

    def rows(k, v):
        return pl.ds(pl.multiple_of(((L - 1 - k) if reverse else k) * S5_SEG + 8 * v, 8), 8)

    def local(k, carry):
        out = []
        for b in range(S5_NB):
            for v in range(V):
                idx = rows(k, v)
                mr, mi = _cmul(a_r[b], a_i[b], *carry[b * V + v])
                nr, ni = mr + sr_ref[b, idx, :], mi + si_ref[b, idx, :]
                sr_ref[b, idx, :] = nr
                si_ref[b, idx, :] = ni
                out.append((nr, ni))
        return tuple(out)

    z = jnp.zeros((8, 128), F32)
    ends = lax.fori_loop(0, L, local, tuple((z, z) for _ in range(S5_NB * V)))

    carries = []
    for b in range(S5_NB):
        pr, pi = a_r[b][0:1], a_i[b][0:1]
        n = L
        while n > 1:
            pr, pi = _cmul(pr, pi, pr, pi)
            n //= 2
        cr, ci = jnp.zeros((1, 128), F32), jnp.zeros((1, 128), F32)
        into = [None] * S5_SEG
        for j in (reversed(range(S5_SEG)) if reverse else range(S5_SEG)):
            into[j] = (cr, ci)
            er, ei = ends[b * V + j // 8]
            mr, mi = _cmul(pr, pi, cr, ci)
            cr, ci = mr + er[j % 8:j % 8 + 1], mi + ei[j % 8:j % 8 + 1]
        carries.append([(jnp.concatenate([into[8 * v + s][0] for s in range(8)], axis=0),
                         jnp.concatenate([into[8 * v + s][1] for s in range(8)], axis=0)) for v in range(V)])

    def fix(k, powers):
        out = []
        for b in range(S5_NB):
            pr, pi = powers[b]
            for v in range(V):
                idx = rows(k, v)
                dr, di = _cmul(pr, pi, *carries[b][v])
                sr_ref[b, idx, :] += dr
                si_ref[b, idx, :] += di
            out.append(_cmul(pr, pi, a_r[b], a_i[b]))
        return tuple(out)

    lax.fori_loop(0, L, fix, tuple((a_r[b], a_i[b]) for b in range(S5_NB)))


def _s5_in(ut, b_ref, s_ref):
    for b in range(S5_NB):
        s_ref[b] = _dot(ut, b_ref[:, 128 * b:128 * (b + 1)], NN)


def _to_seg(v):
    T, C = v.shape
    return v.reshape(S5_SEG, T // S5_SEG, C).transpose(1, 0, 2).reshape(T, C)


def _from_seg(v):
    T, C = v.shape
    return v.reshape(T // S5_SEG, S5_SEG, C).transpose(1, 0, 2).reshape(T, C)


def _s5_specs(T):
    u_spec = pl.BlockSpec((T, 128), lambda j, h: (0, j))
    b_spec = pl.BlockSpec((None, 128, S5_HALF), lambda j, h: (j, 0, h))
    c_spec = pl.BlockSpec((None, S5_HALF, 128), lambda j, h: (j, h, 0))
    a_spec = pl.BlockSpec((None, 1, S5_HALF), lambda j, h: (j, 0, h))
    return u_spec, b_spec, c_spec, a_spec


def _s5_fwd(u, tiles):
    T = u.shape[0]
    u_spec, b_spec, c_spec, a_spec = _s5_specs(T)

    def body(u_ref, br_ref, bi_ref, cr_ref, ci_ref, ar_ref, ai_ref, y_ref, sr_ref, si_ref):
        ut = u_ref[...].astype(MXU)
        _s5_in(ut, br_ref, sr_ref)
        _s5_in(ut, bi_ref, si_ref)
        _s5_scan(sr_ref, si_ref, ar_ref[...], ai_ref[...], T, False)
        y = None
        for b in range(S5_NB):
            blk = slice(128 * b, 128 * (b + 1))
            yb = _dot(sr_ref[b], cr_ref[blk, :], NN) - _dot(si_ref[b], ci_ref[blk, :], NN)
            y = yb if y is None else y + yb

        @pl.when(pl.program_id(1) == 0)
        def _():
            y_ref[...] = y

        @pl.when(pl.program_id(1) == 1)
        def _():
            y_ref[...] += y

    scr = pltpu.VMEM((S5_NB, T, 128), F32)
    return pl.pallas_call(
        body, name="s5_fwd", grid=(S5_TILES, 2),
        in_specs=[u_spec, b_spec, b_spec, c_spec, c_spec, a_spec, a_spec],
        out_specs=u_spec, out_shape=jax.ShapeDtypeStruct(u.shape, F32),
        scratch_shapes=[scr, scr],
        compiler_params=_params(("parallel", "arbitrary")),
    )(u, *tiles)


def _s5_bwd(u, dys, du_skip, tiles):
    T = u.shape[0]
    u_spec, b_spec, c_spec, a_spec = _s5_specs(T)

    def body(u_ref, dy_ref, sk_ref, br_ref, bi_ref, cr_ref, ci_ref, ar_ref, ai_ref,
             du_ref, dbr_ref, dbi_ref, dcr_ref, dci_ref, dar_ref, dai_ref, sr_ref, si_ref, lr_ref, li_ref):
        ut, dy = u_ref[...].astype(MXU), dy_ref[...].astype(MXU)
        ar, ai = ar_ref[...], ai_ref[...]
        _s5_in(ut, br_ref, sr_ref)
        _s5_in(ut, bi_ref, si_ref)
        _s5_scan(sr_ref, si_ref, ar, ai, T, False)
        for b in range(S5_NB):
            blk = slice(128 * b, 128 * (b + 1))
            dcr_ref[blk, :] = _dot(sr_ref[b], dy, TN)
            dci_ref[blk, :] = -_dot(si_ref[b], dy, TN)
            lr_ref[b] = _dot(dy, cr_ref[blk, :], NT)
            li_ref[b] = -_dot(dy, ci_ref[blk, :], NT)
        _s5_scan(lr_ref, li_ref, ar, ai, T, True)
        du = None
        head = lax.broadcasted_iota(jnp.int32, (T, 128), 0) < S5_SEG
        seg0 = lax.broadcasted_iota(jnp.int32, (S5_SEG, 128), 0) == 0
        for b in range(S5_NB):
            blk = slice(128 * b, 128 * (b + 1))
            lr, li = lr_ref[b], li_ref[b]
            pr = jnp.where(head, 0.0, pltpu.roll(sr_ref[b], S5_SEG, 0))
            pi = jnp.where(head, 0.0, pltpu.roll(si_ref[b], S5_SEG, 0))
            er = jnp.where(seg0, 0.0, pltpu.roll(sr_ref[b, T - S5_SEG:T, :], 1, 0))
            ei = jnp.where(seg0, 0.0, pltpu.roll(si_ref[b, T - S5_SEG:T, :], 1, 0))
            hr, hi = lr[0:S5_SEG], li[0:S5_SEG]
            dar_ref[:, blk] = _colsum(lr * pr + li * pi) + _colsum(hr * er + hi * ei)
            dai_ref[:, blk] = _colsum(li * pr - lr * pi) + _colsum(hi * er - hr * ei)
            dbr_ref[:, blk] = _dot(ut, lr, TN)
            dbi_ref[:, blk] = _dot(ut, li, TN)
            dub = _dot(lr, br_ref[:, blk], NT) + _dot(li, bi_ref[:, blk], NT)
            du = dub if du is None else du + dub

        @pl.when(pl.program_id(1) == 0)
        def _():
            du_ref[...] = du + sk_ref[...]

        @pl.when(pl.program_id(1) == 1)
        def _():
            du_ref[...] += du

    scr = pltpu.VMEM((S5_NB, T, 128), F32)
    return pl.pallas_call(
        body, name="s5_bwd", grid=(S5_TILES, 2),
        in_specs=[u_spec, u_spec, u_spec, b_spec, b_spec, c_spec, c_spec, a_spec, a_spec],
        out_specs=[u_spec, b_spec, b_spec, c_spec, c_spec, a_spec, a_spec],
        out_shape=[jax.ShapeDtypeStruct(u.shape, F32)] + [jax.ShapeDtypeStruct(t.shape, F32) for t in tiles],
        scratch_shapes=[scr, scr, scr, scr],
        compiler_params=_params(("parallel", "arbitrary")),
    )(u, dys, du_skip, *tiles)


CONV_TILE = 256


def _conv_pre(x, w, b):
    T = x.shape[0]
    row = lax.broadcasted_iota(jnp.int32, x.shape, 0)
    acc = x * w[CONV_K - 1:CONV_K, :] + b
    for lag in range(1, CONV_K):
        acc = acc + jnp.where(row >= lag, pltpu.roll(x, lag, 0), 0.0) * w[CONV_K - 1 - lag:CONV_K - lag, :]
    return acc


def _conv_fwd(x, w, b):
    T, C = x.shape
    col = pl.BlockSpec((T, CONV_TILE), lambda j: (0, j))

    def body(x_ref, w_ref, b_ref, o_ref):
        o_ref[...] = jax.nn.silu(_conv_pre(x_ref[...], w_ref[...], b_ref[...]))

    return pl.pallas_call(
        body, name="conv_fwd", grid=(C // CONV_TILE,),
        in_specs=[col, pl.BlockSpec((CONV_K, CONV_TILE), lambda j: (0, j)), pl.BlockSpec((1, CONV_TILE), lambda j: (0, j))],
        out_specs=col, out_shape=jax.ShapeDtypeStruct((T, C), F32),
        compiler_params=_params(("parallel",)),
    )(x, w, b)


def _conv_bwd(x, w, b, dout, col0, name):
    T, C = dout.shape
    off = col0 // CONV_TILE
    xcol = pl.BlockSpec((T, CONV_TILE), lambda j: (0, j + off))
    dcol = pl.BlockSpec((T, CONV_TILE), lambda j: (0, j))

    def body(x_ref, w_ref, b_ref, d_ref, dx_ref, dw_ref, db_ref):
        x, w = x_ref[...], w_ref[...]
        _, vjp = jax.vjp(jax.nn.silu, _conv_pre(x, w, b_ref[...]))
        (dy,) = vjp(d_ref[...])
        row = lax.broadcasted_iota(jnp.int32, x.shape, 0)
        dx = dy * w[CONV_K - 1:CONV_K, :]
        dw_ref[CONV_K - 1:CONV_K, :] = _colsum(dy * x)
        for lag in range(1, CONV_K):
            dx = dx + jnp.where(row < T - lag, pltpu.roll(dy, T - lag, 0), 0.0) * w[CONV_K - 1 - lag:CONV_K - lag, :]
            xs = jnp.where(row >= lag, pltpu.roll(x, lag, 0), 0.0)
            dw_ref[CONV_K - 1 - lag:CONV_K - lag, :] = _colsum(dy * xs)
        dx_ref[...] = dx.astype(dx_ref.dtype)
        db_ref[...] = _colsum(dy)

    return pl.pallas_call(
        body, name=name, grid=(C // CONV_TILE,),
        in_specs=[xcol, pl.BlockSpec((CONV_K, CONV_TILE), lambda j: (0, j + off)),
                  pl.BlockSpec((1, CONV_TILE), lambda j: (0, j + off)), dcol],
        out_specs=[dcol, pl.BlockSpec((CONV_K, CONV_TILE), lambda j: (0, j)), pl.BlockSpec((1, CONV_TILE), lambda j: (0, j))],
        out_shape=[jax.ShapeDtypeStruct((T, C), BF16), jax.ShapeDtypeStruct((CONV_K, C), F32), jax.ShapeDtypeStruct((1, C), F32)],
        compiler_params=_params(("parallel",)),
    )(x, w, b, dout)


def _ssd_common(x_ref, b_ref, c_ref, dtr_ref, bias_ref, alog_ref, tri_ref, triu_ref):
    Q = x_ref.shape[0]
    x, Bm, Cm = x_ref[...], b_ref[...], c_ref[...]
    pre = dtr_ref[...] + bias_ref[...]
    dt = _softplus(pre)
    A = -jnp.exp(alog_ref[...])
    adt = dt * A
    ac4 = _dot(tri_ref[...], adt, NN, HI)
    ar4 = _dot(adt, triu_ref[...], TN, HI)
    cb = _dot(Cm, Bm, NT)
    ii = lax.broadcasted_iota(jnp.int32, (Q, Q), 0)
    jj = lax.broadcasted_iota(jnp.int32, (Q, Q), 1)
    atot4 = ac4[Q - 1:Q, :]
    return x, Bm, Cm, pre, dt, A, ac4, ar4, cb, ii >= jj, atot4


def _ssd_decay(k, ac4, ar4, causal):
    seg = ac4[:, k:k + 1] - ar4[k:k + 1, :]
    return jnp.where(causal, jnp.exp(jnp.where(causal, seg, 0.0)), 0.0)


def _per_head(c4, n):
    lane = lax.broadcasted_iota(jnp.int32, (c4.shape[0], n), 1)
    out = jnp.broadcast_to(c4[:, 0:1], (c4.shape[0], n))
    for k in range(1, SSD_HPG):
        out = jnp.where(lane >= SSD_HEADDIM * k, c4[:, k:k + 1], out)
    return out


def _head_sums(v):
    lane = lax.broadcasted_iota(jnp.int32, (v.shape[0], SSD_HPG), 1)
    out = jnp.zeros((v.shape[0], SSD_HPG), F32)
    for k in range(SSD_HPG):
        s = jnp.sum(v[:, SSD_HEADDIM * k:SSD_HEADDIM * (k + 1)], axis=1, keepdims=True)
        out = jnp.where(lane == k, s, out)
    return out


def _head_rows(c4):
    n = SSD_HPG * SSD_HEADDIM
    row = lax.broadcasted_iota(jnp.int32, (n, 1), 0)
    out = jnp.broadcast_to(c4[:, 0:1], (n, 1))
    for k in range(1, SSD_HPG):
        out = jnp.where(row >= SSD_HEADDIM * k, c4[:, k:k + 1], out)
    return out


def _ssd_specs(T, Q, rev):
    NC = T // Q
    cc = (lambda c: NC - 1 - c) if rev else (lambda c: c)
    x_spec = pl.BlockSpec((Q, SSD_HPG * SSD_HEADDIM), lambda g, c: (cc(c), g))
    b_spec = pl.BlockSpec((Q, SSD_STATE), lambda g, c: (cc(c), D_INNER // SSD_STATE + g))
    c_spec = pl.BlockSpec((Q, SSD_STATE), lambda g, c: (cc(c), D_INNER // SSD_STATE + SSD_GROUPS + g))
    dt_spec = pl.BlockSpec((None, Q, SSD_HPG), lambda g, c: (g, cc(c), 0))
    p_spec = pl.BlockSpec((None, 1, SSD_HPG), lambda g, c: (g, 0, 0))
    tri_spec = pl.BlockSpec((Q, Q), lambda g, c: (0, 0))
    h_spec = pl.BlockSpec((None, None, SSD_HPG * SSD_HEADDIM, SSD_STATE), lambda g, c: (g, cc(c), 0, 0))
    return x_spec, b_spec, c_spec, dt_spec, p_spec, tri_spec, h_spec


def _tri(Q):
    tri = jnp.tril(jnp.ones((Q, Q), F32))
    return tri, tri.T


def _ssd_fwd(xc, dtr8, bias8, alog8, dsk8):
    T, Q = xc.shape[0], SSD_Q
    NC = T // Q
    P, W = SSD_HEADDIM, SSD_HPG * SSD_HEADDIM
    x_spec, b_spec, c_spec, dt_spec, p_spec, tri_spec, h_spec = _ssd_specs(T, Q, False)

    def body(x_ref, b_ref, c_ref, dtr_ref, bias_ref, alog_ref, dsk_ref, tri_ref, triu_ref, y_ref, hs_ref, h_scr):
        @pl.when(pl.program_id(1) == 0)
        def _():
            h_scr[...] = jnp.zeros_like(h_scr)

        hin = h_scr[...]
        hs_ref[...] = hin
        x, Bm, Cm, pre, dt, A, ac4, ar4, cb, causal, atot4 = _ssd_common(
            x_ref, b_ref, c_ref, dtr_ref, bias_ref, alog_ref, tri_ref, triu_ref)
        xdt = x * _per_head(dt, W)
        rest = _dot(Cm, hin, NT) * _per_head(jnp.exp(ac4), W) + _per_head(dsk_ref[...], W) * x
        for k in range(SSD_HPG):
            blk = slice(P * k, P * (k + 1))
            G = cb * _ssd_decay(k, ac4, ar4, causal)
            y_ref[:, blk] = _dot(G, xdt[:, blk], NN) + rest[:, blk]
        xw = xdt * _per_head(jnp.exp(atot4 - ac4), W)
        h_scr[...] = _head_rows(jnp.exp(atot4)) * hin + _dot(xw, Bm, TN)

    tri, triu = _tri(Q)
    return pl.pallas_call(
        body, name="ssd_fwd", grid=(SSD_GROUPS, NC),
        in_specs=[x_spec, b_spec, c_spec, dt_spec, p_spec, p_spec, p_spec, tri_spec, tri_spec],
        out_specs=[pl.BlockSpec((Q, W), lambda g, c: (c, g)), h_spec],
        out_shape=[jax.ShapeDtypeStruct((T, D_INNER), F32),
                   jax.ShapeDtypeStruct((SSD_GROUPS, NC, W, SSD_STATE), F32)],
        scratch_shapes=[pltpu.VMEM((W, SSD_STATE), F32)],
        compiler_params=_params(("parallel", "arbitrary")),
    )(xc, xc, xc, dtr8, bias8, alog8, dsk8, tri, triu)


def _ssd_bwd(xc, dtr8, bias8, alog8, dsk8, hs, dy):
    T, Q = xc.shape[0], SSD_Q
    NC = T // Q
    P, W = SSD_HEADDIM, SSD_HPG * SSD_HEADDIM
    x_spec, b_spec, c_spec, dt_spec, p_spec, tri_spec, h_spec = _ssd_specs(T, Q, True)
    dy_spec = pl.BlockSpec((Q, W), lambda g, c: (NC - 1 - c, g))
    dbc_spec = pl.BlockSpec((Q, SSD_STATE), lambda g, c: (NC - 1 - c, g))

    def body(x_ref, b_ref, c_ref, dtr_ref, bias_ref, alog_ref, dsk_ref, tri_ref, triu_ref, hs_ref, dy_ref,
             dx_ref, db_ref, dc_ref, ddtr_ref, dbias_ref, dalog_ref, ddsk_ref, dh_scr):
        @pl.when(pl.program_id(1) == 0)
        def _():
            dh_scr[...] = jnp.zeros_like(dh_scr)
            dbias_ref[...] = jnp.zeros_like(dbias_ref)
            dalog_ref[...] = jnp.zeros_like(dalog_ref)
            ddsk_ref[...] = jnp.zeros_like(ddsk_ref)

        x, Bm, Cm, pre, dt, A, ac4, ar4, cb, causal, atot4 = _ssd_common(
            x_ref, b_ref, c_ref, dtr_ref, bias_ref, alog_ref, tri_ref, triu_ref)
        dyv, hin, dho = dy_ref[...], hs_ref[...], dh_scr[...]
        dt_w = _per_head(dt, W)
        xdt = x * dt_w
        E4, F4, etot4 = jnp.exp(ac4), jnp.exp(atot4 - ac4), jnp.exp(atot4)
        F_w = _per_head(F4, W)
        ddsk_ref[...] += _head_sums(_colsum(dyv * x))
        Z = _dot(Cm, hin, NT)
        dZ = dyv * _per_head(E4, W)
        dac4 = _head_sums(dyv * Z) * E4
        dC = _dot(dZ, hin, NN)
        dh_scr[...] = _dot(dZ, Cm, TN) + _head_rows(etot4) * dho
        per_row = jnp.sum(dho * hin, axis=1, keepdims=True)
        lane4 = lax.broadcasted_iota(jnp.int32, (1, SSD_HPG), 1)
        datot4 = jnp.zeros((1, SSD_HPG), F32)
        for k in range(SSD_HPG):
            datot4 = jnp.where(lane4 == k, jnp.sum(per_row[P * k:P * (k + 1)], keepdims=True), datot4)
        datot4 = datot4 * etot4
        dxw = _dot(Bm, dho, NT)
        dB = _dot(xdt * F_w, dho, NN)
        dFa4 = _head_sums(dxw * xdt) * F4
        datot4 = datot4 + _colsum(dFa4)
        dac4 = dac4 - dFa4
        dcb = jnp.zeros((Q, Q), F32)
        lane_q4 = lax.broadcasted_iota(jnp.int32, (Q, SSD_HPG), 1)
        sub_4q = lax.broadcasted_iota(jnp.int32, (SSD_HPG, Q), 0)
        row_sums = jnp.zeros((Q, SSD_HPG), F32)
        col_sums = jnp.zeros((SSD_HPG, Q), F32)
        dxdt_heads = []
        for k in range(SSD_HPG):
            blk = slice(P * k, P * (k + 1))
            L = _ssd_decay(k, ac4, ar4, causal)
            G = cb * L
            dG = _dot(dyv[:, blk], xdt[:, blk], NT)
            dxdt_heads.append(_dot(G, dyv[:, blk], TN))
            dcb = dcb + dG * L
            Mseg = dG * G
            row_sums = jnp.where(lane_q4 == k, jnp.sum(Mseg, axis=1, keepdims=True), row_sums)
            col_sums = jnp.where(sub_4q == k, jnp.sum(Mseg, axis=0, keepdims=True), col_sums)
        dxdt = dxw * F_w + jnp.concatenate(dxdt_heads, axis=1)
        last = lax.broadcasted_iota(jnp.int32, (Q, SSD_HPG), 0) == Q - 1
        dac4 = dac4 + row_sums + jnp.where(last, datot4, 0.0)
        dadt4 = _dot(triu_ref[...], dac4, NN, HI) - _dot(triu_ref[...], col_sums, NT, HI)
        ddt4 = _head_sums(dxdt * x) + dadt4 * A
        dalog_ref[...] += _colsum(dadt4 * dt) * A
        ddtr4 = ddt4 * jax.nn.sigmoid(pre)
        dbias_ref[...] += _colsum(ddtr4)
        ddtr_ref[...] = ddtr4
        dx_ref[...] = (_per_head(dsk_ref[...], W) * dyv + dxdt * dt_w).astype(dx_ref.dtype)
        db_ref[...] = dB + _dot(dcb, Cm, TN)
        dc_ref[...] = dC + _dot(dcb, Bm, NN)

    tri, triu = _tri(Q)
    return pl.pallas_call(
        body, name="ssd_bwd", grid=(SSD_GROUPS, NC),
        in_specs=[x_spec, b_spec, c_spec, dt_spec, p_spec, p_spec, p_spec, tri_spec, tri_spec, h_spec, dy_spec],
        out_specs=[dy_spec, dbc_spec, dbc_spec, dt_spec, p_spec, p_spec, p_spec],
        out_shape=[jax.ShapeDtypeStruct((T, D_INNER), F32),
                   jax.ShapeDtypeStruct((T, SSD_GROUPS * SSD_STATE), F32),
                   jax.ShapeDtypeStruct((T, SSD_GROUPS * SSD_STATE), F32),
                   jax.ShapeDtypeStruct(dtr8.shape, F32)] + [jax.ShapeDtypeStruct(bias8.shape, F32)] * 3,
        scratch_shapes=[pltpu.VMEM((W, SSD_STATE), F32)],
        compiler_params=_params(("parallel", "arbitrary")),
    )(xc, xc, xc, dtr8, bias8, alog8, dsk8, tri, triu, hs, dy)


def _to_groups(v):
    return v.reshape(SSD_GROUPS, 1, SSD_HPG)


def _mixer_fwd(x, p, w):
    T, D = x.shape
    (hn,) = _rowwise(lambda xt, g: (_rms(xt, g),), [x], [p["mix_norm"]], [(D, BF16)], name="mix_norm")
    winT = w["w_in"]
    u, z, xbc, dtr, gl = [
        _matmul(hn, winT[IN_OFFS[i]:IN_OFFS[i + 1]], "nt", name=f"mix_in{i}") for i in range(5)]
    tiles = _s5_derive(*[p[k][0] for k in ("s5_A_re", "s5_A_im", "s5_log_dt", "s5_B_re", "s5_B_im", "s5_C_re", "s5_C_im")])
    u = _to_seg(u)
    ys = _s5_fwd(u, tiles)
    (g,) = _rowwise(lambda yt, ut, d: (jax.nn.gelu(yt + d * ut),), [ys, u], [p["s5_D"]], [(D, F32)], name="s5_gelu")
    y5 = _matmul(g, w["s5_w_glu"], "nn", name="s5_glu", out=(BF16,),
                 epilogue=lambda acc, gt, b: (gt * jax.nn.sigmoid(acc + b),), extras=[g, p["s5_b_glu"]])
    xc = _conv_fwd(xbc, w["conv_w"], p["conv_b"])
    dtr8 = dtr.reshape(T, SSD_GROUPS, SSD_HPG).transpose(1, 0, 2)
    ssd_p = [_to_groups(p[k]) for k in ("ssd_dt_bias", "ssd_A_log", "ssd_D")]
    yssd_raw, hs = _ssd_fwd(xc, dtr8, *ssd_p)
    (yssd,) = _rowwise(lambda yt, zt, nw: (_rms(yt * jax.nn.silu(zt), nw),), [yssd_raw, z], [p["ssd_norm"]],
                       [(D_INNER, BF16)], name="ssd_gate")
    p5 = _from_seg(_matmul(y5, w["w_proj_s5"], "nn", name="mix_p5"))
    pssd = _matmul(yssd, w["w_proj_ssd"], "nn", name="mix_pssd")

    def merge(glt, at, bt, bg):
        gates = jax.nn.sigmoid(glt + bg)
        return (gates[:, :D] * at + gates[:, D:] * bt,)

    (merged,) = _rowwise(merge, [gl, p5, pssd], [p["b_gate"]], [(D, BF16)], name="mix_merge")
    y = _matmul(merged, w["w_out"], "nn", name="mix_out", epilogue=lambda acc, xt: (xt + acc,), extras=[x])
    res = dict(hn=hn, u=u, z=z, xbc=xbc, gl=gl, tiles=tiles, ys=ys, g=g, y5=y5, xc=xc, dtr8=dtr8, ssd_p=ssd_p,
               yssd_raw=yssd_raw, hs=hs, yssd=yssd, p5=p5, pssd=pssd, merged=merged)
    return y, res


def _mixer_bwd(dy, x, p, w, r):
    T, D = x.shape
    gp = {}
    dmerged = _matmul(dy, w["w_out"], "nt", name="mix_d_merged")
    gp["w_out"] = _matmul(r["merged"], dy, "tn", name="mix_d_wout")

    def merge_bwd(glt, at, bt, dm, bg):
        def f(q, a_, b_):
            gates = jax.nn.sigmoid(q + bg)
            return gates[:, :D] * a_ + gates[:, D:] * b_
        _, vjp = jax.vjp(f, glt, at, bt)
        dq, da_, db_ = vjp(dm)
        return da_, db_, dq, _colsum(dq)

    dp5, dpssd, dgl, gp["b_gate"] = _rowwise(
        merge_bwd, [r["gl"], r["p5"], r["pssd"], dmerged], [p["b_gate"]],
        [(D, BF16), (D, BF16), (2 * D, BF16)], [((1, 2 * D), F32)], name="mix_d_merge")
    dp5 = _to_seg(dp5)
    dy5 = _matmul(dp5, w["w_proj_s5"], "nt", name="mix_d_y5")
    gp["w_proj_s5"] = _matmul(r["y5"], dp5, "tn", name="mix_d_wp5")
    dyssd = _matmul(dpssd, w["w_proj_ssd"], "nt", name="mix_d_yssd")
    gp["w_proj_ssd"] = _matmul(r["yssd"], dpssd, "tn", name="mix_d_wpssd")

    def gate_bwd(yt, zt, dyt, nw):
        _, vjp = jax.vjp(lambda a_, b_, c_: _rms(a_ * jax.nn.silu(b_), c_), yt, zt, nw)
        return vjp(dyt)

    dyraw, dz, gp["ssd_norm"] = _rowwise(
        gate_bwd, [r["yssd_raw"], r["z"], dyssd], [p["ssd_norm"]],
        [(D_INNER, F32), (D_INNER, BF16)], [((1, D_INNER), F32)], name="ssd_d_gate")
    dxs, dBm, dCm, ddtr8, dbias8, dalog8, ddsk8 = _ssd_bwd(r["xc"], r["dtr8"], *r["ssd_p"], r["hs"], dyraw)
    gp["ssd_dt_bias"], gp["ssd_A_log"], gp["ssd_D"] = [v.reshape(1, SSD_HEADS) for v in (dbias8, dalog8, ddsk8)]
    ddtr = ddtr8.transpose(1, 0, 2).reshape(T, SSD_HEADS)
    conv = [_conv_bwd(r["xbc"], w["conv_w"], p["conv_b"], d, c0, f"conv_bwd{i}")
            for i, (d, c0) in enumerate(((dxs, 0), (dBm, D_INNER), (dCm, D_INNER + SSD_GROUPS * SSD_STATE)))]
    dxbc = [c[0] for c in conv]
    gp["conv_w"] = jnp.concatenate([c[1] for c in conv], axis=1)
    gp["conv_b"] = jnp.concatenate([c[2] for c in conv], axis=1)

    g = r["g"]

    def glu_ep(acc, gt, b, dyt):
        _, vjp = jax.vjp(lambda g_, t_: g_ * jax.nn.sigmoid(t_ + b), gt, acc)
        return vjp(dyt)

    dg1, dt_ = _matmul(g, w["s5_w_glu"], "nn", name="s5_d_glu", out=(F32, BF16), epilogue=glu_ep,
                       extras=[g, p["s5_b_glu"], dy5])
    gp["s5_w_glu"] = _matmul(g, dt_, "tn", name="s5_d_wglu")
    dg = _matmul(dt_, w["s5_w_glu"], "nt", name="s5_d_g", epilogue=lambda acc, e: (acc + e,), extras=[dg1])

    def gelu_bwd(yt, ut, dgt, dtt, d):
        _, vjp = jax.vjp(lambda y_, d_: jax.nn.gelu(y_ + d_ * ut), yt, d)
        dys, dd = vjp(dgt)
        return dys, dys * d, dd, _colsum(dtt.astype(F32))

    dys, dusk, gp["s5_D"], gp["s5_b_glu"] = _rowwise(
        gelu_bwd, [r["ys"], r["u"], dg, dt_], [p["s5_D"]], [(D, F32), (D, F32)], [((1, D), F32), ((1, D), F32)],
        name="s5_d_gelu")
    du, *dtiles = _s5_bwd(r["u"], dys, dusk, r["tiles"])
    du = _from_seg(du)

    pieces = [du, dz, None, ddtr, dgl]
    winT = w["w_in"]
    hn = r["hn"]
    d_rows, dhn = [], None
    cols = [(du, 0, 1024), (dz, 1024, 3072), (dxbc[0], 3072, 5120), (dxbc[1], 5120, 6144), (dxbc[2], 6144, 7168),
            (ddtr, 7168, 7200), (dgl, 7200, 9248)]
    for i, (d, lo, hi) in enumerate(cols):
        d_rows.append(_matmul(d, hn, "tn", name=f"mix_d_win{i}"))
        if dhn is None:
            dhn = _matmul(d, winT[lo:hi], "nn", name=f"mix_d_hn{i}")
        else:
            dhn = _matmul(d, winT[lo:hi], "nn", name=f"mix_d_hn{i}", epilogue=lambda acc, e: (acc + e,), extras=[dhn])
    gp["w_in"] = jnp.concatenate(d_rows, axis=0)
    del pieces

    def norm_bwd(xt, dh, dyt, gn):
        _, vjp = jax.vjp(_rms, xt, gn)
        dx, dgn = vjp(dh)
        return dyt + dx, dgn

    dx, gp["mix_norm"] = _rowwise(norm_bwd, [x, dhn, dy], [p["mix_norm"]], [(D, F32)], [((1, D), F32)], name="mix_d_norm")
    return dx, gp, dtiles


def _s5_param_grads(p, dtiles):
    keys = ("s5_A_re", "s5_A_im", "s5_log_dt", "s5_B_re", "s5_B_im", "s5_C_re", "s5_C_im")
    _, vjp = jax.vjp(_s5_derive, *[p[k][0] for k in keys])
    return {k: v[None] for k, v in zip(keys, vjp(tuple(dtiles)))}


def _local_step(x, target, p, w, late_weights):
    T, D = x.shape
    x1, r1 = _ffn_fwd(x, p["ffn1_norm"], w["ffn1_w_gate"], w["ffn1_w_up"], w["ffn1_w_down"], "ffn1")
    x2, rm = _mixer_fwd(x1, p, w)
    w = {**w, **late_weights(rm["p5"])}
    x3, r2 = _ffn_fwd(x2, p["ffn2_norm"], w["ffn2_w_gate"], w["ffn2_w_up"], w["ffn2_w_down"], "ffn2")

    def head(xt, tt, g):
        def f(x_, g_):
            e = _rms(x_, g_) - tt
            return 0.5 * jnp.sum(jnp.mean(e * e, axis=-1))
        l, (dx_, dg_) = jax.value_and_grad(f, argnums=(0, 1))(xt, g)
        return dx_, l.reshape(1, 1), dg_

    dx3, loss, d_final = _rowwise(head, [x3, target], [p["final_norm"]], [(D, F32)],
                                  [((1, 1), F32), ((1, D), F32)], name="loss_head")
    gp, gw = {"final_norm": d_final}, {}
    dx2, gp["ffn2_norm"], gw["ffn2_w_gate"], gw["ffn2_w_up"], gw["ffn2_w_down"] = _ffn_bwd(
        dx3, x2, p["ffn2_norm"], w["ffn2_w_gate"], w["ffn2_w_up"], w["ffn2_w_down"], r2, "ffn2")
    dx1, gm, dtiles = _mixer_bwd(dx2, x1, p, w, rm)
    for k in ("w_out", "w_proj_s5", "w_proj_ssd", "s5_w_glu", "w_in", "conv_w"):
        gw[k] = gm.pop(k)
    gp.update(gm)
    gp.update(_s5_param_grads(p, dtiles))
    dx0, gp["ffn1_norm"], gw["ffn1_w_gate"], gw["ffn1_w_up"], gw["ffn1_w_down"] = _ffn_bwd(
        dx1, x, p["ffn1_norm"], w["ffn1_w_gate"], w["ffn1_w_up"], w["ffn1_w_down"], r1, "ffn1")
    return loss, dx0, gp, gw


def _adamw(w, m, v, gs, *, name, emit_g, tm=128):
    n_g = len(gs)
    C = w.shape[1]

    def fn(wt, mt, vt, *gt):
        g = gt[0]
        for e in gt[1:]:
            g = g + e
        m2 = ADAM_B1 * mt + (1.0 - ADAM_B1) * g
        v2 = ADAM_B2 * vt + (1.0 - ADAM_B2) * (g * g)
        m_hat = m2 / (1.0 - ADAM_B1 ** ADAM_STEP)
        v_hat = v2 / (1.0 - ADAM_B2 ** ADAM_STEP)
        delta = -ADAM_LR * (m_hat / (jnp.sqrt(v_hat) + ADAM_EPS) + ADAM_WD * wt)
        return ((g,) if emit_g else ()) + (delta, m2, v2)

    assert n_g >= 1
    return _rowwise(fn, [w, m, v, *gs], [], [(C, F32)] * (4 if emit_g else 3), name=name, tm=_rtile(w.shape[0], tm))


def _rtile(r, cap=512):
    if r <= cap:
        return r
    best = None
    for t in range(8, cap + 1, 8):
        if r % t == 0:
            best = t
    assert best is not None, r
    return best


ANY = pl.BlockSpec(memory_space=pl.ANY)


def _where_am_i():
    x, y, c = lax.axis_index("x"), lax.axis_index("y"), lax.axis_index("c")
    ks = (2 * x + y, 2 * (1 - x) + y, 2 * x + (1 - y), 2 * (1 - x) + (1 - y))
    return x, y, c, ks


def _rc(src, dst, ssem, rsem, to):
    return pltpu.make_async_remote_copy(src_ref=src, dst_ref=dst, send_sem=ssem, recv_sem=rsem,
                                        device_id=to, device_id_type=MESH)


def _gather_shards(srcs):
    n = len(srcs)

    def body(*refs):
        _gather_body(refs[:n], refs[n:2 * n], *refs[2 * n:], extra=())

    return pl.pallas_call(
        body, name="gather_shards",
        in_specs=[ANY] * n, out_specs=[ANY] * n,
        out_shape=[jax.ShapeDtypeStruct((4,) + s.shape, s.dtype) for s in srcs],
        scratch_shapes=[pltpu.SemaphoreType.DMA((n, 6)), pltpu.SemaphoreType.DMA((n, 6)), pltpu.SemaphoreType.DMA((n,))],
        compiler_params=pltpu.CompilerParams(has_side_effects=True),
    )(*srcs)


def _handshake_all():
    x, y, c = lax.axis_index("x"), lax.axis_index("y"), lax.axis_index("c")
    barrier = pltpu.get_barrier_semaphore()
    for dx, dy, dc in itertools.product((0, 1), repeat=3):
        if (dx, dy, dc) != (0, 0, 0):
            to = (1 - x if dx else x, 1 - y if dy else y, 1 - c if dc else c)
            pl.semaphore_signal(barrier, inc=1, device_id=to, device_id_type=MESH)
    pl.semaphore_wait(barrier, 7)


SEQUENCER = dict(axis_name="seq", num_cores=1)


def _gather_shards_async(srcs, token, name, collective_id):
    n = len(srcs)

    def body(*refs):
        tok, tok_out = refs[n], refs[2 * n + 1]
        ssem, rsem, lsem = refs[2 * n + 2:]
        _handshake_all()
        _gather_body(refs[:n], refs[n + 1:2 * n + 1], ssem, rsem, lsem,
                     extra=(pltpu.make_async_copy(tok, tok_out, lsem.at[n]),))

    res = pl.kernel(
        body, name=name,
        out_type=[jax.ShapeDtypeStruct((4,) + s.shape, s.dtype) for s in srcs]
        + [jax.ShapeDtypeStruct(token.shape, token.dtype)],
        mesh=plsc.ScalarSubcoreMesh(**SEQUENCER),
        scratch_types=[pltpu.SemaphoreType.DMA((n, 6)), pltpu.SemaphoreType.DMA((n, 6)), pltpu.SemaphoreType.DMA((n + 1,))],
        compiler_params=pltpu.CompilerParams(collective_id=collective_id),
    )(*srcs, token)
    return res[:n], res[n]


def _gather_body(src, out, ssem, rsem, lsem, extra):
        n = len(src)
        x, y, c, (k_me, k_x, k_y, k_d) = _where_am_i()
        sib = (x, y, 1 - c)
        local = [pltpu.make_async_copy(src[i], out[i].at[k_me], lsem.at[i]) for i in range(n)] + list(extra)
        for cp in local:
            cp.start()

        def own(i, q, to):
            return _rc(src[i], out[i].at[k_me], ssem.at[i, q], rsem.at[i, q], to)

        def slab(i, q, k, to):
            return _rc(out[i].at[k], out[i].at[k], ssem.at[i, q], rsem.at[i, q], to)

        @pl.when(c == 1)
        def _():
            sends = [own(i, 0, (1 - x, y, 1)) for i in range(n)]
            for cp in sends:
                cp.start()
            fwd = [slab(i, 3, k_x, sib) for i in range(n)]
            for i in range(n):
                slab(i, 0, k_x, sib).wait_recv()
                fwd[i].start()
            for i in range(n):
                slab(i, 4, k_y, sib).wait_recv()
                slab(i, 5, k_d, sib).wait_recv()
            for cp in sends + fwd:
                cp.wait_send()

        @pl.when(c == 0)
        def _():
            sends = [own(i, 1, (x, 1 - y, 0)) for i in range(n)] + [own(i, 2, (1 - x, 1 - y, 0)) for i in range(n)]
            for cp in sends:
                cp.start()
            fwd = [slab(i, 4, k_y, sib) for i in range(n)] + [slab(i, 5, k_d, sib) for i in range(n)]
            for i in range(n):
                slab(i, 1, k_y, sib).wait_recv()
                fwd[i].start()
            for i in range(n):
                slab(i, 2, k_d, sib).wait_recv()
                fwd[n + i].start()
            for i in range(n):
                slab(i, 3, k_x, sib).wait_recv()
            for cp in sends + fwd:
                cp.wait_send()

        for cp in local:
            cp.wait()


def _swap_slabs(arrs, n_slabs, name, async_id=None):
    n = len(arrs)
    J = max(n_slabs, 1)

    def body(*refs):
        src, out = refs[:n], refs[n:2 * n]
        ssem, rsem = refs[2 * n:]
        x, y, c, (k_me, k_x, k_y, k_d) = _where_am_i()
        sib = (x, y, 1 - c)
        sel = (jnp.where(c == 1, k_y, k_me), jnp.where(c == 1, k_d, k_x))
        cps = []
        for i in range(n):
            for j in range(J):
                s = src[i].at[sel[j]] if n_slabs else src[i]
                d = out[i].at[j] if n_slabs else out[i]
                cps.append(_rc(s, d, ssem.at[i, j], rsem.at[i, j], sib))
        for cp in cps:
            cp.start()
        for cp in cps:
            cp.wait()

    return _exchange_call(
        body, arrs, [jax.ShapeDtypeStruct(((n_slabs,) if n_slabs else ()) + a.shape[-2:], a.dtype) for a in arrs],
        [pltpu.SemaphoreType.DMA((n, J)), pltpu.SemaphoreType.DMA((n, J))], name, async_id)


def _exchange_call(body, arrs, out_shape, sems, name, async_id):
    n = len(arrs)
    if async_id is None:
        return pl.pallas_call(
            body, name=name, in_specs=[ANY] * n, out_specs=[ANY] * len(out_shape), out_shape=out_shape,
            scratch_shapes=sems, compiler_params=pltpu.CompilerParams(has_side_effects=True))(*arrs)

    def seq_body(*refs):
        _handshake_all()
        body(*refs)

    return pl.kernel(
        seq_body, name=name, out_type=out_shape, mesh=plsc.ScalarSubcoreMesh(**SEQUENCER), scratch_types=sems,
        compiler_params=pltpu.CompilerParams(collective_id=async_id))(*arrs)


def _send_chip_sums(arrs, name, async_id=None):
    n = len(arrs)

    def body(*refs):
        src, out = refs[:n], refs[n:2 * n]
        ssem, rsem = refs[2 * n:]
        x, y, c, _ = _where_am_i()

        @pl.when(c == 1)
        def _():
            cps = [_rc(src[i].at[1], out[i].at[0], ssem.at[i, 0], rsem.at[i, 0], (1 - x, y, 1)) for i in range(n)]
            for cp in cps:
                cp.start()
            for cp in cps:
                cp.wait()

        @pl.when(c == 0)
        def _():
            cps = [_rc(src[i].at[0], out[i].at[0], ssem.at[i, 0], rsem.at[i, 0], (x, 1 - y, 0)) for i in range(n)]
            cps += [_rc(src[i].at[1], out[i].at[1], ssem.at[i, 1], rsem.at[i, 1], (1 - x, 1 - y, 0)) for i in range(n)]
            for cp in cps:
                cp.start()
            for cp in cps:
                cp.wait()

    return _exchange_call(
        body, arrs, [jax.ShapeDtypeStruct(a.shape, a.dtype) for a in arrs],
        [pltpu.SemaphoreType.DMA((n, 2)), pltpu.SemaphoreType.DMA((n, 2))], name, async_id)


def _chip_sum(g, recv, sel, name):
    _, r, C = g.shape
    tr = _rtile(r)

    def body(sel_ref, g_ref, r_ref, o32_ref, o16_ref):
        s = g_ref[...] + r_ref[...]
        o32_ref[...] = s
        o16_ref[...] = s.astype(BF16)

    blk = pl.BlockSpec((None, tr, C), lambda j, t, sel_ref: (j, t, 0))
    return pl.pallas_call(
        body, name=name,
        grid_spec=pltpu.PrefetchScalarGridSpec(
            num_scalar_prefetch=1, grid=(2, r // tr),
            in_specs=[pl.BlockSpec((None, tr, C), lambda j, t, sel_ref: (sel_ref[j], t, 0)), blk],
            out_specs=[blk, blk]),
        out_shape=[jax.ShapeDtypeStruct((2, r, C), F32), jax.ShapeDtypeStruct((2, r, C), BF16)],
        compiler_params=_params(("arbitrary", "arbitrary")),
    )(sel, g, recv)


def _cross_sum(s32, got, north, name):
    _, r, C = s32.shape
    tr = _rtile(r)

    def body(f_ref, p_ref, a_ref, b_ref, o_ref):
        a = a_ref[...].astype(F32)

        @pl.when(f_ref[0] == 1)
        def _():
            o_ref[...] = p_ref[...] + a

        @pl.when(f_ref[0] == 0)
        def _():
            o_ref[...] = a + b_ref[...].astype(F32)

    return pl.pallas_call(
        body, name=name,
        grid_spec=pltpu.PrefetchScalarGridSpec(
            num_scalar_prefetch=1, grid=(r // tr,),
            in_specs=[pl.BlockSpec((None, tr, C), lambda t, f: (0, t, 0)),
                      pl.BlockSpec((None, tr, C), lambda t, f: (0, t, 0)),
                      pl.BlockSpec((None, tr, C), lambda t, f: (1 - f[0], t, 0))],
            out_specs=pl.BlockSpec((tr, C), lambda t, f: (t, 0))),
        out_shape=jax.ShapeDtypeStruct((r, C), F32),
        compiler_params=_params(("arbitrary",)),
    )(north, s32, got, got)


def _reduce_scatter(grads, tag, async_ids=(None, None, None)):
    x, y, c, (k_me, k_x, k_y, k_d) = _where_am_i()
    sel = jnp.stack([jnp.where(c == 1, k_me, k_y), jnp.where(c == 1, k_x, k_d)]).astype(jnp.int32)
    north = jnp.reshape(c, (1,)).astype(jnp.int32)
    from_sib = _swap_slabs(grads, 2, f"swap_grad_slabs_{tag}", async_ids[0])
    sums = [_chip_sum(g, r, sel, f"chip_sum_{tag}{i}") for i, (g, r) in enumerate(zip(grads, from_sib))]
    got = _send_chip_sums([s16 for _, s16 in sums], f"send_chip_sums_{tag}", async_ids[1])
    parts = [_cross_sum(s32, b, north, f"cross_sum_{tag}{i}") for i, ((s32, _), b) in enumerate(zip(sums, got))]
    theirs = _swap_slabs(parts, 0, f"swap_parts_{tag}", async_ids[2])
    return list(zip(parts, theirs))


def _all_reduce_small(v):
    R, C = v.shape

    def body(v_ref, o_ref, buf, ssem, rsem, lsem):
        x, y, c = lax.axis_index("x"), lax.axis_index("y"), lax.axis_index("c")
        me, sib = (x, y, c), (x, y, 1 - c)
        chips = [(1 - x, y), (x, 1 - y), (1 - x, 1 - y)]

        def slot(px, py, pc):
            return buf.at[4 * px + 2 * py + pc]

        def copy(k, block, to, src=None):
            return _rc(slot(*block) if src is None else src, slot(*block), ssem.at[k], rsem.at[k], to)

        mine = pltpu.make_async_copy(v_ref, slot(*me), lsem)
        mine.start()
        first = [copy(0, me, sib, src=v_ref)] + [copy(1 + j, me, (*chip, c), src=v_ref) for j, chip in enumerate(chips)]
        for cp in first:
            cp.start()
        passed = [copy(4 + j, (*chip, c), sib) for j, chip in enumerate(chips)]
        for j, chip in enumerate(chips):
            copy(1 + j, (*chip, c), me).wait_recv()
            passed[j].start()
        copy(0, sib, me).wait_recv()
        for j, chip in enumerate(chips):
            copy(4 + j, (*chip, 1 - c), me).wait_recv()
        for cp in first + passed:
            cp.wait_send()
        mine.wait()
        acc = buf[0]
        for k in range(1, 8):
            acc = acc + buf[k]
        o_ref[...] = acc

    return pl.pallas_call(
        body, name="all_reduce_small",
        in_specs=[pl.BlockSpec(memory_space=pltpu.VMEM)], out_specs=pl.BlockSpec(memory_space=pltpu.VMEM),
        out_shape=jax.ShapeDtypeStruct((R, C), F32),
        scratch_shapes=[pltpu.VMEM((8, R, C), F32), pltpu.SemaphoreType.DMA((7,)), pltpu.SemaphoreType.DMA((7,)),
                        pltpu.SemaphoreType.DMA],
        compiler_params=pltpu.CompilerParams(has_side_effects=True, vmem_limit_bytes=VMEM_LIMIT),
    )(v)


WEIGHTS = ['ffn1_norm', 'ffn1_w_gate', 'ffn1_w_up', 'ffn1_w_down', 'mix_norm', 'w_in', 'conv_w', 'conv_b', 's5_A_re',
           's5_A_im', 's5_log_dt', 's5_B_re', 's5_B_im', 's5_C_re', 's5_C_im', 's5_D', 's5_w_glu', 's5_b_glu',
           'ssd_A_log', 'ssd_dt_bias', 'ssd_D', 'ssd_norm', 'w_proj_s5', 'w_proj_ssd', 'b_gate', 'w_out', 'ffn2_norm',
           'ffn2_w_gate', 'ffn2_w_up', 'ffn2_w_down', 'final_norm']
ARGS = ['x'] + WEIGHTS + ['loss_target'] + ['m_' + n for n in WEIGHTS] + ['v_' + n for n in WEIGHTS]
COL_SHARDED = ('ffn1_w_gate', 'ffn1_w_up', 'w_in', 'ffn2_w_gate', 'ffn2_w_up')
ROW_SHARDED = ('ffn1_w_down', 's5_w_glu', 'w_proj_s5', 'w_proj_ssd', 'w_out', 'ffn2_w_down')
MATRICES = COL_SHARDED + ROW_SHARDED
GATHER_FIRST = ('ffn1_w_gate', 'ffn1_w_up', 'ffn1_w_down')
GATHER_SECOND = ('w_in', 'conv_w', 's5_w_glu', 'w_proj_s5', 'w_proj_ssd', 'w_out')
GATHER_THIRD = ('ffn2_w_gate', 'ffn2_w_up', 'ffn2_w_down')
SMALL = [n for n in WEIGHTS if n not in MATRICES]


def _pack(arrs, width=1024):
    flat = jnp.concatenate([a.reshape(-1).astype(F32) for a in arrs])
    rows = -(-flat.shape[0] // (8 * width)) * 8
    return jnp.pad(flat, (0, rows * width - flat.shape[0])).reshape(rows, width)


def _unpack(packed, shapes):
    flat, out, o = packed.reshape(-1), [], 0
    for s in shapes:
        n = math.prod(s)
        out.append(flat[o:o + n].reshape(s))
        o += n
    return out


def kernel(x, ffn1_norm, ffn1_w_gate, ffn1_w_up, ffn1_w_down, mix_norm, w_in, conv_w, conv_b, s5_A_re, s5_A_im, s5_log_dt, s5_B_re, s5_B_im, s5_C_re, s5_C_im, s5_D, s5_w_glu, s5_b_glu, ssd_A_log, ssd_dt_bias, ssd_D, ssd_norm, w_proj_s5, w_proj_ssd, b_gate, w_out, ffn2_norm, ffn2_w_gate, ffn2_w_up, ffn2_w_down, final_norm, loss_target, m_ffn1_norm, m_ffn1_w_gate, m_ffn1_w_up, m_ffn1_w_down, m_mix_norm, m_w_in, m_conv_w, m_conv_b, m_s5_A_re, m_s5_A_im, m_s5_log_dt, m_s5_B_re, m_s5_B_im, m_s5_C_re, m_s5_C_im, m_s5_D, m_s5_w_glu, m_s5_b_glu, m_ssd_A_log, m_ssd_dt_bias, m_ssd_D, m_ssd_norm, m_w_proj_s5, m_w_proj_ssd, m_b_gate, m_w_out, m_ffn2_norm, m_ffn2_w_gate, m_ffn2_w_up, m_ffn2_w_down, m_final_norm, v_ffn1_norm, v_ffn1_w_gate, v_ffn1_w_up, v_ffn1_w_down, v_mix_norm, v_w_in, v_conv_w, v_conv_b, v_s5_A_re, v_s5_A_im, v_s5_log_dt, v_s5_B_re, v_s5_B_im, v_s5_C_re, v_s5_C_im, v_s5_D, v_s5_w_glu, v_s5_b_glu, v_ssd_A_log, v_ssd_dt_bias, v_ssd_D, v_ssd_norm, v_w_proj_s5, v_w_proj_ssd, v_b_gate, v_w_out, v_ffn2_norm, v_ffn2_w_gate, v_ffn2_w_up, v_ffn2_w_down, v_final_norm):
    a = dict(locals())
    assert list(a) == ARGS
    x, target = a['x'][0], a['loss_target'][0]
    k_me = 2 * lax.axis_index("x") + lax.axis_index("y")

    src = {n: a[n][0].T.astype(BF16) for n in COL_SHARDED}
    src.update({n: a[n][0].astype(BF16) for n in ROW_SHARDED})
    src['conv_w'] = a['conv_w'][0]
    def flat(f):
        return f.reshape(-1, f.shape[-1])

    full = _gather_shards([src[n] for n in GATHER_FIRST])
    w = {n: flat(f) for n, f in zip(GATHER_FIRST, full)}
    got, _ = _gather_shards_async([src[n] for n in GATHER_SECOND], full[0][0, :8, :128].astype(F32), "gather_mixer", 1)
    w.update({n: flat(f) for n, f in zip(GATHER_SECOND, got)})
    w['conv_w'] = w['conv_w'].reshape(4, CONV_K, -1).transpose(1, 0, 2).reshape(CONV_K, CONV_DIM)

    def gather_last(after):
        got, _ = _gather_shards_async([src[n] for n in GATHER_THIRD], after[:8, :128], "gather_ffn2", 2)
        return {n: flat(f) for n, f in zip(GATHER_THIRD, got)}

    p = {n: a[n] for n in SMALL if n != 'conv_w'}
    p['final_norm'] = a['final_norm'][None]

    loss, grad_x, gp, gw = _local_step(x, target, p, w, gather_last)

    pairs = {}
    for tag, names, ids in (("ffn2", GATHER_THIRD, (3, 4, 5)), ("mixer", [n for n in GATHER_SECOND if n != 'conv_w'], (6, 7, 8)),
                            ("ffn1", GATHER_FIRST, (None, None, None))):
        pairs.update(zip(names, _reduce_scatter([gw[n].reshape(4, -1, D_MODEL) for n in names], tag, ids)))
    pairs = [pairs[n] for n in MATRICES]
    gp['conv_w'] = gw['conv_w'][None]
    small_shapes = [(1, 1)] + [gp[n].shape if n != 'final_norm' else (1, D_MODEL) for n in SMALL]
    red = _unpack(_all_reduce_small(_pack([loss] + [gp[n] for n in SMALL])), small_shapes)
    loss_all = red[0].reshape(())
    gsmall = dict(zip(SMALL, red[1:]))
    gsmall['conv_w'] = lax.dynamic_slice_in_dim(gsmall['conv_w'], k_me * 1024, 1024, axis=2)
    gsmall = {n: g.reshape(a[n].shape) for n, g in gsmall.items()}

    grads, delta, new_m, new_v = {}, {}, {}, {}
    for n, (mine, theirs) in zip(MATRICES, pairs):
        wn, mn, vn = a[n][0], a['m_' + n][0], a['v_' + n][0]
        if n in COL_SHARDED:
            (gt,) = _rowwise(lambda p_, q_: (p_ + q_,), [mine, theirs], [], [(D_MODEL, F32)], name=f"sum_{n}",
                             tm=_rtile(mine.shape[0]))
            g = gt.T
            d, m2, v2 = _adamw(wn, mn, vn, [g], name=f"adamw_{n}", emit_g=False)
        else:
            g, d, m2, v2 = _adamw(wn, mn, vn, [mine, theirs], name=f"adamw_{n}", emit_g=True)
        grads[n], delta[n], new_m[n], new_v[n] = g[None], d[None], m2[None], v2[None]
    sw, sm, sv, sg = [_pack([t[n] for n in SMALL]) for t in (a, {n: a['m_' + n] for n in SMALL},
                                                             {n: a['v_' + n] for n in SMALL}, gsmall)]
    d, m2, v2 = _adamw(sw, sm, sv, [sg], name="adamw_small", emit_g=False)
    shapes = [a[n].shape for n in SMALL]
    for n, dd, mm, vv in zip(SMALL, _unpack(d, shapes), _unpack(m2, shapes), _unpack(v2, shapes)):
        grads[n], delta[n], new_m[n], new_v[n] = gsmall[n], dd, mm, vv
    return (loss_all, grad_x[None], *[grads[n] for n in WEIGHTS], *[delta[n] for n in WEIGHTS],
            *[new_m[n] for n in WEIGHTS], *[new_v[n] for n in WEIGHTS])
```

```python
import itertools
import math

import jax
import jax.numpy as jnp
from jax import lax
from jax.experimental import pallas as pl
from jax.experimental.pallas import tpu as pltpu
from jax.experimental.pallas import tpu_sc as plsc

F32 = jnp.float32
BF16 = jnp.bfloat16
MXU = BF16
HI = lax.Precision.HIGHEST

D_MODEL = 1024
D_FF = 2816
EPS = 1e-6
S5_GROUPS, S5_GROUP, S5_STATE = 64, 16, 64
D_INNER = 2048
SSD_HEADDIM, SSD_HEADS, SSD_GROUPS, SSD_HPG, SSD_STATE = 64, 32, 8, 4, 128
CONV_K, CONV_DIM = 4, 4096
IN_SPLITS = (1024, 2048, 4096, 32, 2048)
IN_OFFS = (0, 1024, 3072, 7168, 7200, 9248)
SSD_Q = 256
ADAM_LR, ADAM_B1, ADAM_B2, ADAM_EPS, ADAM_WD, ADAM_STEP = 0.001, 0.9, 0.999, 1e-08, 0.01, 10

VMEM_LIMIT = 56 * 1024 * 1024
MATMUL_VMEM = 32 * 1024 * 1024
MESH = pl.DeviceIdType.MESH

NN = ((1,), (0,))
NT = ((1,), (1,))
TN = ((0,), (0,))


def _dot(a, b, dims, precision=None):
    if precision is None:
        a, b = a.astype(MXU), b.astype(MXU)
    return lax.dot_general(a, b, (dims, ((), ())), precision=precision, preferred_element_type=F32)


def _tile(n, pref):
    if n <= pref:
        return n
    best = None
    for t in range(128, pref + 1, 128):
        if n % t == 0:
            best = t
    assert best is not None, (n, pref)
    return best


def _params(sem):
    return pltpu.CompilerParams(dimension_semantics=sem, vmem_limit_bytes=VMEM_LIMIT)


def _matmul(a, b, mode, *, name, out=((F32),), epilogue=None, extras=(), tm=512, tn=1536):
    if mode == "nn":
        (M, K), (_, N) = a.shape, b.shape
    elif mode == "nt":
        (M, K), (N, _) = a.shape, b.shape
    else:
        (K, M), (_, N) = a.shape, b.shape
    tm, tn = _tile(M, tm), _tile(N, tn)

    def vmem(tm_, tn_):
        per = K * tm_ * a.dtype.itemsize + K * tn_ * b.dtype.itemsize
        per += sum(tm_ * tn_ * jnp.dtype(dt).itemsize for dt in out)
        per += sum((1 if e.shape[0] == 1 else tm_) * tn_ * e.dtype.itemsize for e in extras)
        return 2 * per

    while vmem(tm, tn) > MATMUL_VMEM and (tn > 128 or tm > 128):
        if tn >= tm and tn > 128:
            tn = _tile(N, tn - 128)
        else:
            tm = _tile(M, tm - 128)
    a_spec = pl.BlockSpec((K, tm), lambda i, j: (0, i)) if mode == "tn" else pl.BlockSpec((tm, K), lambda i, j: (i, 0))
    b_spec = pl.BlockSpec((tn, K), lambda i, j: (j, 0)) if mode == "nt" else pl.BlockSpec((K, tn), lambda i, j: (0, j))
    dims = {"nn": NN, "nt": NT, "tn": TN}[mode]
    e_specs = []
    for e in extras:
        if e.shape[0] == 1:
            e_specs.append(pl.BlockSpec((1, tn), lambda i, j: (0, j)))
        else:
            assert e.shape == (M, N), (e.shape, M, N)
            e_specs.append(pl.BlockSpec((tm, tn), lambda i, j: (i, j)))
    n_e, n_o = len(extras), len(out)

    def body(a_ref, b_ref, *refs):
        acc = _dot(a_ref[...], b_ref[...], dims)
        res = (acc,) if epilogue is None else epilogue(acc, *[r[...] for r in refs[:n_e]])
        for r, v in zip(refs[n_e:], res):
            r[...] = v.astype(r.dtype)

    res = pl.pallas_call(
        body, name=name, grid=(M // tm, N // tn),
        in_specs=[a_spec, b_spec] + e_specs,
        out_specs=[pl.BlockSpec((tm, tn), lambda i, j: (i, j)) for _ in range(n_o)],
        out_shape=[jax.ShapeDtypeStruct((M, N), dt) for dt in out],
        compiler_params=_params(("parallel", "parallel")),
    )(a, b, *extras)
    return res[0] if n_o == 1 else res


def _rowwise(fn, rows, fulls, outs, reds=(), *, name, tm=256):
    T = rows[0].shape[0]
    tm = min(tm, T)
    n_r, n_f, n_o, n_d = len(rows), len(fulls), len(outs), len(reds)

    def body(*refs):
        ins = [r[...] for r in refs[:n_r + n_f]]
        o_refs = refs[n_r + n_f:n_r + n_f + n_o]
        d_refs = refs[n_r + n_f + n_o:]
        res = fn(*ins)
        for r, v in zip(o_refs, res[:n_o]):
            r[...] = v.astype(r.dtype)
        if n_d:
            @pl.when(pl.program_id(0) == 0)
            def _():
                for r in d_refs:
                    r[...] = jnp.zeros_like(r)
            for r, v in zip(d_refs, res[n_o:]):
                r[...] += v.astype(r.dtype)

    res = pl.pallas_call(
        body, name=name, grid=(T // tm,),
        in_specs=[pl.BlockSpec((tm, r.shape[1]), lambda i: (i, 0)) for r in rows]
        + [pl.BlockSpec(f.shape, lambda i, nd=f.ndim: (0,) * nd) for f in fulls],
        out_specs=[pl.BlockSpec((tm, c), lambda i: (i, 0)) for c, _ in outs]
        + [pl.BlockSpec(s, lambda i, nd=len(s): (0,) * nd) for s, _ in reds],
        out_shape=[jax.ShapeDtypeStruct((T, c), dt) for c, dt in outs]
        + [jax.ShapeDtypeStruct(s, dt) for s, dt in reds],
        compiler_params=_params(("arbitrary",)),
    )(*rows, *fulls)
    return res


def _rms(x, g):
    return x * lax.rsqrt(jnp.mean(x * x, axis=-1, keepdims=True) + EPS) * g


def _colsum(v):
    return jnp.sum(v, axis=0, keepdims=True)


def _softplus(x):
    return jnp.maximum(x, 0.0) + jnp.log1p(jnp.exp(-jnp.abs(x)))


def _ffn_fwd(x, norm, wgT, wuT, wd, tag):
    D = x.shape[1]
    (hn,) = _rowwise(lambda xt, g: (_rms(xt, g),), [x], [norm], [(D, BF16)], name=f"{tag}_norm")
    a = _matmul(hn, wgT, "nt", name=f"{tag}_gate")
    b, hid = _matmul(hn, wuT, "nt", name=f"{tag}_up", out=(BF16, BF16),
                     epilogue=lambda acc, at: (acc, jax.nn.silu(at) * acc), extras=[a])
    y = _matmul(hid, wd, "nn", name=f"{tag}_down", epilogue=lambda acc, xt: (xt + 0.5 * acc,), extras=[x])
    return y, (hn, a, b, hid)


def _ffn_bwd(dy, x, norm, wgT, wuT, wd, res, tag):
    hn, a, b, hid = res
    D, F = x.shape[1], a.shape[1]
    def act_bwd(acc, at, bt):
        _, vjp = jax.vjp(lambda p, q: jax.nn.silu(p) * q, at, bt.astype(F32))
        return vjp(0.5 * acc)

    da, db = _matmul(dy, wd, "nt", name=f"{tag}_d_hid", out=(BF16, BF16), epilogue=act_bwd, extras=[a, b])
    d_wd = _matmul(hid, dy, "tn", name=f"{tag}_d_wd", epilogue=lambda acc: (0.5 * acc,))
    d_wgT = _matmul(da, hn, "tn", name=f"{tag}_d_wg")
    d_wuT = _matmul(db, hn, "tn", name=f"{tag}_d_wu")
    dhn = _matmul(da, wgT, "nn", name=f"{tag}_d_hn1")
    dhn = _matmul(db, wuT, "nn", name=f"{tag}_d_hn2", epilogue=lambda acc, e: (acc + e,), extras=[dhn])

    def norm_bwd(xt, dh, dyt, g):
        _, vjp = jax.vjp(_rms, xt, g)
        dx, dg = vjp(dh)
        return dyt + dx, dg

    dx, d_norm = _rowwise(norm_bwd, [x, dhn, dy], [norm], [(D, F32)], [((1, D), F32)], name=f"{tag}_d_norm")
    return dx, d_norm, d_wgT, d_wuT, d_wd


S5_TILES = 8
S5_HALF = 256


def _s5_derive(A_re, A_im, log_dt, B_re, B_im, C_re, C_im):
    G, N, M = S5_GROUPS, S5_STATE, S5_GROUP
    dt = jnp.exp(log_dt)[:, None]
    mag = jnp.exp(A_re * dt)
    ar, ai = mag * jnp.cos(A_im * dt), mag * jnp.sin(A_im * dt)
    den = A_re * A_re + A_im * A_im
    cr = ((ar - 1.0) * A_re + ai * A_im) / den
    ci = (ai * A_re - (ar - 1.0) * A_im) / den
    bbr = cr[..., None] * B_re - ci[..., None] * B_im
    bbi = cr[..., None] * B_im + ci[..., None] * B_re
    eye = jnp.eye(8, dtype=F32)

    def tile_in(bb):
        t = bb.reshape(S5_TILES, 8, N, M).transpose(0, 1, 3, 2)
        return jnp.einsum("jamn,ab->jambn", t, eye).reshape(S5_TILES, 8 * M, 8 * N)

    def tile_out(c):
        t = c.reshape(S5_TILES, 8, M, N).transpose(0, 1, 3, 2)
        return jnp.einsum("janm,ab->janbm", t, eye).reshape(S5_TILES, 8 * N, 8 * M)

    return (tile_in(bbr), tile_in(bbi), tile_out(C_re), tile_out(C_im),
            ar.reshape(S5_TILES, 1, 8 * N), ai.reshape(S5_TILES, 1, 8 * N))


S5_NB = S5_HALF // 128
S5_SEG = 32


def _cmul(ar, ai, br, bi):
    return ar * br - ai * bi, ar * bi + ai * br


def _s5_scan(sr_ref, si_ref, ar, ai, T, reverse):
    L, V = T // S5_SEG, S5_SEG // 8
    assert L * S5_SEG == T and L & (L - 1) == 0, T
    sg = -1.0 if reverse else 1.0
    a_r = [jnp.broadcast_to(ar[:, 128 * b:128 * (b + 1)], (8, 128)) for b in range(S5_NB)]
    a_i = [jnp.broadcast_to(sg * ai[:, 128 * b:128 * (b + 1)], (8, 128)) for b in range(S5_NB)]

    def rows(k, v):
        return pl.ds(pl.multiple_of(((L - 1 - k) if reverse else k) * S5_SEG + 8 * v, 8), 8)

    def local(k, carry):
        out = []
        for b in range(S5_NB):
            for v in range(V):
                idx = rows(k, v)
                mr, mi = _cmul(a_r[b], a_i[b], *carry[b * V + v])
                nr, ni = mr + sr_ref[b, idx, :], mi + si_ref[b, idx, :]
                sr_ref[b, idx, :] = nr
                si_ref[b, idx, :] = ni
                out.append((nr, ni))
        return tuple(out)

    z = jnp.zeros((8, 128), F32)
    ends = lax.fori_loop(0, L, local, tuple((z, z) for _ in range(S5_NB * V)))

    carries = []
    for b in range(S5_NB):
        pr, pi = a_r[b][0:1], a_i[b][0:1]
        n = L
        while n > 1:
            pr, pi = _cmul(pr, pi, pr, pi)
            n //= 2
        cr, ci = jnp.zeros((1, 128), F32), jnp.zeros((1, 128), F32)
        into = [None] * S5_SEG
        for j in (reversed(range(S5_SEG)) if reverse else range(S5_SEG)):
            into[j] = (cr, ci)
            er, ei = ends[b * V + j // 8]
            mr, mi = _cmul(pr, pi, cr, ci)
            cr, ci = mr + er[j % 8:j % 8 + 1], mi + ei[j % 8:j % 8 + 1]
        carries.append([(jnp.concatenate([into[8 * v + s][0] for s in range(8)], axis=0),
                         jnp.concatenate([into[8 * v + s][1] for s in range(8)], axis=0)) for v in range(V)])

    def fix(k, powers):
        out = []
        for b in range(S5_NB):
            pr, pi = powers[b]
            for v in range(V):
                idx = rows(k, v)
                dr, di = _cmul(pr, pi, *carries[b][v])
                sr_ref[b, idx, :] += dr
                si_ref[b, idx, :] += di
            out.append(_cmul(pr, pi, a_r[b], a_i[b]))
        return tuple(out)

    lax.fori_loop(0, L, fix, tuple((a_r[b], a_i[b]) for b in range(S5_NB)))


def _s5_in(ut, b_ref, s_ref):
    for b in range(S5_NB):
        s_ref[b] = _dot(ut, b_ref[:, 128 * b:128 * (b + 1)], NN)


def _to_seg(v):
    T, C = v.shape
    return v.reshape(S5_SEG, T // S5_SEG, C).transpose(1, 0, 2).reshape(T, C)


def _from_seg(v):
    T, C = v.shape
    return v.reshape(T // S5_SEG, S5_SEG, C).transpose(1, 0, 2).reshape(T, C)


def _s5_specs(T):
    u_spec = pl.BlockSpec((T, 128), lambda j, h: (0, j))
    b_spec = pl.BlockSpec((None, 128, S5_HALF), lambda j, h: (j, 0, h))
    c_spec = pl.BlockSpec((None, S5_HALF, 128), lambda j, h: (j, h, 0))
    a_spec = pl.BlockSpec((None, 1, S5_HALF), lambda j, h: (j, 0, h))
    return u_spec, b_spec, c_spec, a_spec


def _s5_fwd(u, tiles):
    T = u.shape[0]
    u_spec, b_spec, c_spec, a_spec = _s5_specs(T)

    def body(u_ref, br_ref, bi_ref, cr_ref, ci_ref, ar_ref, ai_ref, y_ref, sr_ref, si_ref):
        ut = u_ref[...].astype(MXU)
        _s5_in(ut, br_ref, sr_ref)
        _s5_in(ut, bi_ref, si_ref)
        _s5_scan(sr_ref, si_ref, ar_ref[...], ai_ref[...], T, False)
        y = None
        for b in range(S5_NB):
            blk = slice(128 * b, 128 * (b + 1))
            yb = _dot(sr_ref[b], cr_ref[blk, :], NN) - _dot(si_ref[b], ci_ref[blk, :], NN)
            y = yb if y is None else y + yb

        @pl.when(pl.program_id(1) == 0)
        def _():
            y_ref[...] = y

        @pl.when(pl.program_id(1) == 1)
        def _():
            y_ref[...] += y

    scr = pltpu.VMEM((S5_NB, T, 128), F32)
    return pl.pallas_call(
        body, name="s5_fwd", grid=(S5_TILES, 2),
        in_specs=[u_spec, b_spec, b_spec, c_spec, c_spec, a_spec, a_spec],
        out_specs=u_spec, out_shape=jax.ShapeDtypeStruct(u.shape, F32),
        scratch_shapes=[scr, scr],
        compiler_params=_params(("parallel", "arbitrary")),
    )(u, *tiles)


def _s5_bwd(u, dys, du_skip, tiles):
    T = u.shape[0]
    u_spec, b_spec, c_spec, a_spec = _s5_specs(T)

    def body(u_ref, dy_ref, sk_ref, br_ref, bi_ref, cr_ref, ci_ref, ar_ref, ai_ref,
             du_ref, dbr_ref, dbi_ref, dcr_ref, dci_ref, dar_ref, dai_ref, sr_ref, si_ref, lr_ref, li_ref):
        ut, dy = u_ref[...].astype(MXU), dy_ref[...].astype(MXU)
        ar, ai = ar_ref[...], ai_ref[...]
        _s5_in(ut, br_ref, sr_ref)
        _s5_in(ut, bi_ref, si_ref)
        _s5_scan(sr_ref, si_ref, ar, ai, T, False)
        for b in range(S5_NB):
            blk = slice(128 * b, 128 * (b + 1))
            dcr_ref[blk, :] = _dot(sr_ref[b], dy, TN)
            dci_ref[blk, :] = -_dot(si_ref[b], dy, TN)
            lr_ref[b] = _dot(dy, cr_ref[blk, :], NT)
            li_ref[b] = -_dot(dy, ci_ref[blk, :], NT)
        _s5_scan(lr_ref, li_ref, ar, ai, T, True)
        du = None
        head = lax.broadcasted_iota(jnp.int32, (T, 128), 0) < S5_SEG
        seg0 = lax.broadcasted_iota(jnp.int32, (S5_SEG, 128), 0) == 0
        for b in range(S5_NB):
            blk = slice(128 * b, 128 * (b + 1))
            lr, li = lr_ref[b], li_ref[b]
            pr = jnp.where(head, 0.0, pltpu.roll(sr_ref[b], S5_SEG, 0))
            pi = jnp.where(head, 0.0, pltpu.roll(si_ref[b], S5_SEG, 0))
            er = jnp.where(seg0, 0.0, pltpu.roll(sr_ref[b, T - S5_SEG:T, :], 1, 0))
            ei = jnp.where(seg0, 0.0, pltpu.roll(si_ref[b, T - S5_SEG:T, :], 1, 0))
            hr, hi = lr[0:S5_SEG], li[0:S5_SEG]
            dar_ref[:, blk] = _colsum(lr * pr + li * pi) + _colsum(hr * er + hi * ei)
            dai_ref[:, blk] = _colsum(li * pr - lr * pi) + _colsum(hi * er - hr * ei)
            dbr_ref[:, blk] = _dot(ut, lr, TN)
            dbi_ref[:, blk] = _dot(ut, li, TN)
            dub = _dot(lr, br_ref[:, blk], NT) + _dot(li, bi_ref[:, blk], NT)
            du = dub if du is None else du + dub

        @pl.when(pl.program_id(1) == 0)
        def _():
            du_ref[...] = du + sk_ref[...]

        @pl.when(pl.program_id(1) == 1)
        def _():
            du_ref[...] += du

    scr = pltpu.VMEM((S5_NB, T, 128), F32)
    return pl.pallas_call(
        body, name="s5_bwd", grid=(S5_TILES, 2),
        in_specs=[u_spec, u_spec, u_spec, b_spec, b_spec, c_spec, c_spec, a_spec, a_spec],
        out_specs=[u_spec, b_spec, b_spec, c_spec, c_spec, a_spec, a_spec],
        out_shape=[jax.ShapeDtypeStruct(u.shape, F32)] + [jax.ShapeDtypeStruct(t.shape, F32) for t in tiles],
        scratch_shapes=[scr, scr, scr, scr],
        compiler_params=_params(("parallel", "arbitrary")),
    )(u, dys, du_skip, *tiles)


CONV_TILE = 256


def _conv_pre(x, w, b):
    T = x.shape[0]
    row = lax.broadcasted_iota(jnp.int32, x.shape, 0)
    acc = x * w[CONV_K - 1:CONV_K, :] + b
    for lag in range(1, CONV_K):
        acc = acc + jnp.where(row >= lag, pltpu.roll(x, lag, 0), 0.0) * w[CONV_K - 1 - lag:CONV_K - lag, :]
    return acc


def _conv_fwd(x, w, b):
    T, C = x.shape
    col = pl.BlockSpec((T, CONV_TILE), lambda j: (0, j))

    def body(x_ref, w_ref, b_ref, o_ref):
        o_ref[...] = jax.nn.silu(_conv_pre(x_ref[...], w_ref[...], b_ref[...]))

    return pl.pallas_call(
        body, name="conv_fwd", grid=(C // CONV_TILE,),
        in_specs=[col, pl.BlockSpec((CONV_K, CONV_TILE), lambda j: (0, j)), pl.BlockSpec((1, CONV_TILE), lambda j: (0, j))],
        out_specs=col, out_shape=jax.ShapeDtypeStruct((T, C), F32),
        compiler_params=_params(("parallel",)),
    )(x, w, b)


def _conv_bwd(x, w, b, dout, col0, name):
    T, C = dout.shape
    off = col0 // CONV_TILE
    xcol = pl.BlockSpec((T, CONV_TILE), lambda j: (0, j + off))
    dcol = pl.BlockSpec((T, CONV_TILE), lambda j: (0, j))

    def body(x_ref, w_ref, b_ref, d_ref, dx_ref, dw_ref, db_ref):
        x, w = x_ref[...], w_ref[...]
        _, vjp = jax.vjp(jax.nn.silu, _conv_pre(x, w, b_ref[...]))
        (dy,) = vjp(d_ref[...])
        row = lax.broadcasted_iota(jnp.int32, x.shape, 0)
        dx = dy * w[CONV_K - 1:CONV_K, :]
        dw_ref[CONV_K - 1:CONV_K, :] = _colsum(dy * x)
        for lag in range(1, CONV_K):
            dx = dx + jnp.where(row < T - lag, pltpu.roll(dy, T - lag, 0), 0.0) * w[CONV_K - 1 - lag:CONV_K - lag, :]
            xs = jnp.where(row >= lag, pltpu.roll(x, lag, 0), 0.0)
            dw_ref[CONV_K - 1 - lag:CONV_K - lag, :] = _colsum(dy * xs)
        dx_ref[...] = dx.astype(dx_ref.dtype)
        db_ref[...] = _colsum(dy)

    return pl.pallas_call(
        body, name=name, grid=(C // CONV_TILE,),
        in_specs=[xcol, pl.BlockSpec((CONV_K, CONV_TILE), lambda j: (0, j + off)),
                  pl.BlockSpec((1, CONV_TILE), lambda j: (0, j + off)), dcol],
        out_specs=[dcol, pl.BlockSpec((CONV_K, CONV_TILE), lambda j: (0, j)), pl.BlockSpec((1, CONV_TILE), lambda j: (0, j))],
        out_shape=[jax.ShapeDtypeStruct((T, C), BF16), jax.ShapeDtypeStruct((CONV_K, C), F32), jax.ShapeDtypeStruct((1, C), F32)],
        compiler_params=_params(("parallel",)),
    )(x, w, b, dout)


def _ssd_common(x_ref, b_ref, c_ref, dtr_ref, bias_ref, alog_ref, tri_ref, triu_ref):
    Q = x_ref.shape[0]
    x, Bm, Cm = x_ref[...], b_ref[...], c_ref[...]
    pre = dtr_ref[...] + bias_ref[...]
    dt = _softplus(pre)
    A = -jnp.exp(alog_ref[...])
    adt = dt * A
    ac4 = _dot(tri_ref[...], adt, NN, HI)
    ar4 = _dot(adt, triu_ref[...], TN, HI)
    cb = _dot(Cm, Bm, NT)
    ii = lax.broadcasted_iota(jnp.int32, (Q, Q), 0)
    jj = lax.broadcasted_iota(jnp.int32, (Q, Q), 1)
    atot4 = ac4[Q - 1:Q, :]
    return x, Bm, Cm, pre, dt, A, ac4, ar4, cb, ii >= jj, atot4


def _ssd_decay(k, ac4, ar4, causal):
    seg = ac4[:, k:k + 1] - ar4[k:k + 1, :]
    return jnp.where(causal, jnp.exp(jnp.where(causal, seg, 0.0)), 0.0)


def _per_head(c4, n):
    lane = lax.broadcasted_iota(jnp.int32, (c4.shape[0], n), 1)
    out = jnp.broadcast_to(c4[:, 0:1], (c4.shape[0], n))
    for k in range(1, SSD_HPG):
        out = jnp.where(lane >= SSD_HEADDIM * k, c4[:, k:k + 1], out)
    return out


def _head_sums(v):
    lane = lax.broadcasted_iota(jnp.int32, (v.shape[0], SSD_HPG), 1)
    out = jnp.zeros((v.shape[0], SSD_HPG), F32)
    for k in range(SSD_HPG):
        s = jnp.sum(v[:, SSD_HEADDIM * k:SSD_HEADDIM * (k + 1)], axis=1, keepdims=True)
        out = jnp.where(lane == k, s, out)
    return out


def _head_rows(c4):
    n = SSD_HPG * SSD_HEADDIM
    row = lax.broadcasted_iota(jnp.int32, (n, 1), 0)
    out = jnp.broadcast_to(c4[:, 0:1], (n, 1))
    for k in range(1, SSD_HPG):
        out = jnp.where(row >= SSD_HEADDIM * k, c4[:, k:k + 1], out)
    return out


def _ssd_specs(T, Q, rev):
    NC = T // Q
    cc = (lambda c: NC - 1 - c) if rev else (lambda c: c)
    x_spec = pl.BlockSpec((Q, SSD_HPG * SSD_HEADDIM), lambda g, c: (cc(c), g))
    b_spec = pl.BlockSpec((Q, SSD_STATE), lambda g, c: (cc(c), D_INNER // SSD_STATE + g))
    c_spec = pl.BlockSpec((Q, SSD_STATE), lambda g, c: (cc(c), D_INNER // SSD_STATE + SSD_GROUPS + g))
    dt_spec = pl.BlockSpec((None, Q, SSD_HPG), lambda g, c: (g, cc(c), 0))
    p_spec = pl.BlockSpec((None, 1, SSD_HPG), lambda g, c: (g, 0, 0))
    tri_spec = pl.BlockSpec((Q, Q), lambda g, c: (0, 0))
    h_spec = pl.BlockSpec((None, None, SSD_HPG * SSD_HEADDIM, SSD_STATE), lambda g, c: (g, cc(c), 0, 0))
    return x_spec, b_spec, c_spec, dt_spec, p_spec, tri_spec, h_spec


def _tri(Q):
    tri = jnp.tril(jnp.ones((Q, Q), F32))
    return tri, tri.T


def _ssd_fwd(xc, dtr8, bias8, alog8, dsk8):
    T, Q = xc.shape[0], SSD_Q
    NC = T // Q
    P, W = SSD_HEADDIM, SSD_HPG * SSD_HEADDIM
    x_spec, b_spec, c_spec, dt_spec, p_spec, tri_spec, h_spec = _ssd_specs(T, Q, False)

    def body(x_ref, b_ref, c_ref, dtr_ref, bias_ref, alog_ref, dsk_ref, tri_ref, triu_ref, y_ref, hs_ref, h_scr):
        @pl.when(pl.program_id(1) == 0)
        def _():
            h_scr[...] = jnp.zeros_like(h_scr)

        hin = h_scr[...]
        hs_ref[...] = hin
        x, Bm, Cm, pre, dt, A, ac4, ar4, cb, causal, atot4 = _ssd_common(
            x_ref, b_ref, c_ref, dtr_ref, bias_ref, alog_ref, tri_ref, triu_ref)
        xdt = x * _per_head(dt, W)
        rest = _dot(Cm, hin, NT) * _per_head(jnp.exp(ac4), W) + _per_head(dsk_ref[...], W) * x
        for k in range(SSD_HPG):
            blk = slice(P * k, P * (k + 1))
            G = cb * _ssd_decay(k, ac4, ar4, causal)
            y_ref[:, blk] = _dot(G, xdt[:, blk], NN) + rest[:, blk]
        xw = xdt * _per_head(jnp.exp(atot4 - ac4), W)
        h_scr[...] = _head_rows(jnp.exp(atot4)) * hin + _dot(xw, Bm, TN)

    tri, triu = _tri(Q)
    return pl.pallas_call(
        body, name="ssd_fwd", grid=(SSD_GROUPS, NC),
        in_specs=[x_spec, b_spec, c_spec, dt_spec, p_spec, p_spec, p_spec, tri_spec, tri_spec],
        out_specs=[pl.BlockSpec((Q, W), lambda g, c: (c, g)), h_spec],
        out_shape=[jax.ShapeDtypeStruct((T, D_INNER), F32),
                   jax.ShapeDtypeStruct((SSD_GROUPS, NC, W, SSD_STATE), F32)],
        scratch_shapes=[pltpu.VMEM((W, SSD_STATE), F32)],
        compiler_params=_params(("parallel", "arbitrary")),
    )(xc, xc, xc, dtr8, bias8, alog8, dsk8, tri, triu)


def _ssd_bwd(xc, dtr8, bias8, alog8, dsk8, hs, dy):
    T, Q = xc.shape[0], SSD_Q
    NC = T // Q
    P, W = SSD_HEADDIM, SSD_HPG * SSD_HEADDIM
    x_spec, b_spec, c_spec, dt_spec, p_spec, tri_spec, h_spec = _ssd_specs(T, Q, True)
    dy_spec = pl.BlockSpec((Q, W), lambda g, c: (NC - 1 - c, g))
    dbc_spec = pl.BlockSpec((Q, SSD_STATE), lambda g, c: (NC - 1 - c, g))

    def body(x_ref, b_ref, c_ref, dtr_ref, bias_ref, alog_ref, dsk_ref, tri_ref, triu_ref, hs_ref, dy_ref,
             dx_ref, db_ref, dc_ref, ddtr_ref, dbias_ref, dalog_ref, ddsk_ref, dh_scr):
        @pl.when(pl.program_id(1) == 0)
        def _():
            dh_scr[...] = jnp.zeros_like(dh_scr)
            dbias_ref[...] = jnp.zeros_like(dbias_ref)
            dalog_ref[...] = jnp.zeros_like(dalog_ref)
            ddsk_ref[...] = jnp.zeros_like(ddsk_ref)

        x, Bm, Cm, pre, dt, A, ac4, ar4, cb, causal, atot4 = _ssd_common(
            x_ref, b_ref, c_ref, dtr_ref, bias_ref, alog_ref, tri_ref, triu_ref)
        dyv, hin, dho = dy_ref[...], hs_ref[...], dh_scr[...]
        dt_w = _per_head(dt, W)
        xdt = x * dt_w
        E4, F4, etot4 = jnp.exp(ac4), jnp.exp(atot4 - ac4), jnp.exp(atot4)
        F_w = _per_head(F4, W)
        ddsk_ref[...] += _head_sums(_colsum(dyv * x))
        Z = _dot(Cm, hin, NT)
        dZ = dyv * _per_head(E4, W)
        dac4 = _head_sums(dyv * Z) * E4
        dC = _dot(dZ, hin, NN)
        dh_scr[...] = _dot(dZ, Cm, TN) + _head_rows(etot4) * dho
        per_row = jnp.sum(dho * hin, axis=1, keepdims=True)
        lane4 = lax.broadcasted_iota(jnp.int32, (1, SSD_HPG), 1)
        datot4 = jnp.zeros((1, SSD_HPG), F32)
        for k in range(SSD_HPG):
            datot4 = jnp.where(lane4 == k, jnp.sum(per_row[P * k:P * (k + 1)], keepdims=True), datot4)
        datot4 = datot4 * etot4
        dxw = _dot(Bm, dho, NT)
        dB = _dot(xdt * F_w, dho, NN)
        dFa4 = _head_sums(dxw * xdt) * F4
        datot4 = datot4 + _colsum(dFa4)
        dac4 = dac4 - dFa4
        dcb = jnp.zeros((Q, Q), F32)
        lane_q4 = lax.broadcasted_iota(jnp.int32, (Q, SSD_HPG), 1)
        sub_4q = lax.broadcasted_iota(jnp.int32, (SSD_HPG, Q), 0)
        row_sums = jnp.zeros((Q, SSD_HPG), F32)
        col_sums = jnp.zeros((SSD_HPG, Q), F32)
        dxdt_heads = []
        for k in range(SSD_HPG):
            blk = slice(P * k, P * (k + 1))
            L = _ssd_decay(k, ac4, ar4, causal)
            G = cb * L
            dG = _dot(dyv[:, blk], xdt[:, blk], NT)
            dxdt_heads.append(_dot(G, dyv[:, blk], TN))
            dcb = dcb + dG * L
            Mseg = dG * G
            row_sums = jnp.where(lane_q4 == k, jnp.sum(Mseg, axis=1, keepdims=True), row_sums)
            col_sums = jnp.where(sub_4q == k, jnp.sum(Mseg, axis=0, keepdims=True), col_sums)
        dxdt = dxw * F_w + jnp.concatenate(dxdt_heads, axis=1)
        last = lax.broadcasted_iota(jnp.int32, (Q, SSD_HPG), 0) == Q - 1
        dac4 = dac4 + row_sums + jnp.where(last, datot4, 0.0)
        dadt4 = _dot(triu_ref[...], dac4, NN, HI) - _dot(triu_ref[...], col_sums, NT, HI)
        ddt4 = _head_sums(dxdt * x) + dadt4 * A
        dalog_ref[...] += _colsum(dadt4 * dt) * A
        ddtr4 = ddt4 * jax.nn.sigmoid(pre)
        dbias_ref[...] += _colsum(ddtr4)
        ddtr_ref[...] = ddtr4
        dx_ref[...] = (_per_head(dsk_ref[...], W) * dyv + dxdt * dt_w).astype(dx_ref.dtype)
        db_ref[...] = dB + _dot(dcb, Cm, TN)
        dc_ref[...] = dC + _dot(dcb, Bm, NN)

    tri, triu = _tri(Q)
    return pl.pallas_call(
        body, name="ssd_bwd", grid=(SSD_GROUPS, NC),
        in_specs=[x_spec, b_spec, c_spec, dt_spec, p_spec, p_spec, p_spec, tri_spec, tri_spec, h_spec, dy_spec],
        out_specs=[dy_spec, dbc_spec, dbc_spec, dt_spec, p_spec, p_spec, p_spec],
        out_shape=[jax.ShapeDtypeStruct((T, D_INNER), F32),
                   jax.ShapeDtypeStruct((T, SSD_GROUPS * SSD_STATE), F32),
                   jax.ShapeDtypeStruct((T, SSD_GROUPS * SSD_STATE), F32),
                   jax.ShapeDtypeStruct(dtr8.shape, F32)] + [jax.ShapeDtypeStruct(bias8.shape, F32)] * 3,
        scratch_shapes=[pltpu.VMEM((W, SSD_STATE), F32)],
        compiler_params=_params(("parallel", "arbitrary")),
    )(xc, xc, xc, dtr8, bias8, alog8, dsk8, tri, triu, hs, dy)


def _to_groups(v):
    return v.reshape(SSD_GROUPS, 1, SSD_HPG)


def _mixer_fwd(x, p, w):
    T, D = x.shape
    (hn,) = _rowwise(lambda xt, g: (_rms(xt, g),), [x], [p["mix_norm"]], [(D, BF16)], name="mix_norm")
    winT = w["w_in"]
    u, z, xbc, dtr, gl = [
        _matmul(hn, winT[IN_OFFS[i]:IN_OFFS[i + 1]], "nt", name=f"mix_in{i}") for i in range(5)]
    tiles = _s5_derive(*[p[k][0] for k in ("s5_A_re", "s5_A_im", "s5_log_dt", "s5_B_re", "s5_B_im", "s5_C_re", "s5_C_im")])
    u = _to_seg(u)
    ys = _s5_fwd(u, tiles)
    (g,) = _rowwise(lambda yt, ut, d: (jax.nn.gelu(yt + d * ut),), [ys, u], [p["s5_D"]], [(D, F32)], name="s5_gelu")
    y5 = _matmul(g, w["s5_w_glu"], "nn", name="s5_glu", out=(BF16,),
                 epilogue=lambda acc, gt, b: (gt * jax.nn.sigmoid(acc + b),), extras=[g, p["s5_b_glu"]])
    xc = _conv_fwd(xbc, w["conv_w"], p["conv_b"])
    dtr8 = dtr.reshape(T, SSD_GROUPS, SSD_HPG).transpose(1, 0, 2)
    ssd_p = [_to_groups(p[k]) for k in ("ssd_dt_bias", "ssd_A_log", "ssd_D")]
    yssd_raw, hs = _ssd_fwd(xc, dtr8, *ssd_p)
    (yssd,) = _rowwise(lambda yt, zt, nw: (_rms(yt * jax.nn.silu(zt), nw),), [yssd_raw, z], [p["ssd_norm"]],
                       [(D_INNER, BF16)], name="ssd_gate")
    p5 = _from_seg(_matmul(y5, w["w_proj_s5"], "nn", name="mix_p5"))
    pssd = _matmul(yssd, w["w_proj_ssd"], "nn", name="mix_pssd")

    def merge(glt, at, bt, bg):
        gates = jax.nn.sigmoid(glt + bg)
        return (gates[:, :D] * at + gates[:, D:] * bt,)

    (merged,) = _rowwise(merge, [gl, p5, pssd], [p["b_gate"]], [(D, BF16)], name="mix_merge")
    y = _matmul(merged, w["w_out"], "nn", name="mix_out", epilogue=lambda acc, xt: (xt + acc,), extras=[x])
    res = dict(hn=hn, u=u, z=z, xbc=xbc, gl=gl, tiles=tiles, ys=ys, g=g, y5=y5, xc=xc, dtr8=dtr8, ssd_p=ssd_p,
               yssd_raw=yssd_raw, hs=hs, yssd=yssd, p5=p5, pssd=pssd, merged=merged)
    return y, res


def _mixer_bwd(dy, x, p, w, r):
    T, D = x.shape
    gp = {}
    dmerged = _matmul(dy, w["w_out"], "nt", name="mix_d_merged")
    gp["w_out"] = _matmul(r["merged"], dy, "tn", name="mix_d_wout")

    def merge_bwd(glt, at, bt, dm, bg):
        def f(q, a_, b_):
            gates = jax.nn.sigmoid(q + bg)
            return gates[:, :D] * a_ + gates[:, D:] * b_
        _, vjp = jax.vjp(f, glt, at, bt)
        dq, da_, db_ = vjp(dm)
        return da_, db_, dq, _colsum(dq)

    dp5, dpssd, dgl, gp["b_gate"] = _rowwise(
        merge_bwd, [r["gl"], r["p5"], r["pssd"], dmerged], [p["b_gate"]],
        [(D, BF16), (D, BF16), (2 * D, BF16)], [((1, 2 * D), F32)], name="mix_d_merge")
    dp5 = _to_seg(dp5)
    dy5 = _matmul(dp5, w["w_proj_s5"], "nt", name="mix_d_y5")
    gp["w_proj_s5"] = _matmul(r["y5"], dp5, "tn", name="mix_d_wp5")
    dyssd = _matmul(dpssd, w["w_proj_ssd"], "nt", name="mix_d_yssd")
    gp["w_proj_ssd"] = _matmul(r["yssd"], dpssd, "tn", name="mix_d_wpssd")

    def gate_bwd(yt, zt, dyt, nw):
        _, vjp = jax.vjp(lambda a_, b_, c_: _rms(a_ * jax.nn.silu(b_), c_), yt, zt, nw)
        return vjp(dyt)

    dyraw, dz, gp["ssd_norm"] = _rowwise(
        gate_bwd, [r["yssd_raw"], r["z"], dyssd], [p["ssd_norm"]],
        [(D_INNER, F32), (D_INNER, BF16)], [((1, D_INNER), F32)], name="ssd_d_gate")
    dxs, dBm, dCm, ddtr8, dbias8, dalog8, ddsk8 = _ssd_bwd(r["xc"], r["dtr8"], *r["ssd_p"], r["hs"], dyraw)
    gp["ssd_dt_bias"], gp["ssd_A_log"], gp["ssd_D"] = [v.reshape(1, SSD_HEADS) for v in (dbias8, dalog8, ddsk8)]
    ddtr = ddtr8.transpose(1, 0, 2).reshape(T, SSD_HEADS)
    conv = [_conv_bwd(r["xbc"], w["conv_w"], p["conv_b"], d, c0, f"conv_bwd{i}")
            for i, (d, c0) in enumerate(((dxs, 0), (dBm, D_INNER), (dCm, D_INNER + SSD_GROUPS * SSD_STATE)))]
    dxbc = [c[0] for c in conv]
    gp["conv_w"] = jnp.concatenate([c[1] for c in conv], axis=1)
    gp["conv_b"] = jnp.concatenate([c[2] for c in conv], axis=1)

    g = r["g"]

    def glu_ep(acc, gt, b, dyt):
        _, vjp = jax.vjp(lambda g_, t_: g_ * jax.nn.sigmoid(t_ + b), gt, acc)
        return vjp(dyt)

    dg1, dt_ = _matmul(g, w["s5_w_glu"], "nn", name="s5_d_glu", out=(F32, BF16), epilogue=glu_ep,
                       extras=[g, p["s5_b_glu"], dy5])
    gp["s5_w_glu"] = _matmul(g, dt_, "tn", name="s5_d_wglu")
    dg = _matmul(dt_, w["s5_w_glu"], "nt", name="s5_d_g", epilogue=lambda acc, e: (acc + e,), extras=[dg1])

    def gelu_bwd(yt, ut, dgt, dtt, d):
        _, vjp = jax.vjp(lambda y_, d_: jax.nn.gelu(y_ + d_ * ut), yt, d)
        dys, dd = vjp(dgt)
        return dys, dys * d, dd, _colsum(dtt.astype(F32))

    dys, dusk, gp["s5_D"], gp["s5_b_glu"] = _rowwise(
        gelu_bwd, [r["ys"], r["u"], dg, dt_], [p["s5_D"]], [(D, F32), (D, F32)], [((1, D), F32), ((1, D), F32)],
        name="s5_d_gelu")
    du, *dtiles = _s5_bwd(r["u"], dys, dusk, r["tiles"])
    du = _from_seg(du)

    pieces = [du, dz, None, ddtr, dgl]
    winT = w["w_in"]
    hn = r["hn"]
    d_rows, dhn = [], None
    cols = [(du, 0, 1024), (dz, 1024, 3072), (dxbc[0], 3072, 5120), (dxbc[1], 5120, 6144), (dxbc[2], 6144, 7168),
            (ddtr, 7168, 7200), (dgl, 7200, 9248)]
    for i, (d, lo, hi) in enumerate(cols):
        d_rows.append(_matmul(d, hn, "tn", name=f"mix_d_win{i}"))
        if dhn is None:
            dhn = _matmul(d, winT[lo:hi], "nn", name=f"mix_d_hn{i}")
        else:
            dhn = _matmul(d, winT[lo:hi], "nn", name=f"mix_d_hn{i}", epilogue=lambda acc, e: (acc + e,), extras=[dhn])
    gp["w_in"] = jnp.concatenate(d_rows, axis=0)
    del pieces

    def norm_bwd(xt, dh, dyt, gn):
        _, vjp = jax.vjp(_rms, xt, gn)
        dx, dgn = vjp(dh)
        return dyt + dx, dgn

    dx, gp["mix_norm"] = _rowwise(norm_bwd, [x, dhn, dy], [p["mix_norm"]], [(D, F32)], [((1, D), F32)], name="mix_d_norm")
    return dx, gp, dtiles


def _s5_param_grads(p, dtiles):
    keys = ("s5_A_re", "s5_A_im", "s5_log_dt", "s5_B_re", "s5_B_im", "s5_C_re", "s5_C_im")
    _, vjp = jax.vjp(_s5_derive, *[p[k][0] for k in keys])
    return {k: v[None] for k, v in zip(keys, vjp(tuple(dtiles)))}


def _local_step(x, target, p, w, late_weights):
    T, D = x.shape
    x1, r1 = _ffn_fwd(x, p["ffn1_norm"], w["ffn1_w_gate"], w["ffn1_w_up"], w["ffn1_w_down"], "ffn1")
    x2, rm = _mixer_fwd(x1, p, w)
    w = {**w, **late_weights(rm["p5"])}
    x3, r2 = _ffn_fwd(x2, p["ffn2_norm"], w["ffn2_w_gate"], w["ffn2_w_up"], w["ffn2_w_down"], "ffn2")

    def head(xt, tt, g):
        def f(x_, g_):
            e = _rms(x_, g_) - tt
            return 0.5 * jnp.sum(jnp.mean(e * e, axis=-1))
        l, (dx_, dg_) = jax.value_and_grad(f, argnums=(0, 1))(xt, g)
        return dx_, l.reshape(1, 1), dg_

    dx3, loss, d_final = _rowwise(head, [x3, target], [p["final_norm"]], [(D, F32)],
                                  [((1, 1), F32), ((1, D), F32)], name="loss_head")
    gp, gw = {"final_norm": d_final}, {}
    dx2, gp["ffn2_norm"], gw["ffn2_w_gate"], gw["ffn2_w_up"], gw["ffn2_w_down"] = _ffn_bwd(
        dx3, x2, p["ffn2_norm"], w["ffn2_w_gate"], w["ffn2_w_up"], w["ffn2_w_down"], r2, "ffn2")
    dx1, gm, dtiles = _mixer_bwd(dx2, x1, p, w, rm)
    for k in ("w_out", "w_proj_s5", "w_proj_ssd", "s5_w_glu", "w_in", "conv_w"):
        gw[k] = gm.pop(k)
    gp.update(gm)
    gp.update(_s5_param_grads(p, dtiles))
    dx0, gp["ffn1_norm"], gw["ffn1_w_gate"], gw["ffn1_w_up"], gw["ffn1_w_down"] = _ffn_bwd(
        dx1, x, p["ffn1_norm"], w["ffn1_w_gate"], w["ffn1_w_up"], w["ffn1_w_down"], r1, "ffn1")
    return loss, dx0, gp, gw, dx1


def _adamw(w, m, v, gs, *, name, emit_g, tm=128):
    n_g = len(gs)
    C = w.shape[1]

    def fn(wt, mt, vt, *gt):
        g = gt[0]
        for e in gt[1:]:
            g = g + e
        m2 = ADAM_B1 * mt + (1.0 - ADAM_B1) * g
        v2 = ADAM_B2 * vt + (1.0 - ADAM_B2) * (g * g)
        m_hat = m2 / (1.0 - ADAM_B1 ** ADAM_STEP)
        v_hat = v2 / (1.0 - ADAM_B2 ** ADAM_STEP)
        delta = -ADAM_LR * (m_hat / (jnp.sqrt(v_hat) + ADAM_EPS) + ADAM_WD * wt)
        return ((g,) if emit_g else ()) + (delta, m2, v2)

    assert n_g >= 1
    return _rowwise(fn, [w, m, v, *gs], [], [(C, F32)] * (4 if emit_g else 3), name=name, tm=_rtile(w.shape[0], tm))


def _rtile(r, cap=512):
    if r <= cap:
        return r
    best = None
    for t in range(8, cap + 1, 8):
        if r % t == 0:
            best = t
    assert best is not None, r
    return best


ANY = pl.BlockSpec(memory_space=pl.ANY)


def _where_am_i():
    x, y, c = lax.axis_index("x"), lax.axis_index("y"), lax.axis_index("c")
    ks = (2 * x + y, 2 * (1 - x) + y, 2 * x + (1 - y), 2 * (1 - x) + (1 - y))
    return x, y, c, ks


def _rc(src, dst, ssem, rsem, to):
    return pltpu.make_async_remote_copy(src_ref=src, dst_ref=dst, send_sem=ssem, recv_sem=rsem,
                                        device_id=to, device_id_type=MESH)


def _gather_shards(srcs):
    n = len(srcs)

    def body(*refs):
        _gather_body(refs[:n], refs[n:2 * n], *refs[2 * n:], extra=())

    return pl.pallas_call(
        body, name="gather_shards",
        in_specs=[ANY] * n, out_specs=[ANY] * n,
        out_shape=[jax.ShapeDtypeStruct((4,) + s.shape, s.dtype) for s in srcs],
        scratch_shapes=[pltpu.SemaphoreType.DMA((n, 6)), pltpu.SemaphoreType.DMA((n, 6)), pltpu.SemaphoreType.DMA((n,))],
        compiler_params=pltpu.CompilerParams(has_side_effects=True),
    )(*srcs)


def _handshake_all():
    x, y, c = lax.axis_index("x"), lax.axis_index("y"), lax.axis_index("c")
    barrier = pltpu.get_barrier_semaphore()
    for dx, dy, dc in itertools.product((0, 1), repeat=3):
        if (dx, dy, dc) != (0, 0, 0):
            to = (1 - x if dx else x, 1 - y if dy else y, 1 - c if dc else c)
            pl.semaphore_signal(barrier, inc=1, device_id=to, device_id_type=MESH)
    pl.semaphore_wait(barrier, 7)


SEQUENCER = dict(axis_name="seq", num_cores=1)


def _gather_shards_async(srcs, token, name, collective_id):
    n = len(srcs)

    def body(*refs):
        tok, tok_out = refs[n], refs[2 * n + 1]
        ssem, rsem, lsem = refs[2 * n + 2:]
        _handshake_all()
        _gather_body(refs[:n], refs[n + 1:2 * n + 1], ssem, rsem, lsem,
                     extra=(pltpu.make_async_copy(tok, tok_out, lsem.at[n]),))

    res = pl.kernel(
        body, name=name,
        out_type=[jax.ShapeDtypeStruct((4,) + s.shape, s.dtype) for s in srcs]
        + [jax.ShapeDtypeStruct(token.shape, token.dtype)],
        mesh=plsc.ScalarSubcoreMesh(**SEQUENCER),
        scratch_types=[pltpu.SemaphoreType.DMA((n, 6)), pltpu.SemaphoreType.DMA((n, 6)), pltpu.SemaphoreType.DMA((n + 1,))],
        compiler_params=pltpu.CompilerParams(collective_id=collective_id),
    )(*srcs, token)
    return res[:n], res[n]


def _gather_body(src, out, ssem, rsem, lsem, extra):
        n = len(src)
        x, y, c, (k_me, k_x, k_y, k_d) = _where_am_i()
        sib = (x, y, 1 - c)
        local = [pltpu.make_async_copy(src[i], out[i].at[k_me], lsem.at[i]) for i in range(n)] + list(extra)
        for cp in local:
            cp.start()

        def own(i, q, to):
            return _rc(src[i], out[i].at[k_me], ssem.at[i, q], rsem.at[i, q], to)

        def slab(i, q, k, to):
            return _rc(out[i].at[k], out[i].at[k], ssem.at[i, q], rsem.at[i, q], to)

        @pl.when(c == 1)
        def _():
            sends = [own(i, 0, (1 - x, y, 1)) for i in range(n)]
            for cp in sends:
                cp.start()
            fwd = [slab(i, 3, k_x, sib) for i in range(n)]
            for i in range(n):
                slab(i, 0, k_x, sib).wait_recv()
                fwd[i].start()
            for i in range(n):
                slab(i, 4, k_y, sib).wait_recv()
                slab(i, 5, k_d, sib).wait_recv()
            for cp in sends + fwd:
                cp.wait_send()

        @pl.when(c == 0)
        def _():
            sends = [own(i, 1, (x, 1 - y, 0)) for i in range(n)] + [own(i, 2, (1 - x, 1 - y, 0)) for i in range(n)]
            for cp in sends:
                cp.start()
            fwd = [slab(i, 4, k_y, sib) for i in range(n)] + [slab(i, 5, k_d, sib) for i in range(n)]
            for i in range(n):
                slab(i, 1, k_y, sib).wait_recv()
                fwd[i].start()
            for i in range(n):
                slab(i, 2, k_d, sib).wait_recv()
                fwd[n + i].start()
            for i in range(n):
                slab(i, 3, k_x, sib).wait_recv()
            for cp in sends + fwd:
                cp.wait_send()

        for cp in local:
            cp.wait()


def _swap_slabs(arrs, n_slabs, name, async_id=None):
    n = len(arrs)
    J = max(n_slabs, 1)

    def body(*refs):
        src, out = refs[:n], refs[n:2 * n]
        ssem, rsem = refs[2 * n:]
        x, y, c, (k_me, k_x, k_y, k_d) = _where_am_i()
        sib = (x, y, 1 - c)
        sel = (jnp.where(c == 1, k_y, k_me), jnp.where(c == 1, k_d, k_x))
        cps = []
        for i in range(n):
            for j in range(J):
                s = src[i].at[sel[j]] if n_slabs else src[i]
                d = out[i].at[j] if n_slabs else out[i]
                cps.append(_rc(s, d, ssem.at[i, j], rsem.at[i, j], sib))
        for cp in cps:
            cp.start()
        for cp in cps:
            cp.wait()

    return _exchange_call(
        body, arrs, [jax.ShapeDtypeStruct(((n_slabs,) if n_slabs else ()) + a.shape[-2:], a.dtype) for a in arrs],
        [pltpu.SemaphoreType.DMA((n, J)), pltpu.SemaphoreType.DMA((n, J))], name, async_id)


def _exchange_call(body, arrs, out_shape, sems, name, async_id):
    n = len(arrs)
    if async_id is None:
        return pl.pallas_call(
            body, name=name, in_specs=[ANY] * n, out_specs=[ANY] * len(out_shape), out_shape=out_shape,
            scratch_shapes=sems, compiler_params=pltpu.CompilerParams(has_side_effects=True))(*arrs)

    def seq_body(*refs):
        _handshake_all()
        body(*refs)

    return pl.kernel(
        seq_body, name=name, out_type=out_shape, mesh=plsc.ScalarSubcoreMesh(**SEQUENCER), scratch_types=sems,
        compiler_params=pltpu.CompilerParams(collective_id=async_id))(*arrs)


def _send_chip_sums(arrs, name, async_id=None):
    n = len(arrs)

    def body(*refs):
        src, out = refs[:n], refs[n:2 * n]
        ssem, rsem = refs[2 * n:]
        x, y, c, _ = _where_am_i()

        @pl.when(c == 1)
        def _():
            cps = [_rc(src[i].at[1], out[i].at[0], ssem.at[i, 0], rsem.at[i, 0], (1 - x, y, 1)) for i in range(n)]
            for cp in cps:
                cp.start()
            for cp in cps:
                cp.wait()

        @pl.when(c == 0)
        def _():
            cps = [_rc(src[i].at[0], out[i].at[0], ssem.at[i, 0], rsem.at[i, 0], (x, 1 - y, 0)) for i in range(n)]
            cps += [_rc(src[i].at[1], out[i].at[1], ssem.at[i, 1], rsem.at[i, 1], (1 - x, 1 - y, 0)) for i in range(n)]
            for cp in cps:
                cp.start()
            for cp in cps:
                cp.wait()

    return _exchange_call(
        body, arrs, [jax.ShapeDtypeStruct(a.shape, a.dtype) for a in arrs],
        [pltpu.SemaphoreType.DMA((n, 2)), pltpu.SemaphoreType.DMA((n, 2))], name, async_id)


def _chip_sum(g, recv, sel, name):
    _, r, C = g.shape
    tr = _rtile(r)

    def body(sel_ref, g_ref, r_ref, o32_ref, o16_ref):
        s = g_ref[...] + r_ref[...]
        o32_ref[...] = s
        o16_ref[...] = s.astype(BF16)

    blk = pl.BlockSpec((None, tr, C), lambda j, t, sel_ref: (j, t, 0))
    return pl.pallas_call(
        body, name=name,
        grid_spec=pltpu.PrefetchScalarGridSpec(
            num_scalar_prefetch=1, grid=(2, r // tr),
            in_specs=[pl.BlockSpec((None, tr, C), lambda j, t, sel_ref: (sel_ref[j], t, 0)), blk],
            out_specs=[blk, blk]),
        out_shape=[jax.ShapeDtypeStruct((2, r, C), F32), jax.ShapeDtypeStruct((2, r, C), BF16)],
        compiler_params=_params(("arbitrary", "arbitrary")),
    )(sel, g, recv)


def _cross_sum(s32, got, north, after, name):
    _, r, C = s32.shape
    tr = _rtile(r)

    def body(f_ref, p_ref, a_ref, b_ref, after_ref, o_ref):
        a = a_ref[...].astype(F32)

        @pl.when(f_ref[0] == 1)
        def _():
            o_ref[...] = p_ref[...] + a

        @pl.when(f_ref[0] == 0)
        def _():
            o_ref[...] = a + b_ref[...].astype(F32)

    return pl.pallas_call(
        body, name=name,
        grid_spec=pltpu.PrefetchScalarGridSpec(
            num_scalar_prefetch=1, grid=(r // tr,),
            in_specs=[pl.BlockSpec((None, tr, C), lambda t, f: (0, t, 0)),
                      pl.BlockSpec((None, tr, C), lambda t, f: (0, t, 0)),
                      pl.BlockSpec((None, tr, C), lambda t, f: (1 - f[0], t, 0)), ANY],
            out_specs=pl.BlockSpec((tr, C), lambda t, f: (t, 0))),
        out_shape=jax.ShapeDtypeStruct((r, C), F32),
        compiler_params=_params(("arbitrary",)),
    )(north, s32, got, got, after)


def _reduce_scatter(grads, tag, after, async_ids=(None, None, None)):
    x, y, c, (k_me, k_x, k_y, k_d) = _where_am_i()
    sel = jnp.stack([jnp.where(c == 1, k_me, k_y), jnp.where(c == 1, k_x, k_d)]).astype(jnp.int32)
    north = jnp.reshape(c, (1,)).astype(jnp.int32)
    from_sib = _swap_slabs(grads, 2, f"swap_grad_slabs_{tag}", async_ids[0])
    sums = [_chip_sum(g, r, sel, f"chip_sum_{tag}{i}") for i, (g, r) in enumerate(zip(grads, from_sib))]
    got = _send_chip_sums([s16 for _, s16 in sums], f"send_chip_sums_{tag}", async_ids[1])
    parts = [_cross_sum(s32, b, north, after, f"cross_sum_{tag}{i}") for i, ((s32, _), b) in enumerate(zip(sums, got))]
    theirs = _swap_slabs(parts, 0, f"swap_parts_{tag}", async_ids[2])
    return list(zip(parts, theirs))


def _all_reduce_small(v):
    R, C = v.shape

    def body(v_ref, o_ref, buf, ssem, rsem, lsem):
        x, y, c = lax.axis_index("x"), lax.axis_index("y"), lax.axis_index("c")
        me, sib = (x, y, c), (x, y, 1 - c)
        chips = [(1 - x, y), (x, 1 - y), (1 - x, 1 - y)]

        def slot(px, py, pc):
            return buf.at[4 * px + 2 * py + pc]

        def copy(k, block, to, src=None):
            return _rc(slot(*block) if src is None else src, slot(*block), ssem.at[k], rsem.at[k], to)

        mine = pltpu.make_async_copy(v_ref, slot(*me), lsem)
        mine.start()
        first = [copy(0, me, sib, src=v_ref)] + [copy(1 + j, me, (*chip, c), src=v_ref) for j, chip in enumerate(chips)]
        for cp in first:
            cp.start()
        passed = [copy(4 + j, (*chip, c), sib) for j, chip in enumerate(chips)]
        for j, chip in enumerate(chips):
            copy(1 + j, (*chip, c), me).wait_recv()
            passed[j].start()
        copy(0, sib, me).wait_recv()
        for j, chip in enumerate(chips):
            copy(4 + j, (*chip, 1 - c), me).wait_recv()
        for cp in first + passed:
            cp.wait_send()
        mine.wait()
        acc = buf[0]
        for k in range(1, 8):
            acc = acc + buf[k]
        o_ref[...] = acc

    return pl.pallas_call(
        body, name="all_reduce_small",
        in_specs=[pl.BlockSpec(memory_space=pltpu.VMEM)], out_specs=pl.BlockSpec(memory_space=pltpu.VMEM),
        out_shape=jax.ShapeDtypeStruct((R, C), F32),
        scratch_shapes=[pltpu.VMEM((8, R, C), F32), pltpu.SemaphoreType.DMA((7,)), pltpu.SemaphoreType.DMA((7,)),
                        pltpu.SemaphoreType.DMA],
        compiler_params=pltpu.CompilerParams(has_side_effects=True, vmem_limit_bytes=VMEM_LIMIT),
    )(v)


WEIGHTS = ['ffn1_norm', 'ffn1_w_gate', 'ffn1_w_up', 'ffn1_w_down', 'mix_norm', 'w_in', 'conv_w', 'conv_b', 's5_A_re',
           's5_A_im', 's5_log_dt', 's5_B_re', 's5_B_im', 's5_C_re', 's5_C_im', 's5_D', 's5_w_glu', 's5_b_glu',
           'ssd_A_log', 'ssd_dt_bias', 'ssd_D', 'ssd_norm', 'w_proj_s5', 'w_proj_ssd', 'b_gate', 'w_out', 'ffn2_norm',
           'ffn2_w_gate', 'ffn2_w_up', 'ffn2_w_down', 'final_norm']
ARGS = ['x'] + WEIGHTS + ['loss_target'] + ['m_' + n for n in WEIGHTS] + ['v_' + n for n in WEIGHTS]
COL_SHARDED = ('ffn1_w_gate', 'ffn1_w_up', 'w_in', 'ffn2_w_gate', 'ffn2_w_up')
ROW_SHARDED = ('ffn1_w_down', 's5_w_glu', 'w_proj_s5', 'w_proj_ssd', 'w_out', 'ffn2_w_down')
MATRICES = COL_SHARDED + ROW_SHARDED
GATHER_FIRST = ('ffn1_w_gate', 'ffn1_w_up', 'ffn1_w_down')
GATHER_SECOND = ('w_in', 'conv_w', 's5_w_glu', 'w_proj_s5', 'w_proj_ssd', 'w_out')
GATHER_THIRD = ('ffn2_w_gate', 'ffn2_w_up', 'ffn2_w_down')
SMALL = [n for n in WEIGHTS if n not in MATRICES]


def _pack(arrs, width=1024):
    flat = jnp.concatenate([a.reshape(-1).astype(F32) for a in arrs])
    rows = -(-flat.shape[0] // (8 * width)) * 8
    return jnp.pad(flat, (0, rows * width - flat.shape[0])).reshape(rows, width)


def _unpack(packed, shapes):
    flat, out, o = packed.reshape(-1), [], 0
    for s in shapes:
        n = math.prod(s)
        out.append(flat[o:o + n].reshape(s))
        o += n
    return out


def kernel(x, ffn1_norm, ffn1_w_gate, ffn1_w_up, ffn1_w_down, mix_norm, w_in, conv_w, conv_b, s5_A_re, s5_A_im, s5_log_dt, s5_B_re, s5_B_im, s5_C_re, s5_C_im, s5_D, s5_w_glu, s5_b_glu, ssd_A_log, ssd_dt_bias, ssd_D, ssd_norm, w_proj_s5, w_proj_ssd, b_gate, w_out, ffn2_norm, ffn2_w_gate, ffn2_w_up, ffn2_w_down, final_norm, loss_target, m_ffn1_norm, m_ffn1_w_gate, m_ffn1_w_up, m_ffn1_w_down, m_mix_norm, m_w_in, m_conv_w, m_conv_b, m_s5_A_re, m_s5_A_im, m_s5_log_dt, m_s5_B_re, m_s5_B_im, m_s5_C_re, m_s5_C_im, m_s5_D, m_s5_w_glu, m_s5_b_glu, m_ssd_A_log, m_ssd_dt_bias, m_ssd_D, m_ssd_norm, m_w_proj_s5, m_w_proj_ssd, m_b_gate, m_w_out, m_ffn2_norm, m_ffn2_w_gate, m_ffn2_w_up, m_ffn2_w_down, m_final_norm, v_ffn1_norm, v_ffn1_w_gate, v_ffn1_w_up, v_ffn1_w_down, v_mix_norm, v_w_in, v_conv_w, v_conv_b, v_s5_A_re, v_s5_A_im, v_s5_log_dt, v_s5_B_re, v_s5_B_im, v_s5_C_re, v_s5_C_im, v_s5_D, v_s5_w_glu, v_s5_b_glu, v_ssd_A_log, v_ssd_dt_bias, v_ssd_D, v_ssd_norm, v_w_proj_s5, v_w_proj_ssd, v_b_gate, v_w_out, v_ffn2_norm, v_ffn2_w_gate, v_ffn2_w_up, v_ffn2_w_down, v_final_norm):
    a = dict(locals())
    assert list(a) == ARGS
    x, target = a['x'][0], a['loss_target'][0]
    k_me = 2 * lax.axis_index("x") + lax.axis_index("y")

    src = {n: a[n][0].T.astype(BF16) for n in COL_SHARDED}
    src.update({n: a[n][0].astype(BF16) for n in ROW_SHARDED})
    src['conv_w'] = a['conv_w'][0]
    def flat(f):
        return f.reshape(-1, f.shape[-1])

    full = _gather_shards([src[n] for n in GATHER_FIRST])
    w = {n: flat(f) for n, f in zip(GATHER_FIRST, full)}
    got, _ = _gather_shards_async([src[n] for n in GATHER_SECOND], full[0][0, :8, :128].astype(F32), "gather_mixer", 1)
    w.update({n: flat(f) for n, f in zip(GATHER_SECOND, got)})
    w['conv_w'] = w['conv_w'].reshape(4, CONV_K, -1).transpose(1, 0, 2).reshape(CONV_K, CONV_DIM)

    def gather_last(after):
        got, _ = _gather_shards_async([src[n] for n in GATHER_THIRD], after[:8, :128], "gather_ffn2", 2)
        return {n: flat(f) for n, f in zip(GATHER_THIRD, got)}

    p = {n: a[n] for n in SMALL if n != 'conv_w'}
    p['final_norm'] = a['final_norm'][None]

    loss, grad_x, gp, gw, dx_mixer = _local_step(x, target, p, w, gather_last)

    pairs = {}
    for tag, names, ids, after in (("ffn2", GATHER_THIRD, (3, 4, 5), dx_mixer),
                                   ("mixer", [n for n in GATHER_SECOND if n != 'conv_w'], (6, 7, 8), grad_x),
                                   ("ffn1", GATHER_FIRST, (9, 10, 11), grad_x)):
        pairs.update(zip(names, _reduce_scatter([gw[n].reshape(4, -1, D_MODEL) for n in names], tag, after, ids)))
    pairs = [pairs[n] for n in MATRICES]
    gp['conv_w'] = gw['conv_w'][None]
    small_shapes = [(1, 1)] + [gp[n].shape if n != 'final_norm' else (1, D_MODEL) for n in SMALL]
    red = _unpack(_all_reduce_small(_pack([loss] + [gp[n] for n in SMALL])), small_shapes)
    loss_all = red[0].reshape(())
    gsmall = dict(zip(SMALL, red[1:]))
    gsmall['conv_w'] = lax.dynamic_slice_in_dim(gsmall['conv_w'], k_me * 1024, 1024, axis=2)
    gsmall = {n: g.reshape(a[n].shape) for n, g in gsmall.items()}

    grads, delta, new_m, new_v = {}, {}, {}, {}
    for n, (mine, theirs) in zip(MATRICES, pairs):
        wn, mn, vn = a[n][0], a['m_' + n][0], a['v_' + n][0]
        if n in COL_SHARDED:
            (gt,) = _rowwise(lambda p_, q_: (p_ + q_,), [mine, theirs], [], [(D_MODEL, F32)], name=f"sum_{n}",
                             tm=_rtile(mine.shape[0]))
            g = gt.T
            d, m2, v2 = _adamw(wn, mn, vn, [g], name=f"adamw_{n}", emit_g=False)
        else:
            g, d, m2, v2 = _adamw(wn, mn, vn, [mine, theirs], name=f"adamw_{n}", emit_g=True)
        grads[n], delta[n], new_m[n], new_v[n] = g[None], d[None], m2[None], v2[None]
    sw, sm, sv, sg = [_pack([t[n] for n in SMALL]) for t in (a, {n: a['m_' + n] for n in SMALL},
                                                             {n: a['v_' + n] for n in SMALL}, gsmall)]
    d, m2, v2 = _adamw(sw, sm, sv, [sg], name="adamw_small", emit_g=False)
    shapes = [a[n].shape for n in SMALL]
    for n, dd, mm, vv in zip(SMALL, _unpack(d, shapes), _unpack(m2, shapes), _unpack(v2, shapes)):
        grads[n], delta[n], new_m[n], new_v[n] = gsmall[n], dd, mm, vv
    return (loss_all, grad_x[None], *[grads[n] for n in WEIGHTS], *[delta[n] for n in WEIGHTS],
            *[new_m[n] for n in WEIGHTS], *[new_v[n] for n in WEIGHTS])
```

```python
import itertools
import math

import jax
import jax.numpy as jnp
from jax import lax
from jax.experimental import pallas as pl
from jax.experimental.pallas import tpu as pltpu
from jax.experimental.pallas import tpu_sc as plsc

F32 = jnp.float32
BF16 = jnp.bfloat16
MXU = BF16
HI = lax.Precision.HIGHEST

D_MODEL = 1024
D_FF = 2816
EPS = 1e-6
S5_GROUPS, S5_GROUP, S5_STATE = 64, 16, 64
D_INNER = 2048
SSD_HEADDIM, SSD_HEADS, SSD_GROUPS, SSD_HPG, SSD_STATE = 64, 32, 8, 4, 128
CONV_K, CONV_DIM = 4, 4096
IN_SPLITS = (1024, 2048, 4096, 32, 2048)
IN_OFFS = (0, 1024, 3072, 7168, 7200, 9248)
SSD_Q = 256
ADAM_LR, ADAM_B1, ADAM_B2, ADAM_EPS, ADAM_WD, ADAM_STEP = 0.001, 0.9, 0.999, 1e-08, 0.01, 10

VMEM_LIMIT = 56 * 1024 * 1024
MATMUL_VMEM = 32 * 1024 * 1024
MESH = pl.DeviceIdType.MESH

NN = ((1,), (0,))
NT = ((1,), (1,))
TN = ((0,), (0,))


def _dot(a, b, dims, precision=None):
    if precision is None:
        a, b = a.astype(MXU), b.astype(MXU)
    return lax.dot_general(a, b, (dims, ((), ())), precision=precision, preferred_element_type=F32)


def _tile(n, pref):
    if n <= pref:
        return n
    best = None
    for t in range(128, pref + 1, 128):
        if n % t == 0:
            best = t
    assert best is not None, (n, pref)
    return best


def _params(sem):
    return pltpu.CompilerParams(dimension_semantics=sem, vmem_limit_bytes=VMEM_LIMIT)


def _matmul(a, b, mode, *, name, out=((F32),), epilogue=None, extras=(), after=(), tm=512, tn=1536):
    if mode == "nn":
        (M, K), (_, N) = a.shape, b.shape
    elif mode == "nt":
        (M, K), (N, _) = a.shape, b.shape
    else:
        (K, M), (_, N) = a.shape, b.shape
    tm, tn = _tile(M, tm), _tile(N, tn)

    def vmem(tm_, tn_):
        per = K * tm_ * a.dtype.itemsize + K * tn_ * b.dtype.itemsize
        per += sum(tm_ * tn_ * jnp.dtype(dt).itemsize for dt in out)
        per += sum((1 if e.shape[0] == 1 else tm_) * tn_ * e.dtype.itemsize for e in extras)
        return 2 * per

    while vmem(tm, tn) > MATMUL_VMEM and (tn > 128 or tm > 128):
        if tn >= tm and tn > 128:
            tn = _tile(N, tn - 128)
        else:
            tm = _tile(M, tm - 128)
    a_spec = pl.BlockSpec((K, tm), lambda i, j: (0, i)) if mode == "tn" else pl.BlockSpec((tm, K), lambda i, j: (i, 0))
    b_spec = pl.BlockSpec((tn, K), lambda i, j: (j, 0)) if mode == "nt" else pl.BlockSpec((K, tn), lambda i, j: (0, j))
    dims = {"nn": NN, "nt": NT, "tn": TN}[mode]
    e_specs = []
    for e in extras:
        if e.shape[0] == 1:
            e_specs.append(pl.BlockSpec((1, tn), lambda i, j: (0, j)))
        else:
            assert e.shape == (M, N), (e.shape, M, N)
            e_specs.append(pl.BlockSpec((tm, tn), lambda i, j: (i, j)))
    n_e, n_o, n_a = len(extras), len(out), len(after)

    def body(a_ref, b_ref, *refs):
        acc = _dot(a_ref[...], b_ref[...], dims)
        res = (acc,) if epilogue is None else epilogue(acc, *[r[...] for r in refs[:n_e]])
        for r, v in zip(refs[n_e + n_a:], res):
            r[...] = v.astype(r.dtype)

    res = pl.pallas_call(
        body, name=name, grid=(M // tm, N // tn),
        in_specs=[a_spec, b_spec] + e_specs + [ANY] * n_a,
        out_specs=[pl.BlockSpec((tm, tn), lambda i, j: (i, j)) for _ in range(n_o)],
        out_shape=[jax.ShapeDtypeStruct((M, N), dt) for dt in out],
        compiler_params=_params(("parallel", "parallel")),
    )(a, b, *extras, *after)
    return res[0] if n_o == 1 else res


def _rowwise(fn, rows, fulls, outs, reds=(), *, name, after=(), tm=256):
    T = rows[0].shape[0]
    tm = min(tm, T)
    n_r, n_f, n_o, n_d, n_a = len(rows), len(fulls), len(outs), len(reds), len(after)

    def body(*refs):
        ins = [r[...] for r in refs[:n_r + n_f]]
        o_refs = refs[n_r + n_f + n_a:n_r + n_f + n_a + n_o]
        d_refs = refs[n_r + n_f + n_a + n_o:]
        res = fn(*ins)
        for r, v in zip(o_refs, res[:n_o]):
            r[...] = v.astype(r.dtype)
        if n_d:
            @pl.when(pl.program_id(0) == 0)
            def _():
                for r in d_refs:
                    r[...] = jnp.zeros_like(r)
            for r, v in zip(d_refs, res[n_o:]):
                r[...] += v.astype(r.dtype)

    res = pl.pallas_call(
        body, name=name, grid=(T // tm,),
        in_specs=[pl.BlockSpec((tm, r.shape[1]), lambda i: (i, 0)) for r in rows]
        + [pl.BlockSpec(f.shape, lambda i, nd=f.ndim: (0,) * nd) for f in fulls] + [ANY] * n_a,
        out_specs=[pl.BlockSpec((tm, c), lambda i: (i, 0)) for c, _ in outs]
        + [pl.BlockSpec(s, lambda i, nd=len(s): (0,) * nd) for s, _ in reds],
        out_shape=[jax.ShapeDtypeStruct((T, c), dt) for c, dt in outs]
        + [jax.ShapeDtypeStruct(s, dt) for s, dt in reds],
        compiler_params=_params(("arbitrary",)),
    )(*rows, *fulls, *after)
    return res


def _rms(x, g):
    return x * lax.rsqrt(jnp.mean(x * x, axis=-1, keepdims=True) + EPS) * g


def _colsum(v):
    return jnp.sum(v, axis=0, keepdims=True)


def _softplus(x):
    return jnp.maximum(x, 0.0) + jnp.log1p(jnp.exp(-jnp.abs(x)))


def _ffn_fwd(x, norm, wgT, wuT, wd, tag):
    D = x.shape[1]
    (hn,) = _rowwise(lambda xt, g: (_rms(xt, g),), [x], [norm], [(D, BF16)], name=f"{tag}_norm")
    a = _matmul(hn, wgT, "nt", name=f"{tag}_gate")
    b, hid = _matmul(hn, wuT, "nt", name=f"{tag}_up", out=(BF16, BF16),
                     epilogue=lambda acc, at: (acc, jax.nn.silu(at) * acc), extras=[a])
    y = _matmul(hid, wd, "nn", name=f"{tag}_down", epilogue=lambda acc, xt: (xt + 0.5 * acc,), extras=[x])
    return y, (hn, a, b, hid)


def _ffn_bwd(dy, x, norm, wgT, wuT, wd, res, tag, after, on_grads):
    hn, a, b, hid = res
    D, F = x.shape[1], a.shape[1]

    def act_bwd(acc, at, bt):
        _, vjp = jax.vjp(lambda p, q: jax.nn.silu(p) * q, at, bt.astype(F32))
        return vjp(0.5 * acc)

    da, db = _matmul(dy, wd, "nt", name=f"{tag}_d_hid", out=(BF16, BF16), epilogue=act_bwd, extras=[a, b])
    d_wd = _matmul(hid, dy, "tn", name=f"{tag}_d_wd", epilogue=lambda acc: (0.5 * acc,), after=after)
    d_wgT = _matmul(da, hn, "tn", name=f"{tag}_d_wg")
    d_wuT = _matmul(db, hn, "tn", name=f"{tag}_d_wu")
    started = on_grads(d_wgT, d_wuT, d_wd)
    dhn = _matmul(da, wgT, "nn", name=f"{tag}_d_hn1")
    dhn = _matmul(db, wuT, "nn", name=f"{tag}_d_hn2", epilogue=lambda acc, e: (acc + e,), extras=[dhn])

    def norm_bwd(xt, dh, dyt, g):
        _, vjp = jax.vjp(_rms, xt, g)
        dx, dg = vjp(dh)
        return dyt + dx, dg

    dx, d_norm = _rowwise(norm_bwd, [x, dhn, dy], [norm], [(D, F32)], [((1, D), F32)], name=f"{tag}_d_norm",
                          after=started)
    return dx, d_norm, d_wgT, d_wuT, d_wd


S5_TILES = 8
S5_HALF = 256


def _s5_derive(A_re, A_im, log_dt, B_re, B_im, C_re, C_im):
    G, N, M = S5_GROUPS, S5_STATE, S5_GROUP
    dt = jnp.exp(log_dt)[:, None]
    mag = jnp.exp(A_re * dt)
    ar, ai = mag * jnp.cos(A_im * dt), mag * jnp.sin(A_im * dt)
    den = A_re * A_re + A_im * A_im
    cr = ((ar - 1.0) * A_re + ai * A_im) / den
    ci = (ai * A_re - (ar - 1.0) * A_im) / den
    bbr = cr[..., None] * B_re - ci[..., None] * B_im
    bbi = cr[..., None] * B_im + ci[..., None] * B_re
    eye = jnp.eye(8, dtype=F32)

    def tile_in(bb):
        t = bb.reshape(S5_TILES, 8, N, M).transpose(0, 1, 3, 2)
        return jnp.einsum("jamn,ab->jambn", t, eye).reshape(S5_TILES, 8 * M, 8 * N)

    def tile_out(c):
        t = c.reshape(S5_TILES, 8, M, N).transpose(0, 1, 3, 2)
        return jnp.einsum("janm,ab->janbm", t, eye).reshape(S5_TILES, 8 * N, 8 * M)

    return (tile_in(bbr), tile_in(bbi), tile_out(C_re), tile_out(C_im),
            ar.reshape(S5_TILES, 1, 8 * N), ai.reshape(S5_TILES, 1, 8 * N))


S5_NB = S5_HALF // 128
S5_SEG = 32


def _cmul(ar, ai, br, bi):
    return ar * br - ai * bi, ar * bi + ai * br


def _s5_scan(sr_ref, si_ref, ar, ai, T, reverse):
    L, V = T // S5_SEG, S5_SEG // 8
    assert L * S5_SEG == T and L & (L - 1) == 0, T
    sg = -1.0 if reverse else 1.0
    a_r = [jnp.broadcast_to(ar[:, 128 * b:128 * (b + 1)], (8, 128)) for b in range(S5_NB)]
    a_i = [jnp.broadcast_to(sg * ai[:, 128 * b:128 * (b + 1)], (8, 128)) for b in range(S5_NB)]

    def rows(k, v):
        return pl.ds(pl.multiple_of(((L - 1 - k) if reverse else k) * S5_SEG + 8 * v, 8), 8)

    def local(k, carry):
        out = []
        for b in range(S5_NB):
            for v in range(V):
                idx = rows(k, v)
                mr, mi = _cmul(a_r[b], a_i[b], *carry[b * V + v])
                nr, ni = mr + sr_ref[b, idx, :], mi + si_ref[b, idx, :]
                sr_ref[b, idx, :] = nr
                si_ref[b, idx, :] = ni
                out.append((nr, ni))
        return tuple(out)

    z = jnp.zeros((8, 128), F32)
    ends = lax.fori_loop(0, L, local, tuple((z, z) for _ in range(S5_NB * V)))

    carries = []
    for b in range(S5_NB):
        pr, pi = a_r[b][0:1], a_i[b][0:1]
        n = L
        while n > 1:
            pr, pi = _cmul(pr, pi, pr, pi)
            n //= 2
        cr, ci = jnp.zeros((1, 128), F32), jnp.zeros((1, 128), F32)
        into = [None] * S5_SEG
        for j in (reversed(range(S5_SEG)) if reverse else range(S5_SEG)):
            into[j] = (cr, ci)
            er, ei = ends[b * V + j // 8]
            mr, mi = _cmul(pr, pi, cr, ci)
            cr, ci = mr + er[j % 8:j % 8 + 1], mi + ei[j % 8:j % 8 + 1]
        carries.append([(jnp.concatenate([into[8 * v + s][0] for s in range(8)], axis=0),
                         jnp.concatenate([into[8 * v + s][1] for s in range(8)], axis=0)) for v in range(V)])

    def fix(k, powers):
        out = []
        for b in range(S5_NB):
            pr, pi = powers[b]
            for v in range(V):
                idx = rows(k, v)
                dr, di = _cmul(pr, pi, *carries[b][v])
                sr_ref[b, idx, :] += dr
                si_ref[b, idx, :] += di
            out.append(_cmul(pr, pi, a_r[b], a_i[b]))
        return tuple(out)

    lax.fori_loop(0, L, fix, tuple((a_r[b], a_i[b]) for b in range(S5_NB)))


def _s5_in(ut, b_ref, s_ref):
    for b in range(S5_NB):
        s_ref[b] = _dot(ut, b_ref[:, 128 * b:128 * (b + 1)], NN)


def _to_seg(v):
    T, C = v.shape
    return v.reshape(S5_SEG, T // S5_SEG, C).transpose(1, 0, 2).reshape(T, C)


def _from_seg(v):
    T, C = v.shape
    return v.reshape(T // S5_SEG, S5_SEG, C).transpose(1, 0, 2).reshape(T, C)


def _s5_specs(T):
    u_spec = pl.BlockSpec((T, 128), lambda j, h: (0, j))
    b_spec = pl.BlockSpec((None, 128, S5_HALF), lambda j, h: (j, 0, h))
    c_spec = pl.BlockSpec((None, S5_HALF, 128), lambda j, h: (j, h, 0))
    a_spec = pl.BlockSpec((None, 1, S5_HALF), lambda j, h: (j, 0, h))
    return u_spec, b_spec, c_spec, a_spec


def _s5_fwd(u, tiles):
    T = u.shape[0]
    u_spec, b_spec, c_spec, a_spec = _s5_specs(T)

    def body(u_ref, br_ref, bi_ref, cr_ref, ci_ref, ar_ref, ai_ref, y_ref, sr_ref, si_ref):
        ut = u_ref[...].astype(MXU)
        _s5_in(ut, br_ref, sr_ref)
        _s5_in(ut, bi_ref, si_ref)
        _s5_scan(sr_ref, si_ref, ar_ref[...], ai_ref[...], T, False)
        y = None
        for b in range(S5_NB):
            blk = slice(128 * b, 128 * (b + 1))
            yb = _dot(sr_ref[b], cr_ref[blk, :], NN) - _dot(si_ref[b], ci_ref[blk, :], NN)
            y = yb if y is None else y + yb

        @pl.when(pl.program_id(1) == 0)
        def _():
            y_ref[...] = y

        @pl.when(pl.program_id(1) == 1)
        def _():
            y_ref[...] += y

    scr = pltpu.VMEM((S5_NB, T, 128), F32)
    return pl.pallas_call(
        body, name="s5_fwd", grid=(S5_TILES, 2),
        in_specs=[u_spec, b_spec, b_spec, c_spec, c_spec, a_spec, a_spec],
        out_specs=u_spec, out_shape=jax.ShapeDtypeStruct(u.shape, F32),
        scratch_shapes=[scr, scr],
        compiler_params=_params(("parallel", "arbitrary")),
    )(u, *tiles)


def _s5_bwd(u, dys, du_skip, tiles):
    T = u.shape[0]
    u_spec, b_spec, c_spec, a_spec = _s5_specs(T)

    def body(u_ref, dy_ref, sk_ref, br_ref, bi_ref, cr_ref, ci_ref, ar_ref, ai_ref,
             du_ref, dbr_ref, dbi_ref, dcr_ref, dci_ref, dar_ref, dai_ref, sr_ref, si_ref, lr_ref, li_ref):
        ut, dy = u_ref[...].astype(MXU), dy_ref[...].astype(MXU)
        ar, ai = ar_ref[...], ai_ref[...]
        _s5_in(ut, br_ref, sr_ref)
        _s5_in(ut, bi_ref, si_ref)
        _s5_scan(sr_ref, si_ref, ar, ai, T, False)
        for b in range(S5_NB):
            blk = slice(128 * b, 128 * (b + 1))
            dcr_ref[blk, :] = _dot(sr_ref[b], dy, TN)
            dci_ref[blk, :] = -_dot(si_ref[b], dy, TN)
            lr_ref[b] = _dot(dy, cr_ref[blk, :], NT)
            li_ref[b] = -_dot(dy, ci_ref[blk, :], NT)
        _s5_scan(lr_ref, li_ref, ar, ai, T, True)
        du = None
        head = lax.broadcasted_iota(jnp.int32, (T, 128), 0) < S5_SEG
        seg0 = lax.broadcasted_iota(jnp.int32, (S5_SEG, 128), 0) == 0
        for b in range(S5_NB):
            blk = slice(128 * b, 128 * (b + 1))
            lr, li = lr_ref[b], li_ref[b]
            pr = jnp.where(head, 0.0, pltpu.roll(sr_ref[b], S5_SEG, 0))
            pi = jnp.where(head, 0.0, pltpu.roll(si_ref[b], S5_SEG, 0))
            er = jnp.where(seg0, 0.0, pltpu.roll(sr_ref[b, T - S5_SEG:T, :], 1, 0))
            ei = jnp.where(seg0, 0.0, pltpu.roll(si_ref[b, T - S5_SEG:T, :], 1, 0))
            hr, hi = lr[0:S5_SEG], li[0:S5_SEG]
            dar_ref[:, blk] = _colsum(lr * pr + li * pi) + _colsum(hr * er + hi * ei)
            dai_ref[:, blk] = _colsum(li * pr - lr * pi) + _colsum(hi * er - hr * ei)
            dbr_ref[:, blk] = _dot(ut, lr, TN)
            dbi_ref[:, blk] = _dot(ut, li, TN)
            dub = _dot(lr, br_ref[:, blk], NT) + _dot(li, bi_ref[:, blk], NT)
            du = dub if du is None else du + dub

        @pl.when(pl.program_id(1) == 0)
        def _():
            du_ref[...] = du + sk_ref[...]

        @pl.when(pl.program_id(1) == 1)
        def _():
            du_ref[...] += du

    scr = pltpu.VMEM((S5_NB, T, 128), F32)
    return pl.pallas_call(
        body, name="s5_bwd", grid=(S5_TILES, 2),
        in_specs=[u_spec, u_spec, u_spec, b_spec, b_spec, c_spec, c_spec, a_spec, a_spec],
        out_specs=[u_spec, b_spec, b_spec, c_spec, c_spec, a_spec, a_spec],
        out_shape=[jax.ShapeDtypeStruct(u.shape, F32)] + [jax.ShapeDtypeStruct(t.shape, F32) for t in tiles],
        scratch_shapes=[scr, scr, scr, scr],
        compiler_params=_params(("parallel", "arbitrary")),
    )(u, dys, du_skip, *tiles)


CONV_TILE = 256


def _conv_pre(x, w, b):
    T = x.shape[0]
    row = lax.broadcasted_iota(jnp.int32, x.shape, 0)
    acc = x * w[CONV_K - 1:CONV_K, :] + b
    for lag in range(1, CONV_K):
        acc = acc + jnp.where(row >= lag, pltpu.roll(x, lag, 0), 0.0) * w[CONV_K - 1 - lag:CONV_K - lag, :]
    return acc


def _conv_fwd(x, w, b):
    T, C = x.shape
    col = pl.BlockSpec((T, CONV_TILE), lambda j: (0, j))

    def body(x_ref, w_ref, b_ref, o_ref):
        o_ref[...] = jax.nn.silu(_conv_pre(x_ref[...], w_ref[...], b_ref[...]))

    return pl.pallas_call(
        body, name="conv_fwd", grid=(C // CONV_TILE,),
        in_specs=[col, pl.BlockSpec((CONV_K, CONV_TILE), lambda j: (0, j)), pl.BlockSpec((1, CONV_TILE), lambda j: (0, j))],
        out_specs=col, out_shape=jax.ShapeDtypeStruct((T, C), F32),
        compiler_params=_params(("parallel",)),
    )(x, w, b)


def _conv_bwd(x, w, b, dout, col0, name):
    T, C = dout.shape
    off = col0 // CONV_TILE
    xcol = pl.BlockSpec((T, CONV_TILE), lambda j: (0, j + off))
    dcol = pl.BlockSpec((T, CONV_TILE), lambda j: (0, j))

    def body(x_ref, w_ref, b_ref, d_ref, dx_ref, dw_ref, db_ref):
        x, w = x_ref[...], w_ref[...]
        _, vjp = jax.vjp(jax.nn.silu, _conv_pre(x, w, b_ref[...]))
        (dy,) = vjp(d_ref[...])
        row = lax.broadcasted_iota(jnp.int32, x.shape, 0)
        dx = dy * w[CONV_K - 1:CONV_K, :]
        dw_ref[CONV_K - 1:CONV_K, :] = _colsum(dy * x)
        for lag in range(1, CONV_K):
            dx = dx + jnp.where(row < T - lag, pltpu.roll(dy, T - lag, 0), 0.0) * w[CONV_K - 1 - lag:CONV_K - lag, :]
            xs = jnp.where(row >= lag, pltpu.roll(x, lag, 0), 0.0)
            dw_ref[CONV_K - 1 - lag:CONV_K - lag, :] = _colsum(dy * xs)
        dx_ref[...] = dx.astype(dx_ref.dtype)
        db_ref[...] = _colsum(dy)

    return pl.pallas_call(
        body, name=name, grid=(C // CONV_TILE,),
        in_specs=[xcol, pl.BlockSpec((CONV_K, CONV_TILE), lambda j: (0, j + off)),
                  pl.BlockSpec((1, CONV_TILE), lambda j: (0, j + off)), dcol],
        out_specs=[dcol, pl.BlockSpec((CONV_K, CONV_TILE), lambda j: (0, j)), pl.BlockSpec((1, CONV_TILE), lambda j: (0, j))],
        out_shape=[jax.ShapeDtypeStruct((T, C), BF16), jax.ShapeDtypeStruct((CONV_K, C), F32), jax.ShapeDtypeStruct((1, C), F32)],
        compiler_params=_params(("parallel",)),
    )(x, w, b, dout)


def _ssd_common(x_ref, b_ref, c_ref, dtr_ref, bias_ref, alog_ref, tri_ref, triu_ref):
    Q = x_ref.shape[0]
    x, Bm, Cm = x_ref[...], b_ref[...], c_ref[...]
    pre = dtr_ref[...] + bias_ref[...]
    dt = _softplus(pre)
    A = -jnp.exp(alog_ref[...])
    adt = dt * A
    ac4 = _dot(tri_ref[...], adt, NN, HI)
    ar4 = _dot(adt, triu_ref[...], TN, HI)
    cb = _dot(Cm, Bm, NT)
    ii = lax.broadcasted_iota(jnp.int32, (Q, Q), 0)
    jj = lax.broadcasted_iota(jnp.int32, (Q, Q), 1)
    atot4 = ac4[Q - 1:Q, :]
    return x, Bm, Cm, pre, dt, A, ac4, ar4, cb, ii >= jj, atot4


def _ssd_decay(k, ac4, ar4, causal):
    seg = ac4[:, k:k + 1] - ar4[k:k + 1, :]
    return jnp.where(causal, jnp.exp(jnp.where(causal, seg, 0.0)), 0.0)


def _per_head(c4, n):
    lane = lax.broadcasted_iota(jnp.int32, (c4.shape[0], n), 1)
    out = jnp.broadcast_to(c4[:, 0:1], (c4.shape[0], n))
    for k in range(1, SSD_HPG):
        out = jnp.where(lane >= SSD_HEADDIM * k, c4[:, k:k + 1], out)
    return out


def _head_sums(v):
    lane = lax.broadcasted_iota(jnp.int32, (v.shape[0], SSD_HPG), 1)
    out = jnp.zeros((v.shape[0], SSD_HPG), F32)
    for k in range(SSD_HPG):
        s = jnp.sum(v[:, SSD_HEADDIM * k:SSD_HEADDIM * (k + 1)], axis=1, keepdims=True)
        out = jnp.where(lane == k, s, out)
    return out


def _head_rows(c4):
    n = SSD_HPG * SSD_HEADDIM
    row = lax.broadcasted_iota(jnp.int32, (n, 1), 0)
    out = jnp.broadcast_to(c4[:, 0:1], (n, 1))
    for k in range(1, SSD_HPG):
        out = jnp.where(row >= SSD_HEADDIM * k, c4[:, k:k + 1], out)
    return out


def _ssd_specs(T, Q, rev):
    NC = T // Q
    cc = (lambda c: NC - 1 - c) if rev else (lambda c: c)
    x_spec = pl.BlockSpec((Q, SSD_HPG * SSD_HEADDIM), lambda g, c: (cc(c), g))
    b_spec = pl.BlockSpec((Q, SSD_STATE), lambda g, c: (cc(c), D_INNER // SSD_STATE + g))
    c_spec = pl.BlockSpec((Q, SSD_STATE), lambda g, c: (cc(c), D_INNER // SSD_STATE + SSD_GROUPS + g))
    dt_spec = pl.BlockSpec((None, Q, SSD_HPG), lambda g, c: (g, cc(c), 0))
    p_spec = pl.BlockSpec((None, 1, SSD_HPG), lambda g, c: (g, 0, 0))
    tri_spec = pl.BlockSpec((Q, Q), lambda g, c: (0, 0))
    h_spec = pl.BlockSpec((None, None, SSD_HPG * SSD_HEADDIM, SSD_STATE), lambda g, c: (g, cc(c), 0, 0))
    return x_spec, b_spec, c_spec, dt_spec, p_spec, tri_spec, h_spec


def _tri(Q):
    tri = jnp.tril(jnp.ones((Q, Q), F32))
    return tri, tri.T


def _ssd_fwd(xc, dtr8, bias8, alog8, dsk8):
    T, Q = xc.shape[0], SSD_Q
    NC = T // Q
    P, W = SSD_HEADDIM, SSD_HPG * SSD_HEADDIM
    x_spec, b_spec, c_spec, dt_spec, p_spec, tri_spec, h_spec = _ssd_specs(T, Q, False)

    def body(x_ref, b_ref, c_ref, dtr_ref, bias_ref, alog_ref, dsk_ref, tri_ref, triu_ref, y_ref, hs_ref, h_scr):
        @pl.when(pl.program_id(1) == 0)
        def _():
            h_scr[...] = jnp.zeros_like(h_scr)

        hin = h_scr[...]
        hs_ref[...] = hin
        x, Bm, Cm, pre, dt, A, ac4, ar4, cb, causal, atot4 = _ssd_common(
            x_ref, b_ref, c_ref, dtr_ref, bias_ref, alog_ref, tri_ref, triu_ref)
        xdt = x * _per_head(dt, W)
        rest = _dot(Cm, hin, NT) * _per_head(jnp.exp(ac4), W) + _per_head(dsk_ref[...], W) * x
        for k in range(SSD_HPG):
            blk = slice(P * k, P * (k + 1))
            G = cb * _ssd_decay(k, ac4, ar4, causal)
            y_ref[:, blk] = _dot(G, xdt[:, blk], NN) + rest[:, blk]
        xw = xdt * _per_head(jnp.exp(atot4 - ac4), W)
        h_scr[...] = _head_rows(jnp.exp(atot4)) * hin + _dot(xw, Bm, TN)

    tri, triu = _tri(Q)
    return pl.pallas_call(
        body, name="ssd_fwd", grid=(SSD_GROUPS, NC),
        in_specs=[x_spec, b_spec, c_spec, dt_spec, p_spec, p_spec, p_spec, tri_spec, tri_spec],
        out_specs=[pl.BlockSpec((Q, W), lambda g, c: (c, g)), h_spec],
        out_shape=[jax.ShapeDtypeStruct((T, D_INNER), F32),
                   jax.ShapeDtypeStruct((SSD_GROUPS, NC, W, SSD_STATE), F32)],
        scratch_shapes=[pltpu.VMEM((W, SSD_STATE), F32)],
        compiler_params=_params(("parallel", "arbitrary")),
    )(xc, xc, xc, dtr8, bias8, alog8, dsk8, tri, triu)


def _ssd_bwd(xc, dtr8, bias8, alog8, dsk8, hs, dy):
    T, Q = xc.shape[0], SSD_Q
    NC = T // Q
    P, W = SSD_HEADDIM, SSD_HPG * SSD_HEADDIM
    x_spec, b_spec, c_spec, dt_spec, p_spec, tri_spec, h_spec = _ssd_specs(T, Q, True)
    dy_spec = pl.BlockSpec((Q, W), lambda g, c: (NC - 1 - c, g))
    dbc_spec = pl.BlockSpec((Q, SSD_STATE), lambda g, c: (NC - 1 - c, g))

    def body(x_ref, b_ref, c_ref, dtr_ref, bias_ref, alog_ref, dsk_ref, tri_ref, triu_ref, hs_ref, dy_ref,
             dx_ref, db_ref, dc_ref, ddtr_ref, dbias_ref, dalog_ref, ddsk_ref, dh_scr):
        @pl.when(pl.program_id(1) == 0)
        def _():
            dh_scr[...] = jnp.zeros_like(dh_scr)
            dbias_ref[...] = jnp.zeros_like(dbias_ref)
            dalog_ref[...] = jnp.zeros_like(dalog_ref)
            ddsk_ref[...] = jnp.zeros_like(ddsk_ref)

        x, Bm, Cm, pre, dt, A, ac4, ar4, cb, causal, atot4 = _ssd_common(
            x_ref, b_ref, c_ref, dtr_ref, bias_ref, alog_ref, tri_ref, triu_ref)
        dyv, hin, dho = dy_ref[...], hs_ref[...], dh_scr[...]
        dt_w = _per_head(dt, W)
        xdt = x * dt_w
        E4, F4, etot4 = jnp.exp(ac4), jnp.exp(atot4 - ac4), jnp.exp(atot4)
        F_w = _per_head(F4, W)
        ddsk_ref[...] += _head_sums(_colsum(dyv * x))
        Z = _dot(Cm, hin, NT)
        dZ = dyv * _per_head(E4, W)
        dac4 = _head_sums(dyv * Z) * E4
        dC = _dot(dZ, hin, NN)
        dh_scr[...] = _dot(dZ, Cm, TN) + _head_rows(etot4) * dho
        per_row = jnp.sum(dho * hin, axis=1, keepdims=True)
        lane4 = lax.broadcasted_iota(jnp.int32, (1, SSD_HPG), 1)
        datot4 = jnp.zeros((1, SSD_HPG), F32)
        for k in range(SSD_HPG):
            datot4 = jnp.where(lane4 == k, jnp.sum(per_row[P * k:P * (k + 1)], keepdims=True), datot4)
        datot4 = datot4 * etot4
        dxw = _dot(Bm, dho, NT)
        dB = _dot(xdt * F_w, dho, NN)
        dFa4 = _head_sums(dxw * xdt) * F4
        datot4 = datot4 + _colsum(dFa4)
        dac4 = dac4 - dFa4
        dcb = jnp.zeros((Q, Q), F32)
        lane_q4 = lax.broadcasted_iota(jnp.int32, (Q, SSD_HPG), 1)
        sub_4q = lax.broadcasted_iota(jnp.int32, (SSD_HPG, Q), 0)
        row_sums = jnp.zeros((Q, SSD_HPG), F32)
        col_sums = jnp.zeros((SSD_HPG, Q), F32)
        dxdt_heads = []
        for k in range(SSD_HPG):
            blk = slice(P * k, P * (k + 1))
            L = _ssd_decay(k, ac4, ar4, causal)
            G = cb * L
            dG = _dot(dyv[:, blk], xdt[:, blk], NT)
            dxdt_heads.append(_dot(G, dyv[:, blk], TN))
            dcb = dcb + dG * L
            Mseg = dG * G
            row_sums = jnp.where(lane_q4 == k, jnp.sum(Mseg, axis=1, keepdims=True), row_sums)
            col_sums = jnp.where(sub_4q == k, jnp.sum(Mseg, axis=0, keepdims=True), col_sums)
        dxdt = dxw * F_w + jnp.concatenate(dxdt_heads, axis=1)
        last = lax.broadcasted_iota(jnp.int32, (Q, SSD_HPG), 0) == Q - 1
        dac4 = dac4 + row_sums + jnp.where(last, datot4, 0.0)
        dadt4 = _dot(triu_ref[...], dac4, NN, HI) - _dot(triu_ref[...], col_sums, NT, HI)
        ddt4 = _head_sums(dxdt * x) + dadt4 * A
        dalog_ref[...] += _colsum(dadt4 * dt) * A
        ddtr4 = ddt4 * jax.nn.sigmoid(pre)
        dbias_ref[...] += _colsum(ddtr4)
        ddtr_ref[...] = ddtr4
        dx_ref[...] = (_per_head(dsk_ref[...], W) * dyv + dxdt * dt_w).astype(dx_ref.dtype)
        db_ref[...] = dB + _dot(dcb, Cm, TN)
        dc_ref[...] = dC + _dot(dcb, Bm, NN)

    tri, triu = _tri(Q)
    return pl.pallas_call(
        body, name="ssd_bwd", grid=(SSD_GROUPS, NC),
        in_specs=[x_spec, b_spec, c_spec, dt_spec, p_spec, p_spec, p_spec, tri_spec, tri_spec, h_spec, dy_spec],
        out_specs=[dy_spec, dbc_spec, dbc_spec, dt_spec, p_spec, p_spec, p_spec],
        out_shape=[jax.ShapeDtypeStruct((T, D_INNER), F32),
                   jax.ShapeDtypeStruct((T, SSD_GROUPS * SSD_STATE), F32),
                   jax.ShapeDtypeStruct((T, SSD_GROUPS * SSD_STATE), F32),
                   jax.ShapeDtypeStruct(dtr8.shape, F32)] + [jax.ShapeDtypeStruct(bias8.shape, F32)] * 3,
        scratch_shapes=[pltpu.VMEM((W, SSD_STATE), F32)],
        compiler_params=_params(("parallel", "arbitrary")),
    )(xc, xc, xc, dtr8, bias8, alog8, dsk8, tri, triu, hs, dy)


def _to_groups(v):
    return v.reshape(SSD_GROUPS, 1, SSD_HPG)


def _mixer_fwd(x, p, w):
    T, D = x.shape
    (hn,) = _rowwise(lambda xt, g: (_rms(xt, g),), [x], [p["mix_norm"]], [(D, BF16)], name="mix_norm")
    winT = w["w_in"]
    u, z, xbc, dtr, gl = [
        _matmul(hn, winT[IN_OFFS[i]:IN_OFFS[i + 1]], "nt", name=f"mix_in{i}") for i in range(5)]
    tiles = _s5_derive(*[p[k][0] for k in ("s5_A_re", "s5_A_im", "s5_log_dt", "s5_B_re", "s5_B_im", "s5_C_re", "s5_C_im")])
    u = _to_seg(u)
    ys = _s5_fwd(u, tiles)
    (g,) = _rowwise(lambda yt, ut, d: (jax.nn.gelu(yt + d * ut),), [ys, u], [p["s5_D"]], [(D, F32)], name="s5_gelu")
    y5 = _matmul(g, w["s5_w_glu"], "nn", name="s5_glu", out=(BF16,),
                 epilogue=lambda acc, gt, b: (gt * jax.nn.sigmoid(acc + b),), extras=[g, p["s5_b_glu"]])
    xc = _conv_fwd(xbc, w["conv_w"], p["conv_b"])
    dtr8 = dtr.reshape(T, SSD_GROUPS, SSD_HPG).transpose(1, 0, 2)
    ssd_p = [_to_groups(p[k]) for k in ("ssd_dt_bias", "ssd_A_log", "ssd_D")]
    yssd_raw, hs = _ssd_fwd(xc, dtr8, *ssd_p)
    (yssd,) = _rowwise(lambda yt, zt, nw: (_rms(yt * jax.nn.silu(zt), nw),), [yssd_raw, z], [p["ssd_norm"]],
                       [(D_INNER, BF16)], name="ssd_gate")
    p5 = _from_seg(_matmul(y5, w["w_proj_s5"], "nn", name="mix_p5"))
    pssd = _matmul(yssd, w["w_proj_ssd"], "nn", name="mix_pssd")

    def merge(glt, at, bt, bg):
        gates = jax.nn.sigmoid(glt + bg)
        return (gates[:, :D] * at + gates[:, D:] * bt,)

    (merged,) = _rowwise(merge, [gl, p5, pssd], [p["b_gate"]], [(D, BF16)], name="mix_merge")
    y = _matmul(merged, w["w_out"], "nn", name="mix_out", epilogue=lambda acc, xt: (xt + acc,), extras=[x])
    res = dict(hn=hn, u=u, z=z, xbc=xbc, gl=gl, tiles=tiles, ys=ys, g=g, y5=y5, xc=xc, dtr8=dtr8, ssd_p=ssd_p,
               yssd_raw=yssd_raw, hs=hs, yssd=yssd, p5=p5, pssd=pssd, merged=merged)
    return y, res


def _mixer_bwd(dy, x, p, w, r):
    T, D = x.shape
    gp = {}
    dmerged = _matmul(dy, w["w_out"], "nt", name="mix_d_merged")
    gp["w_out"] = _matmul(r["merged"], dy, "tn", name="mix_d_wout")

    def merge_bwd(glt, at, bt, dm, bg):
        def f(q, a_, b_):
            gates = jax.nn.sigmoid(q + bg)
            return gates[:, :D] * a_ + gates[:, D:] * b_
        _, vjp = jax.vjp(f, glt, at, bt)
        dq, da_, db_ = vjp(dm)
        return da_, db_, dq, _colsum(dq)

    dp5, dpssd, dgl, gp["b_gate"] = _rowwise(
        merge_bwd, [r["gl"], r["p5"], r["pssd"], dmerged], [p["b_gate"]],
        [(D, BF16), (D, BF16), (2 * D, BF16)], [((1, 2 * D), F32)], name="mix_d_merge")
    dp5 = _to_seg(dp5)
    dy5 = _matmul(dp5, w["w_proj_s5"], "nt", name="mix_d_y5")
    gp["w_proj_s5"] = _matmul(r["y5"], dp5, "tn", name="mix_d_wp5")
    dyssd = _matmul(dpssd, w["w_proj_ssd"], "nt", name="mix_d_yssd")
    gp["w_proj_ssd"] = _matmul(r["yssd"], dpssd, "tn", name="mix_d_wpssd")

    def gate_bwd(yt, zt, dyt, nw):
        _, vjp = jax.vjp(lambda a_, b_, c_: _rms(a_ * jax.nn.silu(b_), c_), yt, zt, nw)
        return vjp(dyt)

    dyraw, dz, gp["ssd_norm"] = _rowwise(
        gate_bwd, [r["yssd_raw"], r["z"], dyssd], [p["ssd_norm"]],
        [(D_INNER, F32), (D_INNER, BF16)], [((1, D_INNER), F32)], name="ssd_d_gate")
    dxs, dBm, dCm, ddtr8, dbias8, dalog8, ddsk8 = _ssd_bwd(r["xc"], r["dtr8"], *r["ssd_p"], r["hs"], dyraw)
    gp["ssd_dt_bias"], gp["ssd_A_log"], gp["ssd_D"] = [v.reshape(1, SSD_HEADS) for v in (dbias8, dalog8, ddsk8)]
    ddtr = ddtr8.transpose(1, 0, 2).reshape(T, SSD_HEADS)
    conv = [_conv_bwd(r["xbc"], w["conv_w"], p["conv_b"], d, c0, f"conv_bwd{i}")
            for i, (d, c0) in enumerate(((dxs, 0), (dBm, D_INNER), (dCm, D_INNER + SSD_GROUPS * SSD_STATE)))]
    dxbc = [c[0] for c in conv]
    gp["conv_w"] = jnp.concatenate([c[1] for c in conv], axis=1)
    gp["conv_b"] = jnp.concatenate([c[2] for c in conv], axis=1)

    g = r["g"]

    def glu_ep(acc, gt, b, dyt):
        _, vjp = jax.vjp(lambda g_, t_: g_ * jax.nn.sigmoid(t_ + b), gt, acc)
        return vjp(dyt)

    dg1, dt_ = _matmul(g, w["s5_w_glu"], "nn", name="s5_d_glu", out=(F32, BF16), epilogue=glu_ep,
                       extras=[g, p["s5_b_glu"], dy5])
    gp["s5_w_glu"] = _matmul(g, dt_, "tn", name="s5_d_wglu")
    dg = _matmul(dt_, w["s5_w_glu"], "nt", name="s5_d_g", epilogue=lambda acc, e: (acc + e,), extras=[dg1])

    def gelu_bwd(yt, ut, dgt, dtt, d):
        _, vjp = jax.vjp(lambda y_, d_: jax.nn.gelu(y_ + d_ * ut), yt, d)
        dys, dd = vjp(dgt)
        return dys, dys * d, dd, _colsum(dtt.astype(F32))

    dys, dusk, gp["s5_D"], gp["s5_b_glu"] = _rowwise(
        gelu_bwd, [r["ys"], r["u"], dg, dt_], [p["s5_D"]], [(D, F32), (D, F32)], [((1, D), F32), ((1, D), F32)],
        name="s5_d_gelu")
    du, *dtiles = _s5_bwd(r["u"], dys, dusk, r["tiles"])
    du = _from_seg(du)

    pieces = [du, dz, None, ddtr, dgl]
    winT = w["w_in"]
    hn = r["hn"]
    d_rows, dhn = [], None
    cols = [(du, 0, 1024), (dz, 1024, 3072), (dxbc[0], 3072, 5120), (dxbc[1], 5120, 6144), (dxbc[2], 6144, 7168),
            (ddtr, 7168, 7200), (dgl, 7200, 9248)]
    for i, (d, lo, hi) in enumerate(cols):
        d_rows.append(_matmul(d, hn, "tn", name=f"mix_d_win{i}"))
        if dhn is None:
            dhn = _matmul(d, winT[lo:hi], "nn", name=f"mix_d_hn{i}")
        else:
            dhn = _matmul(d, winT[lo:hi], "nn", name=f"mix_d_hn{i}", epilogue=lambda acc, e: (acc + e,), extras=[dhn])
    gp["w_in"] = jnp.concatenate(d_rows, axis=0)
    del pieces

    def norm_bwd(xt, dh, dyt, gn):
        _, vjp = jax.vjp(_rms, xt, gn)
        dx, dgn = vjp(dh)
        return dyt + dx, dgn

    dx, gp["mix_norm"] = _rowwise(norm_bwd, [x, dhn, dy], [p["mix_norm"]], [(D, F32)], [((1, D), F32)], name="mix_d_norm")
    return dx, gp, dtiles


def _s5_param_grads(p, dtiles):
    keys = ("s5_A_re", "s5_A_im", "s5_log_dt", "s5_B_re", "s5_B_im", "s5_C_re", "s5_C_im")
    _, vjp = jax.vjp(_s5_derive, *[p[k][0] for k in keys])
    return {k: v[None] for k, v in zip(keys, vjp(tuple(dtiles)))}


def _local_step(x, target, p, w, late_weights, exchange):
    T, D = x.shape
    x1, r1 = _ffn_fwd(x, p["ffn1_norm"], w["ffn1_w_gate"], w["ffn1_w_up"], w["ffn1_w_down"], "ffn1")
    x2, rm = _mixer_fwd(x1, p, w)
    w = {**w, **late_weights(rm["p5"])}
    x3, r2 = _ffn_fwd(x2, p["ffn2_norm"], w["ffn2_w_gate"], w["ffn2_w_up"], w["ffn2_w_down"], "ffn2")

    def head(xt, tt, g):
        def f(x_, g_):
            e = _rms(x_, g_) - tt
            return 0.5 * jnp.sum(jnp.mean(e * e, axis=-1))
        l, (dx_, dg_) = jax.value_and_grad(f, argnums=(0, 1))(xt, g)
        return dx_, l.reshape(1, 1), dg_

    dx3, loss, d_final = _rowwise(head, [x3, target], [p["final_norm"]], [(D, F32)],
                                  [((1, 1), F32), ((1, D), F32)], name="loss_head")
    gp = {"final_norm": d_final}

    def ffn_exchange(tag):
        return lambda d_wgT, d_wuT, d_wd: exchange(
            tag, {f"{tag}_w_gate": d_wgT, f"{tag}_w_up": d_wuT, f"{tag}_w_down": d_wd})

    dx2, gp["ffn2_norm"], *_ = _ffn_bwd(
        dx3, x2, p["ffn2_norm"], w["ffn2_w_gate"], w["ffn2_w_up"], w["ffn2_w_down"], r2, "ffn2", (),
        ffn_exchange("ffn2"))
    dx1, gm, dtiles = _mixer_bwd(dx2, x1, p, w, rm)
    conv_w_grad = gm.pop("conv_w")
    started = exchange("mixer", {k: gm.pop(k) for k in ("w_out", "w_proj_s5", "w_proj_ssd", "s5_w_glu", "w_in")})
    gp.update(gm)
    gp.update(_s5_param_grads(p, dtiles))
    dx0, gp["ffn1_norm"], *_ = _ffn_bwd(
        dx1, x, p["ffn1_norm"], w["ffn1_w_gate"], w["ffn1_w_up"], w["ffn1_w_down"], r1, "ffn1", started,
        ffn_exchange("ffn1"))
    return loss, dx0, gp, conv_w_grad, dx1


def _adamw(w, m, v, gs, *, name, emit_g, tm=128):
    n_g = len(gs)
    C = w.shape[1]

    def fn(wt, mt, vt, *gt):
        g = gt[0]
        for e in gt[1:]:
            g = g + e
        m2 = ADAM_B1 * mt + (1.0 - ADAM_B1) * g
        v2 = ADAM_B2 * vt + (1.0 - ADAM_B2) * (g * g)
        m_hat = m2 / (1.0 - ADAM_B1 ** ADAM_STEP)
        v_hat = v2 / (1.0 - ADAM_B2 ** ADAM_STEP)
        delta = -ADAM_LR * (m_hat / (jnp.sqrt(v_hat) + ADAM_EPS) + ADAM_WD * wt)
        return ((g,) if emit_g else ()) + (delta, m2, v2)

    assert n_g >= 1
    return _rowwise(fn, [w, m, v, *gs], [], [(C, F32)] * (4 if emit_g else 3), name=name, tm=_rtile(w.shape[0], tm))


def _rtile(r, cap=512):
    if r <= cap:
        return r
    best = None
    for t in range(8, cap + 1, 8):
        if r % t == 0:
            best = t
    assert best is not None, r
    return best


ANY = pl.BlockSpec(memory_space=pl.ANY)


def _where_am_i():
    x, y, c = lax.axis_index("x"), lax.axis_index("y"), lax.axis_index("c")
    ks = (2 * x + y, 2 * (1 - x) + y, 2 * x + (1 - y), 2 * (1 - x) + (1 - y))
    return x, y, c, ks


def _rc(src, dst, ssem, rsem, to):
    return pltpu.make_async_remote_copy(src_ref=src, dst_ref=dst, send_sem=ssem, recv_sem=rsem,
                                        device_id=to, device_id_type=MESH)


def _gather_shards(srcs):
    n = len(srcs)

    def body(*refs):
        _gather_body(refs[:n], refs[n:2 * n], *refs[2 * n:], extra=())

    return pl.pallas_call(
        body, name="gather_shards",
        in_specs=[ANY] * n, out_specs=[ANY] * n,
        out_shape=[jax.ShapeDtypeStruct((4,) + s.shape, s.dtype) for s in srcs],
        scratch_shapes=[pltpu.SemaphoreType.DMA((n, 6)), pltpu.SemaphoreType.DMA((n, 6)), pltpu.SemaphoreType.DMA((n,))],
        compiler_params=pltpu.CompilerParams(has_side_effects=True),
    )(*srcs)


def _handshake_all():
    x, y, c = lax.axis_index("x"), lax.axis_index("y"), lax.axis_index("c")
    barrier = pltpu.get_barrier_semaphore()
    for dx, dy, dc in itertools.product((0, 1), repeat=3):
        if (dx, dy, dc) != (0, 0, 0):
            to = (1 - x if dx else x, 1 - y if dy else y, 1 - c if dc else c)
            pl.semaphore_signal(barrier, inc=1, device_id=to, device_id_type=MESH)
    pl.semaphore_wait(barrier, 7)


SEQUENCER = dict(axis_name="seq", num_cores=1)


def _gather_shards_async(srcs, token, name, collective_id):
    n = len(srcs)

    def body(*refs):
        tok, tok_out = refs[n], refs[2 * n + 1]
        ssem, rsem, lsem = refs[2 * n + 2:]
        _handshake_all()
        _gather_body(refs[:n], refs[n + 1:2 * n + 1], ssem, rsem, lsem,
                     extra=(pltpu.make_async_copy(tok, tok_out, lsem.at[n]),))

    res = pl.kernel(
        body, name=name,
        out_type=[jax.ShapeDtypeStruct((4,) + s.shape, s.dtype) for s in srcs]
        + [jax.ShapeDtypeStruct(token.shape, token.dtype)],
        mesh=plsc.ScalarSubcoreMesh(**SEQUENCER),
        scratch_types=[pltpu.SemaphoreType.DMA((n, 6)), pltpu.SemaphoreType.DMA((n, 6)), pltpu.SemaphoreType.DMA((n + 1,))],
        compiler_params=pltpu.CompilerParams(collective_id=collective_id),
    )(*srcs, token)
    return res[:n], res[n]


def _gather_body(src, out, ssem, rsem, lsem, extra):
        n = len(src)
        x, y, c, (k_me, k_x, k_y, k_d) = _where_am_i()
        sib = (x, y, 1 - c)
        local = [pltpu.make_async_copy(src[i], out[i].at[k_me], lsem.at[i]) for i in range(n)] + list(extra)
        for cp in local:
            cp.start()

        def own(i, q, to):
            return _rc(src[i], out[i].at[k_me], ssem.at[i, q], rsem.at[i, q], to)

        def slab(i, q, k, to):
            return _rc(out[i].at[k], out[i].at[k], ssem.at[i, q], rsem.at[i, q], to)

        @pl.when(c == 1)
        def _():
            sends = [own(i, 0, (1 - x, y, 1)) for i in range(n)]
            for cp in sends:
                cp.start()
            fwd = [slab(i, 3, k_x, sib) for i in range(n)]
            for i in range(n):
                slab(i, 0, k_x, sib).wait_recv()
                fwd[i].start()
            for i in range(n):
                slab(i, 4, k_y, sib).wait_recv()
                slab(i, 5, k_d, sib).wait_recv()
            for cp in sends + fwd:
                cp.wait_send()

        @pl.when(c == 0)
        def _():
            sends = [own(i, 1, (x, 1 - y, 0)) for i in range(n)] + [own(i, 2, (1 - x, 1 - y, 0)) for i in range(n)]
            for cp in sends:
                cp.start()
            fwd = [slab(i, 4, k_y, sib) for i in range(n)] + [slab(i, 5, k_d, sib) for i in range(n)]
            for i in range(n):
                slab(i, 1, k_y, sib).wait_recv()
                fwd[i].start()
            for i in range(n):
                slab(i, 2, k_d, sib).wait_recv()
                fwd[n + i].start()
            for i in range(n):
                slab(i, 3, k_x, sib).wait_recv()
            for cp in sends + fwd:
                cp.wait_send()

        for cp in local:
            cp.wait()


def _swap_slabs(arrs, n_slabs, name, async_id=None):
    n = len(arrs)
    J = max(n_slabs, 1)

    def body(*refs):
        src, out = refs[:n], refs[n:2 * n]
        ssem, rsem = refs[2 * n:]
        x, y, c, (k_me, k_x, k_y, k_d) = _where_am_i()
        sib = (x, y, 1 - c)
        sel = (jnp.where(c == 1, k_y, k_me), jnp.where(c == 1, k_d, k_x))
        cps = []
        for i in range(n):
            for j in range(J):
                s = src[i].at[sel[j]] if n_slabs else src[i]
                d = out[i].at[j] if n_slabs else out[i]
                cps.append(_rc(s, d, ssem.at[i, j], rsem.at[i, j], sib))
        for cp in cps:
            cp.start()
        for cp in cps:
            cp.wait()

    return _exchange_call(
        body, arrs, [jax.ShapeDtypeStruct(((n_slabs,) if n_slabs else ()) + a.shape[-2:], a.dtype) for a in arrs],
        [pltpu.SemaphoreType.DMA((n, J)), pltpu.SemaphoreType.DMA((n, J))], name, async_id)


def _exchange_call(body, arrs, out_shape, sems, name, async_id):
    n = len(arrs)
    if async_id is None:
        return pl.pallas_call(
            body, name=name, in_specs=[ANY] * n, out_specs=[ANY] * len(out_shape), out_shape=out_shape,
            scratch_shapes=sems, compiler_params=pltpu.CompilerParams(has_side_effects=True))(*arrs)

    def seq_body(*refs):
        _handshake_all()
        body(*refs)

    return pl.kernel(
        seq_body, name=name, out_type=out_shape, mesh=plsc.ScalarSubcoreMesh(**SEQUENCER), scratch_types=sems,
        compiler_params=pltpu.CompilerParams(collective_id=async_id))(*arrs)


def _send_chip_sums(arrs, name, async_id=None):
    n = len(arrs)

    def body(*refs):
        src, out = refs[:n], refs[n:2 * n]
        ssem, rsem = refs[2 * n:]
        x, y, c, _ = _where_am_i()

        @pl.when(c == 1)
        def _():
            cps = [_rc(src[i].at[1], out[i].at[0], ssem.at[i, 0], rsem.at[i, 0], (1 - x, y, 1)) for i in range(n)]
            for cp in cps:
                cp.start()
            for cp in cps:
                cp.wait()

        @pl.when(c == 0)
        def _():
            cps = [_rc(src[i].at[0], out[i].at[0], ssem.at[i, 0], rsem.at[i, 0], (x, 1 - y, 0)) for i in range(n)]
            cps += [_rc(src[i].at[1], out[i].at[1], ssem.at[i, 1], rsem.at[i, 1], (1 - x, 1 - y, 0)) for i in range(n)]
            for cp in cps:
                cp.start()
            for cp in cps:
                cp.wait()

    return _exchange_call(
        body, arrs, [jax.ShapeDtypeStruct(a.shape, a.dtype) for a in arrs],
        [pltpu.SemaphoreType.DMA((n, 2)), pltpu.SemaphoreType.DMA((n, 2))], name, async_id)


def _chip_sum(g, recv, sel, name):
    _, r, C = g.shape
    tr = _rtile(r)

    def body(sel_ref, g_ref, r_ref, o32_ref, o16_ref):
        s = g_ref[...] + r_ref[...]
        o32_ref[...] = s
        o16_ref[...] = s.astype(BF16)

    blk = pl.BlockSpec((None, tr, C), lambda j, t, sel_ref: (j, t, 0))
    return pl.pallas_call(
        body, name=name,
        grid_spec=pltpu.PrefetchScalarGridSpec(
            num_scalar_prefetch=1, grid=(2, r // tr),
            in_specs=[pl.BlockSpec((None, tr, C), lambda j, t, sel_ref: (sel_ref[j], t, 0)), blk],
            out_specs=[blk, blk]),
        out_shape=[jax.ShapeDtypeStruct((2, r, C), F32), jax.ShapeDtypeStruct((2, r, C), BF16)],
        compiler_params=_params(("arbitrary", "arbitrary")),
    )(sel, g, recv)


def _cross_sum(s32, got, north, after, name):
    _, r, C = s32.shape
    tr = _rtile(r)

    def body(f_ref, p_ref, a_ref, b_ref, after_ref, o_ref):
        a = a_ref[...].astype(F32)

        @pl.when(f_ref[0] == 1)
        def _():
            o_ref[...] = p_ref[...] + a

        @pl.when(f_ref[0] == 0)
        def _():
            o_ref[...] = a + b_ref[...].astype(F32)

    return pl.pallas_call(
        body, name=name,
        grid_spec=pltpu.PrefetchScalarGridSpec(
            num_scalar_prefetch=1, grid=(r // tr,),
            in_specs=[pl.BlockSpec((None, tr, C), lambda t, f: (0, t, 0)),
                      pl.BlockSpec((None, tr, C), lambda t, f: (0, t, 0)),
                      pl.BlockSpec((None, tr, C), lambda t, f: (1 - f[0], t, 0)), ANY],
            out_specs=pl.BlockSpec((tr, C), lambda t, f: (t, 0))),
        out_shape=jax.ShapeDtypeStruct((r, C), F32),
        compiler_params=_params(("arbitrary",)),
    )(north, s32, got, got, after)


def _reduce_scatter_begin(grads, tag, async_ids):
    x, y, c, (k_me, k_x, k_y, k_d) = _where_am_i()
    sel = jnp.stack([jnp.where(c == 1, k_me, k_y), jnp.where(c == 1, k_x, k_d)]).astype(jnp.int32)
    from_sib = _swap_slabs(grads, 2, f"swap_grad_slabs_{tag}", async_ids[0])
    sums = [_chip_sum(g, r, sel, f"chip_sum_{tag}{i}") for i, (g, r) in enumerate(zip(grads, from_sib))]
    got = _send_chip_sums([s16 for _, s16 in sums], f"send_chip_sums_{tag}", async_ids[1])
    return sums, got


def _reduce_scatter_end(sums, got, tag, after, async_id):
    north = jnp.reshape(lax.axis_index("c"), (1,)).astype(jnp.int32)
    parts = [_cross_sum(s32, b, north, after, f"cross_sum_{tag}{i}") for i, ((s32, _), b) in enumerate(zip(sums, got))]
    theirs = _swap_slabs(parts, 0, f"swap_parts_{tag}", async_id)
    return list(zip(parts, theirs))


def _all_reduce_small(v):
    R, C = v.shape

    def body(v_ref, o_ref, buf, ssem, rsem, lsem):
        x, y, c = lax.axis_index("x"), lax.axis_index("y"), lax.axis_index("c")
        me, sib = (x, y, c), (x, y, 1 - c)
        chips = [(1 - x, y), (x, 1 - y), (1 - x, 1 - y)]

        def slot(px, py, pc):
            return buf.at[4 * px + 2 * py + pc]

        def copy(k, block, to, src=None):
            return _rc(slot(*block) if src is None else src, slot(*block), ssem.at[k], rsem.at[k], to)

        mine = pltpu.make_async_copy(v_ref, slot(*me), lsem)
        mine.start()
        first = [copy(0, me, sib, src=v_ref)] + [copy(1 + j, me, (*chip, c), src=v_ref) for j, chip in enumerate(chips)]
        for cp in first:
            cp.start()
        passed = [copy(4 + j, (*chip, c), sib) for j, chip in enumerate(chips)]
        for j, chip in enumerate(chips):
            copy(1 + j, (*chip, c), me).wait_recv()
            passed[j].start()
        copy(0, sib, me).wait_recv()
        for j, chip in enumerate(chips):
            copy(4 + j, (*chip, 1 - c), me).wait_recv()
        for cp in first + passed:
            cp.wait_send()
        mine.wait()
        acc = buf[0]
        for k in range(1, 8):
            acc = acc + buf[k]
        o_ref[...] = acc

    return pl.pallas_call(
        body, name="all_reduce_small",
        in_specs=[pl.BlockSpec(memory_space=pltpu.VMEM)], out_specs=pl.BlockSpec(memory_space=pltpu.VMEM),
        out_shape=jax.ShapeDtypeStruct((R, C), F32),
        scratch_shapes=[pltpu.VMEM((8, R, C), F32), pltpu.SemaphoreType.DMA((7,)), pltpu.SemaphoreType.DMA((7,)),
                        pltpu.SemaphoreType.DMA],
        compiler_params=pltpu.CompilerParams(has_side_effects=True, vmem_limit_bytes=VMEM_LIMIT),
    )(v)


WEIGHTS = ['ffn1_norm', 'ffn1_w_gate', 'ffn1_w_up', 'ffn1_w_down', 'mix_norm', 'w_in', 'conv_w', 'conv_b', 's5_A_re',
           's5_A_im', 's5_log_dt', 's5_B_re', 's5_B_im', 's5_C_re', 's5_C_im', 's5_D', 's5_w_glu', 's5_b_glu',
           'ssd_A_log', 'ssd_dt_bias', 'ssd_D', 'ssd_norm', 'w_proj_s5', 'w_proj_ssd', 'b_gate', 'w_out', 'ffn2_norm',
           'ffn2_w_gate', 'ffn2_w_up', 'ffn2_w_down', 'final_norm']
ARGS = ['x'] + WEIGHTS + ['loss_target'] + ['m_' + n for n in WEIGHTS] + ['v_' + n for n in WEIGHTS]
COL_SHARDED = ('ffn1_w_gate', 'ffn1_w_up', 'w_in', 'ffn2_w_gate', 'ffn2_w_up')
ROW_SHARDED = ('ffn1_w_down', 's5_w_glu', 'w_proj_s5', 'w_proj_ssd', 'w_out', 'ffn2_w_down')
MATRICES = COL_SHARDED + ROW_SHARDED
GATHER_FIRST = ('ffn1_w_gate', 'ffn1_w_up', 'ffn1_w_down')
GATHER_SECOND = ('w_in', 'conv_w', 's5_w_glu', 'w_proj_s5', 'w_proj_ssd', 'w_out')
GATHER_THIRD = ('ffn2_w_gate', 'ffn2_w_up', 'ffn2_w_down')
SMALL = [n for n in WEIGHTS if n not in MATRICES]


def _pack(arrs, width=1024):
    flat = jnp.concatenate([a.reshape(-1).astype(F32) for a in arrs])
    rows = -(-flat.shape[0] // (8 * width)) * 8
    return jnp.pad(flat, (0, rows * width - flat.shape[0])).reshape(rows, width)


def _unpack(packed, shapes):
    flat, out, o = packed.reshape(-1), [], 0
    for s in shapes:
        n = math.prod(s)
        out.append(flat[o:o + n].reshape(s))
        o += n
    return out


def kernel(x, ffn1_norm, ffn1_w_gate, ffn1_w_up, ffn1_w_down, mix_norm, w_in, conv_w, conv_b, s5_A_re, s5_A_im, s5_log_dt, s5_B_re, s5_B_im, s5_C_re, s5_C_im, s5_D, s5_w_glu, s5_b_glu, ssd_A_log, ssd_dt_bias, ssd_D, ssd_norm, w_proj_s5, w_proj_ssd, b_gate, w_out, ffn2_norm, ffn2_w_gate, ffn2_w_up, ffn2_w_down, final_norm, loss_target, m_ffn1_norm, m_ffn1_w_gate, m_ffn1_w_up, m_ffn1_w_down, m_mix_norm, m_w_in, m_conv_w, m_conv_b, m_s5_A_re, m_s5_A_im, m_s5_log_dt, m_s5_B_re, m_s5_B_im, m_s5_C_re, m_s5_C_im, m_s5_D, m_s5_w_glu, m_s5_b_glu, m_ssd_A_log, m_ssd_dt_bias, m_ssd_D, m_ssd_norm, m_w_proj_s5, m_w_proj_ssd, m_b_gate, m_w_out, m_ffn2_norm, m_ffn2_w_gate, m_ffn2_w_up, m_ffn2_w_down, m_final_norm, v_ffn1_norm, v_ffn1_w_gate, v_ffn1_w_up, v_ffn1_w_down, v_mix_norm, v_w_in, v_conv_w, v_conv_b, v_s5_A_re, v_s5_A_im, v_s5_log_dt, v_s5_B_re, v_s5_B_im, v_s5_C_re, v_s5_C_im, v_s5_D, v_s5_w_glu, v_s5_b_glu, v_ssd_A_log, v_ssd_dt_bias, v_ssd_D, v_ssd_norm, v_w_proj_s5, v_w_proj_ssd, v_b_gate, v_w_out, v_ffn2_norm, v_ffn2_w_gate, v_ffn2_w_up, v_ffn2_w_down, v_final_norm):
    a = dict(locals())
    assert list(a) == ARGS
    x, target = a['x'][0], a['loss_target'][0]
    k_me = 2 * lax.axis_index("x") + lax.axis_index("y")

    src = {n: a[n][0].T.astype(BF16) for n in COL_SHARDED}
    src.update({n: a[n][0].astype(BF16) for n in ROW_SHARDED})
    src['conv_w'] = a['conv_w'][0]
    def flat(f):
        return f.reshape(-1, f.shape[-1])

    full = _gather_shards([src[n] for n in GATHER_FIRST])
    w = {n: flat(f) for n, f in zip(GATHER_FIRST, full)}
    got, _ = _gather_shards_async([src[n] for n in GATHER_SECOND], full[0][0, :8, :128].astype(F32), "gather_mixer", 1)
    w.update({n: flat(f) for n, f in zip(GATHER_SECOND, got)})
    w['conv_w'] = w['conv_w'].reshape(4, CONV_K, -1).transpose(1, 0, 2).reshape(CONV_K, CONV_DIM)

    def gather_last(after):
        got, _ = _gather_shards_async([src[n] for n in GATHER_THIRD], after[:8, :128], "gather_ffn2", 2)
        return {n: flat(f) for n, f in zip(GATHER_THIRD, got)}

    p = {n: a[n] for n in SMALL if n != 'conv_w'}
    p['final_norm'] = a['final_norm'][None]

    ids = {"ffn2": (3, 4, 5), "mixer": (6, 7, 8), "ffn1": (9, 10, 11)}
    begun = {}

    def exchange(tag, grads):
        names = list(grads)
        sums, got = _reduce_scatter_begin([grads[n].reshape(4, -1, D_MODEL) for n in names], tag, ids[tag][:2])
        begun[tag] = (names, sums, got)
        return tuple(s16 for _, s16 in sums)

    loss, grad_x, gp, conv_w_grad, dx_mixer = _local_step(x, target, p, w, gather_last, exchange)
    pairs = {}
    for tag, after in (("ffn2", dx_mixer), ("mixer", grad_x), ("ffn1", grad_x)):
        names, sums, got = begun[tag]
        pairs.update(zip(names, _reduce_scatter_end(sums, got, tag, after, ids[tag][2])))
    pairs = [pairs[n] for n in MATRICES]
    gp['conv_w'] = conv_w_grad[None]
    small_shapes = [(1, 1)] + [gp[n].shape if n != 'final_norm' else (1, D_MODEL) for n in SMALL]
    red = _unpack(_all_reduce_small(_pack([loss] + [gp[n] for n in SMALL])), small_shapes)
    loss_all = red[0].reshape(())
    gsmall = dict(zip(SMALL, red[1:]))
    gsmall['conv_w'] = lax.dynamic_slice_in_dim(gsmall['conv_w'], k_me * 1024, 1024, axis=2)
    gsmall = {n: g.reshape(a[n].shape) for n, g in gsmall.items()}

    grads, delta, new_m, new_v = {}, {}, {}, {}
    for n, (mine, theirs) in zip(MATRICES, pairs):
        wn, mn, vn = a[n][0], a['m_' + n][0], a['v_' + n][0]
        if n in COL_SHARDED:
            (gt,) = _rowwise(lambda p_, q_: (p_ + q_,), [mine, theirs], [], [(D_MODEL, F32)], name=f"sum_{n}",
                             tm=_rtile(mine.shape[0]))
            g = gt.T
            d, m2, v2 = _adamw(wn, mn, vn, [g], name=f"adamw_{n}", emit_g=False)
        else:
            g, d, m2, v2 = _adamw(wn, mn, vn, [mine, theirs], name=f"adamw_{n}", emit_g=True)
        grads[n], delta[n], new_m[n], new_v[n] = g[None], d[None], m2[None], v2[None]
    sw, sm, sv, sg = [_pack([t[n] for n in SMALL]) for t in (a, {n: a['m_' + n] for n in SMALL},
                                                             {n: a['v_' + n] for n in SMALL}, gsmall)]
    d, m2, v2 = _adamw(sw, sm, sv, [sg], name="adamw_small", emit_g=False)
    shapes = [a[n].shape for n in SMALL]
    for n, dd, mm, vv in zip(SMALL, _unpack(d, shapes), _unpack(m2, shapes), _unpack(v2, shapes)):
        grads[n], delta[n], new_m[n], new_v[n] = gsmall[n], dd, mm, vv
    return (loss_all, grad_x[None], *[grads[n] for n in WEIGHTS], *[delta[n] for n in WEIGHTS],
            *[new_m[n] for n in WEIGHTS], *[new_v[n] for n in WEIGHTS])
```

```python
import itertools
import math

import jax
import jax.numpy as jnp
from jax import lax
from jax.experimental import pallas as pl
from jax.experimental.pallas import tpu as pltpu
from jax.experimental.pallas import tpu_sc as plsc

F32 = jnp.float32
BF16 = jnp.bfloat16
MXU = BF16
HI = lax.Precision.HIGHEST

D_MODEL = 1024
D_FF = 2816
EPS = 1e-6
S5_GROUPS, S5_GROUP, S5_STATE = 64, 16, 64
D_INNER = 2048
SSD_HEADDIM, SSD_HEADS, SSD_GROUPS, SSD_HPG, SSD_STATE = 64, 32, 8, 4, 128
CONV_K, CONV_DIM = 4, 4096
IN_SPLITS = (1024, 2048, 4096, 32, 2048)
IN_OFFS = (0, 1024, 3072, 7168, 7200, 9248)
SSD_Q = 256
SSD_STEP_GROUPS = 2
ADAM_LR, ADAM_B1, ADAM_B2, ADAM_EPS, ADAM_WD, ADAM_STEP = 0.001, 0.9, 0.999, 1e-08, 0.01, 10

VMEM_LIMIT = 56 * 1024 * 1024
MATMUL_VMEM = 32 * 1024 * 1024
MESH = pl.DeviceIdType.MESH

NN = ((1,), (0,))
NT = ((1,), (1,))
TN = ((0,), (0,))


def _dot(a, b, dims, precision=None):
    if precision is None:
        a, b = a.astype(MXU), b.astype(MXU)
    return lax.dot_general(a, b, (dims, ((), ())), precision=precision, preferred_element_type=F32)


def _tile(n, pref):
    if n <= pref:
        return n
    best = None
    for t in range(128, pref + 1, 128):
        if n % t == 0:
            best = t
    assert best is not None, (n, pref)
    return best


def _params(sem):
    return pltpu.CompilerParams(dimension_semantics=sem, vmem_limit_bytes=VMEM_LIMIT)


def _matmul(a, b, mode, *, name, out=((F32),), epilogue=None, extras=(), after=(), tm=512, tn=1536):
    if mode == "nn":
        (M, K), (_, N) = a.shape, b.shape
    elif mode == "nt":
        (M, K), (N, _) = a.shape, b.shape
    else:
        (K, M), (_, N) = a.shape, b.shape
    tm, tn = _tile(M, tm), _tile(N, tn)

    def vmem(tm_, tn_):
        per = K * tm_ * a.dtype.itemsize + K * tn_ * b.dtype.itemsize
        per += sum(tm_ * tn_ * jnp.dtype(dt).itemsize for dt in out)
        per += sum((1 if e.shape[0] == 1 else tm_) * tn_ * e.dtype.itemsize for e in extras)
        return 2 * per

    while vmem(tm, tn) > MATMUL_VMEM and (tn > 128 or tm > 128):
        if tn >= tm and tn > 128:
            tn = _tile(N, tn - 128)
        else:
            tm = _tile(M, tm - 128)
    a_spec = pl.BlockSpec((K, tm), lambda i, j: (0, i)) if mode == "tn" else pl.BlockSpec((tm, K), lambda i, j: (i, 0))
    b_spec = pl.BlockSpec((tn, K), lambda i, j: (j, 0)) if mode == "nt" else pl.BlockSpec((K, tn), lambda i, j: (0, j))
    dims = {"nn": NN, "nt": NT, "tn": TN}[mode]
    e_specs = []
    for e in extras:
        if e.shape[0] == 1:
            e_specs.append(pl.BlockSpec((1, tn), lambda i, j: (0, j)))
        else:
            assert e.shape == (M, N), (e.shape, M, N)
            e_specs.append(pl.BlockSpec((tm, tn), lambda i, j: (i, j)))
    n_e, n_o, n_a = len(extras), len(out), len(after)

    def body(a_ref, b_ref, *refs):
        acc = _dot(a_ref[...], b_ref[...], dims)
        res = (acc,) if epilogue is None else epilogue(acc, *[r[...] for r in refs[:n_e]])
        for r, v in zip(refs[n_e + n_a:], res):
            r[...] = v.astype(r.dtype)

    res = pl.pallas_call(
        body, name=name, grid=(M // tm, N // tn),
        in_specs=[a_spec, b_spec] + e_specs + [ANY] * n_a,
        out_specs=[pl.BlockSpec((tm, tn), lambda i, j: (i, j)) for _ in range(n_o)],
        out_shape=[jax.ShapeDtypeStruct((M, N), dt) for dt in out],
        compiler_params=_params(("parallel", "parallel")),
    )(a, b, *extras, *after)
    return res[0] if n_o == 1 else res


def _rowwise(fn, rows, fulls, outs, reds=(), *, name, after=(), tm=256):
    T = rows[0].shape[0]
    tm = min(tm, T)
    n_r, n_f, n_o, n_d, n_a = len(rows), len(fulls), len(outs), len(reds), len(after)

    def body(*refs):
        ins = [r[...] for r in refs[:n_r + n_f]]
        o_refs = refs[n_r + n_f + n_a:n_r + n_f + n_a + n_o]
        d_refs = refs[n_r + n_f + n_a + n_o:]
        res = fn(*ins)
        for r, v in zip(o_refs, res[:n_o]):
            r[...] = v.astype(r.dtype)
        if n_d:
            @pl.when(pl.program_id(0) == 0)
            def _():
                for r in d_refs:
                    r[...] = jnp.zeros_like(r)
            for r, v in zip(d_refs, res[n_o:]):
                r[...] += v.astype(r.dtype)

    res = pl.pallas_call(
        body, name=name, grid=(T // tm,),
        in_specs=[pl.BlockSpec((tm, r.shape[1]), lambda i: (i, 0)) for r in rows]
        + [pl.BlockSpec(f.shape, lambda i, nd=f.ndim: (0,) * nd) for f in fulls] + [ANY] * n_a,
        out_specs=[pl.BlockSpec((tm, c), lambda i: (i, 0)) for c, _ in outs]
        + [pl.BlockSpec(s, lambda i, nd=len(s): (0,) * nd) for s, _ in reds],
        out_shape=[jax.ShapeDtypeStruct((T, c), dt) for c, dt in outs]
        + [jax.ShapeDtypeStruct(s, dt) for s, dt in reds],
        compiler_params=_params(("arbitrary",)),
    )(*rows, *fulls, *after)
    return res


def _rms(x, g):
    return x * lax.rsqrt(jnp.mean(x * x, axis=-1, keepdims=True) + EPS) * g


def _colsum(v):
    return jnp.sum(v, axis=0, keepdims=True)


def _softplus(x):
    return jnp.maximum(x, 0.0) + jnp.log1p(jnp.exp(-jnp.abs(x)))


def _ffn_fwd(x, norm, wgT, wuT, wd, tag):
    D = x.shape[1]
    (hn,) = _rowwise(lambda xt, g: (_rms(xt, g),), [x], [norm], [(D, BF16)], name=f"{tag}_norm")
    a = _matmul(hn, wgT, "nt", name=f"{tag}_gate")
    b, hid = _matmul(hn, wuT, "nt", name=f"{tag}_up", out=(BF16, BF16),
                     epilogue=lambda acc, at: (acc, jax.nn.silu(at) * acc), extras=[a])
    y = _matmul(hid, wd, "nn", name=f"{tag}_down", epilogue=lambda acc, xt: (xt + 0.5 * acc,), extras=[x])
    return y, (hn, a, b, hid)


def _ffn_bwd(dy, x, norm, wgT, wuT, wd, res, tag, on_grads):
    hn, a, b, hid = res
    D, F = x.shape[1], a.shape[1]

    def act_bwd(acc, at, bt):
        _, vjp = jax.vjp(lambda p, q: jax.nn.silu(p) * q, at, bt.astype(F32))
        return vjp(0.5 * acc)

    da, db = _matmul(dy, wd, "nt", name=f"{tag}_d_hid", out=(BF16, BF16), epilogue=act_bwd, extras=[a, b])
    d_wd = _matmul(hid, dy, "tn", name=f"{tag}_d_wd", epilogue=lambda acc: (0.5 * acc,))
    d_wgT = _matmul(da, hn, "tn", name=f"{tag}_d_wg")
    d_wuT = _matmul(db, hn, "tn", name=f"{tag}_d_wu")
    on_grads(d_wgT, d_wuT, d_wd)
    dhn = _matmul(da, wgT, "nn", name=f"{tag}_d_hn1")
    dhn = _matmul(db, wuT, "nn", name=f"{tag}_d_hn2", epilogue=lambda acc, e: (acc + e,), extras=[dhn])

    def norm_bwd(xt, dh, dyt, g):
        _, vjp = jax.vjp(_rms, xt, g)
        dx, dg = vjp(dh)
        return dyt + dx, dg

    dx, d_norm = _rowwise(norm_bwd, [x, dhn, dy], [norm], [(D, F32)], [((1, D), F32)], name=f"{tag}_d_norm")
    return dx, d_norm, d_wgT, d_wuT, d_wd


S5_TILES = 8
S5_HALF = 256


def _s5_derive(A_re, A_im, log_dt, B_re, B_im, C_re, C_im):
    G, N, M = S5_GROUPS, S5_STATE, S5_GROUP
    dt = jnp.exp(log_dt)[:, None]
    mag = jnp.exp(A_re * dt)
    ar, ai = mag * jnp.cos(A_im * dt), mag * jnp.sin(A_im * dt)
    den = A_re * A_re + A_im * A_im
    cr = ((ar - 1.0) * A_re + ai * A_im) / den
    ci = (ai * A_re - (ar - 1.0) * A_im) / den
    bbr = cr[..., None] * B_re - ci[..., None] * B_im
    bbi = cr[..., None] * B_im + ci[..., None] * B_re
    eye = jnp.eye(8, dtype=F32)

    def tile_in(bb):
        t = bb.reshape(S5_TILES, 8, N, M).transpose(0, 1, 3, 2)
        return jnp.einsum("jamn,ab->jambn", t, eye).reshape(S5_TILES, 8 * M, 8 * N)

    def tile_out(c):
        t = c.reshape(S5_TILES, 8, M, N).transpose(0, 1, 3, 2)
        return jnp.einsum("janm,ab->janbm", t, eye).reshape(S5_TILES, 8 * N, 8 * M)

    return (tile_in(bbr), tile_in(bbi), tile_out(C_re), tile_out(C_im),
            ar.reshape(S5_TILES, 1, 8 * N), ai.reshape(S5_TILES, 1, 8 * N))


S5_NB = S5_HALF // 128
S5_SEG = 32


def _cmul(ar, ai, br, bi):
    return ar * br - ai * bi, ar * bi + ai * br


def _s5_scan(sr_ref, si_ref, ar, ai, T, reverse):
    L, V = T // S5_SEG, S5_SEG // 8
    assert L * S5_SEG == T and L & (L - 1) == 0, T
    sg = -1.0 if reverse else 1.0
    a_r = [jnp.broadcast_to(ar[:, 128 * b:128 * (b + 1)], (8, 128)) for b in range(S5_NB)]
    a_i = [jnp.broadcast_to(sg * ai[:, 128 * b:128 * (b + 1)], (8, 128)) for b in range(S5_NB)]

    def rows(k, v):
        return pl.ds(pl.multiple_of(((L - 1 - k) if reverse else k) * S5_SEG + 8 * v, 8), 8)

    def local(k, carry):
        out = []
        for b in range(S5_NB):
            for v in range(V):
                idx = rows(k, v)
                mr, mi = _cmul(a_r[b], a_i[b], *carry[b * V + v])
                nr, ni = mr + sr_ref[b, idx, :], mi + si_ref[b, idx, :]
                sr_ref[b, idx, :] = nr
                si_ref[b, idx, :] = ni
                out.append((nr, ni))
        return tuple(out)

    z = jnp.zeros((8, 128), F32)
    ends = lax.fori_loop(0, L, local, tuple((z, z) for _ in range(S5_NB * V)))

    carries = []
    for b in range(S5_NB):
        pr, pi = a_r[b][0:1], a_i[b][0:1]
        n = L
        while n > 1:
            pr, pi = _cmul(pr, pi, pr, pi)
            n //= 2
        cr, ci = jnp.zeros((1, 128), F32), jnp.zeros((1, 128), F32)
        into = [None] * S5_SEG
        for j in (reversed(range(S5_SEG)) if reverse else range(S5_SEG)):
            into[j] = (cr, ci)
            er, ei = ends[b * V + j // 8]
            mr, mi = _cmul(pr, pi, cr, ci)
            cr, ci = mr + er[j % 8:j % 8 + 1], mi + ei[j % 8:j % 8 + 1]
        carries.append([(jnp.concatenate([into[8 * v + s][0] for s in range(8)], axis=0),
                         jnp.concatenate([into[8 * v + s][1] for s in range(8)], axis=0)) for v in range(V)])

    def fix(k, powers):
        out = []
        for b in range(S5_NB):
            pr, pi = powers[b]
            for v in range(V):
                idx = rows(k, v)
                dr, di = _cmul(pr, pi, *carries[b][v])
                sr_ref[b, idx, :] += dr
                si_ref[b, idx, :] += di
            out.append(_cmul(pr, pi, a_r[b], a_i[b]))
        return tuple(out)

    lax.fori_loop(0, L, fix, tuple((a_r[b], a_i[b]) for b in range(S5_NB)))


def _s5_in(ut, b_ref, s_ref):
    for b in range(S5_NB):
        s_ref[b] = _dot(ut, b_ref[:, 128 * b:128 * (b + 1)], NN)


def _to_seg(v):
    T, C = v.shape
    return v.reshape(S5_SEG, T // S5_SEG, C).transpose(1, 0, 2).reshape(T, C)


def _from_seg(v):
    T, C = v.shape
    return v.reshape(T // S5_SEG, S5_SEG, C).transpose(1, 0, 2).reshape(T, C)


def _s5_specs(T):
    u_spec = pl.BlockSpec((T, 128), lambda j, h: (0, j))
    b_spec = pl.BlockSpec((None, 128, S5_HALF), lambda j, h: (j, 0, h))
    c_spec = pl.BlockSpec((None, S5_HALF, 128), lambda j, h: (j, h, 0))
    a_spec = pl.BlockSpec((None, 1, S5_HALF), lambda j, h: (j, 0, h))
    return u_spec, b_spec, c_spec, a_spec


def _s5_fwd(u, tiles):
    T = u.shape[0]
    u_spec, b_spec, c_spec, a_spec = _s5_specs(T)

    def body(u_ref, br_ref, bi_ref, cr_ref, ci_ref, ar_ref, ai_ref, y_ref, sr_ref, si_ref):
        ut = u_ref[...].astype(MXU)
        _s5_in(ut, br_ref, sr_ref)
        _s5_in(ut, bi_ref, si_ref)
        _s5_scan(sr_ref, si_ref, ar_ref[...], ai_ref[...], T, False)
        y = None
        for b in range(S5_NB):
            blk = slice(128 * b, 128 * (b + 1))
            yb = _dot(sr_ref[b], cr_ref[blk, :], NN) - _dot(si_ref[b], ci_ref[blk, :], NN)
            y = yb if y is None else y + yb

        @pl.when(pl.program_id(1) == 0)
        def _():
            y_ref[...] = y

        @pl.when(pl.program_id(1) == 1)
        def _():
            y_ref[...] += y

    scr = pltpu.VMEM((S5_NB, T, 128), F32)
    return pl.pallas_call(
        body, name="s5_fwd", grid=(S5_TILES, 2),
        in_specs=[u_spec, b_spec, b_spec, c_spec, c_spec, a_spec, a_spec],
        out_specs=u_spec, out_shape=jax.ShapeDtypeStruct(u.shape, F32),
        scratch_shapes=[scr, scr],
        compiler_params=_params(("parallel", "arbitrary")),
    )(u, *tiles)


def _s5_bwd(u, dys, du_skip, tiles):
    T = u.shape[0]
    u_spec, b_spec, c_spec, a_spec = _s5_specs(T)

    def body(u_ref, dy_ref, sk_ref, br_ref, bi_ref, cr_ref, ci_ref, ar_ref, ai_ref,
             du_ref, dbr_ref, dbi_ref, dcr_ref, dci_ref, dar_ref, dai_ref, sr_ref, si_ref, lr_ref, li_ref):
        ut, dy = u_ref[...].astype(MXU), dy_ref[...].astype(MXU)
        ar, ai = ar_ref[...], ai_ref[...]
        _s5_in(ut, br_ref, sr_ref)
        _s5_in(ut, bi_ref, si_ref)
        _s5_scan(sr_ref, si_ref, ar, ai, T, False)
        for b in range(S5_NB):
            blk = slice(128 * b, 128 * (b + 1))
            dcr_ref[blk, :] = _dot(sr_ref[b], dy, TN)
            dci_ref[blk, :] = -_dot(si_ref[b], dy, TN)
            lr_ref[b] = _dot(dy, cr_ref[blk, :], NT)
            li_ref[b] = -_dot(dy, ci_ref[blk, :], NT)
        _s5_scan(lr_ref, li_ref, ar, ai, T, True)
        du = None
        head = lax.broadcasted_iota(jnp.int32, (T, 128), 0) < S5_SEG
        seg0 = lax.broadcasted_iota(jnp.int32, (S5_SEG, 128), 0) == 0
        for b in range(S5_NB):
            blk = slice(128 * b, 128 * (b + 1))
            lr, li = lr_ref[b], li_ref[b]
            pr = jnp.where(head, 0.0, pltpu.roll(sr_ref[b], S5_SEG, 0))
            pi = jnp.where(head, 0.0, pltpu.roll(si_ref[b], S5_SEG, 0))
            er = jnp.where(seg0, 0.0, pltpu.roll(sr_ref[b, T - S5_SEG:T, :], 1, 0))
            ei = jnp.where(seg0, 0.0, pltpu.roll(si_ref[b, T - S5_SEG:T, :], 1, 0))
            hr, hi = lr[0:S5_SEG], li[0:S5_SEG]
            dar_ref[:, blk] = _colsum(lr * pr + li * pi) + _colsum(hr * er + hi * ei)
            dai_ref[:, blk] = _colsum(li * pr - lr * pi) + _colsum(hi * er - hr * ei)
            dbr_ref[:, blk] = _dot(ut, lr, TN)
            dbi_ref[:, blk] = _dot(ut, li, TN)
            dub = _dot(lr, br_ref[:, blk], NT) + _dot(li, bi_ref[:, blk], NT)
            du = dub if du is None else du + dub

        @pl.when(pl.program_id(1) == 0)
        def _():
            du_ref[...] = du + sk_ref[...]

        @pl.when(pl.program_id(1) == 1)
        def _():
            du_ref[...] += du

    scr = pltpu.VMEM((S5_NB, T, 128), F32)
    return pl.pallas_call(
        body, name="s5_bwd", grid=(S5_TILES, 2),
        in_specs=[u_spec, u_spec, u_spec, b_spec, b_spec, c_spec, c_spec, a_spec, a_spec],
        out_specs=[u_spec, b_spec, b_spec, c_spec, c_spec, a_spec, a_spec],
        out_shape=[jax.ShapeDtypeStruct(u.shape, F32)] + [jax.ShapeDtypeStruct(t.shape, F32) for t in tiles],
        scratch_shapes=[scr, scr, scr, scr],
        compiler_params=_params(("parallel", "arbitrary")),
    )(u, dys, du_skip, *tiles)


CONV_TILE = 256


def _conv_pre(x, w, b):
    T = x.shape[0]
    row = lax.broadcasted_iota(jnp.int32, x.shape, 0)
    acc = x * w[CONV_K - 1:CONV_K, :] + b
    for lag in range(1, CONV_K):
        acc = acc + jnp.where(row >= lag, pltpu.roll(x, lag, 0), 0.0) * w[CONV_K - 1 - lag:CONV_K - lag, :]
    return acc


def _conv_fwd(x, w, b):
    T, C = x.shape
    col = pl.BlockSpec((T, CONV_TILE), lambda j: (0, j))

    def body(x_ref, w_ref, b_ref, o_ref):
        o_ref[...] = jax.nn.silu(_conv_pre(x_ref[...], w_ref[...], b_ref[...]))

    return pl.pallas_call(
        body, name="conv_fwd", grid=(C // CONV_TILE,),
        in_specs=[col, pl.BlockSpec((CONV_K, CONV_TILE), lambda j: (0, j)), pl.BlockSpec((1, CONV_TILE), lambda j: (0, j))],
        out_specs=col, out_shape=jax.ShapeDtypeStruct((T, C), F32),
        compiler_params=_params(("parallel",)),
    )(x, w, b)


def _conv_bwd(x, w, b, dout, col0, name):
    T, C = dout.shape
    off = col0 // CONV_TILE
    xcol = pl.BlockSpec((T, CONV_TILE), lambda j: (0, j + off))
    dcol = pl.BlockSpec((T, CONV_TILE), lambda j: (0, j))

    def body(x_ref, w_ref, b_ref, d_ref, dx_ref, dw_ref, db_ref):
        x, w = x_ref[...], w_ref[...]
        _, vjp = jax.vjp(jax.nn.silu, _conv_pre(x, w, b_ref[...]))
        (dy,) = vjp(d_ref[...])
        row = lax.broadcasted_iota(jnp.int32, x.shape, 0)
        dx = dy * w[CONV_K - 1:CONV_K, :]
        dw_ref[CONV_K - 1:CONV_K, :] = _colsum(dy * x)
        for lag in range(1, CONV_K):
            dx = dx + jnp.where(row < T - lag, pltpu.roll(dy, T - lag, 0), 0.0) * w[CONV_K - 1 - lag:CONV_K - lag, :]
            xs = jnp.where(row >= lag, pltpu.roll(x, lag, 0), 0.0)
            dw_ref[CONV_K - 1 - lag:CONV_K - lag, :] = _colsum(dy * xs)
        dx_ref[...] = dx.astype(dx_ref.dtype)
        db_ref[...] = _colsum(dy)

    return pl.pallas_call(
        body, name=name, grid=(C // CONV_TILE,),
        in_specs=[xcol, pl.BlockSpec((CONV_K, CONV_TILE), lambda j: (0, j + off)),
                  pl.BlockSpec((1, CONV_TILE), lambda j: (0, j + off)), dcol],
        out_specs=[dcol, pl.BlockSpec((CONV_K, CONV_TILE), lambda j: (0, j)), pl.BlockSpec((1, CONV_TILE), lambda j: (0, j))],
        out_shape=[jax.ShapeDtypeStruct((T, C), BF16), jax.ShapeDtypeStruct((CONV_K, C), F32), jax.ShapeDtypeStruct((1, C), F32)],
        compiler_params=_params(("parallel",)),
    )(x, w, b, dout)


def _ssd_common(x_ref, b_ref, c_ref, dtr_ref, bias_ref, alog_ref, tri_ref, triu_ref):
    Q = x_ref.shape[0]
    x, Bm, Cm = x_ref[...], b_ref[...], c_ref[...]
    pre = dtr_ref[...] + bias_ref[...]
    dt = _softplus(pre)
    A = -jnp.exp(alog_ref[...])
    adt = dt * A
    ac4 = _dot(tri_ref[...], adt, NN, HI)
    ar4 = _dot(adt, triu_ref[...], TN, HI)
    cb = _dot(Cm, Bm, NT)
    ii = lax.broadcasted_iota(jnp.int32, (Q, Q), 0)
    jj = lax.broadcasted_iota(jnp.int32, (Q, Q), 1)
    atot4 = ac4[Q - 1:Q, :]
    return x, Bm, Cm, pre, dt, A, ac4, ar4, cb, ii >= jj, atot4


def _ssd_decay(k, ac4, ar4, causal):
    seg = ac4[:, k:k + 1] - ar4[k:k + 1, :]
    return jnp.where(causal, jnp.exp(jnp.where(causal, seg, 0.0)), 0.0)


def _per_head(c4, n):
    lane = lax.broadcasted_iota(jnp.int32, (c4.shape[0], n), 1)
    out = jnp.broadcast_to(c4[:, 0:1], (c4.shape[0], n))
    for k in range(1, SSD_HPG):
        out = jnp.where(lane >= SSD_HEADDIM * k, c4[:, k:k + 1], out)
    return out


def _head_sums(v):
    lane = lax.broadcasted_iota(jnp.int32, (v.shape[0], SSD_HPG), 1)
    out = jnp.zeros((v.shape[0], SSD_HPG), F32)
    for k in range(SSD_HPG):
        s = jnp.sum(v[:, SSD_HEADDIM * k:SSD_HEADDIM * (k + 1)], axis=1, keepdims=True)
        out = jnp.where(lane == k, s, out)
    return out


def _head_rows(c4):
    n = SSD_HPG * SSD_HEADDIM
    row = lax.broadcasted_iota(jnp.int32, (n, 1), 0)
    out = jnp.broadcast_to(c4[:, 0:1], (n, 1))
    for k in range(1, SSD_HPG):
        out = jnp.where(row >= SSD_HEADDIM * k, c4[:, k:k + 1], out)
    return out


def _ssd_specs(T, Q, rev):
    NC = T // Q
    cc = (lambda c: NC - 1 - c) if rev else (lambda c: c)
    W, N, S = SSD_HPG * SSD_HEADDIM, SSD_STATE, SSD_STEP_GROUPS
    x_spec = pl.BlockSpec((Q, S * W), lambda g, c: (cc(c), g))
    b_spec = pl.BlockSpec((Q, S * N), lambda g, c: (cc(c), D_INNER // (S * N) + g))
    c_spec = pl.BlockSpec((Q, S * N), lambda g, c: (cc(c), (D_INNER + SSD_GROUPS * N) // (S * N) + g))
    dt_spec = pl.BlockSpec((S, Q, SSD_HPG), lambda g, c: (g, cc(c), 0))
    p_spec = pl.BlockSpec((S, 1, SSD_HPG), lambda g, c: (g, 0, 0))
    tri_spec = pl.BlockSpec((Q, Q), lambda g, c: (0, 0))
    h_spec = pl.BlockSpec((S, None, W, N), lambda g, c: (g, cc(c), 0, 0))
    return x_spec, b_spec, c_spec, dt_spec, p_spec, tri_spec, h_spec


def _ssd_views(gi, wide, narrow, lead):
    W, N = SSD_HPG * SSD_HEADDIM, SSD_STATE
    return ([r.at[:, pl.ds(W * gi, W)] for r in wide] + [r.at[:, pl.ds(N * gi, N)] for r in narrow]
            + [r.at[gi] for r in lead])


def _tri(Q):
    tri = jnp.tril(jnp.ones((Q, Q), F32))
    return tri, tri.T


def _ssd_fwd(xc, dtr8, bias8, alog8, dsk8):
    T, Q = xc.shape[0], SSD_Q
    NC = T // Q
    P, W = SSD_HEADDIM, SSD_HPG * SSD_HEADDIM
    x_spec, b_spec, c_spec, dt_spec, p_spec, tri_spec, h_spec = _ssd_specs(T, Q, False)

    def body(x_ref, b_ref, c_ref, dtr_ref, bias_ref, alog_ref, dsk_ref, tri_ref, triu_ref, y_ref, hs_ref, h_scr):
        for gi in range(SSD_STEP_GROUPS):
            xv, yv, bv, cv, dtv, biasv, alogv, dskv, hsv, hv = _ssd_views(
                gi, (x_ref, y_ref), (b_ref, c_ref), (dtr_ref, bias_ref, alog_ref, dsk_ref, hs_ref, h_scr))
            one(xv, bv, cv, dtv, biasv, alogv, dskv, tri_ref, triu_ref, yv, hsv, hv)

    def one(x_ref, b_ref, c_ref, dtr_ref, bias_ref, alog_ref, dsk_ref, tri_ref, triu_ref, y_ref, hs_ref, h_scr):
        @pl.when(pl.program_id(1) == 0)
        def _():
            h_scr[...] = jnp.zeros_like(h_scr)

        hin = h_scr[...]
        hs_ref[...] = hin
        x, Bm, Cm, pre, dt, A, ac4, ar4, cb, causal, atot4 = _ssd_common(
            x_ref, b_ref, c_ref, dtr_ref, bias_ref, alog_ref, tri_ref, triu_ref)
        xdt = x * _per_head(dt, W)
        rest = _dot(Cm, hin, NT) * _per_head(jnp.exp(ac4), W) + _per_head(dsk_ref[...], W) * x
        for k in range(SSD_HPG):
            blk = slice(P * k, P * (k + 1))
            G = cb * _ssd_decay(k, ac4, ar4, causal)
            y_ref[:, blk] = _dot(G, xdt[:, blk], NN) + rest[:, blk]
        xw = xdt * _per_head(jnp.exp(atot4 - ac4), W)
        h_scr[...] = _head_rows(jnp.exp(atot4)) * hin + _dot(xw, Bm, TN)

    tri, triu = _tri(Q)
    return pl.pallas_call(
        body, name="ssd_fwd", grid=(SSD_GROUPS // SSD_STEP_GROUPS, NC),
        in_specs=[x_spec, b_spec, c_spec, dt_spec, p_spec, p_spec, p_spec, tri_spec, tri_spec],
        out_specs=[pl.BlockSpec((Q, SSD_STEP_GROUPS * W), lambda g, c: (c, g)), h_spec],
        out_shape=[jax.ShapeDtypeStruct((T, D_INNER), F32),
                   jax.ShapeDtypeStruct((SSD_GROUPS, NC, W, SSD_STATE), F32)],
        scratch_shapes=[pltpu.VMEM((SSD_STEP_GROUPS, W, SSD_STATE), F32)],
        compiler_params=_params(("parallel", "arbitrary")),
    )(xc, xc, xc, dtr8, bias8, alog8, dsk8, tri, triu)


def _ssd_bwd(xc, dtr8, bias8, alog8, dsk8, hs, dy):
    T, Q = xc.shape[0], SSD_Q
    NC = T // Q
    P, W = SSD_HEADDIM, SSD_HPG * SSD_HEADDIM
    x_spec, b_spec, c_spec, dt_spec, p_spec, tri_spec, h_spec = _ssd_specs(T, Q, True)
    dy_spec = pl.BlockSpec((Q, SSD_STEP_GROUPS * W), lambda g, c: (NC - 1 - c, g))
    dbc_spec = pl.BlockSpec((Q, SSD_STEP_GROUPS * SSD_STATE), lambda g, c: (NC - 1 - c, g))

    def body(x_ref, b_ref, c_ref, dtr_ref, bias_ref, alog_ref, dsk_ref, tri_ref, triu_ref, hs_ref, dy_ref,
             dx_ref, db_ref, dc_ref, ddtr_ref, dbias_ref, dalog_ref, ddsk_ref, dh_scr):
        for gi in range(SSD_STEP_GROUPS):
            (xv, dyv, dxv, bv, cv, dbv, dcv, dtv, biasv, alogv, dskv, hsv, ddtv, dbiasv, dalogv, ddskv, dhv) = _ssd_views(
                gi, (x_ref, dy_ref, dx_ref), (b_ref, c_ref, db_ref, dc_ref),
                (dtr_ref, bias_ref, alog_ref, dsk_ref, hs_ref, ddtr_ref, dbias_ref, dalog_ref, ddsk_ref, dh_scr))
            one(xv, bv, cv, dtv, biasv, alogv, dskv, tri_ref, triu_ref, hsv, dyv,
                dxv, dbv, dcv, ddtv, dbiasv, dalogv, ddskv, dhv)

    def one(x_ref, b_ref, c_ref, dtr_ref, bias_ref, alog_ref, dsk_ref, tri_ref, triu_ref, hs_ref, dy_ref,
            dx_ref, db_ref, dc_ref, ddtr_ref, dbias_ref, dalog_ref, ddsk_ref, dh_scr):
        @pl.when(pl.program_id(1) == 0)
        def _():
            dh_scr[...] = jnp.zeros_like(dh_scr)
            dbias_ref[...] = jnp.zeros_like(dbias_ref)
            dalog_ref[...] = jnp.zeros_like(dalog_ref)
            ddsk_ref[...] = jnp.zeros_like(ddsk_ref)

        x, Bm, Cm, pre, dt, A, ac4, ar4, cb, causal, atot4 = _ssd_common(
            x_ref, b_ref, c_ref, dtr_ref, bias_ref, alog_ref, tri_ref, triu_ref)
        dyv, hin, dho = dy_ref[...], hs_ref[...], dh_scr[...]
        dt_w = _per_head(dt, W)
        xdt = x * dt_w
        E4, F4, etot4 = jnp.exp(ac4), jnp.exp(atot4 - ac4), jnp.exp(atot4)
        F_w = _per_head(F4, W)
        ddsk_ref[...] += _head_sums(_colsum(dyv * x))
        Z = _dot(Cm, hin, NT)
        dZ = dyv * _per_head(E4, W)
        dac4 = _head_sums(dyv * Z) * E4
        dC = _dot(dZ, hin, NN)
        dh_scr[...] = _dot(dZ, Cm, TN) + _head_rows(etot4) * dho
        per_row = jnp.sum(dho * hin, axis=1, keepdims=True)
        lane4 = lax.broadcasted_iota(jnp.int32, (1, SSD_HPG), 1)
        datot4 = jnp.zeros((1, SSD_HPG), F32)
        for k in range(SSD_HPG):
            datot4 = jnp.where(lane4 == k, jnp.sum(per_row[P * k:P * (k + 1)], keepdims=True), datot4)
        datot4 = datot4 * etot4
        dxw = _dot(Bm, dho, NT)
        dB = _dot(xdt * F_w, dho, NN)
        dFa4 = _head_sums(dxw * xdt) * F4
        datot4 = datot4 + _colsum(dFa4)
        dac4 = dac4 - dFa4
        dcb = jnp.zeros((Q, Q), F32)
        lane_q4 = lax.broadcasted_iota(jnp.int32, (Q, SSD_HPG), 1)
        sub_4q = lax.broadcasted_iota(jnp.int32, (SSD_HPG, Q), 0)
        row_sums = jnp.zeros((Q, SSD_HPG), F32)
        col_sums = jnp.zeros((SSD_HPG, Q), F32)
        dxdt_heads = []
        for k in range(SSD_HPG):
            blk = slice(P * k, P * (k + 1))
            L = _ssd_decay(k, ac4, ar4, causal)
            G = cb * L
            dG = _dot(dyv[:, blk], xdt[:, blk], NT)
            dxdt_heads.append(_dot(G, dyv[:, blk], TN))
            dcb = dcb + dG * L
            Mseg = dG * G
            row_sums = jnp.where(lane_q4 == k, jnp.sum(Mseg, axis=1, keepdims=True), row_sums)
            col_sums = jnp.where(sub_4q == k, jnp.sum(Mseg, axis=0, keepdims=True), col_sums)
        dxdt = dxw * F_w + jnp.concatenate(dxdt_heads, axis=1)
        last = lax.broadcasted_iota(jnp.int32, (Q, SSD_HPG), 0) == Q - 1
        dac4 = dac4 + row_sums + jnp.where(last, datot4, 0.0)
        dadt4 = _dot(triu_ref[...], dac4, NN, HI) - _dot(triu_ref[...], col_sums, NT, HI)
        ddt4 = _head_sums(dxdt * x) + dadt4 * A
        dalog_ref[...] += _colsum(dadt4 * dt) * A
        ddtr4 = ddt4 * jax.nn.sigmoid(pre)
        dbias_ref[...] += _colsum(ddtr4)
        ddtr_ref[...] = ddtr4
        dx_ref[...] = (_per_head(dsk_ref[...], W) * dyv + dxdt * dt_w).astype(dx_ref.dtype)
        db_ref[...] = dB + _dot(dcb, Cm, TN)
        dc_ref[...] = dC + _dot(dcb, Bm, NN)

    tri, triu = _tri(Q)
    return pl.pallas_call(
        body, name="ssd_bwd", grid=(SSD_GROUPS // SSD_STEP_GROUPS, NC),
        in_specs=[x_spec, b_spec, c_spec, dt_spec, p_spec, p_spec, p_spec, tri_spec, tri_spec, h_spec, dy_spec],
        out_specs=[dy_spec, dbc_spec, dbc_spec, dt_spec, p_spec, p_spec, p_spec],
        out_shape=[jax.ShapeDtypeStruct((T, D_INNER), F32),
                   jax.ShapeDtypeStruct((T, SSD_GROUPS * SSD_STATE), F32),
                   jax.ShapeDtypeStruct((T, SSD_GROUPS * SSD_STATE), F32),
                   jax.ShapeDtypeStruct(dtr8.shape, F32)] + [jax.ShapeDtypeStruct(bias8.shape, F32)] * 3,
        scratch_shapes=[pltpu.VMEM((SSD_STEP_GROUPS, W, SSD_STATE), F32)],
        compiler_params=_params(("parallel", "arbitrary")),
    )(xc, xc, xc, dtr8, bias8, alog8, dsk8, tri, triu, hs, dy)


def _to_groups(v):
    return v.reshape(SSD_GROUPS, 1, SSD_HPG)


def _mixer_fwd(x, p, w):
    T, D = x.shape
    (hn,) = _rowwise(lambda xt, g: (_rms(xt, g),), [x], [p["mix_norm"]], [(D, BF16)], name="mix_norm")
    winT = w["w_in"]
    u, z, xbc, dtr, gl = [
        _matmul(hn, winT[IN_OFFS[i]:IN_OFFS[i + 1]], "nt", name=f"mix_in{i}") for i in range(5)]
    tiles = _s5_derive(*[p[k][0] for k in ("s5_A_re", "s5_A_im", "s5_log_dt", "s5_B_re", "s5_B_im", "s5_C_re", "s5_C_im")])
    u = _to_seg(u)
    ys = _s5_fwd(u, tiles)
    (g,) = _rowwise(lambda yt, ut, d: (jax.nn.gelu(yt + d * ut),), [ys, u], [p["s5_D"]], [(D, F32)], name="s5_gelu")
    y5 = _matmul(g, w["s5_w_glu"], "nn", name="s5_glu", out=(BF16,),
                 epilogue=lambda acc, gt, b: (gt * jax.nn.sigmoid(acc + b),), extras=[g, p["s5_b_glu"]])
    xc = _conv_fwd(xbc, w["conv_w"], p["conv_b"])
    dtr8 = dtr.reshape(T, SSD_GROUPS, SSD_HPG).transpose(1, 0, 2)
    ssd_p = [_to_groups(p[k]) for k in ("ssd_dt_bias", "ssd_A_log", "ssd_D")]
    yssd_raw, hs = _ssd_fwd(xc, dtr8, *ssd_p)
    (yssd,) = _rowwise(lambda yt, zt, nw: (_rms(yt * jax.nn.silu(zt), nw),), [yssd_raw, z], [p["ssd_norm"]],
                       [(D_INNER, BF16)], name="ssd_gate")
    p5 = _from_seg(_matmul(y5, w["w_proj_s5"], "nn", name="mix_p5"))
    pssd = _matmul(yssd, w["w_proj_ssd"], "nn", name="mix_pssd")

    def merge(glt, at, bt, bg):
        gates = jax.nn.sigmoid(glt + bg)
        return (gates[:, :D] * at + gates[:, D:] * bt,)

    (merged,) = _rowwise(merge, [gl, p5, pssd], [p["b_gate"]], [(D, BF16)], name="mix_merge")
    y = _matmul(merged, w["w_out"], "nn", name="mix_out", epilogue=lambda acc, xt: (xt + acc,), extras=[x])
    res = dict(hn=hn, u=u, z=z, xbc=xbc, gl=gl, tiles=tiles, ys=ys, g=g, y5=y5, xc=xc, dtr8=dtr8, ssd_p=ssd_p,
               yssd_raw=yssd_raw, hs=hs, yssd=yssd, p5=p5, pssd=pssd, merged=merged)
    return y, res


def _mixer_bwd(dy, x, p, w, r):
    T, D = x.shape
    gp = {}
    dmerged = _matmul(dy, w["w_out"], "nt", name="mix_d_merged")
    gp["w_out"] = _matmul(r["merged"], dy, "tn", name="mix_d_wout")

    def merge_bwd(glt, at, bt, dm, bg):
        def f(q, a_, b_):
            gates = jax.nn.sigmoid(q + bg)
            return gates[:, :D] * a_ + gates[:, D:] * b_
        _, vjp = jax.vjp(f, glt, at, bt)
        dq, da_, db_ = vjp(dm)
        return da_, db_, dq, _colsum(dq)

    dp5, dpssd, dgl, gp["b_gate"] = _rowwise(
        merge_bwd, [r["gl"], r["p5"], r["pssd"], dmerged], [p["b_gate"]],
        [(D, BF16), (D, BF16), (2 * D, BF16)], [((1, 2 * D), F32)], name="mix_d_merge")
    dyssd = _matmul(dpssd, w["w_proj_ssd"], "nt", name="mix_d_yssd")
    gp["w_proj_ssd"] = _matmul(r["yssd"], dpssd, "tn", name="mix_d_wpssd")

    def gate_bwd(yt, zt, dyt, nw):
        _, vjp = jax.vjp(lambda a_, b_, c_: _rms(a_ * jax.nn.silu(b_), c_), yt, zt, nw)
        return vjp(dyt)

    dyraw, dz, gp["ssd_norm"] = _rowwise(
        gate_bwd, [r["yssd_raw"], r["z"], dyssd], [p["ssd_norm"]],
        [(D_INNER, F32), (D_INNER, BF16)], [((1, D_INNER), F32)], name="ssd_d_gate")
    dxs, dBm, dCm, ddtr8, dbias8, dalog8, ddsk8 = _ssd_bwd(r["xc"], r["dtr8"], *r["ssd_p"], r["hs"], dyraw)
    gp["ssd_dt_bias"], gp["ssd_A_log"], gp["ssd_D"] = [v.reshape(1, SSD_HEADS) for v in (dbias8, dalog8, ddsk8)]
    ddtr = ddtr8.transpose(1, 0, 2).reshape(T, SSD_HEADS)
    conv = [_conv_bwd(r["xbc"], w["conv_w"], p["conv_b"], d, c0, f"conv_bwd{i}")
            for i, (d, c0) in enumerate(((dxs, 0), (dBm, D_INNER), (dCm, D_INNER + SSD_GROUPS * SSD_STATE)))]
    dxbc = [c[0] for c in conv]
    gp["conv_w"] = jnp.concatenate([c[1] for c in conv], axis=1)
    gp["conv_b"] = jnp.concatenate([c[2] for c in conv], axis=1)

    (dp5,) = _rowwise(lambda t: (t,), [dp5], [], [(D, BF16)], name="mix_d_p5", after=(dxs,))
    dp5 = _to_seg(dp5)
    dy5 = _matmul(dp5, w["w_proj_s5"], "nt", name="mix_d_y5")
    gp["w_proj_s5"] = _matmul(r["y5"], dp5, "tn", name="mix_d_wp5")
    g = r["g"]

    def glu_ep(acc, gt, b, dyt):
        _, vjp = jax.vjp(lambda g_, t_: g_ * jax.nn.sigmoid(t_ + b), gt, acc)
        return vjp(dyt)

    dg1, dt_ = _matmul(g, w["s5_w_glu"], "nn", name="s5_d_glu", out=(F32, BF16), epilogue=glu_ep,
                       extras=[g, p["s5_b_glu"], dy5])
    gp["s5_w_glu"] = _matmul(g, dt_, "tn", name="s5_d_wglu")
    dg = _matmul(dt_, w["s5_w_glu"], "nt", name="s5_d_g", epilogue=lambda acc, e: (acc + e,), extras=[dg1])

    def gelu_bwd(yt, ut, dgt, dtt, d):
        _, vjp = jax.vjp(lambda y_, d_: jax.nn.gelu(y_ + d_ * ut), yt, d)
        dys, dd = vjp(dgt)
        return dys, dys * d, dd, _colsum(dtt.astype(F32))

    dys, dusk, gp["s5_D"], gp["s5_b_glu"] = _rowwise(
        gelu_bwd, [r["ys"], r["u"], dg, dt_], [p["s5_D"]], [(D, F32), (D, F32)], [((1, D), F32), ((1, D), F32)],
        name="s5_d_gelu")
    du, *dtiles = _s5_bwd(r["u"], dys, dusk, r["tiles"])
    du = _from_seg(du)

    pieces = [du, dz, None, ddtr, dgl]
    winT = w["w_in"]
    hn = r["hn"]
    d_rows, dhn = [], None
    cols = [(du, 0, 1024), (dz, 1024, 3072), (dxbc[0], 3072, 5120), (dxbc[1], 5120, 6144), (dxbc[2], 6144, 7168),
            (ddtr, 7168, 7200), (dgl, 7200, 9248)]
    for i, (d, lo, hi) in enumerate(cols):
        d_rows.append(_matmul(d, hn, "tn", name=f"mix_d_win{i}"))
        if dhn is None:
            dhn = _matmul(d, winT[lo:hi], "nn", name=f"mix_d_hn{i}")
        else:
            dhn = _matmul(d, winT[lo:hi], "nn", name=f"mix_d_hn{i}", epilogue=lambda acc, e: (acc + e,), extras=[dhn])
    gp["w_in"] = jnp.concatenate(d_rows, axis=0)
    del pieces

    def norm_bwd(xt, dh, dyt, gn):
        _, vjp = jax.vjp(_rms, xt, gn)
        dx, dgn = vjp(dh)
        return dyt + dx, dgn

    dx, gp["mix_norm"] = _rowwise(norm_bwd, [x, dhn, dy], [p["mix_norm"]], [(D, F32)], [((1, D), F32)], name="mix_d_norm")
    return dx, gp, dtiles


def _s5_param_grads(p, dtiles):
    keys = ("s5_A_re", "s5_A_im", "s5_log_dt", "s5_B_re", "s5_B_im", "s5_C_re", "s5_C_im")
    _, vjp = jax.vjp(_s5_derive, *[p[k][0] for k in keys])
    return {k: v[None] for k, v in zip(keys, vjp(tuple(dtiles)))}


def _local_step(x, target, p, w, late_weights, exchange):
    T, D = x.shape
    x1, r1 = _ffn_fwd(x, p["ffn1_norm"], w["ffn1_w_gate"], w["ffn1_w_up"], w["ffn1_w_down"], "ffn1")
    x2, rm = _mixer_fwd(x1, p, w)
    w = {**w, **late_weights(rm["p5"])}
    x3, r2 = _ffn_fwd(x2, p["ffn2_norm"], w["ffn2_w_gate"], w["ffn2_w_up"], w["ffn2_w_down"], "ffn2")

    def head(xt, tt, g):
        def f(x_, g_):
            e = _rms(x_, g_) - tt
            return 0.5 * jnp.sum(jnp.mean(e * e, axis=-1))
        l, (dx_, dg_) = jax.value_and_grad(f, argnums=(0, 1))(xt, g)
        return dx_, l.reshape(1, 1), dg_

    dx3, loss, d_final = _rowwise(head, [x3, target], [p["final_norm"]], [(D, F32)],
                                  [((1, 1), F32), ((1, D), F32)], name="loss_head")
    gp = {"final_norm": d_final}

    def ffn_exchange(tag):
        return lambda d_wgT, d_wuT, d_wd: exchange(
            tag, {f"{tag}_w_gate": d_wgT, f"{tag}_w_up": d_wuT, f"{tag}_w_down": d_wd})

    dx2, gp["ffn2_norm"], *_ = _ffn_bwd(
        dx3, x2, p["ffn2_norm"], w["ffn2_w_gate"], w["ffn2_w_up"], w["ffn2_w_down"], r2, "ffn2", ffn_exchange("ffn2"))
    dx1, gm, dtiles = _mixer_bwd(dx2, x1, p, w, rm)
    conv_w_grad = gm.pop("conv_w")
    started = exchange("mixer", {k: gm.pop(k) for k in ("w_out", "w_proj_s5", "w_proj_ssd", "s5_w_glu", "w_in")})
    gp.update(gm)
    gp.update(_s5_param_grads(p, dtiles))
    dx0, gp["ffn1_norm"], *_ = _ffn_bwd(
        dx1, x, p["ffn1_norm"], w["ffn1_w_gate"], w["ffn1_w_up"], w["ffn1_w_down"], r1, "ffn1", ffn_exchange("ffn1"))
    return loss, dx0, gp, conv_w_grad, dx1


def _adamw(w, m, v, gs, *, name, emit_g, tm=128):
    n_g = len(gs)
    C = w.shape[1]

    def fn(wt, mt, vt, *gt):
        g = gt[0]
        for e in gt[1:]:
            g = g + e
        m2 = ADAM_B1 * mt + (1.0 - ADAM_B1) * g
        v2 = ADAM_B2 * vt + (1.0 - ADAM_B2) * (g * g)
        m_hat = m2 / (1.0 - ADAM_B1 ** ADAM_STEP)
        v_hat = v2 / (1.0 - ADAM_B2 ** ADAM_STEP)
        delta = -ADAM_LR * (m_hat / (jnp.sqrt(v_hat) + ADAM_EPS) + ADAM_WD * wt)
        return ((g,) if emit_g else ()) + (delta, m2, v2)

    assert n_g >= 1
    return _rowwise(fn, [w, m, v, *gs], [], [(C, F32)] * (4 if emit_g else 3), name=name, tm=_rtile(w.shape[0], tm))


def _rtile(r, cap=512):
    if r <= cap:
        return r
    best = None
    for t in range(8, cap + 1, 8):
        if r % t == 0:
            best = t
    assert best is not None, r
    return best


ANY = pl.BlockSpec(memory_space=pl.ANY)


def _where_am_i():
    x, y, c = lax.axis_index("x"), lax.axis_index("y"), lax.axis_index("c")
    ks = (2 * x + y, 2 * (1 - x) + y, 2 * x + (1 - y), 2 * (1 - x) + (1 - y))
    return x, y, c, ks


def _rc(src, dst, ssem, rsem, to):
    return pltpu.make_async_remote_copy(src_ref=src, dst_ref=dst, send_sem=ssem, recv_sem=rsem,
                                        device_id=to, device_id_type=MESH)


def _gather_shards(srcs):
    n = len(srcs)

    def body(*refs):
        _gather_body(refs[:n], refs[n:2 * n], *refs[2 * n:], extra=())

    return pl.pallas_call(
        body, name="gather_shards",
        in_specs=[ANY] * n, out_specs=[ANY] * n,
        out_shape=[jax.ShapeDtypeStruct((4,) + s.shape, s.dtype) for s in srcs],
        scratch_shapes=[pltpu.SemaphoreType.DMA((n, 6)), pltpu.SemaphoreType.DMA((n, 6)), pltpu.SemaphoreType.DMA((n,))],
        compiler_params=pltpu.CompilerParams(has_side_effects=True),
    )(*srcs)


def _handshake_all():
    x, y, c = lax.axis_index("x"), lax.axis_index("y"), lax.axis_index("c")
    barrier = pltpu.get_barrier_semaphore()
    for dx, dy, dc in itertools.product((0, 1), repeat=3):
        if (dx, dy, dc) != (0, 0, 0):
            to = (1 - x if dx else x, 1 - y if dy else y, 1 - c if dc else c)
            pl.semaphore_signal(barrier, inc=1, device_id=to, device_id_type=MESH)
    pl.semaphore_wait(barrier, 7)


SEQUENCER = dict(axis_name="seq", num_cores=1)


def _gather_shards_async(srcs, token, name, collective_id):
    n = len(srcs)

    def body(*refs):
        tok, tok_out = refs[n], refs[2 * n + 1]
        ssem, rsem, lsem = refs[2 * n + 2:]
        _handshake_all()
        _gather_body(refs[:n], refs[n + 1:2 * n + 1], ssem, rsem, lsem,
                     extra=(pltpu.make_async_copy(tok, tok_out, lsem.at[n]),))

    res = pl.kernel(
        body, name=name,
        out_type=[jax.ShapeDtypeStruct((4,) + s.shape, s.dtype) for s in srcs]
        + [jax.ShapeDtypeStruct(token.shape, token.dtype)],
        mesh=plsc.ScalarSubcoreMesh(**SEQUENCER),
        scratch_types=[pltpu.SemaphoreType.DMA((n, 6)), pltpu.SemaphoreType.DMA((n, 6)), pltpu.SemaphoreType.DMA((n + 1,))],
        compiler_params=pltpu.CompilerParams(collective_id=collective_id),
    )(*srcs, token)
    return res[:n], res[n]


def _gather_body(src, out, ssem, rsem, lsem, extra):
        n = len(src)
        x, y, c, (k_me, k_x, k_y, k_d) = _where_am_i()
        sib = (x, y, 1 - c)
        local = [pltpu.make_async_copy(src[i], out[i].at[k_me], lsem.at[i]) for i in range(n)] + list(extra)
        for cp in local:
            cp.start()

        def own(i, q, to):
            return _rc(src[i], out[i].at[k_me], ssem.at[i, q], rsem.at[i, q], to)

        def slab(i, q, k, to):
            return _rc(out[i].at[k], out[i].at[k], ssem.at[i, q], rsem.at[i, q], to)

        @pl.when(c == 1)
        def _():
            sends = [own(i, 0, (1 - x, y, 1)) for i in range(n)]
            for cp in sends:
                cp.start()
            fwd = [slab(i, 3, k_x, sib) for i in range(n)]
            for i in range(n):
                slab(i, 0, k_x, sib).wait_recv()
                fwd[i].start()
            for i in range(n):
                slab(i, 4, k_y, sib).wait_recv()
                slab(i, 5, k_d, sib).wait_recv()
            for cp in sends + fwd:
                cp.wait_send()

        @pl.when(c == 0)
        def _():
            sends = [own(i, 1, (x, 1 - y, 0)) for i in range(n)] + [own(i, 2, (1 - x, 1 - y, 0)) for i in range(n)]
            for cp in sends:
                cp.start()
            fwd = [slab(i, 4, k_y, sib) for i in range(n)] + [slab(i, 5, k_d, sib) for i in range(n)]
            for i in range(n):
                slab(i, 1, k_y, sib).wait_recv()
                fwd[i].start()
            for i in range(n):
                slab(i, 2, k_d, sib).wait_recv()
                fwd[n + i].start()
            for i in range(n):
                slab(i, 3, k_x, sib).wait_recv()
            for cp in sends + fwd:
                cp.wait_send()

        for cp in local:
            cp.wait()


def _swap_slabs(arrs, n_slabs, name, async_id=None):
    n = len(arrs)
    J = max(n_slabs, 1)

    def body(*refs):
        src, out = refs[:n], refs[n:2 * n]
        ssem, rsem = refs[2 * n:]
        x, y, c, (k_me, k_x, k_y, k_d) = _where_am_i()
        sib = (x, y, 1 - c)
        sel = (jnp.where(c == 1, k_y, k_me), jnp.where(c == 1, k_d, k_x))
        cps = []
        for i in range(n):
            for j in range(J):
                s = src[i].at[sel[j]] if n_slabs else src[i]
                d = out[i].at[j] if n_slabs else out[i]
                cps.append(_rc(s, d, ssem.at[i, j], rsem.at[i, j], sib))
        for cp in cps:
            cp.start()
        for cp in cps:
            cp.wait()

    return _exchange_call(
        body, arrs, [jax.ShapeDtypeStruct(((n_slabs,) if n_slabs else ()) + a.shape[-2:], a.dtype) for a in arrs],
        [pltpu.SemaphoreType.DMA((n, J)), pltpu.SemaphoreType.DMA((n, J))], name, async_id)


def _exchange_call(body, arrs, out_shape, sems, name, async_id):
    n = len(arrs)
    if async_id is None:
        return pl.pallas_call(
            body, name=name, in_specs=[ANY] * n, out_specs=[ANY] * len(out_shape), out_shape=out_shape,
            scratch_shapes=sems, compiler_params=pltpu.CompilerParams(has_side_effects=True))(*arrs)

    def seq_body(*refs):
        _handshake_all()
        body(*refs)

    return pl.kernel(
        seq_body, name=name, out_type=out_shape, mesh=plsc.ScalarSubcoreMesh(**SEQUENCER), scratch_types=sems,
        compiler_params=pltpu.CompilerParams(collective_id=async_id))(*arrs)


def _send_chip_sums(arrs, name, async_id=None):
    n = len(arrs)

    def body(*refs):
        src, out = refs[:n], refs[n:2 * n]
        ssem, rsem = refs[2 * n:]
        x, y, c, _ = _where_am_i()

        @pl.when(c == 1)
        def _():
            cps = [_rc(src[i].at[1], out[i].at[0], ssem.at[i, 0], rsem.at[i, 0], (1 - x, y, 1)) for i in range(n)]
            for cp in cps:
                cp.start()
            for cp in cps:
                cp.wait()

        @pl.when(c == 0)
        def _():
            cps = [_rc(src[i].at[0], out[i].at[0], ssem.at[i, 0], rsem.at[i, 0], (x, 1 - y, 0)) for i in range(n)]
            cps += [_rc(src[i].at[1], out[i].at[1], ssem.at[i, 1], rsem.at[i, 1], (1 - x, 1 - y, 0)) for i in range(n)]
            for cp in cps:
                cp.start()
            for cp in cps:
                cp.wait()

    return _exchange_call(
        body, arrs, [jax.ShapeDtypeStruct(a.shape, a.dtype) for a in arrs],
        [pltpu.SemaphoreType.DMA((n, 2)), pltpu.SemaphoreType.DMA((n, 2))], name, async_id)


def _chip_sum(g, recv, sel, name):
    _, r, C = g.shape
    tr = _rtile(r)

    def body(sel_ref, g_ref, r_ref, o32_ref, o16_ref):
        s = g_ref[...] + r_ref[...]
        o32_ref[...] = s
        o16_ref[...] = s.astype(BF16)

    blk = pl.BlockSpec((None, tr, C), lambda j, t, sel_ref: (j, t, 0))
    return pl.pallas_call(
        body, name=name,
        grid_spec=pltpu.PrefetchScalarGridSpec(
            num_scalar_prefetch=1, grid=(2, r // tr),
            in_specs=[pl.BlockSpec((None, tr, C), lambda j, t, sel_ref: (sel_ref[j], t, 0)), blk],
            out_specs=[blk, blk]),
        out_shape=[jax.ShapeDtypeStruct((2, r, C), F32), jax.ShapeDtypeStruct((2, r, C), BF16)],
        compiler_params=_params(("arbitrary", "arbitrary")),
    )(sel, g, recv)


def _cross_sum(s32, got, north, after, name):
    _, r, C = s32.shape
    tr = _rtile(r)

    def body(f_ref, p_ref, a_ref, b_ref, after_ref, o_ref):
        a = a_ref[...].astype(F32)

        @pl.when(f_ref[0] == 1)
        def _():
            o_ref[...] = p_ref[...] + a

        @pl.when(f_ref[0] == 0)
        def _():
            o_ref[...] = a + b_ref[...].astype(F32)

    return pl.pallas_call(
        body, name=name,
        grid_spec=pltpu.PrefetchScalarGridSpec(
            num_scalar_prefetch=1, grid=(r // tr,),
            in_specs=[pl.BlockSpec((None, tr, C), lambda t, f: (0, t, 0)),
                      pl.BlockSpec((None, tr, C), lambda t, f: (0, t, 0)),
                      pl.BlockSpec((None, tr, C), lambda t, f: (1 - f[0], t, 0)), ANY],
            out_specs=pl.BlockSpec((tr, C), lambda t, f: (t, 0))),
        out_shape=jax.ShapeDtypeStruct((r, C), F32),
        compiler_params=_params(("arbitrary",)),
    )(north, s32, got, got, after)


def _reduce_scatter_begin(grads, tag, async_ids):
    x, y, c, (k_me, k_x, k_y, k_d) = _where_am_i()
    sel = jnp.stack([jnp.where(c == 1, k_me, k_y), jnp.where(c == 1, k_x, k_d)]).astype(jnp.int32)
    from_sib = _swap_slabs(grads, 2, f"swap_grad_slabs_{tag}", async_ids[0])
    sums = [_chip_sum(g, r, sel, f"chip_sum_{tag}{i}") for i, (g, r) in enumerate(zip(grads, from_sib))]
    got = _send_chip_sums([s16 for _, s16 in sums], f"send_chip_sums_{tag}", async_ids[1])
    return sums, got


def _reduce_scatter_end(sums, got, tag, after, async_id):
    north = jnp.reshape(lax.axis_index("c"), (1,)).astype(jnp.int32)
    parts = [_cross_sum(s32, b, north, after, f"cross_sum_{tag}{i}") for i, ((s32, _), b) in enumerate(zip(sums, got))]
    theirs = _swap_slabs(parts, 0, f"swap_parts_{tag}", async_id)
    return list(zip(parts, theirs))


def _all_reduce_small(v):
    R, C = v.shape

    def body(v_ref, o_ref, buf, ssem, rsem, lsem):
        x, y, c = lax.axis_index("x"), lax.axis_index("y"), lax.axis_index("c")
        me, sib = (x, y, c), (x, y, 1 - c)
        chips = [(1 - x, y), (x, 1 - y), (1 - x, 1 - y)]

        def slot(px, py, pc):
            return buf.at[4 * px + 2 * py + pc]

        def copy(k, block, to, src=None):
            return _rc(slot(*block) if src is None else src, slot(*block), ssem.at[k], rsem.at[k], to)

        mine = pltpu.make_async_copy(v_ref, slot(*me), lsem)
        mine.start()
        first = [copy(0, me, sib, src=v_ref)] + [copy(1 + j, me, (*chip, c), src=v_ref) for j, chip in enumerate(chips)]
        for cp in first:
            cp.start()
        passed = [copy(4 + j, (*chip, c), sib) for j, chip in enumerate(chips)]
        for j, chip in enumerate(chips):
            copy(1 + j, (*chip, c), me).wait_recv()
            passed[j].start()
        copy(0, sib, me).wait_recv()
        for j, chip in enumerate(chips):
            copy(4 + j, (*chip, 1 - c), me).wait_recv()
        for cp in first + passed:
            cp.wait_send()
        mine.wait()
        acc = buf[0]
        for k in range(1, 8):
            acc = acc + buf[k]
        o_ref[...] = acc

    return pl.pallas_call(
        body, name="all_reduce_small",
        in_specs=[pl.BlockSpec(memory_space=pltpu.VMEM)], out_specs=pl.BlockSpec(memory_space=pltpu.VMEM),
        out_shape=jax.ShapeDtypeStruct((R, C), F32),
        scratch_shapes=[pltpu.VMEM((8, R, C), F32), pltpu.SemaphoreType.DMA((7,)), pltpu.SemaphoreType.DMA((7,)),
                        pltpu.SemaphoreType.DMA],
        compiler_params=pltpu.CompilerParams(has_side_effects=True, vmem_limit_bytes=VMEM_LIMIT),
    )(v)


WEIGHTS = ['ffn1_norm', 'ffn1_w_gate', 'ffn1_w_up', 'ffn1_w_down', 'mix_norm', 'w_in', 'conv_w', 'conv_b', 's5_A_re',
           's5_A_im', 's5_log_dt', 's5_B_re', 's5_B_im', 's5_C_re', 's5_C_im', 's5_D', 's5_w_glu', 's5_b_glu',
           'ssd_A_log', 'ssd_dt_bias', 'ssd_D', 'ssd_norm', 'w_proj_s5', 'w_proj_ssd', 'b_gate', 'w_out', 'ffn2_norm',
           'ffn2_w_gate', 'ffn2_w_up', 'ffn2_w_down', 'final_norm']
ARGS = ['x'] + WEIGHTS + ['loss_target'] + ['m_' + n for n in WEIGHTS] + ['v_' + n for n in WEIGHTS]
COL_SHARDED = ('ffn1_w_gate', 'ffn1_w_up', 'w_in', 'ffn2_w_gate', 'ffn2_w_up')
ROW_SHARDED = ('ffn1_w_down', 's5_w_glu', 'w_proj_s5', 'w_proj_ssd', 'w_out', 'ffn2_w_down')
MATRICES = COL_SHARDED + ROW_SHARDED
GATHER_FIRST = ('ffn1_w_gate', 'ffn1_w_up', 'ffn1_w_down')
GATHER_SECOND = ('w_in', 'conv_w', 's5_w_glu', 'w_proj_s5', 'w_proj_ssd', 'w_out')
GATHER_THIRD = ('ffn2_w_gate', 'ffn2_w_up', 'ffn2_w_down')
SMALL = [n for n in WEIGHTS if n not in MATRICES]


def _pack(arrs, width=1024):
    flat = jnp.concatenate([a.reshape(-1).astype(F32) for a in arrs])
    rows = -(-flat.shape[0] // (8 * width)) * 8
    return jnp.pad(flat, (0, rows * width - flat.shape[0])).reshape(rows, width)


def _unpack(packed, shapes):
    flat, out, o = packed.reshape(-1), [], 0
    for s in shapes:
        n = math.prod(s)
        out.append(flat[o:o + n].reshape(s))
        o += n
    return out


def kernel(x, ffn1_norm, ffn1_w_gate, ffn1_w_up, ffn1_w_down, mix_norm, w_in, conv_w, conv_b, s5_A_re, s5_A_im, s5_log_dt, s5_B_re, s5_B_im, s5_C_re, s5_C_im, s5_D, s5_w_glu, s5_b_glu, ssd_A_log, ssd_dt_bias, ssd_D, ssd_norm, w_proj_s5, w_proj_ssd, b_gate, w_out, ffn2_norm, ffn2_w_gate, ffn2_w_up, ffn2_w_down, final_norm, loss_target, m_ffn1_norm, m_ffn1_w_gate, m_ffn1_w_up, m_ffn1_w_down, m_mix_norm, m_w_in, m_conv_w, m_conv_b, m_s5_A_re, m_s5_A_im, m_s5_log_dt, m_s5_B_re, m_s5_B_im, m_s5_C_re, m_s5_C_im, m_s5_D, m_s5_w_glu, m_s5_b_glu, m_ssd_A_log, m_ssd_dt_bias, m_ssd_D, m_ssd_norm, m_w_proj_s5, m_w_proj_ssd, m_b_gate, m_w_out, m_ffn2_norm, m_ffn2_w_gate, m_ffn2_w_up, m_ffn2_w_down, m_final_norm, v_ffn1_norm, v_ffn1_w_gate, v_ffn1_w_up, v_ffn1_w_down, v_mix_norm, v_w_in, v_conv_w, v_conv_b, v_s5_A_re, v_s5_A_im, v_s5_log_dt, v_s5_B_re, v_s5_B_im, v_s5_C_re, v_s5_C_im, v_s5_D, v_s5_w_glu, v_s5_b_glu, v_ssd_A_log, v_ssd_dt_bias, v_ssd_D, v_ssd_norm, v_w_proj_s5, v_w_proj_ssd, v_b_gate, v_w_out, v_ffn2_norm, v_ffn2_w_gate, v_ffn2_w_up, v_ffn2_w_down, v_final_norm):
    a = dict(locals())
    assert list(a) == ARGS
    x, target = a['x'][0], a['loss_target'][0]
    k_me = 2 * lax.axis_index("x") + lax.axis_index("y")

    src = {n: a[n][0].T.astype(BF16) for n in COL_SHARDED}
    src.update({n: a[n][0].astype(BF16) for n in ROW_SHARDED})
    src['conv_w'] = a['conv_w'][0]
    def flat(f):
        return f.reshape(-1, f.shape[-1])

    full = _gather_shards([src[n] for n in GATHER_FIRST])
    w = {n: flat(f) for n, f in zip(GATHER_FIRST, full)}
    got, _ = _gather_shards_async([src[n] for n in GATHER_SECOND], full[0][0, :8, :128].astype(F32), "gather_mixer", 1)
    w.update({n: flat(f) for n, f in zip(GATHER_SECOND, got)})
    w['conv_w'] = w['conv_w'].reshape(4, CONV_K, -1).transpose(1, 0, 2).reshape(CONV_K, CONV_DIM)

    def gather_last(after):
        got, _ = _gather_shards_async([src[n] for n in GATHER_THIRD], after[:8, :128], "gather_ffn2", 2)
        return {n: flat(f) for n, f in zip(GATHER_THIRD, got)}

    p = {n: a[n] for n in SMALL if n != 'conv_w'}
    p['final_norm'] = a['final_norm'][None]

    ids = {"ffn2": (3, 4, 5), "mixer": (6, 7, 8), "ffn1": (9, 10, 11)}
    begun = {}

    def exchange(tag, grads):
        names = list(grads)
        sums, got = _reduce_scatter_begin([grads[n].reshape(4, -1, D_MODEL) for n in names], tag, ids[tag][:2])
        begun[tag] = (names, sums, got)
        return tuple(s16 for _, s16 in sums)

    loss, grad_x, gp, conv_w_grad, dx_mixer = _local_step(x, target, p, w, gather_last, exchange)
    pairs = {}
    for tag, after in (("ffn2", dx_mixer), ("mixer", grad_x), ("ffn1", grad_x)):
        names, sums, got = begun[tag]
        pairs.update(zip(names, _reduce_scatter_end(sums, got, tag, after, ids[tag][2])))
    pairs = [pairs[n] for n in MATRICES]
    gp['conv_w'] = conv_w_grad[None]
    small_shapes = [(1, 1)] + [gp[n].shape if n != 'final_norm' else (1, D_MODEL) for n in SMALL]
    red = _unpack(_all_reduce_small(_pack([loss] + [gp[n] for n in SMALL])), small_shapes)
    loss_all = red[0].reshape(())
    gsmall = dict(zip(SMALL, red[1:]))
    gsmall['conv_w'] = lax.dynamic_slice_in_dim(gsmall['conv_w'], k_me * 1024, 1024, axis=2)
    gsmall = {n: g.reshape(a[n].shape) for n, g in gsmall.items()}

    grads, delta, new_m, new_v = {}, {}, {}, {}
    for n, (mine, theirs) in zip(MATRICES, pairs):
        wn, mn, vn = a[n][0], a['m_' + n][0], a['v_' + n][0]
        if n in COL_SHARDED:
            (gt,) = _rowwise(lambda p_, q_: (p_ + q_,), [mine, theirs], [], [(D_MODEL, F32)], name=f"sum_{n}",
                             tm=_rtile(mine.shape[0]))
            g = gt.T
            d, m2, v2 = _adamw(wn, mn, vn, [g], name=f"adamw_{n}", emit_g=False)
        else:
            g, d, m2, v2 = _adamw(wn, mn, vn, [mine, theirs], name=f"adamw_{n}", emit_g=True)
        grads[n], delta[n], new_m[n], new_v[n] = g[None], d[None], m2[None], v2[None]
    sw, sm, sv, sg = [_pack([t[n] for n in SMALL]) for t in (a, {n: a['m_' + n] for n in SMALL},
                                                             {n: a['v_' + n] for n in SMALL}, gsmall)]
    d, m2, v2 = _adamw(sw, sm, sv, [sg], name="adamw_small", emit_g=False)
    shapes = [a[n].shape for n in SMALL]
    for n, dd, mm, vv in zip(SMALL, _unpack(d, shapes), _unpack(m2, shapes), _unpack(v2, shapes)):
        grads[n], delta[n], new_m[n], new_v[n] = gsmall[n], dd, mm, vv
    return (loss_all, grad_x[None], *[grads[n] for n in WEIGHTS], *[delta[n] for n in WEIGHTS],
            *[new_m[n] for n in WEIGHTS], *[new_v[n] for n in WEIGHTS])
```

```python
import itertools
import math

import jax
import jax.numpy as jnp
from jax import lax
from jax.experimental import pallas as pl
from jax.experimental.pallas import tpu as pltpu
from jax.experimental.pallas import tpu_sc as plsc

F32 = jnp.float32
BF16 = jnp.bfloat16
MXU = BF16
HI = lax.Precision.HIGHEST

D_MODEL = 1024
D_FF = 2816
EPS = 1e-6
S5_GROUPS, S5_GROUP, S5_STATE = 64, 16, 64
D_INNER = 2048
SSD_HEADDIM, SSD_HEADS, SSD_GROUPS, SSD_HPG, SSD_STATE = 64, 32, 8, 4, 128
CONV_K, CONV_DIM = 4, 4096
IN_SPLITS = (1024, 2048, 4096, 32, 2048)
IN_OFFS = (0, 1024, 3072, 7168, 7200, 9248)
SSD_Q = 256
SSD_STEP_GROUPS = 2
ADAM_LR, ADAM_B1, ADAM_B2, ADAM_EPS, ADAM_WD, ADAM_STEP = 0.001, 0.9, 0.999, 1e-08, 0.01, 10

VMEM_LIMIT = 56 * 1024 * 1024
MATMUL_VMEM = 32 * 1024 * 1024
MESH = pl.DeviceIdType.MESH

NN = ((1,), (0,))
NT = ((1,), (1,))
TN = ((0,), (0,))


def _dot(a, b, dims, precision=None):
    if precision is None:
        a, b = a.astype(MXU), b.astype(MXU)
    return lax.dot_general(a, b, (dims, ((), ())), precision=precision, preferred_element_type=F32)


def _tile(n, pref):
    if n <= pref:
        return n
    best = None
    for t in range(128, pref + 1, 128):
        if n % t == 0:
            best = t
    assert best is not None, (n, pref)
    return best


def _params(sem):
    return pltpu.CompilerParams(dimension_semantics=sem, vmem_limit_bytes=VMEM_LIMIT)


def _matmul(a, b, mode, *, name, out=((F32),), epilogue=None, extras=(), after=(), tm=512, tn=1536):
    if mode == "nn":
        (M, K), (_, N) = a.shape, b.shape
    elif mode == "nt":
        (M, K), (N, _) = a.shape, b.shape
    else:
        (K, M), (_, N) = a.shape, b.shape
    tm, tn = _tile(M, tm), _tile(N, tn)

    def vmem(tm_, tn_):
        per = K * tm_ * a.dtype.itemsize + K * tn_ * b.dtype.itemsize
        per += sum(tm_ * tn_ * jnp.dtype(dt).itemsize for dt in out)
        per += sum((1 if e.shape[0] == 1 else tm_) * tn_ * e.dtype.itemsize for e in extras)
        return 2 * per

    while vmem(tm, tn) > MATMUL_VMEM and (tn > 128 or tm > 128):
        if tn >= tm and tn > 128:
            tn = _tile(N, tn - 128)
        else:
            tm = _tile(M, tm - 128)
    a_spec = pl.BlockSpec((K, tm), lambda i, j: (0, i)) if mode == "tn" else pl.BlockSpec((tm, K), lambda i, j: (i, 0))
    b_spec = pl.BlockSpec((tn, K), lambda i, j: (j, 0)) if mode == "nt" else pl.BlockSpec((K, tn), lambda i, j: (0, j))
    dims = {"nn": NN, "nt": NT, "tn": TN}[mode]
    e_specs = []
    for e in extras:
        if e.shape[0] == 1:
            e_specs.append(pl.BlockSpec((1, tn), lambda i, j: (0, j)))
        else:
            assert e.shape == (M, N), (e.shape, M, N)
            e_specs.append(pl.BlockSpec((tm, tn), lambda i, j: (i, j)))
    n_e, n_o, n_a = len(extras), len(out), len(after)

    def body(a_ref, b_ref, *refs):
        acc = _dot(a_ref[...], b_ref[...], dims)
        res = (acc,) if epilogue is None else epilogue(acc, *[r[...] for r in refs[:n_e]])
        for r, v in zip(refs[n_e + n_a:], res):
            r[...] = v.astype(r.dtype)

    res = pl.pallas_call(
        body, name=name, grid=(M // tm, N // tn),
        in_specs=[a_spec, b_spec] + e_specs + [ANY] * n_a,
        out_specs=[pl.BlockSpec((tm, tn), lambda i, j: (i, j)) for _ in range(n_o)],
        out_shape=[jax.ShapeDtypeStruct((M, N), dt) for dt in out],
        compiler_params=_params(("parallel", "parallel")),
    )(a, b, *extras, *after)
    return res[0] if n_o == 1 else res


def _rowwise(fn, rows, fulls, outs, reds=(), *, name, after=(), tm=256):
    T = rows[0].shape[0]
    tm = min(tm, T)
    n_r, n_f, n_o, n_d, n_a = len(rows), len(fulls), len(outs), len(reds), len(after)

    def body(*refs):
        ins = [r[...] for r in refs[:n_r + n_f]]
        o_refs = refs[n_r + n_f + n_a:n_r + n_f + n_a + n_o]
        d_refs = refs[n_r + n_f + n_a + n_o:]
        res = fn(*ins)
        for r, v in zip(o_refs, res[:n_o]):
            r[...] = v.astype(r.dtype)
        if n_d:
            @pl.when(pl.program_id(0) == 0)
            def _():
                for r in d_refs:
                    r[...] = jnp.zeros_like(r)
            for r, v in zip(d_refs, res[n_o:]):
                r[...] += v.astype(r.dtype)

    res = pl.pallas_call(
        body, name=name, grid=(T // tm,),
        in_specs=[pl.BlockSpec((tm, r.shape[1]), lambda i: (i, 0)) for r in rows]
        + [pl.BlockSpec(f.shape, lambda i, nd=f.ndim: (0,) * nd) for f in fulls] + [ANY] * n_a,
        out_specs=[pl.BlockSpec((tm, c), lambda i: (i, 0)) for c, _ in outs]
        + [pl.BlockSpec(s, lambda i, nd=len(s): (0,) * nd) for s, _ in reds],
        out_shape=[jax.ShapeDtypeStruct((T, c), dt) for c, dt in outs]
        + [jax.ShapeDtypeStruct(s, dt) for s, dt in reds],
        compiler_params=_params(("arbitrary",)),
    )(*rows, *fulls, *after)
    return res


def _rms(x, g):
    return x * lax.rsqrt(jnp.mean(x * x, axis=-1, keepdims=True) + EPS) * g


def _colsum(v):
    return jnp.sum(v, axis=0, keepdims=True)


def _softplus(x):
    return jnp.maximum(x, 0.0) + jnp.log1p(jnp.exp(-jnp.abs(x)))


def _ffn_fwd(x, norm, wgT, wuT, wd, tag):
    D = x.shape[1]
    (hn,) = _rowwise(lambda xt, g: (_rms(xt, g),), [x], [norm], [(D, BF16)], name=f"{tag}_norm")
    a = _matmul(hn, wgT, "nt", name=f"{tag}_gate")
    b, hid = _matmul(hn, wuT, "nt", name=f"{tag}_up", out=(BF16, BF16),
                     epilogue=lambda acc, at: (acc, jax.nn.silu(at) * acc), extras=[a])
    y = _matmul(hid, wd, "nn", name=f"{tag}_down", epilogue=lambda acc, xt: (xt + 0.5 * acc,), extras=[x])
    return y, (hn, a, b, hid)


def _ffn_bwd(dy, x, norm, wgT, wuT, wd, res, tag, after, on_grads):
    hn, a, b, hid = res
    D, F = x.shape[1], a.shape[1]

    def act_bwd(acc, at, bt):
        _, vjp = jax.vjp(lambda p, q: jax.nn.silu(p) * q, at, bt.astype(F32))
        return vjp(0.5 * acc)

    da, db = _matmul(dy, wd, "nt", name=f"{tag}_d_hid", out=(BF16, BF16), epilogue=act_bwd, extras=[a, b])
    d_wd = _matmul(hid, dy, "tn", name=f"{tag}_d_wd", epilogue=lambda acc: (0.5 * acc,))
    d_wgT = _matmul(da, hn, "tn", name=f"{tag}_d_wg", after=after)
    d_wuT = _matmul(db, hn, "tn", name=f"{tag}_d_wu")
    started = on_grads(d_wgT, d_wuT, d_wd, dy)
    dhn = _matmul(da, wgT, "nn", name=f"{tag}_d_hn1")
    dhn = _matmul(db, wuT, "nn", name=f"{tag}_d_hn2", epilogue=lambda acc, e: (acc + e,), extras=[dhn])

    def norm_bwd(xt, dh, dyt, g):
        _, vjp = jax.vjp(_rms, xt, g)
        dx, dg = vjp(dh)
        return dyt + dx, dg

    dx, d_norm = _rowwise(norm_bwd, [x, dhn, dy], [norm], [(D, F32)], [((1, D), F32)], name=f"{tag}_d_norm",
                          after=started)
    return dx, d_norm


S5_TILES = 8
S5_HALF = 256


def _s5_derive(A_re, A_im, log_dt, B_re, B_im, C_re, C_im):
    G, N, M = S5_GROUPS, S5_STATE, S5_GROUP
    dt = jnp.exp(log_dt)[:, None]
    mag = jnp.exp(A_re * dt)
    ar, ai = mag * jnp.cos(A_im * dt), mag * jnp.sin(A_im * dt)
    den = A_re * A_re + A_im * A_im
    cr = ((ar - 1.0) * A_re + ai * A_im) / den
    ci = (ai * A_re - (ar - 1.0) * A_im) / den
    bbr = cr[..., None] * B_re - ci[..., None] * B_im
    bbi = cr[..., None] * B_im + ci[..., None] * B_re
    eye = jnp.eye(8, dtype=F32)

    def tile_in(bb):
        t = bb.reshape(S5_TILES, 8, N, M).transpose(0, 1, 3, 2)
        return jnp.einsum("jamn,ab->jambn", t, eye).reshape(S5_TILES, 8 * M, 8 * N)

    def tile_out(c):
        t = c.reshape(S5_TILES, 8, M, N).transpose(0, 1, 3, 2)
        return jnp.einsum("janm,ab->janbm", t, eye).reshape(S5_TILES, 8 * N, 8 * M)

    return (tile_in(bbr), tile_in(bbi), tile_out(C_re), tile_out(C_im),
            ar.reshape(S5_TILES, 1, 8 * N), ai.reshape(S5_TILES, 1, 8 * N))


S5_NB = S5_HALF // 128
S5_SEG = 32


def _cmul(ar, ai, br, bi):
    return ar * br - ai * bi, ar * bi + ai * br


def _s5_scan(sr_ref, si_ref, ar, ai, T, reverse):
    L, V = T // S5_SEG, S5_SEG // 8
    assert L * S5_SEG == T and L & (L - 1) == 0, T
    sg = -1.0 if reverse else 1.0
    a_r = [jnp.broadcast_to(ar[:, 128 * b:128 * (b + 1)], (8, 128)) for b in range(S5_NB)]
    a_i = [jnp.broadcast_to(sg * ai[:, 128 * b:128 * (b + 1)], (8, 128)) for b in range(S5_NB)]

    def rows(k, v):
        return pl.ds(pl.multiple_of(((L - 1 - k) if reverse else k) * S5_SEG + 8 * v, 8), 8)

    def local(k, carry):
        out = []
        for b in range(S5_NB):
            for v in range(V):
                idx = rows(k, v)
                mr, mi = _cmul(a_r[b], a_i[b], *carry[b * V + v])
                nr, ni = mr + sr_ref[b, idx, :], mi + si_ref[b, idx, :]
                sr_ref[b, idx, :] = nr
                si_ref[b, idx, :] = ni
                out.append((nr, ni))
        return tuple(out)

    z = jnp.zeros((8, 128), F32)
    ends = lax.fori_loop(0, L, local, tuple((z, z) for _ in range(S5_NB * V)))

    carries = []
    for b in range(S5_NB):
        pr, pi = a_r[b][0:1], a_i[b][0:1]
        n = L
        while n > 1:
            pr, pi = _cmul(pr, pi, pr, pi)
            n //= 2
        cr, ci = jnp.zeros((1, 128), F32), jnp.zeros((1, 128), F32)
        into = [None] * S5_SEG
        for j in (reversed(range(S5_SEG)) if reverse else range(S5_SEG)):
            into[j] = (cr, ci)
            er, ei = ends[b * V + j // 8]
            mr, mi = _cmul(pr, pi, cr, ci)
            cr, ci = mr + er[j % 8:j % 8 + 1], mi + ei[j % 8:j % 8 + 1]
        carries.append([(jnp.concatenate([into[8 * v + s][0] for s in range(8)], axis=0),
                         jnp.concatenate([into[8 * v + s][1] for s in range(8)], axis=0)) for v in range(V)])

    def fix(k, powers):
        out = []
        for b in range(S5_NB):
            pr, pi = powers[b]
            for v in range(V):
                idx = rows(k, v)
                dr, di = _cmul(pr, pi, *carries[b][v])
                sr_ref[b, idx, :] += dr
                si_ref[b, idx, :] += di
            out.append(_cmul(pr, pi, a_r[b], a_i[b]))
        return tuple(out)

    lax.fori_loop(0, L, fix, tuple((a_r[b], a_i[b]) for b in range(S5_NB)))


def _s5_in(ut, b_ref, s_ref):
    for b in range(S5_NB):
        s_ref[b] = _dot(ut, b_ref[:, 128 * b:128 * (b + 1)], NN)


def _to_seg(v):
    T, C = v.shape
    return v.reshape(S5_SEG, T // S5_SEG, C).transpose(1, 0, 2).reshape(T, C)


def _from_seg(v):
    T, C = v.shape
    return v.reshape(T // S5_SEG, S5_SEG, C).transpose(1, 0, 2).reshape(T, C)


def _s5_specs(T):
    u_spec = pl.BlockSpec((T, 128), lambda j, h: (0, j))
    b_spec = pl.BlockSpec((None, 128, S5_HALF), lambda j, h: (j, 0, h))
    c_spec = pl.BlockSpec((None, S5_HALF, 128), lambda j, h: (j, h, 0))
    a_spec = pl.BlockSpec((None, 1, S5_HALF), lambda j, h: (j, 0, h))
    return u_spec, b_spec, c_spec, a_spec


def _s5_fwd(u, tiles):
    T = u.shape[0]
    u_spec, b_spec, c_spec, a_spec = _s5_specs(T)

    def body(u_ref, br_ref, bi_ref, cr_ref, ci_ref, ar_ref, ai_ref, y_ref, sr_ref, si_ref):
        ut = u_ref[...].astype(MXU)
        _s5_in(ut, br_ref, sr_ref)
        _s5_in(ut, bi_ref, si_ref)
        _s5_scan(sr_ref, si_ref, ar_ref[...], ai_ref[...], T, False)
        y = None
        for b in range(S5_NB):
            blk = slice(128 * b, 128 * (b + 1))
            yb = _dot(sr_ref[b], cr_ref[blk, :], NN) - _dot(si_ref[b], ci_ref[blk, :], NN)
            y = yb if y is None else y + yb

        @pl.when(pl.program_id(1) == 0)
        def _():
            y_ref[...] = y

        @pl.when(pl.program_id(1) == 1)
        def _():
            y_ref[...] += y

    scr = pltpu.VMEM((S5_NB, T, 128), F32)
    return pl.pallas_call(
        body, name="s5_fwd", grid=(S5_TILES, 2),
        in_specs=[u_spec, b_spec, b_spec, c_spec, c_spec, a_spec, a_spec],
        out_specs=u_spec, out_shape=jax.ShapeDtypeStruct(u.shape, F32),
        scratch_shapes=[scr, scr],
        compiler_params=_params(("parallel", "arbitrary")),
    )(u, *tiles)


def _s5_bwd(u, dys, du_skip, tiles):
    T = u.shape[0]
    u_spec, b_spec, c_spec, a_spec = _s5_specs(T)

    def body(u_ref, dy_ref, sk_ref, br_ref, bi_ref, cr_ref, ci_ref, ar_ref, ai_ref,
             du_ref, dbr_ref, dbi_ref, dcr_ref, dci_ref, dar_ref, dai_ref, sr_ref, si_ref, lr_ref, li_ref):
        ut, dy = u_ref[...].astype(MXU), dy_ref[...].astype(MXU)
        ar, ai = ar_ref[...], ai_ref[...]
        _s5_in(ut, br_ref, sr_ref)
        _s5_in(ut, bi_ref, si_ref)
        _s5_scan(sr_ref, si_ref, ar, ai, T, False)
        for b in range(S5_NB):
            blk = slice(128 * b, 128 * (b + 1))
            dcr_ref[blk, :] = _dot(sr_ref[b], dy, TN)
            dci_ref[blk, :] = -_dot(si_ref[b], dy, TN)
            lr_ref[b] = _dot(dy, cr_ref[blk, :], NT)
            li_ref[b] = -_dot(dy, ci_ref[blk, :], NT)
        _s5_scan(lr_ref, li_ref, ar, ai, T, True)
        du = None
        head = lax.broadcasted_iota(jnp.int32, (T, 128), 0) < S5_SEG
        seg0 = lax.broadcasted_iota(jnp.int32, (S5_SEG, 128), 0) == 0
        for b in range(S5_NB):
            blk = slice(128 * b, 128 * (b + 1))
            lr, li = lr_ref[b], li_ref[b]
            pr = jnp.where(head, 0.0, pltpu.roll(sr_ref[b], S5_SEG, 0))
            pi = jnp.where(head, 0.0, pltpu.roll(si_ref[b], S5_SEG, 0))
            er = jnp.where(seg0, 0.0, pltpu.roll(sr_ref[b, T - S5_SEG:T, :], 1, 0))
            ei = jnp.where(seg0, 0.0, pltpu.roll(si_ref[b, T - S5_SEG:T, :], 1, 0))
            hr, hi = lr[0:S5_SEG], li[0:S5_SEG]
            dar_ref[:, blk] = _colsum(lr * pr + li * pi) + _colsum(hr * er + hi * ei)
            dai_ref[:, blk] = _colsum(li * pr - lr * pi) + _colsum(hi * er - hr * ei)
            dbr_ref[:, blk] = _dot(ut, lr, TN)
            dbi_ref[:, blk] = _dot(ut, li, TN)
            dub = _dot(lr, br_ref[:, blk], NT) + _dot(li, bi_ref[:, blk], NT)
            du = dub if du is None else du + dub

        @pl.when(pl.program_id(1) == 0)
        def _():
            du_ref[...] = du + sk_ref[...]

        @pl.when(pl.program_id(1) == 1)
        def _():
            du_ref[...] += du

    scr = pltpu.VMEM((S5_NB, T, 128), F32)
    return pl.pallas_call(
        body, name="s5_bwd", grid=(S5_TILES, 2),
        in_specs=[u_spec, u_spec, u_spec, b_spec, b_spec, c_spec, c_spec, a_spec, a_spec],
        out_specs=[u_spec, b_spec, b_spec, c_spec, c_spec, a_spec, a_spec],
        out_shape=[jax.ShapeDtypeStruct(u.shape, F32)] + [jax.ShapeDtypeStruct(t.shape, F32) for t in tiles],
        scratch_shapes=[scr, scr, scr, scr],
        compiler_params=_params(("parallel", "arbitrary")),
    )(u, dys, du_skip, *tiles)


CONV_TILE = 256


def _conv_pre(x, w, b):
    T = x.shape[0]
    row = lax.broadcasted_iota(jnp.int32, x.shape, 0)
    acc = x * w[CONV_K - 1:CONV_K, :] + b
    for lag in range(1, CONV_K):
        acc = acc + jnp.where(row >= lag, pltpu.roll(x, lag, 0), 0.0) * w[CONV_K - 1 - lag:CONV_K - lag, :]
    return acc


def _conv_fwd(x, w, b):
    T, C = x.shape
    col = pl.BlockSpec((T, CONV_TILE), lambda j: (0, j))

    def body(x_ref, w_ref, b_ref, o_ref):
        o_ref[...] = jax.nn.silu(_conv_pre(x_ref[...], w_ref[...], b_ref[...]))

    return pl.pallas_call(
        body, name="conv_fwd", grid=(C // CONV_TILE,),
        in_specs=[col, pl.BlockSpec((CONV_K, CONV_TILE), lambda j: (0, j)), pl.BlockSpec((1, CONV_TILE), lambda j: (0, j))],
        out_specs=col, out_shape=jax.ShapeDtypeStruct((T, C), F32),
        compiler_params=_params(("parallel",)),
    )(x, w, b)


def _conv_bwd(x, w, b, dout, col0, name):
    T, C = dout.shape
    off = col0 // CONV_TILE
    xcol = pl.BlockSpec((T, CONV_TILE), lambda j: (0, j + off))
    dcol = pl.BlockSpec((T, CONV_TILE), lambda j: (0, j))

    def body(x_ref, w_ref, b_ref, d_ref, dx_ref, dw_ref, db_ref):
        x, w = x_ref[...], w_ref[...]
        _, vjp = jax.vjp(jax.nn.silu, _conv_pre(x, w, b_ref[...]))
        (dy,) = vjp(d_ref[...])
        row = lax.broadcasted_iota(jnp.int32, x.shape, 0)
        dx = dy * w[CONV_K - 1:CONV_K, :]
        dw_ref[CONV_K - 1:CONV_K, :] = _colsum(dy * x)
        for lag in range(1, CONV_K):
            dx = dx + jnp.where(row < T - lag, pltpu.roll(dy, T - lag, 0), 0.0) * w[CONV_K - 1 - lag:CONV_K - lag, :]
            xs = jnp.where(row >= lag, pltpu.roll(x, lag, 0), 0.0)
            dw_ref[CONV_K - 1 - lag:CONV_K - lag, :] = _colsum(dy * xs)
        dx_ref[...] = dx.astype(dx_ref.dtype)
        db_ref[...] = _colsum(dy)

    return pl.pallas_call(
        body, name=name, grid=(C // CONV_TILE,),
        in_specs=[xcol, pl.BlockSpec((CONV_K, CONV_TILE), lambda j: (0, j + off)),
                  pl.BlockSpec((1, CONV_TILE), lambda j: (0, j + off)), dcol],
        out_specs=[dcol, pl.BlockSpec((CONV_K, CONV_TILE), lambda j: (0, j)), pl.BlockSpec((1, CONV_TILE), lambda j: (0, j))],
        out_shape=[jax.ShapeDtypeStruct((T, C), BF16), jax.ShapeDtypeStruct((CONV_K, C), F32), jax.ShapeDtypeStruct((1, C), F32)],
        compiler_params=_params(("parallel",)),
    )(x, w, b, dout)


def _ssd_common(x_ref, b_ref, c_ref, dtr_ref, bias_ref, alog_ref, tri_ref, triu_ref):
    Q = x_ref.shape[0]
    x, Bm, Cm = x_ref[...], b_ref[...], c_ref[...]
    pre = dtr_ref[...] + bias_ref[...]
    dt = _softplus(pre)
    A = -jnp.exp(alog_ref[...])
    adt = dt * A
    ac4 = _dot(tri_ref[...], adt, NN, HI)
    ar4 = _dot(adt, triu_ref[...], TN, HI)
    cb = _dot(Cm, Bm, NT)
    ii = lax.broadcasted_iota(jnp.int32, (Q, Q), 0)
    jj = lax.broadcasted_iota(jnp.int32, (Q, Q), 1)
    atot4 = ac4[Q - 1:Q, :]
    return x, Bm, Cm, pre, dt, A, ac4, ar4, cb, ii >= jj, atot4


def _ssd_decay(k, ac4, ar4, causal):
    seg = ac4[:, k:k + 1] - ar4[k:k + 1, :]
    return jnp.where(causal, jnp.exp(jnp.where(causal, seg, 0.0)), 0.0)


def _per_head(c4, n):
    lane = lax.broadcasted_iota(jnp.int32, (c4.shape[0], n), 1)
    out = jnp.broadcast_to(c4[:, 0:1], (c4.shape[0], n))
    for k in range(1, SSD_HPG):
        out = jnp.where(lane >= SSD_HEADDIM * k, c4[:, k:k + 1], out)
    return out


def _head_sums(v):
    lane = lax.broadcasted_iota(jnp.int32, (v.shape[0], SSD_HPG), 1)
    out = jnp.zeros((v.shape[0], SSD_HPG), F32)
    for k in range(SSD_HPG):
        s = jnp.sum(v[:, SSD_HEADDIM * k:SSD_HEADDIM * (k + 1)], axis=1, keepdims=True)
        out = jnp.where(lane == k, s, out)
    return out


def _head_rows(c4):
    n = SSD_HPG * SSD_HEADDIM
    row = lax.broadcasted_iota(jnp.int32, (n, 1), 0)
    out = jnp.broadcast_to(c4[:, 0:1], (n, 1))
    for k in range(1, SSD_HPG):
        out = jnp.where(row >= SSD_HEADDIM * k, c4[:, k:k + 1], out)
    return out


def _ssd_specs(T, Q, rev):
    NC = T // Q
    cc = (lambda c: NC - 1 - c) if rev else (lambda c: c)
    W, N, S = SSD_HPG * SSD_HEADDIM, SSD_STATE, SSD_STEP_GROUPS
    x_spec = pl.BlockSpec((Q, S * W), lambda g, c: (cc(c), g))
    b_spec = pl.BlockSpec((Q, S * N), lambda g, c: (cc(c), D_INNER // (S * N) + g))
    c_spec = pl.BlockSpec((Q, S * N), lambda g, c: (cc(c), (D_INNER + SSD_GROUPS * N) // (S * N) + g))
    dt_spec = pl.BlockSpec((S, Q, SSD_HPG), lambda g, c: (g, cc(c), 0))
    p_spec = pl.BlockSpec((S, 1, SSD_HPG), lambda g, c: (g, 0, 0))
    tri_spec = pl.BlockSpec((Q, Q), lambda g, c: (0, 0))
    h_spec = pl.BlockSpec((S, None, W, N), lambda g, c: (g, cc(c), 0, 0))
    return x_spec, b_spec, c_spec, dt_spec, p_spec, tri_spec, h_spec


def _ssd_views(gi, wide, narrow, lead):
    W, N = SSD_HPG * SSD_HEADDIM, SSD_STATE
    return ([r.at[:, pl.ds(W * gi, W)] for r in wide] + [r.at[:, pl.ds(N * gi, N)] for r in narrow]
            + [r.at[gi] for r in lead])


def _tri(Q):
    tri = jnp.tril(jnp.ones((Q, Q), F32))
    return tri, tri.T


def _ssd_fwd(xc, dtr8, bias8, alog8, dsk8):
    T, Q = xc.shape[0], SSD_Q
    NC = T // Q
    P, W = SSD_HEADDIM, SSD_HPG * SSD_HEADDIM
    x_spec, b_spec, c_spec, dt_spec, p_spec, tri_spec, h_spec = _ssd_specs(T, Q, False)

    def body(x_ref, b_ref, c_ref, dtr_ref, bias_ref, alog_ref, dsk_ref, tri_ref, triu_ref, y_ref, hs_ref, h_scr):
        for gi in range(SSD_STEP_GROUPS):
            xv, yv, bv, cv, dtv, biasv, alogv, dskv, hsv, hv = _ssd_views(
                gi, (x_ref, y_ref), (b_ref, c_ref), (dtr_ref, bias_ref, alog_ref, dsk_ref, hs_ref, h_scr))
            one(xv, bv, cv, dtv, biasv, alogv, dskv, tri_ref, triu_ref, yv, hsv, hv)

    def one(x_ref, b_ref, c_ref, dtr_ref, bias_ref, alog_ref, dsk_ref, tri_ref, triu_ref, y_ref, hs_ref, h_scr):
        @pl.when(pl.program_id(1) == 0)
        def _():
            h_scr[...] = jnp.zeros_like(h_scr)

        hin = h_scr[...]
        hs_ref[...] = hin
        x, Bm, Cm, pre, dt, A, ac4, ar4, cb, causal, atot4 = _ssd_common(
            x_ref, b_ref, c_ref, dtr_ref, bias_ref, alog_ref, tri_ref, triu_ref)
        xdt = x * _per_head(dt, W)
        rest = _dot(Cm, hin, NT) * _per_head(jnp.exp(ac4), W) + _per_head(dsk_ref[...], W) * x
        for k in range(SSD_HPG):
            blk = slice(P * k, P * (k + 1))
            G = cb * _ssd_decay(k, ac4, ar4, causal)
            y_ref[:, blk] = _dot(G, xdt[:, blk], NN) + rest[:, blk]
        xw = xdt * _per_head(jnp.exp(atot4 - ac4), W)
        h_scr[...] = _head_rows(jnp.exp(atot4)) * hin + _dot(xw, Bm, TN)

    tri, triu = _tri(Q)
    return pl.pallas_call(
        body, name="ssd_fwd", grid=(SSD_GROUPS // SSD_STEP_GROUPS, NC),
        in_specs=[x_spec, b_spec, c_spec, dt_spec, p_spec, p_spec, p_spec, tri_spec, tri_spec],
        out_specs=[pl.BlockSpec((Q, SSD_STEP_GROUPS * W), lambda g, c: (c, g)), h_spec],
        out_shape=[jax.ShapeDtypeStruct((T, D_INNER), F32),
                   jax.ShapeDtypeStruct((SSD_GROUPS, NC, W, SSD_STATE), F32)],
        scratch_shapes=[pltpu.VMEM((SSD_STEP_GROUPS, W, SSD_STATE), F32)],
        compiler_params=_params(("parallel", "arbitrary")),
    )(xc, xc, xc, dtr8, bias8, alog8, dsk8, tri, triu)


def _ssd_bwd(xc, dtr8, bias8, alog8, dsk8, hs, dy):
    T, Q = xc.shape[0], SSD_Q
    NC = T // Q
    P, W = SSD_HEADDIM, SSD_HPG * SSD_HEADDIM
    x_spec, b_spec, c_spec, dt_spec, p_spec, tri_spec, h_spec = _ssd_specs(T, Q, True)
    dy_spec = pl.BlockSpec((Q, SSD_STEP_GROUPS * W), lambda g, c: (NC - 1 - c, g))
    dbc_spec = pl.BlockSpec((Q, SSD_STEP_GROUPS * SSD_STATE), lambda g, c: (NC - 1 - c, g))

    def body(x_ref, b_ref, c_ref, dtr_ref, bias_ref, alog_ref, dsk_ref, tri_ref, triu_ref, hs_ref, dy_ref,
             dx_ref, db_ref, dc_ref, ddtr_ref, dbias_ref, dalog_ref, ddsk_ref, dh_scr):
        for gi in range(SSD_STEP_GROUPS):
            (xv, dyv, dxv, bv, cv, dbv, dcv, dtv, biasv, alogv, dskv, hsv, ddtv, dbiasv, dalogv, ddskv, dhv) = _ssd_views(
                gi, (x_ref, dy_ref, dx_ref), (b_ref, c_ref, db_ref, dc_ref),
                (dtr_ref, bias_ref, alog_ref, dsk_ref, hs_ref, ddtr_ref, dbias_ref, dalog_ref, ddsk_ref, dh_scr))
            one(xv, bv, cv, dtv, biasv, alogv, dskv, tri_ref, triu_ref, hsv, dyv,
                dxv, dbv, dcv, ddtv, dbiasv, dalogv, ddskv, dhv)

    def one(x_ref, b_ref, c_ref, dtr_ref, bias_ref, alog_ref, dsk_ref, tri_ref, triu_ref, hs_ref, dy_ref,
            dx_ref, db_ref, dc_ref, ddtr_ref, dbias_ref, dalog_ref, ddsk_ref, dh_scr):
        @pl.when(pl.program_id(1) == 0)
        def _():
            dh_scr[...] = jnp.zeros_like(dh_scr)
            dbias_ref[...] = jnp.zeros_like(dbias_ref)
            dalog_ref[...] = jnp.zeros_like(dalog_ref)
            ddsk_ref[...] = jnp.zeros_like(ddsk_ref)

        x, Bm, Cm, pre, dt, A, ac4, ar4, cb, causal, atot4 = _ssd_common(
            x_ref, b_ref, c_ref, dtr_ref, bias_ref, alog_ref, tri_ref, triu_ref)
        dyv, hin, dho = dy_ref[...], hs_ref[...], dh_scr[...]
        dt_w = _per_head(dt, W)
        xdt = x * dt_w
        E4, F4, etot4 = jnp.exp(ac4), jnp.exp(atot4 - ac4), jnp.exp(atot4)
        F_w = _per_head(F4, W)
        ddsk_ref[...] += _head_sums(_colsum(dyv * x))
        Z = _dot(Cm, hin, NT)
        dZ = dyv * _per_head(E4, W)
        dac4 = _head_sums(dyv * Z) * E4
        dC = _dot(dZ, hin, NN)
        dh_scr[...] = _dot(dZ, Cm, TN) + _head_rows(etot4) * dho
        per_row = jnp.sum(dho * hin, axis=1, keepdims=True)
        lane4 = lax.broadcasted_iota(jnp.int32, (1, SSD_HPG), 1)
        datot4 = jnp.zeros((1, SSD_HPG), F32)
        for k in range(SSD_HPG):
            datot4 = jnp.where(lane4 == k, jnp.sum(per_row[P * k:P * (k + 1)], keepdims=True), datot4)
        datot4 = datot4 * etot4
        dxw = _dot(Bm, dho, NT)
        dB = _dot(xdt * F_w, dho, NN)
        dFa4 = _head_sums(dxw * xdt) * F4
        datot4 = datot4 + _colsum(dFa4)
        dac4 = dac4 - dFa4
        dcb = jnp.zeros((Q, Q), F32)
        lane_q4 = lax.broadcasted_iota(jnp.int32, (Q, SSD_HPG), 1)
        sub_4q = lax.broadcasted_iota(jnp.int32, (SSD_HPG, Q), 0)
        row_sums = jnp.zeros((Q, SSD_HPG), F32)
        col_sums = jnp.zeros((SSD_HPG, Q), F32)
        dxdt_heads = []
        for k in range(SSD_HPG):
            blk = slice(P * k, P * (k + 1))
            L = _ssd_decay(k, ac4, ar4, causal)
            G = cb * L
            dG = _dot(dyv[:, blk], xdt[:, blk], NT)
            dxdt_heads.append(_dot(G, dyv[:, blk], TN))
            dcb = dcb + dG * L
            Mseg = dG * G
            row_sums = jnp.where(lane_q4 == k, jnp.sum(Mseg, axis=1, keepdims=True), row_sums)
            col_sums = jnp.where(sub_4q == k, jnp.sum(Mseg, axis=0, keepdims=True), col_sums)
        dxdt = dxw * F_w + jnp.concatenate(dxdt_heads, axis=1)
        last = lax.broadcasted_iota(jnp.int32, (Q, SSD_HPG), 0) == Q - 1
        dac4 = dac4 + row_sums + jnp.where(last, datot4, 0.0)
        dadt4 = _dot(triu_ref[...], dac4, NN, HI) - _dot(triu_ref[...], col_sums, NT, HI)
        ddt4 = _head_sums(dxdt * x) + dadt4 * A
        dalog_ref[...] += _colsum(dadt4 * dt) * A
        ddtr4 = ddt4 * jax.nn.sigmoid(pre)
        dbias_ref[...] += _colsum(ddtr4)
        ddtr_ref[...] = ddtr4
        dx_ref[...] = (_per_head(dsk_ref[...], W) * dyv + dxdt * dt_w).astype(dx_ref.dtype)
        db_ref[...] = dB + _dot(dcb, Cm, TN)
        dc_ref[...] = dC + _dot(dcb, Bm, NN)

    tri, triu = _tri(Q)
    return pl.pallas_call(
        body, name="ssd_bwd", grid=(SSD_GROUPS // SSD_STEP_GROUPS, NC),
        in_specs=[x_spec, b_spec, c_spec, dt_spec, p_spec, p_spec, p_spec, tri_spec, tri_spec, h_spec, dy_spec],
        out_specs=[dy_spec, dbc_spec, dbc_spec, dt_spec, p_spec, p_spec, p_spec],
        out_shape=[jax.ShapeDtypeStruct((T, D_INNER), F32),
                   jax.ShapeDtypeStruct((T, SSD_GROUPS * SSD_STATE), F32),
                   jax.ShapeDtypeStruct((T, SSD_GROUPS * SSD_STATE), F32),
                   jax.ShapeDtypeStruct(dtr8.shape, F32)] + [jax.ShapeDtypeStruct(bias8.shape, F32)] * 3,
        scratch_shapes=[pltpu.VMEM((SSD_STEP_GROUPS, W, SSD_STATE), F32)],
        compiler_params=_params(("parallel", "arbitrary")),
    )(xc, xc, xc, dtr8, bias8, alog8, dsk8, tri, triu, hs, dy)


def _to_groups(v):
    return v.reshape(SSD_GROUPS, 1, SSD_HPG)


def _mixer_fwd(x, p, w):
    T, D = x.shape
    (hn,) = _rowwise(lambda xt, g: (_rms(xt, g),), [x], [p["mix_norm"]], [(D, BF16)], name="mix_norm")
    winT = w["w_in"]
    u, z, xbc, dtr, gl = [
        _matmul(hn, winT[IN_OFFS[i]:IN_OFFS[i + 1]], "nt", name=f"mix_in{i}") for i in range(5)]
    tiles = _s5_derive(*[p[k][0] for k in ("s5_A_re", "s5_A_im", "s5_log_dt", "s5_B_re", "s5_B_im", "s5_C_re", "s5_C_im")])
    u = _to_seg(u)
    ys = _s5_fwd(u, tiles)
    (g,) = _rowwise(lambda yt, ut, d: (jax.nn.gelu(yt + d * ut),), [ys, u], [p["s5_D"]], [(D, F32)], name="s5_gelu")
    y5 = _matmul(g, w["s5_w_glu"], "nn", name="s5_glu", out=(BF16,),
                 epilogue=lambda acc, gt, b: (gt * jax.nn.sigmoid(acc + b),), extras=[g, p["s5_b_glu"]])
    xc = _conv_fwd(xbc, w["conv_w"], p["conv_b"])
    dtr8 = dtr.reshape(T, SSD_GROUPS, SSD_HPG).transpose(1, 0, 2)
    ssd_p = [_to_groups(p[k]) for k in ("ssd_dt_bias", "ssd_A_log", "ssd_D")]
    yssd_raw, hs = _ssd_fwd(xc, dtr8, *ssd_p)
    (yssd,) = _rowwise(lambda yt, zt, nw: (_rms(yt * jax.nn.silu(zt), nw),), [yssd_raw, z], [p["ssd_norm"]],
                       [(D_INNER, BF16)], name="ssd_gate")
    p5 = _from_seg(_matmul(y5, w["w_proj_s5"], "nn", name="mix_p5"))
    pssd = _matmul(yssd, w["w_proj_ssd"], "nn", name="mix_pssd")

    def merge(glt, at, bt, bg):
        gates = jax.nn.sigmoid(glt + bg)
        return (gates[:, :D] * at + gates[:, D:] * bt,)

    (merged,) = _rowwise(merge, [gl, p5, pssd], [p["b_gate"]], [(D, BF16)], name="mix_merge")
    y = _matmul(merged, w["w_out"], "nn", name="mix_out", epilogue=lambda acc, xt: (xt + acc,), extras=[x])
    res = dict(hn=hn, u=u, z=z, xbc=xbc, gl=gl, tiles=tiles, ys=ys, g=g, y5=y5, xc=xc, dtr8=dtr8, ssd_p=ssd_p,
               yssd_raw=yssd_raw, hs=hs, yssd=yssd, p5=p5, pssd=pssd, merged=merged)
    return y, res


def _mixer_bwd(dy, x, p, w, r):
    T, D = x.shape
    gp = {}
    dmerged = _matmul(dy, w["w_out"], "nt", name="mix_d_merged")
    gp["w_out"] = _matmul(r["merged"], dy, "tn", name="mix_d_wout")

    def merge_bwd(glt, at, bt, dm, bg):
        def f(q, a_, b_):
            gates = jax.nn.sigmoid(q + bg)
            return gates[:, :D] * a_ + gates[:, D:] * b_
        _, vjp = jax.vjp(f, glt, at, bt)
        dq, da_, db_ = vjp(dm)
        return da_, db_, dq, _colsum(dq)

    dp5, dpssd, dgl, gp["b_gate"] = _rowwise(
        merge_bwd, [r["gl"], r["p5"], r["pssd"], dmerged], [p["b_gate"]],
        [(D, BF16), (D, BF16), (2 * D, BF16)], [((1, 2 * D), F32)], name="mix_d_merge")
    dyssd = _matmul(dpssd, w["w_proj_ssd"], "nt", name="mix_d_yssd")
    gp["w_proj_ssd"] = _matmul(r["yssd"], dpssd, "tn", name="mix_d_wpssd")

    def gate_bwd(yt, zt, dyt, nw):
        _, vjp = jax.vjp(lambda a_, b_, c_: _rms(a_ * jax.nn.silu(b_), c_), yt, zt, nw)
        return vjp(dyt)

    dyraw, dz, gp["ssd_norm"] = _rowwise(
        gate_bwd, [r["yssd_raw"], r["z"], dyssd], [p["ssd_norm"]],
        [(D_INNER, F32), (D_INNER, BF16)], [((1, D_INNER), F32)], name="ssd_d_gate")
    dxs, dBm, dCm, ddtr8, dbias8, dalog8, ddsk8 = _ssd_bwd(r["xc"], r["dtr8"], *r["ssd_p"], r["hs"], dyraw)
    gp["ssd_dt_bias"], gp["ssd_A_log"], gp["ssd_D"] = [v.reshape(1, SSD_HEADS) for v in (dbias8, dalog8, ddsk8)]
    ddtr = ddtr8.transpose(1, 0, 2).reshape(T, SSD_HEADS)
    conv = [_conv_bwd(r["xbc"], w["conv_w"], p["conv_b"], d, c0, f"conv_bwd{i}")
            for i, (d, c0) in enumerate(((dxs, 0), (dBm, D_INNER), (dCm, D_INNER + SSD_GROUPS * SSD_STATE)))]
    dxbc = [c[0] for c in conv]
    gp["conv_w"] = jnp.concatenate([c[1] for c in conv], axis=1)
    gp["conv_b"] = jnp.concatenate([c[2] for c in conv], axis=1)

    (dp5,) = _rowwise(lambda t: (t,), [dp5], [], [(D, BF16)], name="mix_d_p5", after=(dxs,))
    dp5 = _to_seg(dp5)
    dy5 = _matmul(dp5, w["w_proj_s5"], "nt", name="mix_d_y5")
    gp["w_proj_s5"] = _matmul(r["y5"], dp5, "tn", name="mix_d_wp5")
    g = r["g"]

    def glu_ep(acc, gt, b, dyt):
        _, vjp = jax.vjp(lambda g_, t_: g_ * jax.nn.sigmoid(t_ + b), gt, acc)
        return vjp(dyt)

    dg1, dt_ = _matmul(g, w["s5_w_glu"], "nn", name="s5_d_glu", out=(F32, BF16), epilogue=glu_ep,
                       extras=[g, p["s5_b_glu"], dy5])
    gp["s5_w_glu"] = _matmul(g, dt_, "tn", name="s5_d_wglu")
    dg = _matmul(dt_, w["s5_w_glu"], "nt", name="s5_d_g", epilogue=lambda acc, e: (acc + e,), extras=[dg1])

    def gelu_bwd(yt, ut, dgt, dtt, d):
        _, vjp = jax.vjp(lambda y_, d_: jax.nn.gelu(y_ + d_ * ut), yt, d)
        dys, dd = vjp(dgt)
        return dys, dys * d, dd, _colsum(dtt.astype(F32))

    dys, dusk, gp["s5_D"], gp["s5_b_glu"] = _rowwise(
        gelu_bwd, [r["ys"], r["u"], dg, dt_], [p["s5_D"]], [(D, F32), (D, F32)], [((1, D), F32), ((1, D), F32)],
        name="s5_d_gelu")
    du, *dtiles = _s5_bwd(r["u"], dys, dusk, r["tiles"])
    du = _from_seg(du)

    pieces = [du, dz, None, ddtr, dgl]
    winT = w["w_in"]
    hn = r["hn"]
    d_rows, dhn = [], None
    cols = [(du, 0, 1024), (dz, 1024, 3072), (dxbc[0], 3072, 5120), (dxbc[1], 5120, 6144), (dxbc[2], 6144, 7168),
            (ddtr, 7168, 7200), (dgl, 7200, 9248)]
    for i, (d, lo, hi) in enumerate(cols):
        d_rows.append(_matmul(d, hn, "tn", name=f"mix_d_win{i}"))
        if dhn is None:
            dhn = _matmul(d, winT[lo:hi], "nn", name=f"mix_d_hn{i}")
        else:
            dhn = _matmul(d, winT[lo:hi], "nn", name=f"mix_d_hn{i}", epilogue=lambda acc, e: (acc + e,), extras=[dhn])
    gp["w_in"] = jnp.concatenate(d_rows, axis=0)
    del pieces

    def norm_bwd(xt, dh, dyt, gn):
        _, vjp = jax.vjp(_rms, xt, gn)
        dx, dgn = vjp(dh)
        return dyt + dx, dgn

    dx, gp["mix_norm"] = _rowwise(norm_bwd, [x, dhn, dy], [p["mix_norm"]], [(D, F32)], [((1, D), F32)], name="mix_d_norm")
    return dx, gp, dtiles


def _s5_param_grads(p, dtiles):
    keys = ("s5_A_re", "s5_A_im", "s5_log_dt", "s5_B_re", "s5_B_im", "s5_C_re", "s5_C_im")
    _, vjp = jax.vjp(_s5_derive, *[p[k][0] for k in keys])
    return {k: v[None] for k, v in zip(keys, vjp(tuple(dtiles)))}


def _local_step(x, target, p, w, late_weights, exchange):
    T, D = x.shape
    x1, r1 = _ffn_fwd(x, p["ffn1_norm"], w["ffn1_w_gate"], w["ffn1_w_up"], w["ffn1_w_down"], "ffn1")
    x2, rm = _mixer_fwd(x1, p, w)
    w = {**w, **late_weights(rm["p5"])}
    x3, r2 = _ffn_fwd(x2, p["ffn2_norm"], w["ffn2_w_gate"], w["ffn2_w_up"], w["ffn2_w_down"], "ffn2")

    def head(xt, tt, g):
        def f(x_, g_):
            e = _rms(x_, g_) - tt
            return 0.5 * jnp.sum(jnp.mean(e * e, axis=-1))
        l, (dx_, dg_) = jax.value_and_grad(f, argnums=(0, 1))(xt, g)
        return dx_, l.reshape(1, 1), dg_

    dx3, loss, d_final = _rowwise(head, [x3, target], [p["final_norm"]], [(D, F32)],
                                  [((1, 1), F32), ((1, D), F32)], name="loss_head")
    gp = {"final_norm": d_final}

    def ffn_exchange(tag):
        return lambda d_wgT, d_wuT, d_wd, after: exchange(
            tag, {f"{tag}_w_gate": d_wgT, f"{tag}_w_up": d_wuT, f"{tag}_w_down": d_wd}, after)

    dx2, gp["ffn2_norm"] = _ffn_bwd(
        dx3, x2, p["ffn2_norm"], w["ffn2_w_gate"], w["ffn2_w_up"], w["ffn2_w_down"], r2, "ffn2", (),
        ffn_exchange("ffn2"))
    dx1, gm, dtiles = _mixer_bwd(dx2, x1, p, w, rm)
    conv_w_grad = gm.pop("conv_w")
    started = exchange("mixer", {k: gm.pop(k) for k in ("w_out", "w_proj_s5", "w_proj_ssd", "s5_w_glu", "w_in")}, dx1)
    gp.update(gm)
    gp.update(_s5_param_grads(p, dtiles))
    dx0, gp["ffn1_norm"] = _ffn_bwd(
        dx1, x, p["ffn1_norm"], w["ffn1_w_gate"], w["ffn1_w_up"], w["ffn1_w_down"], r1, "ffn1", started,
        ffn_exchange("ffn1"))
    return loss, dx0, gp, conv_w_grad


def _adamw(w, m, v, gs, *, name, emit_g, tm=128):
    n_g = len(gs)
    C = w.shape[1]

    def fn(wt, mt, vt, *gt):
        g = gt[0]
        for e in gt[1:]:
            g = g + e
        m2 = ADAM_B1 * mt + (1.0 - ADAM_B1) * g
        v2 = ADAM_B2 * vt + (1.0 - ADAM_B2) * (g * g)
        m_hat = m2 / (1.0 - ADAM_B1 ** ADAM_STEP)
        v_hat = v2 / (1.0 - ADAM_B2 ** ADAM_STEP)
        delta = -ADAM_LR * (m_hat / (jnp.sqrt(v_hat) + ADAM_EPS) + ADAM_WD * wt)
        return ((g,) if emit_g else ()) + (delta, m2, v2)

    assert n_g >= 1
    return _rowwise(fn, [w, m, v, *gs], [], [(C, F32)] * (4 if emit_g else 3), name=name, tm=_rtile(w.shape[0], tm))


def _rtile(r, cap=512):
    if r <= cap:
        return r
    best = None
    for t in range(8, cap + 1, 8):
        if r % t == 0:
            best = t
    assert best is not None, r
    return best


ANY = pl.BlockSpec(memory_space=pl.ANY)


def _where_am_i():
    x, y, c = lax.axis_index("x"), lax.axis_index("y"), lax.axis_index("c")
    ks = (2 * x + y, 2 * (1 - x) + y, 2 * x + (1 - y), 2 * (1 - x) + (1 - y))
    return x, y, c, ks


def _rc(src, dst, ssem, rsem, to):
    return pltpu.make_async_remote_copy(src_ref=src, dst_ref=dst, send_sem=ssem, recv_sem=rsem,
                                        device_id=to, device_id_type=MESH)


def _gather_shards(srcs):
    n = len(srcs)

    def body(*refs):
        _gather_body(refs[:n], refs[n:2 * n], *refs[2 * n:], extra=())

    return pl.pallas_call(
        body, name="gather_shards",
        in_specs=[ANY] * n, out_specs=[ANY] * n,
        out_shape=[jax.ShapeDtypeStruct((4,) + s.shape, s.dtype) for s in srcs],
        scratch_shapes=[pltpu.SemaphoreType.DMA((n, 6)), pltpu.SemaphoreType.DMA((n, 6)), pltpu.SemaphoreType.DMA((n,))],
        compiler_params=pltpu.CompilerParams(has_side_effects=True),
    )(*srcs)


def _handshake_all():
    x, y, c = lax.axis_index("x"), lax.axis_index("y"), lax.axis_index("c")
    barrier = pltpu.get_barrier_semaphore()
    for dx, dy, dc in itertools.product((0, 1), repeat=3):
        if (dx, dy, dc) != (0, 0, 0):
            to = (1 - x if dx else x, 1 - y if dy else y, 1 - c if dc else c)
            pl.semaphore_signal(barrier, inc=1, device_id=to, device_id_type=MESH)
    pl.semaphore_wait(barrier, 7)


SEQUENCER = dict(axis_name="seq", num_cores=1)


def _gather_shards_async(srcs, token, name, collective_id):
    n = len(srcs)

    def body(*refs):
        tok, tok_out = refs[n], refs[2 * n + 1]
        ssem, rsem, lsem = refs[2 * n + 2:]
        _handshake_all()
        _gather_body(refs[:n], refs[n + 1:2 * n + 1], ssem, rsem, lsem,
                     extra=(pltpu.make_async_copy(tok, tok_out, lsem.at[n]),))

    res = pl.kernel(
        body, name=name,
        out_type=[jax.ShapeDtypeStruct((4,) + s.shape, s.dtype) for s in srcs]
        + [jax.ShapeDtypeStruct(token.shape, token.dtype)],
        mesh=plsc.ScalarSubcoreMesh(**SEQUENCER),
        scratch_types=[pltpu.SemaphoreType.DMA((n, 6)), pltpu.SemaphoreType.DMA((n, 6)), pltpu.SemaphoreType.DMA((n + 1,))],
        compiler_params=pltpu.CompilerParams(collective_id=collective_id),
    )(*srcs, token)
    return res[:n], res[n]


def _gather_body(src, out, ssem, rsem, lsem, extra):
        n = len(src)
        x, y, c, (k_me, k_x, k_y, k_d) = _where_am_i()
        sib = (x, y, 1 - c)
        local = [pltpu.make_async_copy(src[i], out[i].at[k_me], lsem.at[i]) for i in range(n)] + list(extra)
        for cp in local:
            cp.start()

        def own(i, q, to):
            return _rc(src[i], out[i].at[k_me], ssem.at[i, q], rsem.at[i, q], to)

        def slab(i, q, k, to):
            return _rc(out[i].at[k], out[i].at[k], ssem.at[i, q], rsem.at[i, q], to)

        @pl.when(c == 1)
        def _():
            sends = [own(i, 0, (1 - x, y, 1)) for i in range(n)]
            for cp in sends:
                cp.start()
            fwd = [slab(i, 3, k_x, sib) for i in range(n)]
            for i in range(n):
                slab(i, 0, k_x, sib).wait_recv()
                fwd[i].start()
            for i in range(n):
                slab(i, 4, k_y, sib).wait_recv()
                slab(i, 5, k_d, sib).wait_recv()
            for cp in sends + fwd:
                cp.wait_send()

        @pl.when(c == 0)
        def _():
            sends = [own(i, 1, (x, 1 - y, 0)) for i in range(n)] + [own(i, 2, (1 - x, 1 - y, 0)) for i in range(n)]
            for cp in sends:
                cp.start()
            fwd = [slab(i, 4, k_y, sib) for i in range(n)] + [slab(i, 5, k_d, sib) for i in range(n)]
            for i in range(n):
                slab(i, 1, k_y, sib).wait_recv()
                fwd[i].start()
            for i in range(n):
                slab(i, 2, k_d, sib).wait_recv()
                fwd[n + i].start()
            for i in range(n):
                slab(i, 3, k_x, sib).wait_recv()
            for cp in sends + fwd:
                cp.wait_send()

        for cp in local:
            cp.wait()


def _swap_slabs(arrs, n_slabs, name, async_id, token):
    n = len(arrs)
    J = max(n_slabs, 1)

    def body(*refs):
        src, out = refs[:n], refs[n:2 * n]
        ssem, rsem = refs[2 * n:]
        x, y, c, (k_me, k_x, k_y, k_d) = _where_am_i()
        sib = (x, y, 1 - c)
        sel = (jnp.where(c == 1, k_y, k_me), jnp.where(c == 1, k_d, k_x))
        cps = []
        for i in range(n):
            for j in range(J):
                s = src[i].at[sel[j]] if n_slabs else src[i]
                d = out[i].at[j] if n_slabs else out[i]
                cps.append(_rc(s, d, ssem.at[i, j], rsem.at[i, j], sib))
        for cp in cps:
            cp.start()
        for cp in cps:
            cp.wait()

    return _exchange_call(
        body, arrs, [jax.ShapeDtypeStruct(((n_slabs,) if n_slabs else ()) + a.shape[-2:], a.dtype) for a in arrs],
        [pltpu.SemaphoreType.DMA((n, J)), pltpu.SemaphoreType.DMA((n, J))], name, async_id, token)


def _exchange_call(body, arrs, out_shape, sems, name, collective_id, token):
    n, m = len(arrs), len(out_shape)

    def seq_body(*refs):
        tok, tok_out, tok_sem = refs[n], refs[n + 1 + m], refs[-1]
        _handshake_all()
        cp = pltpu.make_async_copy(tok, tok_out, tok_sem)
        cp.start()
        body(*refs[:n], *refs[n + 1:n + 1 + m], *refs[n + 2 + m:-1])
        cp.wait()

    res = pl.kernel(
        seq_body, name=name, out_type=list(out_shape) + [jax.ShapeDtypeStruct(token.shape, token.dtype)],
        mesh=plsc.ScalarSubcoreMesh(**SEQUENCER), scratch_types=list(sems) + [pltpu.SemaphoreType.DMA],
        compiler_params=pltpu.CompilerParams(collective_id=collective_id))(*arrs, token)
    return res[:m], res[m]


def _send_chip_sums(arrs, name, async_id, token):
    n = len(arrs)

    def body(*refs):
        src, out = refs[:n], refs[n:2 * n]
        ssem, rsem = refs[2 * n:]
        x, y, c, _ = _where_am_i()

        @pl.when(c == 1)
        def _():
            cps = [_rc(src[i].at[1], out[i].at[0], ssem.at[i, 0], rsem.at[i, 0], (1 - x, y, 1)) for i in range(n)]
            for cp in cps:
                cp.start()
            for cp in cps:
                cp.wait()

        @pl.when(c == 0)
        def _():
            cps = [_rc(src[i].at[0], out[i].at[0], ssem.at[i, 0], rsem.at[i, 0], (x, 1 - y, 0)) for i in range(n)]
            cps += [_rc(src[i].at[1], out[i].at[1], ssem.at[i, 1], rsem.at[i, 1], (1 - x, 1 - y, 0)) for i in range(n)]
            for cp in cps:
                cp.start()
            for cp in cps:
                cp.wait()

    return _exchange_call(
        body, arrs, [jax.ShapeDtypeStruct(a.shape, a.dtype) for a in arrs],
        [pltpu.SemaphoreType.DMA((n, 2)), pltpu.SemaphoreType.DMA((n, 2))], name, async_id, token)


def _chip_sum(g, recv, sel, name):
    _, r, C = g.shape
    tr = _rtile(r)

    def body(sel_ref, g_ref, r_ref, o32_ref, o16_ref):
        s = g_ref[...] + r_ref[...]
        o32_ref[...] = s
        o16_ref[...] = s.astype(BF16)

    blk = pl.BlockSpec((None, tr, C), lambda j, t, sel_ref: (j, t, 0))
    return pl.pallas_call(
        body, name=name,
        grid_spec=pltpu.PrefetchScalarGridSpec(
            num_scalar_prefetch=1, grid=(2, r // tr),
            in_specs=[pl.BlockSpec((None, tr, C), lambda j, t, sel_ref: (sel_ref[j], t, 0)), blk],
            out_specs=[blk, blk]),
        out_shape=[jax.ShapeDtypeStruct((2, r, C), F32), jax.ShapeDtypeStruct((2, r, C), BF16)],
        compiler_params=_params(("arbitrary", "arbitrary")),
    )(sel, g, recv)


def _cross_sum(s32, got, north, after, name):
    _, r, C = s32.shape
    tr = _rtile(r)

    def body(f_ref, p_ref, a_ref, b_ref, after_ref, o_ref):
        a = a_ref[...].astype(F32)

        @pl.when(f_ref[0] == 1)
        def _():
            o_ref[...] = p_ref[...] + a

        @pl.when(f_ref[0] == 0)
        def _():
            o_ref[...] = a + b_ref[...].astype(F32)

    return pl.pallas_call(
        body, name=name,
        grid_spec=pltpu.PrefetchScalarGridSpec(
            num_scalar_prefetch=1, grid=(r // tr,),
            in_specs=[pl.BlockSpec((None, tr, C), lambda t, f: (0, t, 0)),
                      pl.BlockSpec((None, tr, C), lambda t, f: (0, t, 0)),
                      pl.BlockSpec((None, tr, C), lambda t, f: (1 - f[0], t, 0)), ANY],
            out_specs=pl.BlockSpec((tr, C), lambda t, f: (t, 0))),
        out_shape=jax.ShapeDtypeStruct((r, C), F32),
        compiler_params=_params(("arbitrary",)),
    )(north, s32, got, got, after)


class _Reducer:
    def __init__(self, token):
        self.token, self.pending, self.pairs = token, None, {}

    def begin(self, tag, ids, grads, after):
        names = list(grads)
        arrs = [grads[n].reshape(4, -1, D_MODEL) for n in names]
        x, y, c, (k_me, k_x, k_y, k_d) = _where_am_i()
        sel = jnp.stack([jnp.where(c == 1, k_me, k_y), jnp.where(c == 1, k_x, k_d)]).astype(jnp.int32)
        from_sib, self.token = _swap_slabs(arrs, 2, f"swap_grad_slabs_{tag}", ids[0], self.token)
        sums = [_chip_sum(g, r, sel, f"chip_sum_{tag}{i}") for i, (g, r) in enumerate(zip(arrs, from_sib))]
        self.finish(after)
        got, self.token = _send_chip_sums([s16 for _, s16 in sums], f"send_chip_sums_{tag}", ids[1], self.token)
        self.pending = (tag, ids[2], names, sums, got)
        return tuple(s16 for _, s16 in sums)

    def finish(self, after):
        if self.pending is None:
            return
        tag, cid, names, sums, got = self.pending
        north = jnp.reshape(lax.axis_index("c"), (1,)).astype(jnp.int32)
        parts = [_cross_sum(s32, b, north, after, f"cross_sum_{tag}{i}") for i, ((s32, _), b) in enumerate(zip(sums, got))]
        theirs, self.token = _swap_slabs(parts, 0, f"swap_parts_{tag}", cid, self.token)
        self.pairs.update(zip(names, zip(parts, theirs)))
        self.pending = None


def _all_reduce_small(v):
    R, C = v.shape

    def body(v_ref, o_ref, buf, ssem, rsem, lsem):
        x, y, c = lax.axis_index("x"), lax.axis_index("y"), lax.axis_index("c")
        me, sib = (x, y, c), (x, y, 1 - c)
        chips = [(1 - x, y), (x, 1 - y), (1 - x, 1 - y)]

        def slot(px, py, pc):
            return buf.at[4 * px + 2 * py + pc]

        def copy(k, block, to, src=None):
            return _rc(slot(*block) if src is None else src, slot(*block), ssem.at[k], rsem.at[k], to)

        mine = pltpu.make_async_copy(v_ref, slot(*me), lsem)
        mine.start()
        first = [copy(0, me, sib, src=v_ref)] + [copy(1 + j, me, (*chip, c), src=v_ref) for j, chip in enumerate(chips)]
        for cp in first:
            cp.start()
        passed = [copy(4 + j, (*chip, c), sib) for j, chip in enumerate(chips)]
        for j, chip in enumerate(chips):
            copy(1 + j, (*chip, c), me).wait_recv()
            passed[j].start()
        copy(0, sib, me).wait_recv()
        for j, chip in enumerate(chips):
            copy(4 + j, (*chip, 1 - c), me).wait_recv()
        for cp in first + passed:
            cp.wait_send()
        mine.wait()
        acc = buf[0]
        for k in range(1, 8):
            acc = acc + buf[k]
        o_ref[...] = acc

    return pl.pallas_call(
        body, name="all_reduce_small",
        in_specs=[pl.BlockSpec(memory_space=pltpu.VMEM)], out_specs=pl.BlockSpec(memory_space=pltpu.VMEM),
        out_shape=jax.ShapeDtypeStruct((R, C), F32),
        scratch_shapes=[pltpu.VMEM((8, R, C), F32), pltpu.SemaphoreType.DMA((7,)), pltpu.SemaphoreType.DMA((7,)),
                        pltpu.SemaphoreType.DMA],
        compiler_params=pltpu.CompilerParams(has_side_effects=True, vmem_limit_bytes=VMEM_LIMIT),
    )(v)


WEIGHTS = ['ffn1_norm', 'ffn1_w_gate', 'ffn1_w_up', 'ffn1_w_down', 'mix_norm', 'w_in', 'conv_w', 'conv_b', 's5_A_re',
           's5_A_im', 's5_log_dt', 's5_B_re', 's5_B_im', 's5_C_re', 's5_C_im', 's5_D', 's5_w_glu', 's5_b_glu',
           'ssd_A_log', 'ssd_dt_bias', 'ssd_D', 'ssd_norm', 'w_proj_s5', 'w_proj_ssd', 'b_gate', 'w_out', 'ffn2_norm',
           'ffn2_w_gate', 'ffn2_w_up', 'ffn2_w_down', 'final_norm']
ARGS = ['x'] + WEIGHTS + ['loss_target'] + ['m_' + n for n in WEIGHTS] + ['v_' + n for n in WEIGHTS]
COL_SHARDED = ('ffn1_w_gate', 'ffn1_w_up', 'w_in', 'ffn2_w_gate', 'ffn2_w_up')
ROW_SHARDED = ('ffn1_w_down', 's5_w_glu', 'w_proj_s5', 'w_proj_ssd', 'w_out', 'ffn2_w_down')
MATRICES = COL_SHARDED + ROW_SHARDED
GATHER_FIRST = ('ffn1_w_gate', 'ffn1_w_up', 'ffn1_w_down')
GATHER_SECOND = ('w_in', 'conv_w', 's5_w_glu', 'w_proj_s5', 'w_proj_ssd', 'w_out')
GATHER_THIRD = ('ffn2_w_gate', 'ffn2_w_up', 'ffn2_w_down')
SMALL = [n for n in WEIGHTS if n not in MATRICES]


def _pack(arrs, width=1024):
    flat = jnp.concatenate([a.reshape(-1).astype(F32) for a in arrs])
    rows = -(-flat.shape[0] // (8 * width)) * 8
    return jnp.pad(flat, (0, rows * width - flat.shape[0])).reshape(rows, width)


def _unpack(packed, shapes):
    flat, out, o = packed.reshape(-1), [], 0
    for s in shapes:
        n = math.prod(s)
        out.append(flat[o:o + n].reshape(s))
        o += n
    return out


def kernel(x, ffn1_norm, ffn1_w_gate, ffn1_w_up, ffn1_w_down, mix_norm, w_in, conv_w, conv_b, s5_A_re, s5_A_im, s5_log_dt, s5_B_re, s5_B_im, s5_C_re, s5_C_im, s5_D, s5_w_glu, s5_b_glu, ssd_A_log, ssd_dt_bias, ssd_D, ssd_norm, w_proj_s5, w_proj_ssd, b_gate, w_out, ffn2_norm, ffn2_w_gate, ffn2_w_up, ffn2_w_down, final_norm, loss_target, m_ffn1_norm, m_ffn1_w_gate, m_ffn1_w_up, m_ffn1_w_down, m_mix_norm, m_w_in, m_conv_w, m_conv_b, m_s5_A_re, m_s5_A_im, m_s5_log_dt, m_s5_B_re, m_s5_B_im, m_s5_C_re, m_s5_C_im, m_s5_D, m_s5_w_glu, m_s5_b_glu, m_ssd_A_log, m_ssd_dt_bias, m_ssd_D, m_ssd_norm, m_w_proj_s5, m_w_proj_ssd, m_b_gate, m_w_out, m_ffn2_norm, m_ffn2_w_gate, m_ffn2_w_up, m_ffn2_w_down, m_final_norm, v_ffn1_norm, v_ffn1_w_gate, v_ffn1_w_up, v_ffn1_w_down, v_mix_norm, v_w_in, v_conv_w, v_conv_b, v_s5_A_re, v_s5_A_im, v_s5_log_dt, v_s5_B_re, v_s5_B_im, v_s5_C_re, v_s5_C_im, v_s5_D, v_s5_w_glu, v_s5_b_glu, v_ssd_A_log, v_ssd_dt_bias, v_ssd_D, v_ssd_norm, v_w_proj_s5, v_w_proj_ssd, v_b_gate, v_w_out, v_ffn2_norm, v_ffn2_w_gate, v_ffn2_w_up, v_ffn2_w_down, v_final_norm):
    a = dict(locals())
    assert list(a) == ARGS
    x, target = a['x'][0], a['loss_target'][0]
    k_me = 2 * lax.axis_index("x") + lax.axis_index("y")

    src = {n: a[n][0].T.astype(BF16) for n in COL_SHARDED}
    src.update({n: a[n][0].astype(BF16) for n in ROW_SHARDED})
    src['conv_w'] = a['conv_w'][0]
    def flat(f):
        return f.reshape(-1, f.shape[-1])

    full = _gather_shards([src[n] for n in GATHER_FIRST])
    w = {n: flat(f) for n, f in zip(GATHER_FIRST, full)}
    got, token = _gather_shards_async([src[n] for n in GATHER_SECOND], full[0][0, :8, :128].astype(F32), "gather_mixer", 1)
    w.update({n: flat(f) for n, f in zip(GATHER_SECOND, got)})
    w['conv_w'] = w['conv_w'].reshape(4, CONV_K, -1).transpose(1, 0, 2).reshape(CONV_K, CONV_DIM)
    reducer = _Reducer(token)

    def gather_last(after):
        got, reducer.token = _gather_shards_async(
            [src[n] for n in GATHER_THIRD], reducer.token + after[:8, :128], "gather_ffn2", 2)
        return {n: flat(f) for n, f in zip(GATHER_THIRD, got)}

    p = {n: a[n] for n in SMALL if n != 'conv_w'}
    p['final_norm'] = a['final_norm'][None]
    ids = {"ffn2": (3, 4, 5), "mixer": (6, 7, 8), "ffn1": (9, 10, 11)}
    loss, grad_x, gp, conv_w_grad = _local_step(
        x, target, p, w, gather_last, lambda tag, grads, after: reducer.begin(tag, ids[tag], grads, after))
    reducer.finish(grad_x)
    pairs = [reducer.pairs[n] for n in MATRICES]
    gp['conv_w'] = conv_w_grad[None]
    small_shapes = [(1, 1)] + [gp[n].shape if n != 'final_norm' else (1, D_MODEL) for n in SMALL]
    red = _unpack(_all_reduce_small(_pack([loss] + [gp[n] for n in SMALL])), small_shapes)
    loss_all = red[0].reshape(())
    gsmall = dict(zip(SMALL, red[1:]))
    gsmall['conv_w'] = lax.dynamic_slice_in_dim(gsmall['conv_w'], k_me * 1024, 1024, axis=2)
    gsmall = {n: g.reshape(a[n].shape) for n, g in gsmall.items()}

    grads, delta, new_m, new_v = {}, {}, {}, {}
    for n, (mine, theirs) in zip(MATRICES, pairs):
        wn, mn, vn = a[n][0], a['m_' + n][0], a['v_' + n][0]
        if n in COL_SHARDED:
            (gt,) = _rowwise(lambda p_, q_: (p_ + q_,), [mine, theirs], [], [(D_MODEL, F32)], name=f"sum_{n}",
                             tm=_rtile(mine.shape[0]))
            g = gt.T
            d, m2, v2 = _adamw(wn, mn, vn, [g], name=f"adamw_{n}", emit_g=False)
        else:
            g, d, m2, v2 = _adamw(wn, mn, vn, [mine, theirs], name=f"adamw_{n}", emit_g=True)
        grads[n], delta[n], new_m[n], new_v[n] = g[None], d[None], m2[None], v2[None]
    sw, sm, sv, sg = [_pack([t[n] for n in SMALL]) for t in (a, {n: a['m_' + n] for n in SMALL},
                                                             {n: a['v_' + n] for n in SMALL}, gsmall)]
    d, m2, v2 = _adamw(sw, sm, sv, [sg], name="adamw_small", emit_g=False)
    shapes = [a[n].shape for n in SMALL]
    for n, dd, mm, vv in zip(SMALL, _unpack(d, shapes), _unpack(m2, shapes), _unpack(v2, shapes)):
        grads[n], delta[n], new_m[n], new_v[n] = gsmall[n], dd, mm, vv
    return (loss_all, grad_x[None], *[grads[n] for n in WEIGHTS], *[delta[n] for n in WEIGHTS],
            *[new_m[n] for n in WEIGHTS], *[new_v[n] for n in WEIGHTS])
```

```python
import itertools
import math

import jax
import jax.numpy as jnp
from jax import lax
from jax.experimental import pallas as pl
from jax.experimental.pallas import tpu as pltpu
from jax.experimental.pallas import tpu_sc as plsc

F32 = jnp.float32
BF16 = jnp.bfloat16
MXU = BF16
HI = lax.Precision.HIGHEST

D_MODEL = 1024
D_FF = 2816
EPS = 1e-6
S5_GROUPS, S5_GROUP, S5_STATE = 64, 16, 64
D_INNER = 2048
SSD_HEADDIM, SSD_HEADS, SSD_GROUPS, SSD_HPG, SSD_STATE = 64, 32, 8, 4, 128
CONV_K, CONV_DIM = 4, 4096
IN_SPLITS = (1024, 2048, 4096, 32, 2048)
IN_OFFS = (0, 1024, 3072, 7168, 7200, 9248)
SSD_Q = 256
SSD_STEP_GROUPS = 2
ADAM_LR, ADAM_B1, ADAM_B2, ADAM_EPS, ADAM_WD, ADAM_STEP = 0.001, 0.9, 0.999, 1e-08, 0.01, 10

VMEM_LIMIT = 56 * 1024 * 1024
MATMUL_VMEM = 32 * 1024 * 1024
MESH = pl.DeviceIdType.MESH

NN = ((1,), (0,))
NT = ((1,), (1,))
TN = ((0,), (0,))


def _dot(a, b, dims, precision=None):
    if precision is None:
        a, b = a.astype(MXU), b.astype(MXU)
    return lax.dot_general(a, b, (dims, ((), ())), precision=precision, preferred_element_type=F32)


def _tile(n, pref):
    if n <= pref:
        return n
    best = None
    for t in range(128, pref + 1, 128):
        if n % t == 0:
            best = t
    assert best is not None, (n, pref)
    return best


def _params(sem):
    return pltpu.CompilerParams(dimension_semantics=sem, vmem_limit_bytes=VMEM_LIMIT)


def _matmul(a, b, mode, *, name, out=((F32),), epilogue=None, extras=(), after=(), tm=512, tn=1536):
    if mode == "nn":
        (M, K), (_, N) = a.shape, b.shape
    elif mode == "nt":
        (M, K), (N, _) = a.shape, b.shape
    else:
        (K, M), (_, N) = a.shape, b.shape
    tm, tn = _tile(M, tm), _tile(N, tn)

    def vmem(tm_, tn_):
        per = K * tm_ * a.dtype.itemsize + K * tn_ * b.dtype.itemsize
        per += sum(tm_ * tn_ * jnp.dtype(dt).itemsize for dt in out)
        per += sum((1 if e.shape[0] == 1 else tm_) * tn_ * e.dtype.itemsize for e in extras)
        return 2 * per

    while vmem(tm, tn) > MATMUL_VMEM and (tn > 128 or tm > 128):
        if tn >= tm and tn > 128:
            tn = _tile(N, tn - 128)
        else:
            tm = _tile(M, tm - 128)
    a_spec = pl.BlockSpec((K, tm), lambda i, j: (0, i)) if mode == "tn" else pl.BlockSpec((tm, K), lambda i, j: (i, 0))
    b_spec = pl.BlockSpec((tn, K), lambda i, j: (j, 0)) if mode == "nt" else pl.BlockSpec((K, tn), lambda i, j: (0, j))
    dims = {"nn": NN, "nt": NT, "tn": TN}[mode]
    e_specs = []
    for e in extras:
        if e.shape[0] == 1:
            e_specs.append(pl.BlockSpec((1, tn), lambda i, j: (0, j)))
        else:
            assert e.shape == (M, N), (e.shape, M, N)
            e_specs.append(pl.BlockSpec((tm, tn), lambda i, j: (i, j)))
    n_e, n_o, n_a = len(extras), len(out), len(after)

    def body(a_ref, b_ref, *refs):
        acc = _dot(a_ref[...], b_ref[...], dims)
        res = (acc,) if epilogue is None else epilogue(acc, *[r[...] for r in refs[:n_e]])
        for r, v in zip(refs[n_e + n_a:], res):
            r[...] = v.astype(r.dtype)

    res = pl.pallas_call(
        body, name=name, grid=(M // tm, N // tn),
        in_specs=[a_spec, b_spec] + e_specs + [ANY] * n_a,
        out_specs=[pl.BlockSpec((tm, tn), lambda i, j: (i, j)) for _ in range(n_o)],
        out_shape=[jax.ShapeDtypeStruct((M, N), dt) for dt in out],
        compiler_params=_params(("parallel", "parallel")),
    )(a, b, *extras, *after)
    return res[0] if n_o == 1 else res


def _rowwise(fn, rows, fulls, outs, reds=(), *, name, after=(), tm=256):
    T = rows[0].shape[0]
    tm = min(tm, T)
    n_r, n_f, n_o, n_d, n_a = len(rows), len(fulls), len(outs), len(reds), len(after)

    def body(*refs):
        ins = [r[...] for r in refs[:n_r + n_f]]
        o_refs = refs[n_r + n_f + n_a:n_r + n_f + n_a + n_o]
        d_refs = refs[n_r + n_f + n_a + n_o:]
        res = fn(*ins)
        for r, v in zip(o_refs, res[:n_o]):
            r[...] = v.astype(r.dtype)
        if n_d:
            @pl.when(pl.program_id(0) == 0)
            def _():
                for r in d_refs:
                    r[...] = jnp.zeros_like(r)
            for r, v in zip(d_refs, res[n_o:]):
                r[...] += v.astype(r.dtype)

    res = pl.pallas_call(
        body, name=name, grid=(T // tm,),
        in_specs=[pl.BlockSpec((tm, r.shape[1]), lambda i: (i, 0)) for r in rows]
        + [pl.BlockSpec(f.shape, lambda i, nd=f.ndim: (0,) * nd) for f in fulls] + [ANY] * n_a,
        out_specs=[pl.BlockSpec((tm, c), lambda i: (i, 0)) for c, _ in outs]
        + [pl.BlockSpec(s, lambda i, nd=len(s): (0,) * nd) for s, _ in reds],
        out_shape=[jax.ShapeDtypeStruct((T, c), dt) for c, dt in outs]
        + [jax.ShapeDtypeStruct(s, dt) for s, dt in reds],
        compiler_params=_params(("arbitrary",)),
    )(*rows, *fulls, *after)
    return res


def _rms(x, g):
    return x * lax.rsqrt(jnp.mean(x * x, axis=-1, keepdims=True) + EPS) * g


def _colsum(v):
    return jnp.sum(v, axis=0, keepdims=True)


def _softplus(x):
    return jnp.maximum(x, 0.0) + jnp.log1p(jnp.exp(-jnp.abs(x)))


def _ffn_fwd(x, norm, wgT, wuT, wd, tag):
    D = x.shape[1]
    (hn,) = _rowwise(lambda xt, g: (_rms(xt, g),), [x], [norm], [(D, BF16)], name=f"{tag}_norm")
    a = _matmul(hn, wgT, "nt", name=f"{tag}_gate")
    b, hid = _matmul(hn, wuT, "nt", name=f"{tag}_up", out=(BF16, BF16),
                     epilogue=lambda acc, at: (acc, jax.nn.silu(at) * acc), extras=[a])
    y = _matmul(hid, wd, "nn", name=f"{tag}_down", epilogue=lambda acc, xt: (xt + 0.5 * acc,), extras=[x])
    return y, (hn, a, b, hid)


def _ffn_bwd(dy, x, norm, wgT, wuT, wd, res, tag, after, on_grads):
    hn, a, b, hid = res
    D, F = x.shape[1], a.shape[1]

    def act_bwd(acc, at, bt):
        _, vjp = jax.vjp(lambda p, q: jax.nn.silu(p) * q, at, bt.astype(F32))
        return vjp(0.5 * acc)

    da, db = _matmul(dy, wd, "nt", name=f"{tag}_d_hid", out=(BF16, BF16), epilogue=act_bwd, extras=[a, b])
    d_wd = _matmul(hid, dy, "tn", name=f"{tag}_d_wd", epilogue=lambda acc: (0.5 * acc,))
    d_wgT = _matmul(da, hn, "tn", name=f"{tag}_d_wg", after=after)
    d_wuT = _matmul(db, hn, "tn", name=f"{tag}_d_wu")
    started = on_grads(d_wgT, d_wuT, d_wd, dy)
    dhn = _matmul(da, wgT, "nn", name=f"{tag}_d_hn1")
    dhn = _matmul(db, wuT, "nn", name=f"{tag}_d_hn2", epilogue=lambda acc, e: (acc + e,), extras=[dhn])

    def norm_bwd(xt, dh, dyt, g):
        _, vjp = jax.vjp(_rms, xt, g)
        dx, dg = vjp(dh)
        return dyt + dx, dg

    dx, d_norm = _rowwise(norm_bwd, [x, dhn, dy], [norm], [(D, F32)], [((1, D), F32)], name=f"{tag}_d_norm",
                          after=started)
    return dx, d_norm


S5_TILES = 8
S5_HALF = 256


def _s5_derive(A_re, A_im, log_dt, B_re, B_im, C_re, C_im):
    G, N, M = S5_GROUPS, S5_STATE, S5_GROUP
    dt = jnp.exp(log_dt)[:, None]
    mag = jnp.exp(A_re * dt)
    ar, ai = mag * jnp.cos(A_im * dt), mag * jnp.sin(A_im * dt)
    den = A_re * A_re + A_im * A_im
    cr = ((ar - 1.0) * A_re + ai * A_im) / den
    ci = (ai * A_re - (ar - 1.0) * A_im) / den
    bbr = cr[..., None] * B_re - ci[..., None] * B_im
    bbi = cr[..., None] * B_im + ci[..., None] * B_re
    eye = jnp.eye(8, dtype=F32)

    def tile_in(bb):
        t = bb.reshape(S5_TILES, 8, N, M).transpose(0, 1, 3, 2)
        return jnp.einsum("jamn,ab->jambn", t, eye).reshape(S5_TILES, 8 * M, 8 * N)

    def tile_out(c):
        t = c.reshape(S5_TILES, 8, M, N).transpose(0, 1, 3, 2)
        return jnp.einsum("janm,ab->janbm", t, eye).reshape(S5_TILES, 8 * N, 8 * M)

    return (tile_in(bbr), tile_in(bbi), tile_out(C_re), tile_out(C_im),
            ar.reshape(S5_TILES, 1, 8 * N), ai.reshape(S5_TILES, 1, 8 * N))


S5_NB = S5_HALF // 128
S5_SEG = 32


def _cmul(ar, ai, br, bi):
    return ar * br - ai * bi, ar * bi + ai * br


def _s5_scan(sr_ref, si_ref, ar, ai, T, reverse):
    L, V = T // S5_SEG, S5_SEG // 8
    assert L * S5_SEG == T and L & (L - 1) == 0, T
    sg = -1.0 if reverse else 1.0
    a_r = [jnp.broadcast_to(ar[:, 128 * b:128 * (b + 1)], (8, 128)) for b in range(S5_NB)]
    a_i = [jnp.broadcast_to(sg * ai[:, 128 * b:128 * (b + 1)], (8, 128)) for b in range(S5_NB)]

    def rows(k, v):
        return pl.ds(pl.multiple_of(((L - 1 - k) if reverse else k) * S5_SEG + 8 * v, 8), 8)

    def local(k, carry):
        out = []
        for b in range(S5_NB):
            for v in range(V):
                idx = rows(k, v)
                mr, mi = _cmul(a_r[b], a_i[b], *carry[b * V + v])
                nr, ni = mr + sr_ref[b, idx, :], mi + si_ref[b, idx, :]
                sr_ref[b, idx, :] = nr
                si_ref[b, idx, :] = ni
                out.append((nr, ni))
        return tuple(out)

    z = jnp.zeros((8, 128), F32)
    ends = lax.fori_loop(0, L, local, tuple((z, z) for _ in range(S5_NB * V)))

    carries = []
    for b in range(S5_NB):
        pr, pi = a_r[b][0:1], a_i[b][0:1]
        n = L
        while n > 1:
            pr, pi = _cmul(pr, pi, pr, pi)
            n //= 2
        cr, ci = jnp.zeros((1, 128), F32), jnp.zeros((1, 128), F32)
        into = [None] * S5_SEG
        for j in (reversed(range(S5_SEG)) if reverse else range(S5_SEG)):
            into[j] = (cr, ci)
            er, ei = ends[b * V + j // 8]
            mr, mi = _cmul(pr, pi, cr, ci)
            cr, ci = mr + er[j % 8:j % 8 + 1], mi + ei[j % 8:j % 8 + 1]
        carries.append([(jnp.concatenate([into[8 * v + s][0] for s in range(8)], axis=0),
                         jnp.concatenate([into[8 * v + s][1] for s in range(8)], axis=0)) for v in range(V)])

    def fix(k, powers):
        out = []
        for b in range(S5_NB):
            pr, pi = powers[b]
            for v in range(V):
                idx = rows(k, v)
                dr, di = _cmul(pr, pi, *carries[b][v])
                sr_ref[b, idx, :] += dr
                si_ref[b, idx, :] += di
            out.append(_cmul(pr, pi, a_r[b], a_i[b]))
        return tuple(out)

    lax.fori_loop(0, L, fix, tuple((a_r[b], a_i[b]) for b in range(S5_NB)))


def _s5_in(ut, b_ref, s_ref):
    for b in range(S5_NB):
        s_ref[b] = _dot(ut, b_ref[:, 128 * b:128 * (b + 1)], NN)


def _to_seg(v):
    T, C = v.shape
    return v.reshape(S5_SEG, T // S5_SEG, C).transpose(1, 0, 2).reshape(T, C)


def _from_seg(v):
    T, C = v.shape
    return v.reshape(T // S5_SEG, S5_SEG, C).transpose(1, 0, 2).reshape(T, C)


def _s5_specs(T):
    u_spec = pl.BlockSpec((T, 128), lambda j, h: (0, j))
    b_spec = pl.BlockSpec((None, 128, S5_HALF), lambda j, h: (j, 0, h))
    c_spec = pl.BlockSpec((None, S5_HALF, 128), lambda j, h: (j, h, 0))
    a_spec = pl.BlockSpec((None, 1, S5_HALF), lambda j, h: (j, 0, h))
    return u_spec, b_spec, c_spec, a_spec


def _s5_fwd(u, tiles):
    T = u.shape[0]
    u_spec, b_spec, c_spec, a_spec = _s5_specs(T)

    def body(u_ref, br_ref, bi_ref, cr_ref, ci_ref, ar_ref, ai_ref, y_ref, sr_ref, si_ref):
        ut = u_ref[...].astype(MXU)
        _s5_in(ut, br_ref, sr_ref)
        _s5_in(ut, bi_ref, si_ref)
        _s5_scan(sr_ref, si_ref, ar_ref[...], ai_ref[...], T, False)
        y = None
        for b in range(S5_NB):
            blk = slice(128 * b, 128 * (b + 1))
            yb = _dot(sr_ref[b], cr_ref[blk, :], NN) - _dot(si_ref[b], ci_ref[blk, :], NN)
            y = yb if y is None else y + yb

        @pl.when(pl.program_id(1) == 0)
        def _():
            y_ref[...] = y

        @pl.when(pl.program_id(1) == 1)
        def _():
            y_ref[...] += y

    scr = pltpu.VMEM((S5_NB, T, 128), F32)
    return pl.pallas_call(
        body, name="s5_fwd", grid=(S5_TILES, 2),
        in_specs=[u_spec, b_spec, b_spec, c_spec, c_spec, a_spec, a_spec],
        out_specs=u_spec, out_shape=jax.ShapeDtypeStruct(u.shape, F32),
        scratch_shapes=[scr, scr],
        compiler_params=_params(("parallel", "arbitrary")),
    )(u, *tiles)


def _s5_bwd(u, dys, du_skip, tiles):
    T = u.shape[0]
    u_spec, b_spec, c_spec, a_spec = _s5_specs(T)

    def body(u_ref, dy_ref, sk_ref, br_ref, bi_ref, cr_ref, ci_ref, ar_ref, ai_ref,
             du_ref, dbr_ref, dbi_ref, dcr_ref, dci_ref, dar_ref, dai_ref, sr_ref, si_ref, lr_ref, li_ref):
        ut, dy = u_ref[...].astype(MXU), dy_ref[...].astype(MXU)
        ar, ai = ar_ref[...], ai_ref[...]
        _s5_in(ut, br_ref, sr_ref)
        _s5_in(ut, bi_ref, si_ref)
        _s5_scan(sr_ref, si_ref, ar, ai, T, False)
        for b in range(S5_NB):
            blk = slice(128 * b, 128 * (b + 1))
            dcr_ref[blk, :] = _dot(sr_ref[b], dy, TN)
            dci_ref[blk, :] = -_dot(si_ref[b], dy, TN)
            lr_ref[b] = _dot(dy, cr_ref[blk, :], NT)
            li_ref[b] = -_dot(dy, ci_ref[blk, :], NT)
        _s5_scan(lr_ref, li_ref, ar, ai, T, True)
        du = None
        head = lax.broadcasted_iota(jnp.int32, (T, 128), 0) < S5_SEG
        seg0 = lax.broadcasted_iota(jnp.int32, (S5_SEG, 128), 0) == 0
        for b in range(S5_NB):
            blk = slice(128 * b, 128 * (b + 1))
            lr, li = lr_ref[b], li_ref[b]
            pr = jnp.where(head, 0.0, pltpu.roll(sr_ref[b], S5_SEG, 0))
            pi = jnp.where(head, 0.0, pltpu.roll(si_ref[b], S5_SEG, 0))
            er = jnp.where(seg0, 0.0, pltpu.roll(sr_ref[b, T - S5_SEG:T, :], 1, 0))
            ei = jnp.where(seg0, 0.0, pltpu.roll(si_ref[b, T - S5_SEG:T, :], 1, 0))
            hr, hi = lr[0:S5_SEG], li[0:S5_SEG]
            dar_ref[:, blk] = _colsum(lr * pr + li * pi) + _colsum(hr * er + hi * ei)
            dai_ref[:, blk] = _colsum(li * pr - lr * pi) + _colsum(hi * er - hr * ei)
            dbr_ref[:, blk] = _dot(ut, lr, TN)
            dbi_ref[:, blk] = _dot(ut, li, TN)
            dub = _dot(lr, br_ref[:, blk], NT) + _dot(li, bi_ref[:, blk], NT)
            du = dub if du is None else du + dub

        @pl.when(pl.program_id(1) == 0)
        def _():
            du_ref[...] = du + sk_ref[...]

        @pl.when(pl.program_id(1) == 1)
        def _():
            du_ref[...] += du

    scr = pltpu.VMEM((S5_NB, T, 128), F32)
    return pl.pallas_call(
        body, name="s5_bwd", grid=(S5_TILES, 2),
        in_specs=[u_spec, u_spec, u_spec, b_spec, b_spec, c_spec, c_spec, a_spec, a_spec],
        out_specs=[u_spec, b_spec, b_spec, c_spec, c_spec, a_spec, a_spec],
        out_shape=[jax.ShapeDtypeStruct(u.shape, F32)] + [jax.ShapeDtypeStruct(t.shape, F32) for t in tiles],
        scratch_shapes=[scr, scr, scr, scr],
        compiler_params=_params(("parallel", "arbitrary")),
    )(u, dys, du_skip, *tiles)


CONV_TILE = 256


def _conv_pre(x, w, b):
    T = x.shape[0]
    row = lax.broadcasted_iota(jnp.int32, x.shape, 0)
    acc = x * w[CONV_K - 1:CONV_K, :] + b
    for lag in range(1, CONV_K):
        acc = acc + jnp.where(row >= lag, pltpu.roll(x, lag, 0), 0.0) * w[CONV_K - 1 - lag:CONV_K - lag, :]
    return acc


def _conv_fwd(x, w, b):
    T, C = x.shape
    col = pl.BlockSpec((T, CONV_TILE), lambda j: (0, j))

    def body(x_ref, w_ref, b_ref, o_ref):
        o_ref[...] = jax.nn.silu(_conv_pre(x_ref[...], w_ref[...], b_ref[...]))

    return pl.pallas_call(
        body, name="conv_fwd", grid=(C // CONV_TILE,),
        in_specs=[col, pl.BlockSpec((CONV_K, CONV_TILE), lambda j: (0, j)), pl.BlockSpec((1, CONV_TILE), lambda j: (0, j))],
        out_specs=col, out_shape=jax.ShapeDtypeStruct((T, C), F32),
        compiler_params=_params(("parallel",)),
    )(x, w, b)


def _conv_bwd(x, w, b, dout, col0, name):
    T, C = dout.shape
    off = col0 // CONV_TILE
    xcol = pl.BlockSpec((T, CONV_TILE), lambda j: (0, j + off))
    dcol = pl.BlockSpec((T, CONV_TILE), lambda j: (0, j))

    def body(x_ref, w_ref, b_ref, d_ref, dx_ref, dw_ref, db_ref):
        x, w = x_ref[...], w_ref[...]
        _, vjp = jax.vjp(jax.nn.silu, _conv_pre(x, w, b_ref[...]))
        (dy,) = vjp(d_ref[...])
        row = lax.broadcasted_iota(jnp.int32, x.shape, 0)
        dx = dy * w[CONV_K - 1:CONV_K, :]
        dw_ref[CONV_K - 1:CONV_K, :] = _colsum(dy * x)
        for lag in range(1, CONV_K):
            dx = dx + jnp.where(row < T - lag, pltpu.roll(dy, T - lag, 0), 0.0) * w[CONV_K - 1 - lag:CONV_K - lag, :]
            xs = jnp.where(row >= lag, pltpu.roll(x, lag, 0), 0.0)
            dw_ref[CONV_K - 1 - lag:CONV_K - lag, :] = _colsum(dy * xs)
        dx_ref[...] = dx.astype(dx_ref.dtype)
        db_ref[...] = _colsum(dy)

    return pl.pallas_call(
        body, name=name, grid=(C // CONV_TILE,),
        in_specs=[xcol, pl.BlockSpec((CONV_K, CONV_TILE), lambda j: (0, j + off)),
                  pl.BlockSpec((1, CONV_TILE), lambda j: (0, j + off)), dcol],
        out_specs=[dcol, pl.BlockSpec((CONV_K, CONV_TILE), lambda j: (0, j)), pl.BlockSpec((1, CONV_TILE), lambda j: (0, j))],
        out_shape=[jax.ShapeDtypeStruct((T, C), BF16), jax.ShapeDtypeStruct((CONV_K, C), F32), jax.ShapeDtypeStruct((1, C), F32)],
        compiler_params=_params(("parallel",)),
    )(x, w, b, dout)


def _ssd_common(x_ref, b_ref, c_ref, dtr_ref, bias_ref, alog_ref, tri_ref, triu_ref):
    Q = x_ref.shape[0]
    x, Bm, Cm = x_ref[...], b_ref[...], c_ref[...]
    pre = dtr_ref[...] + bias_ref[...]
    dt = _softplus(pre)
    A = -jnp.exp(alog_ref[...])
    adt = dt * A
    ac4 = _dot(tri_ref[...], adt, NN, HI)
    ar4 = _dot(adt, triu_ref[...], TN, HI)
    cb = _dot(Cm, Bm, NT)
    ii = lax.broadcasted_iota(jnp.int32, (Q, Q), 0)
    jj = lax.broadcasted_iota(jnp.int32, (Q, Q), 1)
    atot4 = ac4[Q - 1:Q, :]
    return x, Bm, Cm, pre, dt, A, ac4, ar4, cb, ii >= jj, atot4


def _ssd_decay(k, ac4, ar4, causal):
    seg = ac4[:, k:k + 1] - ar4[k:k + 1, :]
    return jnp.where(causal, jnp.exp(jnp.where(causal, seg, 0.0)), 0.0)


def _per_head(c4, n):
    lane = lax.broadcasted_iota(jnp.int32, (c4.shape[0], n), 1)
    out = jnp.broadcast_to(c4[:, 0:1], (c4.shape[0], n))
    for k in range(1, SSD_HPG):
        out = jnp.where(lane >= SSD_HEADDIM * k, c4[:, k:k + 1], out)
    return out


def _head_sums(v):
    lane = lax.broadcasted_iota(jnp.int32, (v.shape[0], SSD_HPG), 1)
    out = jnp.zeros((v.shape[0], SSD_HPG), F32)
    for k in range(SSD_HPG):
        s = jnp.sum(v[:, SSD_HEADDIM * k:SSD_HEADDIM * (k + 1)], axis=1, keepdims=True)
        out = jnp.where(lane == k, s, out)
    return out


def _head_rows(c4):
    n = SSD_HPG * SSD_HEADDIM
    row = lax.broadcasted_iota(jnp.int32, (n, 1), 0)
    out = jnp.broadcast_to(c4[:, 0:1], (n, 1))
    for k in range(1, SSD_HPG):
        out = jnp.where(row >= SSD_HEADDIM * k, c4[:, k:k + 1], out)
    return out


def _ssd_specs(T, Q, rev):
    NC = T // Q
    cc = (lambda c: NC - 1 - c) if rev else (lambda c: c)
    W, N, S = SSD_HPG * SSD_HEADDIM, SSD_STATE, SSD_STEP_GROUPS
    x_spec = pl.BlockSpec((Q, S * W), lambda g, c: (cc(c), g))
    b_spec = pl.BlockSpec((Q, S * N), lambda g, c: (cc(c), D_INNER // (S * N) + g))
    c_spec = pl.BlockSpec((Q, S * N), lambda g, c: (cc(c), (D_INNER + SSD_GROUPS * N) // (S * N) + g))
    dt_spec = pl.BlockSpec((S, Q, SSD_HPG), lambda g, c: (g, cc(c), 0))
    p_spec = pl.BlockSpec((S, 1, SSD_HPG), lambda g, c: (g, 0, 0))
    tri_spec = pl.BlockSpec((Q, Q), lambda g, c: (0, 0))
    h_spec = pl.BlockSpec((S, None, W, N), lambda g, c: (g, cc(c), 0, 0))
    return x_spec, b_spec, c_spec, dt_spec, p_spec, tri_spec, h_spec


def _ssd_views(gi, wide, narrow, lead):
    W, N = SSD_HPG * SSD_HEADDIM, SSD_STATE
    return ([r.at[:, pl.ds(W * gi, W)] for r in wide] + [r.at[:, pl.ds(N * gi, N)] for r in narrow]
            + [r.at[gi] for r in lead])


def _tri(Q):
    tri = jnp.tril(jnp.ones((Q, Q), F32))
    return tri, tri.T


def _ssd_fwd(xc, dtr8, bias8, alog8, dsk8):
    T, Q = xc.shape[0], SSD_Q
    NC = T // Q
    P, W = SSD_HEADDIM, SSD_HPG * SSD_HEADDIM
    x_spec, b_spec, c_spec, dt_spec, p_spec, tri_spec, h_spec = _ssd_specs(T, Q, False)

    def body(x_ref, b_ref, c_ref, dtr_ref, bias_ref, alog_ref, dsk_ref, tri_ref, triu_ref, y_ref, hs_ref, h_scr):
        for gi in range(SSD_STEP_GROUPS):
            xv, yv, bv, cv, dtv, biasv, alogv, dskv, hsv, hv = _ssd_views(
                gi, (x_ref, y_ref), (b_ref, c_ref), (dtr_ref, bias_ref, alog_ref, dsk_ref, hs_ref, h_scr))
            one(xv, bv, cv, dtv, biasv, alogv, dskv, tri_ref, triu_ref, yv, hsv, hv)

    def one(x_ref, b_ref, c_ref, dtr_ref, bias_ref, alog_ref, dsk_ref, tri_ref, triu_ref, y_ref, hs_ref, h_scr):
        @pl.when(pl.program_id(1) == 0)
        def _():
            h_scr[...] = jnp.zeros_like(h_scr)

        hin = h_scr[...]
        hs_ref[...] = hin
        x, Bm, Cm, pre, dt, A, ac4, ar4, cb, causal, atot4 = _ssd_common(
            x_ref, b_ref, c_ref, dtr_ref, bias_ref, alog_ref, tri_ref, triu_ref)
        xdt = x * _per_head(dt, W)
        rest = _dot(Cm, hin, NT) * _per_head(jnp.exp(ac4), W) + _per_head(dsk_ref[...], W) * x
        for k in range(SSD_HPG):
            blk = slice(P * k, P * (k + 1))
            G = cb * _ssd_decay(k, ac4, ar4, causal)
            y_ref[:, blk] = _dot(G, xdt[:, blk], NN) + rest[:, blk]
        xw = xdt * _per_head(jnp.exp(atot4 - ac4), W)
        h_scr[...] = _head_rows(jnp.exp(atot4)) * hin + _dot(xw, Bm, TN)

    tri, triu = _tri(Q)
    return pl.pallas_call(
        body, name="ssd_fwd", grid=(SSD_GROUPS // SSD_STEP_GROUPS, NC),
        in_specs=[x_spec, b_spec, c_spec, dt_spec, p_spec, p_spec, p_spec, tri_spec, tri_spec],
        out_specs=[pl.BlockSpec((Q, SSD_STEP_GROUPS * W), lambda g, c: (c, g)), h_spec],
        out_shape=[jax.ShapeDtypeStruct((T, D_INNER), F32),
                   jax.ShapeDtypeStruct((SSD_GROUPS, NC, W, SSD_STATE), F32)],
        scratch_shapes=[pltpu.VMEM((SSD_STEP_GROUPS, W, SSD_STATE), F32)],
        compiler_params=_params(("parallel", "arbitrary")),
    )(xc, xc, xc, dtr8, bias8, alog8, dsk8, tri, triu)


def _ssd_bwd(xc, dtr8, bias8, alog8, dsk8, hs, dy):
    T, Q = xc.shape[0], SSD_Q
    NC = T // Q
    P, W = SSD_HEADDIM, SSD_HPG * SSD_HEADDIM
    x_spec, b_spec, c_spec, dt_spec, p_spec, tri_spec, h_spec = _ssd_specs(T, Q, True)
    dy_spec = pl.BlockSpec((Q, SSD_STEP_GROUPS * W), lambda g, c: (NC - 1 - c, g))
    dbc_spec = pl.BlockSpec((Q, SSD_STEP_GROUPS * SSD_STATE), lambda g, c: (NC - 1 - c, g))

    def body(x_ref, b_ref, c_ref, dtr_ref, bias_ref, alog_ref, dsk_ref, tri_ref, triu_ref, hs_ref, dy_ref,
             dx_ref, db_ref, dc_ref, ddtr_ref, dbias_ref, dalog_ref, ddsk_ref, dh_scr):
        for gi in range(SSD_STEP_GROUPS):
            (xv, dyv, dxv, bv, cv, dbv, dcv, dtv, biasv, alogv, dskv, hsv, ddtv, dbiasv, dalogv, ddskv, dhv) = _ssd_views(
                gi, (x_ref, dy_ref, dx_ref), (b_ref, c_ref, db_ref, dc_ref),
                (dtr_ref, bias_ref, alog_ref, dsk_ref, hs_ref, ddtr_ref, dbias_ref, dalog_ref, ddsk_ref, dh_scr))
            one(xv, bv, cv, dtv, biasv, alogv, dskv, tri_ref, triu_ref, hsv, dyv,
                dxv, dbv, dcv, ddtv, dbiasv, dalogv, ddskv, dhv)

    def one(x_ref, b_ref, c_ref, dtr_ref, bias_ref, alog_ref, dsk_ref, tri_ref, triu_ref, hs_ref, dy_ref,
            dx_ref, db_ref, dc_ref, ddtr_ref, dbias_ref, dalog_ref, ddsk_ref, dh_scr):
        @pl.when(pl.program_id(1) == 0)
        def _():
            dh_scr[...] = jnp.zeros_like(dh_scr)
            dbias_ref[...] = jnp.zeros_like(dbias_ref)
            dalog_ref[...] = jnp.zeros_like(dalog_ref)
            ddsk_ref[...] = jnp.zeros_like(ddsk_ref)

        x, Bm, Cm, pre, dt, A, ac4, ar4, cb, causal, atot4 = _ssd_common(
            x_ref, b_ref, c_ref, dtr_ref, bias_ref, alog_ref, tri_ref, triu_ref)
        dyv, hin, dho = dy_ref[...], hs_ref[...], dh_scr[...]
        dt_w = _per_head(dt, W)
        xdt = x * dt_w
        E4, F4, etot4 = jnp.exp(ac4), jnp.exp(atot4 - ac4), jnp.exp(atot4)
        F_w = _per_head(F4, W)
        ddsk_ref[...] += _head_sums(_colsum(dyv * x))
        Z = _dot(Cm, hin, NT)
        dZ = dyv * _per_head(E4, W)
        dac4 = _head_sums(dyv * Z) * E4
        dC = _dot(dZ, hin, NN)
        dh_scr[...] = _dot(dZ, Cm, TN) + _head_rows(etot4) * dho
        per_row = jnp.sum(dho * hin, axis=1, keepdims=True)
        lane4 = lax.broadcasted_iota(jnp.int32, (1, SSD_HPG), 1)
        datot4 = jnp.zeros((1, SSD_HPG), F32)
        for k in range(SSD_HPG):
            datot4 = jnp.where(lane4 == k, jnp.sum(per_row[P * k:P * (k + 1)], keepdims=True), datot4)
        datot4 = datot4 * etot4
        dxw = _dot(Bm, dho, NT)
        dB = _dot(xdt * F_w, dho, NN)
        dFa4 = _head_sums(dxw * xdt) * F4
        datot4 = datot4 + _colsum(dFa4)
        dac4 = dac4 - dFa4
        dcb = jnp.zeros((Q, Q), F32)
        lane_q4 = lax.broadcasted_iota(jnp.int32, (Q, SSD_HPG), 1)
        sub_4q = lax.broadcasted_iota(jnp.int32, (SSD_HPG, Q), 0)
        row_sums = jnp.zeros((Q, SSD_HPG), F32)
        col_sums = jnp.zeros((SSD_HPG, Q), F32)
        dxdt_heads = []
        for k in range(SSD_HPG):
            blk = slice(P * k, P * (k + 1))
            L = _ssd_decay(k, ac4, ar4, causal)
            G = cb * L
            dG = _dot(dyv[:, blk], xdt[:, blk], NT)
            dxdt_heads.append(_dot(G, dyv[:, blk], TN))
            dcb = dcb + dG * L
            Mseg = dG * G
            row_sums = jnp.where(lane_q4 == k, jnp.sum(Mseg, axis=1, keepdims=True), row_sums)
            col_sums = jnp.where(sub_4q == k, jnp.sum(Mseg, axis=0, keepdims=True), col_sums)
        dxdt = dxw * F_w + jnp.concatenate(dxdt_heads, axis=1)
        last = lax.broadcasted_iota(jnp.int32, (Q, SSD_HPG), 0) == Q - 1
        dac4 = dac4 + row_sums + jnp.where(last, datot4, 0.0)
        dadt4 = _dot(triu_ref[...], dac4, NN, HI) - _dot(triu_ref[...], col_sums, NT, HI)
        ddt4 = _head_sums(dxdt * x) + dadt4 * A
        dalog_ref[...] += _colsum(dadt4 * dt) * A
        ddtr4 = ddt4 * jax.nn.sigmoid(pre)
        dbias_ref[...] += _colsum(ddtr4)
        ddtr_ref[...] = ddtr4
        dx_ref[...] = (_per_head(dsk_ref[...], W) * dyv + dxdt * dt_w).astype(dx_ref.dtype)
        db_ref[...] = dB + _dot(dcb, Cm, TN)
        dc_ref[...] = dC + _dot(dcb, Bm, NN)

    tri, triu = _tri(Q)
    return pl.pallas_call(
        body, name="ssd_bwd", grid=(SSD_GROUPS // SSD_STEP_GROUPS, NC),
        in_specs=[x_spec, b_spec, c_spec, dt_spec, p_spec, p_spec, p_spec, tri_spec, tri_spec, h_spec, dy_spec],
        out_specs=[dy_spec, dbc_spec, dbc_spec, dt_spec, p_spec, p_spec, p_spec],
        out_shape=[jax.ShapeDtypeStruct((T, D_INNER), F32),
                   jax.ShapeDtypeStruct((T, SSD_GROUPS * SSD_STATE), F32),
                   jax.ShapeDtypeStruct((T, SSD_GROUPS * SSD_STATE), F32),
                   jax.ShapeDtypeStruct(dtr8.shape, F32)] + [jax.ShapeDtypeStruct(bias8.shape, F32)] * 3,
        scratch_shapes=[pltpu.VMEM((SSD_STEP_GROUPS, W, SSD_STATE), F32)],
        compiler_params=_params(("parallel", "arbitrary")),
    )(xc, xc, xc, dtr8, bias8, alog8, dsk8, tri, triu, hs, dy)


def _to_groups(v):
    return v.reshape(SSD_GROUPS, 1, SSD_HPG)


def _mixer_fwd(x, p, w):
    T, D = x.shape
    (hn,) = _rowwise(lambda xt, g: (_rms(xt, g),), [x], [p["mix_norm"]], [(D, BF16)], name="mix_norm")
    winT = w["w_in"]
    u, z, xbc, dtr, gl = [
        _matmul(hn, winT[IN_OFFS[i]:IN_OFFS[i + 1]], "nt", name=f"mix_in{i}") for i in range(5)]
    tiles = _s5_derive(*[p[k][0] for k in ("s5_A_re", "s5_A_im", "s5_log_dt", "s5_B_re", "s5_B_im", "s5_C_re", "s5_C_im")])
    u = _to_seg(u)
    ys = _s5_fwd(u, tiles)
    (g,) = _rowwise(lambda yt, ut, d: (jax.nn.gelu(yt + d * ut),), [ys, u], [p["s5_D"]], [(D, F32)], name="s5_gelu")
    y5 = _matmul(g, w["s5_w_glu"], "nn", name="s5_glu", out=(BF16,),
                 epilogue=lambda acc, gt, b: (gt * jax.nn.sigmoid(acc + b),), extras=[g, p["s5_b_glu"]])
    xc = _conv_fwd(xbc, w["conv_w"], p["conv_b"])
    dtr8 = dtr.reshape(T, SSD_GROUPS, SSD_HPG).transpose(1, 0, 2)
    ssd_p = [_to_groups(p[k]) for k in ("ssd_dt_bias", "ssd_A_log", "ssd_D")]
    yssd_raw, hs = _ssd_fwd(xc, dtr8, *ssd_p)
    (yssd,) = _rowwise(lambda yt, zt, nw: (_rms(yt * jax.nn.silu(zt), nw),), [yssd_raw, z], [p["ssd_norm"]],
                       [(D_INNER, BF16)], name="ssd_gate")
    p5 = _from_seg(_matmul(y5, w["w_proj_s5"], "nn", name="mix_p5"))
    pssd = _matmul(yssd, w["w_proj_ssd"], "nn", name="mix_pssd")

    def merge(glt, at, bt, bg):
        gates = jax.nn.sigmoid(glt + bg)
        return (gates[:, :D] * at + gates[:, D:] * bt,)

    (merged,) = _rowwise(merge, [gl, p5, pssd], [p["b_gate"]], [(D, BF16)], name="mix_merge")
    y = _matmul(merged, w["w_out"], "nn", name="mix_out", epilogue=lambda acc, xt: (xt + acc,), extras=[x])
    res = dict(hn=hn, u=u, z=z, xbc=xbc, gl=gl, tiles=tiles, ys=ys, g=g, y5=y5, xc=xc, dtr8=dtr8, ssd_p=ssd_p,
               yssd_raw=yssd_raw, hs=hs, yssd=yssd, p5=p5, pssd=pssd, merged=merged)
    return y, res


def _mixer_bwd(dy, x, p, w, r):
    T, D = x.shape
    gp = {}
    dmerged = _matmul(dy, w["w_out"], "nt", name="mix_d_merged")
    gp["w_out"] = _matmul(r["merged"], dy, "tn", name="mix_d_wout")

    def merge_bwd(glt, at, bt, dm, bg):
        def f(q, a_, b_):
            gates = jax.nn.sigmoid(q + bg)
            return gates[:, :D] * a_ + gates[:, D:] * b_
        _, vjp = jax.vjp(f, glt, at, bt)
        dq, da_, db_ = vjp(dm)
        return da_, db_, dq, _colsum(dq)

    dp5, dpssd, dgl, gp["b_gate"] = _rowwise(
        merge_bwd, [r["gl"], r["p5"], r["pssd"], dmerged], [p["b_gate"]],
        [(D, BF16), (D, BF16), (2 * D, BF16)], [((1, 2 * D), F32)], name="mix_d_merge")
    dyssd = _matmul(dpssd, w["w_proj_ssd"], "nt", name="mix_d_yssd")
    gp["w_proj_ssd"] = _matmul(r["yssd"], dpssd, "tn", name="mix_d_wpssd")

    def gate_bwd(yt, zt, dyt, nw):
        _, vjp = jax.vjp(lambda a_, b_, c_: _rms(a_ * jax.nn.silu(b_), c_), yt, zt, nw)
        return vjp(dyt)

    dyraw, dz, gp["ssd_norm"] = _rowwise(
        gate_bwd, [r["yssd_raw"], r["z"], dyssd], [p["ssd_norm"]],
        [(D_INNER, F32), (D_INNER, BF16)], [((1, D_INNER), F32)], name="ssd_d_gate")
    dxs, dBm, dCm, ddtr8, dbias8, dalog8, ddsk8 = _ssd_bwd(r["xc"], r["dtr8"], *r["ssd_p"], r["hs"], dyraw)
    gp["ssd_dt_bias"], gp["ssd_A_log"], gp["ssd_D"] = [v.reshape(1, SSD_HEADS) for v in (dbias8, dalog8, ddsk8)]
    ddtr = ddtr8.transpose(1, 0, 2).reshape(T, SSD_HEADS)
    conv = [_conv_bwd(r["xbc"], w["conv_w"], p["conv_b"], d, c0, f"conv_bwd{i}")
            for i, (d, c0) in enumerate(((dxs, 0), (dBm, D_INNER), (dCm, D_INNER + SSD_GROUPS * SSD_STATE)))]
    dxbc = [c[0] for c in conv]
    gp["conv_w"] = jnp.concatenate([c[1] for c in conv], axis=1)
    gp["conv_b"] = jnp.concatenate([c[2] for c in conv], axis=1)

    (dp5,) = _rowwise(lambda t: (t,), [dp5], [], [(D, BF16)], name="mix_d_p5", after=(dxs,))
    dp5 = _to_seg(dp5)
    dy5 = _matmul(dp5, w["w_proj_s5"], "nt", name="mix_d_y5")
    gp["w_proj_s5"] = _matmul(r["y5"], dp5, "tn", name="mix_d_wp5")
    g = r["g"]

    def glu_ep(acc, gt, b, dyt):
        _, vjp = jax.vjp(lambda g_, t_: g_ * jax.nn.sigmoid(t_ + b), gt, acc)
        return vjp(dyt)

    dg1, dt_ = _matmul(g, w["s5_w_glu"], "nn", name="s5_d_glu", out=(F32, BF16), epilogue=glu_ep,
                       extras=[g, p["s5_b_glu"], dy5])
    gp["s5_w_glu"] = _matmul(g, dt_, "tn", name="s5_d_wglu")
    dg = _matmul(dt_, w["s5_w_glu"], "nt", name="s5_d_g", epilogue=lambda acc, e: (acc + e,), extras=[dg1])

    def gelu_bwd(yt, ut, dgt, dtt, d):
        _, vjp = jax.vjp(lambda y_, d_: jax.nn.gelu(y_ + d_ * ut), yt, d)
        dys, dd = vjp(dgt)
        return dys, dys * d, dd, _colsum(dtt.astype(F32))

    dys, dusk, gp["s5_D"], gp["s5_b_glu"] = _rowwise(
        gelu_bwd, [r["ys"], r["u"], dg, dt_], [p["s5_D"]], [(D, F32), (D, F32)], [((1, D), F32), ((1, D), F32)],
        name="s5_d_gelu")
    du, *dtiles = _s5_bwd(r["u"], dys, dusk, r["tiles"])
    du = _from_seg(du)

    pieces = [du, dz, None, ddtr, dgl]
    winT = w["w_in"]
    hn = r["hn"]
    d_rows, dhn = [], None
    cols = [(du, 0, 1024), (dz, 1024, 3072), (dxbc[0], 3072, 5120), (dxbc[1], 5120, 6144), (dxbc[2], 6144, 7168),
            (ddtr, 7168, 7200), (dgl, 7200, 9248)]
    for i, (d, lo, hi) in enumerate(cols):
        d_rows.append(_matmul(d, hn, "tn", name=f"mix_d_win{i}"))
        if dhn is None:
            dhn = _matmul(d, winT[lo:hi], "nn", name=f"mix_d_hn{i}")
        else:
            dhn = _matmul(d, winT[lo:hi], "nn", name=f"mix_d_hn{i}", epilogue=lambda acc, e: (acc + e,), extras=[dhn])
    gp["w_in"] = jnp.concatenate(d_rows, axis=0)
    del pieces

    def norm_bwd(xt, dh, dyt, gn):
        _, vjp = jax.vjp(_rms, xt, gn)
        dx, dgn = vjp(dh)
        return dyt + dx, dgn

    dx, gp["mix_norm"] = _rowwise(norm_bwd, [x, dhn, dy], [p["mix_norm"]], [(D, F32)], [((1, D), F32)], name="mix_d_norm")
    return dx, gp, dtiles


def _s5_param_grads(p, dtiles):
    keys = ("s5_A_re", "s5_A_im", "s5_log_dt", "s5_B_re", "s5_B_im", "s5_C_re", "s5_C_im")
    _, vjp = jax.vjp(_s5_derive, *[p[k][0] for k in keys])
    return {k: v[None] for k, v in zip(keys, vjp(tuple(dtiles)))}


def _local_step(x, target, p, w, late_weights, exchange):
    T, D = x.shape
    x1, r1 = _ffn_fwd(x, p["ffn1_norm"], w["ffn1_w_gate"], w["ffn1_w_up"], w["ffn1_w_down"], "ffn1")
    x2, rm = _mixer_fwd(x1, p, w)
    w = {**w, **late_weights(rm["p5"])}
    x3, r2 = _ffn_fwd(x2, p["ffn2_norm"], w["ffn2_w_gate"], w["ffn2_w_up"], w["ffn2_w_down"], "ffn2")

    def head(xt, tt, g):
        def f(x_, g_):
            e = _rms(x_, g_) - tt
            return 0.5 * jnp.sum(jnp.mean(e * e, axis=-1))
        l, (dx_, dg_) = jax.value_and_grad(f, argnums=(0, 1))(xt, g)
        return dx_, l.reshape(1, 1), dg_

    dx3, loss, d_final = _rowwise(head, [x3, target], [p["final_norm"]], [(D, F32)],
                                  [((1, 1), F32), ((1, D), F32)], name="loss_head")
    gp = {"final_norm": d_final}

    def ffn_exchange(tag):
        return lambda d_wgT, d_wuT, d_wd, after: exchange(
            tag, {f"{tag}_w_gate": d_wgT, f"{tag}_w_up": d_wuT, f"{tag}_w_down": d_wd}, after)

    dx2, gp["ffn2_norm"] = _ffn_bwd(
        dx3, x2, p["ffn2_norm"], w["ffn2_w_gate"], w["ffn2_w_up"], w["ffn2_w_down"], r2, "ffn2", (),
        ffn_exchange("ffn2"))
    dx1, gm, dtiles = _mixer_bwd(dx2, x1, p, w, rm)
    conv_w_grad = gm.pop("conv_w")
    started = exchange("mixer", {k: gm.pop(k) for k in ("w_out", "w_proj_s5", "w_proj_ssd", "s5_w_glu", "w_in")}, dx1)
    gp.update(gm)
    gp.update(_s5_param_grads(p, dtiles))
    dx0, gp["ffn1_norm"] = _ffn_bwd(
        dx1, x, p["ffn1_norm"], w["ffn1_w_gate"], w["ffn1_w_up"], w["ffn1_w_down"], r1, "ffn1", started,
        ffn_exchange("ffn1"))
    return loss, dx0, gp, conv_w_grad


ADAMW_BLOCK = 1 << 18


def _adamw(w, m, v, gs, *, name, emit_g):
    _, R, C = w.shape
    n_g = len(gs)
    n_o = 4 if emit_g else 3
    tm = _rtile(R, max(8, ADAMW_BLOCK // C))
    lead = pl.BlockSpec((None, tm, C), lambda i: (0, i, 0))

    def body(w_ref, m_ref, v_ref, *refs):
        g = refs[0][...]
        for r in refs[1:n_g]:
            g = g + r[...]
        m2 = ADAM_B1 * m_ref[...] + (1.0 - ADAM_B1) * g
        v2 = ADAM_B2 * v_ref[...] + (1.0 - ADAM_B2) * (g * g)
        m_hat = m2 / (1.0 - ADAM_B1 ** ADAM_STEP)
        v_hat = v2 / (1.0 - ADAM_B2 ** ADAM_STEP)
        delta = -ADAM_LR * (m_hat / (jnp.sqrt(v_hat) + ADAM_EPS) + ADAM_WD * w_ref[...])
        for r, val in zip(refs[n_g:], ((g,) if emit_g else ()) + (delta, m2, v2)):
            r[...] = val

    return pl.pallas_call(
        body, name=name, grid=(R // tm,),
        in_specs=[lead] * 3 + [pl.BlockSpec((tm, C), lambda i: (i, 0))] * n_g, out_specs=[lead] * n_o,
        out_shape=[jax.ShapeDtypeStruct((1, R, C), F32)] * n_o,
        compiler_params=_params(("parallel",)),
    )(w, m, v, *gs)


def _rtile(r, cap=512):
    if r <= cap:
        return r
    best = None
    for t in range(8, cap + 1, 8):
        if r % t == 0:
            best = t
    assert best is not None, r
    return best


ANY = pl.BlockSpec(memory_space=pl.ANY)


def _where_am_i():
    x, y, c = lax.axis_index("x"), lax.axis_index("y"), lax.axis_index("c")
    ks = (2 * x + y, 2 * (1 - x) + y, 2 * x + (1 - y), 2 * (1 - x) + (1 - y))
    return x, y, c, ks


def _rc(src, dst, ssem, rsem, to):
    return pltpu.make_async_remote_copy(src_ref=src, dst_ref=dst, send_sem=ssem, recv_sem=rsem,
                                        device_id=to, device_id_type=MESH)


def _gather_shards(srcs):
    n = len(srcs)

    def body(*refs):
        _gather_body(refs[:n], refs[n:2 * n], *refs[2 * n:], extra=())

    return pl.pallas_call(
        body, name="gather_shards",
        in_specs=[ANY] * n, out_specs=[ANY] * n,
        out_shape=[jax.ShapeDtypeStruct((4,) + s.shape, s.dtype) for s in srcs],
        scratch_shapes=[pltpu.SemaphoreType.DMA((n, 6)), pltpu.SemaphoreType.DMA((n, 6)), pltpu.SemaphoreType.DMA((n,))],
        compiler_params=pltpu.CompilerParams(has_side_effects=True),
    )(*srcs)


def _handshake_all():
    x, y, c = lax.axis_index("x"), lax.axis_index("y"), lax.axis_index("c")
    barrier = pltpu.get_barrier_semaphore()
    for dx, dy, dc in itertools.product((0, 1), repeat=3):
        if (dx, dy, dc) != (0, 0, 0):
            to = (1 - x if dx else x, 1 - y if dy else y, 1 - c if dc else c)
            pl.semaphore_signal(barrier, inc=1, device_id=to, device_id_type=MESH)
    pl.semaphore_wait(barrier, 7)


SEQUENCER = dict(axis_name="seq", num_cores=1)


def _gather_shards_async(srcs, token, name, collective_id):
    n = len(srcs)

    def body(*refs):
        tok, tok_out = refs[n], refs[2 * n + 1]
        ssem, rsem, lsem = refs[2 * n + 2:]
        _handshake_all()
        _gather_body(refs[:n], refs[n + 1:2 * n + 1], ssem, rsem, lsem,
                     extra=(pltpu.make_async_copy(tok, tok_out, lsem.at[n]),))

    res = pl.kernel(
        body, name=name,
        out_type=[jax.ShapeDtypeStruct((4,) + s.shape, s.dtype) for s in srcs]
        + [jax.ShapeDtypeStruct(token.shape, token.dtype)],
        mesh=plsc.ScalarSubcoreMesh(**SEQUENCER),
        scratch_types=[pltpu.SemaphoreType.DMA((n, 6)), pltpu.SemaphoreType.DMA((n, 6)), pltpu.SemaphoreType.DMA((n + 1,))],
        compiler_params=pltpu.CompilerParams(collective_id=collective_id),
    )(*srcs, token)
    return res[:n], res[n]


def _gather_body(src, out, ssem, rsem, lsem, extra):
        n = len(src)
        x, y, c, (k_me, k_x, k_y, k_d) = _where_am_i()
        sib = (x, y, 1 - c)
        local = [pltpu.make_async_copy(src[i], out[i].at[k_me], lsem.at[i]) for i in range(n)] + list(extra)
        for cp in local:
            cp.start()

        def own(i, q, to):
            return _rc(src[i], out[i].at[k_me], ssem.at[i, q], rsem.at[i, q], to)

        def slab(i, q, k, to):
            return _rc(out[i].at[k], out[i].at[k], ssem.at[i, q], rsem.at[i, q], to)

        @pl.when(c == 1)
        def _():
            sends = [own(i, 0, (1 - x, y, 1)) for i in range(n)]
            for cp in sends:
                cp.start()
            fwd = [slab(i, 3, k_x, sib) for i in range(n)]
            for i in range(n):
                slab(i, 0, k_x, sib).wait_recv()
                fwd[i].start()
            for i in range(n):
                slab(i, 4, k_y, sib).wait_recv()
                slab(i, 5, k_d, sib).wait_recv()
            for cp in sends + fwd:
                cp.wait_send()

        @pl.when(c == 0)
        def _():
            sends = [own(i, 1, (x, 1 - y, 0)) for i in range(n)] + [own(i, 2, (1 - x, 1 - y, 0)) for i in range(n)]
            for cp in sends:
                cp.start()
            fwd = [slab(i, 4, k_y, sib) for i in range(n)] + [slab(i, 5, k_d, sib) for i in range(n)]
            for i in range(n):
                slab(i, 1, k_y, sib).wait_recv()
                fwd[i].start()
            for i in range(n):
                slab(i, 2, k_d, sib).wait_recv()
                fwd[n + i].start()
            for i in range(n):
                slab(i, 3, k_x, sib).wait_recv()
            for cp in sends + fwd:
                cp.wait_send()

        for cp in local:
            cp.wait()


def _swap_slabs(arrs, n_slabs, name, async_id, token):
    n = len(arrs)
    J = max(n_slabs, 1)

    def body(*refs):
        src, out = refs[:n], refs[n:2 * n]
        ssem, rsem = refs[2 * n:]
        x, y, c, (k_me, k_x, k_y, k_d) = _where_am_i()
        sib = (x, y, 1 - c)
        sel = (jnp.where(c == 1, k_y, k_me), jnp.where(c == 1, k_d, k_x))
        cps = []
        for i in range(n):
            for j in range(J):
                s = src[i].at[sel[j]] if n_slabs else src[i]
                d = out[i].at[j] if n_slabs else out[i]
                cps.append(_rc(s, d, ssem.at[i, j], rsem.at[i, j], sib))
        for cp in cps:
            cp.start()
        for cp in cps:
            cp.wait()

    return _exchange_call(
        body, arrs, [jax.ShapeDtypeStruct(((n_slabs,) if n_slabs else ()) + a.shape[-2:], a.dtype) for a in arrs],
        [pltpu.SemaphoreType.DMA((n, J)), pltpu.SemaphoreType.DMA((n, J))], name, async_id, token)


def _exchange_call(body, arrs, out_shape, sems, name, collective_id, token):
    n, m = len(arrs), len(out_shape)

    def seq_body(*refs):
        tok, tok_out, tok_sem = refs[n], refs[n + 1 + m], refs[-1]
        _handshake_all()
        cp = pltpu.make_async_copy(tok, tok_out, tok_sem)
        cp.start()
        body(*refs[:n], *refs[n + 1:n + 1 + m], *refs[n + 2 + m:-1])
        cp.wait()

    res = pl.kernel(
        seq_body, name=name, out_type=list(out_shape) + [jax.ShapeDtypeStruct(token.shape, token.dtype)],
        mesh=plsc.ScalarSubcoreMesh(**SEQUENCER), scratch_types=list(sems) + [pltpu.SemaphoreType.DMA],
        compiler_params=pltpu.CompilerParams(collective_id=collective_id))(*arrs, token)
    return res[:m], res[m]


def _send_chip_sums(arrs, name, async_id, token):
    n = len(arrs)

    def body(*refs):
        src, out = refs[:n], refs[n:2 * n]
        ssem, rsem = refs[2 * n:]
        x, y, c, _ = _where_am_i()

        @pl.when(c == 1)
        def _():
            cps = [_rc(src[i].at[1], out[i].at[0], ssem.at[i, 0], rsem.at[i, 0], (1 - x, y, 1)) for i in range(n)]
            for cp in cps:
                cp.start()
            for cp in cps:
                cp.wait()

        @pl.when(c == 0)
        def _():
            cps = [_rc(src[i].at[0], out[i].at[0], ssem.at[i, 0], rsem.at[i, 0], (x, 1 - y, 0)) for i in range(n)]
            cps += [_rc(src[i].at[1], out[i].at[1], ssem.at[i, 1], rsem.at[i, 1], (1 - x, 1 - y, 0)) for i in range(n)]
            for cp in cps:
                cp.start()
            for cp in cps:
                cp.wait()

    return _exchange_call(
        body, arrs, [jax.ShapeDtypeStruct(a.shape, a.dtype) for a in arrs],
        [pltpu.SemaphoreType.DMA((n, 2)), pltpu.SemaphoreType.DMA((n, 2))], name, async_id, token)


def _chip_sum(g, recv, sel, name):
    _, r, C = g.shape
    tr = _rtile(r)

    def body(sel_ref, g_ref, r_ref, o32_ref, o16_ref):
        s = g_ref[...] + r_ref[...]
        o32_ref[...] = s
        o16_ref[...] = s.astype(BF16)

    blk = pl.BlockSpec((None, tr, C), lambda j, t, sel_ref: (j, t, 0))
    return pl.pallas_call(
        body, name=name,
        grid_spec=pltpu.PrefetchScalarGridSpec(
            num_scalar_prefetch=1, grid=(2, r // tr),
            in_specs=[pl.BlockSpec((None, tr, C), lambda j, t, sel_ref: (sel_ref[j], t, 0)), blk],
            out_specs=[blk, blk]),
        out_shape=[jax.ShapeDtypeStruct((2, r, C), F32), jax.ShapeDtypeStruct((2, r, C), BF16)],
        compiler_params=_params(("arbitrary", "arbitrary")),
    )(sel, g, recv)


def _cross_sum(s32, got, north, after, name):
    _, r, C = s32.shape
    tr = _rtile(r)

    def body(f_ref, p_ref, a_ref, b_ref, after_ref, o_ref):
        a = a_ref[...].astype(F32)

        @pl.when(f_ref[0] == 1)
        def _():
            o_ref[...] = p_ref[...] + a

        @pl.when(f_ref[0] == 0)
        def _():
            o_ref[...] = a + b_ref[...].astype(F32)

    return pl.pallas_call(
        body, name=name,
        grid_spec=pltpu.PrefetchScalarGridSpec(
            num_scalar_prefetch=1, grid=(r // tr,),
            in_specs=[pl.BlockSpec((None, tr, C), lambda t, f: (0, t, 0)),
                      pl.BlockSpec((None, tr, C), lambda t, f: (0, t, 0)),
                      pl.BlockSpec((None, tr, C), lambda t, f: (1 - f[0], t, 0)), ANY],
            out_specs=pl.BlockSpec((tr, C), lambda t, f: (t, 0))),
        out_shape=jax.ShapeDtypeStruct((r, C), F32),
        compiler_params=_params(("arbitrary",)),
    )(north, s32, got, got, after)


class _Reducer:
    def __init__(self, token):
        self.token, self.pending, self.pairs = token, None, {}

    def begin(self, tag, ids, grads, after):
        names = list(grads)
        arrs = [grads[n].reshape(4, -1, D_MODEL) for n in names]
        x, y, c, (k_me, k_x, k_y, k_d) = _where_am_i()
        sel = jnp.stack([jnp.where(c == 1, k_me, k_y), jnp.where(c == 1, k_x, k_d)]).astype(jnp.int32)
        from_sib, self.token = _swap_slabs(arrs, 2, f"swap_grad_slabs_{tag}", ids[0], self.token)
        sums = [_chip_sum(g, r, sel, f"chip_sum_{tag}{i}") for i, (g, r) in enumerate(zip(arrs, from_sib))]
        self.finish(after)
        got, self.token = _send_chip_sums([s16 for _, s16 in sums], f"send_chip_sums_{tag}", ids[1], self.token)
        self.pending = (tag, ids[2], names, sums, got)
        return tuple(s16 for _, s16 in sums)

    def finish(self, after):
        if self.pending is None:
            return
        tag, cid, names, sums, got = self.pending
        north = jnp.reshape(lax.axis_index("c"), (1,)).astype(jnp.int32)
        parts = [_cross_sum(s32, b, north, after, f"cross_sum_{tag}{i}") for i, ((s32, _), b) in enumerate(zip(sums, got))]
        theirs, self.token = _swap_slabs(parts, 0, f"swap_parts_{tag}", cid, self.token)
        self.pairs.update(zip(names, zip(parts, theirs)))
        self.pending = None


def _all_reduce_small(v):
    R, C = v.shape

    def body(v_ref, o_ref, buf, ssem, rsem, lsem):
        x, y, c = lax.axis_index("x"), lax.axis_index("y"), lax.axis_index("c")
        me, sib = (x, y, c), (x, y, 1 - c)
        chips = [(1 - x, y), (x, 1 - y), (1 - x, 1 - y)]

        def slot(px, py, pc):
            return buf.at[4 * px + 2 * py + pc]

        def copy(k, block, to, src=None):
            return _rc(slot(*block) if src is None else src, slot(*block), ssem.at[k], rsem.at[k], to)

        mine = pltpu.make_async_copy(v_ref, slot(*me), lsem)
        mine.start()
        first = [copy(0, me, sib, src=v_ref)] + [copy(1 + j, me, (*chip, c), src=v_ref) for j, chip in enumerate(chips)]
        for cp in first:
            cp.start()
        passed = [copy(4 + j, (*chip, c), sib) for j, chip in enumerate(chips)]
        for j, chip in enumerate(chips):
            copy(1 + j, (*chip, c), me).wait_recv()
            passed[j].start()
        copy(0, sib, me).wait_recv()
        for j, chip in enumerate(chips):
            copy(4 + j, (*chip, 1 - c), me).wait_recv()
        for cp in first + passed:
            cp.wait_send()
        mine.wait()
        acc = buf[0]
        for k in range(1, 8):
            acc = acc + buf[k]
        o_ref[...] = acc

    return pl.pallas_call(
        body, name="all_reduce_small",
        in_specs=[pl.BlockSpec(memory_space=pltpu.VMEM)], out_specs=pl.BlockSpec(memory_space=pltpu.VMEM),
        out_shape=jax.ShapeDtypeStruct((R, C), F32),
        scratch_shapes=[pltpu.VMEM((8, R, C), F32), pltpu.SemaphoreType.DMA((7,)), pltpu.SemaphoreType.DMA((7,)),
                        pltpu.SemaphoreType.DMA],
        compiler_params=pltpu.CompilerParams(has_side_effects=True, vmem_limit_bytes=VMEM_LIMIT),
    )(v)


WEIGHTS = ['ffn1_norm', 'ffn1_w_gate', 'ffn1_w_up', 'ffn1_w_down', 'mix_norm', 'w_in', 'conv_w', 'conv_b', 's5_A_re',
           's5_A_im', 's5_log_dt', 's5_B_re', 's5_B_im', 's5_C_re', 's5_C_im', 's5_D', 's5_w_glu', 's5_b_glu',
           'ssd_A_log', 'ssd_dt_bias', 'ssd_D', 'ssd_norm', 'w_proj_s5', 'w_proj_ssd', 'b_gate', 'w_out', 'ffn2_norm',
           'ffn2_w_gate', 'ffn2_w_up', 'ffn2_w_down', 'final_norm']
ARGS = ['x'] + WEIGHTS + ['loss_target'] + ['m_' + n for n in WEIGHTS] + ['v_' + n for n in WEIGHTS]
COL_SHARDED = ('ffn1_w_gate', 'ffn1_w_up', 'w_in', 'ffn2_w_gate', 'ffn2_w_up')
ROW_SHARDED = ('ffn1_w_down', 's5_w_glu', 'w_proj_s5', 'w_proj_ssd', 'w_out', 'ffn2_w_down')
MATRICES = COL_SHARDED + ROW_SHARDED
GATHER_FIRST = ('ffn1_w_gate', 'ffn1_w_up', 'ffn1_w_down')
GATHER_SECOND = ('w_in', 'conv_w', 's5_w_glu', 'w_proj_s5', 'w_proj_ssd', 'w_out')
GATHER_THIRD = ('ffn2_w_gate', 'ffn2_w_up', 'ffn2_w_down')
SMALL = [n for n in WEIGHTS if n not in MATRICES]


PACK_WIDTH = 1024


def _pack_rows(shape):
    return -(-math.prod(shape) // (8 * PACK_WIDTH)) * 8


def _pack(arrs):
    parts = []
    for a in arrs:
        flat = a.reshape(-1).astype(F32)
        rows = _pack_rows(a.shape)
        parts.append(jnp.pad(flat, (0, rows * PACK_WIDTH - flat.shape[0])).reshape(rows, PACK_WIDTH))
    return jnp.concatenate(parts, axis=0)


def _unpack(packed, shapes):
    out, o = [], 0
    for s in shapes:
        rows = _pack_rows(s)
        out.append(packed[o:o + rows].reshape(-1)[:math.prod(s)].reshape(s))
        o += rows
    return out


def kernel(x, ffn1_norm, ffn1_w_gate, ffn1_w_up, ffn1_w_down, mix_norm, w_in, conv_w, conv_b, s5_A_re, s5_A_im, s5_log_dt, s5_B_re, s5_B_im, s5_C_re, s5_C_im, s5_D, s5_w_glu, s5_b_glu, ssd_A_log, ssd_dt_bias, ssd_D, ssd_norm, w_proj_s5, w_proj_ssd, b_gate, w_out, ffn2_norm, ffn2_w_gate, ffn2_w_up, ffn2_w_down, final_norm, loss_target, m_ffn1_norm, m_ffn1_w_gate, m_ffn1_w_up, m_ffn1_w_down, m_mix_norm, m_w_in, m_conv_w, m_conv_b, m_s5_A_re, m_s5_A_im, m_s5_log_dt, m_s5_B_re, m_s5_B_im, m_s5_C_re, m_s5_C_im, m_s5_D, m_s5_w_glu, m_s5_b_glu, m_ssd_A_log, m_ssd_dt_bias, m_ssd_D, m_ssd_norm, m_w_proj_s5, m_w_proj_ssd, m_b_gate, m_w_out, m_ffn2_norm, m_ffn2_w_gate, m_ffn2_w_up, m_ffn2_w_down, m_final_norm, v_ffn1_norm, v_ffn1_w_gate, v_ffn1_w_up, v_ffn1_w_down, v_mix_norm, v_w_in, v_conv_w, v_conv_b, v_s5_A_re, v_s5_A_im, v_s5_log_dt, v_s5_B_re, v_s5_B_im, v_s5_C_re, v_s5_C_im, v_s5_D, v_s5_w_glu, v_s5_b_glu, v_ssd_A_log, v_ssd_dt_bias, v_ssd_D, v_ssd_norm, v_w_proj_s5, v_w_proj_ssd, v_b_gate, v_w_out, v_ffn2_norm, v_ffn2_w_gate, v_ffn2_w_up, v_ffn2_w_down, v_final_norm):
    a = dict(locals())
    assert list(a) == ARGS
    x, target = a['x'][0], a['loss_target'][0]
    k_me = 2 * lax.axis_index("x") + lax.axis_index("y")

    src = {n: a[n][0].T.astype(BF16) for n in COL_SHARDED}
    src.update({n: a[n][0].astype(BF16) for n in ROW_SHARDED})
    src['conv_w'] = a['conv_w'][0]
    def flat(f):
        return f.reshape(-1, f.shape[-1])

    full = _gather_shards([src[n] for n in GATHER_FIRST])
    w = {n: flat(f) for n, f in zip(GATHER_FIRST, full)}
    got, token = _gather_shards_async([src[n] for n in GATHER_SECOND], full[0][0, :8, :128].astype(F32), "gather_mixer", 1)
    w.update({n: flat(f) for n, f in zip(GATHER_SECOND, got)})
    w['conv_w'] = w['conv_w'].reshape(4, CONV_K, -1).transpose(1, 0, 2).reshape(CONV_K, CONV_DIM)
    reducer = _Reducer(token)

    def gather_last(after):
        got, reducer.token = _gather_shards_async(
            [src[n] for n in GATHER_THIRD], reducer.token + after[:8, :128], "gather_ffn2", 2)
        return {n: flat(f) for n, f in zip(GATHER_THIRD, got)}

    p = {n: a[n] for n in SMALL if n != 'conv_w'}
    p['final_norm'] = a['final_norm'][None]
    ids = {"ffn2": (3, 4, 5), "mixer": (6, 7, 8), "ffn1": (9, 10, 11)}
    loss, grad_x, gp, conv_w_grad = _local_step(
        x, target, p, w, gather_last, lambda tag, grads, after: reducer.begin(tag, ids[tag], grads, after))
    reducer.finish(grad_x)
    pairs = [reducer.pairs[n] for n in MATRICES]
    gp['conv_w'] = conv_w_grad[None]
    small_shapes = [(1, 1)] + [gp[n].shape if n != 'final_norm' else (1, D_MODEL) for n in SMALL]
    red = _unpack(_all_reduce_small(_pack([loss] + [gp[n] for n in SMALL])), small_shapes)
    loss_all = red[0].reshape(())
    gsmall = dict(zip(SMALL, red[1:]))
    gsmall['conv_w'] = lax.dynamic_slice_in_dim(gsmall['conv_w'], k_me * 1024, 1024, axis=2)
    gsmall = {n: g.reshape(a[n].shape) for n, g in gsmall.items()}

    grads, delta, new_m, new_v = {}, {}, {}, {}
    for n, (mine, theirs) in zip(MATRICES, pairs):
        wn, mn, vn = a[n], a['m_' + n], a['v_' + n]
        if n in COL_SHARDED:
            (gt,) = _rowwise(lambda p_, q_: (p_ + q_,), [mine, theirs], [], [(D_MODEL, F32)], name=f"sum_{n}",
                             tm=_rtile(mine.shape[0]))
            g = gt.T
            d, m2, v2 = _adamw(wn, mn, vn, [g], name=f"adamw_{n}", emit_g=False)
            g = g[None]
        else:
            g, d, m2, v2 = _adamw(wn, mn, vn, [mine, theirs], name=f"adamw_{n}", emit_g=True)
        grads[n], delta[n], new_m[n], new_v[n] = g, d, m2, v2
    sw, sm, sv, sg = [_pack([t[n] for n in SMALL])[None] for t in (a, {n: a['m_' + n] for n in SMALL},
                                                                   {n: a['v_' + n] for n in SMALL}, gsmall)]
    d, m2, v2 = _adamw(sw, sm, sv, [sg[0]], name="adamw_small", emit_g=False)
    shapes = [a[n].shape for n in SMALL]
    for n, dd, mm, vv in zip(SMALL, _unpack(d[0], shapes), _unpack(m2[0], shapes), _unpack(v2[0], shapes)):
        grads[n], delta[n], new_m[n], new_v[n] = gsmall[n], dd, mm, vv
    return (loss_all, grad_x[None], *[grads[n] for n in WEIGHTS], *[delta[n] for n in WEIGHTS],
            *[new_m[n] for n in WEIGHTS], *[new_v[n] for n in WEIGHTS])
```

```python
import itertools
import math

import jax
import jax.numpy as jnp
from jax import lax
from jax.experimental import pallas as pl
from jax.experimental.pallas import tpu as pltpu
from jax.experimental.pallas import tpu_sc as plsc

F32 = jnp.float32
BF16 = jnp.bfloat16
MXU = BF16
HI = lax.Precision.HIGHEST

D_MODEL = 1024
D_FF = 2816
EPS = 1e-6
S5_GROUPS, S5_GROUP, S5_STATE = 64, 16, 64
D_INNER = 2048
SSD_HEADDIM, SSD_HEADS, SSD_GROUPS, SSD_HPG, SSD_STATE = 64, 32, 8, 4, 128
CONV_K, CONV_DIM = 4, 4096
IN_SPLITS = (1024, 2048, 4096, 32, 2048)
IN_OFFS = (0, 1024, 3072, 7168, 7200, 9248)
SSD_Q = 256
SSD_STEP_GROUPS = 2
ADAM_LR, ADAM_B1, ADAM_B2, ADAM_EPS, ADAM_WD, ADAM_STEP = 0.001, 0.9, 0.999, 1e-08, 0.01, 10

VMEM_LIMIT = 56 * 1024 * 1024
MATMUL_VMEM = 32 * 1024 * 1024
MESH = pl.DeviceIdType.MESH

NN = ((1,), (0,))
NT = ((1,), (1,))
TN = ((0,), (0,))


def _dot(a, b, dims, precision=None):
    if precision is None:
        a, b = a.astype(MXU), b.astype(MXU)
    return lax.dot_general(a, b, (dims, ((), ())), precision=precision, preferred_element_type=F32)


def _tile(n, pref):
    if n <= pref:
        return n
    best = None
    for t in range(128, pref + 1, 128):
        if n % t == 0:
            best = t
    assert best is not None, (n, pref)
    return best


def _params(sem):
    return pltpu.CompilerParams(dimension_semantics=sem, vmem_limit_bytes=VMEM_LIMIT)


def _matmul(a, b, mode, *, name, out=((F32),), epilogue=None, extras=(), after=(), tm=512, tn=1536):
    if mode == "nn":
        (M, K), (_, N) = a.shape, b.shape
    elif mode == "nt":
        (M, K), (N, _) = a.shape, b.shape
    else:
        (K, M), (_, N) = a.shape, b.shape
    tm, tn = _tile(M, tm), _tile(N, tn)

    def vmem(tm_, tn_):
        per = K * tm_ * a.dtype.itemsize + K * tn_ * b.dtype.itemsize
        per += sum(tm_ * tn_ * jnp.dtype(dt).itemsize for dt in out)
        per += sum((1 if e.shape[0] == 1 else tm_) * tn_ * e.dtype.itemsize for e in extras)
        return 2 * per

    while vmem(tm, tn) > MATMUL_VMEM and (tn > 128 or tm > 128):
        if tn >= tm and tn > 128:
            tn = _tile(N, tn - 128)
        else:
            tm = _tile(M, tm - 128)
    a_spec = pl.BlockSpec((K, tm), lambda i, j: (0, i)) if mode == "tn" else pl.BlockSpec((tm, K), lambda i, j: (i, 0))
    b_spec = pl.BlockSpec((tn, K), lambda i, j: (j, 0)) if mode == "nt" else pl.BlockSpec((K, tn), lambda i, j: (0, j))
    dims = {"nn": NN, "nt": NT, "tn": TN}[mode]
    e_specs = []
    for e in extras:
        if e.shape[0] == 1:
            e_specs.append(pl.BlockSpec((1, tn), lambda i, j: (0, j)))
        else:
            assert e.shape == (M, N), (e.shape, M, N)
            e_specs.append(pl.BlockSpec((tm, tn), lambda i, j: (i, j)))
    n_e, n_o, n_a = len(extras), len(out), len(after)

    def body(a_ref, b_ref, *refs):
        acc = _dot(a_ref[...], b_ref[...], dims)
        res = (acc,) if epilogue is None else epilogue(acc, *[r[...] for r in refs[:n_e]])
        for r, v in zip(refs[n_e + n_a:], res):
            r[...] = v.astype(r.dtype)

    res = pl.pallas_call(
        body, name=name, grid=(M // tm, N // tn),
        in_specs=[a_spec, b_spec] + e_specs + [ANY] * n_a,
        out_specs=[pl.BlockSpec((tm, tn), lambda i, j: (i, j)) for _ in range(n_o)],
        out_shape=[jax.ShapeDtypeStruct((M, N), dt) for dt in out],
        compiler_params=_params(("parallel", "parallel")),
    )(a, b, *extras, *after)
    return res[0] if n_o == 1 else res


def _rowwise(fn, rows, fulls, outs, reds=(), *, name, after=(), tm=256):
    T = rows[0].shape[0]
    tm = min(tm, T)
    n_r, n_f, n_o, n_d, n_a = len(rows), len(fulls), len(outs), len(reds), len(after)

    def body(*refs):
        ins = [r[...] for r in refs[:n_r + n_f]]
        o_refs = refs[n_r + n_f + n_a:n_r + n_f + n_a + n_o]
        d_refs = refs[n_r + n_f + n_a + n_o:]
        res = fn(*ins)
        for r, v in zip(o_refs, res[:n_o]):
            r[...] = v.astype(r.dtype)
        if n_d:
            @pl.when(pl.program_id(0) == 0)
            def _():
                for r in d_refs:
                    r[...] = jnp.zeros_like(r)
            for r, v in zip(d_refs, res[n_o:]):
                r[...] += v.astype(r.dtype)

    res = pl.pallas_call(
        body, name=name, grid=(T // tm,),
        in_specs=[pl.BlockSpec((tm, r.shape[1]), lambda i: (i, 0)) for r in rows]
        + [pl.BlockSpec(f.shape, lambda i, nd=f.ndim: (0,) * nd) for f in fulls] + [ANY] * n_a,
        out_specs=[pl.BlockSpec((tm, c), lambda i: (i, 0)) for c, _ in outs]
        + [pl.BlockSpec(s, lambda i, nd=len(s): (0,) * nd) for s, _ in reds],
        out_shape=[jax.ShapeDtypeStruct((T, c), dt) for c, dt in outs]
        + [jax.ShapeDtypeStruct(s, dt) for s, dt in reds],
        compiler_params=_params(("arbitrary",)),
    )(*rows, *fulls, *after)
    return res


def _rms(x, g):
    return x * lax.rsqrt(jnp.mean(x * x, axis=-1, keepdims=True) + EPS) * g


def _colsum(v):
    return jnp.sum(v, axis=0, keepdims=True)


def _softplus(x):
    return jnp.maximum(x, 0.0) + jnp.log1p(jnp.exp(-jnp.abs(x)))


def _ffn_fwd(x, norm, wgT, wuT, wd, tag):
    D = x.shape[1]
    (hn,) = _rowwise(lambda xt, g: (_rms(xt, g),), [x], [norm], [(D, BF16)], name=f"{tag}_norm")
    a = _matmul(hn, wgT, "nt", name=f"{tag}_gate")
    b, hid = _matmul(hn, wuT, "nt", name=f"{tag}_up", out=(BF16, BF16),
                     epilogue=lambda acc, at: (acc, jax.nn.silu(at) * acc), extras=[a])
    y = _matmul(hid, wd, "nn", name=f"{tag}_down", epilogue=lambda acc, xt: (xt + 0.5 * acc,), extras=[x])
    return y, (hn, a, b, hid)


def _ffn_bwd(dy, dy16, x, norm, wgT, wuT, wd, res, tag, after, on_grads):
    hn, a, b, hid = res
    D, F = x.shape[1], a.shape[1]

    def act_bwd(acc, at, bt):
        _, vjp = jax.vjp(lambda p, q: jax.nn.silu(p) * q, at, bt.astype(F32))
        return vjp(0.5 * acc)

    da, db = _matmul(dy16, wd, "nt", name=f"{tag}_d_hid", out=(BF16, BF16), epilogue=act_bwd, extras=[a, b])
    d_wd = _matmul(hid, dy16, "tn", name=f"{tag}_d_wd", epilogue=lambda acc: (0.5 * acc,))
    d_wgT = _matmul(da, hn, "tn", name=f"{tag}_d_wg", after=after)
    d_wuT = _matmul(db, hn, "tn", name=f"{tag}_d_wu")
    started = on_grads(d_wgT, d_wuT, d_wd, dy)
    dhn = _matmul(da, wgT, "nn", name=f"{tag}_d_hn1")
    dhn = _matmul(db, wuT, "nn", name=f"{tag}_d_hn2", epilogue=lambda acc, e: (acc + e,), extras=[dhn])

    def norm_bwd(xt, dh, dyt, g):
        _, vjp = jax.vjp(_rms, xt, g)
        dx, dg = vjp(dh)
        return dyt + dx, dyt + dx, dg

    dx, dx16, d_norm = _rowwise(norm_bwd, [x, dhn, dy], [norm], [(D, F32), (D, BF16)], [((1, D), F32)],
                                name=f"{tag}_d_norm", after=started)
    return dx, dx16, d_norm


S5_TILES = 8
S5_HALF = 256


def _s5_derive(A_re, A_im, log_dt, B_re, B_im, C_re, C_im):
    G, N, M = S5_GROUPS, S5_STATE, S5_GROUP
    dt = jnp.exp(log_dt)[:, None]
    mag = jnp.exp(A_re * dt)
    ar, ai = mag * jnp.cos(A_im * dt), mag * jnp.sin(A_im * dt)
    den = A_re * A_re + A_im * A_im
    cr = ((ar - 1.0) * A_re + ai * A_im) / den
    ci = (ai * A_re - (ar - 1.0) * A_im) / den
    bbr = cr[..., None] * B_re - ci[..., None] * B_im
    bbi = cr[..., None] * B_im + ci[..., None] * B_re
    eye = jnp.eye(8, dtype=F32)

    def tile_in(bb):
        t = bb.reshape(S5_TILES, 8, N, M).transpose(0, 1, 3, 2)
        return jnp.einsum("jamn,ab->jambn", t, eye).reshape(S5_TILES, 8 * M, 8 * N)

    def tile_out(c):
        t = c.reshape(S5_TILES, 8, M, N).transpose(0, 1, 3, 2)
        return jnp.einsum("janm,ab->janbm", t, eye).reshape(S5_TILES, 8 * N, 8 * M)

    return (tile_in(bbr), tile_in(bbi), tile_out(C_re), tile_out(C_im),
            ar.reshape(S5_TILES, 1, 8 * N), ai.reshape(S5_TILES, 1, 8 * N))


S5_NB = S5_HALF // 128
S5_SEG = 32


def _cmul(ar, ai, br, bi):
    return ar * br - ai * bi, ar * bi + ai * br


def _s5_scan(sr_ref, si_ref, ar, ai, T, reverse):
    L, V = T // S5_SEG, S5_SEG // 8
    assert L * S5_SEG == T and L & (L - 1) == 0, T
    sg = -1.0 if reverse else 1.0
    a_r = [jnp.broadcast_to(ar[:, 128 * b:128 * (b + 1)], (8, 128)) for b in range(S5_NB)]
    a_i = [jnp.broadcast_to(sg * ai[:, 128 * b:128 * (b + 1)], (8, 128)) for b in range(S5_NB)]

    def rows(k, v):
        return pl.ds(pl.multiple_of(((L - 1 - k) if reverse else k) * S5_SEG + 8 * v, 8), 8)

    def local(k, carry):
        out = []
        for b in range(S5_NB):
            for v in range(V):
                idx = rows(k, v)
                mr, mi = _cmul(a_r[b], a_i[b], *carry[b * V + v])
                nr, ni = mr + sr_ref[b, idx, :], mi + si_ref[b, idx, :]
                sr_ref[b, idx, :] = nr
                si_ref[b, idx, :] = ni
                out.append((nr, ni))
        return tuple(out)

    z = jnp.zeros((8, 128), F32)
    ends = lax.fori_loop(0, L, local, tuple((z, z) for _ in range(S5_NB * V)))

    carries = []
    for b in range(S5_NB):
        pr, pi = a_r[b][0:1], a_i[b][0:1]
        n = L
        while n > 1:
            pr, pi = _cmul(pr, pi, pr, pi)
            n //= 2
        cr, ci = jnp.zeros((1, 128), F32), jnp.zeros((1, 128), F32)
        into = [None] * S5_SEG
        for j in (reversed(range(S5_SEG)) if reverse else range(S5_SEG)):
            into[j] = (cr, ci)
            er, ei = ends[b * V + j // 8]
            mr, mi = _cmul(pr, pi, cr, ci)
            cr, ci = mr + er[j % 8:j % 8 + 1], mi + ei[j % 8:j % 8 + 1]
        carries.append([(jnp.concatenate([into[8 * v + s][0] for s in range(8)], axis=0),
                         jnp.concatenate([into[8 * v + s][1] for s in range(8)], axis=0)) for v in range(V)])

    def fix(k, powers):
        out = []
        for b in range(S5_NB):
            pr, pi = powers[b]
            for v in range(V):
                idx = rows(k, v)
                dr, di = _cmul(pr, pi, *carries[b][v])
                sr_ref[b, idx, :] += dr
                si_ref[b, idx, :] += di
            out.append(_cmul(pr, pi, a_r[b], a_i[b]))
        return tuple(out)

    lax.fori_loop(0, L, fix, tuple((a_r[b], a_i[b]) for b in range(S5_NB)))


def _s5_in(ut, b_ref, s_ref):
    for b in range(S5_NB):
        s_ref[b] = _dot(ut, b_ref[:, 128 * b:128 * (b + 1)], NN)


def _to_seg(v):
    T, C = v.shape
    return v.reshape(S5_SEG, T // S5_SEG, C).transpose(1, 0, 2).reshape(T, C)


def _from_seg(v):
    T, C = v.shape
    return v.reshape(T // S5_SEG, S5_SEG, C).transpose(1, 0, 2).reshape(T, C)


def _s5_specs(T):
    u_spec = pl.BlockSpec((T, 128), lambda j, h: (0, j))
    b_spec = pl.BlockSpec((None, 128, S5_HALF), lambda j, h: (j, 0, h))
    c_spec = pl.BlockSpec((None, S5_HALF, 128), lambda j, h: (j, h, 0))
    a_spec = pl.BlockSpec((None, 1, S5_HALF), lambda j, h: (j, 0, h))
    return u_spec, b_spec, c_spec, a_spec


def _s5_fwd(u, tiles):
    T = u.shape[0]
    u_spec, b_spec, c_spec, a_spec = _s5_specs(T)

    def body(u_ref, br_ref, bi_ref, cr_ref, ci_ref, ar_ref, ai_ref, y_ref, sr_ref, si_ref):
        ut = u_ref[...].astype(MXU)
        _s5_in(ut, br_ref, sr_ref)
        _s5_in(ut, bi_ref, si_ref)
        _s5_scan(sr_ref, si_ref, ar_ref[...], ai_ref[...], T, False)
        y = None
        for b in range(S5_NB):
            blk = slice(128 * b, 128 * (b + 1))
            yb = _dot(sr_ref[b], cr_ref[blk, :], NN) - _dot(si_ref[b], ci_ref[blk, :], NN)
            y = yb if y is None else y + yb

        @pl.when(pl.program_id(1) == 0)
        def _():
            y_ref[...] = y

        @pl.when(pl.program_id(1) == 1)
        def _():
            y_ref[...] += y

    scr = pltpu.VMEM((S5_NB, T, 128), F32)
    return pl.pallas_call(
        body, name="s5_fwd", grid=(S5_TILES, 2),
        in_specs=[u_spec, b_spec, b_spec, c_spec, c_spec, a_spec, a_spec],
        out_specs=u_spec, out_shape=jax.ShapeDtypeStruct(u.shape, F32),
        scratch_shapes=[scr, scr],
        compiler_params=_params(("parallel", "arbitrary")),
    )(u, *tiles)


def _s5_bwd(u, dys, du_skip, tiles):
    T = u.shape[0]
    u_spec, b_spec, c_spec, a_spec = _s5_specs(T)

    def body(u_ref, dy_ref, sk_ref, br_ref, bi_ref, cr_ref, ci_ref, ar_ref, ai_ref,
             du_ref, dbr_ref, dbi_ref, dcr_ref, dci_ref, dar_ref, dai_ref, sr_ref, si_ref, lr_ref, li_ref):
        ut, dy = u_ref[...].astype(MXU), dy_ref[...].astype(MXU)
        ar, ai = ar_ref[...], ai_ref[...]
        _s5_in(ut, br_ref, sr_ref)
        _s5_in(ut, bi_ref, si_ref)
        _s5_scan(sr_ref, si_ref, ar, ai, T, False)
        for b in range(S5_NB):
            blk = slice(128 * b, 128 * (b + 1))
            dcr_ref[blk, :] = _dot(sr_ref[b], dy, TN)
            dci_ref[blk, :] = -_dot(si_ref[b], dy, TN)
            lr_ref[b] = _dot(dy, cr_ref[blk, :], NT)
            li_ref[b] = -_dot(dy, ci_ref[blk, :], NT)
        _s5_scan(lr_ref, li_ref, ar, ai, T, True)
        du = None
        head = lax.broadcasted_iota(jnp.int32, (T, 128), 0) < S5_SEG
        seg0 = lax.broadcasted_iota(jnp.int32, (S5_SEG, 128), 0) == 0
        for b in range(S5_NB):
            blk = slice(128 * b, 128 * (b + 1))
            lr, li = lr_ref[b], li_ref[b]
            pr = jnp.where(head, 0.0, pltpu.roll(sr_ref[b], S5_SEG, 0))
            pi = jnp.where(head, 0.0, pltpu.roll(si_ref[b], S5_SEG, 0))
            er = jnp.where(seg0, 0.0, pltpu.roll(sr_ref[b, T - S5_SEG:T, :], 1, 0))
            ei = jnp.where(seg0, 0.0, pltpu.roll(si_ref[b, T - S5_SEG:T, :], 1, 0))
            hr, hi = lr[0:S5_SEG], li[0:S5_SEG]
            dar_ref[:, blk] = _colsum(lr * pr + li * pi) + _colsum(hr * er + hi * ei)
            dai_ref[:, blk] = _colsum(li * pr - lr * pi) + _colsum(hi * er - hr * ei)
            dbr_ref[:, blk] = _dot(ut, lr, TN)
            dbi_ref[:, blk] = _dot(ut, li, TN)
            dub = _dot(lr, br_ref[:, blk], NT) + _dot(li, bi_ref[:, blk], NT)
            du = dub if du is None else du + dub

        @pl.when(pl.program_id(1) == 0)
        def _():
            du_ref[...] = du + sk_ref[...]

        @pl.when(pl.program_id(1) == 1)
        def _():
            du_ref[...] += du

    scr = pltpu.VMEM((S5_NB, T, 128), F32)
    return pl.pallas_call(
        body, name="s5_bwd", grid=(S5_TILES, 2),
        in_specs=[u_spec, u_spec, u_spec, b_spec, b_spec, c_spec, c_spec, a_spec, a_spec],
        out_specs=[u_spec, b_spec, b_spec, c_spec, c_spec, a_spec, a_spec],
        out_shape=[jax.ShapeDtypeStruct(u.shape, F32)] + [jax.ShapeDtypeStruct(t.shape, F32) for t in tiles],
        scratch_shapes=[scr, scr, scr, scr],
        compiler_params=_params(("parallel", "arbitrary")),
    )(u, dys, du_skip, *tiles)


CONV_TILE = 256


def _conv_pre(x, w, b):
    T = x.shape[0]
    row = lax.broadcasted_iota(jnp.int32, x.shape, 0)
    acc = x * w[CONV_K - 1:CONV_K, :] + b
    for lag in range(1, CONV_K):
        acc = acc + jnp.where(row >= lag, pltpu.roll(x, lag, 0), 0.0) * w[CONV_K - 1 - lag:CONV_K - lag, :]
    return acc


def _conv_fwd(x, w, b):
    T, C = x.shape
    col = pl.BlockSpec((T, CONV_TILE), lambda j: (0, j))

    def body(x_ref, w_ref, b_ref, o_ref):
        o_ref[...] = jax.nn.silu(_conv_pre(x_ref[...], w_ref[...], b_ref[...]))

    return pl.pallas_call(
        body, name="conv_fwd", grid=(C // CONV_TILE,),
        in_specs=[col, pl.BlockSpec((CONV_K, CONV_TILE), lambda j: (0, j)), pl.BlockSpec((1, CONV_TILE), lambda j: (0, j))],
        out_specs=col, out_shape=jax.ShapeDtypeStruct((T, C), F32),
        compiler_params=_params(("parallel",)),
    )(x, w, b)


def _conv_bwd(x, w, b, dout, col0, name):
    T, C = dout.shape
    off = col0 // CONV_TILE
    xcol = pl.BlockSpec((T, CONV_TILE), lambda j: (0, j + off))
    dcol = pl.BlockSpec((T, CONV_TILE), lambda j: (0, j))

    def body(x_ref, w_ref, b_ref, d_ref, dx_ref, dw_ref, db_ref):
        x, w = x_ref[...], w_ref[...]
        _, vjp = jax.vjp(jax.nn.silu, _conv_pre(x, w, b_ref[...]))
        (dy,) = vjp(d_ref[...])
        row = lax.broadcasted_iota(jnp.int32, x.shape, 0)
        dx = dy * w[CONV_K - 1:CONV_K, :]
        dw_ref[CONV_K - 1:CONV_K, :] = _colsum(dy * x)
        for lag in range(1, CONV_K):
            ahead = jnp.where(row < T - lag, pltpu.roll(dy, T - lag, 0), 0.0)
            dx = dx + ahead * w[CONV_K - 1 - lag:CONV_K - lag, :]
            dw_ref[CONV_K - 1 - lag:CONV_K - lag, :] = _colsum(ahead * x)
        dx_ref[...] = dx.astype(dx_ref.dtype)
        db_ref[...] = _colsum(dy)

    return pl.pallas_call(
        body, name=name, grid=(C // CONV_TILE,),
        in_specs=[xcol, pl.BlockSpec((CONV_K, CONV_TILE), lambda j: (0, j + off)),
                  pl.BlockSpec((1, CONV_TILE), lambda j: (0, j + off)), dcol],
        out_specs=[dcol, pl.BlockSpec((CONV_K, CONV_TILE), lambda j: (0, j)), pl.BlockSpec((1, CONV_TILE), lambda j: (0, j))],
        out_shape=[jax.ShapeDtypeStruct((T, C), BF16), jax.ShapeDtypeStruct((CONV_K, C), F32), jax.ShapeDtypeStruct((1, C), F32)],
        compiler_params=_params(("parallel",)),
    )(x, w, b, dout)


def _ssd_common(x_ref, b_ref, c_ref, dtr_ref, bias_ref, alog_ref, tri_ref, triu_ref):
    Q = x_ref.shape[0]
    x, Bm, Cm = x_ref[...], b_ref[...], c_ref[...]
    pre = dtr_ref[...] + bias_ref[...]
    dt = _softplus(pre)
    A = -jnp.exp(alog_ref[...])
    adt = dt * A
    ac4 = _dot(tri_ref[...], adt, NN, HI)
    ar4 = _dot(adt, triu_ref[...], TN, HI)
    cb = _dot(Cm, Bm, NT)
    ii = lax.broadcasted_iota(jnp.int32, (Q, Q), 0)
    jj = lax.broadcasted_iota(jnp.int32, (Q, Q), 1)
    atot4 = ac4[Q - 1:Q, :]
    return x, Bm, Cm, pre, dt, A, ac4, ar4, cb, ii >= jj, atot4


def _ssd_decay(k, ac4, ar4, causal):
    seg = ac4[:, k:k + 1] - ar4[k:k + 1, :]
    return jnp.where(causal, jnp.exp(jnp.where(causal, seg, 0.0)), 0.0)


def _per_head(c4, n):
    lane = lax.broadcasted_iota(jnp.int32, (c4.shape[0], n), 1)
    out = jnp.broadcast_to(c4[:, 0:1], (c4.shape[0], n))
    for k in range(1, SSD_HPG):
        out = jnp.where(lane >= SSD_HEADDIM * k, c4[:, k:k + 1], out)
    return out


def _head_sums(v):
    lane = lax.broadcasted_iota(jnp.int32, (v.shape[0], SSD_HPG), 1)
    out = jnp.zeros((v.shape[0], SSD_HPG), F32)
    for k in range(SSD_HPG):
        s = jnp.sum(v[:, SSD_HEADDIM * k:SSD_HEADDIM * (k + 1)], axis=1, keepdims=True)
        out = jnp.where(lane == k, s, out)
    return out


def _head_rows(c4):
    n = SSD_HPG * SSD_HEADDIM
    row = lax.broadcasted_iota(jnp.int32, (n, 1), 0)
    out = jnp.broadcast_to(c4[:, 0:1], (n, 1))
    for k in range(1, SSD_HPG):
        out = jnp.where(row >= SSD_HEADDIM * k, c4[:, k:k + 1], out)
    return out


def _ssd_specs(T, Q, rev):
    NC = T // Q
    cc = (lambda c: NC - 1 - c) if rev else (lambda c: c)
    W, N, S = SSD_HPG * SSD_HEADDIM, SSD_STATE, SSD_STEP_GROUPS
    x_spec = pl.BlockSpec((Q, S * W), lambda g, c: (cc(c), g))
    b_spec = pl.BlockSpec((Q, S * N), lambda g, c: (cc(c), D_INNER // (S * N) + g))
    c_spec = pl.BlockSpec((Q, S * N), lambda g, c: (cc(c), (D_INNER + SSD_GROUPS * N) // (S * N) + g))
    dt_spec = pl.BlockSpec((S, Q, SSD_HPG), lambda g, c: (g, cc(c), 0))
    p_spec = pl.BlockSpec((S, 1, SSD_HPG), lambda g, c: (g, 0, 0))
    tri_spec = pl.BlockSpec((Q, Q), lambda g, c: (0, 0))
    h_spec = pl.BlockSpec((S, None, W, N), lambda g, c: (g, cc(c), 0, 0))
    return x_spec, b_spec, c_spec, dt_spec, p_spec, tri_spec, h_spec


def _ssd_views(gi, wide, narrow, lead):
    W, N = SSD_HPG * SSD_HEADDIM, SSD_STATE
    return ([r.at[:, pl.ds(W * gi, W)] for r in wide] + [r.at[:, pl.ds(N * gi, N)] for r in narrow]
            + [r.at[gi] for r in lead])


def _tri(Q):
    tri = jnp.tril(jnp.ones((Q, Q), F32))
    return tri, tri.T


def _ssd_fwd(xc, dtr8, bias8, alog8, dsk8):
    T, Q = xc.shape[0], SSD_Q
    NC = T // Q
    P, W = SSD_HEADDIM, SSD_HPG * SSD_HEADDIM
    x_spec, b_spec, c_spec, dt_spec, p_spec, tri_spec, h_spec = _ssd_specs(T, Q, False)

    def body(x_ref, b_ref, c_ref, dtr_ref, bias_ref, alog_ref, dsk_ref, tri_ref, triu_ref, y_ref, hs_ref, h_scr):
        for gi in range(SSD_STEP_GROUPS):
            xv, yv, bv, cv, dtv, biasv, alogv, dskv, hsv, hv = _ssd_views(
                gi, (x_ref, y_ref), (b_ref, c_ref), (dtr_ref, bias_ref, alog_ref, dsk_ref, hs_ref, h_scr))
            one(xv, bv, cv, dtv, biasv, alogv, dskv, tri_ref, triu_ref, yv, hsv, hv)

    def one(x_ref, b_ref, c_ref, dtr_ref, bias_ref, alog_ref, dsk_ref, tri_ref, triu_ref, y_ref, hs_ref, h_scr):
        @pl.when(pl.program_id(1) == 0)
        def _():
            h_scr[...] = jnp.zeros_like(h_scr)

        hin = h_scr[...]
        hs_ref[...] = hin
        x, Bm, Cm, pre, dt, A, ac4, ar4, cb, causal, atot4 = _ssd_common(
            x_ref, b_ref, c_ref, dtr_ref, bias_ref, alog_ref, tri_ref, triu_ref)
        xdt = x * _per_head(dt, W)
        rest = _dot(Cm, hin, NT) * _per_head(jnp.exp(ac4), W) + _per_head(dsk_ref[...], W) * x
        for k in range(SSD_HPG):
            blk = slice(P * k, P * (k + 1))
            G = cb * _ssd_decay(k, ac4, ar4, causal)
            y_ref[:, blk] = _dot(G, xdt[:, blk], NN) + rest[:, blk]
        xw = xdt * _per_head(jnp.exp(atot4 - ac4), W)
        h_scr[...] = _head_rows(jnp.exp(atot4)) * hin + _dot(xw, Bm, TN)

    tri, triu = _tri(Q)
    return pl.pallas_call(
        body, name="ssd_fwd", grid=(SSD_GROUPS // SSD_STEP_GROUPS, NC),
        in_specs=[x_spec, b_spec, c_spec, dt_spec, p_spec, p_spec, p_spec, tri_spec, tri_spec],
        out_specs=[pl.BlockSpec((Q, SSD_STEP_GROUPS * W), lambda g, c: (c, g)), h_spec],
        out_shape=[jax.ShapeDtypeStruct((T, D_INNER), F32),
                   jax.ShapeDtypeStruct((SSD_GROUPS, NC, W, SSD_STATE), F32)],
        scratch_shapes=[pltpu.VMEM((SSD_STEP_GROUPS, W, SSD_STATE), F32)],
        compiler_params=_params(("parallel", "arbitrary")),
    )(xc, xc, xc, dtr8, bias8, alog8, dsk8, tri, triu)


def _ssd_bwd(xc, dtr8, bias8, alog8, dsk8, hs, dy):
    T, Q = xc.shape[0], SSD_Q
    NC = T // Q
    P, W = SSD_HEADDIM, SSD_HPG * SSD_HEADDIM
    x_spec, b_spec, c_spec, dt_spec, p_spec, tri_spec, h_spec = _ssd_specs(T, Q, True)
    dy_spec = pl.BlockSpec((Q, SSD_STEP_GROUPS * W), lambda g, c: (NC - 1 - c, g))
    dbc_spec = pl.BlockSpec((Q, SSD_STEP_GROUPS * SSD_STATE), lambda g, c: (NC - 1 - c, g))

    def body(x_ref, b_ref, c_ref, dtr_ref, bias_ref, alog_ref, dsk_ref, tri_ref, triu_ref, hs_ref, dy_ref,
             dx_ref, db_ref, dc_ref, ddtr_ref, dbias_ref, dalog_ref, ddsk_ref, dh_scr):
        for gi in range(SSD_STEP_GROUPS):
            (xv, dyv, dxv, bv, cv, dbv, dcv, dtv, biasv, alogv, dskv, hsv, ddtv, dbiasv, dalogv, ddskv, dhv) = _ssd_views(
                gi, (x_ref, dy_ref, dx_ref), (b_ref, c_ref, db_ref, dc_ref),
                (dtr_ref, bias_ref, alog_ref, dsk_ref, hs_ref, ddtr_ref, dbias_ref, dalog_ref, ddsk_ref, dh_scr))
            one(xv, bv, cv, dtv, biasv, alogv, dskv, tri_ref, triu_ref, hsv, dyv,
                dxv, dbv, dcv, ddtv, dbiasv, dalogv, ddskv, dhv)

    def one(x_ref, b_ref, c_ref, dtr_ref, bias_ref, alog_ref, dsk_ref, tri_ref, triu_ref, hs_ref, dy_ref,
            dx_ref, db_ref, dc_ref, ddtr_ref, dbias_ref, dalog_ref, ddsk_ref, dh_scr):
        @pl.when(pl.program_id(1) == 0)
        def _():
            dh_scr[...] = jnp.zeros_like(dh_scr)
            dbias_ref[...] = jnp.zeros_like(dbias_ref)
            dalog_ref[...] = jnp.zeros_like(dalog_ref)
            ddsk_ref[...] = jnp.zeros_like(ddsk_ref)

        x, Bm, Cm, pre, dt, A, ac4, ar4, cb, causal, atot4 = _ssd_common(
            x_ref, b_ref, c_ref, dtr_ref, bias_ref, alog_ref, tri_ref, triu_ref)
        dyv, hin, dho = dy_ref[...], hs_ref[...], dh_scr[...]
        dt_w = _per_head(dt, W)
        xdt = x * dt_w
        E4, F4, etot4 = jnp.exp(ac4), jnp.exp(atot4 - ac4), jnp.exp(atot4)
        F_w = _per_head(F4, W)
        ddsk_ref[...] += _head_sums(_colsum(dyv * x))
        Z = _dot(Cm, hin, NT)
        dZ = dyv * _per_head(E4, W)
        dac4 = _head_sums(dyv * Z) * E4
        dC = _dot(dZ, hin, NN)
        dh_scr[...] = _dot(dZ, Cm, TN) + _head_rows(etot4) * dho
        per_row = jnp.sum(dho * hin, axis=1, keepdims=True)
        lane4 = lax.broadcasted_iota(jnp.int32, (1, SSD_HPG), 1)
        datot4 = jnp.zeros((1, SSD_HPG), F32)
        for k in range(SSD_HPG):
            datot4 = jnp.where(lane4 == k, jnp.sum(per_row[P * k:P * (k + 1)], keepdims=True), datot4)
        datot4 = datot4 * etot4
        dxw = _dot(Bm, dho, NT)
        dB = _dot(xdt * F_w, dho, NN)
        dFa4 = _head_sums(dxw * xdt) * F4
        datot4 = datot4 + _colsum(dFa4)
        dac4 = dac4 - dFa4
        dcb = jnp.zeros((Q, Q), F32)
        lane_q4 = lax.broadcasted_iota(jnp.int32, (Q, SSD_HPG), 1)
        sub_4q = lax.broadcasted_iota(jnp.int32, (SSD_HPG, Q), 0)
        row_sums = jnp.zeros((Q, SSD_HPG), F32)
        col_sums = jnp.zeros((SSD_HPG, Q), F32)
        dxdt_heads = []
        for k in range(SSD_HPG):
            blk = slice(P * k, P * (k + 1))
            L = _ssd_decay(k, ac4, ar4, causal)
            G = cb * L
            dG = _dot(dyv[:, blk], xdt[:, blk], NT)
            dxdt_heads.append(_dot(G, dyv[:, blk], TN))
            dcb = dcb + dG * L
            Mseg = dG * G
            row_sums = jnp.where(lane_q4 == k, jnp.sum(Mseg, axis=1, keepdims=True), row_sums)
            col_sums = jnp.where(sub_4q == k, jnp.sum(Mseg, axis=0, keepdims=True), col_sums)
        dxdt = dxw * F_w + jnp.concatenate(dxdt_heads, axis=1)
        last = lax.broadcasted_iota(jnp.int32, (Q, SSD_HPG), 0) == Q - 1
        dac4 = dac4 + row_sums + jnp.where(last, datot4, 0.0)
        dadt4 = _dot(triu_ref[...], dac4, NN, HI) - _dot(triu_ref[...], col_sums, NT, HI)
        ddt4 = _head_sums(dxdt * x) + dadt4 * A
        dalog_ref[...] += _colsum(dadt4 * dt) * A
        ddtr4 = ddt4 * jax.nn.sigmoid(pre)
        dbias_ref[...] += _colsum(ddtr4)
        ddtr_ref[...] = ddtr4
        dx_ref[...] = (_per_head(dsk_ref[...], W) * dyv + dxdt * dt_w).astype(dx_ref.dtype)
        db_ref[...] = dB + _dot(dcb, Cm, TN)
        dc_ref[...] = dC + _dot(dcb, Bm, NN)

    tri, triu = _tri(Q)
    return pl.pallas_call(
        body, name="ssd_bwd", grid=(SSD_GROUPS // SSD_STEP_GROUPS, NC),
        in_specs=[x_spec, b_spec, c_spec, dt_spec, p_spec, p_spec, p_spec, tri_spec, tri_spec, h_spec, dy_spec],
        out_specs=[dy_spec, dbc_spec, dbc_spec, dt_spec, p_spec, p_spec, p_spec],
        out_shape=[jax.ShapeDtypeStruct((T, D_INNER), F32),
                   jax.ShapeDtypeStruct((T, SSD_GROUPS * SSD_STATE), F32),
                   jax.ShapeDtypeStruct((T, SSD_GROUPS * SSD_STATE), F32),
                   jax.ShapeDtypeStruct(dtr8.shape, F32)] + [jax.ShapeDtypeStruct(bias8.shape, F32)] * 3,
        scratch_shapes=[pltpu.VMEM((SSD_STEP_GROUPS, W, SSD_STATE), F32)],
        compiler_params=_params(("parallel", "arbitrary")),
    )(xc, xc, xc, dtr8, bias8, alog8, dsk8, tri, triu, hs, dy)


def _to_groups(v):
    return v.reshape(SSD_GROUPS, 1, SSD_HPG)


def _mixer_fwd(x, p, w):
    T, D = x.shape
    (hn,) = _rowwise(lambda xt, g: (_rms(xt, g),), [x], [p["mix_norm"]], [(D, BF16)], name="mix_norm")
    winT = w["w_in"]
    u, z, xbc, dtr, gl = [
        _matmul(hn, winT[IN_OFFS[i]:IN_OFFS[i + 1]], "nt", name=f"mix_in{i}") for i in range(5)]
    tiles = _s5_derive(*[p[k][0] for k in ("s5_A_re", "s5_A_im", "s5_log_dt", "s5_B_re", "s5_B_im", "s5_C_re", "s5_C_im")])
    u = _to_seg(u)
    ys = _s5_fwd(u, tiles)
    (g,) = _rowwise(lambda yt, ut, d: (jax.nn.gelu(yt + d * ut),), [ys, u], [p["s5_D"]], [(D, F32)], name="s5_gelu")
    y5 = _matmul(g, w["s5_w_glu"], "nn", name="s5_glu", out=(BF16,),
                 epilogue=lambda acc, gt, b: (gt * jax.nn.sigmoid(acc + b),), extras=[g, p["s5_b_glu"]])
    xc = _conv_fwd(xbc, w["conv_w"], p["conv_b"])
    dtr8 = dtr.reshape(T, SSD_GROUPS, SSD_HPG).transpose(1, 0, 2)
    ssd_p = [_to_groups(p[k]) for k in ("ssd_dt_bias", "ssd_A_log", "ssd_D")]
    yssd_raw, hs = _ssd_fwd(xc, dtr8, *ssd_p)
    (yssd,) = _rowwise(lambda yt, zt, nw: (_rms(yt * jax.nn.silu(zt), nw),), [yssd_raw, z], [p["ssd_norm"]],
                       [(D_INNER, BF16)], name="ssd_gate")
    p5 = _from_seg(_matmul(y5, w["w_proj_s5"], "nn", name="mix_p5"))
    pssd = _matmul(yssd, w["w_proj_ssd"], "nn", name="mix_pssd")

    def merge(glt, at, bt, bg):
        gates = jax.nn.sigmoid(glt + bg)
        return (gates[:, :D] * at + gates[:, D:] * bt,)

    (merged,) = _rowwise(merge, [gl, p5, pssd], [p["b_gate"]], [(D, BF16)], name="mix_merge")
    y = _matmul(merged, w["w_out"], "nn", name="mix_out", epilogue=lambda acc, xt: (xt + acc,), extras=[x])
    res = dict(hn=hn, u=u, z=z, xbc=xbc, gl=gl, tiles=tiles, ys=ys, g=g, y5=y5, xc=xc, dtr8=dtr8, ssd_p=ssd_p,
               yssd_raw=yssd_raw, hs=hs, yssd=yssd, p5=p5, pssd=pssd, merged=merged)
    return y, res


def _mixer_bwd(dy, dy16, x, p, w, r):
    T, D = x.shape
    gp = {}
    dmerged = _matmul(dy16, w["w_out"], "nt", name="mix_d_merged")
    gp["w_out"] = _matmul(r["merged"], dy16, "tn", name="mix_d_wout")

    def merge_bwd(glt, at, bt, dm, bg):
        def f(q, a_, b_):
            gates = jax.nn.sigmoid(q + bg)
            return gates[:, :D] * a_ + gates[:, D:] * b_
        _, vjp = jax.vjp(f, glt, at, bt)
        dq, da_, db_ = vjp(dm)
        return da_, db_, dq, _colsum(dq)

    dp5, dpssd, dgl, gp["b_gate"] = _rowwise(
        merge_bwd, [r["gl"], r["p5"], r["pssd"], dmerged], [p["b_gate"]],
        [(D, BF16), (D, BF16), (2 * D, BF16)], [((1, 2 * D), F32)], name="mix_d_merge")
    dyssd = _matmul(dpssd, w["w_proj_ssd"], "nt", name="mix_d_yssd")
    gp["w_proj_ssd"] = _matmul(r["yssd"], dpssd, "tn", name="mix_d_wpssd")

    def gate_bwd(yt, zt, dyt, nw):
        _, vjp = jax.vjp(lambda a_, b_, c_: _rms(a_ * jax.nn.silu(b_), c_), yt, zt, nw)
        return vjp(dyt)

    dyraw, dz, gp["ssd_norm"] = _rowwise(
        gate_bwd, [r["yssd_raw"], r["z"], dyssd], [p["ssd_norm"]],
        [(D_INNER, F32), (D_INNER, BF16)], [((1, D_INNER), F32)], name="ssd_d_gate")
    dxs, dBm, dCm, ddtr8, dbias8, dalog8, ddsk8 = _ssd_bwd(r["xc"], r["dtr8"], *r["ssd_p"], r["hs"], dyraw)
    gp["ssd_dt_bias"], gp["ssd_A_log"], gp["ssd_D"] = [v.reshape(1, SSD_HEADS) for v in (dbias8, dalog8, ddsk8)]
    ddtr = ddtr8.transpose(1, 0, 2).reshape(T, SSD_HEADS)
    conv = [_conv_bwd(r["xbc"], w["conv_w"], p["conv_b"], d, c0, f"conv_bwd{i}")
            for i, (d, c0) in enumerate(((dxs, 0), (dBm, D_INNER), (dCm, D_INNER + SSD_GROUPS * SSD_STATE)))]
    dxbc = [c[0] for c in conv]
    gp["conv_w"] = jnp.concatenate([c[1] for c in conv], axis=1)
    gp["conv_b"] = jnp.concatenate([c[2] for c in conv], axis=1)

    (dp5,) = _rowwise(lambda t: (t,), [dp5], [], [(D, BF16)], name="mix_d_p5", after=(dxs,))
    dp5 = _to_seg(dp5)
    dy5 = _matmul(dp5, w["w_proj_s5"], "nt", name="mix_d_y5")
    gp["w_proj_s5"] = _matmul(r["y5"], dp5, "tn", name="mix_d_wp5")
    g = r["g"]

    def glu_ep(acc, gt, b, dyt):
        _, vjp = jax.vjp(lambda g_, t_: g_ * jax.nn.sigmoid(t_ + b), gt, acc)
        return vjp(dyt)

    dg1, dt_ = _matmul(g, w["s5_w_glu"], "nn", name="s5_d_glu", out=(F32, BF16), epilogue=glu_ep,
                       extras=[g, p["s5_b_glu"], dy5])
    gp["s5_w_glu"] = _matmul(g, dt_, "tn", name="s5_d_wglu")
    dg = _matmul(dt_, w["s5_w_glu"], "nt", name="s5_d_g", epilogue=lambda acc, e: (acc + e,), extras=[dg1])

    def gelu_bwd(yt, ut, dgt, dtt, d):
        _, vjp = jax.vjp(lambda y_, d_: jax.nn.gelu(y_ + d_ * ut), yt, d)
        dys, dd = vjp(dgt)
        return dys, dys * d, dd, _colsum(dtt.astype(F32))

    dys, dusk, gp["s5_D"], gp["s5_b_glu"] = _rowwise(
        gelu_bwd, [r["ys"], r["u"], dg, dt_], [p["s5_D"]], [(D, F32), (D, F32)], [((1, D), F32), ((1, D), F32)],
        name="s5_d_gelu")
    du, *dtiles = _s5_bwd(r["u"], dys, dusk, r["tiles"])
    du = _from_seg(du)

    pieces = [du, dz, None, ddtr, dgl]
    winT = w["w_in"]
    hn = r["hn"]
    d_rows, dhn = [], None
    cols = [(du, 0, 1024), (dz, 1024, 3072), (dxbc[0], 3072, 5120), (dxbc[1], 5120, 6144), (dxbc[2], 6144, 7168),
            (ddtr, 7168, 7200), (dgl, 7200, 9248)]
    for i, (d, lo, hi) in enumerate(cols):
        d_rows.append(_matmul(d, hn, "tn", name=f"mix_d_win{i}"))
        if dhn is None:
            dhn = _matmul(d, winT[lo:hi], "nn", name=f"mix_d_hn{i}")
        else:
            dhn = _matmul(d, winT[lo:hi], "nn", name=f"mix_d_hn{i}", epilogue=lambda acc, e: (acc + e,), extras=[dhn])
    gp["w_in"] = jnp.concatenate(d_rows, axis=0)
    del pieces

    def norm_bwd(xt, dh, dyt, gn):
        _, vjp = jax.vjp(_rms, xt, gn)
        dx, dgn = vjp(dh)
        return dyt + dx, dyt + dx, dgn

    dx, dx16, gp["mix_norm"] = _rowwise(norm_bwd, [x, dhn, dy], [p["mix_norm"]], [(D, F32), (D, BF16)],
                                        [((1, D), F32)], name="mix_d_norm")
    return dx, dx16, gp, dtiles


def _s5_param_grads(p, dtiles):
    keys = ("s5_A_re", "s5_A_im", "s5_log_dt", "s5_B_re", "s5_B_im", "s5_C_re", "s5_C_im")
    _, vjp = jax.vjp(_s5_derive, *[p[k][0] for k in keys])
    return {k: v[None] for k, v in zip(keys, vjp(tuple(dtiles)))}


def _local_step(x, target, p, w, late_weights, exchange):
    T, D = x.shape
    x1, r1 = _ffn_fwd(x, p["ffn1_norm"], w["ffn1_w_gate"], w["ffn1_w_up"], w["ffn1_w_down"], "ffn1")
    x2, rm = _mixer_fwd(x1, p, w)
    w = {**w, **late_weights(rm["p5"])}
    x3, r2 = _ffn_fwd(x2, p["ffn2_norm"], w["ffn2_w_gate"], w["ffn2_w_up"], w["ffn2_w_down"], "ffn2")

    def head(xt, tt, g):
        def f(x_, g_):
            e = _rms(x_, g_) - tt
            return 0.5 * jnp.sum(jnp.mean(e * e, axis=-1))
        l, (dx_, dg_) = jax.value_and_grad(f, argnums=(0, 1))(xt, g)
        return dx_, dx_, l.reshape(1, 1), dg_

    dx3, dx3_16, loss, d_final = _rowwise(head, [x3, target], [p["final_norm"]], [(D, F32), (D, BF16)],
                                          [((1, 1), F32), ((1, D), F32)], name="loss_head")
    gp = {"final_norm": d_final}

    def ffn_exchange(tag):
        return lambda d_wgT, d_wuT, d_wd, after: exchange(
            tag, {f"{tag}_w_gate": d_wgT, f"{tag}_w_up": d_wuT, f"{tag}_w_down": d_wd}, after)

    dx2, dx2_16, gp["ffn2_norm"] = _ffn_bwd(
        dx3, dx3_16, x2, p["ffn2_norm"], w["ffn2_w_gate"], w["ffn2_w_up"], w["ffn2_w_down"], r2, "ffn2", (),
        ffn_exchange("ffn2"))
    dx1, dx1_16, gm, dtiles = _mixer_bwd(dx2, dx2_16, x1, p, w, rm)
    conv_w_grad = gm.pop("conv_w")
    started = exchange("mixer", {k: gm.pop(k) for k in ("w_out", "w_proj_s5", "w_proj_ssd", "s5_w_glu", "w_in")}, dx1)
    gp.update(gm)
    gp.update(_s5_param_grads(p, dtiles))
    dx0, _, gp["ffn1_norm"] = _ffn_bwd(
        dx1, dx1_16, x, p["ffn1_norm"], w["ffn1_w_gate"], w["ffn1_w_up"], w["ffn1_w_down"], r1, "ffn1", started,
        ffn_exchange("ffn1"))
    return loss, dx0, gp, conv_w_grad


ADAMW_BLOCK = 1 << 18


def _adamw(w, m, v, gs, *, name, emit_g):
    _, R, C = w.shape
    n_g = len(gs)
    n_o = 4 if emit_g else 3
    tm = _rtile(R, max(8, ADAMW_BLOCK // C))
    lead = pl.BlockSpec((None, tm, C), lambda i: (0, i, 0))

    def body(w_ref, m_ref, v_ref, *refs):
        g = refs[0][...]
        for r in refs[1:n_g]:
            g = g + r[...]
        m2 = ADAM_B1 * m_ref[...] + (1.0 - ADAM_B1) * g
        v2 = ADAM_B2 * v_ref[...] + (1.0 - ADAM_B2) * (g * g)
        m_hat = m2 / (1.0 - ADAM_B1 ** ADAM_STEP)
        v_hat = v2 / (1.0 - ADAM_B2 ** ADAM_STEP)
        delta = -ADAM_LR * (m_hat / (jnp.sqrt(v_hat) + ADAM_EPS) + ADAM_WD * w_ref[...])
        for r, val in zip(refs[n_g:], ((g,) if emit_g else ()) + (delta, m2, v2)):
            r[...] = val

    return pl.pallas_call(
        body, name=name, grid=(R // tm,),
        in_specs=[lead] * 3 + [pl.BlockSpec((tm, C), lambda i: (i, 0))] * n_g, out_specs=[lead] * n_o,
        out_shape=[jax.ShapeDtypeStruct((1, R, C), F32)] * n_o,
        compiler_params=_params(("parallel",)),
    )(w, m, v, *gs)


def _rtile(r, cap=512):
    if r <= cap:
        return r
    best = None
    for t in range(8, cap + 1, 8):
        if r % t == 0:
            best = t
    assert best is not None, r
    return best


ANY = pl.BlockSpec(memory_space=pl.ANY)


def _where_am_i():
    x, y, c = lax.axis_index("x"), lax.axis_index("y"), lax.axis_index("c")
    ks = (2 * x + y, 2 * (1 - x) + y, 2 * x + (1 - y), 2 * (1 - x) + (1 - y))
    return x, y, c, ks


def _rc(src, dst, ssem, rsem, to):
    return pltpu.make_async_remote_copy(src_ref=src, dst_ref=dst, send_sem=ssem, recv_sem=rsem,
                                        device_id=to, device_id_type=MESH)


def _gather_shards(srcs):
    n = len(srcs)

    def body(*refs):
        _gather_body(refs[:n], refs[n:2 * n], *refs[2 * n:], extra=())

    return pl.pallas_call(
        body, name="gather_shards",
        in_specs=[ANY] * n, out_specs=[ANY] * n,
        out_shape=[jax.ShapeDtypeStruct((4,) + s.shape, s.dtype) for s in srcs],
        scratch_shapes=[pltpu.SemaphoreType.DMA((n, 6)), pltpu.SemaphoreType.DMA((n, 6)), pltpu.SemaphoreType.DMA((n,))],
        compiler_params=pltpu.CompilerParams(has_side_effects=True),
    )(*srcs)


def _handshake_all():
    x, y, c = lax.axis_index("x"), lax.axis_index("y"), lax.axis_index("c")
    barrier = pltpu.get_barrier_semaphore()
    for dx, dy, dc in itertools.product((0, 1), repeat=3):
        if (dx, dy, dc) != (0, 0, 0):
            to = (1 - x if dx else x, 1 - y if dy else y, 1 - c if dc else c)
            pl.semaphore_signal(barrier, inc=1, device_id=to, device_id_type=MESH)
    pl.semaphore_wait(barrier, 7)


SEQUENCER = dict(axis_name="seq", num_cores=1)


def _gather_shards_async(srcs, token, name, collective_id):
    n = len(srcs)

    def body(*refs):
        tok, tok_out = refs[n], refs[2 * n + 1]
        ssem, rsem, lsem = refs[2 * n + 2:]
        _handshake_all()
        _gather_body(refs[:n], refs[n + 1:2 * n + 1], ssem, rsem, lsem,
                     extra=(pltpu.make_async_copy(tok, tok_out, lsem.at[n]),))

    res = pl.kernel(
        body, name=name,
        out_type=[jax.ShapeDtypeStruct((4,) + s.shape, s.dtype) for s in srcs]
        + [jax.ShapeDtypeStruct(token.shape, token.dtype)],
        mesh=plsc.ScalarSubcoreMesh(**SEQUENCER),
        scratch_types=[pltpu.SemaphoreType.DMA((n, 6)), pltpu.SemaphoreType.DMA((n, 6)), pltpu.SemaphoreType.DMA((n + 1,))],
        compiler_params=pltpu.CompilerParams(collective_id=collective_id),
    )(*srcs, token)
    return res[:n], res[n]


def _gather_body(src, out, ssem, rsem, lsem, extra):
        n = len(src)
        x, y, c, (k_me, k_x, k_y, k_d) = _where_am_i()
        sib = (x, y, 1 - c)
        local = [pltpu.make_async_copy(src[i], out[i].at[k_me], lsem.at[i]) for i in range(n)] + list(extra)
        for cp in local:
            cp.start()

        def own(i, q, to):
            return _rc(src[i], out[i].at[k_me], ssem.at[i, q], rsem.at[i, q], to)

        def slab(i, q, k, to):
            return _rc(out[i].at[k], out[i].at[k], ssem.at[i, q], rsem.at[i, q], to)

        @pl.when(c == 1)
        def _():
            sends = [own(i, 0, (1 - x, y, 1)) for i in range(n)]
            for cp in sends:
                cp.start()
            fwd = [slab(i, 3, k_x, sib) for i in range(n)]
            for i in range(n):
                slab(i, 0, k_x, sib).wait_recv()
                fwd[i].start()
            for i in range(n):
                slab(i, 4, k_y, sib).wait_recv()
                slab(i, 5, k_d, sib).wait_recv()
            for cp in sends + fwd:
                cp.wait_send()

        @pl.when(c == 0)
        def _():
            sends = [own(i, 1, (x, 1 - y, 0)) for i in range(n)] + [own(i, 2, (1 - x, 1 - y, 0)) for i in range(n)]
            for cp in sends:
                cp.start()
            fwd = [slab(i, 4, k_y, sib) for i in range(n)] + [slab(i, 5, k_d, sib) for i in range(n)]
            for i in range(n):
                slab(i, 1, k_y, sib).wait_recv()
                fwd[i].start()
            for i in range(n):
                slab(i, 2, k_d, sib).wait_recv()
                fwd[n + i].start()
            for i in range(n):
                slab(i, 3, k_x, sib).wait_recv()
            for cp in sends + fwd:
                cp.wait_send()

        for cp in local:
            cp.wait()


def _swap_slabs(arrs, n_slabs, name, async_id, token):
    n = len(arrs)
    J = max(n_slabs, 1)

    def body(*refs):
        src, out = refs[:n], refs[n:2 * n]
        ssem, rsem = refs[2 * n:]
        x, y, c, (k_me, k_x, k_y, k_d) = _where_am_i()
        sib = (x, y, 1 - c)
        sel = (jnp.where(c == 1, k_y, k_me), jnp.where(c == 1, k_d, k_x))
        cps = []
        for i in range(n):
            for j in range(J):
                s = src[i].at[sel[j]] if n_slabs else src[i]
                d = out[i].at[j] if n_slabs else out[i]
                cps.append(_rc(s, d, ssem.at[i, j], rsem.at[i, j], sib))
        for cp in cps:
            cp.start()
        for cp in cps:
            cp.wait()

    return _exchange_call(
        body, arrs, [jax.ShapeDtypeStruct(((n_slabs,) if n_slabs else ()) + a.shape[-2:], a.dtype) for a in arrs],
        [pltpu.SemaphoreType.DMA((n, J)), pltpu.SemaphoreType.DMA((n, J))], name, async_id, token)


def _exchange_call(body, arrs, out_shape, sems, name, collective_id, token):
    n, m = len(arrs), len(out_shape)

    def seq_body(*refs):
        tok, tok_out, tok_sem = refs[n], refs[n + 1 + m], refs[-1]
        _handshake_all()
        cp = pltpu.make_async_copy(tok, tok_out, tok_sem)
        cp.start()
        body(*refs[:n], *refs[n + 1:n + 1 + m], *refs[n + 2 + m:-1])
        cp.wait()

    res = pl.kernel(
        seq_body, name=name, out_type=list(out_shape) + [jax.ShapeDtypeStruct(token.shape, token.dtype)],
        mesh=plsc.ScalarSubcoreMesh(**SEQUENCER), scratch_types=list(sems) + [pltpu.SemaphoreType.DMA],
        compiler_params=pltpu.CompilerParams(collective_id=collective_id))(*arrs, token)
    return res[:m], res[m]


def _send_chip_sums(arrs, name, async_id, token):
    n = len(arrs)

    def body(*refs):
        src, out = refs[:n], refs[n:2 * n]
        ssem, rsem = refs[2 * n:]
        x, y, c, _ = _where_am_i()

        @pl.when(c == 1)
        def _():
            cps = [_rc(src[i].at[1], out[i].at[0], ssem.at[i, 0], rsem.at[i, 0], (1 - x, y, 1)) for i in range(n)]
            for cp in cps:
                cp.start()
            for cp in cps:
                cp.wait()

        @pl.when(c == 0)
        def _():
            cps = [_rc(src[i].at[0], out[i].at[0], ssem.at[i, 0], rsem.at[i, 0], (x, 1 - y, 0)) for i in range(n)]
            cps += [_rc(src[i].at[1], out[i].at[1], ssem.at[i, 1], rsem.at[i, 1], (1 - x, 1 - y, 0)) for i in range(n)]
            for cp in cps:
                cp.start()
            for cp in cps:
                cp.wait()

    return _exchange_call(
        body, arrs, [jax.ShapeDtypeStruct(a.shape, a.dtype) for a in arrs],
        [pltpu.SemaphoreType.DMA((n, 2)), pltpu.SemaphoreType.DMA((n, 2))], name, async_id, token)


def _chip_sum(g, recv, sel, name):
    _, r, C = g.shape
    tr = _rtile(r)

    def body(sel_ref, g_ref, r_ref, o32_ref, o16_ref):
        s = g_ref[...] + r_ref[...]
        o32_ref[...] = s
        o16_ref[...] = s.astype(BF16)

    blk = pl.BlockSpec((None, tr, C), lambda j, t, sel_ref: (j, t, 0))
    return pl.pallas_call(
        body, name=name,
        grid_spec=pltpu.PrefetchScalarGridSpec(
            num_scalar_prefetch=1, grid=(2, r // tr),
            in_specs=[pl.BlockSpec((None, tr, C), lambda j, t, sel_ref: (sel_ref[j], t, 0)), blk],
            out_specs=[blk, blk]),
        out_shape=[jax.ShapeDtypeStruct((2, r, C), F32), jax.ShapeDtypeStruct((2, r, C), BF16)],
        compiler_params=_params(("arbitrary", "arbitrary")),
    )(sel, g, recv)


def _cross_sum(s32, got, north, after, name):
    _, r, C = s32.shape
    tr = _rtile(r)

    def body(f_ref, p_ref, a_ref, b_ref, after_ref, o_ref):
        a = a_ref[...].astype(F32)

        @pl.when(f_ref[0] == 1)
        def _():
            o_ref[...] = p_ref[...] + a

        @pl.when(f_ref[0] == 0)
        def _():
            o_ref[...] = a + b_ref[...].astype(F32)

    return pl.pallas_call(
        body, name=name,
        grid_spec=pltpu.PrefetchScalarGridSpec(
            num_scalar_prefetch=1, grid=(r // tr,),
            in_specs=[pl.BlockSpec((None, tr, C), lambda t, f: (0, t, 0)),
                      pl.BlockSpec((None, tr, C), lambda t, f: (0, t, 0)),
                      pl.BlockSpec((None, tr, C), lambda t, f: (1 - f[0], t, 0)), ANY],
            out_specs=pl.BlockSpec((tr, C), lambda t, f: (t, 0))),
        out_shape=jax.ShapeDtypeStruct((r, C), F32),
        compiler_params=_params(("arbitrary",)),
    )(north, s32, got, got, after)


class _Reducer:
    def __init__(self, token):
        self.token, self.pending, self.pairs = token, None, {}

    def begin(self, tag, ids, grads, after):
        names = list(grads)
        arrs = [grads[n].reshape(4, -1, D_MODEL) for n in names]
        x, y, c, (k_me, k_x, k_y, k_d) = _where_am_i()
        sel = jnp.stack([jnp.where(c == 1, k_me, k_y), jnp.where(c == 1, k_x, k_d)]).astype(jnp.int32)
        from_sib, self.token = _swap_slabs(arrs, 2, f"swap_grad_slabs_{tag}", ids[0], self.token)
        sums = [_chip_sum(g, r, sel, f"chip_sum_{tag}{i}") for i, (g, r) in enumerate(zip(arrs, from_sib))]
        self.finish(after)
        got, self.token = _send_chip_sums([s16 for _, s16 in sums], f"send_chip_sums_{tag}", ids[1], self.token)
        self.pending = (tag, ids[2], names, sums, got)
        return tuple(s16 for _, s16 in sums)

    def finish(self, after):
        if self.pending is None:
            return
        tag, cid, names, sums, got = self.pending
        north = jnp.reshape(lax.axis_index("c"), (1,)).astype(jnp.int32)
        parts = [_cross_sum(s32, b, north, after, f"cross_sum_{tag}{i}") for i, ((s32, _), b) in enumerate(zip(sums, got))]
        theirs, self.token = _swap_slabs(parts, 0, f"swap_parts_{tag}", cid, self.token)
        self.pairs.update(zip(names, zip(parts, theirs)))
        self.pending = None


def _all_reduce_small(v):
    R, C = v.shape

    def body(v_ref, o_ref, buf, ssem, rsem, lsem):
        x, y, c = lax.axis_index("x"), lax.axis_index("y"), lax.axis_index("c")
        me, sib = (x, y, c), (x, y, 1 - c)
        chips = [(1 - x, y), (x, 1 - y), (1 - x, 1 - y)]

        def slot(px, py, pc):
            return buf.at[4 * px + 2 * py + pc]

        def copy(k, block, to, src=None):
            return _rc(slot(*block) if src is None else src, slot(*block), ssem.at[k], rsem.at[k], to)

        mine = pltpu.make_async_copy(v_ref, slot(*me), lsem)
        mine.start()
        first = [copy(0, me, sib, src=v_ref)] + [copy(1 + j, me, (*chip, c), src=v_ref) for j, chip in enumerate(chips)]
        for cp in first:
            cp.start()
        passed = [copy(4 + j, (*chip, c), sib) for j, chip in enumerate(chips)]
        for j, chip in enumerate(chips):
            copy(1 + j, (*chip, c), me).wait_recv()
            passed[j].start()
        copy(0, sib, me).wait_recv()
        for j, chip in enumerate(chips):
            copy(4 + j, (*chip, 1 - c), me).wait_recv()
        for cp in first + passed:
            cp.wait_send()
        mine.wait()
        acc = buf[0]
        for k in range(1, 8):
            acc = acc + buf[k]
        o_ref[...] = acc

    return pl.pallas_call(
        body, name="all_reduce_small",
        in_specs=[pl.BlockSpec(memory_space=pltpu.VMEM)], out_specs=pl.BlockSpec(memory_space=pltpu.VMEM),
        out_shape=jax.ShapeDtypeStruct((R, C), F32),
        scratch_shapes=[pltpu.VMEM((8, R, C), F32), pltpu.SemaphoreType.DMA((7,)), pltpu.SemaphoreType.DMA((7,)),
                        pltpu.SemaphoreType.DMA],
        compiler_params=pltpu.CompilerParams(has_side_effects=True, vmem_limit_bytes=VMEM_LIMIT),
    )(v)


WEIGHTS = ['ffn1_norm', 'ffn1_w_gate', 'ffn1_w_up', 'ffn1_w_down', 'mix_norm', 'w_in', 'conv_w', 'conv_b', 's5_A_re',
           's5_A_im', 's5_log_dt', 's5_B_re', 's5_B_im', 's5_C_re', 's5_C_im', 's5_D', 's5_w_glu', 's5_b_glu',
           'ssd_A_log', 'ssd_dt_bias', 'ssd_D', 'ssd_norm', 'w_proj_s5', 'w_proj_ssd', 'b_gate', 'w_out', 'ffn2_norm',
           'ffn2_w_gate', 'ffn2_w_up', 'ffn2_w_down', 'final_norm']
ARGS = ['x'] + WEIGHTS + ['loss_target'] + ['m_' + n for n in WEIGHTS] + ['v_' + n for n in WEIGHTS]
COL_SHARDED = ('ffn1_w_gate', 'ffn1_w_up', 'w_in', 'ffn2_w_gate', 'ffn2_w_up')
ROW_SHARDED = ('ffn1_w_down', 's5_w_glu', 'w_proj_s5', 'w_proj_ssd', 'w_out', 'ffn2_w_down')
MATRICES = COL_SHARDED + ROW_SHARDED
GATHER_FIRST = ('ffn1_w_gate', 'ffn1_w_up', 'ffn1_w_down')
GATHER_SECOND = ('w_in', 'conv_w', 's5_w_glu', 'w_proj_s5', 'w_proj_ssd', 'w_out')
GATHER_THIRD = ('ffn2_w_gate', 'ffn2_w_up', 'ffn2_w_down')
SMALL = [n for n in WEIGHTS if n not in MATRICES]


def _pack(arrs, width=1024):
    flat = jnp.concatenate([a.reshape(-1).astype(F32) for a in arrs])
    rows = -(-flat.shape[0] // (8 * width)) * 8
    return jnp.pad(flat, (0, rows * width - flat.shape[0])).reshape(rows, width)


def _unpack(packed, shapes):
    flat, out, o = packed.reshape(-1), [], 0
    for s in shapes:
        n = math.prod(s)
        out.append(flat[o:o + n].reshape(s))
        o += n
    return out


def kernel(x, ffn1_norm, ffn1_w_gate, ffn1_w_up, ffn1_w_down, mix_norm, w_in, conv_w, conv_b, s5_A_re, s5_A_im, s5_log_dt, s5_B_re, s5_B_im, s5_C_re, s5_C_im, s5_D, s5_w_glu, s5_b_glu, ssd_A_log, ssd_dt_bias, ssd_D, ssd_norm, w_proj_s5, w_proj_ssd, b_gate, w_out, ffn2_norm, ffn2_w_gate, ffn2_w_up, ffn2_w_down, final_norm, loss_target, m_ffn1_norm, m_ffn1_w_gate, m_ffn1_w_up, m_ffn1_w_down, m_mix_norm, m_w_in, m_conv_w, m_conv_b, m_s5_A_re, m_s5_A_im, m_s5_log_dt, m_s5_B_re, m_s5_B_im, m_s5_C_re, m_s5_C_im, m_s5_D, m_s5_w_glu, m_s5_b_glu, m_ssd_A_log, m_ssd_dt_bias, m_ssd_D, m_ssd_norm, m_w_proj_s5, m_w_proj_ssd, m_b_gate, m_w_out, m_ffn2_norm, m_ffn2_w_gate, m_ffn2_w_up, m_ffn2_w_down, m_final_norm, v_ffn1_norm, v_ffn1_w_gate, v_ffn1_w_up, v_ffn1_w_down, v_mix_norm, v_w_in, v_conv_w, v_conv_b, v_s5_A_re, v_s5_A_im, v_s5_log_dt, v_s5_B_re, v_s5_B_im, v_s5_C_re, v_s5_C_im, v_s5_D, v_s5_w_glu, v_s5_b_glu, v_ssd_A_log, v_ssd_dt_bias, v_ssd_D, v_ssd_norm, v_w_proj_s5, v_w_proj_ssd, v_b_gate, v_w_out, v_ffn2_norm, v_ffn2_w_gate, v_ffn2_w_up, v_ffn2_w_down, v_final_norm):
    a = dict(locals())
    assert list(a) == ARGS
    x, target = a['x'][0], a['loss_target'][0]
    k_me = 2 * lax.axis_index("x") + lax.axis_index("y")

    src = {n: a[n][0].T.astype(BF16) for n in COL_SHARDED}
    src.update({n: a[n][0].astype(BF16) for n in ROW_SHARDED})
    src['conv_w'] = a['conv_w'][0]
    def flat(f):
        return f.reshape(-1, f.shape[-1])

    full = _gather_shards([src[n] for n in GATHER_FIRST])
    w = {n: flat(f) for n, f in zip(GATHER_FIRST, full)}
    token = full[0][0, :8, :128].astype(F32)
    for names, name, cid in ((GATHER_SECOND[:2], "gather_mixer_in", 1), (GATHER_SECOND[2:], "gather_mixer", 12)):
        got, token = _gather_shards_async([src[n] for n in names], token, name, cid)
        w.update({n: flat(f) for n, f in zip(names, got)})
    w['conv_w'] = w['conv_w'].reshape(4, CONV_K, -1).transpose(1, 0, 2).reshape(CONV_K, CONV_DIM)
    reducer = _Reducer(token)

    def gather_last(after):
        got, reducer.token = _gather_shards_async(
            [src[n] for n in GATHER_THIRD], reducer.token + after[:8, :128], "gather_ffn2", 2)
        return {n: flat(f) for n, f in zip(GATHER_THIRD, got)}

    p = {n: a[n] for n in SMALL if n != 'conv_w'}
    p['final_norm'] = a['final_norm'][None]
    ids = {"ffn2": (3, 4, 5), "mixer": (6, 7, 8), "ffn1": (9, 10, 11)}
    loss, grad_x, gp, conv_w_grad = _local_step(
        x, target, p, w, gather_last, lambda tag, grads, after: reducer.begin(tag, ids[tag], grads, after))
    reducer.finish(grad_x)
    pairs = [reducer.pairs[n] for n in MATRICES]
    gp['conv_w'] = conv_w_grad[None]
    small_shapes = [(1, 1)] + [gp[n].shape if n != 'final_norm' else (1, D_MODEL) for n in SMALL]
    red = _unpack(_all_reduce_small(_pack([loss] + [gp[n] for n in SMALL])), small_shapes)
    loss_all = red[0].reshape(())
    gsmall = dict(zip(SMALL, red[1:]))
    gsmall['conv_w'] = lax.dynamic_slice_in_dim(gsmall['conv_w'], k_me * 1024, 1024, axis=2)
    gsmall = {n: g.reshape(a[n].shape) for n, g in gsmall.items()}

    grads, delta, new_m, new_v = {}, {}, {}, {}
    for n, (mine, theirs) in zip(MATRICES, pairs):
        wn, mn, vn = a[n], a['m_' + n], a['v_' + n]
        if n in COL_SHARDED:
            (gt,) = _rowwise(lambda p_, q_: (p_ + q_,), [mine, theirs], [], [(D_MODEL, F32)], name=f"sum_{n}",
                             tm=_rtile(mine.shape[0]))
            g = gt.T
            d, m2, v2 = _adamw(wn, mn, vn, [g], name=f"adamw_{n}", emit_g=False)
            g = g[None]
        else:
            g, d, m2, v2 = _adamw(wn, mn, vn, [mine, theirs], name=f"adamw_{n}", emit_g=True)
        grads[n], delta[n], new_m[n], new_v[n] = g, d, m2, v2
    sw, sm, sv, sg = [_pack([t[n] for n in SMALL])[None] for t in (a, {n: a['m_' + n] for n in SMALL},
                                                                   {n: a['v_' + n] for n in SMALL}, gsmall)]
    d, m2, v2 = _adamw(sw, sm, sv, [sg[0]], name="adamw_small", emit_g=False)
    shapes = [a[n].shape for n in SMALL]
    for n, dd, mm, vv in zip(SMALL, _unpack(d[0], shapes), _unpack(m2[0], shapes), _unpack(v2[0], shapes)):
        grads[n], delta[n], new_m[n], new_v[n] = gsmall[n], dd, mm, vv
    return (loss_all, grad_x[None], *[grads[n] for n in WEIGHTS], *[delta[n] for n in WEIGHTS],
            *[new_m[n] for n in WEIGHTS], *[new_v[n] for n in WEIGHTS])
```

```python
import itertools
import math

import jax
import jax.numpy as jnp
from jax import lax
from jax.experimental import pallas as pl
from jax.experimental.pallas import tpu as pltpu
from jax.experimental.pallas import tpu_sc as plsc

F32 = jnp.float32
BF16 = jnp.bfloat16
MXU = BF16
HI = lax.Precision.HIGHEST

D_MODEL = 1024
D_FF = 2816
EPS = 1e-6
S5_GROUPS, S5_GROUP, S5_STATE = 64, 16, 64
D_INNER = 2048
SSD_HEADDIM, SSD_HEADS, SSD_GROUPS, SSD_HPG, SSD_STATE = 64, 32, 8, 4, 128
CONV_K, CONV_DIM = 4, 4096
IN_SPLITS = (1024, 2048, 4096, 32, 2048)
IN_OFFS = (0, 1024, 3072, 7168, 7200, 9248)
SSD_Q = 256
SSD_STEP_GROUPS = 2
ADAM_LR, ADAM_B1, ADAM_B2, ADAM_EPS, ADAM_WD, ADAM_STEP = 0.001, 0.9, 0.999, 1e-08, 0.01, 10

VMEM_LIMIT = 56 * 1024 * 1024
MATMUL_VMEM = 32 * 1024 * 1024
MESH = pl.DeviceIdType.MESH

NN = ((1,), (0,))
NT = ((1,), (1,))
TN = ((0,), (0,))


def _dot(a, b, dims, precision=None):
    if precision is None:
        a, b = a.astype(MXU), b.astype(MXU)
    return lax.dot_general(a, b, (dims, ((), ())), precision=precision, preferred_element_type=F32)


def _tile(n, pref):
    if n <= pref:
        return n
    best = None
    for t in range(128, pref + 1, 128):
        if n % t == 0:
            best = t
    assert best is not None, (n, pref)
    return best


def _params(sem):
    return pltpu.CompilerParams(dimension_semantics=sem, vmem_limit_bytes=VMEM_LIMIT)


def _matmul(a, b, mode, *, name, out=((F32),), epilogue=None, extras=(), after=(), tm=512, tn=1536):
    if mode == "nn":
        (M, K), (_, N) = a.shape, b.shape
    elif mode == "nt":
        (M, K), (N, _) = a.shape, b.shape
    else:
        (K, M), (_, N) = a.shape, b.shape
    tm, tn = _tile(M, tm), _tile(N, tn)

    def vmem(tm_, tn_):
        per = K * tm_ * a.dtype.itemsize + K * tn_ * b.dtype.itemsize
        per += sum(tm_ * tn_ * jnp.dtype(dt).itemsize for dt in out)
        per += sum((1 if e.shape[0] == 1 else tm_) * tn_ * e.dtype.itemsize for e in extras)
        return 2 * per

    while vmem(tm, tn) > MATMUL_VMEM and (tn > 128 or tm > 128):
        if tn >= tm and tn > 128:
            tn = _tile(N, tn - 128)
        else:
            tm = _tile(M, tm - 128)
    a_spec = pl.BlockSpec((K, tm), lambda i, j: (0, i)) if mode == "tn" else pl.BlockSpec((tm, K), lambda i, j: (i, 0))
    b_spec = pl.BlockSpec((tn, K), lambda i, j: (j, 0)) if mode == "nt" else pl.BlockSpec((K, tn), lambda i, j: (0, j))
    dims = {"nn": NN, "nt": NT, "tn": TN}[mode]
    e_specs = []
    for e in extras:
        if e.shape[0] == 1:
            e_specs.append(pl.BlockSpec((1, tn), lambda i, j: (0, j)))
        else:
            assert e.shape == (M, N), (e.shape, M, N)
            e_specs.append(pl.BlockSpec((tm, tn), lambda i, j: (i, j)))
    n_e, n_o, n_a = len(extras), len(out), len(after)

    def body(a_ref, b_ref, *refs):
        acc = _dot(a_ref[...], b_ref[...], dims)
        res = (acc,) if epilogue is None else epilogue(acc, *[r[...] for r in refs[:n_e]])
        for r, v in zip(refs[n_e + n_a:], res):
            r[...] = v.astype(r.dtype)

    res = pl.pallas_call(
        body, name=name, grid=(M // tm, N // tn),
        in_specs=[a_spec, b_spec] + e_specs + [ANY] * n_a,
        out_specs=[pl.BlockSpec((tm, tn), lambda i, j: (i, j)) for _ in range(n_o)],
        out_shape=[jax.ShapeDtypeStruct((M, N), dt) for dt in out],
        compiler_params=_params(("parallel", "parallel")),
    )(a, b, *extras, *after)
    return res[0] if n_o == 1 else res


def _rowwise(fn, rows, fulls, outs, reds=(), *, name, after=(), tm=256):
    T = rows[0].shape[0]
    tm = min(tm, T)
    n_r, n_f, n_o, n_d, n_a = len(rows), len(fulls), len(outs), len(reds), len(after)

    def body(*refs):
        ins = [r[...] for r in refs[:n_r + n_f]]
        o_refs = refs[n_r + n_f + n_a:n_r + n_f + n_a + n_o]
        d_refs = refs[n_r + n_f + n_a + n_o:]
        res = fn(*ins)
        for r, v in zip(o_refs, res[:n_o]):
            r[...] = v.astype(r.dtype)
        if n_d:
            @pl.when(pl.program_id(0) == 0)
            def _():
                for r in d_refs:
                    r[...] = jnp.zeros_like(r)
            for r, v in zip(d_refs, res[n_o:]):
                r[...] += v.astype(r.dtype)

    res = pl.pallas_call(
        body, name=name, grid=(T // tm,),
        in_specs=[pl.BlockSpec((tm, r.shape[1]), lambda i: (i, 0)) for r in rows]
        + [pl.BlockSpec(f.shape, lambda i, nd=f.ndim: (0,) * nd) for f in fulls] + [ANY] * n_a,
        out_specs=[pl.BlockSpec((tm, c), lambda i: (i, 0)) for c, _ in outs]
        + [pl.BlockSpec(s, lambda i, nd=len(s): (0,) * nd) for s, _ in reds],
        out_shape=[jax.ShapeDtypeStruct((T, c), dt) for c, dt in outs]
        + [jax.ShapeDtypeStruct(s, dt) for s, dt in reds],
        compiler_params=_params(("arbitrary",)),
    )(*rows, *fulls, *after)
    return res


def _rms(x, g):
    return x * lax.rsqrt(jnp.mean(x * x, axis=-1, keepdims=True) + EPS) * g


def _colsum(v):
    return jnp.sum(v, axis=0, keepdims=True)


def _softplus(x):
    return jnp.maximum(x, 0.0) + jnp.log1p(jnp.exp(-jnp.abs(x)))


def _ffn_fwd(x, norm, wgT, wuT, wd, tag):
    D = x.shape[1]
    (hn,) = _rowwise(lambda xt, g: (_rms(xt, g),), [x], [norm], [(D, BF16)], name=f"{tag}_norm")
    a = _matmul(hn, wgT, "nt", name=f"{tag}_gate")
    b, hid = _matmul(hn, wuT, "nt", name=f"{tag}_up", out=(BF16, BF16),
                     epilogue=lambda acc, at: (acc, jax.nn.silu(at) * acc), extras=[a])
    y = _matmul(hid, wd, "nn", name=f"{tag}_down", epilogue=lambda acc, xt: (xt + 0.5 * acc,), extras=[x])
    return y, (hn, a, b, hid)


def _ffn_bwd(dy, dy16, x, norm, wgT, wuT, wd, res, tag, after, on_grads):
    hn, a, b, hid = res
    D, F = x.shape[1], a.shape[1]

    def act_bwd(acc, at, bt):
        _, vjp = jax.vjp(lambda p, q: jax.nn.silu(p) * q, at, bt.astype(F32))
        return vjp(0.5 * acc)

    da, db = _matmul(dy16, wd, "nt", name=f"{tag}_d_hid", out=(BF16, BF16), epilogue=act_bwd, extras=[a, b])
    d_wd = _matmul(hid, dy16, "tn", name=f"{tag}_d_wd", epilogue=lambda acc: (0.5 * acc,))
    d_wgT = _matmul(da, hn, "tn", name=f"{tag}_d_wg", after=after)
    d_wuT = _matmul(db, hn, "tn", name=f"{tag}_d_wu")
    started = on_grads(d_wgT, d_wuT, d_wd, dy)
    dhn = _matmul(da, wgT, "nn", name=f"{tag}_d_hn1")
    dhn = _matmul(db, wuT, "nn", name=f"{tag}_d_hn2", epilogue=lambda acc, e: (acc + e,), extras=[dhn])

    def norm_bwd(xt, dh, dyt, g):
        _, vjp = jax.vjp(_rms, xt, g)
        dx, dg = vjp(dh)
        return dyt + dx, dyt + dx, dg

    dx, dx16, d_norm = _rowwise(norm_bwd, [x, dhn, dy], [norm], [(D, F32), (D, BF16)], [((1, D), F32)],
                                name=f"{tag}_d_norm", after=started)
    return dx, dx16, d_norm


S5_TILES = 8
S5_HALF = 256


def _s5_derive(A_re, A_im, log_dt, B_re, B_im, C_re, C_im):
    G, N, M = S5_GROUPS, S5_STATE, S5_GROUP
    dt = jnp.exp(log_dt)[:, None]
    mag = jnp.exp(A_re * dt)
    ar, ai = mag * jnp.cos(A_im * dt), mag * jnp.sin(A_im * dt)
    den = A_re * A_re + A_im * A_im
    cr = ((ar - 1.0) * A_re + ai * A_im) / den
    ci = (ai * A_re - (ar - 1.0) * A_im) / den
    bbr = cr[..., None] * B_re - ci[..., None] * B_im
    bbi = cr[..., None] * B_im + ci[..., None] * B_re
    eye = jnp.eye(8, dtype=F32)

    def tile_in(bb):
        t = bb.reshape(S5_TILES, 8, N, M).transpose(0, 1, 3, 2)
        return jnp.einsum("jamn,ab->jambn", t, eye).reshape(S5_TILES, 8 * M, 8 * N)

    def tile_out(c):
        t = c.reshape(S5_TILES, 8, M, N).transpose(0, 1, 3, 2)
        return jnp.einsum("janm,ab->janbm", t, eye).reshape(S5_TILES, 8 * N, 8 * M)

    return (tile_in(bbr), tile_in(bbi), tile_out(C_re), tile_out(C_im),
            ar.reshape(S5_TILES, 1, 8 * N), ai.reshape(S5_TILES, 1, 8 * N))


S5_NB = S5_HALF // 128
S5_SEG = 32


def _cmul(ar, ai, br, bi):
    return ar * br - ai * bi, ar * bi + ai * br


def _s5_scan(sr_ref, si_ref, ar, ai, T, reverse):
    L, V = T // S5_SEG, S5_SEG // 8
    assert L * S5_SEG == T and L & (L - 1) == 0, T
    sg = -1.0 if reverse else 1.0
    a_r = [jnp.broadcast_to(ar[:, 128 * b:128 * (b + 1)], (8, 128)) for b in range(S5_NB)]
    a_i = [jnp.broadcast_to(sg * ai[:, 128 * b:128 * (b + 1)], (8, 128)) for b in range(S5_NB)]

    def rows(k, v):
        return pl.ds(pl.multiple_of(((L - 1 - k) if reverse else k) * S5_SEG + 8 * v, 8), 8)

    def local(k, carry):
        out = []
        for b in range(S5_NB):
            for v in range(V):
                idx = rows(k, v)
                mr, mi = _cmul(a_r[b], a_i[b], *carry[b * V + v])
                nr, ni = mr + sr_ref[b, idx, :], mi + si_ref[b, idx, :]
                sr_ref[b, idx, :] = nr
                si_ref[b, idx, :] = ni
                out.append((nr, ni))
        return tuple(out)

    z = jnp.zeros((8, 128), F32)
    ends = lax.fori_loop(0, L, local, tuple((z, z) for _ in range(S5_NB * V)))

    carries = []
    for b in range(S5_NB):
        pr, pi = a_r[b][0:1], a_i[b][0:1]
        n = L
        while n > 1:
            pr, pi = _cmul(pr, pi, pr, pi)
            n //= 2
        cr, ci = jnp.zeros((1, 128), F32), jnp.zeros((1, 128), F32)
        into = [None] * S5_SEG
        for j in (reversed(range(S5_SEG)) if reverse else range(S5_SEG)):
            into[j] = (cr, ci)
            er, ei = ends[b * V + j // 8]
            mr, mi = _cmul(pr, pi, cr, ci)
            cr, ci = mr + er[j % 8:j % 8 + 1], mi + ei[j % 8:j % 8 + 1]
        carries.append([(jnp.concatenate([into[8 * v + s][0] for s in range(8)], axis=0),
                         jnp.concatenate([into[8 * v + s][1] for s in range(8)], axis=0)) for v in range(V)])

    def fix(k, powers):
        out = []
        for b in range(S5_NB):
            pr, pi = powers[b]
            for v in range(V):
                idx = rows(k, v)
                dr, di = _cmul(pr, pi, *carries[b][v])
                sr_ref[b, idx, :] += dr
                si_ref[b, idx, :] += di
            out.append(_cmul(pr, pi, a_r[b], a_i[b]))
        return tuple(out)

    lax.fori_loop(0, L, fix, tuple((a_r[b], a_i[b]) for b in range(S5_NB)))


def _s5_in(ut, b_ref, s_ref):
    for b in range(S5_NB):
        s_ref[b] = _dot(ut, b_ref[:, 128 * b:128 * (b + 1)], NN)


def _to_seg(v):
    T, C = v.shape
    return v.reshape(S5_SEG, T // S5_SEG, C).transpose(1, 0, 2).reshape(T, C)


def _from_seg(v):
    T, C = v.shape
    return v.reshape(T // S5_SEG, S5_SEG, C).transpose(1, 0, 2).reshape(T, C)


def _s5_specs(T):
    u_spec = pl.BlockSpec((T, 128), lambda j, h: (0, j))
    b_spec = pl.BlockSpec((None, 128, S5_HALF), lambda j, h: (j, 0, h))
    c_spec = pl.BlockSpec((None, S5_HALF, 128), lambda j, h: (j, h, 0))
    a_spec = pl.BlockSpec((None, 1, S5_HALF), lambda j, h: (j, 0, h))
    return u_spec, b_spec, c_spec, a_spec


def _s5_fwd(u, tiles):
    T = u.shape[0]
    u_spec, b_spec, c_spec, a_spec = _s5_specs(T)

    def body(u_ref, br_ref, bi_ref, cr_ref, ci_ref, ar_ref, ai_ref, y_ref, sr_ref, si_ref):
        ut = u_ref[...].astype(MXU)
        _s5_in(ut, br_ref, sr_ref)
        _s5_in(ut, bi_ref, si_ref)
        _s5_scan(sr_ref, si_ref, ar_ref[...], ai_ref[...], T, False)
        y = None
        for b in range(S5_NB):
            blk = slice(128 * b, 128 * (b + 1))
            yb = _dot(sr_ref[b], cr_ref[blk, :], NN) - _dot(si_ref[b], ci_ref[blk, :], NN)
            y = yb if y is None else y + yb

        @pl.when(pl.program_id(1) == 0)
        def _():
            y_ref[...] = y

        @pl.when(pl.program_id(1) == 1)
        def _():
            y_ref[...] += y

    scr = pltpu.VMEM((S5_NB, T, 128), F32)
    return pl.pallas_call(
        body, name="s5_fwd", grid=(S5_TILES, 2),
        in_specs=[u_spec, b_spec, b_spec, c_spec, c_spec, a_spec, a_spec],
        out_specs=u_spec, out_shape=jax.ShapeDtypeStruct(u.shape, F32),
        scratch_shapes=[scr, scr],
        compiler_params=_params(("parallel", "arbitrary")),
    )(u, *tiles)


def _s5_bwd(u, dys, du_skip, tiles):
    T = u.shape[0]
    u_spec, b_spec, c_spec, a_spec = _s5_specs(T)

    def body(u_ref, dy_ref, sk_ref, br_ref, bi_ref, cr_ref, ci_ref, ar_ref, ai_ref,
             du_ref, dbr_ref, dbi_ref, dcr_ref, dci_ref, dar_ref, dai_ref, sr_ref, si_ref, lr_ref, li_ref):
        ut, dy = u_ref[...].astype(MXU), dy_ref[...].astype(MXU)
        ar, ai = ar_ref[...], ai_ref[...]
        _s5_in(ut, br_ref, sr_ref)
        _s5_in(ut, bi_ref, si_ref)
        _s5_scan(sr_ref, si_ref, ar, ai, T, False)
        for b in range(S5_NB):
            blk = slice(128 * b, 128 * (b + 1))
            dcr_ref[blk, :] = _dot(sr_ref[b], dy, TN)
            dci_ref[blk, :] = -_dot(si_ref[b], dy, TN)
            lr_ref[b] = _dot(dy, cr_ref[blk, :], NT)
            li_ref[b] = -_dot(dy, ci_ref[blk, :], NT)
        _s5_scan(lr_ref, li_ref, ar, ai, T, True)
        du = None
        head = lax.broadcasted_iota(jnp.int32, (T, 128), 0) < S5_SEG
        seg0 = lax.broadcasted_iota(jnp.int32, (S5_SEG, 128), 0) == 0
        for b in range(S5_NB):
            blk = slice(128 * b, 128 * (b + 1))
            lr, li = lr_ref[b], li_ref[b]
            pr = jnp.where(head, 0.0, pltpu.roll(sr_ref[b], S5_SEG, 0))
            pi = jnp.where(head, 0.0, pltpu.roll(si_ref[b], S5_SEG, 0))
            er = jnp.where(seg0, 0.0, pltpu.roll(sr_ref[b, T - S5_SEG:T, :], 1, 0))
            ei = jnp.where(seg0, 0.0, pltpu.roll(si_ref[b, T - S5_SEG:T, :], 1, 0))
            hr, hi = lr[0:S5_SEG], li[0:S5_SEG]
            dar_ref[:, blk] = _colsum(lr * pr + li * pi) + _colsum(hr * er + hi * ei)
            dai_ref[:, blk] = _colsum(li * pr - lr * pi) + _colsum(hi * er - hr * ei)
            dbr_ref[:, blk] = _dot(ut, lr, TN)
            dbi_ref[:, blk] = _dot(ut, li, TN)
            dub = _dot(lr, br_ref[:, blk], NT) + _dot(li, bi_ref[:, blk], NT)
            du = dub if du is None else du + dub

        @pl.when(pl.program_id(1) == 0)
        def _():
            du_ref[...] = du + sk_ref[...]

        @pl.when(pl.program_id(1) == 1)
        def _():
            du_ref[...] += du

    scr = pltpu.VMEM((S5_NB, T, 128), F32)
    return pl.pallas_call(
        body, name="s5_bwd", grid=(S5_TILES, 2),
        in_specs=[u_spec, u_spec, u_spec, b_spec, b_spec, c_spec, c_spec, a_spec, a_spec],
        out_specs=[u_spec, b_spec, b_spec, c_spec, c_spec, a_spec, a_spec],
        out_shape=[jax.ShapeDtypeStruct(u.shape, F32)] + [jax.ShapeDtypeStruct(t.shape, F32) for t in tiles],
        scratch_shapes=[scr, scr, scr, scr],
        compiler_params=_params(("parallel", "arbitrary")),
    )(u, dys, du_skip, *tiles)


CONV_TILE = 256


def _conv_pre(x, w, b):
    T = x.shape[0]
    row = lax.broadcasted_iota(jnp.int32, x.shape, 0)
    acc = x * w[CONV_K - 1:CONV_K, :] + b
    for lag in range(1, CONV_K):
        acc = acc + jnp.where(row >= lag, pltpu.roll(x, lag, 0), 0.0) * w[CONV_K - 1 - lag:CONV_K - lag, :]
    return acc


def _conv_fwd(x, w, b):
    T, C = x.shape
    col = pl.BlockSpec((T, CONV_TILE), lambda j: (0, j))

    def body(x_ref, w_ref, b_ref, o_ref):
        o_ref[...] = jax.nn.silu(_conv_pre(x_ref[...], w_ref[...], b_ref[...]))

    return pl.pallas_call(
        body, name="conv_fwd", grid=(C // CONV_TILE,),
        in_specs=[col, pl.BlockSpec((CONV_K, CONV_TILE), lambda j: (0, j)), pl.BlockSpec((1, CONV_TILE), lambda j: (0, j))],
        out_specs=col, out_shape=jax.ShapeDtypeStruct((T, C), F32),
        compiler_params=_params(("parallel",)),
    )(x, w, b)


def _conv_bwd(x, w, b, dout, col0, name):
    T, C = dout.shape
    off = col0 // CONV_TILE
    xcol = pl.BlockSpec((T, CONV_TILE), lambda j: (0, j + off))
    dcol = pl.BlockSpec((T, CONV_TILE), lambda j: (0, j))

    def body(x_ref, w_ref, b_ref, d_ref, dx_ref, dw_ref, db_ref):
        x, w = x_ref[...], w_ref[...]
        _, vjp = jax.vjp(jax.nn.silu, _conv_pre(x, w, b_ref[...]))
        (dy,) = vjp(d_ref[...])
        row = lax.broadcasted_iota(jnp.int32, x.shape, 0)
        dx = dy * w[CONV_K - 1:CONV_K, :]
        dw_ref[CONV_K - 1:CONV_K, :] = _colsum(dy * x)
        for lag in range(1, CONV_K):
            ahead = jnp.where(row < T - lag, pltpu.roll(dy, T - lag, 0), 0.0)
            dx = dx + ahead * w[CONV_K - 1 - lag:CONV_K - lag, :]
            dw_ref[CONV_K - 1 - lag:CONV_K - lag, :] = _colsum(ahead * x)
        dx_ref[...] = dx.astype(dx_ref.dtype)
        db_ref[...] = _colsum(dy)

    return pl.pallas_call(
        body, name=name, grid=(C // CONV_TILE,),
        in_specs=[xcol, pl.BlockSpec((CONV_K, CONV_TILE), lambda j: (0, j + off)),
                  pl.BlockSpec((1, CONV_TILE), lambda j: (0, j + off)), dcol],
        out_specs=[dcol, pl.BlockSpec((CONV_K, CONV_TILE), lambda j: (0, j)), pl.BlockSpec((1, CONV_TILE), lambda j: (0, j))],
        out_shape=[jax.ShapeDtypeStruct((T, C), BF16), jax.ShapeDtypeStruct((CONV_K, C), F32), jax.ShapeDtypeStruct((1, C), F32)],
        compiler_params=_params(("parallel",)),
    )(x, w, b, dout)


def _ssd_common(x_ref, b_ref, c_ref, dtr_ref, bias_ref, alog_ref, tri_ref, triu_ref):
    Q = x_ref.shape[0]
    x, Bm, Cm = x_ref[...], b_ref[...], c_ref[...]
    pre = dtr_ref[...] + bias_ref[...]
    dt = _softplus(pre)
    A = -jnp.exp(alog_ref[...])
    adt = dt * A
    ac4 = _dot(tri_ref[...], adt, NN, HI)
    ar4 = _dot(adt, triu_ref[...], TN, HI)
    cb = _dot(Cm, Bm, NT)
    ii = lax.broadcasted_iota(jnp.int32, (Q, Q), 0)
    jj = lax.broadcasted_iota(jnp.int32, (Q, Q), 1)
    atot4 = ac4[Q - 1:Q, :]
    return x, Bm, Cm, pre, dt, A, ac4, ar4, cb, ii >= jj, atot4


def _ssd_decay(k, ac4, ar4, causal):
    seg = ac4[:, k:k + 1] - ar4[k:k + 1, :]
    return jnp.where(causal, jnp.exp(jnp.where(causal, seg, 0.0)), 0.0)


def _per_head(c4, n):
    lane = lax.broadcasted_iota(jnp.int32, (c4.shape[0], n), 1)
    out = jnp.broadcast_to(c4[:, 0:1], (c4.shape[0], n))
    for k in range(1, SSD_HPG):
        out = jnp.where(lane >= SSD_HEADDIM * k, c4[:, k:k + 1], out)
    return out


def _head_sums(v):
    n = SSD_HPG * SSD_HEADDIM
    row = lax.broadcasted_iota(jnp.int32, (n, SSD_HPG), 0)
    col = lax.broadcasted_iota(jnp.int32, (n, SSD_HPG), 1)
    member = jnp.where((row >= SSD_HEADDIM * col) & (row < SSD_HEADDIM * (col + 1)), 1.0, 0.0)
    return _dot(v, member, NN, HI)


def _head_rows(c4):
    n = SSD_HPG * SSD_HEADDIM
    row = lax.broadcasted_iota(jnp.int32, (n, 1), 0)
    out = jnp.broadcast_to(c4[:, 0:1], (n, 1))
    for k in range(1, SSD_HPG):
        out = jnp.where(row >= SSD_HEADDIM * k, c4[:, k:k + 1], out)
    return out


def _ssd_specs(T, Q, rev):
    NC = T // Q
    cc = (lambda c: NC - 1 - c) if rev else (lambda c: c)
    W, N, S = SSD_HPG * SSD_HEADDIM, SSD_STATE, SSD_STEP_GROUPS
    x_spec = pl.BlockSpec((Q, S * W), lambda g, c: (cc(c), g))
    b_spec = pl.BlockSpec((Q, S * N), lambda g, c: (cc(c), D_INNER // (S * N) + g))
    c_spec = pl.BlockSpec((Q, S * N), lambda g, c: (cc(c), (D_INNER + SSD_GROUPS * N) // (S * N) + g))
    dt_spec = pl.BlockSpec((S, Q, SSD_HPG), lambda g, c: (g, cc(c), 0))
    p_spec = pl.BlockSpec((S, 1, SSD_HPG), lambda g, c: (g, 0, 0))
    tri_spec = pl.BlockSpec((Q, Q), lambda g, c: (0, 0))
    h_spec = pl.BlockSpec((S, None, W, N), lambda g, c: (g, cc(c), 0, 0))
    return x_spec, b_spec, c_spec, dt_spec, p_spec, tri_spec, h_spec


def _ssd_views(gi, wide, narrow, lead):
    W, N = SSD_HPG * SSD_HEADDIM, SSD_STATE
    return ([r.at[:, pl.ds(W * gi, W)] for r in wide] + [r.at[:, pl.ds(N * gi, N)] for r in narrow]
            + [r.at[gi] for r in lead])


def _tri(Q):
    tri = jnp.tril(jnp.ones((Q, Q), F32))
    return tri, tri.T


def _ssd_fwd(xc, dtr8, bias8, alog8, dsk8):
    T, Q = xc.shape[0], SSD_Q
    NC = T // Q
    P, W = SSD_HEADDIM, SSD_HPG * SSD_HEADDIM
    x_spec, b_spec, c_spec, dt_spec, p_spec, tri_spec, h_spec = _ssd_specs(T, Q, False)

    def body(x_ref, b_ref, c_ref, dtr_ref, bias_ref, alog_ref, dsk_ref, tri_ref, triu_ref, y_ref, hs_ref, h_scr):
        for gi in range(SSD_STEP_GROUPS):
            xv, yv, bv, cv, dtv, biasv, alogv, dskv, hsv, hv = _ssd_views(
                gi, (x_ref, y_ref), (b_ref, c_ref), (dtr_ref, bias_ref, alog_ref, dsk_ref, hs_ref, h_scr))
            one(xv, bv, cv, dtv, biasv, alogv, dskv, tri_ref, triu_ref, yv, hsv, hv)

    def one(x_ref, b_ref, c_ref, dtr_ref, bias_ref, alog_ref, dsk_ref, tri_ref, triu_ref, y_ref, hs_ref, h_scr):
        @pl.when(pl.program_id(1) == 0)
        def _():
            h_scr[...] = jnp.zeros_like(h_scr)

        hin = h_scr[...]
        hs_ref[...] = hin
        x, Bm, Cm, pre, dt, A, ac4, ar4, cb, causal, atot4 = _ssd_common(
            x_ref, b_ref, c_ref, dtr_ref, bias_ref, alog_ref, tri_ref, triu_ref)
        xdt = x * _per_head(dt, W)
        rest = _dot(Cm, hin, NT) * _per_head(jnp.exp(ac4), W) + _per_head(dsk_ref[...], W) * x
        for k in range(SSD_HPG):
            blk = slice(P * k, P * (k + 1))
            G = cb * _ssd_decay(k, ac4, ar4, causal)
            y_ref[:, blk] = _dot(G, xdt[:, blk], NN) + rest[:, blk]
        xw = xdt * _per_head(jnp.exp(atot4 - ac4), W)
        h_scr[...] = _head_rows(jnp.exp(atot4)) * hin + _dot(xw, Bm, TN)

    tri, triu = _tri(Q)
    return pl.pallas_call(
        body, name="ssd_fwd", grid=(SSD_GROUPS // SSD_STEP_GROUPS, NC),
        in_specs=[x_spec, b_spec, c_spec, dt_spec, p_spec, p_spec, p_spec, tri_spec, tri_spec],
        out_specs=[pl.BlockSpec((Q, SSD_STEP_GROUPS * W), lambda g, c: (c, g)), h_spec],
        out_shape=[jax.ShapeDtypeStruct((T, D_INNER), F32),
                   jax.ShapeDtypeStruct((SSD_GROUPS, NC, W, SSD_STATE), F32)],
        scratch_shapes=[pltpu.VMEM((SSD_STEP_GROUPS, W, SSD_STATE), F32)],
        compiler_params=_params(("parallel", "arbitrary")),
    )(xc, xc, xc, dtr8, bias8, alog8, dsk8, tri, triu)


def _ssd_bwd(xc, dtr8, bias8, alog8, dsk8, hs, dy):
    T, Q = xc.shape[0], SSD_Q
    NC = T // Q
    P, W = SSD_HEADDIM, SSD_HPG * SSD_HEADDIM
    x_spec, b_spec, c_spec, dt_spec, p_spec, tri_spec, h_spec = _ssd_specs(T, Q, True)
    dy_spec = pl.BlockSpec((Q, SSD_STEP_GROUPS * W), lambda g, c: (NC - 1 - c, g))
    dbc_spec = pl.BlockSpec((Q, SSD_STEP_GROUPS * SSD_STATE), lambda g, c: (NC - 1 - c, g))

    def body(x_ref, b_ref, c_ref, dtr_ref, bias_ref, alog_ref, dsk_ref, tri_ref, triu_ref, hs_ref, dy_ref,
             dx_ref, db_ref, dc_ref, ddtr_ref, dbias_ref, dalog_ref, ddsk_ref, dh_scr):
        for gi in range(SSD_STEP_GROUPS):
            (xv, dyv, dxv, bv, cv, dbv, dcv, dtv, biasv, alogv, dskv, hsv, ddtv, dbiasv, dalogv, ddskv, dhv) = _ssd_views(
                gi, (x_ref, dy_ref, dx_ref), (b_ref, c_ref, db_ref, dc_ref),
                (dtr_ref, bias_ref, alog_ref, dsk_ref, hs_ref, ddtr_ref, dbias_ref, dalog_ref, ddsk_ref, dh_scr))
            one(xv, bv, cv, dtv, biasv, alogv, dskv, tri_ref, triu_ref, hsv, dyv,
                dxv, dbv, dcv, ddtv, dbiasv, dalogv, ddskv, dhv)

    def one(x_ref, b_ref, c_ref, dtr_ref, bias_ref, alog_ref, dsk_ref, tri_ref, triu_ref, hs_ref, dy_ref,
            dx_ref, db_ref, dc_ref, ddtr_ref, dbias_ref, dalog_ref, ddsk_ref, dh_scr):
        @pl.when(pl.program_id(1) == 0)
        def _():
            dh_scr[...] = jnp.zeros_like(dh_scr)
            dbias_ref[...] = jnp.zeros_like(dbias_ref)
            dalog_ref[...] = jnp.zeros_like(dalog_ref)
            ddsk_ref[...] = jnp.zeros_like(ddsk_ref)

        x, Bm, Cm, pre, dt, A, ac4, ar4, cb, causal, atot4 = _ssd_common(
            x_ref, b_ref, c_ref, dtr_ref, bias_ref, alog_ref, tri_ref, triu_ref)
        dyv, hin, dho = dy_ref[...], hs_ref[...], dh_scr[...]
        dt_w = _per_head(dt, W)
        xdt = x * dt_w
        E4, F4, etot4 = jnp.exp(ac4), jnp.exp(atot4 - ac4), jnp.exp(atot4)
        F_w = _per_head(F4, W)
        ddsk_ref[...] += _head_sums(_colsum(dyv * x))
        Z = _dot(Cm, hin, NT)
        dZ = dyv * _per_head(E4, W)
        dac4 = _head_sums(dyv * Z) * E4
        dC = _dot(dZ, hin, NN)
        dh_scr[...] = _dot(dZ, Cm, TN) + _head_rows(etot4) * dho
        per_row = jnp.sum(dho * hin, axis=1, keepdims=True)
        lane4 = lax.broadcasted_iota(jnp.int32, (1, SSD_HPG), 1)
        datot4 = jnp.zeros((1, SSD_HPG), F32)
        for k in range(SSD_HPG):
            datot4 = jnp.where(lane4 == k, jnp.sum(per_row[P * k:P * (k + 1)], keepdims=True), datot4)
        datot4 = datot4 * etot4
        dxw = _dot(Bm, dho, NT)
        dB = _dot(xdt * F_w, dho, NN)
        dFa4 = _head_sums(dxw * xdt) * F4
        datot4 = datot4 + _colsum(dFa4)
        dac4 = dac4 - dFa4
        dcb = jnp.zeros((Q, Q), F32)
        lane_q4 = lax.broadcasted_iota(jnp.int32, (Q, SSD_HPG), 1)
        sub_4q = lax.broadcasted_iota(jnp.int32, (SSD_HPG, Q), 0)
        row_sums = jnp.zeros((Q, SSD_HPG), F32)
        col_sums = jnp.zeros((SSD_HPG, Q), F32)
        dxdt_heads = []
        for k in range(SSD_HPG):
            blk = slice(P * k, P * (k + 1))
            L = _ssd_decay(k, ac4, ar4, causal)
            G = cb * L
            dG = _dot(dyv[:, blk], xdt[:, blk], NT)
            dxdt_heads.append(_dot(G, dyv[:, blk], TN))
            dcb = dcb + dG * L
            Mseg = dG * G
            row_sums = jnp.where(lane_q4 == k, jnp.sum(Mseg, axis=1, keepdims=True), row_sums)
            col_sums = jnp.where(sub_4q == k, jnp.sum(Mseg, axis=0, keepdims=True), col_sums)
        dxdt = dxw * F_w + jnp.concatenate(dxdt_heads, axis=1)
        last = lax.broadcasted_iota(jnp.int32, (Q, SSD_HPG), 0) == Q - 1
        dac4 = dac4 + row_sums + jnp.where(last, datot4, 0.0)
        dadt4 = _dot(triu_ref[...], dac4, NN, HI) - _dot(triu_ref[...], col_sums, NT, HI)
        ddt4 = _head_sums(dxdt * x) + dadt4 * A
        dalog_ref[...] += _colsum(dadt4 * dt) * A
        ddtr4 = ddt4 * jax.nn.sigmoid(pre)
        dbias_ref[...] += _colsum(ddtr4)
        ddtr_ref[...] = ddtr4
        dx_ref[...] = (_per_head(dsk_ref[...], W) * dyv + dxdt * dt_w).astype(dx_ref.dtype)
        db_ref[...] = dB + _dot(dcb, Cm, TN)
        dc_ref[...] = dC + _dot(dcb, Bm, NN)

    tri, triu = _tri(Q)
    return pl.pallas_call(
        body, name="ssd_bwd", grid=(SSD_GROUPS // SSD_STEP_GROUPS, NC),
        in_specs=[x_spec, b_spec, c_spec, dt_spec, p_spec, p_spec, p_spec, tri_spec, tri_spec, h_spec, dy_spec],
        out_specs=[dy_spec, dbc_spec, dbc_spec, dt_spec, p_spec, p_spec, p_spec],
        out_shape=[jax.ShapeDtypeStruct((T, D_INNER), F32),
                   jax.ShapeDtypeStruct((T, SSD_GROUPS * SSD_STATE), F32),
                   jax.ShapeDtypeStruct((T, SSD_GROUPS * SSD_STATE), F32),
                   jax.ShapeDtypeStruct(dtr8.shape, F32)] + [jax.ShapeDtypeStruct(bias8.shape, F32)] * 3,
        scratch_shapes=[pltpu.VMEM((SSD_STEP_GROUPS, W, SSD_STATE), F32)],
        compiler_params=_params(("parallel", "arbitrary")),
    )(xc, xc, xc, dtr8, bias8, alog8, dsk8, tri, triu, hs, dy)


def _to_groups(v):
    return v.reshape(SSD_GROUPS, 1, SSD_HPG)


def _mixer_fwd(x, p, w):
    T, D = x.shape
    (hn,) = _rowwise(lambda xt, g: (_rms(xt, g),), [x], [p["mix_norm"]], [(D, BF16)], name="mix_norm")
    winT = w["w_in"]
    u, z, xbc, dtr, gl = [
        _matmul(hn, winT[IN_OFFS[i]:IN_OFFS[i + 1]], "nt", name=f"mix_in{i}") for i in range(5)]
    tiles = _s5_derive(*[p[k][0] for k in ("s5_A_re", "s5_A_im", "s5_log_dt", "s5_B_re", "s5_B_im", "s5_C_re", "s5_C_im")])
    u = _to_seg(u)
    ys = _s5_fwd(u, tiles)
    (g,) = _rowwise(lambda yt, ut, d: (jax.nn.gelu(yt + d * ut),), [ys, u], [p["s5_D"]], [(D, F32)], name="s5_gelu")
    y5 = _matmul(g, w["s5_w_glu"], "nn", name="s5_glu", out=(BF16,),
                 epilogue=lambda acc, gt, b: (gt * jax.nn.sigmoid(acc + b),), extras=[g, p["s5_b_glu"]])
    xc = _conv_fwd(xbc, w["conv_w"], p["conv_b"])
    dtr8 = dtr.reshape(T, SSD_GROUPS, SSD_HPG).transpose(1, 0, 2)
    ssd_p = [_to_groups(p[k]) for k in ("ssd_dt_bias", "ssd_A_log", "ssd_D")]
    yssd_raw, hs = _ssd_fwd(xc, dtr8, *ssd_p)
    (yssd,) = _rowwise(lambda yt, zt, nw: (_rms(yt * jax.nn.silu(zt), nw),), [yssd_raw, z], [p["ssd_norm"]],
                       [(D_INNER, BF16)], name="ssd_gate")
    p5 = _from_seg(_matmul(y5, w["w_proj_s5"], "nn", name="mix_p5"))
    pssd = _matmul(yssd, w["w_proj_ssd"], "nn", name="mix_pssd")

    def merge(glt, at, bt, bg):
        gates = jax.nn.sigmoid(glt + bg)
        return (gates[:, :D] * at + gates[:, D:] * bt,)

    (merged,) = _rowwise(merge, [gl, p5, pssd], [p["b_gate"]], [(D, BF16)], name="mix_merge")
    y = _matmul(merged, w["w_out"], "nn", name="mix_out", epilogue=lambda acc, xt: (xt + acc,), extras=[x])
    res = dict(hn=hn, u=u, z=z, xbc=xbc, gl=gl, tiles=tiles, ys=ys, g=g, y5=y5, xc=xc, dtr8=dtr8, ssd_p=ssd_p,
               yssd_raw=yssd_raw, hs=hs, yssd=yssd, p5=p5, pssd=pssd, merged=merged)
    return y, res


def _mixer_bwd(dy, dy16, x, p, w, r):
    T, D = x.shape
    gp = {}
    dmerged = _matmul(dy16, w["w_out"], "nt", name="mix_d_merged")
    gp["w_out"] = _matmul(r["merged"], dy16, "tn", name="mix_d_wout")

    def merge_bwd(glt, at, bt, dm, bg):
        def f(q, a_, b_):
            gates = jax.nn.sigmoid(q + bg)
            return gates[:, :D] * a_ + gates[:, D:] * b_
        _, vjp = jax.vjp(f, glt, at, bt)
        dq, da_, db_ = vjp(dm)
        return da_, db_, dq, _colsum(dq)

    dp5, dpssd, dgl, gp["b_gate"] = _rowwise(
        merge_bwd, [r["gl"], r["p5"], r["pssd"], dmerged], [p["b_gate"]],
        [(D, BF16), (D, BF16), (2 * D, BF16)], [((1, 2 * D), F32)], name="mix_d_merge")
    dyssd = _matmul(dpssd, w["w_proj_ssd"], "nt", name="mix_d_yssd")
    gp["w_proj_ssd"] = _matmul(r["yssd"], dpssd, "tn", name="mix_d_wpssd")

    def gate_bwd(yt, zt, dyt, nw):
        _, vjp = jax.vjp(lambda a_, b_, c_: _rms(a_ * jax.nn.silu(b_), c_), yt, zt, nw)
        return vjp(dyt)

    dyraw, dz, gp["ssd_norm"] = _rowwise(
        gate_bwd, [r["yssd_raw"], r["z"], dyssd], [p["ssd_norm"]],
        [(D_INNER, F32), (D_INNER, BF16)], [((1, D_INNER), F32)], name="ssd_d_gate")
    dxs, dBm, dCm, ddtr8, dbias8, dalog8, ddsk8 = _ssd_bwd(r["xc"], r["dtr8"], *r["ssd_p"], r["hs"], dyraw)
    gp["ssd_dt_bias"], gp["ssd_A_log"], gp["ssd_D"] = [v.reshape(1, SSD_HEADS) for v in (dbias8, dalog8, ddsk8)]
    ddtr = ddtr8.transpose(1, 0, 2).reshape(T, SSD_HEADS)
    conv = [_conv_bwd(r["xbc"], w["conv_w"], p["conv_b"], d, c0, f"conv_bwd{i}")
            for i, (d, c0) in enumerate(((dxs, 0), (dBm, D_INNER), (dCm, D_INNER + SSD_GROUPS * SSD_STATE)))]
    dxbc = [c[0] for c in conv]
    gp["conv_w"] = jnp.concatenate([c[1] for c in conv], axis=1)
    gp["conv_b"] = jnp.concatenate([c[2] for c in conv], axis=1)

    dp5 = _to_seg(dp5)
    dy5 = _matmul(dp5, w["w_proj_s5"], "nt", name="mix_d_y5")
    gp["w_proj_s5"] = _matmul(r["y5"], dp5, "tn", name="mix_d_wp5")
    g = r["g"]

    def glu_ep(acc, gt, b, dyt):
        _, vjp = jax.vjp(lambda g_, t_: g_ * jax.nn.sigmoid(t_ + b), gt, acc)
        return vjp(dyt)

    dg1, dt_ = _matmul(g, w["s5_w_glu"], "nn", name="s5_d_glu", out=(F32, BF16), epilogue=glu_ep,
                       extras=[g, p["s5_b_glu"], dy5])
    gp["s5_w_glu"] = _matmul(g, dt_, "tn", name="s5_d_wglu")
    dg = _matmul(dt_, w["s5_w_glu"], "nt", name="s5_d_g", epilogue=lambda acc, e: (acc + e,), extras=[dg1])

    def gelu_bwd(yt, ut, dgt, dtt, d):
        _, vjp = jax.vjp(lambda y_, d_: jax.nn.gelu(y_ + d_ * ut), yt, d)
        dys, dd = vjp(dgt)
        return dys, dys * d, dd, _colsum(dtt.astype(F32))

    dys, dusk, gp["s5_D"], gp["s5_b_glu"] = _rowwise(
        gelu_bwd, [r["ys"], r["u"], dg, dt_], [p["s5_D"]], [(D, F32), (D, F32)], [((1, D), F32), ((1, D), F32)],
        name="s5_d_gelu")
    du, *dtiles = _s5_bwd(r["u"], dys, dusk, r["tiles"])
    du = _from_seg(du)

    pieces = [du, dz, None, ddtr, dgl]
    winT = w["w_in"]
    hn = r["hn"]
    d_rows, dhn = [], None
    cols = [(du, 0, 1024), (dz, 1024, 3072), (dxbc[0], 3072, 5120), (dxbc[1], 5120, 6144), (dxbc[2], 6144, 7168),
            (ddtr, 7168, 7200), (dgl, 7200, 9248)]
    for i, (d, lo, hi) in enumerate(cols):
        d_rows.append(_matmul(d, hn, "tn", name=f"mix_d_win{i}"))
        if dhn is None:
            dhn = _matmul(d, winT[lo:hi], "nn", name=f"mix_d_hn{i}")
        else:
            dhn = _matmul(d, winT[lo:hi], "nn", name=f"mix_d_hn{i}", epilogue=lambda acc, e: (acc + e,), extras=[dhn])
    gp["w_in"] = jnp.concatenate(d_rows, axis=0)
    del pieces

    def norm_bwd(xt, dh, dyt, gn):
        _, vjp = jax.vjp(_rms, xt, gn)
        dx, dgn = vjp(dh)
        return dyt + dx, dyt + dx, dgn

    dx, dx16, gp["mix_norm"] = _rowwise(norm_bwd, [x, dhn, dy], [p["mix_norm"]], [(D, F32), (D, BF16)],
                                        [((1, D), F32)], name="mix_d_norm")
    return dx, dx16, gp, dtiles


def _s5_param_grads(p, dtiles):
    keys = ("s5_A_re", "s5_A_im", "s5_log_dt", "s5_B_re", "s5_B_im", "s5_C_re", "s5_C_im")
    _, vjp = jax.vjp(_s5_derive, *[p[k][0] for k in keys])
    return {k: v[None] for k, v in zip(keys, vjp(tuple(dtiles)))}


def _local_step(x, target, p, w, late_weights, exchange):
    T, D = x.shape
    x1, r1 = _ffn_fwd(x, p["ffn1_norm"], w["ffn1_w_gate"], w["ffn1_w_up"], w["ffn1_w_down"], "ffn1")
    x2, rm = _mixer_fwd(x1, p, w)
    w = {**w, **late_weights(rm["p5"])}
    x3, r2 = _ffn_fwd(x2, p["ffn2_norm"], w["ffn2_w_gate"], w["ffn2_w_up"], w["ffn2_w_down"], "ffn2")

    def head(xt, tt, g):
        def f(x_, g_):
            e = _rms(x_, g_) - tt
            return 0.5 * jnp.sum(jnp.mean(e * e, axis=-1))
        l, (dx_, dg_) = jax.value_and_grad(f, argnums=(0, 1))(xt, g)
        return dx_, dx_, l.reshape(1, 1), dg_

    dx3, dx3_16, loss, d_final = _rowwise(head, [x3, target], [p["final_norm"]], [(D, F32), (D, BF16)],
                                          [((1, 1), F32), ((1, D), F32)], name="loss_head")
    gp = {"final_norm": d_final}

    def ffn_exchange(tag):
        return lambda d_wgT, d_wuT, d_wd, after: exchange(
            tag, {f"{tag}_w_gate": d_wgT, f"{tag}_w_up": d_wuT, f"{tag}_w_down": d_wd}, after)

    dx2, dx2_16, gp["ffn2_norm"] = _ffn_bwd(
        dx3, dx3_16, x2, p["ffn2_norm"], w["ffn2_w_gate"], w["ffn2_w_up"], w["ffn2_w_down"], r2, "ffn2", (),
        ffn_exchange("ffn2"))
    dx1, dx1_16, gm, dtiles = _mixer_bwd(dx2, dx2_16, x1, p, w, rm)
    conv_w_grad = gm.pop("conv_w")
    started = exchange("mixer", {k: gm.pop(k) for k in ("w_out", "w_proj_s5", "w_proj_ssd", "s5_w_glu", "w_in")}, dx1)
    gp.update(gm)
    gp.update(_s5_param_grads(p, dtiles))
    dx0, _, gp["ffn1_norm"] = _ffn_bwd(
        dx1, dx1_16, x, p["ffn1_norm"], w["ffn1_w_gate"], w["ffn1_w_up"], w["ffn1_w_down"], r1, "ffn1", started,
        ffn_exchange("ffn1"))
    return loss, dx0, gp, conv_w_grad


ADAMW_BLOCK = 1 << 18


def _adamw(w, m, v, gs, *, name, emit_g):
    _, R, C = w.shape
    n_g = len(gs)
    n_o = 4 if emit_g else 3
    tm = _rtile(R, max(8, ADAMW_BLOCK // C))
    lead = pl.BlockSpec((None, tm, C), lambda i: (0, i, 0))

    def body(w_ref, m_ref, v_ref, *refs):
        g = refs[0][...]
        for r in refs[1:n_g]:
            g = g + r[...]
        m2 = ADAM_B1 * m_ref[...] + (1.0 - ADAM_B1) * g
        v2 = ADAM_B2 * v_ref[...] + (1.0 - ADAM_B2) * (g * g)
        m_hat = m2 / (1.0 - ADAM_B1 ** ADAM_STEP)
        v_hat = v2 / (1.0 - ADAM_B2 ** ADAM_STEP)
        delta = -ADAM_LR * (m_hat / (jnp.sqrt(v_hat) + ADAM_EPS) + ADAM_WD * w_ref[...])
        for r, val in zip(refs[n_g:], ((g,) if emit_g else ()) + (delta, m2, v2)):
            r[...] = val

    return pl.pallas_call(
        body, name=name, grid=(R // tm,),
        in_specs=[lead] * 3 + [pl.BlockSpec((tm, C), lambda i: (i, 0))] * n_g, out_specs=[lead] * n_o,
        out_shape=[jax.ShapeDtypeStruct((1, R, C), F32)] * n_o,
        compiler_params=_params(("parallel",)),
    )(w, m, v, *gs)


def _rtile(r, cap=512):
    if r <= cap:
        return r
    best = None
    for t in range(8, cap + 1, 8):
        if r % t == 0:
            best = t
    assert best is not None, r
    return best


ANY = pl.BlockSpec(memory_space=pl.ANY)


def _where_am_i():
    x, y, c = lax.axis_index("x"), lax.axis_index("y"), lax.axis_index("c")
    ks = (2 * x + y, 2 * (1 - x) + y, 2 * x + (1 - y), 2 * (1 - x) + (1 - y))
    return x, y, c, ks


def _rc(src, dst, ssem, rsem, to):
    return pltpu.make_async_remote_copy(src_ref=src, dst_ref=dst, send_sem=ssem, recv_sem=rsem,
                                        device_id=to, device_id_type=MESH)


def _gather_shards(srcs):
    n = len(srcs)

    def body(*refs):
        _gather_body(refs[:n], refs[n:2 * n], *refs[2 * n:], extra=())

    return pl.pallas_call(
        body, name="gather_shards",
        in_specs=[ANY] * n, out_specs=[ANY] * n,
        out_shape=[jax.ShapeDtypeStruct((4,) + s.shape, s.dtype) for s in srcs],
        scratch_shapes=[pltpu.SemaphoreType.DMA((n, 6)), pltpu.SemaphoreType.DMA((n, 6)), pltpu.SemaphoreType.DMA((n,))],
        compiler_params=pltpu.CompilerParams(has_side_effects=True),
    )(*srcs)


def _handshake_all():
    x, y, c = lax.axis_index("x"), lax.axis_index("y"), lax.axis_index("c")
    barrier = pltpu.get_barrier_semaphore()
    for dx, dy, dc in itertools.product((0, 1), repeat=3):
        if (dx, dy, dc) != (0, 0, 0):
            to = (1 - x if dx else x, 1 - y if dy else y, 1 - c if dc else c)
            pl.semaphore_signal(barrier, inc=1, device_id=to, device_id_type=MESH)
    pl.semaphore_wait(barrier, 7)


SEQUENCER = dict(axis_name="seq", num_cores=1)


def _gather_shards_async(srcs, token, name, collective_id):
    n = len(srcs)

    def body(*refs):
        tok, tok_out = refs[n], refs[2 * n + 1]
        ssem, rsem, lsem = refs[2 * n + 2:]
        _handshake_all()
        _gather_body(refs[:n], refs[n + 1:2 * n + 1], ssem, rsem, lsem,
                     extra=(pltpu.make_async_copy(tok, tok_out, lsem.at[n]),))

    res = pl.kernel(
        body, name=name,
        out_type=[jax.ShapeDtypeStruct((4,) + s.shape, s.dtype) for s in srcs]
        + [jax.ShapeDtypeStruct(token.shape, token.dtype)],
        mesh=plsc.ScalarSubcoreMesh(**SEQUENCER),
        scratch_types=[pltpu.SemaphoreType.DMA((n, 6)), pltpu.SemaphoreType.DMA((n, 6)), pltpu.SemaphoreType.DMA((n + 1,))],
        compiler_params=pltpu.CompilerParams(collective_id=collective_id),
    )(*srcs, token)
    return res[:n], res[n]


def _gather_body(src, out, ssem, rsem, lsem, extra):
        n = len(src)
        x, y, c, (k_me, k_x, k_y, k_d) = _where_am_i()
        sib = (x, y, 1 - c)
        local = [pltpu.make_async_copy(src[i], out[i].at[k_me], lsem.at[i]) for i in range(n)] + list(extra)
        for cp in local:
            cp.start()

        def own(i, q, to):
            return _rc(src[i], out[i].at[k_me], ssem.at[i, q], rsem.at[i, q], to)

        def slab(i, q, k, to):
            return _rc(out[i].at[k], out[i].at[k], ssem.at[i, q], rsem.at[i, q], to)

        @pl.when(c == 1)
        def _():
            sends = [own(i, 0, (1 - x, y, 1)) for i in range(n)]
            for cp in sends:
                cp.start()
            fwd = [slab(i, 3, k_x, sib) for i in range(n)]
            for i in range(n):
                slab(i, 0, k_x, sib).wait_recv()
                fwd[i].start()
            for i in range(n):
                slab(i, 4, k_y, sib).wait_recv()
                slab(i, 5, k_d, sib).wait_recv()
            for cp in sends + fwd:
                cp.wait_send()

        @pl.when(c == 0)
        def _():
            sends = [own(i, 1, (x, 1 - y, 0)) for i in range(n)] + [own(i, 2, (1 - x, 1 - y, 0)) for i in range(n)]
            for cp in sends:
                cp.start()
            fwd = [slab(i, 4, k_y, sib) for i in range(n)] + [slab(i, 5, k_d, sib) for i in range(n)]
            for i in range(n):
                slab(i, 1, k_y, sib).wait_recv()
                fwd[i].start()
            for i in range(n):
                slab(i, 2, k_d, sib).wait_recv()
                fwd[n + i].start()
            for i in range(n):
                slab(i, 3, k_x, sib).wait_recv()
            for cp in sends + fwd:
                cp.wait_send()

        for cp in local:
            cp.wait()


def _swap_slabs(arrs, n_slabs, name, async_id, token):
    n = len(arrs)
    J = max(n_slabs, 1)

    def body(*refs):
        src, out = refs[:n], refs[n:2 * n]
        ssem, rsem = refs[2 * n:]
        x, y, c, (k_me, k_x, k_y, k_d) = _where_am_i()
        sib = (x, y, 1 - c)
        sel = (jnp.where(c == 1, k_y, k_me), jnp.where(c == 1, k_d, k_x))
        cps = []
        for i in range(n):
            for j in range(J):
                s = src[i].at[sel[j]] if n_slabs else src[i]
                d = out[i].at[j] if n_slabs else out[i]
                cps.append(_rc(s, d, ssem.at[i, j], rsem.at[i, j], sib))
        for cp in cps:
            cp.start()
        for cp in cps:
            cp.wait()

    return _exchange_call(
        body, arrs, [jax.ShapeDtypeStruct(((n_slabs,) if n_slabs else ()) + a.shape[-2:], a.dtype) for a in arrs],
        [pltpu.SemaphoreType.DMA((n, J)), pltpu.SemaphoreType.DMA((n, J))], name, async_id, token)


def _exchange_call(body, arrs, out_shape, sems, name, collective_id, token):
    n, m = len(arrs), len(out_shape)

    def seq_body(*refs):
        tok, tok_out, tok_sem = refs[n], refs[n + 1 + m], refs[-1]
        _handshake_all()
        cp = pltpu.make_async_copy(tok, tok_out, tok_sem)
        cp.start()
        body(*refs[:n], *refs[n + 1:n + 1 + m], *refs[n + 2 + m:-1])
        cp.wait()

    res = pl.kernel(
        seq_body, name=name, out_type=list(out_shape) + [jax.ShapeDtypeStruct(token.shape, token.dtype)],
        mesh=plsc.ScalarSubcoreMesh(**SEQUENCER), scratch_types=list(sems) + [pltpu.SemaphoreType.DMA],
        compiler_params=pltpu.CompilerParams(collective_id=collective_id))(*arrs, token)
    return res[:m], res[m]


def _send_chip_sums(arrs, name, async_id, token):
    n = len(arrs)

    def body(*refs):
        src, out = refs[:n], refs[n:2 * n]
        ssem, rsem = refs[2 * n:]
        x, y, c, _ = _where_am_i()

        @pl.when(c == 1)
        def _():
            cps = [_rc(src[i].at[1], out[i].at[0], ssem.at[i, 0], rsem.at[i, 0], (1 - x, y, 1)) for i in range(n)]
            for cp in cps:
                cp.start()
            for cp in cps:
                cp.wait()

        @pl.when(c == 0)
        def _():
            cps = [_rc(src[i].at[0], out[i].at[0], ssem.at[i, 0], rsem.at[i, 0], (x, 1 - y, 0)) for i in range(n)]
            cps += [_rc(src[i].at[1], out[i].at[1], ssem.at[i, 1], rsem.at[i, 1], (1 - x, 1 - y, 0)) for i in range(n)]
            for cp in cps:
                cp.start()
            for cp in cps:
                cp.wait()

    return _exchange_call(
        body, arrs, [jax.ShapeDtypeStruct(a.shape, a.dtype) for a in arrs],
        [pltpu.SemaphoreType.DMA((n, 2)), pltpu.SemaphoreType.DMA((n, 2))], name, async_id, token)


def _chip_sum(g, recv, sel, name):
    _, r, C = g.shape
    tr = _rtile(r)

    def body(sel_ref, g_ref, r_ref, o32_ref, o16_ref):
        s = g_ref[...] + r_ref[...]
        o32_ref[...] = s
        o16_ref[...] = s.astype(BF16)

    blk = pl.BlockSpec((None, tr, C), lambda j, t, sel_ref: (j, t, 0))
    return pl.pallas_call(
        body, name=name,
        grid_spec=pltpu.PrefetchScalarGridSpec(
            num_scalar_prefetch=1, grid=(2, r // tr),
            in_specs=[pl.BlockSpec((None, tr, C), lambda j, t, sel_ref: (sel_ref[j], t, 0)), blk],
            out_specs=[blk, blk]),
        out_shape=[jax.ShapeDtypeStruct((2, r, C), F32), jax.ShapeDtypeStruct((2, r, C), BF16)],
        compiler_params=_params(("arbitrary", "arbitrary")),
    )(sel, g, recv)


def _cross_sum(s32, got, north, after, name):
    _, r, C = s32.shape
    tr = _rtile(r)

    def body(f_ref, p_ref, a_ref, b_ref, after_ref, o_ref):
        a = a_ref[...].astype(F32)

        @pl.when(f_ref[0] == 1)
        def _():
            o_ref[...] = p_ref[...] + a

        @pl.when(f_ref[0] == 0)
        def _():
            o_ref[...] = a + b_ref[...].astype(F32)

    return pl.pallas_call(
        body, name=name,
        grid_spec=pltpu.PrefetchScalarGridSpec(
            num_scalar_prefetch=1, grid=(r // tr,),
            in_specs=[pl.BlockSpec((None, tr, C), lambda t, f: (0, t, 0)),
                      pl.BlockSpec((None, tr, C), lambda t, f: (0, t, 0)),
                      pl.BlockSpec((None, tr, C), lambda t, f: (1 - f[0], t, 0)), ANY],
            out_specs=pl.BlockSpec((tr, C), lambda t, f: (t, 0))),
        out_shape=jax.ShapeDtypeStruct((r, C), F32),
        compiler_params=_params(("arbitrary",)),
    )(north, s32, got, got, after)


class _Reducer:
    def __init__(self, token):
        self.token, self.pending, self.pairs = token, None, {}

    def begin(self, tag, ids, grads, after):
        names = list(grads)
        arrs = [grads[n].reshape(4, -1, D_MODEL) for n in names]
        x, y, c, (k_me, k_x, k_y, k_d) = _where_am_i()
        sel = jnp.stack([jnp.where(c == 1, k_me, k_y), jnp.where(c == 1, k_x, k_d)]).astype(jnp.int32)
        from_sib, self.token = _swap_slabs(arrs, 2, f"swap_grad_slabs_{tag}", ids[0], self.token)
        sums = [_chip_sum(g, r, sel, f"chip_sum_{tag}{i}") for i, (g, r) in enumerate(zip(arrs, from_sib))]
        self.finish(after)
        got, self.token = _send_chip_sums([s16 for _, s16 in sums], f"send_chip_sums_{tag}", ids[1], self.token)
        self.pending = (tag, ids[2], names, sums, got)
        return tuple(s16 for _, s16 in sums)

    def finish(self, after):
        if self.pending is None:
            return
        tag, cid, names, sums, got = self.pending
        north = jnp.reshape(lax.axis_index("c"), (1,)).astype(jnp.int32)
        parts = [_cross_sum(s32, b, north, after, f"cross_sum_{tag}{i}") for i, ((s32, _), b) in enumerate(zip(sums, got))]
        theirs, self.token = _swap_slabs(parts, 0, f"swap_parts_{tag}", cid, self.token)
        self.pairs.update(zip(names, zip(parts, theirs)))
        self.pending = None


def _all_reduce_small(v):
    R, C = v.shape

    def body(v_ref, o_ref, buf, ssem, rsem, lsem):
        x, y, c = lax.axis_index("x"), lax.axis_index("y"), lax.axis_index("c")
        me, sib = (x, y, c), (x, y, 1 - c)
        chips = [(1 - x, y), (x, 1 - y), (1 - x, 1 - y)]

        def slot(px, py, pc):
            return buf.at[4 * px + 2 * py + pc]

        def copy(k, block, to, src=None):
            return _rc(slot(*block) if src is None else src, slot(*block), ssem.at[k], rsem.at[k], to)

        mine = pltpu.make_async_copy(v_ref, slot(*me), lsem)
        mine.start()
        first = [copy(0, me, sib, src=v_ref)] + [copy(1 + j, me, (*chip, c), src=v_ref) for j, chip in enumerate(chips)]
        for cp in first:
            cp.start()
        passed = [copy(4 + j, (*chip, c), sib) for j, chip in enumerate(chips)]
        for j, chip in enumerate(chips):
            copy(1 + j, (*chip, c), me).wait_recv()
            passed[j].start()
        copy(0, sib, me).wait_recv()
        for j, chip in enumerate(chips):
            copy(4 + j, (*chip, 1 - c), me).wait_recv()
        for cp in first + passed:
            cp.wait_send()
        mine.wait()
        acc = buf[0]
        for k in range(1, 8):
            acc = acc + buf[k]
        o_ref[...] = acc

    return pl.pallas_call(
        body, name="all_reduce_small",
        in_specs=[pl.BlockSpec(memory_space=pltpu.VMEM)], out_specs=pl.BlockSpec(memory_space=pltpu.VMEM),
        out_shape=jax.ShapeDtypeStruct((R, C), F32),
        scratch_shapes=[pltpu.VMEM((8, R, C), F32), pltpu.SemaphoreType.DMA((7,)), pltpu.SemaphoreType.DMA((7,)),
                        pltpu.SemaphoreType.DMA],
        compiler_params=pltpu.CompilerParams(has_side_effects=True, vmem_limit_bytes=VMEM_LIMIT),
    )(v)


WEIGHTS = ['ffn1_norm', 'ffn1_w_gate', 'ffn1_w_up', 'ffn1_w_down', 'mix_norm', 'w_in', 'conv_w', 'conv_b', 's5_A_re',
           's5_A_im', 's5_log_dt', 's5_B_re', 's5_B_im', 's5_C_re', 's5_C_im', 's5_D', 's5_w_glu', 's5_b_glu',
           'ssd_A_log', 'ssd_dt_bias', 'ssd_D', 'ssd_norm', 'w_proj_s5', 'w_proj_ssd', 'b_gate', 'w_out', 'ffn2_norm',
           'ffn2_w_gate', 'ffn2_w_up', 'ffn2_w_down', 'final_norm']
ARGS = ['x'] + WEIGHTS + ['loss_target'] + ['m_' + n for n in WEIGHTS] + ['v_' + n for n in WEIGHTS]
COL_SHARDED = ('ffn1_w_gate', 'ffn1_w_up', 'w_in', 'ffn2_w_gate', 'ffn2_w_up')
ROW_SHARDED = ('ffn1_w_down', 's5_w_glu', 'w_proj_s5', 'w_proj_ssd', 'w_out', 'ffn2_w_down')
MATRICES = COL_SHARDED + ROW_SHARDED
GATHER_FIRST = ('ffn1_w_gate', 'ffn1_w_up', 'ffn1_w_down')
GATHER_SECOND = ('w_in', 'conv_w', 's5_w_glu', 'w_proj_s5', 'w_proj_ssd', 'w_out')
GATHER_THIRD = ('ffn2_w_gate', 'ffn2_w_up', 'ffn2_w_down')
SMALL = [n for n in WEIGHTS if n not in MATRICES]


def _pack(arrs, width=1024):
    flat = jnp.concatenate([a.reshape(-1).astype(F32) for a in arrs])
    rows = -(-flat.shape[0] // (8 * width)) * 8
    return jnp.pad(flat, (0, rows * width - flat.shape[0])).reshape(rows, width)


def _unpack(packed, shapes):
    flat, out, o = packed.reshape(-1), [], 0
    for s in shapes:
        n = math.prod(s)
        out.append(flat[o:o + n].reshape(s))
        o += n
    return out


def kernel(x, ffn1_norm, ffn1_w_gate, ffn1_w_up, ffn1_w_down, mix_norm, w_in, conv_w, conv_b, s5_A_re, s5_A_im, s5_log_dt, s5_B_re, s5_B_im, s5_C_re, s5_C_im, s5_D, s5_w_glu, s5_b_glu, ssd_A_log, ssd_dt_bias, ssd_D, ssd_norm, w_proj_s5, w_proj_ssd, b_gate, w_out, ffn2_norm, ffn2_w_gate, ffn2_w_up, ffn2_w_down, final_norm, loss_target, m_ffn1_norm, m_ffn1_w_gate, m_ffn1_w_up, m_ffn1_w_down, m_mix_norm, m_w_in, m_conv_w, m_conv_b, m_s5_A_re, m_s5_A_im, m_s5_log_dt, m_s5_B_re, m_s5_B_im, m_s5_C_re, m_s5_C_im, m_s5_D, m_s5_w_glu, m_s5_b_glu, m_ssd_A_log, m_ssd_dt_bias, m_ssd_D, m_ssd_norm, m_w_proj_s5, m_w_proj_ssd, m_b_gate, m_w_out, m_ffn2_norm, m_ffn2_w_gate, m_ffn2_w_up, m_ffn2_w_down, m_final_norm, v_ffn1_norm, v_ffn1_w_gate, v_ffn1_w_up, v_ffn1_w_down, v_mix_norm, v_w_in, v_conv_w, v_conv_b, v_s5_A_re, v_s5_A_im, v_s5_log_dt, v_s5_B_re, v_s5_B_im, v_s5_C_re, v_s5_C_im, v_s5_D, v_s5_w_glu, v_s5_b_glu, v_ssd_A_log, v_ssd_dt_bias, v_ssd_D, v_ssd_norm, v_w_proj_s5, v_w_proj_ssd, v_b_gate, v_w_out, v_ffn2_norm, v_ffn2_w_gate, v_ffn2_w_up, v_ffn2_w_down, v_final_norm):
    a = dict(locals())
    assert list(a) == ARGS
    x, target = a['x'][0], a['loss_target'][0]
    k_me = 2 * lax.axis_index("x") + lax.axis_index("y")

    src = {n: a[n][0].T.astype(BF16) for n in COL_SHARDED}
    src.update({n: a[n][0].astype(BF16) for n in ROW_SHARDED})
    src['conv_w'] = a['conv_w'][0]
    def flat(f):
        return f.reshape(-1, f.shape[-1])

    full = _gather_shards([src[n] for n in GATHER_FIRST])
    w = {n: flat(f) for n, f in zip(GATHER_FIRST, full)}
    token = full[0][0, :8, :128].astype(F32)
    for names, name, cid in ((GATHER_SECOND[:2], "gather_mixer_in", 1), (GATHER_SECOND[2:], "gather_mixer", 12)):
        got, token = _gather_shards_async([src[n] for n in names], token, name, cid)
        w.update({n: flat(f) for n, f in zip(names, got)})
    w['conv_w'] = w['conv_w'].reshape(4, CONV_K, -1).transpose(1, 0, 2).reshape(CONV_K, CONV_DIM)
    reducer = _Reducer(token)

    def gather_last(after):
        got, reducer.token = _gather_shards_async(
            [src[n] for n in GATHER_THIRD], reducer.token + after[:8, :128], "gather_ffn2", 2)
        return {n: flat(f) for n, f in zip(GATHER_THIRD, got)}

    p = {n: a[n] for n in SMALL if n != 'conv_w'}
    p['final_norm'] = a['final_norm'][None]
    ids = {"ffn2": (3, 4, 5), "mixer": (6, 7, 8), "ffn1": (9, 10, 11)}
    loss, grad_x, gp, conv_w_grad = _local_step(
        x, target, p, w, gather_last, lambda tag, grads, after: reducer.begin(tag, ids[tag], grads, after))
    reducer.finish(grad_x)
    pairs = [reducer.pairs[n] for n in MATRICES]
    gp['conv_w'] = conv_w_grad[None]
    small_shapes = [(1, 1)] + [gp[n].shape if n != 'final_norm' else (1, D_MODEL) for n in SMALL]
    red = _unpack(_all_reduce_small(_pack([loss] + [gp[n] for n in SMALL])), small_shapes)
    loss_all = red[0].reshape(())
    gsmall = dict(zip(SMALL, red[1:]))
    gsmall['conv_w'] = lax.dynamic_slice_in_dim(gsmall['conv_w'], k_me * 1024, 1024, axis=2)
    gsmall = {n: g.reshape(a[n].shape) for n, g in gsmall.items()}

    grads, delta, new_m, new_v = {}, {}, {}, {}
    for n, (mine, theirs) in zip(MATRICES, pairs):
        turn = (lambda t: jnp.swapaxes(t, 1, 2)) if n in COL_SHARDED else (lambda t: t)
        out = _adamw(turn(a[n]), turn(a['m_' + n]), turn(a['v_' + n]), [mine, theirs], name=f"adamw_{n}", emit_g=True)
        grads[n], delta[n], new_m[n], new_v[n] = [turn(t) for t in out]
    sw, sm, sv, sg = [_pack([t[n] for n in SMALL])[None] for t in (a, {n: a['m_' + n] for n in SMALL},
                                                                   {n: a['v_' + n] for n in SMALL}, gsmall)]
    d, m2, v2 = _adamw(sw, sm, sv, [sg[0]], name="adamw_small", emit_g=False)
    shapes = [a[n].shape for n in SMALL]
    for n, dd, mm, vv in zip(SMALL, _unpack(d[0], shapes), _unpack(m2[0], shapes), _unpack(v2[0], shapes)):
        grads[n], delta[n], new_m[n], new_v[n] = gsmall[n], dd, mm, vv
    return (loss_all, grad_x[None], *[grads[n] for n in WEIGHTS], *[delta[n] for n in WEIGHTS],
            *[new_m[n] for n in WEIGHTS], *[new_v[n] for n in WEIGHTS])
```

```python
import itertools
import math

import jax
import jax.numpy as jnp
from jax import lax
from jax.experimental import pallas as pl
from jax.experimental.pallas import tpu as pltpu
from jax.experimental.pallas import tpu_sc as plsc

F32 = jnp.float32
BF16 = jnp.bfloat16
MXU = BF16
HI = lax.Precision.HIGHEST

D_MODEL = 1024
D_FF = 2816
EPS = 1e-6
S5_GROUPS, S5_GROUP, S5_STATE = 64, 16, 64
D_INNER = 2048
SSD_HEADDIM, SSD_HEADS, SSD_GROUPS, SSD_HPG, SSD_STATE = 64, 32, 8, 4, 128
CONV_K, CONV_DIM = 4, 4096
IN_SPLITS = (1024, 2048, 4096, 32, 2048)
IN_OFFS = (0, 1024, 3072, 7168, 7200, 9248)
SSD_Q = 256
SSD_STEP_GROUPS = 2
ADAM_LR, ADAM_B1, ADAM_B2, ADAM_EPS, ADAM_WD, ADAM_STEP = 0.001, 0.9, 0.999, 1e-08, 0.01, 10

VMEM_LIMIT = 56 * 1024 * 1024
MATMUL_VMEM = 32 * 1024 * 1024
MESH = pl.DeviceIdType.MESH

NN = ((1,), (0,))
NT = ((1,), (1,))
TN = ((0,), (0,))


def _dot(a, b, dims, precision=None):
    if precision is None:
        a, b = a.astype(MXU), b.astype(MXU)
    return lax.dot_general(a, b, (dims, ((), ())), precision=precision, preferred_element_type=F32)


def _tile(n, pref):
    if n <= pref:
        return n
    best = None
    for t in range(128, pref + 1, 128):
        if n % t == 0:
            best = t
    assert best is not None, (n, pref)
    return best


def _params(sem):
    return pltpu.CompilerParams(dimension_semantics=sem, vmem_limit_bytes=VMEM_LIMIT)


def _matmul(a, b, mode, *, name, out=((F32),), epilogue=None, extras=(), after=(), tm=512, tn=1536):
    if mode == "nn":
        (M, K), (_, N) = a.shape, b.shape
    elif mode == "nt":
        (M, K), (N, _) = a.shape, b.shape
    else:
        (K, M), (_, N) = a.shape, b.shape
    tm, tn = _tile(M, tm), _tile(N, tn)

    def vmem(tm_, tn_):
        per = K * tm_ * a.dtype.itemsize + K * tn_ * b.dtype.itemsize
        per += sum(tm_ * tn_ * jnp.dtype(dt).itemsize for dt in out)
        per += sum((1 if e.shape[0] == 1 else tm_) * tn_ * e.dtype.itemsize for e in extras)
        return 2 * per

    while vmem(tm, tn) > MATMUL_VMEM and (tn > 128 or tm > 128):
        if tn >= tm and tn > 128:
            tn = _tile(N, tn - 128)
        else:
            tm = _tile(M, tm - 128)
    bytes_a, bytes_b = a.size * a.dtype.itemsize, b.size * b.dtype.itemsize
    rows_outer = bytes_a + (M // tm) * bytes_b <= (N // tn) * bytes_a + bytes_b
    grid = (M // tm, N // tn) if rows_outer else (N // tn, M // tm)

    def at(f):
        return (lambda i, j: f(i, j)) if rows_outer else (lambda j, i: f(i, j))

    a_spec = pl.BlockSpec((K, tm), at(lambda i, j: (0, i))) if mode == "tn" else pl.BlockSpec((tm, K), at(lambda i, j: (i, 0)))
    b_spec = pl.BlockSpec((tn, K), at(lambda i, j: (j, 0))) if mode == "nt" else pl.BlockSpec((K, tn), at(lambda i, j: (0, j)))
    dims = {"nn": NN, "nt": NT, "tn": TN}[mode]
    e_specs = []
    for e in extras:
        if e.shape[0] == 1:
            e_specs.append(pl.BlockSpec((1, tn), at(lambda i, j: (0, j))))
        else:
            assert e.shape == (M, N), (e.shape, M, N)
            e_specs.append(pl.BlockSpec((tm, tn), at(lambda i, j: (i, j))))
    n_e, n_o, n_a = len(extras), len(out), len(after)

    def body(a_ref, b_ref, *refs):
        acc = _dot(a_ref[...], b_ref[...], dims)
        res = (acc,) if epilogue is None else epilogue(acc, *[r[...] for r in refs[:n_e]])
        for r, v in zip(refs[n_e + n_a:], res):
            r[...] = v.astype(r.dtype)

    res = pl.pallas_call(
        body, name=name, grid=grid,
        in_specs=[a_spec, b_spec] + e_specs + [ANY] * n_a,
        out_specs=[pl.BlockSpec((tm, tn), at(lambda i, j: (i, j))) for _ in range(n_o)],
        out_shape=[jax.ShapeDtypeStruct((M, N), dt) for dt in out],
        compiler_params=_params(("parallel", "parallel")),
    )(a, b, *extras, *after)
    return res[0] if n_o == 1 else res


def _rowwise(fn, rows, fulls, outs, reds=(), *, name, after=(), tm=256):
    T = rows[0].shape[0]
    tm = min(tm, T)
    n_r, n_f, n_o, n_d, n_a = len(rows), len(fulls), len(outs), len(reds), len(after)

    def body(*refs):
        ins = [r[...] for r in refs[:n_r + n_f]]
        o_refs = refs[n_r + n_f + n_a:n_r + n_f + n_a + n_o]
        d_refs = refs[n_r + n_f + n_a + n_o:]
        res = fn(*ins)
        for r, v in zip(o_refs, res[:n_o]):
            r[...] = v.astype(r.dtype)
        if n_d:
            @pl.when(pl.program_id(0) == 0)
            def _():
                for r in d_refs:
                    r[...] = jnp.zeros_like(r)
            for r, v in zip(d_refs, res[n_o:]):
                r[...] += v.astype(r.dtype)

    res = pl.pallas_call(
        body, name=name, grid=(T // tm,),
        in_specs=[pl.BlockSpec((tm, r.shape[1]), lambda i: (i, 0)) for r in rows]
        + [pl.BlockSpec(f.shape, lambda i, nd=f.ndim: (0,) * nd) for f in fulls] + [ANY] * n_a,
        out_specs=[pl.BlockSpec((tm, c), lambda i: (i, 0)) for c, _ in outs]
        + [pl.BlockSpec(s, lambda i, nd=len(s): (0,) * nd) for s, _ in reds],
        out_shape=[jax.ShapeDtypeStruct((T, c), dt) for c, dt in outs]
        + [jax.ShapeDtypeStruct(s, dt) for s, dt in reds],
        compiler_params=_params(("arbitrary",)),
    )(*rows, *fulls, *after)
    return res


def _rms(x, g):
    return x * lax.rsqrt(jnp.mean(x * x, axis=-1, keepdims=True) + EPS) * g


def _colsum(v):
    return jnp.sum(v, axis=0, keepdims=True)


def _softplus(x):
    return jnp.maximum(x, 0.0) + jnp.log1p(jnp.exp(-jnp.abs(x)))


def _ffn_fwd(x, norm, wgT, wuT, wd, tag):
    D = x.shape[1]
    (hn,) = _rowwise(lambda xt, g: (_rms(xt, g),), [x], [norm], [(D, BF16)], name=f"{tag}_norm")
    a = _matmul(hn, wgT, "nt", name=f"{tag}_gate", out=(BF16,))
    b, hid = _matmul(hn, wuT, "nt", name=f"{tag}_up", out=(BF16, BF16),
                     epilogue=lambda acc, at: (acc, jax.nn.silu(at.astype(F32)) * acc), extras=[a])
    y = _matmul(hid, wd, "nn", name=f"{tag}_down", epilogue=lambda acc, xt: (xt + 0.5 * acc,), extras=[x])
    return y, (hn, a, b, hid)


def _ffn_bwd(dy, dy16, x, norm, wgT, wuT, wd, res, tag, after, on_grads):
    hn, a, b, hid = res
    D, F = x.shape[1], a.shape[1]

    def act_bwd(acc, at, bt):
        _, vjp = jax.vjp(lambda p, q: jax.nn.silu(p) * q, at.astype(F32), bt.astype(F32))
        return vjp(0.5 * acc)

    da, db = _matmul(dy16, wd, "nt", name=f"{tag}_d_hid", out=(BF16, BF16), epilogue=act_bwd, extras=[a, b])
    d_wd = _matmul(hid, dy16, "tn", name=f"{tag}_d_wd", epilogue=lambda acc: (0.5 * acc,))
    d_wgT = _matmul(da, hn, "tn", name=f"{tag}_d_wg", after=after)
    d_wuT = _matmul(db, hn, "tn", name=f"{tag}_d_wu")
    started = on_grads(d_wgT, d_wuT, d_wd, dy)
    dhn = _matmul(da, wgT, "nn", name=f"{tag}_d_hn1")
    dhn = _matmul(db, wuT, "nn", name=f"{tag}_d_hn2", epilogue=lambda acc, e: (acc + e,), extras=[dhn])

    def norm_bwd(xt, dh, dyt, g):
        _, vjp = jax.vjp(_rms, xt, g)
        dx, dg = vjp(dh)
        return dyt + dx, dyt + dx, dg

    dx, dx16, d_norm = _rowwise(norm_bwd, [x, dhn, dy], [norm], [(D, F32), (D, BF16)], [((1, D), F32)],
                                name=f"{tag}_d_norm", after=started)
    return dx, dx16, d_norm


S5_TILES = 8
S5_HALF = 256


def _s5_derive(A_re, A_im, log_dt, B_re, B_im, C_re, C_im):
    G, N, M = S5_GROUPS, S5_STATE, S5_GROUP
    dt = jnp.exp(log_dt)[:, None]
    mag = jnp.exp(A_re * dt)
    ar, ai = mag * jnp.cos(A_im * dt), mag * jnp.sin(A_im * dt)
    den = A_re * A_re + A_im * A_im
    cr = ((ar - 1.0) * A_re + ai * A_im) / den
    ci = (ai * A_re - (ar - 1.0) * A_im) / den
    bbr = cr[..., None] * B_re - ci[..., None] * B_im
    bbi = cr[..., None] * B_im + ci[..., None] * B_re
    eye = jnp.eye(8, dtype=F32)

    def tile_in(bb):
        t = bb.reshape(S5_TILES, 8, N, M).transpose(0, 1, 3, 2)
        return jnp.einsum("jamn,ab->jambn", t, eye).reshape(S5_TILES, 8 * M, 8 * N)

    def tile_out(c):
        t = c.reshape(S5_TILES, 8, M, N).transpose(0, 1, 3, 2)
        return jnp.einsum("janm,ab->janbm", t, eye).reshape(S5_TILES, 8 * N, 8 * M)

    return (tile_in(bbr), tile_in(bbi), tile_out(C_re), tile_out(C_im),
            ar.reshape(S5_TILES, 1, 8 * N), ai.reshape(S5_TILES, 1, 8 * N))


S5_NB = S5_HALF // 128
S5_SEG = 32


def _cmul(ar, ai, br, bi):
    return ar * br - ai * bi, ar * bi + ai * br


def _s5_scan(sr_ref, si_ref, ar, ai, T, reverse):
    L, V = T // S5_SEG, S5_SEG // 8
    assert L * S5_SEG == T and L & (L - 1) == 0, T
    sg = -1.0 if reverse else 1.0
    a_r = [jnp.broadcast_to(ar[:, 128 * b:128 * (b + 1)], (8, 128)) for b in range(S5_NB)]
    a_i = [jnp.broadcast_to(sg * ai[:, 128 * b:128 * (b + 1)], (8, 128)) for b in range(S5_NB)]

    def rows(k, v):
        return pl.ds(pl.multiple_of(((L - 1 - k) if reverse else k) * S5_SEG + 8 * v, 8), 8)

    def local(k, carry):
        out = []
        for b in range(S5_NB):
            for v in range(V):
                idx = rows(k, v)
                mr, mi = _cmul(a_r[b], a_i[b], *carry[b * V + v])
                nr, ni = mr + sr_ref[b, idx, :], mi + si_ref[b, idx, :]
                sr_ref[b, idx, :] = nr
                si_ref[b, idx, :] = ni
                out.append((nr, ni))
        return tuple(out)

    z = jnp.zeros((8, 128), F32)
    ends = lax.fori_loop(0, L, local, tuple((z, z) for _ in range(S5_NB * V)))

    carries = []
    for b in range(S5_NB):
        pr, pi = a_r[b][0:1], a_i[b][0:1]
        n = L
        while n > 1:
            pr, pi = _cmul(pr, pi, pr, pi)
            n //= 2
        cr, ci = jnp.zeros((1, 128), F32), jnp.zeros((1, 128), F32)
        into = [None] * S5_SEG
        for j in (reversed(range(S5_SEG)) if reverse else range(S5_SEG)):
            into[j] = (cr, ci)
            er, ei = ends[b * V + j // 8]
            mr, mi = _cmul(pr, pi, cr, ci)
            cr, ci = mr + er[j % 8:j % 8 + 1], mi + ei[j % 8:j % 8 + 1]
        carries.append([(jnp.concatenate([into[8 * v + s][0] for s in range(8)], axis=0),
                         jnp.concatenate([into[8 * v + s][1] for s in range(8)], axis=0)) for v in range(V)])

    def fix(k, powers):
        out = []
        for b in range(S5_NB):
            pr, pi = powers[b]
            for v in range(V):
                idx = rows(k, v)
                dr, di = _cmul(pr, pi, *carries[b][v])
                sr_ref[b, idx, :] += dr
                si_ref[b, idx, :] += di
            out.append(_cmul(pr, pi, a_r[b], a_i[b]))
        return tuple(out)

    lax.fori_loop(0, L, fix, tuple((a_r[b], a_i[b]) for b in range(S5_NB)))


def _s5_in(ut, b_ref, s_ref):
    for b in range(S5_NB):
        s_ref[b] = _dot(ut, b_ref[:, 128 * b:128 * (b + 1)], NN)


def _to_seg(v):
    T, C = v.shape
    return v.reshape(S5_SEG, T // S5_SEG, C).transpose(1, 0, 2).reshape(T, C)


def _from_seg(v):
    T, C = v.shape
    return v.reshape(T // S5_SEG, S5_SEG, C).transpose(1, 0, 2).reshape(T, C)


def _s5_specs(T):
    u_spec = pl.BlockSpec((T, 128), lambda j, h: (0, j))
    b_spec = pl.BlockSpec((None, 128, S5_HALF), lambda j, h: (j, 0, h))
    c_spec = pl.BlockSpec((None, S5_HALF, 128), lambda j, h: (j, h, 0))
    a_spec = pl.BlockSpec((None, 1, S5_HALF), lambda j, h: (j, 0, h))
    return u_spec, b_spec, c_spec, a_spec


def _s5_fwd(u, tiles):
    T = u.shape[0]
    u_spec, b_spec, c_spec, a_spec = _s5_specs(T)

    def body(u_ref, br_ref, bi_ref, cr_ref, ci_ref, ar_ref, ai_ref, y_ref, sr_ref, si_ref):
        ut = u_ref[...].astype(MXU)
        _s5_in(ut, br_ref, sr_ref)
        _s5_in(ut, bi_ref, si_ref)
        _s5_scan(sr_ref, si_ref, ar_ref[...], ai_ref[...], T, False)
        y = None
        for b in range(S5_NB):
            blk = slice(128 * b, 128 * (b + 1))
            yb = _dot(sr_ref[b], cr_ref[blk, :], NN) - _dot(si_ref[b], ci_ref[blk, :], NN)
            y = yb if y is None else y + yb

        @pl.when(pl.program_id(1) == 0)
        def _():
            y_ref[...] = y

        @pl.when(pl.program_id(1) == 1)
        def _():
            y_ref[...] += y

    scr = pltpu.VMEM((S5_NB, T, 128), F32)
    return pl.pallas_call(
        body, name="s5_fwd", grid=(S5_TILES, 2),
        in_specs=[u_spec, b_spec, b_spec, c_spec, c_spec, a_spec, a_spec],
        out_specs=u_spec, out_shape=jax.ShapeDtypeStruct(u.shape, F32),
        scratch_shapes=[scr, scr],
        compiler_params=_params(("parallel", "arbitrary")),
    )(u, *tiles)


def _s5_bwd(u, dys, du_skip, tiles):
    T = u.shape[0]
    u_spec, b_spec, c_spec, a_spec = _s5_specs(T)

    def body(u_ref, dy_ref, sk_ref, br_ref, bi_ref, cr_ref, ci_ref, ar_ref, ai_ref,
             du_ref, dbr_ref, dbi_ref, dcr_ref, dci_ref, dar_ref, dai_ref, sr_ref, si_ref, lr_ref, li_ref):
        ut, dy = u_ref[...].astype(MXU), dy_ref[...].astype(MXU)
        ar, ai = ar_ref[...], ai_ref[...]
        _s5_in(ut, br_ref, sr_ref)
        _s5_in(ut, bi_ref, si_ref)
        _s5_scan(sr_ref, si_ref, ar, ai, T, False)
        for b in range(S5_NB):
            blk = slice(128 * b, 128 * (b + 1))
            dcr_ref[blk, :] = _dot(sr_ref[b], dy, TN)
            dci_ref[blk, :] = -_dot(si_ref[b], dy, TN)
            lr_ref[b] = _dot(dy, cr_ref[blk, :], NT)
            li_ref[b] = -_dot(dy, ci_ref[blk, :], NT)
        _s5_scan(lr_ref, li_ref, ar, ai, T, True)
        du = None
        head = lax.broadcasted_iota(jnp.int32, (T, 128), 0) < S5_SEG
        seg0 = lax.broadcasted_iota(jnp.int32, (S5_SEG, 128), 0) == 0
        for b in range(S5_NB):
            blk = slice(128 * b, 128 * (b + 1))
            lr, li = lr_ref[b], li_ref[b]
            pr = jnp.where(head, 0.0, pltpu.roll(sr_ref[b], S5_SEG, 0))
            pi = jnp.where(head, 0.0, pltpu.roll(si_ref[b], S5_SEG, 0))
            er = jnp.where(seg0, 0.0, pltpu.roll(sr_ref[b, T - S5_SEG:T, :], 1, 0))
            ei = jnp.where(seg0, 0.0, pltpu.roll(si_ref[b, T - S5_SEG:T, :], 1, 0))
            hr, hi = lr[0:S5_SEG], li[0:S5_SEG]
            dar_ref[:, blk] = _colsum(lr * pr + li * pi) + _colsum(hr * er + hi * ei)
            dai_ref[:, blk] = _colsum(li * pr - lr * pi) + _colsum(hi * er - hr * ei)
            dbr_ref[:, blk] = _dot(ut, lr, TN)
            dbi_ref[:, blk] = _dot(ut, li, TN)
            dub = _dot(lr, br_ref[:, blk], NT) + _dot(li, bi_ref[:, blk], NT)
            du = dub if du is None else du + dub

        @pl.when(pl.program_id(1) == 0)
        def _():
            du_ref[...] = du + sk_ref[...]

        @pl.when(pl.program_id(1) == 1)
        def _():
            du_ref[...] += du

    scr = pltpu.VMEM((S5_NB, T, 128), F32)
    return pl.pallas_call(
        body, name="s5_bwd", grid=(S5_TILES, 2),
        in_specs=[u_spec, u_spec, u_spec, b_spec, b_spec, c_spec, c_spec, a_spec, a_spec],
        out_specs=[u_spec, b_spec, b_spec, c_spec, c_spec, a_spec, a_spec],
        out_shape=[jax.ShapeDtypeStruct(u.shape, F32)] + [jax.ShapeDtypeStruct(t.shape, F32) for t in tiles],
        scratch_shapes=[scr, scr, scr, scr],
        compiler_params=_params(("parallel", "arbitrary")),
    )(u, dys, du_skip, *tiles)


CONV_TILE = 256


def _conv_pre(x, w, b):
    T = x.shape[0]
    row = lax.broadcasted_iota(jnp.int32, x.shape, 0)
    acc = x * w[CONV_K - 1:CONV_K, :] + b
    for lag in range(1, CONV_K):
        acc = acc + jnp.where(row >= lag, pltpu.roll(x, lag, 0), 0.0) * w[CONV_K - 1 - lag:CONV_K - lag, :]
    return acc


def _conv_fwd(x, w, b):
    T, C = x.shape
    col = pl.BlockSpec((T, CONV_TILE), lambda j: (0, j))

    def body(x_ref, w_ref, b_ref, o_ref):
        o_ref[...] = jax.nn.silu(_conv_pre(x_ref[...], w_ref[...], b_ref[...]))

    return pl.pallas_call(
        body, name="conv_fwd", grid=(C // CONV_TILE,),
        in_specs=[col, pl.BlockSpec((CONV_K, CONV_TILE), lambda j: (0, j)), pl.BlockSpec((1, CONV_TILE), lambda j: (0, j))],
        out_specs=col, out_shape=jax.ShapeDtypeStruct((T, C), F32),
        compiler_params=_params(("parallel",)),
    )(x, w, b)


def _conv_bwd(x, w, b, dout, col0, name):
    T, C = dout.shape
    off = col0 // CONV_TILE
    xcol = pl.BlockSpec((T, CONV_TILE), lambda j: (0, j + off))
    dcol = pl.BlockSpec((T, CONV_TILE), lambda j: (0, j))

    def body(x_ref, w_ref, b_ref, d_ref, dx_ref, dw_ref, db_ref):
        x, w = x_ref[...], w_ref[...]
        _, vjp = jax.vjp(jax.nn.silu, _conv_pre(x, w, b_ref[...]))
        (dy,) = vjp(d_ref[...])
        row = lax.broadcasted_iota(jnp.int32, x.shape, 0)
        dx = dy * w[CONV_K - 1:CONV_K, :]
        dw_ref[CONV_K - 1:CONV_K, :] = _colsum(dy * x)
        for lag in range(1, CONV_K):
            ahead = jnp.where(row < T - lag, pltpu.roll(dy, T - lag, 0), 0.0)
            dx = dx + ahead * w[CONV_K - 1 - lag:CONV_K - lag, :]
            dw_ref[CONV_K - 1 - lag:CONV_K - lag, :] = _colsum(ahead * x)
        dx_ref[...] = dx.astype(dx_ref.dtype)
        db_ref[...] = _colsum(dy)

    return pl.pallas_call(
        body, name=name, grid=(C // CONV_TILE,),
        in_specs=[xcol, pl.BlockSpec((CONV_K, CONV_TILE), lambda j: (0, j + off)),
                  pl.BlockSpec((1, CONV_TILE), lambda j: (0, j + off)), dcol],
        out_specs=[dcol, pl.BlockSpec((CONV_K, CONV_TILE), lambda j: (0, j)), pl.BlockSpec((1, CONV_TILE), lambda j: (0, j))],
        out_shape=[jax.ShapeDtypeStruct((T, C), BF16), jax.ShapeDtypeStruct((CONV_K, C), F32), jax.ShapeDtypeStruct((1, C), F32)],
        compiler_params=_params(("parallel",)),
    )(x, w, b, dout)


def _ssd_common(x_ref, b_ref, c_ref, dtr_ref, bias_ref, alog_ref, tri_ref, triu_ref):
    Q = x_ref.shape[0]
    x, Bm, Cm = x_ref[...], b_ref[...], c_ref[...]
    pre = dtr_ref[...] + bias_ref[...]
    dt = _softplus(pre)
    A = -jnp.exp(alog_ref[...])
    adt = dt * A
    ac4 = _dot(tri_ref[...], adt, NN, HI)
    ar4 = _dot(adt, triu_ref[...], TN, HI)
    cb = _dot(Cm, Bm, NT)
    ii = lax.broadcasted_iota(jnp.int32, (Q, Q), 0)
    jj = lax.broadcasted_iota(jnp.int32, (Q, Q), 1)
    atot4 = ac4[Q - 1:Q, :]
    return x, Bm, Cm, pre, dt, A, ac4, ar4, cb, ii >= jj, atot4


def _ssd_decay(k, ac4, ar4, causal):
    seg = ac4[:, k:k + 1] - ar4[k:k + 1, :]
    return jnp.where(causal, jnp.exp(jnp.where(causal, seg, 0.0)), 0.0)


def _per_head(c4, n):
    lane = lax.broadcasted_iota(jnp.int32, (c4.shape[0], n), 1)
    out = jnp.broadcast_to(c4[:, 0:1], (c4.shape[0], n))
    for k in range(1, SSD_HPG):
        out = jnp.where(lane >= SSD_HEADDIM * k, c4[:, k:k + 1], out)
    return out


def _head_sums(v):
    n = SSD_HPG * SSD_HEADDIM
    row = lax.broadcasted_iota(jnp.int32, (n, SSD_HPG), 0)
    col = lax.broadcasted_iota(jnp.int32, (n, SSD_HPG), 1)
    member = jnp.where((row >= SSD_HEADDIM * col) & (row < SSD_HEADDIM * (col + 1)), 1.0, 0.0)
    return _dot(v, member, NN, HI)


def _head_rows(c4):
    n = SSD_HPG * SSD_HEADDIM
    row = lax.broadcasted_iota(jnp.int32, (n, 1), 0)
    out = jnp.broadcast_to(c4[:, 0:1], (n, 1))
    for k in range(1, SSD_HPG):
        out = jnp.where(row >= SSD_HEADDIM * k, c4[:, k:k + 1], out)
    return out


def _ssd_specs(T, Q, rev):
    NC = T // Q
    cc = (lambda c: NC - 1 - c) if rev else (lambda c: c)
    W, N, S = SSD_HPG * SSD_HEADDIM, SSD_STATE, SSD_STEP_GROUPS
    x_spec = pl.BlockSpec((Q, S * W), lambda g, c: (cc(c), g))
    b_spec = pl.BlockSpec((Q, S * N), lambda g, c: (cc(c), D_INNER // (S * N) + g))
    c_spec = pl.BlockSpec((Q, S * N), lambda g, c: (cc(c), (D_INNER + SSD_GROUPS * N) // (S * N) + g))
    dt_spec = pl.BlockSpec((S, Q, SSD_HPG), lambda g, c: (g, cc(c), 0))
    p_spec = pl.BlockSpec((S, 1, SSD_HPG), lambda g, c: (g, 0, 0))
    tri_spec = pl.BlockSpec((Q, Q), lambda g, c: (0, 0))
    h_spec = pl.BlockSpec((S, None, W, N), lambda g, c: (g, cc(c), 0, 0))
    return x_spec, b_spec, c_spec, dt_spec, p_spec, tri_spec, h_spec


def _ssd_views(gi, wide, narrow, lead):
    W, N = SSD_HPG * SSD_HEADDIM, SSD_STATE
    return ([r.at[:, pl.ds(W * gi, W)] for r in wide] + [r.at[:, pl.ds(N * gi, N)] for r in narrow]
            + [r.at[gi] for r in lead])


def _tri(Q):
    tri = jnp.tril(jnp.ones((Q, Q), F32))
    return tri, tri.T


def _ssd_fwd(xc, dtr8, bias8, alog8, dsk8):
    T, Q = xc.shape[0], SSD_Q
    NC = T // Q
    P, W = SSD_HEADDIM, SSD_HPG * SSD_HEADDIM
    x_spec, b_spec, c_spec, dt_spec, p_spec, tri_spec, h_spec = _ssd_specs(T, Q, False)

    def body(x_ref, b_ref, c_ref, dtr_ref, bias_ref, alog_ref, dsk_ref, tri_ref, triu_ref, y_ref, hs_ref, h_scr):
        for gi in range(SSD_STEP_GROUPS):
            xv, yv, bv, cv, dtv, biasv, alogv, dskv, hsv, hv = _ssd_views(
                gi, (x_ref, y_ref), (b_ref, c_ref), (dtr_ref, bias_ref, alog_ref, dsk_ref, hs_ref, h_scr))
            one(xv, bv, cv, dtv, biasv, alogv, dskv, tri_ref, triu_ref, yv, hsv, hv)

    def one(x_ref, b_ref, c_ref, dtr_ref, bias_ref, alog_ref, dsk_ref, tri_ref, triu_ref, y_ref, hs_ref, h_scr):
        @pl.when(pl.program_id(1) == 0)
        def _():
            h_scr[...] = jnp.zeros_like(h_scr)

        hin = h_scr[...]
        hs_ref[...] = hin
        x, Bm, Cm, pre, dt, A, ac4, ar4, cb, causal, atot4 = _ssd_common(
            x_ref, b_ref, c_ref, dtr_ref, bias_ref, alog_ref, tri_ref, triu_ref)
        xdt = x * _per_head(dt, W)
        rest = _dot(Cm, hin, NT) * _per_head(jnp.exp(ac4), W) + _per_head(dsk_ref[...], W) * x
        for k in range(SSD_HPG):
            blk = slice(P * k, P * (k + 1))
            G = cb * _ssd_decay(k, ac4, ar4, causal)
            y_ref[:, blk] = _dot(G, xdt[:, blk], NN) + rest[:, blk]
        xw = xdt * _per_head(jnp.exp(atot4 - ac4), W)
        h_scr[...] = _head_rows(jnp.exp(atot4)) * hin + _dot(xw, Bm, TN)

    tri, triu = _tri(Q)
    return pl.pallas_call(
        body, name="ssd_fwd", grid=(SSD_GROUPS // SSD_STEP_GROUPS, NC),
        in_specs=[x_spec, b_spec, c_spec, dt_spec, p_spec, p_spec, p_spec, tri_spec, tri_spec],
        out_specs=[pl.BlockSpec((Q, SSD_STEP_GROUPS * W), lambda g, c: (c, g)), h_spec],
        out_shape=[jax.ShapeDtypeStruct((T, D_INNER), F32),
                   jax.ShapeDtypeStruct((SSD_GROUPS, NC, W, SSD_STATE), F32)],
        scratch_shapes=[pltpu.VMEM((SSD_STEP_GROUPS, W, SSD_STATE), F32)],
        compiler_params=_params(("parallel", "arbitrary")),
    )(xc, xc, xc, dtr8, bias8, alog8, dsk8, tri, triu)


def _ssd_bwd(xc, dtr8, bias8, alog8, dsk8, hs, dy):
    T, Q = xc.shape[0], SSD_Q
    NC = T // Q
    P, W = SSD_HEADDIM, SSD_HPG * SSD_HEADDIM
    x_spec, b_spec, c_spec, dt_spec, p_spec, tri_spec, h_spec = _ssd_specs(T, Q, True)
    dy_spec = pl.BlockSpec((Q, SSD_STEP_GROUPS * W), lambda g, c: (NC - 1 - c, g))
    dbc_spec = pl.BlockSpec((Q, SSD_STEP_GROUPS * SSD_STATE), lambda g, c: (NC - 1 - c, g))

    def body(x_ref, b_ref, c_ref, dtr_ref, bias_ref, alog_ref, dsk_ref, tri_ref, triu_ref, hs_ref, dy_ref,
             dx_ref, db_ref, dc_ref, ddtr_ref, dbias_ref, dalog_ref, ddsk_ref, dh_scr):
        for gi in range(SSD_STEP_GROUPS):
            (xv, dyv, dxv, bv, cv, dbv, dcv, dtv, biasv, alogv, dskv, hsv, ddtv, dbiasv, dalogv, ddskv, dhv) = _ssd_views(
                gi, (x_ref, dy_ref, dx_ref), (b_ref, c_ref, db_ref, dc_ref),
                (dtr_ref, bias_ref, alog_ref, dsk_ref, hs_ref, ddtr_ref, dbias_ref, dalog_ref, ddsk_ref, dh_scr))
            one(xv, bv, cv, dtv, biasv, alogv, dskv, tri_ref, triu_ref, hsv, dyv,
                dxv, dbv, dcv, ddtv, dbiasv, dalogv, ddskv, dhv)

    def one(x_ref, b_ref, c_ref, dtr_ref, bias_ref, alog_ref, dsk_ref, tri_ref, triu_ref, hs_ref, dy_ref,
            dx_ref, db_ref, dc_ref, ddtr_ref, dbias_ref, dalog_ref, ddsk_ref, dh_scr):
        @pl.when(pl.program_id(1) == 0)
        def _():
            dh_scr[...] = jnp.zeros_like(dh_scr)
            dbias_ref[...] = jnp.zeros_like(dbias_ref)
            dalog_ref[...] = jnp.zeros_like(dalog_ref)
            ddsk_ref[...] = jnp.zeros_like(ddsk_ref)

        x, Bm, Cm, pre, dt, A, ac4, ar4, cb, causal, atot4 = _ssd_common(
            x_ref, b_ref, c_ref, dtr_ref, bias_ref, alog_ref, tri_ref, triu_ref)
        dyv, hin, dho = dy_ref[...], hs_ref[...], dh_scr[...]
        dt_w = _per_head(dt, W)
        xdt = x * dt_w
        E4, F4, etot4 = jnp.exp(ac4), jnp.exp(atot4 - ac4), jnp.exp(atot4)
        F_w = _per_head(F4, W)
        ddsk_ref[...] += _head_sums(_colsum(dyv * x))
        Z = _dot(Cm, hin, NT)
        dZ = dyv * _per_head(E4, W)
        dac4 = _head_sums(dyv * Z) * E4
        dC = _dot(dZ, hin, NN)
        dh_scr[...] = _dot(dZ, Cm, TN) + _head_rows(etot4) * dho
        per_row = jnp.sum(dho * hin, axis=1, keepdims=True)
        lane4 = lax.broadcasted_iota(jnp.int32, (1, SSD_HPG), 1)
        datot4 = jnp.zeros((1, SSD_HPG), F32)
        for k in range(SSD_HPG):
            datot4 = jnp.where(lane4 == k, jnp.sum(per_row[P * k:P * (k + 1)], keepdims=True), datot4)
        datot4 = datot4 * etot4
        dxw = _dot(Bm, dho, NT)
        dB = _dot(xdt * F_w, dho, NN)
        dFa4 = _head_sums(dxw * xdt) * F4
        datot4 = datot4 + _colsum(dFa4)
        dac4 = dac4 - dFa4
        dcb = jnp.zeros((Q, Q), F32)
        lane_q4 = lax.broadcasted_iota(jnp.int32, (Q, SSD_HPG), 1)
        sub_4q = lax.broadcasted_iota(jnp.int32, (SSD_HPG, Q), 0)
        row_sums = jnp.zeros((Q, SSD_HPG), F32)
        col_sums = jnp.zeros((SSD_HPG, Q), F32)
        dxdt_heads = []
        for k in range(SSD_HPG):
            blk = slice(P * k, P * (k + 1))
            L = _ssd_decay(k, ac4, ar4, causal)
            G = cb * L
            dG = _dot(dyv[:, blk], xdt[:, blk], NT)
            dxdt_heads.append(_dot(G, dyv[:, blk], TN))
            dcb = dcb + dG * L
            Mseg = dG * G
            row_sums = jnp.where(lane_q4 == k, jnp.sum(Mseg, axis=1, keepdims=True), row_sums)
            col_sums = jnp.where(sub_4q == k, jnp.sum(Mseg, axis=0, keepdims=True), col_sums)
        dxdt = dxw * F_w + jnp.concatenate(dxdt_heads, axis=1)
        last = lax.broadcasted_iota(jnp.int32, (Q, SSD_HPG), 0) == Q - 1
        dac4 = dac4 + row_sums + jnp.where(last, datot4, 0.0)
        dadt4 = _dot(triu_ref[...], dac4, NN, HI) - _dot(triu_ref[...], col_sums, NT, HI)
        ddt4 = _head_sums(dxdt * x) + dadt4 * A
        dalog_ref[...] += _colsum(dadt4 * dt) * A
        ddtr4 = ddt4 * jax.nn.sigmoid(pre)
        dbias_ref[...] += _colsum(ddtr4)
        ddtr_ref[...] = ddtr4
        dx_ref[...] = (_per_head(dsk_ref[...], W) * dyv + dxdt * dt_w).astype(dx_ref.dtype)
        db_ref[...] = dB + _dot(dcb, Cm, TN)
        dc_ref[...] = dC + _dot(dcb, Bm, NN)

    tri, triu = _tri(Q)
    return pl.pallas_call(
        body, name="ssd_bwd", grid=(SSD_GROUPS // SSD_STEP_GROUPS, NC),
        in_specs=[x_spec, b_spec, c_spec, dt_spec, p_spec, p_spec, p_spec, tri_spec, tri_spec, h_spec, dy_spec],
        out_specs=[dy_spec, dbc_spec, dbc_spec, dt_spec, p_spec, p_spec, p_spec],
        out_shape=[jax.ShapeDtypeStruct((T, D_INNER), F32),
                   jax.ShapeDtypeStruct((T, SSD_GROUPS * SSD_STATE), F32),
                   jax.ShapeDtypeStruct((T, SSD_GROUPS * SSD_STATE), F32),
                   jax.ShapeDtypeStruct(dtr8.shape, F32)] + [jax.ShapeDtypeStruct(bias8.shape, F32)] * 3,
        scratch_shapes=[pltpu.VMEM((SSD_STEP_GROUPS, W, SSD_STATE), F32)],
        compiler_params=_params(("parallel", "arbitrary")),
    )(xc, xc, xc, dtr8, bias8, alog8, dsk8, tri, triu, hs, dy)


def _to_groups(v):
    return v.reshape(SSD_GROUPS, 1, SSD_HPG)


def _mixer_fwd(x, p, w):
    T, D = x.shape
    (hn,) = _rowwise(lambda xt, g: (_rms(xt, g),), [x], [p["mix_norm"]], [(D, BF16)], name="mix_norm")
    winT = w["w_in"]
    u, z, xbc, dtr, gl = [
        _matmul(hn, winT[IN_OFFS[i]:IN_OFFS[i + 1]], "nt", name=f"mix_in{i}") for i in range(5)]
    tiles = _s5_derive(*[p[k][0] for k in ("s5_A_re", "s5_A_im", "s5_log_dt", "s5_B_re", "s5_B_im", "s5_C_re", "s5_C_im")])
    u = _to_seg(u)
    ys = _s5_fwd(u, tiles)
    (g,) = _rowwise(lambda yt, ut, d: (jax.nn.gelu(yt + d * ut),), [ys, u], [p["s5_D"]], [(D, F32)], name="s5_gelu")
    y5 = _matmul(g, w["s5_w_glu"], "nn", name="s5_glu", out=(BF16,),
                 epilogue=lambda acc, gt, b: (gt * jax.nn.sigmoid(acc + b),), extras=[g, p["s5_b_glu"]])
    xc = _conv_fwd(xbc, w["conv_w"], p["conv_b"])
    dtr8 = dtr.reshape(T, SSD_GROUPS, SSD_HPG).transpose(1, 0, 2)
    ssd_p = [_to_groups(p[k]) for k in ("ssd_dt_bias", "ssd_A_log", "ssd_D")]
    yssd_raw, hs = _ssd_fwd(xc, dtr8, *ssd_p)
    (yssd,) = _rowwise(lambda yt, zt, nw: (_rms(yt * jax.nn.silu(zt), nw),), [yssd_raw, z], [p["ssd_norm"]],
                       [(D_INNER, BF16)], name="ssd_gate")
    p5 = _from_seg(_matmul(y5, w["w_proj_s5"], "nn", name="mix_p5"))
    pssd = _matmul(yssd, w["w_proj_ssd"], "nn", name="mix_pssd")

    def merge(glt, at, bt, bg):
        gates = jax.nn.sigmoid(glt + bg)
        return (gates[:, :D] * at + gates[:, D:] * bt,)

    (merged,) = _rowwise(merge, [gl, p5, pssd], [p["b_gate"]], [(D, BF16)], name="mix_merge")
    y = _matmul(merged, w["w_out"], "nn", name="mix_out", epilogue=lambda acc, xt: (xt + acc,), extras=[x])
    res = dict(hn=hn, u=u, z=z, xbc=xbc, gl=gl, tiles=tiles, ys=ys, g=g, y5=y5, xc=xc, dtr8=dtr8, ssd_p=ssd_p,
               yssd_raw=yssd_raw, hs=hs, yssd=yssd, p5=p5, pssd=pssd, merged=merged)
    return y, res


def _mixer_bwd(dy, dy16, x, p, w, r):
    T, D = x.shape
    gp = {}
    dmerged = _matmul(dy16, w["w_out"], "nt", name="mix_d_merged")
    gp["w_out"] = _matmul(r["merged"], dy16, "tn", name="mix_d_wout")

    def merge_bwd(glt, at, bt, dm, bg):
        def f(q, a_, b_):
            gates = jax.nn.sigmoid(q + bg)
            return gates[:, :D] * a_ + gates[:, D:] * b_
        _, vjp = jax.vjp(f, glt, at, bt)
        dq, da_, db_ = vjp(dm)
        return da_, db_, dq, _colsum(dq)

    dp5, dpssd, dgl, gp["b_gate"] = _rowwise(
        merge_bwd, [r["gl"], r["p5"], r["pssd"], dmerged], [p["b_gate"]],
        [(D, BF16), (D, BF16), (2 * D, BF16)], [((1, 2 * D), F32)], name="mix_d_merge")
    dyssd = _matmul(dpssd, w["w_proj_ssd"], "nt", name="mix_d_yssd")
    gp["w_proj_ssd"] = _matmul(r["yssd"], dpssd, "tn", name="mix_d_wpssd")

    def gate_bwd(yt, zt, dyt, nw):
        _, vjp = jax.vjp(lambda a_, b_, c_: _rms(a_ * jax.nn.silu(b_), c_), yt, zt, nw)
        return vjp(dyt)

    dyraw, dz, gp["ssd_norm"] = _rowwise(
        gate_bwd, [r["yssd_raw"], r["z"], dyssd], [p["ssd_norm"]],
        [(D_INNER, F32), (D_INNER, BF16)], [((1, D_INNER), F32)], name="ssd_d_gate")
    dxs, dBm, dCm, ddtr8, dbias8, dalog8, ddsk8 = _ssd_bwd(r["xc"], r["dtr8"], *r["ssd_p"], r["hs"], dyraw)
    gp["ssd_dt_bias"], gp["ssd_A_log"], gp["ssd_D"] = [v.reshape(1, SSD_HEADS) for v in (dbias8, dalog8, ddsk8)]
    ddtr = ddtr8.transpose(1, 0, 2).reshape(T, SSD_HEADS)
    conv = [_conv_bwd(r["xbc"], w["conv_w"], p["conv_b"], d, c0, f"conv_bwd{i}")
            for i, (d, c0) in enumerate(((dxs, 0), (dBm, D_INNER), (dCm, D_INNER + SSD_GROUPS * SSD_STATE)))]
    dxbc = [c[0] for c in conv]
    gp["conv_w"] = jnp.concatenate([c[1] for c in conv], axis=1)
    gp["conv_b"] = jnp.concatenate([c[2] for c in conv], axis=1)

    dp5 = _to_seg(dp5)
    dy5 = _matmul(dp5, w["w_proj_s5"], "nt", name="mix_d_y5")
    gp["w_proj_s5"] = _matmul(r["y5"], dp5, "tn", name="mix_d_wp5")
    g = r["g"]

    def glu_ep(acc, gt, b, dyt):
        _, vjp = jax.vjp(lambda g_, t_: g_ * jax.nn.sigmoid(t_ + b), gt, acc)
        return vjp(dyt)

    dg1, dt_ = _matmul(g, w["s5_w_glu"], "nn", name="s5_d_glu", out=(F32, BF16), epilogue=glu_ep,
                       extras=[g, p["s5_b_glu"], dy5])
    gp["s5_w_glu"] = _matmul(g, dt_, "tn", name="s5_d_wglu")
    dg = _matmul(dt_, w["s5_w_glu"], "nt", name="s5_d_g", epilogue=lambda acc, e: (acc + e,), extras=[dg1])

    def gelu_bwd(yt, ut, dgt, dtt, d):
        _, vjp = jax.vjp(lambda y_, d_: jax.nn.gelu(y_ + d_ * ut), yt, d)
        dys, dd = vjp(dgt)
        return dys, dys * d, dd, _colsum(dtt.astype(F32))

    dys, dusk, gp["s5_D"], gp["s5_b_glu"] = _rowwise(
        gelu_bwd, [r["ys"], r["u"], dg, dt_], [p["s5_D"]], [(D, F32), (D, F32)], [((1, D), F32), ((1, D), F32)],
        name="s5_d_gelu")
    du, *dtiles = _s5_bwd(r["u"], dys, dusk, r["tiles"])
    du = _from_seg(du)

    pieces = [du, dz, None, ddtr, dgl]
    winT = w["w_in"]
    hn = r["hn"]
    d_rows, dhn = [], None
    cols = [(du, 0, 1024), (dz, 1024, 3072), (dxbc[0], 3072, 5120), (dxbc[1], 5120, 6144), (dxbc[2], 6144, 7168),
            (ddtr, 7168, 7200), (dgl, 7200, 9248)]
    for i, (d, lo, hi) in enumerate(cols):
        d_rows.append(_matmul(d, hn, "tn", name=f"mix_d_win{i}"))
        if dhn is None:
            dhn = _matmul(d, winT[lo:hi], "nn", name=f"mix_d_hn{i}")
        else:
            dhn = _matmul(d, winT[lo:hi], "nn", name=f"mix_d_hn{i}", epilogue=lambda acc, e: (acc + e,), extras=[dhn])
    gp["w_in"] = jnp.concatenate(d_rows, axis=0)
    del pieces

    def norm_bwd(xt, dh, dyt, gn):
        _, vjp = jax.vjp(_rms, xt, gn)
        dx, dgn = vjp(dh)
        return dyt + dx, dyt + dx, dgn

    dx, dx16, gp["mix_norm"] = _rowwise(norm_bwd, [x, dhn, dy], [p["mix_norm"]], [(D, F32), (D, BF16)],
                                        [((1, D), F32)], name="mix_d_norm")
    return dx, dx16, gp, dtiles


def _s5_param_grads(p, dtiles):
    keys = ("s5_A_re", "s5_A_im", "s5_log_dt", "s5_B_re", "s5_B_im", "s5_C_re", "s5_C_im")
    _, vjp = jax.vjp(_s5_derive, *[p[k][0] for k in keys])
    return {k: v[None] for k, v in zip(keys, vjp(tuple(dtiles)))}


def _local_step(x, target, p, w, late_weights, exchange):
    T, D = x.shape
    x1, r1 = _ffn_fwd(x, p["ffn1_norm"], w["ffn1_w_gate"], w["ffn1_w_up"], w["ffn1_w_down"], "ffn1")
    x2, rm = _mixer_fwd(x1, p, w)
    w = {**w, **late_weights(rm["p5"])}
    x3, r2 = _ffn_fwd(x2, p["ffn2_norm"], w["ffn2_w_gate"], w["ffn2_w_up"], w["ffn2_w_down"], "ffn2")

    def head(xt, tt, g):
        def f(x_, g_):
            e = _rms(x_, g_) - tt
            return 0.5 * jnp.sum(jnp.mean(e * e, axis=-1))
        l, (dx_, dg_) = jax.value_and_grad(f, argnums=(0, 1))(xt, g)
        return dx_, dx_, l.reshape(1, 1), dg_

    dx3, dx3_16, loss, d_final = _rowwise(head, [x3, target], [p["final_norm"]], [(D, F32), (D, BF16)],
                                          [((1, 1), F32), ((1, D), F32)], name="loss_head")
    gp = {"final_norm": d_final}

    def ffn_exchange(tag):
        return lambda d_wgT, d_wuT, d_wd, after: exchange(
            tag, {f"{tag}_w_gate": d_wgT, f"{tag}_w_up": d_wuT, f"{tag}_w_down": d_wd}, after)

    dx2, dx2_16, gp["ffn2_norm"] = _ffn_bwd(
        dx3, dx3_16, x2, p["ffn2_norm"], w["ffn2_w_gate"], w["ffn2_w_up"], w["ffn2_w_down"], r2, "ffn2", (),
        ffn_exchange("ffn2"))
    dx1, dx1_16, gm, dtiles = _mixer_bwd(dx2, dx2_16, x1, p, w, rm)
    conv_w_grad = gm.pop("conv_w")
    started = exchange("mixer", {k: gm.pop(k) for k in ("w_out", "w_proj_s5", "w_proj_ssd", "s5_w_glu", "w_in")}, dx1)
    gp.update(gm)
    gp.update(_s5_param_grads(p, dtiles))
    dx0, _, gp["ffn1_norm"] = _ffn_bwd(
        dx1, dx1_16, x, p["ffn1_norm"], w["ffn1_w_gate"], w["ffn1_w_up"], w["ffn1_w_down"], r1, "ffn1", started,
        ffn_exchange("ffn1"))
    return loss, dx0, gp, conv_w_grad


ADAMW_BLOCK = 1 << 18


def _adamw(w, m, v, gs, *, name, emit_g):
    _, R, C = w.shape
    n_g = len(gs)
    n_o = 4 if emit_g else 3
    tm = _rtile(R, max(8, ADAMW_BLOCK // C))
    lead = pl.BlockSpec((None, tm, C), lambda i: (0, i, 0))

    def body(w_ref, m_ref, v_ref, *refs):
        g = refs[0][...]
        for r in refs[1:n_g]:
            g = g + r[...]
        m2 = ADAM_B1 * m_ref[...] + (1.0 - ADAM_B1) * g
        v2 = ADAM_B2 * v_ref[...] + (1.0 - ADAM_B2) * (g * g)
        m_hat = m2 / (1.0 - ADAM_B1 ** ADAM_STEP)
        v_hat = v2 / (1.0 - ADAM_B2 ** ADAM_STEP)
        delta = -ADAM_LR * (m_hat / (jnp.sqrt(v_hat) + ADAM_EPS) + ADAM_WD * w_ref[...])
        for r, val in zip(refs[n_g:], ((g,) if emit_g else ()) + (delta, m2, v2)):
            r[...] = val

    return pl.pallas_call(
        body, name=name, grid=(R // tm,),
        in_specs=[lead] * 3 + [pl.BlockSpec((tm, C), lambda i: (i, 0))] * n_g, out_specs=[lead] * n_o,
        out_shape=[jax.ShapeDtypeStruct((1, R, C), F32)] * n_o,
        compiler_params=_params(("parallel",)),
    )(w, m, v, *gs)


def _rtile(r, cap=512):
    if r <= cap:
        return r
    best = None
    for t in range(8, cap + 1, 8):
        if r % t == 0:
            best = t
    assert best is not None, r
    return best


ANY = pl.BlockSpec(memory_space=pl.ANY)


def _where_am_i():
    x, y, c = lax.axis_index("x"), lax.axis_index("y"), lax.axis_index("c")
    ks = (2 * x + y, 2 * (1 - x) + y, 2 * x + (1 - y), 2 * (1 - x) + (1 - y))
    return x, y, c, ks


def _rc(src, dst, ssem, rsem, to):
    return pltpu.make_async_remote_copy(src_ref=src, dst_ref=dst, send_sem=ssem, recv_sem=rsem,
                                        device_id=to, device_id_type=MESH)


def _gather_shards(srcs):
    n = len(srcs)

    def body(*refs):
        _gather_body(refs[:n], refs[n:2 * n], *refs[2 * n:], extra=())

    return pl.pallas_call(
        body, name="gather_shards",
        in_specs=[ANY] * n, out_specs=[ANY] * n,
        out_shape=[jax.ShapeDtypeStruct((4,) + s.shape, s.dtype) for s in srcs],
        scratch_shapes=[pltpu.SemaphoreType.DMA((n, 6)), pltpu.SemaphoreType.DMA((n, 6)), pltpu.SemaphoreType.DMA((n,))],
        compiler_params=pltpu.CompilerParams(has_side_effects=True),
    )(*srcs)


def _handshake_all():
    x, y, c = lax.axis_index("x"), lax.axis_index("y"), lax.axis_index("c")
    barrier = pltpu.get_barrier_semaphore()
    for dx, dy, dc in itertools.product((0, 1), repeat=3):
        if (dx, dy, dc) != (0, 0, 0):
            to = (1 - x if dx else x, 1 - y if dy else y, 1 - c if dc else c)
            pl.semaphore_signal(barrier, inc=1, device_id=to, device_id_type=MESH)
    pl.semaphore_wait(barrier, 7)


SEQUENCER = dict(axis_name="seq", num_cores=1)


def _gather_shards_async(srcs, token, name, collective_id):
    n = len(srcs)

    def body(*refs):
        tok, tok_out = refs[n], refs[2 * n + 1]
        ssem, rsem, lsem = refs[2 * n + 2:]
        _handshake_all()
        _gather_body(refs[:n], refs[n + 1:2 * n + 1], ssem, rsem, lsem,
                     extra=(pltpu.make_async_copy(tok, tok_out, lsem.at[n]),))

    res = pl.kernel(
        body, name=name,
        out_type=[jax.ShapeDtypeStruct((4,) + s.shape, s.dtype) for s in srcs]
        + [jax.ShapeDtypeStruct(token.shape, token.dtype)],
        mesh=plsc.ScalarSubcoreMesh(**SEQUENCER),
        scratch_types=[pltpu.SemaphoreType.DMA((n, 6)), pltpu.SemaphoreType.DMA((n, 6)), pltpu.SemaphoreType.DMA((n + 1,))],
        compiler_params=pltpu.CompilerParams(collective_id=collective_id),
    )(*srcs, token)
    return res[:n], res[n]


def _gather_body(src, out, ssem, rsem, lsem, extra):
        n = len(src)
        x, y, c, (k_me, k_x, k_y, k_d) = _where_am_i()
        sib = (x, y, 1 - c)
        local = [pltpu.make_async_copy(src[i], out[i].at[k_me], lsem.at[i]) for i in range(n)] + list(extra)
        for cp in local:
            cp.start()

        def own(i, q, to):
            return _rc(src[i], out[i].at[k_me], ssem.at[i, q], rsem.at[i, q], to)

        def slab(i, q, k, to):
            return _rc(out[i].at[k], out[i].at[k], ssem.at[i, q], rsem.at[i, q], to)

        @pl.when(c == 1)
        def _():
            sends = [own(i, 0, (1 - x, y, 1)) for i in range(n)]
            for cp in sends:
                cp.start()
            fwd = [slab(i, 3, k_x, sib) for i in range(n)]
            for i in range(n):
                slab(i, 0, k_x, sib).wait_recv()
                fwd[i].start()
            for i in range(n):
                slab(i, 4, k_y, sib).wait_recv()
                slab(i, 5, k_d, sib).wait_recv()
            for cp in sends + fwd:
                cp.wait_send()

        @pl.when(c == 0)
        def _():
            sends = [own(i, 1, (x, 1 - y, 0)) for i in range(n)] + [own(i, 2, (1 - x, 1 - y, 0)) for i in range(n)]
            for cp in sends:
                cp.start()
            fwd = [slab(i, 4, k_y, sib) for i in range(n)] + [slab(i, 5, k_d, sib) for i in range(n)]
            for i in range(n):
                slab(i, 1, k_y, sib).wait_recv()
                fwd[i].start()
            for i in range(n):
                slab(i, 2, k_d, sib).wait_recv()
                fwd[n + i].start()
            for i in range(n):
                slab(i, 3, k_x, sib).wait_recv()
            for cp in sends + fwd:
                cp.wait_send()

        for cp in local:
            cp.wait()


def _swap_slabs(arrs, n_slabs, name, async_id, token):
    n = len(arrs)
    J = max(n_slabs, 1)

    def body(*refs):
        src, out = refs[:n], refs[n:2 * n]
        ssem, rsem = refs[2 * n:]
        x, y, c, (k_me, k_x, k_y, k_d) = _where_am_i()
        sib = (x, y, 1 - c)
        sel = (jnp.where(c == 1, k_y, k_me), jnp.where(c == 1, k_d, k_x))
        cps = []
        for i in range(n):
            for j in range(J):
                s = src[i].at[sel[j]] if n_slabs else src[i]
                d = out[i].at[j] if n_slabs else out[i]
                cps.append(_rc(s, d, ssem.at[i, j], rsem.at[i, j], sib))
        for cp in cps:
            cp.start()
        for cp in cps:
            cp.wait()

    return _exchange_call(
        body, arrs, [jax.ShapeDtypeStruct(((n_slabs,) if n_slabs else ()) + a.shape[-2:], a.dtype) for a in arrs],
        [pltpu.SemaphoreType.DMA((n, J)), pltpu.SemaphoreType.DMA((n, J))], name, async_id, token)


def _exchange_call(body, arrs, out_shape, sems, name, collective_id, token):
    n, m = len(arrs), len(out_shape)

    def seq_body(*refs):
        tok, tok_out, tok_sem = refs[n], refs[n + 1 + m], refs[-1]
        _handshake_all()
        cp = pltpu.make_async_copy(tok, tok_out, tok_sem)
        cp.start()
        body(*refs[:n], *refs[n + 1:n + 1 + m], *refs[n + 2 + m:-1])
        cp.wait()

    res = pl.kernel(
        seq_body, name=name, out_type=list(out_shape) + [jax.ShapeDtypeStruct(token.shape, token.dtype)],
        mesh=plsc.ScalarSubcoreMesh(**SEQUENCER), scratch_types=list(sems) + [pltpu.SemaphoreType.DMA],
        compiler_params=pltpu.CompilerParams(collective_id=collective_id))(*arrs, token)
    return res[:m], res[m]


def _send_chip_sums(arrs, name, async_id, token):
    n = len(arrs)

    def body(*refs):
        src, out = refs[:n], refs[n:2 * n]
        ssem, rsem = refs[2 * n:]
        x, y, c, _ = _where_am_i()

        @pl.when(c == 1)
        def _():
            cps = [_rc(src[i].at[1], out[i].at[0], ssem.at[i, 0], rsem.at[i, 0], (1 - x, y, 1)) for i in range(n)]
            for cp in cps:
                cp.start()
            for cp in cps:
                cp.wait()

        @pl.when(c == 0)
        def _():
            cps = [_rc(src[i].at[0], out[i].at[0], ssem.at[i, 0], rsem.at[i, 0], (x, 1 - y, 0)) for i in range(n)]
            cps += [_rc(src[i].at[1], out[i].at[1], ssem.at[i, 1], rsem.at[i, 1], (1 - x, 1 - y, 0)) for i in range(n)]
            for cp in cps:
                cp.start()
            for cp in cps:
                cp.wait()

    return _exchange_call(
        body, arrs, [jax.ShapeDtypeStruct(a.shape, a.dtype) for a in arrs],
        [pltpu.SemaphoreType.DMA((n, 2)), pltpu.SemaphoreType.DMA((n, 2))], name, async_id, token)


def _chip_sum(g, recv, sel, name):
    _, r, C = g.shape
    tr = _rtile(r)

    def body(sel_ref, g_ref, r_ref, o32_ref, o16_ref):
        s = g_ref[...] + r_ref[...]
        o32_ref[...] = s
        o16_ref[...] = s.astype(BF16)

    blk = pl.BlockSpec((None, tr, C), lambda j, t, sel_ref: (j, t, 0))
    return pl.pallas_call(
        body, name=name,
        grid_spec=pltpu.PrefetchScalarGridSpec(
            num_scalar_prefetch=1, grid=(2, r // tr),
            in_specs=[pl.BlockSpec((None, tr, C), lambda j, t, sel_ref: (sel_ref[j], t, 0)), blk],
            out_specs=[blk, blk]),
        out_shape=[jax.ShapeDtypeStruct((2, r, C), F32), jax.ShapeDtypeStruct((2, r, C), BF16)],
        compiler_params=_params(("arbitrary", "arbitrary")),
    )(sel, g, recv)


def _cross_sum(s32, got, north, after, name):
    _, r, C = s32.shape
    tr = _rtile(r)

    def body(f_ref, p_ref, a_ref, b_ref, after_ref, o_ref):
        a = a_ref[...].astype(F32)

        @pl.when(f_ref[0] == 1)
        def _():
            o_ref[...] = p_ref[...] + a

        @pl.when(f_ref[0] == 0)
        def _():
            o_ref[...] = a + b_ref[...].astype(F32)

    return pl.pallas_call(
        body, name=name,
        grid_spec=pltpu.PrefetchScalarGridSpec(
            num_scalar_prefetch=1, grid=(r // tr,),
            in_specs=[pl.BlockSpec((None, tr, C), lambda t, f: (0, t, 0)),
                      pl.BlockSpec((None, tr, C), lambda t, f: (0, t, 0)),
                      pl.BlockSpec((None, tr, C), lambda t, f: (1 - f[0], t, 0)), ANY],
            out_specs=pl.BlockSpec((tr, C), lambda t, f: (t, 0))),
        out_shape=jax.ShapeDtypeStruct((r, C), F32),
        compiler_params=_params(("arbitrary",)),
    )(north, s32, got, got, after)


class _Reducer:
    def __init__(self, token):
        self.token, self.pending, self.pairs = token, None, {}

    def begin(self, tag, ids, grads, after):
        names = list(grads)
        arrs = [grads[n].reshape(4, -1, D_MODEL) for n in names]
        x, y, c, (k_me, k_x, k_y, k_d) = _where_am_i()
        sel = jnp.stack([jnp.where(c == 1, k_me, k_y), jnp.where(c == 1, k_x, k_d)]).astype(jnp.int32)
        from_sib, self.token = _swap_slabs(arrs, 2, f"swap_grad_slabs_{tag}", ids[0], self.token)
        sums = [_chip_sum(g, r, sel, f"chip_sum_{tag}{i}") for i, (g, r) in enumerate(zip(arrs, from_sib))]
        self.finish(after)
        got, self.token = _send_chip_sums([s16 for _, s16 in sums], f"send_chip_sums_{tag}", ids[1], self.token)
        self.pending = (tag, ids[2], names, sums, got)
        return tuple(s16 for _, s16 in sums)

    def finish(self, after):
        if self.pending is None:
            return
        tag, cid, names, sums, got = self.pending
        north = jnp.reshape(lax.axis_index("c"), (1,)).astype(jnp.int32)
        parts = [_cross_sum(s32, b, north, after, f"cross_sum_{tag}{i}") for i, ((s32, _), b) in enumerate(zip(sums, got))]
        theirs, self.token = _swap_slabs(parts, 0, f"swap_parts_{tag}", cid, self.token)
        self.pairs.update(zip(names, zip(parts, theirs)))
        self.pending = None


def _all_reduce_small(v):
    R, C = v.shape

    def body(v_ref, o_ref, buf, ssem, rsem, lsem):
        x, y, c = lax.axis_index("x"), lax.axis_index("y"), lax.axis_index("c")
        me, sib = (x, y, c), (x, y, 1 - c)
        chips = [(1 - x, y), (x, 1 - y), (1 - x, 1 - y)]

        def slot(px, py, pc):
            return buf.at[4 * px + 2 * py + pc]

        def copy(k, block, to, src=None):
            return _rc(slot(*block) if src is None else src, slot(*block), ssem.at[k], rsem.at[k], to)

        mine = pltpu.make_async_copy(v_ref, slot(*me), lsem)
        mine.start()
        first = [copy(0, me, sib, src=v_ref)] + [copy(1 + j, me, (*chip, c), src=v_ref) for j, chip in enumerate(chips)]
        for cp in first:
            cp.start()
        passed = [copy(4 + j, (*chip, c), sib) for j, chip in enumerate(chips)]
        for j, chip in enumerate(chips):
            copy(1 + j, (*chip, c), me).wait_recv()
            passed[j].start()
        copy(0, sib, me).wait_recv()
        for j, chip in enumerate(chips):
            copy(4 + j, (*chip, 1 - c), me).wait_recv()
        for cp in first + passed:
            cp.wait_send()
        mine.wait()
        acc = buf[0]
        for k in range(1, 8):
            acc = acc + buf[k]
        o_ref[...] = acc

    return pl.pallas_call(
        body, name="all_reduce_small",
        in_specs=[pl.BlockSpec(memory_space=pltpu.VMEM)], out_specs=pl.BlockSpec(memory_space=pltpu.VMEM),
        out_shape=jax.ShapeDtypeStruct((R, C), F32),
        scratch_shapes=[pltpu.VMEM((8, R, C), F32), pltpu.SemaphoreType.DMA((7,)), pltpu.SemaphoreType.DMA((7,)),
                        pltpu.SemaphoreType.DMA],
        compiler_params=pltpu.CompilerParams(has_side_effects=True, vmem_limit_bytes=VMEM_LIMIT),
    )(v)


WEIGHTS = ['ffn1_norm', 'ffn1_w_gate', 'ffn1_w_up', 'ffn1_w_down', 'mix_norm', 'w_in', 'conv_w', 'conv_b', 's5_A_re',
           's5_A_im', 's5_log_dt', 's5_B_re', 's5_B_im', 's5_C_re', 's5_C_im', 's5_D', 's5_w_glu', 's5_b_glu',
           'ssd_A_log', 'ssd_dt_bias', 'ssd_D', 'ssd_norm', 'w_proj_s5', 'w_proj_ssd', 'b_gate', 'w_out', 'ffn2_norm',
           'ffn2_w_gate', 'ffn2_w_up', 'ffn2_w_down', 'final_norm']
ARGS = ['x'] + WEIGHTS + ['loss_target'] + ['m_' + n for n in WEIGHTS] + ['v_' + n for n in WEIGHTS]
COL_SHARDED = ('ffn1_w_gate', 'ffn1_w_up', 'w_in', 'ffn2_w_gate', 'ffn2_w_up')
ROW_SHARDED = ('ffn1_w_down', 's5_w_glu', 'w_proj_s5', 'w_proj_ssd', 'w_out', 'ffn2_w_down')
MATRICES = COL_SHARDED + ROW_SHARDED
GATHER_FIRST = ('ffn1_w_gate', 'ffn1_w_up', 'ffn1_w_down')
GATHER_SECOND = ('w_in', 'conv_w', 's5_w_glu', 'w_proj_s5', 'w_proj_ssd', 'w_out')
GATHER_THIRD = ('ffn2_w_gate', 'ffn2_w_up', 'ffn2_w_down')
SMALL = [n for n in WEIGHTS if n not in MATRICES]


def _pack(arrs, width=1024):
    flat = jnp.concatenate([a.reshape(-1).astype(F32) for a in arrs])
    rows = -(-flat.shape[0] // (8 * width)) * 8
    return jnp.pad(flat, (0, rows * width - flat.shape[0])).reshape(rows, width)


def _unpack(packed, shapes):
    flat, out, o = packed.reshape(-1), [], 0
    for s in shapes:
        n = math.prod(s)
        out.append(flat[o:o + n].reshape(s))
        o += n
    return out


def kernel(x, ffn1_norm, ffn1_w_gate, ffn1_w_up, ffn1_w_down, mix_norm, w_in, conv_w, conv_b, s5_A_re, s5_A_im, s5_log_dt, s5_B_re, s5_B_im, s5_C_re, s5_C_im, s5_D, s5_w_glu, s5_b_glu, ssd_A_log, ssd_dt_bias, ssd_D, ssd_norm, w_proj_s5, w_proj_ssd, b_gate, w_out, ffn2_norm, ffn2_w_gate, ffn2_w_up, ffn2_w_down, final_norm, loss_target, m_ffn1_norm, m_ffn1_w_gate, m_ffn1_w_up, m_ffn1_w_down, m_mix_norm, m_w_in, m_conv_w, m_conv_b, m_s5_A_re, m_s5_A_im, m_s5_log_dt, m_s5_B_re, m_s5_B_im, m_s5_C_re, m_s5_C_im, m_s5_D, m_s5_w_glu, m_s5_b_glu, m_ssd_A_log, m_ssd_dt_bias, m_ssd_D, m_ssd_norm, m_w_proj_s5, m_w_proj_ssd, m_b_gate, m_w_out, m_ffn2_norm, m_ffn2_w_gate, m_ffn2_w_up, m_ffn2_w_down, m_final_norm, v_ffn1_norm, v_ffn1_w_gate, v_ffn1_w_up, v_ffn1_w_down, v_mix_norm, v_w_in, v_conv_w, v_conv_b, v_s5_A_re, v_s5_A_im, v_s5_log_dt, v_s5_B_re, v_s5_B_im, v_s5_C_re, v_s5_C_im, v_s5_D, v_s5_w_glu, v_s5_b_glu, v_ssd_A_log, v_ssd_dt_bias, v_ssd_D, v_ssd_norm, v_w_proj_s5, v_w_proj_ssd, v_b_gate, v_w_out, v_ffn2_norm, v_ffn2_w_gate, v_ffn2_w_up, v_ffn2_w_down, v_final_norm):
    a = dict(locals())
    assert list(a) == ARGS
    x, target = a['x'][0], a['loss_target'][0]
    k_me = 2 * lax.axis_index("x") + lax.axis_index("y")

    src = {n: a[n][0].T.astype(BF16) for n in COL_SHARDED}
    src.update({n: a[n][0].astype(BF16) for n in ROW_SHARDED})
    src['conv_w'] = a['conv_w'][0]
    def flat(f):
        return f.reshape(-1, f.shape[-1])

    full = _gather_shards([src[n] for n in GATHER_FIRST])
    w = {n: flat(f) for n, f in zip(GATHER_FIRST, full)}
    token = full[0][0, :8, :128].astype(F32)
    for names, name, cid in ((GATHER_SECOND[:2], "gather_mixer_in", 1), (GATHER_SECOND[2:], "gather_mixer", 12)):
        got, token = _gather_shards_async([src[n] for n in names], token, name, cid)
        w.update({n: flat(f) for n, f in zip(names, got)})
    w['conv_w'] = w['conv_w'].reshape(4, CONV_K, -1).transpose(1, 0, 2).reshape(CONV_K, CONV_DIM)
    reducer = _Reducer(token)

    def gather_last(after):
        got, reducer.token = _gather_shards_async(
            [src[n] for n in GATHER_THIRD], reducer.token + after[:8, :128], "gather_ffn2", 2)
        return {n: flat(f) for n, f in zip(GATHER_THIRD, got)}

    p = {n: a[n] for n in SMALL if n != 'conv_w'}
    p['final_norm'] = a['final_norm'][None]
    ids = {"ffn2": (3, 4, 5), "mixer": (6, 7, 8), "ffn1": (9, 10, 11)}
    loss, grad_x, gp, conv_w_grad = _local_step(
        x, target, p, w, gather_last, lambda tag, grads, after: reducer.begin(tag, ids[tag], grads, after))
    reducer.finish(grad_x)
    pairs = [reducer.pairs[n] for n in MATRICES]
    gp['conv_w'] = conv_w_grad[None]
    small_shapes = [(1, 1)] + [gp[n].shape if n != 'final_norm' else (1, D_MODEL) for n in SMALL]
    red = _unpack(_all_reduce_small(_pack([loss] + [gp[n] for n in SMALL])), small_shapes)
    loss_all = red[0].reshape(())
    gsmall = dict(zip(SMALL, red[1:]))
    gsmall['conv_w'] = lax.dynamic_slice_in_dim(gsmall['conv_w'], k_me * 1024, 1024, axis=2)
    gsmall = {n: g.reshape(a[n].shape) for n, g in gsmall.items()}

    grads, delta, new_m, new_v = {}, {}, {}, {}
    for n, (mine, theirs) in zip(MATRICES, pairs):
        turn = (lambda t: jnp.swapaxes(t, 1, 2)) if n in COL_SHARDED else (lambda t: t)
        out = _adamw(turn(a[n]), turn(a['m_' + n]), turn(a['v_' + n]), [mine, theirs], name=f"adamw_{n}", emit_g=True)
        grads[n], delta[n], new_m[n], new_v[n] = [turn(t) for t in out]
    sw, sm, sv, sg = [_pack([t[n] for n in SMALL])[None] for t in (a, {n: a['m_' + n] for n in SMALL},
                                                                   {n: a['v_' + n] for n in SMALL}, gsmall)]
    d, m2, v2 = _adamw(sw, sm, sv, [sg[0]], name="adamw_small", emit_g=False)
    shapes = [a[n].shape for n in SMALL]
    for n, dd, mm, vv in zip(SMALL, _unpack(d[0], shapes), _unpack(m2[0], shapes), _unpack(v2[0], shapes)):
        grads[n], delta[n], new_m[n], new_v[n] = gsmall[n], dd, mm, vv
    return (loss_all, grad_x[None], *[grads[n] for n in WEIGHTS], *[delta[n] for n in WEIGHTS],
            *[new_m[n] for n in WEIGHTS], *[new_v[n] for n in WEIGHTS])
```

```python
import itertools
import math

import jax
import jax.numpy as jnp
from jax import lax
from jax.experimental import pallas as pl
from jax.experimental.pallas import tpu as pltpu
from jax.experimental.pallas import tpu_sc as plsc

F32 = jnp.float32
BF16 = jnp.bfloat16
MXU = BF16
HI = lax.Precision.HIGHEST

D_MODEL = 1024
D_FF = 2816
EPS = 1e-6
S5_GROUPS, S5_GROUP, S5_STATE = 64, 16, 64
D_INNER = 2048
SSD_HEADDIM, SSD_HEADS, SSD_GROUPS, SSD_HPG, SSD_STATE = 64, 32, 8, 4, 128
CONV_K, CONV_DIM = 4, 4096
IN_SPLITS = (1024, 2048, 4096, 32, 2048)
IN_OFFS = (0, 1024, 3072, 7168, 7200, 9248)
SSD_Q = 256
SSD_STEP_GROUPS = 2
ADAM_LR, ADAM_B1, ADAM_B2, ADAM_EPS, ADAM_WD, ADAM_STEP = 0.001, 0.9, 0.999, 1e-08, 0.01, 10

VMEM_LIMIT = 56 * 1024 * 1024
MATMUL_VMEM = 32 * 1024 * 1024
MESH = pl.DeviceIdType.MESH

NN = ((1,), (0,))
NT = ((1,), (1,))
TN = ((0,), (0,))


def _dot(a, b, dims, precision=None):
    if precision is None:
        a, b = a.astype(MXU), b.astype(MXU)
    return lax.dot_general(a, b, (dims, ((), ())), precision=precision, preferred_element_type=F32)


def _tile(n, pref):
    if n <= pref:
        return n
    best = None
    for t in range(128, pref + 1, 128):
        if n % t == 0:
            best = t
    assert best is not None, (n, pref)
    return best


def _params(sem):
    return pltpu.CompilerParams(dimension_semantics=sem, vmem_limit_bytes=VMEM_LIMIT)


def _matmul(a, b, mode, *, name, out=((F32),), epilogue=None, extras=(), after=(), tm=512, tn=1536):
    if mode == "nn":
        (M, K), (_, N) = a.shape, b.shape
    elif mode == "nt":
        (M, K), (N, _) = a.shape, b.shape
    else:
        (K, M), (_, N) = a.shape, b.shape
    tm, tn = _tile(M, tm), _tile(N, tn)

    def vmem(tm_, tn_):
        per = K * tm_ * a.dtype.itemsize + K * tn_ * b.dtype.itemsize
        per += sum(tm_ * tn_ * jnp.dtype(dt).itemsize for dt in out)
        per += sum((1 if e.shape[0] == 1 else tm_) * tn_ * e.dtype.itemsize for e in extras)
        return 2 * per

    while vmem(tm, tn) > MATMUL_VMEM and (tn > 128 or tm > 128):
        if tn >= tm and tn > 128:
            tn = _tile(N, tn - 128)
        else:
            tm = _tile(M, tm - 128)
    bytes_a, bytes_b = a.size * a.dtype.itemsize, b.size * b.dtype.itemsize
    rows_outer = bytes_a + (M // tm) * bytes_b <= (N // tn) * bytes_a + bytes_b
    grid = (M // tm, N // tn) if rows_outer else (N // tn, M // tm)

    def at(f):
        return (lambda i, j: f(i, j)) if rows_outer else (lambda j, i: f(i, j))

    a_spec = pl.BlockSpec((K, tm), at(lambda i, j: (0, i))) if mode == "tn" else pl.BlockSpec((tm, K), at(lambda i, j: (i, 0)))
    b_spec = pl.BlockSpec((tn, K), at(lambda i, j: (j, 0))) if mode == "nt" else pl.BlockSpec((K, tn), at(lambda i, j: (0, j)))
    dims = {"nn": NN, "nt": NT, "tn": TN}[mode]
    e_specs = []
    for e in extras:
        if e.shape[0] == 1:
            e_specs.append(pl.BlockSpec((1, tn), at(lambda i, j: (0, j))))
        else:
            assert e.shape == (M, N), (e.shape, M, N)
            e_specs.append(pl.BlockSpec((tm, tn), at(lambda i, j: (i, j))))
    n_e, n_o, n_a = len(extras), len(out), len(after)

    def body(a_ref, b_ref, *refs):
        acc = _dot(a_ref[...], b_ref[...], dims)
        res = (acc,) if epilogue is None else epilogue(acc, *[r[...] for r in refs[:n_e]])
        for r, v in zip(refs[n_e + n_a:], res):
            r[...] = v.astype(r.dtype)

    res = pl.pallas_call(
        body, name=name, grid=grid,
        in_specs=[a_spec, b_spec] + e_specs + [ANY] * n_a,
        out_specs=[pl.BlockSpec((tm, tn), at(lambda i, j: (i, j))) for _ in range(n_o)],
        out_shape=[jax.ShapeDtypeStruct((M, N), dt) for dt in out],
        compiler_params=_params(("parallel", "parallel")),
    )(a, b, *extras, *after)
    return res[0] if n_o == 1 else res


def _rowwise(fn, rows, fulls, outs, reds=(), *, name, after=(), tm=256):
    T = rows[0].shape[0]
    tm = min(tm, T)
    n_r, n_f, n_o, n_d, n_a = len(rows), len(fulls), len(outs), len(reds), len(after)

    def body(*refs):
        ins = [r[...] for r in refs[:n_r + n_f]]
        o_refs = refs[n_r + n_f + n_a:n_r + n_f + n_a + n_o]
        d_refs = refs[n_r + n_f + n_a + n_o:]
        res = fn(*ins)
        for r, v in zip(o_refs, res[:n_o]):
            r[...] = v.astype(r.dtype)
        if n_d:
            @pl.when(pl.program_id(0) == 0)
            def _():
                for r in d_refs:
                    r[...] = jnp.zeros_like(r)
            for r, v in zip(d_refs, res[n_o:]):
                r[...] += v.astype(r.dtype)

    res = pl.pallas_call(
        body, name=name, grid=(T // tm,),
        in_specs=[pl.BlockSpec((tm, r.shape[1]), lambda i: (i, 0)) for r in rows]
        + [pl.BlockSpec(f.shape, lambda i, nd=f.ndim: (0,) * nd) for f in fulls] + [ANY] * n_a,
        out_specs=[pl.BlockSpec((tm, c), lambda i: (i, 0)) for c, _ in outs]
        + [pl.BlockSpec(s, lambda i, nd=len(s): (0,) * nd) for s, _ in reds],
        out_shape=[jax.ShapeDtypeStruct((T, c), dt) for c, dt in outs]
        + [jax.ShapeDtypeStruct(s, dt) for s, dt in reds],
        compiler_params=_params(("arbitrary",)),
    )(*rows, *fulls, *after)
    return res


def _rms(x, g):
    return x * lax.rsqrt(jnp.mean(x * x, axis=-1, keepdims=True) + EPS) * g


def _colsum(v):
    return jnp.sum(v, axis=0, keepdims=True)


def _softplus(x):
    return jnp.maximum(x, 0.0) + jnp.log1p(jnp.exp(-jnp.abs(x)))


def _ffn_fwd(x, norm, wgT, wuT, wd, tag):
    D = x.shape[1]
    (hn,) = _rowwise(lambda xt, g: (_rms(xt, g),), [x], [norm], [(D, BF16)], name=f"{tag}_norm")
    a = _matmul(hn, wgT, "nt", name=f"{tag}_gate", out=(BF16,))
    b, hid = _matmul(hn, wuT, "nt", name=f"{tag}_up", out=(BF16, BF16),
                     epilogue=lambda acc, at: (acc, jax.nn.silu(at.astype(F32)) * acc), extras=[a])
    y = _matmul(hid, wd, "nn", name=f"{tag}_down", epilogue=lambda acc, xt: (xt + 0.5 * acc,), extras=[x])
    return y, (hn, a, b, hid)


def _ffn_bwd(dy, dy16, x, norm, wgT, wuT, wd, res, tag, after, on_grads):
    hn, a, b, hid = res
    D, F = x.shape[1], a.shape[1]

    def act_bwd(acc, at, bt):
        _, vjp = jax.vjp(lambda p, q: jax.nn.silu(p) * q, at.astype(F32), bt.astype(F32))
        return vjp(0.5 * acc)

    da, db = _matmul(dy16, wd, "nt", name=f"{tag}_d_hid", out=(BF16, BF16), epilogue=act_bwd, extras=[a, b])
    d_wd = _matmul(hid, dy16, "tn", name=f"{tag}_d_wd", epilogue=lambda acc: (0.5 * acc,))
    d_wgT = _matmul(da, hn, "tn", name=f"{tag}_d_wg", after=after)
    d_wuT = _matmul(db, hn, "tn", name=f"{tag}_d_wu")
    started = on_grads(d_wgT, d_wuT, d_wd, dy)
    dhn = _matmul(da, wgT, "nn", name=f"{tag}_d_hn1")
    dhn = _matmul(db, wuT, "nn", name=f"{tag}_d_hn2", epilogue=lambda acc, e: (acc + e,), extras=[dhn])

    def norm_bwd(xt, dh, dyt, g):
        _, vjp = jax.vjp(_rms, xt, g)
        dx, dg = vjp(dh)
        return dyt + dx, dyt + dx, dg

    dx, dx16, d_norm = _rowwise(norm_bwd, [x, dhn, dy], [norm], [(D, F32), (D, BF16)], [((1, D), F32)],
                                name=f"{tag}_d_norm", after=started)
    return dx, dx16, d_norm


S5_TILES = 8
S5_HALF = 256


def _s5_derive(A_re, A_im, log_dt, B_re, B_im, C_re, C_im):
    G, N, M = S5_GROUPS, S5_STATE, S5_GROUP
    dt = jnp.exp(log_dt)[:, None]
    mag = jnp.exp(A_re * dt)
    ar, ai = mag * jnp.cos(A_im * dt), mag * jnp.sin(A_im * dt)
    den = A_re * A_re + A_im * A_im
    cr = ((ar - 1.0) * A_re + ai * A_im) / den
    ci = (ai * A_re - (ar - 1.0) * A_im) / den
    bbr = cr[..., None] * B_re - ci[..., None] * B_im
    bbi = cr[..., None] * B_im + ci[..., None] * B_re
    eye = jnp.eye(8, dtype=F32)

    def tile_in(bb):
        t = bb.reshape(S5_TILES, 8, N, M).transpose(0, 1, 3, 2)
        return jnp.einsum("jamn,ab->jambn", t, eye).reshape(S5_TILES, 8 * M, 8 * N)

    def tile_out(c):
        t = c.reshape(S5_TILES, 8, M, N).transpose(0, 1, 3, 2)
        return jnp.einsum("janm,ab->janbm", t, eye).reshape(S5_TILES, 8 * N, 8 * M)

    return (tile_in(bbr), tile_in(bbi), tile_out(C_re), tile_out(C_im),
            ar.reshape(S5_TILES, 1, 8 * N), ai.reshape(S5_TILES, 1, 8 * N))


S5_NB = S5_HALF // 128
S5_SEG = 32


def _cmul(ar, ai, br, bi):
    return ar * br - ai * bi, ar * bi + ai * br


def _s5_scan(sr_ref, si_ref, ar, ai, T, reverse):
    L, V = T // S5_SEG, S5_SEG // 8
    assert L * S5_SEG == T and L & (L - 1) == 0, T
    sg = -1.0 if reverse else 1.0
    lanes = [slice(128 * b, 128 * (b + 1)) for b in range(S5_NB)]
    a_r = [jnp.broadcast_to(ar[:, 128 * b:128 * (b + 1)], (8, 128)) for b in range(S5_NB)]
    a_i = [jnp.broadcast_to(sg * ai[:, 128 * b:128 * (b + 1)], (8, 128)) for b in range(S5_NB)]

    def rows(k, v):
        return pl.ds(pl.multiple_of(((L - 1 - k) if reverse else k) * S5_SEG + 8 * v, 8), 8)

    def local(k, carry):
        out = []
        for b in range(S5_NB):
            for v in range(V):
                idx = rows(k, v)
                mr, mi = _cmul(a_r[b], a_i[b], *carry[b * V + v])
                nr, ni = mr + sr_ref[idx, lanes[b]], mi + si_ref[idx, lanes[b]]
                sr_ref[idx, lanes[b]] = nr
                si_ref[idx, lanes[b]] = ni
                out.append((nr, ni))
        return tuple(out)

    z = jnp.zeros((8, 128), F32)
    ends = lax.fori_loop(0, L, local, tuple((z, z) for _ in range(S5_NB * V)))

    carries = []
    for b in range(S5_NB):
        pr, pi = a_r[b][0:1], a_i[b][0:1]
        n = L
        while n > 1:
            pr, pi = _cmul(pr, pi, pr, pi)
            n //= 2
        cr, ci = jnp.zeros((1, 128), F32), jnp.zeros((1, 128), F32)
        into = [None] * S5_SEG
        for j in (reversed(range(S5_SEG)) if reverse else range(S5_SEG)):
            into[j] = (cr, ci)
            er, ei = ends[b * V + j // 8]
            mr, mi = _cmul(pr, pi, cr, ci)
            cr, ci = mr + er[j % 8:j % 8 + 1], mi + ei[j % 8:j % 8 + 1]
        carries.append([(jnp.concatenate([into[8 * v + s][0] for s in range(8)], axis=0),
                         jnp.concatenate([into[8 * v + s][1] for s in range(8)], axis=0)) for v in range(V)])

    def fix(k, powers):
        out = []
        for b in range(S5_NB):
            pr, pi = powers[b]
            for v in range(V):
                idx = rows(k, v)
                dr, di = _cmul(pr, pi, *carries[b][v])
                sr_ref[idx, lanes[b]] += dr
                si_ref[idx, lanes[b]] += di
            out.append(_cmul(pr, pi, a_r[b], a_i[b]))
        return tuple(out)

    lax.fori_loop(0, L, fix, tuple((a_r[b], a_i[b]) for b in range(S5_NB)))


def _to_seg(v):
    T, C = v.shape
    return v.reshape(S5_SEG, T // S5_SEG, C).transpose(1, 0, 2).reshape(T, C)


def _from_seg(v):
    T, C = v.shape
    return v.reshape(T // S5_SEG, S5_SEG, C).transpose(1, 0, 2).reshape(T, C)


def _s5_specs(T):
    u_spec = pl.BlockSpec((T, 128), lambda j, h: (0, j))
    b_spec = pl.BlockSpec((None, 128, S5_HALF), lambda j, h: (j, 0, h))
    c_spec = pl.BlockSpec((None, S5_HALF, 128), lambda j, h: (j, h, 0))
    a_spec = pl.BlockSpec((None, 1, S5_HALF), lambda j, h: (j, 0, h))
    return u_spec, b_spec, c_spec, a_spec


def _s5_fwd(u, tiles):
    T = u.shape[0]
    u_spec, b_spec, c_spec, a_spec = _s5_specs(T)

    def body(u_ref, br_ref, bi_ref, cr_ref, ci_ref, ar_ref, ai_ref, y_ref, sr_ref, si_ref):
        ut = u_ref[...].astype(MXU)
        sr_ref[...] = _dot(ut, br_ref[...], NN)
        si_ref[...] = _dot(ut, bi_ref[...], NN)
        _s5_scan(sr_ref, si_ref, ar_ref[...], ai_ref[...], T, False)
        y = _dot(sr_ref[...], cr_ref[...], NN) - _dot(si_ref[...], ci_ref[...], NN)

        @pl.when(pl.program_id(1) == 0)
        def _():
            y_ref[...] = y

        @pl.when(pl.program_id(1) == 1)
        def _():
            y_ref[...] += y

    scr = pltpu.VMEM((T, S5_HALF), F32)
    return pl.pallas_call(
        body, name="s5_fwd", grid=(S5_TILES, 2),
        in_specs=[u_spec, b_spec, b_spec, c_spec, c_spec, a_spec, a_spec],
        out_specs=u_spec, out_shape=jax.ShapeDtypeStruct(u.shape, F32),
        scratch_shapes=[scr, scr],
        compiler_params=_params(("parallel", "arbitrary")),
    )(u, *tiles)


def _s5_bwd(u, dys, du_skip, tiles):
    T = u.shape[0]
    u_spec, b_spec, c_spec, a_spec = _s5_specs(T)

    def body(u_ref, dy_ref, sk_ref, br_ref, bi_ref, cr_ref, ci_ref, ar_ref, ai_ref,
             du_ref, dbr_ref, dbi_ref, dcr_ref, dci_ref, dar_ref, dai_ref, sr_ref, si_ref, lr_ref, li_ref):
        ut, dy = u_ref[...].astype(MXU), dy_ref[...].astype(MXU)
        ar, ai = ar_ref[...], ai_ref[...]
        sr_ref[...] = _dot(ut, br_ref[...], NN)
        si_ref[...] = _dot(ut, bi_ref[...], NN)
        _s5_scan(sr_ref, si_ref, ar, ai, T, False)
        dcr_ref[...] = _dot(sr_ref[...], dy, TN)
        dci_ref[...] = -_dot(si_ref[...], dy, TN)
        lr_ref[...] = _dot(dy, cr_ref[...], NT)
        li_ref[...] = -_dot(dy, ci_ref[...], NT)
        _s5_scan(lr_ref, li_ref, ar, ai, T, True)
        head = lax.broadcasted_iota(jnp.int32, (T, S5_HALF), 0) < S5_SEG
        seg0 = lax.broadcasted_iota(jnp.int32, (S5_SEG, S5_HALF), 0) == 0
        lr, li = lr_ref[...], li_ref[...]
        pr = jnp.where(head, 0.0, pltpu.roll(sr_ref[...], S5_SEG, 0))
        pi = jnp.where(head, 0.0, pltpu.roll(si_ref[...], S5_SEG, 0))
        er = jnp.where(seg0, 0.0, pltpu.roll(sr_ref[T - S5_SEG:T, :], 1, 0))
        ei = jnp.where(seg0, 0.0, pltpu.roll(si_ref[T - S5_SEG:T, :], 1, 0))
        hr, hi = lr[0:S5_SEG], li[0:S5_SEG]
        dar_ref[...] = _colsum(lr * pr + li * pi) + _colsum(hr * er + hi * ei)
        dai_ref[...] = _colsum(li * pr - lr * pi) + _colsum(hi * er - hr * ei)
        dbr_ref[...] = _dot(ut, lr, TN)
        dbi_ref[...] = _dot(ut, li, TN)
        du = _dot(lr, br_ref[...], NT) + _dot(li, bi_ref[...], NT)

        @pl.when(pl.program_id(1) == 0)
        def _():
            du_ref[...] = du + sk_ref[...]

        @pl.when(pl.program_id(1) == 1)
        def _():
            du_ref[...] += du

    scr = pltpu.VMEM((T, S5_HALF), F32)
    return pl.pallas_call(
        body, name="s5_bwd", grid=(S5_TILES, 2),
        in_specs=[u_spec, u_spec, u_spec, b_spec, b_spec, c_spec, c_spec, a_spec, a_spec],
        out_specs=[u_spec, b_spec, b_spec, c_spec, c_spec, a_spec, a_spec],
        out_shape=[jax.ShapeDtypeStruct(u.shape, F32)] + [jax.ShapeDtypeStruct(t.shape, F32) for t in tiles],
        scratch_shapes=[scr, scr, scr, scr],
        compiler_params=_params(("parallel", "arbitrary")),
    )(u, dys, du_skip, *tiles)


CONV_TILE = 256


def _conv_pre(x, w, b):
    T = x.shape[0]
    row = lax.broadcasted_iota(jnp.int32, x.shape, 0)
    acc = x * w[CONV_K - 1:CONV_K, :] + b
    for lag in range(1, CONV_K):
        acc = acc + jnp.where(row >= lag, pltpu.roll(x, lag, 0), 0.0) * w[CONV_K - 1 - lag:CONV_K - lag, :]
    return acc


def _conv_fwd(x, w, b):
    T, C = x.shape
    col = pl.BlockSpec((T, CONV_TILE), lambda j: (0, j))

    def body(x_ref, w_ref, b_ref, o_ref):
        o_ref[...] = jax.nn.silu(_conv_pre(x_ref[...], w_ref[...], b_ref[...]))

    return pl.pallas_call(
        body, name="conv_fwd", grid=(C // CONV_TILE,),
        in_specs=[col, pl.BlockSpec((CONV_K, CONV_TILE), lambda j: (0, j)), pl.BlockSpec((1, CONV_TILE), lambda j: (0, j))],
        out_specs=col, out_shape=jax.ShapeDtypeStruct((T, C), F32),
        compiler_params=_params(("parallel",)),
    )(x, w, b)


def _conv_bwd(x, w, b, dout, col0, name):
    T, C = dout.shape
    off = col0 // CONV_TILE
    xcol = pl.BlockSpec((T, CONV_TILE), lambda j: (0, j + off))
    dcol = pl.BlockSpec((T, CONV_TILE), lambda j: (0, j))

    def body(x_ref, w_ref, b_ref, d_ref, dx_ref, dw_ref, db_ref):
        x, w = x_ref[...], w_ref[...]
        _, vjp = jax.vjp(jax.nn.silu, _conv_pre(x, w, b_ref[...]))
        (dy,) = vjp(d_ref[...])
        row = lax.broadcasted_iota(jnp.int32, x.shape, 0)
        dx = dy * w[CONV_K - 1:CONV_K, :]
        dw_ref[CONV_K - 1:CONV_K, :] = _colsum(dy * x)
        for lag in range(1, CONV_K):
            ahead = jnp.where(row < T - lag, pltpu.roll(dy, T - lag, 0), 0.0)
            dx = dx + ahead * w[CONV_K - 1 - lag:CONV_K - lag, :]
            dw_ref[CONV_K - 1 - lag:CONV_K - lag, :] = _colsum(ahead * x)
        dx_ref[...] = dx.astype(dx_ref.dtype)
        db_ref[...] = _colsum(dy)

    return pl.pallas_call(
        body, name=name, grid=(C // CONV_TILE,),
        in_specs=[xcol, pl.BlockSpec((CONV_K, CONV_TILE), lambda j: (0, j + off)),
                  pl.BlockSpec((1, CONV_TILE), lambda j: (0, j + off)), dcol],
        out_specs=[dcol, pl.BlockSpec((CONV_K, CONV_TILE), lambda j: (0, j)), pl.BlockSpec((1, CONV_TILE), lambda j: (0, j))],
        out_shape=[jax.ShapeDtypeStruct((T, C), BF16), jax.ShapeDtypeStruct((CONV_K, C), F32), jax.ShapeDtypeStruct((1, C), F32)],
        compiler_params=_params(("parallel",)),
    )(x, w, b, dout)


def _ssd_common(x_ref, b_ref, c_ref, dtr_ref, bias_ref, alog_ref, tri_ref, triu_ref):
    Q = x_ref.shape[0]
    x, Bm, Cm = x_ref[...], b_ref[...], c_ref[...]
    pre = dtr_ref[...] + bias_ref[...]
    dt = _softplus(pre)
    A = -jnp.exp(alog_ref[...])
    adt = dt * A
    ac4 = _dot(tri_ref[...], adt, NN, HI)
    ar4 = _dot(adt, triu_ref[...], TN, HI)
    cb = _dot(Cm, Bm, NT)
    ii = lax.broadcasted_iota(jnp.int32, (Q, Q), 0)
    jj = lax.broadcasted_iota(jnp.int32, (Q, Q), 1)
    atot4 = ac4[Q - 1:Q, :]
    return x, Bm, Cm, pre, dt, A, ac4, ar4, cb, ii >= jj, atot4


def _ssd_decay(k, ac4, ar4, causal):
    seg = ac4[:, k:k + 1] - ar4[k:k + 1, :]
    return jnp.where(causal, jnp.exp(jnp.where(causal, seg, 0.0)), 0.0)


def _per_head(c4, n):
    lane = lax.broadcasted_iota(jnp.int32, (c4.shape[0], n), 1)
    out = jnp.broadcast_to(c4[:, 0:1], (c4.shape[0], n))
    for k in range(1, SSD_HPG):
        out = jnp.where(lane >= SSD_HEADDIM * k, c4[:, k:k + 1], out)
    return out


def _head_sums(v):
    n = SSD_HPG * SSD_HEADDIM
    row = lax.broadcasted_iota(jnp.int32, (n, SSD_HPG), 0)
    col = lax.broadcasted_iota(jnp.int32, (n, SSD_HPG), 1)
    member = jnp.where((row >= SSD_HEADDIM * col) & (row < SSD_HEADDIM * (col + 1)), 1.0, 0.0)
    return _dot(v, member, NN, HI)


def _head_rows(c4):
    n = SSD_HPG * SSD_HEADDIM
    row = lax.broadcasted_iota(jnp.int32, (n, 1), 0)
    out = jnp.broadcast_to(c4[:, 0:1], (n, 1))
    for k in range(1, SSD_HPG):
        out = jnp.where(row >= SSD_HEADDIM * k, c4[:, k:k + 1], out)
    return out


def _ssd_specs(T, Q, rev):
    NC = T // Q
    cc = (lambda c: NC - 1 - c) if rev else (lambda c: c)
    W, N, S = SSD_HPG * SSD_HEADDIM, SSD_STATE, SSD_STEP_GROUPS
    x_spec = pl.BlockSpec((Q, S * W), lambda g, c: (cc(c), g))
    b_spec = pl.BlockSpec((Q, S * N), lambda g, c: (cc(c), D_INNER // (S * N) + g))
    c_spec = pl.BlockSpec((Q, S * N), lambda g, c: (cc(c), (D_INNER + SSD_GROUPS * N) // (S * N) + g))
    dt_spec = pl.BlockSpec((S, Q, SSD_HPG), lambda g, c: (g, cc(c), 0))
    p_spec = pl.BlockSpec((S, 1, SSD_HPG), lambda g, c: (g, 0, 0))
    tri_spec = pl.BlockSpec((Q, Q), lambda g, c: (0, 0))
    h_spec = pl.BlockSpec((S, None, W, N), lambda g, c: (g, cc(c), 0, 0))
    return x_spec, b_spec, c_spec, dt_spec, p_spec, tri_spec, h_spec


def _ssd_views(gi, wide, narrow, lead):
    W, N = SSD_HPG * SSD_HEADDIM, SSD_STATE
    return ([r.at[:, pl.ds(W * gi, W)] for r in wide] + [r.at[:, pl.ds(N * gi, N)] for r in narrow]
            + [r.at[gi] for r in lead])


def _tri(Q):
    tri = jnp.tril(jnp.ones((Q, Q), F32))
    return tri, tri.T


def _ssd_fwd(xc, dtr8, bias8, alog8, dsk8):
    T, Q = xc.shape[0], SSD_Q
    NC = T // Q
    P, W = SSD_HEADDIM, SSD_HPG * SSD_HEADDIM
    x_spec, b_spec, c_spec, dt_spec, p_spec, tri_spec, h_spec = _ssd_specs(T, Q, False)

    def body(x_ref, b_ref, c_ref, dtr_ref, bias_ref, alog_ref, dsk_ref, tri_ref, triu_ref, y_ref, hs_ref, h_scr):
        for gi in range(SSD_STEP_GROUPS):
            xv, yv, bv, cv, dtv, biasv, alogv, dskv, hsv, hv = _ssd_views(
                gi, (x_ref, y_ref), (b_ref, c_ref), (dtr_ref, bias_ref, alog_ref, dsk_ref, hs_ref, h_scr))
            one(xv, bv, cv, dtv, biasv, alogv, dskv, tri_ref, triu_ref, yv, hsv, hv)

    def one(x_ref, b_ref, c_ref, dtr_ref, bias_ref, alog_ref, dsk_ref, tri_ref, triu_ref, y_ref, hs_ref, h_scr):
        @pl.when(pl.program_id(1) == 0)
        def _():
            h_scr[...] = jnp.zeros_like(h_scr)

        hin = h_scr[...]
        hs_ref[...] = hin
        x, Bm, Cm, pre, dt, A, ac4, ar4, cb, causal, atot4 = _ssd_common(
            x_ref, b_ref, c_ref, dtr_ref, bias_ref, alog_ref, tri_ref, triu_ref)
        xdt = x * _per_head(dt, W)
        rest = _dot(Cm, hin, NT) * _per_head(jnp.exp(ac4), W) + _per_head(dsk_ref[...], W) * x
        for k in range(SSD_HPG):
            blk = slice(P * k, P * (k + 1))
            G = cb * _ssd_decay(k, ac4, ar4, causal)
            y_ref[:, blk] = _dot(G, xdt[:, blk], NN) + rest[:, blk]
        xw = xdt * _per_head(jnp.exp(atot4 - ac4), W)
        h_scr[...] = _head_rows(jnp.exp(atot4)) * hin + _dot(xw, Bm, TN)

    tri, triu = _tri(Q)
    return pl.pallas_call(
        body, name="ssd_fwd", grid=(SSD_GROUPS // SSD_STEP_GROUPS, NC),
        in_specs=[x_spec, b_spec, c_spec, dt_spec, p_spec, p_spec, p_spec, tri_spec, tri_spec],
        out_specs=[pl.BlockSpec((Q, SSD_STEP_GROUPS * W), lambda g, c: (c, g)), h_spec],
        out_shape=[jax.ShapeDtypeStruct((T, D_INNER), F32),
                   jax.ShapeDtypeStruct((SSD_GROUPS, NC, W, SSD_STATE), F32)],
        scratch_shapes=[pltpu.VMEM((SSD_STEP_GROUPS, W, SSD_STATE), F32)],
        compiler_params=_params(("parallel", "arbitrary")),
    )(xc, xc, xc, dtr8, bias8, alog8, dsk8, tri, triu)


def _ssd_bwd(xc, dtr8, bias8, alog8, dsk8, hs, dy):
    T, Q = xc.shape[0], SSD_Q
    NC = T // Q
    P, W = SSD_HEADDIM, SSD_HPG * SSD_HEADDIM
    x_spec, b_spec, c_spec, dt_spec, p_spec, tri_spec, h_spec = _ssd_specs(T, Q, True)
    dy_spec = pl.BlockSpec((Q, SSD_STEP_GROUPS * W), lambda g, c: (NC - 1 - c, g))
    dbc_spec = pl.BlockSpec((Q, SSD_STEP_GROUPS * SSD_STATE), lambda g, c: (NC - 1 - c, g))

    def body(x_ref, b_ref, c_ref, dtr_ref, bias_ref, alog_ref, dsk_ref, tri_ref, triu_ref, hs_ref, dy_ref,
             dx_ref, db_ref, dc_ref, ddtr_ref, dbias_ref, dalog_ref, ddsk_ref, dh_scr):
        for gi in range(SSD_STEP_GROUPS):
            (xv, dyv, dxv, bv, cv, dbv, dcv, dtv, biasv, alogv, dskv, hsv, ddtv, dbiasv, dalogv, ddskv, dhv) = _ssd_views(
                gi, (x_ref, dy_ref, dx_ref), (b_ref, c_ref, db_ref, dc_ref),
                (dtr_ref, bias_ref, alog_ref, dsk_ref, hs_ref, ddtr_ref, dbias_ref, dalog_ref, ddsk_ref, dh_scr))
            one(xv, bv, cv, dtv, biasv, alogv, dskv, tri_ref, triu_ref, hsv, dyv,
                dxv, dbv, dcv, ddtv, dbiasv, dalogv, ddskv, dhv)

    def one(x_ref, b_ref, c_ref, dtr_ref, bias_ref, alog_ref, dsk_ref, tri_ref, triu_ref, hs_ref, dy_ref,
            dx_ref, db_ref, dc_ref, ddtr_ref, dbias_ref, dalog_ref, ddsk_ref, dh_scr):
        @pl.when(pl.program_id(1) == 0)
        def _():
            dh_scr[...] = jnp.zeros_like(dh_scr)
            dbias_ref[...] = jnp.zeros_like(dbias_ref)
            dalog_ref[...] = jnp.zeros_like(dalog_ref)
            ddsk_ref[...] = jnp.zeros_like(ddsk_ref)

        x, Bm, Cm, pre, dt, A, ac4, ar4, cb, causal, atot4 = _ssd_common(
            x_ref, b_ref, c_ref, dtr_ref, bias_ref, alog_ref, tri_ref, triu_ref)
        dyv, hin, dho = dy_ref[...], hs_ref[...], dh_scr[...]
        dt_w = _per_head(dt, W)
        xdt = x * dt_w
        E4, F4, etot4 = jnp.exp(ac4), jnp.exp(atot4 - ac4), jnp.exp(atot4)
        F_w = _per_head(F4, W)
        ddsk_ref[...] += _head_sums(_colsum(dyv * x))
        Z = _dot(Cm, hin, NT)
        dZ = dyv * _per_head(E4, W)
        dac4 = _head_sums(dyv * Z) * E4
        dC = _dot(dZ, hin, NN)
        dh_scr[...] = _dot(dZ, Cm, TN) + _head_rows(etot4) * dho
        per_row = jnp.sum(dho * hin, axis=1, keepdims=True)
        lane4 = lax.broadcasted_iota(jnp.int32, (1, SSD_HPG), 1)
        datot4 = jnp.zeros((1, SSD_HPG), F32)
        for k in range(SSD_HPG):
            datot4 = jnp.where(lane4 == k, jnp.sum(per_row[P * k:P * (k + 1)], keepdims=True), datot4)
        datot4 = datot4 * etot4
        dxw = _dot(Bm, dho, NT)
        dB = _dot(xdt * F_w, dho, NN)
        dFa4 = _head_sums(dxw * xdt) * F4
        datot4 = datot4 + _colsum(dFa4)
        dac4 = dac4 - dFa4
        dcb = jnp.zeros((Q, Q), F32)
        lane_q4 = lax.broadcasted_iota(jnp.int32, (Q, SSD_HPG), 1)
        sub_4q = lax.broadcasted_iota(jnp.int32, (SSD_HPG, Q), 0)
        row_sums = jnp.zeros((Q, SSD_HPG), F32)
        col_sums = jnp.zeros((SSD_HPG, Q), F32)
        dxdt_heads = []
        for k in range(SSD_HPG):
            blk = slice(P * k, P * (k + 1))
            L = _ssd_decay(k, ac4, ar4, causal)
            G = cb * L
            dG = _dot(dyv[:, blk], xdt[:, blk], NT)
            dxdt_heads.append(_dot(G, dyv[:, blk], TN))
            dcb = dcb + dG * L
            Mseg = dG * G
            row_sums = jnp.where(lane_q4 == k, jnp.sum(Mseg, axis=1, keepdims=True), row_sums)
            col_sums = jnp.where(sub_4q == k, jnp.sum(Mseg, axis=0, keepdims=True), col_sums)
        dxdt = dxw * F_w + jnp.concatenate(dxdt_heads, axis=1)
        last = lax.broadcasted_iota(jnp.int32, (Q, SSD_HPG), 0) == Q - 1
        dac4 = dac4 + row_sums + jnp.where(last, datot4, 0.0)
        dadt4 = _dot(triu_ref[...], dac4, NN, HI) - _dot(triu_ref[...], col_sums, NT, HI)
        ddt4 = _head_sums(dxdt * x) + dadt4 * A
        dalog_ref[...] += _colsum(dadt4 * dt) * A
        ddtr4 = ddt4 * jax.nn.sigmoid(pre)
        dbias_ref[...] += _colsum(ddtr4)
        ddtr_ref[...] = ddtr4
        dx_ref[...] = (_per_head(dsk_ref[...], W) * dyv + dxdt * dt_w).astype(dx_ref.dtype)
        db_ref[...] = dB + _dot(dcb, Cm, TN)
        dc_ref[...] = dC + _dot(dcb, Bm, NN)

    tri, triu = _tri(Q)
    return pl.pallas_call(
        body, name="ssd_bwd", grid=(SSD_GROUPS // SSD_STEP_GROUPS, NC),
        in_specs=[x_spec, b_spec, c_spec, dt_spec, p_spec, p_spec, p_spec, tri_spec, tri_spec, h_spec, dy_spec],
        out_specs=[dy_spec, dbc_spec, dbc_spec, dt_spec, p_spec, p_spec, p_spec],
        out_shape=[jax.ShapeDtypeStruct((T, D_INNER), F32),
                   jax.ShapeDtypeStruct((T, SSD_GROUPS * SSD_STATE), F32),
                   jax.ShapeDtypeStruct((T, SSD_GROUPS * SSD_STATE), F32),
                   jax.ShapeDtypeStruct(dtr8.shape, F32)] + [jax.ShapeDtypeStruct(bias8.shape, F32)] * 3,
        scratch_shapes=[pltpu.VMEM((SSD_STEP_GROUPS, W, SSD_STATE), F32)],
        compiler_params=_params(("parallel", "arbitrary")),
    )(xc, xc, xc, dtr8, bias8, alog8, dsk8, tri, triu, hs, dy)


def _to_groups(v):
    return v.reshape(SSD_GROUPS, 1, SSD_HPG)


def _mixer_fwd(x, p, w):
    T, D = x.shape
    (hn,) = _rowwise(lambda xt, g: (_rms(xt, g),), [x], [p["mix_norm"]], [(D, BF16)], name="mix_norm")
    winT = w["w_in"]
    u, z, xbc, dtr, gl = [
        _matmul(hn, winT[IN_OFFS[i]:IN_OFFS[i + 1]], "nt", name=f"mix_in{i}") for i in range(5)]
    tiles = _s5_derive(*[p[k][0] for k in ("s5_A_re", "s5_A_im", "s5_log_dt", "s5_B_re", "s5_B_im", "s5_C_re", "s5_C_im")])
    u = _to_seg(u)
    ys = _s5_fwd(u, tiles)
    (g,) = _rowwise(lambda yt, ut, d: (jax.nn.gelu(yt + d * ut),), [ys, u], [p["s5_D"]], [(D, F32)], name="s5_gelu")
    y5 = _matmul(g, w["s5_w_glu"], "nn", name="s5_glu", out=(BF16,),
                 epilogue=lambda acc, gt, b: (gt * jax.nn.sigmoid(acc + b),), extras=[g, p["s5_b_glu"]])
    xc = _conv_fwd(xbc, w["conv_w"], p["conv_b"])
    dtr8 = dtr.reshape(T, SSD_GROUPS, SSD_HPG).transpose(1, 0, 2)
    ssd_p = [_to_groups(p[k]) for k in ("ssd_dt_bias", "ssd_A_log", "ssd_D")]
    yssd_raw, hs = _ssd_fwd(xc, dtr8, *ssd_p)
    (yssd,) = _rowwise(lambda yt, zt, nw: (_rms(yt * jax.nn.silu(zt), nw),), [yssd_raw, z], [p["ssd_norm"]],
                       [(D_INNER, BF16)], name="ssd_gate")
    p5 = _from_seg(_matmul(y5, w["w_proj_s5"], "nn", name="mix_p5"))
    pssd = _matmul(yssd, w["w_proj_ssd"], "nn", name="mix_pssd")

    def merge(glt, at, bt, bg):
        gates = jax.nn.sigmoid(glt + bg)
        return (gates[:, :D] * at + gates[:, D:] * bt,)

    (merged,) = _rowwise(merge, [gl, p5, pssd], [p["b_gate"]], [(D, BF16)], name="mix_merge")
    y = _matmul(merged, w["w_out"], "nn", name="mix_out", epilogue=lambda acc, xt: (xt + acc,), extras=[x])
    res = dict(hn=hn, u=u, z=z, xbc=xbc, gl=gl, tiles=tiles, ys=ys, g=g, y5=y5, xc=xc, dtr8=dtr8, ssd_p=ssd_p,
               yssd_raw=yssd_raw, hs=hs, yssd=yssd, p5=p5, pssd=pssd, merged=merged)
    return y, res


def _mixer_bwd(dy, dy16, x, p, w, r):
    T, D = x.shape
    gp = {}
    dmerged = _matmul(dy16, w["w_out"], "nt", name="mix_d_merged")
    gp["w_out"] = _matmul(r["merged"], dy16, "tn", name="mix_d_wout")

    def merge_bwd(glt, at, bt, dm, bg):
        def f(q, a_, b_):
            gates = jax.nn.sigmoid(q + bg)
            return gates[:, :D] * a_ + gates[:, D:] * b_
        _, vjp = jax.vjp(f, glt, at, bt)
        dq, da_, db_ = vjp(dm)
        return da_, db_, dq, _colsum(dq)

    dp5, dpssd, dgl, gp["b_gate"] = _rowwise(
        merge_bwd, [r["gl"], r["p5"], r["pssd"], dmerged], [p["b_gate"]],
        [(D, BF16), (D, BF16), (2 * D, BF16)], [((1, 2 * D), F32)], name="mix_d_merge")
    dyssd = _matmul(dpssd, w["w_proj_ssd"], "nt", name="mix_d_yssd")
    gp["w_proj_ssd"] = _matmul(r["yssd"], dpssd, "tn", name="mix_d_wpssd")

    def gate_bwd(yt, zt, dyt, nw):
        _, vjp = jax.vjp(lambda a_, b_, c_: _rms(a_ * jax.nn.silu(b_), c_), yt, zt, nw)
        return vjp(dyt)

    dyraw, dz, gp["ssd_norm"] = _rowwise(
        gate_bwd, [r["yssd_raw"], r["z"], dyssd], [p["ssd_norm"]],
        [(D_INNER, F32), (D_INNER, BF16)], [((1, D_INNER), F32)], name="ssd_d_gate")
    dxs, dBm, dCm, ddtr8, dbias8, dalog8, ddsk8 = _ssd_bwd(r["xc"], r["dtr8"], *r["ssd_p"], r["hs"], dyraw)
    gp["ssd_dt_bias"], gp["ssd_A_log"], gp["ssd_D"] = [v.reshape(1, SSD_HEADS) for v in (dbias8, dalog8, ddsk8)]
    ddtr = ddtr8.transpose(1, 0, 2).reshape(T, SSD_HEADS)
    conv = [_conv_bwd(r["xbc"], w["conv_w"], p["conv_b"], d, c0, f"conv_bwd{i}")
            for i, (d, c0) in enumerate(((dxs, 0), (dBm, D_INNER), (dCm, D_INNER + SSD_GROUPS * SSD_STATE)))]
    dxbc = [c[0] for c in conv]
    gp["conv_w"] = jnp.concatenate([c[1] for c in conv], axis=1)
    gp["conv_b"] = jnp.concatenate([c[2] for c in conv], axis=1)

    dp5 = _to_seg(dp5)
    dy5 = _matmul(dp5, w["w_proj_s5"], "nt", name="mix_d_y5")
    gp["w_proj_s5"] = _matmul(r["y5"], dp5, "tn", name="mix_d_wp5")
    g = r["g"]

    def glu_ep(acc, gt, b, dyt):
        _, vjp = jax.vjp(lambda g_, t_: g_ * jax.nn.sigmoid(t_ + b), gt, acc)
        return vjp(dyt)

    dg1, dt_ = _matmul(g, w["s5_w_glu"], "nn", name="s5_d_glu", out=(F32, BF16), epilogue=glu_ep,
                       extras=[g, p["s5_b_glu"], dy5])
    gp["s5_w_glu"] = _matmul(g, dt_, "tn", name="s5_d_wglu")
    dg = _matmul(dt_, w["s5_w_glu"], "nt", name="s5_d_g", epilogue=lambda acc, e: (acc + e,), extras=[dg1])

    def gelu_bwd(yt, ut, dgt, dtt, d):
        _, vjp = jax.vjp(lambda y_, d_: jax.nn.gelu(y_ + d_ * ut), yt, d)
        dys, dd = vjp(dgt)
        return dys, dys * d, dd, _colsum(dtt.astype(F32))

    dys, dusk, gp["s5_D"], gp["s5_b_glu"] = _rowwise(
        gelu_bwd, [r["ys"], r["u"], dg, dt_], [p["s5_D"]], [(D, F32), (D, F32)], [((1, D), F32), ((1, D), F32)],
        name="s5_d_gelu")
    du, *dtiles = _s5_bwd(r["u"], dys, dusk, r["tiles"])
    du = _from_seg(du)

    pieces = [du, dz, None, ddtr, dgl]
    winT = w["w_in"]
    hn = r["hn"]
    d_rows, dhn = [], None
    cols = [(du, 0, 1024), (dz, 1024, 3072), (dxbc[0], 3072, 5120), (dxbc[1], 5120, 6144), (dxbc[2], 6144, 7168),
            (ddtr, 7168, 7200), (dgl, 7200, 9248)]
    for i, (d, lo, hi) in enumerate(cols):
        d_rows.append(_matmul(d, hn, "tn", name=f"mix_d_win{i}"))
        if dhn is None:
            dhn = _matmul(d, winT[lo:hi], "nn", name=f"mix_d_hn{i}")
        else:
            dhn = _matmul(d, winT[lo:hi], "nn", name=f"mix_d_hn{i}", epilogue=lambda acc, e: (acc + e,), extras=[dhn])
    gp["w_in"] = jnp.concatenate(d_rows, axis=0)
    del pieces

    def norm_bwd(xt, dh, dyt, gn):
        _, vjp = jax.vjp(_rms, xt, gn)
        dx, dgn = vjp(dh)
        return dyt + dx, dyt + dx, dgn

    dx, dx16, gp["mix_norm"] = _rowwise(norm_bwd, [x, dhn, dy], [p["mix_norm"]], [(D, F32), (D, BF16)],
                                        [((1, D), F32)], name="mix_d_norm")
    return dx, dx16, gp, dtiles


def _s5_param_grads(p, dtiles):
    keys = ("s5_A_re", "s5_A_im", "s5_log_dt", "s5_B_re", "s5_B_im", "s5_C_re", "s5_C_im")
    _, vjp = jax.vjp(_s5_derive, *[p[k][0] for k in keys])
    return {k: v[None] for k, v in zip(keys, vjp(tuple(dtiles)))}


def _local_step(x, target, p, w, late_weights, exchange):
    T, D = x.shape
    x1, r1 = _ffn_fwd(x, p["ffn1_norm"], w["ffn1_w_gate"], w["ffn1_w_up"], w["ffn1_w_down"], "ffn1")
    x2, rm = _mixer_fwd(x1, p, w)
    w = {**w, **late_weights(rm["p5"])}
    x3, r2 = _ffn_fwd(x2, p["ffn2_norm"], w["ffn2_w_gate"], w["ffn2_w_up"], w["ffn2_w_down"], "ffn2")

    def head(xt, tt, g):
        def f(x_, g_):
            e = _rms(x_, g_) - tt
            return 0.5 * jnp.sum(jnp.mean(e * e, axis=-1))
        l, (dx_, dg_) = jax.value_and_grad(f, argnums=(0, 1))(xt, g)
        return dx_, dx_, l.reshape(1, 1), dg_

    dx3, dx3_16, loss, d_final = _rowwise(head, [x3, target], [p["final_norm"]], [(D, F32), (D, BF16)],
                                          [((1, 1), F32), ((1, D), F32)], name="loss_head")
    gp = {"final_norm": d_final}

    def ffn_exchange(tag):
        return lambda d_wgT, d_wuT, d_wd, after: exchange(
            tag, {f"{tag}_w_gate": d_wgT, f"{tag}_w_up": d_wuT, f"{tag}_w_down": d_wd}, after)

    dx2, dx2_16, gp["ffn2_norm"] = _ffn_bwd(
        dx3, dx3_16, x2, p["ffn2_norm"], w["ffn2_w_gate"], w["ffn2_w_up"], w["ffn2_w_down"], r2, "ffn2", (),
        ffn_exchange("ffn2"))
    dx1, dx1_16, gm, dtiles = _mixer_bwd(dx2, dx2_16, x1, p, w, rm)
    conv_w_grad = gm.pop("conv_w")
    started = exchange("mixer", {k: gm.pop(k) for k in ("w_out", "w_proj_s5", "w_proj_ssd", "s5_w_glu", "w_in")}, dx1)
    gp.update(gm)
    gp.update(_s5_param_grads(p, dtiles))
    dx0, _, gp["ffn1_norm"] = _ffn_bwd(
        dx1, dx1_16, x, p["ffn1_norm"], w["ffn1_w_gate"], w["ffn1_w_up"], w["ffn1_w_down"], r1, "ffn1", started,
        ffn_exchange("ffn1"))
    return loss, dx0, gp, conv_w_grad


ADAMW_BLOCK = 1 << 18


def _adamw(w, m, v, gs, *, name, emit_g):
    _, R, C = w.shape
    n_g = len(gs)
    n_o = 4 if emit_g else 3
    tm = _rtile(R, max(8, ADAMW_BLOCK // C))
    lead = pl.BlockSpec((None, tm, C), lambda i: (0, i, 0))

    def body(w_ref, m_ref, v_ref, *refs):
        g = refs[0][...]
        for r in refs[1:n_g]:
            g = g + r[...]
        m2 = ADAM_B1 * m_ref[...] + (1.0 - ADAM_B1) * g
        v2 = ADAM_B2 * v_ref[...] + (1.0 - ADAM_B2) * (g * g)
        m_hat = m2 / (1.0 - ADAM_B1 ** ADAM_STEP)
        v_hat = v2 / (1.0 - ADAM_B2 ** ADAM_STEP)
        delta = -ADAM_LR * (m_hat / (jnp.sqrt(v_hat) + ADAM_EPS) + ADAM_WD * w_ref[...])
        for r, val in zip(refs[n_g:], ((g,) if emit_g else ()) + (delta, m2, v2)):
            r[...] = val

    return pl.pallas_call(
        body, name=name, grid=(R // tm,),
        in_specs=[lead] * 3 + [pl.BlockSpec((tm, C), lambda i: (i, 0))] * n_g, out_specs=[lead] * n_o,
        out_shape=[jax.ShapeDtypeStruct((1, R, C), F32)] * n_o,
        compiler_params=_params(("parallel",)),
    )(w, m, v, *gs)


def _rtile(r, cap=512):
    if r <= cap:
        return r
    best = None
    for t in range(8, cap + 1, 8):
        if r % t == 0:
            best = t
    assert best is not None, r
    return best


ANY = pl.BlockSpec(memory_space=pl.ANY)


def _where_am_i():
    x, y, c = lax.axis_index("x"), lax.axis_index("y"), lax.axis_index("c")
    ks = (2 * x + y, 2 * (1 - x) + y, 2 * x + (1 - y), 2 * (1 - x) + (1 - y))
    return x, y, c, ks


def _rc(src, dst, ssem, rsem, to):
    return pltpu.make_async_remote_copy(src_ref=src, dst_ref=dst, send_sem=ssem, recv_sem=rsem,
                                        device_id=to, device_id_type=MESH)


def _gather_shards(srcs):
    n = len(srcs)

    def body(*refs):
        _gather_body(refs[:n], refs[n:2 * n], *refs[2 * n:], extra=())

    return pl.pallas_call(
        body, name="gather_shards",
        in_specs=[ANY] * n, out_specs=[ANY] * n,
        out_shape=[jax.ShapeDtypeStruct((4,) + s.shape, s.dtype) for s in srcs],
        scratch_shapes=[pltpu.SemaphoreType.DMA((n, 6)), pltpu.SemaphoreType.DMA((n, 6)), pltpu.SemaphoreType.DMA((n,))],
        compiler_params=pltpu.CompilerParams(has_side_effects=True),
    )(*srcs)


def _handshake_all():
    x, y, c = lax.axis_index("x"), lax.axis_index("y"), lax.axis_index("c")
    barrier = pltpu.get_barrier_semaphore()
    for dx, dy, dc in itertools.product((0, 1), repeat=3):
        if (dx, dy, dc) != (0, 0, 0):
            to = (1 - x if dx else x, 1 - y if dy else y, 1 - c if dc else c)
            pl.semaphore_signal(barrier, inc=1, device_id=to, device_id_type=MESH)
    pl.semaphore_wait(barrier, 7)


SEQUENCER = dict(axis_name="seq", num_cores=1)


def _gather_shards_async(srcs, token, name, collective_id):
    n = len(srcs)

    def body(*refs):
        tok, tok_out = refs[n], refs[2 * n + 1]
        ssem, rsem, lsem = refs[2 * n + 2:]
        _handshake_all()
        _gather_body(refs[:n], refs[n + 1:2 * n + 1], ssem, rsem, lsem,
                     extra=(pltpu.make_async_copy(tok, tok_out, lsem.at[n]),))

    res = pl.kernel(
        body, name=name,
        out_type=[jax.ShapeDtypeStruct((4,) + s.shape, s.dtype) for s in srcs]
        + [jax.ShapeDtypeStruct(token.shape, token.dtype)],
        mesh=plsc.ScalarSubcoreMesh(**SEQUENCER),
        scratch_types=[pltpu.SemaphoreType.DMA((n, 6)), pltpu.SemaphoreType.DMA((n, 6)), pltpu.SemaphoreType.DMA((n + 1,))],
        compiler_params=pltpu.CompilerParams(collective_id=collective_id),
    )(*srcs, token)
    return res[:n], res[n]


def _gather_body(src, out, ssem, rsem, lsem, extra):
        n = len(src)
        x, y, c, (k_me, k_x, k_y, k_d) = _where_am_i()
        sib = (x, y, 1 - c)
        local = [pltpu.make_async_copy(src[i], out[i].at[k_me], lsem.at[i]) for i in range(n)] + list(extra)
        for cp in local:
            cp.start()

        def own(i, q, to):
            return _rc(src[i], out[i].at[k_me], ssem.at[i, q], rsem.at[i, q], to)

        def slab(i, q, k, to):
            return _rc(out[i].at[k], out[i].at[k], ssem.at[i, q], rsem.at[i, q], to)

        @pl.when(c == 1)
        def _():
            sends = [own(i, 0, (1 - x, y, 1)) for i in range(n)]
            for cp in sends:
                cp.start()
            fwd = [slab(i, 3, k_x, sib) for i in range(n)]
            for i in range(n):
                slab(i, 0, k_x, sib).wait_recv()
                fwd[i].start()
            for i in range(n):
                slab(i, 4, k_y, sib).wait_recv()
                slab(i, 5, k_d, sib).wait_recv()
            for cp in sends + fwd:
                cp.wait_send()

        @pl.when(c == 0)
        def _():
            sends = [own(i, 1, (x, 1 - y, 0)) for i in range(n)] + [own(i, 2, (1 - x, 1 - y, 0)) for i in range(n)]
            for cp in sends:
                cp.start()
            fwd = [slab(i, 4, k_y, sib) for i in range(n)] + [slab(i, 5, k_d, sib) for i in range(n)]
            for i in range(n):
                slab(i, 1, k_y, sib).wait_recv()
                fwd[i].start()
            for i in range(n):
                slab(i, 2, k_d, sib).wait_recv()
                fwd[n + i].start()
            for i in range(n):
                slab(i, 3, k_x, sib).wait_recv()
            for cp in sends + fwd:
                cp.wait_send()

        for cp in local:
            cp.wait()


def _swap_slabs(arrs, n_slabs, name, async_id, token):
    n = len(arrs)
    J = max(n_slabs, 1)

    def body(*refs):
        src, out = refs[:n], refs[n:2 * n]
        ssem, rsem = refs[2 * n:]
        x, y, c, (k_me, k_x, k_y, k_d) = _where_am_i()
        sib = (x, y, 1 - c)
        sel = (jnp.where(c == 1, k_y, k_me), jnp.where(c == 1, k_d, k_x))
        cps = []
        for i in range(n):
            for j in range(J):
                s = src[i].at[sel[j]] if n_slabs else src[i]
                d = out[i].at[j] if n_slabs else out[i]
                cps.append(_rc(s, d, ssem.at[i, j], rsem.at[i, j], sib))
        for cp in cps:
            cp.start()
        for cp in cps:
            cp.wait()

    return _exchange_call(
        body, arrs, [jax.ShapeDtypeStruct(((n_slabs,) if n_slabs else ()) + a.shape[-2:], a.dtype) for a in arrs],
        [pltpu.SemaphoreType.DMA((n, J)), pltpu.SemaphoreType.DMA((n, J))], name, async_id, token)


def _exchange_call(body, arrs, out_shape, sems, name, collective_id, token):
    n, m = len(arrs), len(out_shape)

    def seq_body(*refs):
        tok, tok_out, tok_sem = refs[n], refs[n + 1 + m], refs[-1]
        _handshake_all()
        cp = pltpu.make_async_copy(tok, tok_out, tok_sem)
        cp.start()
        body(*refs[:n], *refs[n + 1:n + 1 + m], *refs[n + 2 + m:-1])
        cp.wait()

    res = pl.kernel(
        seq_body, name=name, out_type=list(out_shape) + [jax.ShapeDtypeStruct(token.shape, token.dtype)],
        mesh=plsc.ScalarSubcoreMesh(**SEQUENCER), scratch_types=list(sems) + [pltpu.SemaphoreType.DMA],
        compiler_params=pltpu.CompilerParams(collective_id=collective_id))(*arrs, token)
    return res[:m], res[m]


def _send_chip_sums(arrs, name, async_id, token):
    n = len(arrs)

    def body(*refs):
        src, out = refs[:n], refs[n:2 * n]
        ssem, rsem = refs[2 * n:]
        x, y, c, _ = _where_am_i()

        @pl.when(c == 1)
        def _():
            cps = [_rc(src[i].at[1], out[i].at[0], ssem.at[i, 0], rsem.at[i, 0], (1 - x, y, 1)) for i in range(n)]
            for cp in cps:
                cp.start()
            for cp in cps:
                cp.wait()

        @pl.when(c == 0)
        def _():
            cps = [_rc(src[i].at[0], out[i].at[0], ssem.at[i, 0], rsem.at[i, 0], (x, 1 - y, 0)) for i in range(n)]
            cps += [_rc(src[i].at[1], out[i].at[1], ssem.at[i, 1], rsem.at[i, 1], (1 - x, 1 - y, 0)) for i in range(n)]
            for cp in cps:
                cp.start()
            for cp in cps:
                cp.wait()

    return _exchange_call(
        body, arrs, [jax.ShapeDtypeStruct(a.shape, a.dtype) for a in arrs],
        [pltpu.SemaphoreType.DMA((n, 2)), pltpu.SemaphoreType.DMA((n, 2))], name, async_id, token)


def _chip_sum(g, recv, sel, name):
    _, r, C = g.shape
    tr = _rtile(r)

    def body(sel_ref, g_ref, r_ref, o32_ref, o16_ref):
        s = g_ref[...] + r_ref[...]
        o32_ref[...] = s
        o16_ref[...] = s.astype(BF16)

    blk = pl.BlockSpec((None, tr, C), lambda j, t, sel_ref: (j, t, 0))
    return pl.pallas_call(
        body, name=name,
        grid_spec=pltpu.PrefetchScalarGridSpec(
            num_scalar_prefetch=1, grid=(2, r // tr),
            in_specs=[pl.BlockSpec((None, tr, C), lambda j, t, sel_ref: (sel_ref[j], t, 0)), blk],
            out_specs=[blk, blk]),
        out_shape=[jax.ShapeDtypeStruct((2, r, C), F32), jax.ShapeDtypeStruct((2, r, C), BF16)],
        compiler_params=_params(("arbitrary", "arbitrary")),
    )(sel, g, recv)


def _cross_sum(s32, got, north, after, name):
    _, r, C = s32.shape
    tr = _rtile(r)

    def body(f_ref, p_ref, a_ref, b_ref, after_ref, o_ref):
        a = a_ref[...].astype(F32)

        @pl.when(f_ref[0] == 1)
        def _():
            o_ref[...] = p_ref[...] + a

        @pl.when(f_ref[0] == 0)
        def _():
            o_ref[...] = a + b_ref[...].astype(F32)

    return pl.pallas_call(
        body, name=name,
        grid_spec=pltpu.PrefetchScalarGridSpec(
            num_scalar_prefetch=1, grid=(r // tr,),
            in_specs=[pl.BlockSpec((None, tr, C), lambda t, f: (0, t, 0)),
                      pl.BlockSpec((None, tr, C), lambda t, f: (0, t, 0)),
                      pl.BlockSpec((None, tr, C), lambda t, f: (1 - f[0], t, 0)), ANY],
            out_specs=pl.BlockSpec((tr, C), lambda t, f: (t, 0))),
        out_shape=jax.ShapeDtypeStruct((r, C), F32),
        compiler_params=_params(("arbitrary",)),
    )(north, s32, got, got, after)


class _Reducer:
    def __init__(self, token):
        self.token, self.pending, self.pairs = token, None, {}

    def begin(self, tag, ids, grads, after):
        names = list(grads)
        arrs = [grads[n].reshape(4, -1, D_MODEL) for n in names]
        x, y, c, (k_me, k_x, k_y, k_d) = _where_am_i()
        sel = jnp.stack([jnp.where(c == 1, k_me, k_y), jnp.where(c == 1, k_x, k_d)]).astype(jnp.int32)
        from_sib, self.token = _swap_slabs(arrs, 2, f"swap_grad_slabs_{tag}", ids[0], self.token)
        sums = [_chip_sum(g, r, sel, f"chip_sum_{tag}{i}") for i, (g, r) in enumerate(zip(arrs, from_sib))]
        self.finish(after)
        got, self.token = _send_chip_sums([s16 for _, s16 in sums], f"send_chip_sums_{tag}", ids[1], self.token)
        self.pending = (tag, ids[2], names, sums, got)
        return tuple(s16 for _, s16 in sums)

    def finish(self, after):
        if self.pending is None:
            return
        tag, cid, names, sums, got = self.pending
        north = jnp.reshape(lax.axis_index("c"), (1,)).astype(jnp.int32)
        parts = [_cross_sum(s32, b, north, after, f"cross_sum_{tag}{i}") for i, ((s32, _), b) in enumerate(zip(sums, got))]
        theirs, self.token = _swap_slabs(parts, 0, f"swap_parts_{tag}", cid, self.token)
        self.pairs.update(zip(names, zip(parts, theirs)))
        self.pending = None


def _all_reduce_small(v, token, collective_id):
    R, C = v.shape
    tr = R

    def body(v_ref, buf, ssem, rsem, lsem):
        x, y, c = lax.axis_index("x"), lax.axis_index("y"), lax.axis_index("c")
        me, sib = (x, y, c), (x, y, 1 - c)
        chips = [(1 - x, y), (x, 1 - y), (1 - x, 1 - y)]

        def slot(px, py, pc):
            return buf.at[4 * px + 2 * py + pc]

        def copy(k, block, to, src=None):
            return _rc(slot(*block) if src is None else src, slot(*block), ssem.at[k], rsem.at[k], to)

        mine = pltpu.make_async_copy(v_ref, slot(*me), lsem)
        mine.start()
        first = [copy(0, me, sib, src=v_ref)] + [copy(1 + j, me, (*chip, c), src=v_ref) for j, chip in enumerate(chips)]
        for cp in first:
            cp.start()
        passed = [copy(4 + j, (*chip, c), sib) for j, chip in enumerate(chips)]
        for j, chip in enumerate(chips):
            copy(1 + j, (*chip, c), me).wait_recv()
            passed[j].start()
        copy(0, sib, me).wait_recv()
        for j, chip in enumerate(chips):
            copy(4 + j, (*chip, 1 - c), me).wait_recv()
        for cp in first + passed:
            cp.wait_send()
        mine.wait()

    (gathered,), token = _exchange_call(
        body, [v], [jax.ShapeDtypeStruct((8, R, C), F32)],
        [pltpu.SemaphoreType.DMA((7,)), pltpu.SemaphoreType.DMA((7,)), pltpu.SemaphoreType.DMA],
        "gather_small", collective_id, token)

    def add(g_ref, o_ref):
        acc = g_ref[0]
        for k in range(1, 8):
            acc = acc + g_ref[k]
        o_ref[...] = acc

    total = pl.pallas_call(
        add, name="sum_small", grid=(R // tr,),
        in_specs=[pl.BlockSpec((8, tr, C), lambda i: (0, i, 0))], out_specs=pl.BlockSpec((tr, C), lambda i: (i, 0)),
        out_shape=jax.ShapeDtypeStruct((R, C), F32), compiler_params=_params(("parallel",)),
    )(gathered)
    return total, token


WEIGHTS = ['ffn1_norm', 'ffn1_w_gate', 'ffn1_w_up', 'ffn1_w_down', 'mix_norm', 'w_in', 'conv_w', 'conv_b', 's5_A_re',
           's5_A_im', 's5_log_dt', 's5_B_re', 's5_B_im', 's5_C_re', 's5_C_im', 's5_D', 's5_w_glu', 's5_b_glu',
           'ssd_A_log', 'ssd_dt_bias', 'ssd_D', 'ssd_norm', 'w_proj_s5', 'w_proj_ssd', 'b_gate', 'w_out', 'ffn2_norm',
           'ffn2_w_gate', 'ffn2_w_up', 'ffn2_w_down', 'final_norm']
ARGS = ['x'] + WEIGHTS + ['loss_target'] + ['m_' + n for n in WEIGHTS] + ['v_' + n for n in WEIGHTS]
COL_SHARDED = ('ffn1_w_gate', 'ffn1_w_up', 'w_in', 'ffn2_w_gate', 'ffn2_w_up')
ROW_SHARDED = ('ffn1_w_down', 's5_w_glu', 'w_proj_s5', 'w_proj_ssd', 'w_out', 'ffn2_w_down')
MATRICES = COL_SHARDED + ROW_SHARDED
GATHER_FIRST = ('ffn1_w_gate', 'ffn1_w_up')
GATHER_FIRST_LATER = ('ffn1_w_down',)
GATHER_SECOND = ('w_in', 'conv_w', 's5_w_glu', 'w_proj_s5', 'w_proj_ssd', 'w_out')
GATHER_THIRD = ('ffn2_w_gate', 'ffn2_w_up', 'ffn2_w_down')
SMALL = [n for n in WEIGHTS if n not in MATRICES]


def _pack(arrs, width=1024):
    flat = jnp.concatenate([a.reshape(-1).astype(F32) for a in arrs])
    rows = -(-flat.shape[0] // (8 * width)) * 8
    return jnp.pad(flat, (0, rows * width - flat.shape[0])).reshape(rows, width)


def _unpack(packed, shapes):
    flat, out, o = packed.reshape(-1), [], 0
    for s in shapes:
        n = math.prod(s)
        out.append(flat[o:o + n].reshape(s))
        o += n
    return out


def kernel(x, ffn1_norm, ffn1_w_gate, ffn1_w_up, ffn1_w_down, mix_norm, w_in, conv_w, conv_b, s5_A_re, s5_A_im, s5_log_dt, s5_B_re, s5_B_im, s5_C_re, s5_C_im, s5_D, s5_w_glu, s5_b_glu, ssd_A_log, ssd_dt_bias, ssd_D, ssd_norm, w_proj_s5, w_proj_ssd, b_gate, w_out, ffn2_norm, ffn2_w_gate, ffn2_w_up, ffn2_w_down, final_norm, loss_target, m_ffn1_norm, m_ffn1_w_gate, m_ffn1_w_up, m_ffn1_w_down, m_mix_norm, m_w_in, m_conv_w, m_conv_b, m_s5_A_re, m_s5_A_im, m_s5_log_dt, m_s5_B_re, m_s5_B_im, m_s5_C_re, m_s5_C_im, m_s5_D, m_s5_w_glu, m_s5_b_glu, m_ssd_A_log, m_ssd_dt_bias, m_ssd_D, m_ssd_norm, m_w_proj_s5, m_w_proj_ssd, m_b_gate, m_w_out, m_ffn2_norm, m_ffn2_w_gate, m_ffn2_w_up, m_ffn2_w_down, m_final_norm, v_ffn1_norm, v_ffn1_w_gate, v_ffn1_w_up, v_ffn1_w_down, v_mix_norm, v_w_in, v_conv_w, v_conv_b, v_s5_A_re, v_s5_A_im, v_s5_log_dt, v_s5_B_re, v_s5_B_im, v_s5_C_re, v_s5_C_im, v_s5_D, v_s5_w_glu, v_s5_b_glu, v_ssd_A_log, v_ssd_dt_bias, v_ssd_D, v_ssd_norm, v_w_proj_s5, v_w_proj_ssd, v_b_gate, v_w_out, v_ffn2_norm, v_ffn2_w_gate, v_ffn2_w_up, v_ffn2_w_down, v_final_norm):
    a = dict(locals())
    assert list(a) == ARGS
    x, target = a['x'][0], a['loss_target'][0]
    k_me = 2 * lax.axis_index("x") + lax.axis_index("y")

    src = {n: a[n][0].T.astype(BF16) for n in COL_SHARDED}
    src.update({n: a[n][0].astype(BF16) for n in ROW_SHARDED})
    src['conv_w'] = a['conv_w'][0]
    def flat(f):
        return f.reshape(-1, f.shape[-1])

    full = _gather_shards([src[n] for n in GATHER_FIRST])
    w = {n: flat(f) for n, f in zip(GATHER_FIRST, full)}
    token = full[0][0, :8, :128].astype(F32)
    for names, name, cid in ((GATHER_FIRST_LATER, "gather_ffn1_down", 14), (GATHER_SECOND[:2], "gather_mixer_in", 1),
                             (GATHER_SECOND[2:], "gather_mixer", 12)):
        got, token = _gather_shards_async([src[n] for n in names], token, name, cid)
        w.update({n: flat(f) for n, f in zip(names, got)})
    w['conv_w'] = w['conv_w'].reshape(4, CONV_K, -1).transpose(1, 0, 2).reshape(CONV_K, CONV_DIM)
    reducer = _Reducer(token)

    def gather_last(after):
        got, reducer.token = _gather_shards_async(
            [src[n] for n in GATHER_THIRD], reducer.token + after[:8, :128], "gather_ffn2", 2)
        return {n: flat(f) for n, f in zip(GATHER_THIRD, got)}

    p = {n: a[n] for n in SMALL if n != 'conv_w'}
    p['final_norm'] = a['final_norm'][None]
    ids = {"ffn2": (3, 4, 5), "mixer": (6, 7, 8), "ffn1": (9, 10, 11)}
    loss, grad_x, gp, conv_w_grad = _local_step(
        x, target, p, w, gather_last, lambda tag, grads, after: reducer.begin(tag, ids[tag], grads, after))
    gp['conv_w'] = conv_w_grad[None]
    small_shapes = [(1, 1)] + [gp[n].shape if n != 'final_norm' else (1, D_MODEL) for n in SMALL]
    summed, reducer.token = _all_reduce_small(_pack([loss] + [gp[n] for n in SMALL]), reducer.token, 13)
    red = _unpack(summed, small_shapes)
    reducer.finish(grad_x)
    pairs = [reducer.pairs[n] for n in MATRICES]
    loss_all = red[0].reshape(())
    gsmall = dict(zip(SMALL, red[1:]))
    gsmall['conv_w'] = lax.dynamic_slice_in_dim(gsmall['conv_w'], k_me * 1024, 1024, axis=2)
    gsmall = {n: g.reshape(a[n].shape) for n, g in gsmall.items()}

    grads, delta, new_m, new_v = {}, {}, {}, {}
    for n, (mine, theirs) in zip(MATRICES, pairs):
        turn = (lambda t: jnp.swapaxes(t, 1, 2)) if n in COL_SHARDED else (lambda t: t)
        out = _adamw(turn(a[n]), turn(a['m_' + n]), turn(a['v_' + n]), [mine, theirs], name=f"adamw_{n}", emit_g=True)
        grads[n], delta[n], new_m[n], new_v[n] = [turn(t) for t in out]
    sw, sm, sv, sg = [_pack([t[n] for n in SMALL])[None] for t in (a, {n: a['m_' + n] for n in SMALL},
                                                                   {n: a['v_' + n] for n in SMALL}, gsmall)]
    d, m2, v2 = _adamw(sw, sm, sv, [sg[0]], name="adamw_small", emit_g=False)
    shapes = [a[n].shape for n in SMALL]
    for n, dd, mm, vv in zip(SMALL, _unpack(d[0], shapes), _unpack(m2[0], shapes), _unpack(v2[0], shapes)):
        grads[n], delta[n], new_m[n], new_v[n] = gsmall[n], dd, mm, vv
    return (loss_all, grad_x[None], *[grads[n] for n in WEIGHTS], *[delta[n] for n in WEIGHTS],
            *[new_m[n] for n in WEIGHTS], *[new_v[n] for n in WEIGHTS])
```

```python
import itertools
import math

import jax
import jax.numpy as jnp
from jax import lax
from jax.experimental import pallas as pl
from jax.experimental.pallas import tpu as pltpu
from jax.experimental.pallas import tpu_sc as plsc

F32 = jnp.float32
BF16 = jnp.bfloat16
MXU = BF16
HI = lax.Precision.HIGHEST

D_MODEL = 1024
D_FF = 2816
EPS = 1e-6
S5_GROUPS, S5_GROUP, S5_STATE = 64, 16, 64
D_INNER = 2048
SSD_HEADDIM, SSD_HEADS, SSD_GROUPS, SSD_HPG, SSD_STATE = 64, 32, 8, 4, 128
CONV_K, CONV_DIM = 4, 4096
IN_SPLITS = (1024, 2048, 4096, 32, 2048)
IN_OFFS = (0, 1024, 3072, 7168, 7200, 9248)
SSD_Q = 256
SSD_STEP_GROUPS = 2
ADAM_LR, ADAM_B1, ADAM_B2, ADAM_EPS, ADAM_WD, ADAM_STEP = 0.001, 0.9, 0.999, 1e-08, 0.01, 10

VMEM_LIMIT = 56 * 1024 * 1024
MATMUL_VMEM = 32 * 1024 * 1024
MESH = pl.DeviceIdType.MESH

NN = ((1,), (0,))
NT = ((1,), (1,))
TN = ((0,), (0,))


def _dot(a, b, dims, precision=None):
    if precision is None:
        a, b = a.astype(MXU), b.astype(MXU)
    return lax.dot_general(a, b, (dims, ((), ())), precision=precision, preferred_element_type=F32)


def _tile(n, pref):
    if n <= pref:
        return n
    best = None
    for t in range(128, pref + 1, 128):
        if n % t == 0:
            best = t
    assert best is not None, (n, pref)
    return best


def _params(sem):
    return pltpu.CompilerParams(dimension_semantics=sem, vmem_limit_bytes=VMEM_LIMIT)


def _matmul(a, b, mode, *, name, out=((F32),), epilogue=None, extras=(), after=(), tm=512, tn=1536):
    if mode == "nn":
        (M, K), (_, N) = a.shape, b.shape
    elif mode == "nt":
        (M, K), (N, _) = a.shape, b.shape
    else:
        (K, M), (_, N) = a.shape, b.shape
    tm, tn = _tile(M, tm), _tile(N, tn)

    def vmem(tm_, tn_):
        per = K * tm_ * a.dtype.itemsize + K * tn_ * b.dtype.itemsize
        per += sum(tm_ * tn_ * jnp.dtype(dt).itemsize for dt in out)
        per += sum((1 if e.shape[0] == 1 else tm_) * tn_ * e.dtype.itemsize for e in extras)
        return 2 * per

    while vmem(tm, tn) > MATMUL_VMEM and (tn > 128 or tm > 128):
        if tn >= tm and tn > 128:
            tn = _tile(N, tn - 128)
        else:
            tm = _tile(M, tm - 128)
    bytes_a, bytes_b = a.size * a.dtype.itemsize, b.size * b.dtype.itemsize
    rows_outer = bytes_a + (M // tm) * bytes_b <= (N // tn) * bytes_a + bytes_b
    grid = (M // tm, N // tn) if rows_outer else (N // tn, M // tm)

    def at(f):
        return (lambda i, j: f(i, j)) if rows_outer else (lambda j, i: f(i, j))

    a_spec = pl.BlockSpec((K, tm), at(lambda i, j: (0, i))) if mode == "tn" else pl.BlockSpec((tm, K), at(lambda i, j: (i, 0)))
    b_spec = pl.BlockSpec((tn, K), at(lambda i, j: (j, 0))) if mode == "nt" else pl.BlockSpec((K, tn), at(lambda i, j: (0, j)))
    dims = {"nn": NN, "nt": NT, "tn": TN}[mode]
    e_specs = []
    for e in extras:
        if e.shape[0] == 1:
            e_specs.append(pl.BlockSpec((1, tn), at(lambda i, j: (0, j))))
        else:
            assert e.shape == (M, N), (e.shape, M, N)
            e_specs.append(pl.BlockSpec((tm, tn), at(lambda i, j: (i, j))))
    n_e, n_o, n_a = len(extras), len(out), len(after)

    def body(a_ref, b_ref, *refs):
        acc = _dot(a_ref[...], b_ref[...], dims)
        res = (acc,) if epilogue is None else epilogue(acc, *[r[...] for r in refs[:n_e]])
        for r, v in zip(refs[n_e + n_a:], res):
            r[...] = v.astype(r.dtype)

    res = pl.pallas_call(
        body, name=name, grid=grid,
        in_specs=[a_spec, b_spec] + e_specs + [ANY] * n_a,
        out_specs=[pl.BlockSpec((tm, tn), at(lambda i, j: (i, j))) for _ in range(n_o)],
        out_shape=[jax.ShapeDtypeStruct((M, N), dt) for dt in out],
        compiler_params=_params(("parallel", "parallel")),
    )(a, b, *extras, *after)
    return res[0] if n_o == 1 else res


def _rowwise(fn, rows, fulls, outs, reds=(), *, name, after=(), tm=256):
    T = rows[0].shape[0]
    tm = min(tm, T)
    n_r, n_f, n_o, n_d, n_a = len(rows), len(fulls), len(outs), len(reds), len(after)

    def body(*refs):
        ins = [r[...] for r in refs[:n_r + n_f]]
        o_refs = refs[n_r + n_f + n_a:n_r + n_f + n_a + n_o]
        d_refs = refs[n_r + n_f + n_a + n_o:]
        res = fn(*ins)
        for r, v in zip(o_refs, res[:n_o]):
            r[...] = v.astype(r.dtype)
        if n_d:
            @pl.when(pl.program_id(0) == 0)
            def _():
                for r in d_refs:
                    r[...] = jnp.zeros_like(r)
            for r, v in zip(d_refs, res[n_o:]):
                r[...] += v.astype(r.dtype)

    res = pl.pallas_call(
        body, name=name, grid=(T // tm,),
        in_specs=[pl.BlockSpec((tm, r.shape[1]), lambda i: (i, 0)) for r in rows]
        + [pl.BlockSpec(f.shape, lambda i, nd=f.ndim: (0,) * nd) for f in fulls] + [ANY] * n_a,
        out_specs=[pl.BlockSpec((tm, c), lambda i: (i, 0)) for c, _ in outs]
        + [pl.BlockSpec(s, lambda i, nd=len(s): (0,) * nd) for s, _ in reds],
        out_shape=[jax.ShapeDtypeStruct((T, c), dt) for c, dt in outs]
        + [jax.ShapeDtypeStruct(s, dt) for s, dt in reds],
        compiler_params=_params(("arbitrary",)),
    )(*rows, *fulls, *after)
    return res


def _rms(x, g):
    return x * lax.rsqrt(jnp.mean(x * x, axis=-1, keepdims=True) + EPS) * g


def _colsum(v):
    return jnp.sum(v, axis=0, keepdims=True)


def _softplus(x):
    return jnp.maximum(x, 0.0) + jnp.log1p(jnp.exp(-jnp.abs(x)))


def _ffn_fwd(x, norm, wgT, wuT, wd, tag):
    D = x.shape[1]
    (hn,) = _rowwise(lambda xt, g: (_rms(xt, g),), [x], [norm], [(D, BF16)], name=f"{tag}_norm")
    a = _matmul(hn, wgT, "nt", name=f"{tag}_gate", out=(BF16,))
    b, hid = _matmul(hn, wuT, "nt", name=f"{tag}_up", out=(BF16, BF16),
                     epilogue=lambda acc, at: (acc, jax.nn.silu(at.astype(F32)) * acc), extras=[a])
    wd, hid = lax.optimization_barrier((wd, hid))
    y = _matmul(hid, wd, "nn", name=f"{tag}_down", epilogue=lambda acc, xt: (xt + 0.5 * acc,), extras=[x])
    return y, (hn, a, b, hid)


def _ffn_bwd(dy, dy16, x, norm, wgT, wuT, wd, res, tag, after, on_grads):
    hn, a, b, hid = res
    D, F = x.shape[1], a.shape[1]

    def act_bwd(acc, at, bt):
        _, vjp = jax.vjp(lambda p, q: jax.nn.silu(p) * q, at.astype(F32), bt.astype(F32))
        return vjp(0.5 * acc)

    da, db = _matmul(dy16, wd, "nt", name=f"{tag}_d_hid", out=(BF16, BF16), epilogue=act_bwd, extras=[a, b])
    d_wd = _matmul(hid, dy16, "tn", name=f"{tag}_d_wd", epilogue=lambda acc: (0.5 * acc,))
    d_wgT = _matmul(da, hn, "tn", name=f"{tag}_d_wg", after=after)
    d_wuT = _matmul(db, hn, "tn", name=f"{tag}_d_wu")
    started = on_grads(d_wgT, d_wuT, d_wd, dy)
    dhn = _matmul(da, wgT, "nn", name=f"{tag}_d_hn1")
    dhn = _matmul(db, wuT, "nn", name=f"{tag}_d_hn2", epilogue=lambda acc, e: (acc + e,), extras=[dhn])

    def norm_bwd(xt, dh, dyt, g):
        _, vjp = jax.vjp(_rms, xt, g)
        dx, dg = vjp(dh)
        return dyt + dx, dyt + dx, dg

    dx, dx16, d_norm = _rowwise(norm_bwd, [x, dhn, dy], [norm], [(D, F32), (D, BF16)], [((1, D), F32)],
                                name=f"{tag}_d_norm", after=started)
    return dx, dx16, d_norm


S5_TILES = 8
S5_HALF = 256


def _s5_derive(A_re, A_im, log_dt, B_re, B_im, C_re, C_im):
    G, N, M = S5_GROUPS, S5_STATE, S5_GROUP
    dt = jnp.exp(log_dt)[:, None]
    mag = jnp.exp(A_re * dt)
    ar, ai = mag * jnp.cos(A_im * dt), mag * jnp.sin(A_im * dt)
    den = A_re * A_re + A_im * A_im
    cr = ((ar - 1.0) * A_re + ai * A_im) / den
    ci = (ai * A_re - (ar - 1.0) * A_im) / den
    bbr = cr[..., None] * B_re - ci[..., None] * B_im
    bbi = cr[..., None] * B_im + ci[..., None] * B_re
    eye = jnp.eye(8, dtype=F32)

    def tile_in(bb):
        t = bb.reshape(S5_TILES, 8, N, M).transpose(0, 1, 3, 2)
        return jnp.einsum("jamn,ab->jambn", t, eye).reshape(S5_TILES, 8 * M, 8 * N)

    def tile_out(c):
        t = c.reshape(S5_TILES, 8, M, N).transpose(0, 1, 3, 2)
        return jnp.einsum("janm,ab->janbm", t, eye).reshape(S5_TILES, 8 * N, 8 * M)

    return (tile_in(bbr), tile_in(bbi), tile_out(C_re), tile_out(C_im),
            ar.reshape(S5_TILES, 1, 8 * N), ai.reshape(S5_TILES, 1, 8 * N))


S5_NB = S5_HALF // 128
S5_SEG = 32


def _cmul(ar, ai, br, bi):
    return ar * br - ai * bi, ar * bi + ai * br


def _s5_scan(sr_ref, si_ref, ar, ai, T, reverse):
    L, V = T // S5_SEG, S5_SEG // 8
    assert L * S5_SEG == T and L & (L - 1) == 0, T
    sg = -1.0 if reverse else 1.0
    a_r = [jnp.broadcast_to(ar[:, 128 * b:128 * (b + 1)], (8, 128)) for b in range(S5_NB)]
    a_i = [jnp.broadcast_to(sg * ai[:, 128 * b:128 * (b + 1)], (8, 128)) for b in range(S5_NB)]

    def rows(k, v):
        return pl.ds(pl.multiple_of(((L - 1 - k) if reverse else k) * S5_SEG + 8 * v, 8), 8)

    def local(k, carry):
        out = []
        for b in range(S5_NB):
            for v in range(V):
                idx = rows(k, v)
                mr, mi = _cmul(a_r[b], a_i[b], *carry[b * V + v])
                nr, ni = mr + sr_ref[b, idx, :], mi + si_ref[b, idx, :]
                sr_ref[b, idx, :] = nr
                si_ref[b, idx, :] = ni
                out.append((nr, ni))
        return tuple(out)

    z = jnp.zeros((8, 128), F32)
    ends = lax.fori_loop(0, L, local, tuple((z, z) for _ in range(S5_NB * V)))

    carries = []
    for b in range(S5_NB):
        pr, pi = a_r[b][0:1], a_i[b][0:1]
        n = L
        while n > 1:
            pr, pi = _cmul(pr, pi, pr, pi)
            n //= 2
        cr, ci = jnp.zeros((1, 128), F32), jnp.zeros((1, 128), F32)
        into = [None] * S5_SEG
        for j in (reversed(range(S5_SEG)) if reverse else range(S5_SEG)):
            into[j] = (cr, ci)
            er, ei = ends[b * V + j // 8]
            mr, mi = _cmul(pr, pi, cr, ci)
            cr, ci = mr + er[j % 8:j % 8 + 1], mi + ei[j % 8:j % 8 + 1]
        carries.append([(jnp.concatenate([into[8 * v + s][0] for s in range(8)], axis=0),
                         jnp.concatenate([into[8 * v + s][1] for s in range(8)], axis=0)) for v in range(V)])

    def fix(k, powers):
        out = []
        for b in range(S5_NB):
            pr, pi = powers[b]
            for v in range(V):
                idx = rows(k, v)
                dr, di = _cmul(pr, pi, *carries[b][v])
                sr_ref[b, idx, :] += dr
                si_ref[b, idx, :] += di
            out.append(_cmul(pr, pi, a_r[b], a_i[b]))
        return tuple(out)

    lax.fori_loop(0, L, fix, tuple((a_r[b], a_i[b]) for b in range(S5_NB)))


def _s5_in(ut, b_ref, s_ref):
    for b in range(S5_NB):
        s_ref[b] = _dot(ut, b_ref[:, 128 * b:128 * (b + 1)], NN)


def _to_seg(v):
    T, C = v.shape
    return v.reshape(S5_SEG, T // S5_SEG, C).transpose(1, 0, 2).reshape(T, C)


def _from_seg(v):
    T, C = v.shape
    return v.reshape(T // S5_SEG, S5_SEG, C).transpose(1, 0, 2).reshape(T, C)


def _s5_specs(T):
    u_spec = pl.BlockSpec((T, 128), lambda j, h: (0, j))
    b_spec = pl.BlockSpec((None, 128, S5_HALF), lambda j, h: (j, 0, h))
    c_spec = pl.BlockSpec((None, S5_HALF, 128), lambda j, h: (j, h, 0))
    a_spec = pl.BlockSpec((None, 1, S5_HALF), lambda j, h: (j, 0, h))
    return u_spec, b_spec, c_spec, a_spec


def _s5_fwd(u, tiles):
    T = u.shape[0]
    u_spec, b_spec, c_spec, a_spec = _s5_specs(T)

    def body(u_ref, br_ref, bi_ref, cr_ref, ci_ref, ar_ref, ai_ref, y_ref, sr_ref, si_ref):
        ut = u_ref[...].astype(MXU)
        _s5_in(ut, br_ref, sr_ref)
        _s5_in(ut, bi_ref, si_ref)
        _s5_scan(sr_ref, si_ref, ar_ref[...], ai_ref[...], T, False)
        y = None
        for b in range(S5_NB):
            blk = slice(128 * b, 128 * (b + 1))
            yb = _dot(sr_ref[b], cr_ref[blk, :], NN) - _dot(si_ref[b], ci_ref[blk, :], NN)
            y = yb if y is None else y + yb

        @pl.when(pl.program_id(1) == 0)
        def _():
            y_ref[...] = y

        @pl.when(pl.program_id(1) == 1)
        def _():
            y_ref[...] += y

    scr = pltpu.VMEM((S5_NB, T, 128), F32)
    return pl.pallas_call(
        body, name="s5_fwd", grid=(S5_TILES, 2),
        in_specs=[u_spec, b_spec, b_spec, c_spec, c_spec, a_spec, a_spec],
        out_specs=u_spec, out_shape=jax.ShapeDtypeStruct(u.shape, F32),
        scratch_shapes=[scr, scr],
        compiler_params=_params(("parallel", "arbitrary")),
    )(u, *tiles)


def _s5_bwd(u, dys, du_skip, tiles):
    T = u.shape[0]
    u_spec, b_spec, c_spec, a_spec = _s5_specs(T)

    def body(u_ref, dy_ref, sk_ref, br_ref, bi_ref, cr_ref, ci_ref, ar_ref, ai_ref,
             du_ref, dbr_ref, dbi_ref, dcr_ref, dci_ref, dar_ref, dai_ref, sr_ref, si_ref, lr_ref, li_ref):
        ut, dy = u_ref[...].astype(MXU), dy_ref[...].astype(MXU)
        ar, ai = ar_ref[...], ai_ref[...]
        _s5_in(ut, br_ref, sr_ref)
        _s5_in(ut, bi_ref, si_ref)
        _s5_scan(sr_ref, si_ref, ar, ai, T, False)
        for b in range(S5_NB):
            blk = slice(128 * b, 128 * (b + 1))
            dcr_ref[blk, :] = _dot(sr_ref[b], dy, TN)
            dci_ref[blk, :] = -_dot(si_ref[b], dy, TN)
            lr_ref[b] = _dot(dy, cr_ref[blk, :], NT)
            li_ref[b] = -_dot(dy, ci_ref[blk, :], NT)
        _s5_scan(lr_ref, li_ref, ar, ai, T, True)
        du = None
        head = lax.broadcasted_iota(jnp.int32, (T, 128), 0) < S5_SEG
        seg0 = lax.broadcasted_iota(jnp.int32, (S5_SEG, 128), 0) == 0
        for b in range(S5_NB):
            blk = slice(128 * b, 128 * (b + 1))
            lr, li = lr_ref[b], li_ref[b]
            pr = jnp.where(head, 0.0, pltpu.roll(sr_ref[b], S5_SEG, 0))
            pi = jnp.where(head, 0.0, pltpu.roll(si_ref[b], S5_SEG, 0))
            er = jnp.where(seg0, 0.0, pltpu.roll(sr_ref[b, T - S5_SEG:T, :], 1, 0))
            ei = jnp.where(seg0, 0.0, pltpu.roll(si_ref[b, T - S5_SEG:T, :], 1, 0))
            hr, hi = lr[0:S5_SEG], li[0:S5_SEG]
            dar_ref[:, blk] = _colsum(lr * pr + li * pi) + _colsum(hr * er + hi * ei)
            dai_ref[:, blk] = _colsum(li * pr - lr * pi) + _colsum(hi * er - hr * ei)
            dbr_ref[:, blk] = _dot(ut, lr, TN)
            dbi_ref[:, blk] = _dot(ut, li, TN)
            dub = _dot(lr, br_ref[:, blk], NT) + _dot(li, bi_ref[:, blk], NT)
            du = dub if du is None else du + dub

        @pl.when(pl.program_id(1) == 0)
        def _():
            du_ref[...] = du + sk_ref[...]

        @pl.when(pl.program_id(1) == 1)
        def _():
            du_ref[...] += du

    scr = pltpu.VMEM((S5_NB, T, 128), F32)
    return pl.pallas_call(
        body, name="s5_bwd", grid=(S5_TILES, 2),
        in_specs=[u_spec, u_spec, u_spec, b_spec, b_spec, c_spec, c_spec, a_spec, a_spec],
        out_specs=[u_spec, b_spec, b_spec, c_spec, c_spec, a_spec, a_spec],
        out_shape=[jax.ShapeDtypeStruct(u.shape, F32)] + [jax.ShapeDtypeStruct(t.shape, F32) for t in tiles],
        scratch_shapes=[scr, scr, scr, scr],
        compiler_params=_params(("parallel", "arbitrary")),
    )(u, dys, du_skip, *tiles)


CONV_TILE = 256


def _conv_pre(x, w, b):
    T = x.shape[0]
    row = lax.broadcasted_iota(jnp.int32, x.shape, 0)
    acc = x * w[CONV_K - 1:CONV_K, :] + b
    for lag in range(1, CONV_K):
        acc = acc + jnp.where(row >= lag, pltpu.roll(x, lag, 0), 0.0) * w[CONV_K - 1 - lag:CONV_K - lag, :]
    return acc


def _conv_fwd(x, w, b):
    T, C = x.shape
    col = pl.BlockSpec((T, CONV_TILE), lambda j: (0, j))

    def body(x_ref, w_ref, b_ref, o_ref):
        o_ref[...] = jax.nn.silu(_conv_pre(x_ref[...], w_ref[...], b_ref[...]))

    return pl.pallas_call(
        body, name="conv_fwd", grid=(C // CONV_TILE,),
        in_specs=[col, pl.BlockSpec((CONV_K, CONV_TILE), lambda j: (0, j)), pl.BlockSpec((1, CONV_TILE), lambda j: (0, j))],
        out_specs=col, out_shape=jax.ShapeDtypeStruct((T, C), F32),
        compiler_params=_params(("parallel",)),
    )(x, w, b)


def _conv_bwd(x, w, b, dout, col0, name):
    T, C = dout.shape
    off = col0 // CONV_TILE
    xcol = pl.BlockSpec((T, CONV_TILE), lambda j: (0, j + off))
    dcol = pl.BlockSpec((T, CONV_TILE), lambda j: (0, j))

    def body(x_ref, w_ref, b_ref, d_ref, dx_ref, dw_ref, db_ref):
        x, w = x_ref[...], w_ref[...]
        _, vjp = jax.vjp(jax.nn.silu, _conv_pre(x, w, b_ref[...]))
        (dy,) = vjp(d_ref[...])
        row = lax.broadcasted_iota(jnp.int32, x.shape, 0)
        dx = dy * w[CONV_K - 1:CONV_K, :]
        dw_ref[CONV_K - 1:CONV_K, :] = _colsum(dy * x)
        for lag in range(1, CONV_K):
            ahead = jnp.where(row < T - lag, pltpu.roll(dy, T - lag, 0), 0.0)
            dx = dx + ahead * w[CONV_K - 1 - lag:CONV_K - lag, :]
            dw_ref[CONV_K - 1 - lag:CONV_K - lag, :] = _colsum(ahead * x)
        dx_ref[...] = dx.astype(dx_ref.dtype)
        db_ref[...] = _colsum(dy)

    return pl.pallas_call(
        body, name=name, grid=(C // CONV_TILE,),
        in_specs=[xcol, pl.BlockSpec((CONV_K, CONV_TILE), lambda j: (0, j + off)),
                  pl.BlockSpec((1, CONV_TILE), lambda j: (0, j + off)), dcol],
        out_specs=[dcol, pl.BlockSpec((CONV_K, CONV_TILE), lambda j: (0, j)), pl.BlockSpec((1, CONV_TILE), lambda j: (0, j))],
        out_shape=[jax.ShapeDtypeStruct((T, C), BF16), jax.ShapeDtypeStruct((CONV_K, C), F32), jax.ShapeDtypeStruct((1, C), F32)],
        compiler_params=_params(("parallel",)),
    )(x, w, b, dout)


def _ssd_common(x_ref, b_ref, c_ref, dtr_ref, bias_ref, alog_ref, tri_ref, triu_ref):
    Q = x_ref.shape[0]
    x, Bm, Cm = x_ref[...], b_ref[...], c_ref[...]
    pre = dtr_ref[...] + bias_ref[...]
    dt = _softplus(pre)
    A = -jnp.exp(alog_ref[...])
    adt = dt * A
    ac4 = _dot(tri_ref[...], adt, NN, HI)
    ar4 = _dot(adt, triu_ref[...], TN, HI)
    cb = _dot(Cm, Bm, NT)
    ii = lax.broadcasted_iota(jnp.int32, (Q, Q), 0)
    jj = lax.broadcasted_iota(jnp.int32, (Q, Q), 1)
    atot4 = ac4[Q - 1:Q, :]
    return x, Bm, Cm, pre, dt, A, ac4, ar4, cb, ii >= jj, atot4


def _ssd_decay(k, ac4, ar4, causal):
    seg = ac4[:, k:k + 1] - ar4[k:k + 1, :]
    return jnp.where(causal, jnp.exp(jnp.where(causal, seg, 0.0)), 0.0)


def _per_head(c4, n):
    lane = lax.broadcasted_iota(jnp.int32, (c4.shape[0], n), 1)
    out = jnp.broadcast_to(c4[:, 0:1], (c4.shape[0], n))
    for k in range(1, SSD_HPG):
        out = jnp.where(lane >= SSD_HEADDIM * k, c4[:, k:k + 1], out)
    return out


def _head_sums(v):
    n = SSD_HPG * SSD_HEADDIM
    row = lax.broadcasted_iota(jnp.int32, (n, SSD_HPG), 0)
    col = lax.broadcasted_iota(jnp.int32, (n, SSD_HPG), 1)
    member = jnp.where((row >= SSD_HEADDIM * col) & (row < SSD_HEADDIM * (col + 1)), 1.0, 0.0)
    return _dot(v, member, NN, HI)


def _head_rows(c4):
    n = SSD_HPG * SSD_HEADDIM
    row = lax.broadcasted_iota(jnp.int32, (n, 1), 0)
    out = jnp.broadcast_to(c4[:, 0:1], (n, 1))
    for k in range(1, SSD_HPG):
        out = jnp.where(row >= SSD_HEADDIM * k, c4[:, k:k + 1], out)
    return out


def _ssd_specs(T, Q, rev):
    NC = T // Q
    cc = (lambda c: NC - 1 - c) if rev else (lambda c: c)
    W, N, S = SSD_HPG * SSD_HEADDIM, SSD_STATE, SSD_STEP_GROUPS
    x_spec = pl.BlockSpec((Q, S * W), lambda g, c: (cc(c), g))
    b_spec = pl.BlockSpec((Q, S * N), lambda g, c: (cc(c), D_INNER // (S * N) + g))
    c_spec = pl.BlockSpec((Q, S * N), lambda g, c: (cc(c), (D_INNER + SSD_GROUPS * N) // (S * N) + g))
    dt_spec = pl.BlockSpec((S, Q, SSD_HPG), lambda g, c: (g, cc(c), 0))
    p_spec = pl.BlockSpec((S, 1, SSD_HPG), lambda g, c: (g, 0, 0))
    tri_spec = pl.BlockSpec((Q, Q), lambda g, c: (0, 0))
    h_spec = pl.BlockSpec((S, None, W, N), lambda g, c: (g, cc(c), 0, 0))
    return x_spec, b_spec, c_spec, dt_spec, p_spec, tri_spec, h_spec


def _ssd_views(gi, wide, narrow, lead):
    W, N = SSD_HPG * SSD_HEADDIM, SSD_STATE
    return ([r.at[:, pl.ds(W * gi, W)] for r in wide] + [r.at[:, pl.ds(N * gi, N)] for r in narrow]
            + [r.at[gi] for r in lead])


def _tri(Q):
    tri = jnp.tril(jnp.ones((Q, Q), F32))
    return tri, tri.T


def _ssd_fwd(xc, dtr8, bias8, alog8, dsk8):
    T, Q = xc.shape[0], SSD_Q
    NC = T // Q
    P, W = SSD_HEADDIM, SSD_HPG * SSD_HEADDIM
    x_spec, b_spec, c_spec, dt_spec, p_spec, tri_spec, h_spec = _ssd_specs(T, Q, False)

    def body(x_ref, b_ref, c_ref, dtr_ref, bias_ref, alog_ref, dsk_ref, tri_ref, triu_ref, y_ref, hs_ref, h_scr):
        for gi in range(SSD_STEP_GROUPS):
            xv, yv, bv, cv, dtv, biasv, alogv, dskv, hsv, hv = _ssd_views(
                gi, (x_ref, y_ref), (b_ref, c_ref), (dtr_ref, bias_ref, alog_ref, dsk_ref, hs_ref, h_scr))
            one(xv, bv, cv, dtv, biasv, alogv, dskv, tri_ref, triu_ref, yv, hsv, hv)

    def one(x_ref, b_ref, c_ref, dtr_ref, bias_ref, alog_ref, dsk_ref, tri_ref, triu_ref, y_ref, hs_ref, h_scr):
        @pl.when(pl.program_id(1) == 0)
        def _():
            h_scr[...] = jnp.zeros_like(h_scr)

        hin = h_scr[...]
        hs_ref[...] = hin
        x, Bm, Cm, pre, dt, A, ac4, ar4, cb, causal, atot4 = _ssd_common(
            x_ref, b_ref, c_ref, dtr_ref, bias_ref, alog_ref, tri_ref, triu_ref)
        xdt = x * _per_head(dt, W)
        rest = _dot(Cm, hin, NT) * _per_head(jnp.exp(ac4), W) + _per_head(dsk_ref[...], W) * x
        for k in range(SSD_HPG):
            blk = slice(P * k, P * (k + 1))
            G = cb * _ssd_decay(k, ac4, ar4, causal)
            y_ref[:, blk] = _dot(G, xdt[:, blk], NN) + rest[:, blk]
        xw = xdt * _per_head(jnp.exp(atot4 - ac4), W)
        h_scr[...] = _head_rows(jnp.exp(atot4)) * hin + _dot(xw, Bm, TN)

    tri, triu = _tri(Q)
    return pl.pallas_call(
        body, name="ssd_fwd", grid=(SSD_GROUPS // SSD_STEP_GROUPS, NC),
        in_specs=[x_spec, b_spec, c_spec, dt_spec, p_spec, p_spec, p_spec, tri_spec, tri_spec],
        out_specs=[pl.BlockSpec((Q, SSD_STEP_GROUPS * W), lambda g, c: (c, g)), h_spec],
        out_shape=[jax.ShapeDtypeStruct((T, D_INNER), F32),
                   jax.ShapeDtypeStruct((SSD_GROUPS, NC, W, SSD_STATE), F32)],
        scratch_shapes=[pltpu.VMEM((SSD_STEP_GROUPS, W, SSD_STATE), F32)],
        compiler_params=_params(("parallel", "arbitrary")),
    )(xc, xc, xc, dtr8, bias8, alog8, dsk8, tri, triu)


def _ssd_bwd(xc, dtr8, bias8, alog8, dsk8, hs, dy):
    T, Q = xc.shape[0], SSD_Q
    NC = T // Q
    P, W = SSD_HEADDIM, SSD_HPG * SSD_HEADDIM
    x_spec, b_spec, c_spec, dt_spec, p_spec, tri_spec, h_spec = _ssd_specs(T, Q, True)
    dy_spec = pl.BlockSpec((Q, SSD_STEP_GROUPS * W), lambda g, c: (NC - 1 - c, g))
    dbc_spec = pl.BlockSpec((Q, SSD_STEP_GROUPS * SSD_STATE), lambda g, c: (NC - 1 - c, g))

    def body(x_ref, b_ref, c_ref, dtr_ref, bias_ref, alog_ref, dsk_ref, tri_ref, triu_ref, hs_ref, dy_ref,
             dx_ref, db_ref, dc_ref, ddtr_ref, dbias_ref, dalog_ref, ddsk_ref, dh_scr):
        for gi in range(SSD_STEP_GROUPS):
            (xv, dyv, dxv, bv, cv, dbv, dcv, dtv, biasv, alogv, dskv, hsv, ddtv, dbiasv, dalogv, ddskv, dhv) = _ssd_views(
                gi, (x_ref, dy_ref, dx_ref), (b_ref, c_ref, db_ref, dc_ref),
                (dtr_ref, bias_ref, alog_ref, dsk_ref, hs_ref, ddtr_ref, dbias_ref, dalog_ref, ddsk_ref, dh_scr))
            one(xv, bv, cv, dtv, biasv, alogv, dskv, tri_ref, triu_ref, hsv, dyv,
                dxv, dbv, dcv, ddtv, dbiasv, dalogv, ddskv, dhv)

    def one(x_ref, b_ref, c_ref, dtr_ref, bias_ref, alog_ref, dsk_ref, tri_ref, triu_ref, hs_ref, dy_ref,
            dx_ref, db_ref, dc_ref, ddtr_ref, dbias_ref, dalog_ref, ddsk_ref, dh_scr):
        @pl.when(pl.program_id(1) == 0)
        def _():
            dh_scr[...] = jnp.zeros_like(dh_scr)
            dbias_ref[...] = jnp.zeros_like(dbias_ref)
            dalog_ref[...] = jnp.zeros_like(dalog_ref)
            ddsk_ref[...] = jnp.zeros_like(ddsk_ref)

        x, Bm, Cm, pre, dt, A, ac4, ar4, cb, causal, atot4 = _ssd_common(
            x_ref, b_ref, c_ref, dtr_ref, bias_ref, alog_ref, tri_ref, triu_ref)
        dyv, hin, dho = dy_ref[...], hs_ref[...], dh_scr[...]
        dt_w = _per_head(dt, W)
        xdt = x * dt_w
        E4, F4, etot4 = jnp.exp(ac4), jnp.exp(atot4 - ac4), jnp.exp(atot4)
        F_w = _per_head(F4, W)
        ddsk_ref[...] += _head_sums(_colsum(dyv * x))
        Z = _dot(Cm, hin, NT)
        dZ = dyv * _per_head(E4, W)
        dac4 = _head_sums(dyv * Z) * E4
        dC = _dot(dZ, hin, NN)
        dh_scr[...] = _dot(dZ, Cm, TN) + _head_rows(etot4) * dho
        per_row = jnp.sum(dho * hin, axis=1, keepdims=True)
        lane4 = lax.broadcasted_iota(jnp.int32, (1, SSD_HPG), 1)
        datot4 = jnp.zeros((1, SSD_HPG), F32)
        for k in range(SSD_HPG):
            datot4 = jnp.where(lane4 == k, jnp.sum(per_row[P * k:P * (k + 1)], keepdims=True), datot4)
        datot4 = datot4 * etot4
        dxw = _dot(Bm, dho, NT)
        dB = _dot(xdt * F_w, dho, NN)
        dFa4 = _head_sums(dxw * xdt) * F4
        datot4 = datot4 + _colsum(dFa4)
        dac4 = dac4 - dFa4
        dcb = jnp.zeros((Q, Q), F32)
        lane_q4 = lax.broadcasted_iota(jnp.int32, (Q, SSD_HPG), 1)
        sub_4q = lax.broadcasted_iota(jnp.int32, (SSD_HPG, Q), 0)
        row_sums = jnp.zeros((Q, SSD_HPG), F32)
        col_sums = jnp.zeros((SSD_HPG, Q), F32)
        dxdt_heads = []
        for k in range(SSD_HPG):
            blk = slice(P * k, P * (k + 1))
            L = _ssd_decay(k, ac4, ar4, causal)
            G = cb * L
            dG = _dot(dyv[:, blk], xdt[:, blk], NT)
            dxdt_heads.append(_dot(G, dyv[:, blk], TN))
            dcb = dcb + dG * L
            Mseg = dG * G
            row_sums = jnp.where(lane_q4 == k, jnp.sum(Mseg, axis=1, keepdims=True), row_sums)
            col_sums = jnp.where(sub_4q == k, jnp.sum(Mseg, axis=0, keepdims=True), col_sums)
        dxdt = dxw * F_w + jnp.concatenate(dxdt_heads, axis=1)
        last = lax.broadcasted_iota(jnp.int32, (Q, SSD_HPG), 0) == Q - 1
        dac4 = dac4 + row_sums + jnp.where(last, datot4, 0.0)
        dadt4 = _dot(triu_ref[...], dac4, NN, HI) - _dot(triu_ref[...], col_sums, NT, HI)
        ddt4 = _head_sums(dxdt * x) + dadt4 * A
        dalog_ref[...] += _colsum(dadt4 * dt) * A
        ddtr4 = ddt4 * jax.nn.sigmoid(pre)
        dbias_ref[...] += _colsum(ddtr4)
        ddtr_ref[...] = ddtr4
        dx_ref[...] = (_per_head(dsk_ref[...], W) * dyv + dxdt * dt_w).astype(dx_ref.dtype)
        db_ref[...] = dB + _dot(dcb, Cm, TN)
        dc_ref[...] = dC + _dot(dcb, Bm, NN)

    tri, triu = _tri(Q)
    return pl.pallas_call(
        body, name="ssd_bwd", grid=(SSD_GROUPS // SSD_STEP_GROUPS, NC),
        in_specs=[x_spec, b_spec, c_spec, dt_spec, p_spec, p_spec, p_spec, tri_spec, tri_spec, h_spec, dy_spec],
        out_specs=[dy_spec, dbc_spec, dbc_spec, dt_spec, p_spec, p_spec, p_spec],
        out_shape=[jax.ShapeDtypeStruct((T, D_INNER), F32),
                   jax.ShapeDtypeStruct((T, SSD_GROUPS * SSD_STATE), F32),
                   jax.ShapeDtypeStruct((T, SSD_GROUPS * SSD_STATE), F32),
                   jax.ShapeDtypeStruct(dtr8.shape, F32)] + [jax.ShapeDtypeStruct(bias8.shape, F32)] * 3,
        scratch_shapes=[pltpu.VMEM((SSD_STEP_GROUPS, W, SSD_STATE), F32)],
        compiler_params=_params(("parallel", "arbitrary")),
    )(xc, xc, xc, dtr8, bias8, alog8, dsk8, tri, triu, hs, dy)


def _to_groups(v):
    return v.reshape(SSD_GROUPS, 1, SSD_HPG)


def _mixer_fwd(x, p, w):
    T, D = x.shape
    (hn,) = _rowwise(lambda xt, g: (_rms(xt, g),), [x], [p["mix_norm"]], [(D, BF16)], name="mix_norm")
    winT = w["w_in"]
    u, z, xbc, dtr, gl = [
        _matmul(hn, winT[IN_OFFS[i]:IN_OFFS[i + 1]], "nt", name=f"mix_in{i}") for i in range(5)]
    tiles = _s5_derive(*[p[k][0] for k in ("s5_A_re", "s5_A_im", "s5_log_dt", "s5_B_re", "s5_B_im", "s5_C_re", "s5_C_im")])
    u = _to_seg(u)
    ys = _s5_fwd(u, tiles)
    (g,) = _rowwise(lambda yt, ut, d: (jax.nn.gelu(yt + d * ut),), [ys, u], [p["s5_D"]], [(D, F32)], name="s5_gelu")
    y5 = _matmul(g, w["s5_w_glu"], "nn", name="s5_glu", out=(BF16,),
                 epilogue=lambda acc, gt, b: (gt * jax.nn.sigmoid(acc + b),), extras=[g, p["s5_b_glu"]])
    xc = _conv_fwd(xbc, w["conv_w"], p["conv_b"])
    dtr8 = dtr.reshape(T, SSD_GROUPS, SSD_HPG).transpose(1, 0, 2)
    ssd_p = [_to_groups(p[k]) for k in ("ssd_dt_bias", "ssd_A_log", "ssd_D")]
    yssd_raw, hs = _ssd_fwd(xc, dtr8, *ssd_p)
    (yssd,) = _rowwise(lambda yt, zt, nw: (_rms(yt * jax.nn.silu(zt), nw),), [yssd_raw, z], [p["ssd_norm"]],
                       [(D_INNER, BF16)], name="ssd_gate")
    p5 = _from_seg(_matmul(y5, w["w_proj_s5"], "nn", name="mix_p5"))
    pssd = _matmul(yssd, w["w_proj_ssd"], "nn", name="mix_pssd")

    def merge(glt, at, bt, bg):
        gates = jax.nn.sigmoid(glt + bg)
        return (gates[:, :D] * at + gates[:, D:] * bt,)

    (merged,) = _rowwise(merge, [gl, p5, pssd], [p["b_gate"]], [(D, BF16)], name="mix_merge")
    y = _matmul(merged, w["w_out"], "nn", name="mix_out", epilogue=lambda acc, xt: (xt + acc,), extras=[x])
    res = dict(hn=hn, u=u, z=z, xbc=xbc, gl=gl, tiles=tiles, ys=ys, g=g, y5=y5, xc=xc, dtr8=dtr8, ssd_p=ssd_p,
               yssd_raw=yssd_raw, hs=hs, yssd=yssd, p5=p5, pssd=pssd, merged=merged)
    return y, res


def _mixer_bwd(dy, dy16, x, p, w, r):
    T, D = x.shape
    gp = {}
    dmerged = _matmul(dy16, w["w_out"], "nt", name="mix_d_merged")
    gp["w_out"] = _matmul(r["merged"], dy16, "tn", name="mix_d_wout")

    def merge_bwd(glt, at, bt, dm, bg):
        def f(q, a_, b_):
            gates = jax.nn.sigmoid(q + bg)
            return gates[:, :D] * a_ + gates[:, D:] * b_
        _, vjp = jax.vjp(f, glt, at, bt)
        dq, da_, db_ = vjp(dm)
        return da_, db_, dq, _colsum(dq)

    dp5, dpssd, dgl, gp["b_gate"] = _rowwise(
        merge_bwd, [r["gl"], r["p5"], r["pssd"], dmerged], [p["b_gate"]],
        [(D, BF16), (D, BF16), (2 * D, BF16)], [((1, 2 * D), F32)], name="mix_d_merge")
    dyssd = _matmul(dpssd, w["w_proj_ssd"], "nt", name="mix_d_yssd")
    gp["w_proj_ssd"] = _matmul(r["yssd"], dpssd, "tn", name="mix_d_wpssd")

    def gate_bwd(yt, zt, dyt, nw):
        _, vjp = jax.vjp(lambda a_, b_, c_: _rms(a_ * jax.nn.silu(b_), c_), yt, zt, nw)
        return vjp(dyt)

    dyraw, dz, gp["ssd_norm"] = _rowwise(
        gate_bwd, [r["yssd_raw"], r["z"], dyssd], [p["ssd_norm"]],
        [(D_INNER, F32), (D_INNER, BF16)], [((1, D_INNER), F32)], name="ssd_d_gate")
    dxs, dBm, dCm, ddtr8, dbias8, dalog8, ddsk8 = _ssd_bwd(r["xc"], r["dtr8"], *r["ssd_p"], r["hs"], dyraw)
    gp["ssd_dt_bias"], gp["ssd_A_log"], gp["ssd_D"] = [v.reshape(1, SSD_HEADS) for v in (dbias8, dalog8, ddsk8)]
    ddtr = ddtr8.transpose(1, 0, 2).reshape(T, SSD_HEADS)
    conv = [_conv_bwd(r["xbc"], w["conv_w"], p["conv_b"], d, c0, f"conv_bwd{i}")
            for i, (d, c0) in enumerate(((dxs, 0), (dBm, D_INNER), (dCm, D_INNER + SSD_GROUPS * SSD_STATE)))]
    dxbc = [c[0] for c in conv]
    gp["conv_w"] = jnp.concatenate([c[1] for c in conv], axis=1)
    gp["conv_b"] = jnp.concatenate([c[2] for c in conv], axis=1)

    dp5 = _to_seg(dp5)
    dy5 = _matmul(dp5, w["w_proj_s5"], "nt", name="mix_d_y5")
    gp["w_proj_s5"] = _matmul(r["y5"], dp5, "tn", name="mix_d_wp5")
    g = r["g"]

    def glu_ep(acc, gt, b, dyt):
        _, vjp = jax.vjp(lambda g_, t_: g_ * jax.nn.sigmoid(t_ + b), gt, acc)
        return vjp(dyt)

    dg1, dt_ = _matmul(g, w["s5_w_glu"], "nn", name="s5_d_glu", out=(F32, BF16), epilogue=glu_ep,
                       extras=[g, p["s5_b_glu"], dy5])
    gp["s5_w_glu"] = _matmul(g, dt_, "tn", name="s5_d_wglu")
    dg = _matmul(dt_, w["s5_w_glu"], "nt", name="s5_d_g", epilogue=lambda acc, e: (acc + e,), extras=[dg1])

    def gelu_bwd(yt, ut, dgt, dtt, d):
        _, vjp = jax.vjp(lambda y_, d_: jax.nn.gelu(y_ + d_ * ut), yt, d)
        dys, dd = vjp(dgt)
        return dys, dys * d, dd, _colsum(dtt.astype(F32))

    dys, dusk, gp["s5_D"], gp["s5_b_glu"] = _rowwise(
        gelu_bwd, [r["ys"], r["u"], dg, dt_], [p["s5_D"]], [(D, F32), (D, F32)], [((1, D), F32), ((1, D), F32)],
        name="s5_d_gelu")
    du, *dtiles = _s5_bwd(r["u"], dys, dusk, r["tiles"])
    du = _from_seg(du)

    pieces = [du, dz, None, ddtr, dgl]
    winT = w["w_in"]
    hn = r["hn"]
    d_rows, dhn = [], None
    cols = [(du, 0, 1024), (dz, 1024, 3072), (dxbc[0], 3072, 5120), (dxbc[1], 5120, 6144), (dxbc[2], 6144, 7168),
            (ddtr, 7168, 7200), (dgl, 7200, 9248)]
    for i, (d, lo, hi) in enumerate(cols):
        d_rows.append(_matmul(d, hn, "tn", name=f"mix_d_win{i}"))
        if dhn is None:
            dhn = _matmul(d, winT[lo:hi], "nn", name=f"mix_d_hn{i}")
        else:
            dhn = _matmul(d, winT[lo:hi], "nn", name=f"mix_d_hn{i}", epilogue=lambda acc, e: (acc + e,), extras=[dhn])
    gp["w_in"] = jnp.concatenate(d_rows, axis=0)
    del pieces

    def norm_bwd(xt, dh, dyt, gn):
        _, vjp = jax.vjp(_rms, xt, gn)
        dx, dgn = vjp(dh)
        return dyt + dx, dyt + dx, dgn

    dx, dx16, gp["mix_norm"] = _rowwise(norm_bwd, [x, dhn, dy], [p["mix_norm"]], [(D, F32), (D, BF16)],
                                        [((1, D), F32)], name="mix_d_norm")
    return dx, dx16, gp, dtiles


def _s5_param_grads(p, dtiles):
    keys = ("s5_A_re", "s5_A_im", "s5_log_dt", "s5_B_re", "s5_B_im", "s5_C_re", "s5_C_im")
    _, vjp = jax.vjp(_s5_derive, *[p[k][0] for k in keys])
    return {k: v[None] for k, v in zip(keys, vjp(tuple(dtiles)))}


def _local_step(x, target, p, w, late_weights, exchange):
    T, D = x.shape
    x1, r1 = _ffn_fwd(x, p["ffn1_norm"], w["ffn1_w_gate"], w["ffn1_w_up"], w["ffn1_w_down"], "ffn1")
    x2, rm = _mixer_fwd(x1, p, w)
    w = {**w, **late_weights(rm["p5"])}
    x3, r2 = _ffn_fwd(x2, p["ffn2_norm"], w["ffn2_w_gate"], w["ffn2_w_up"], w["ffn2_w_down"], "ffn2")

    def head(xt, tt, g):
        def f(x_, g_):
            e = _rms(x_, g_) - tt
            return 0.5 * jnp.sum(jnp.mean(e * e, axis=-1))
        l, (dx_, dg_) = jax.value_and_grad(f, argnums=(0, 1))(xt, g)
        return dx_, dx_, l.reshape(1, 1), dg_

    dx3, dx3_16, loss, d_final = _rowwise(head, [x3, target], [p["final_norm"]], [(D, F32), (D, BF16)],
                                          [((1, 1), F32), ((1, D), F32)], name="loss_head")
    gp = {"final_norm": d_final}

    def ffn_exchange(tag):
        return lambda d_wgT, d_wuT, d_wd, after: exchange(
            tag, {f"{tag}_w_gate": d_wgT, f"{tag}_w_up": d_wuT, f"{tag}_w_down": d_wd}, after)

    dx2, dx2_16, gp["ffn2_norm"] = _ffn_bwd(
        dx3, dx3_16, x2, p["ffn2_norm"], w["ffn2_w_gate"], w["ffn2_w_up"], w["ffn2_w_down"], r2, "ffn2", (),
        ffn_exchange("ffn2"))
    dx1, dx1_16, gm, dtiles = _mixer_bwd(dx2, dx2_16, x1, p, w, rm)
    conv_w_grad = gm.pop("conv_w")
    started = exchange("mixer", {k: gm.pop(k) for k in ("w_out", "w_proj_s5", "w_proj_ssd", "s5_w_glu", "w_in")}, dx1)
    gp.update(gm)
    gp.update(_s5_param_grads(p, dtiles))
    dx0, _, gp["ffn1_norm"] = _ffn_bwd(
        dx1, dx1_16, x, p["ffn1_norm"], w["ffn1_w_gate"], w["ffn1_w_up"], w["ffn1_w_down"], r1, "ffn1", started,
        ffn_exchange("ffn1"))
    return loss, dx0, gp, conv_w_grad


ADAMW_BLOCK = 1 << 18


def _adamw(w, m, v, gs, *, name, emit_g):
    _, R, C = w.shape
    n_g = len(gs)
    n_o = 4 if emit_g else 3
    tm = _rtile(R, max(8, ADAMW_BLOCK // C))
    lead = pl.BlockSpec((None, tm, C), lambda i: (0, i, 0))

    def body(w_ref, m_ref, v_ref, *refs):
        g = refs[0][...]
        for r in refs[1:n_g]:
            g = g + r[...]
        m2 = ADAM_B1 * m_ref[...] + (1.0 - ADAM_B1) * g
        v2 = ADAM_B2 * v_ref[...] + (1.0 - ADAM_B2) * (g * g)
        m_hat = m2 / (1.0 - ADAM_B1 ** ADAM_STEP)
        v_hat = v2 / (1.0 - ADAM_B2 ** ADAM_STEP)
        delta = -ADAM_LR * (m_hat / (jnp.sqrt(v_hat) + ADAM_EPS) + ADAM_WD * w_ref[...])
        for r, val in zip(refs[n_g:], ((g,) if emit_g else ()) + (delta, m2, v2)):
            r[...] = val

    return pl.pallas_call(
        body, name=name, grid=(R // tm,),
        in_specs=[lead] * 3 + [pl.BlockSpec((tm, C), lambda i: (i, 0))] * n_g, out_specs=[lead] * n_o,
        out_shape=[jax.ShapeDtypeStruct((1, R, C), F32)] * n_o,
        compiler_params=_params(("parallel",)),
    )(w, m, v, *gs)


def _rtile(r, cap=512):
    if r <= cap:
        return r
    best = None
    for t in range(8, cap + 1, 8):
        if r % t == 0:
            best = t
    assert best is not None, r
    return best


ANY = pl.BlockSpec(memory_space=pl.ANY)


def _where_am_i():
    x, y, c = lax.axis_index("x"), lax.axis_index("y"), lax.axis_index("c")
    ks = (2 * x + y, 2 * (1 - x) + y, 2 * x + (1 - y), 2 * (1 - x) + (1 - y))
    return x, y, c, ks


def _rc(src, dst, ssem, rsem, to):
    return pltpu.make_async_remote_copy(src_ref=src, dst_ref=dst, send_sem=ssem, recv_sem=rsem,
                                        device_id=to, device_id_type=MESH)


def _gather_shards(srcs):
    n = len(srcs)

    def body(*refs):
        _gather_body(refs[:n], refs[n:2 * n], *refs[2 * n:], extra=())

    return pl.pallas_call(
        body, name="gather_shards",
        in_specs=[ANY] * n, out_specs=[ANY] * n,
        out_shape=[jax.ShapeDtypeStruct((4,) + s.shape, s.dtype) for s in srcs],
        scratch_shapes=[pltpu.SemaphoreType.DMA((n, 6)), pltpu.SemaphoreType.DMA((n, 6)), pltpu.SemaphoreType.DMA((n,))],
        compiler_params=pltpu.CompilerParams(has_side_effects=True),
    )(*srcs)


def _handshake_all():
    x, y, c = lax.axis_index("x"), lax.axis_index("y"), lax.axis_index("c")
    barrier = pltpu.get_barrier_semaphore()
    for dx, dy, dc in itertools.product((0, 1), repeat=3):
        if (dx, dy, dc) != (0, 0, 0):
            to = (1 - x if dx else x, 1 - y if dy else y, 1 - c if dc else c)
            pl.semaphore_signal(barrier, inc=1, device_id=to, device_id_type=MESH)
    pl.semaphore_wait(barrier, 7)


SEQUENCER = dict(axis_name="seq", num_cores=1)


def _gather_shards_async(srcs, token, name, collective_id):
    n = len(srcs)

    def body(*refs):
        tok, tok_out = refs[n], refs[2 * n + 1]
        ssem, rsem, lsem = refs[2 * n + 2:]
        _handshake_all()
        _gather_body(refs[:n], refs[n + 1:2 * n + 1], ssem, rsem, lsem,
                     extra=(pltpu.make_async_copy(tok, tok_out, lsem.at[n]),))

    res = pl.kernel(
        body, name=name,
        out_type=[jax.ShapeDtypeStruct((4,) + s.shape, s.dtype) for s in srcs]
        + [jax.ShapeDtypeStruct(token.shape, token.dtype)],
        mesh=plsc.ScalarSubcoreMesh(**SEQUENCER),
        scratch_types=[pltpu.SemaphoreType.DMA((n, 6)), pltpu.SemaphoreType.DMA((n, 6)), pltpu.SemaphoreType.DMA((n + 1,))],
        compiler_params=pltpu.CompilerParams(collective_id=collective_id),
    )(*srcs, token)
    return res[:n], res[n]


def _gather_body(src, out, ssem, rsem, lsem, extra):
        n = len(src)
        x, y, c, (k_me, k_x, k_y, k_d) = _where_am_i()
        sib = (x, y, 1 - c)
        local = [pltpu.make_async_copy(src[i], out[i].at[k_me], lsem.at[i]) for i in range(n)] + list(extra)
        for cp in local:
            cp.start()

        def own(i, q, to):
            return _rc(src[i], out[i].at[k_me], ssem.at[i, q], rsem.at[i, q], to)

        def slab(i, q, k, to):
            return _rc(out[i].at[k], out[i].at[k], ssem.at[i, q], rsem.at[i, q], to)

        @pl.when(c == 1)
        def _():
            sends = [own(i, 0, (1 - x, y, 1)) for i in range(n)]
            for cp in sends:
                cp.start()
            fwd = [slab(i, 3, k_x, sib) for i in range(n)]
            for i in range(n):
                slab(i, 0, k_x, sib).wait_recv()
                fwd[i].start()
            for i in range(n):
                slab(i, 4, k_y, sib).wait_recv()
                slab(i, 5, k_d, sib).wait_recv()
            for cp in sends + fwd:
                cp.wait_send()

        @pl.when(c == 0)
        def _():
            sends = [own(i, 1, (x, 1 - y, 0)) for i in range(n)] + [own(i, 2, (1 - x, 1 - y, 0)) for i in range(n)]
            for cp in sends:
                cp.start()
            fwd = [slab(i, 4, k_y, sib) for i in range(n)] + [slab(i, 5, k_d, sib) for i in range(n)]
            for i in range(n):
                slab(i, 1, k_y, sib).wait_recv()
                fwd[i].start()
            for i in range(n):
                slab(i, 2, k_d, sib).wait_recv()
                fwd[n + i].start()
            for i in range(n):
                slab(i, 3, k_x, sib).wait_recv()
            for cp in sends + fwd:
                cp.wait_send()

        for cp in local:
            cp.wait()


def _swap_slabs(arrs, n_slabs, name, async_id, token):
    n = len(arrs)
    J = max(n_slabs, 1)

    def body(*refs):
        src, out = refs[:n], refs[n:2 * n]
        ssem, rsem = refs[2 * n:]
        x, y, c, (k_me, k_x, k_y, k_d) = _where_am_i()
        sib = (x, y, 1 - c)
        sel = (jnp.where(c == 1, k_y, k_me), jnp.where(c == 1, k_d, k_x))
        cps = []
        for i in range(n):
            for j in range(J):
                s = src[i].at[sel[j]] if n_slabs else src[i]
                d = out[i].at[j] if n_slabs else out[i]
                cps.append(_rc(s, d, ssem.at[i, j], rsem.at[i, j], sib))
        for cp in cps:
            cp.start()
        for cp in cps:
            cp.wait()

    return _exchange_call(
        body, arrs, [jax.ShapeDtypeStruct(((n_slabs,) if n_slabs else ()) + a.shape[-2:], a.dtype) for a in arrs],
        [pltpu.SemaphoreType.DMA((n, J)), pltpu.SemaphoreType.DMA((n, J))], name, async_id, token)


def _exchange_call(body, arrs, out_shape, sems, name, collective_id, token):
    n, m = len(arrs), len(out_shape)

    def seq_body(*refs):
        tok, tok_out, tok_sem = refs[n], refs[n + 1 + m], refs[-1]
        _handshake_all()
        cp = pltpu.make_async_copy(tok, tok_out, tok_sem)
        cp.start()
        body(*refs[:n], *refs[n + 1:n + 1 + m], *refs[n + 2 + m:-1])
        cp.wait()

    res = pl.kernel(
        seq_body, name=name, out_type=list(out_shape) + [jax.ShapeDtypeStruct(token.shape, token.dtype)],
        mesh=plsc.ScalarSubcoreMesh(**SEQUENCER), scratch_types=list(sems) + [pltpu.SemaphoreType.DMA],
        compiler_params=pltpu.CompilerParams(collective_id=collective_id))(*arrs, token)
    return res[:m], res[m]


def _send_chip_sums(arrs, name, async_id, token):
    n = len(arrs)

    def body(*refs):
        src, out = refs[:n], refs[n:2 * n]
        ssem, rsem = refs[2 * n:]
        x, y, c, _ = _where_am_i()

        @pl.when(c == 1)
        def _():
            cps = [_rc(src[i].at[1], out[i].at[0], ssem.at[i, 0], rsem.at[i, 0], (1 - x, y, 1)) for i in range(n)]
            for cp in cps:
                cp.start()
            for cp in cps:
                cp.wait()

        @pl.when(c == 0)
        def _():
            cps = [_rc(src[i].at[0], out[i].at[0], ssem.at[i, 0], rsem.at[i, 0], (x, 1 - y, 0)) for i in range(n)]
            cps += [_rc(src[i].at[1], out[i].at[1], ssem.at[i, 1], rsem.at[i, 1], (1 - x, 1 - y, 0)) for i in range(n)]
            for cp in cps:
                cp.start()
            for cp in cps:
                cp.wait()

    return _exchange_call(
        body, arrs, [jax.ShapeDtypeStruct(a.shape, a.dtype) for a in arrs],
        [pltpu.SemaphoreType.DMA((n, 2)), pltpu.SemaphoreType.DMA((n, 2))], name, async_id, token)


def _chip_sum(g, recv, sel, name):
    _, r, C = g.shape
    tr = _rtile(r)

    def body(sel_ref, g_ref, r_ref, o32_ref, o16_ref):
        s = g_ref[...] + r_ref[...]
        o32_ref[...] = s
        o16_ref[...] = s.astype(BF16)

    blk = pl.BlockSpec((None, tr, C), lambda j, t, sel_ref: (j, t, 0))
    return pl.pallas_call(
        body, name=name,
        grid_spec=pltpu.PrefetchScalarGridSpec(
            num_scalar_prefetch=1, grid=(2, r // tr),
            in_specs=[pl.BlockSpec((None, tr, C), lambda j, t, sel_ref: (sel_ref[j], t, 0)), blk],
            out_specs=[blk, blk]),
        out_shape=[jax.ShapeDtypeStruct((2, r, C), F32), jax.ShapeDtypeStruct((2, r, C), BF16)],
        compiler_params=_params(("arbitrary", "arbitrary")),
    )(sel, g, recv)


def _cross_sum(s32, got, north, after, name):
    _, r, C = s32.shape
    tr = _rtile(r)

    def body(f_ref, p_ref, a_ref, b_ref, after_ref, o_ref):
        a = a_ref[...].astype(F32)

        @pl.when(f_ref[0] == 1)
        def _():
            o_ref[...] = p_ref[...] + a

        @pl.when(f_ref[0] == 0)
        def _():
            o_ref[...] = a + b_ref[...].astype(F32)

    return pl.pallas_call(
        body, name=name,
        grid_spec=pltpu.PrefetchScalarGridSpec(
            num_scalar_prefetch=1, grid=(r // tr,),
            in_specs=[pl.BlockSpec((None, tr, C), lambda t, f: (0, t, 0)),
                      pl.BlockSpec((None, tr, C), lambda t, f: (0, t, 0)),
                      pl.BlockSpec((None, tr, C), lambda t, f: (1 - f[0], t, 0)), ANY],
            out_specs=pl.BlockSpec((tr, C), lambda t, f: (t, 0))),
        out_shape=jax.ShapeDtypeStruct((r, C), F32),
        compiler_params=_params(("arbitrary",)),
    )(north, s32, got, got, after)


class _Reducer:
    def __init__(self, token):
        self.token, self.pending, self.pairs = token, None, {}

    def begin(self, tag, ids, grads, after):
        names = list(grads)
        arrs = [grads[n].reshape(4, -1, D_MODEL) for n in names]
        x, y, c, (k_me, k_x, k_y, k_d) = _where_am_i()
        sel = jnp.stack([jnp.where(c == 1, k_me, k_y), jnp.where(c == 1, k_x, k_d)]).astype(jnp.int32)
        from_sib, self.token = _swap_slabs(arrs, 2, f"swap_grad_slabs_{tag}", ids[0], self.token)
        sums = [_chip_sum(g, r, sel, f"chip_sum_{tag}{i}") for i, (g, r) in enumerate(zip(arrs, from_sib))]
        self.finish(after)
        got, self.token = _send_chip_sums([s16 for _, s16 in sums], f"send_chip_sums_{tag}", ids[1], self.token)
        self.pending = (tag, ids[2], names, sums, got)
        return tuple(s16 for _, s16 in sums)

    def finish(self, after):
        if self.pending is None:
            return
        tag, cid, names, sums, got = self.pending
        north = jnp.reshape(lax.axis_index("c"), (1,)).astype(jnp.int32)
        parts = [_cross_sum(s32, b, north, after, f"cross_sum_{tag}{i}") for i, ((s32, _), b) in enumerate(zip(sums, got))]
        theirs, self.token = _swap_slabs(parts, 0, f"swap_parts_{tag}", cid, self.token)
        self.pairs.update(zip(names, zip(parts, theirs)))
        self.pending = None


def _all_reduce_small(v, token, collective_id):
    R, C = v.shape
    tr = R

    def body(v_ref, buf, ssem, rsem, lsem):
        x, y, c = lax.axis_index("x"), lax.axis_index("y"), lax.axis_index("c")
        me, sib = (x, y, c), (x, y, 1 - c)
        chips = [(1 - x, y), (x, 1 - y), (1 - x, 1 - y)]

        def slot(px, py, pc):
            return buf.at[4 * px + 2 * py + pc]

        def copy(k, block, to, src=None):
            return _rc(slot(*block) if src is None else src, slot(*block), ssem.at[k], rsem.at[k], to)

        mine = pltpu.make_async_copy(v_ref, slot(*me), lsem)
        mine.start()
        first = [copy(0, me, sib, src=v_ref)] + [copy(1 + j, me, (*chip, c), src=v_ref) for j, chip in enumerate(chips)]
        for cp in first:
            cp.start()
        passed = [copy(4 + j, (*chip, c), sib) for j, chip in enumerate(chips)]
        for j, chip in enumerate(chips):
            copy(1 + j, (*chip, c), me).wait_recv()
            passed[j].start()
        copy(0, sib, me).wait_recv()
        for j, chip in enumerate(chips):
            copy(4 + j, (*chip, 1 - c), me).wait_recv()
        for cp in first + passed:
            cp.wait_send()
        mine.wait()

    (gathered,), token = _exchange_call(
        body, [v], [jax.ShapeDtypeStruct((8, R, C), F32)],
        [pltpu.SemaphoreType.DMA((7,)), pltpu.SemaphoreType.DMA((7,)), pltpu.SemaphoreType.DMA],
        "gather_small", collective_id, token)

    def add(g_ref, o_ref):
        acc = g_ref[0]
        for k in range(1, 8):
            acc = acc + g_ref[k]
        o_ref[...] = acc

    total = pl.pallas_call(
        add, name="sum_small", grid=(R // tr,),
        in_specs=[pl.BlockSpec((8, tr, C), lambda i: (0, i, 0))], out_specs=pl.BlockSpec((tr, C), lambda i: (i, 0)),
        out_shape=jax.ShapeDtypeStruct((R, C), F32), compiler_params=_params(("parallel",)),
    )(gathered)
    return total, token


WEIGHTS = ['ffn1_norm', 'ffn1_w_gate', 'ffn1_w_up', 'ffn1_w_down', 'mix_norm', 'w_in', 'conv_w', 'conv_b', 's5_A_re',
           's5_A_im', 's5_log_dt', 's5_B_re', 's5_B_im', 's5_C_re', 's5_C_im', 's5_D', 's5_w_glu', 's5_b_glu',
           'ssd_A_log', 'ssd_dt_bias', 'ssd_D', 'ssd_norm', 'w_proj_s5', 'w_proj_ssd', 'b_gate', 'w_out', 'ffn2_norm',
           'ffn2_w_gate', 'ffn2_w_up', 'ffn2_w_down', 'final_norm']
ARGS = ['x'] + WEIGHTS + ['loss_target'] + ['m_' + n for n in WEIGHTS] + ['v_' + n for n in WEIGHTS]
COL_SHARDED = ('ffn1_w_gate', 'ffn1_w_up', 'w_in', 'ffn2_w_gate', 'ffn2_w_up')
ROW_SHARDED = ('ffn1_w_down', 's5_w_glu', 'w_proj_s5', 'w_proj_ssd', 'w_out', 'ffn2_w_down')
MATRICES = COL_SHARDED + ROW_SHARDED
GATHER_FIRST = ('ffn1_w_gate', 'ffn1_w_up')
GATHER_FIRST_LATER = ('ffn1_w_down',)
GATHER_SECOND = ('w_in', 'conv_w', 's5_w_glu', 'w_proj_s5', 'w_proj_ssd', 'w_out')
GATHER_THIRD = ('ffn2_w_gate', 'ffn2_w_up', 'ffn2_w_down')
SMALL = [n for n in WEIGHTS if n not in MATRICES]


def _pack(arrs, width=1024):
    flat = jnp.concatenate([a.reshape(-1).astype(F32) for a in arrs])
    rows = -(-flat.shape[0] // (8 * width)) * 8
    return jnp.pad(flat, (0, rows * width - flat.shape[0])).reshape(rows, width)


def _unpack(packed, shapes):
    flat, out, o = packed.reshape(-1), [], 0
    for s in shapes:
        n = math.prod(s)
        out.append(flat[o:o + n].reshape(s))
        o += n
    return out


def kernel(x, ffn1_norm, ffn1_w_gate, ffn1_w_up, ffn1_w_down, mix_norm, w_in, conv_w, conv_b, s5_A_re, s5_A_im, s5_log_dt, s5_B_re, s5_B_im, s5_C_re, s5_C_im, s5_D, s5_w_glu, s5_b_glu, ssd_A_log, ssd_dt_bias, ssd_D, ssd_norm, w_proj_s5, w_proj_ssd, b_gate, w_out, ffn2_norm, ffn2_w_gate, ffn2_w_up, ffn2_w_down, final_norm, loss_target, m_ffn1_norm, m_ffn1_w_gate, m_ffn1_w_up, m_ffn1_w_down, m_mix_norm, m_w_in, m_conv_w, m_conv_b, m_s5_A_re, m_s5_A_im, m_s5_log_dt, m_s5_B_re, m_s5_B_im, m_s5_C_re, m_s5_C_im, m_s5_D, m_s5_w_glu, m_s5_b_glu, m_ssd_A_log, m_ssd_dt_bias, m_ssd_D, m_ssd_norm, m_w_proj_s5, m_w_proj_ssd, m_b_gate, m_w_out, m_ffn2_norm, m_ffn2_w_gate, m_ffn2_w_up, m_ffn2_w_down, m_final_norm, v_ffn1_norm, v_ffn1_w_gate, v_ffn1_w_up, v_ffn1_w_down, v_mix_norm, v_w_in, v_conv_w, v_conv_b, v_s5_A_re, v_s5_A_im, v_s5_log_dt, v_s5_B_re, v_s5_B_im, v_s5_C_re, v_s5_C_im, v_s5_D, v_s5_w_glu, v_s5_b_glu, v_ssd_A_log, v_ssd_dt_bias, v_ssd_D, v_ssd_norm, v_w_proj_s5, v_w_proj_ssd, v_b_gate, v_w_out, v_ffn2_norm, v_ffn2_w_gate, v_ffn2_w_up, v_ffn2_w_down, v_final_norm):
    a = dict(locals())
    assert list(a) == ARGS
    x, target = a['x'][0], a['loss_target'][0]
    k_me = 2 * lax.axis_index("x") + lax.axis_index("y")

    src = {n: a[n][0].T.astype(BF16) for n in COL_SHARDED}
    src.update({n: a[n][0].astype(BF16) for n in ROW_SHARDED})
    src['conv_w'] = a['conv_w'][0]
    def flat(f):
        return f.reshape(-1, f.shape[-1])

    full = _gather_shards([src[n] for n in GATHER_FIRST])
    w = {n: flat(f) for n, f in zip(GATHER_FIRST, full)}
    token = full[0][0, :8, :128].astype(F32)
    for names, name, cid in ((GATHER_FIRST_LATER, "gather_ffn1_down", 14), (GATHER_SECOND[:2], "gather_mixer_in", 1),
                             (GATHER_SECOND[2:], "gather_mixer", 12)):
        got, token = _gather_shards_async([src[n] for n in names], token, name, cid)
        w.update({n: flat(f) for n, f in zip(names, got)})
    w['conv_w'] = w['conv_w'].reshape(4, CONV_K, -1).transpose(1, 0, 2).reshape(CONV_K, CONV_DIM)
    reducer = _Reducer(token)

    def gather_last(after):
        got, reducer.token = _gather_shards_async(
            [src[n] for n in GATHER_THIRD], reducer.token + after[:8, :128], "gather_ffn2", 2)
        return {n: flat(f) for n, f in zip(GATHER_THIRD, got)}

    p = {n: a[n] for n in SMALL if n != 'conv_w'}
    p['final_norm'] = a['final_norm'][None]
    ids = {"ffn2": (3, 4, 5), "mixer": (6, 7, 8), "ffn1": (9, 10, 11)}
    loss, grad_x, gp, conv_w_grad = _local_step(
        x, target, p, w, gather_last, lambda tag, grads, after: reducer.begin(tag, ids[tag], grads, after))
    gp['conv_w'] = conv_w_grad[None]
    small_shapes = [(1, 1)] + [gp[n].shape if n != 'final_norm' else (1, D_MODEL) for n in SMALL]
    summed, reducer.token = _all_reduce_small(_pack([loss] + [gp[n] for n in SMALL]), reducer.token, 13)
    red = _unpack(summed, small_shapes)
    reducer.finish(grad_x)
    pairs = [reducer.pairs[n] for n in MATRICES]
    loss_all = red[0].reshape(())
    gsmall = dict(zip(SMALL, red[1:]))
    gsmall['conv_w'] = lax.dynamic_slice_in_dim(gsmall['conv_w'], k_me * 1024, 1024, axis=2)
    gsmall = {n: g.reshape(a[n].shape) for n, g in gsmall.items()}

    grads, delta, new_m, new_v = {}, {}, {}, {}
    for n, (mine, theirs) in zip(MATRICES, pairs):
        turn = (lambda t: jnp.swapaxes(t, 1, 2)) if n in COL_SHARDED else (lambda t: t)
        out = _adamw(turn(a[n]), turn(a['m_' + n]), turn(a['v_' + n]), [mine, theirs], name=f"adamw_{n}", emit_g=True)
        grads[n], delta[n], new_m[n], new_v[n] = [turn(t) for t in out]
    sw, sm, sv, sg = [_pack([t[n] for n in SMALL])[None] for t in (a, {n: a['m_' + n] for n in SMALL},
                                                                   {n: a['v_' + n] for n in SMALL}, gsmall)]
    d, m2, v2 = _adamw(sw, sm, sv, [sg[0]], name="adamw_small", emit_g=False)
    shapes = [a[n].shape for n in SMALL]
    for n, dd, mm, vv in zip(SMALL, _unpack(d[0], shapes), _unpack(m2[0], shapes), _unpack(v2[0], shapes)):
        grads[n], delta[n], new_m[n], new_v[n] = gsmall[n], dd, mm, vv
    return (loss_all, grad_x[None], *[grads[n] for n in WEIGHTS], *[delta[n] for n in WEIGHTS],
            *[new_m[n] for n in WEIGHTS], *[new_v[n] for n in WEIGHTS])
```

```python
import itertools
import math

import jax
import jax.numpy as jnp
from jax import lax
from jax.experimental import pallas as pl
from jax.experimental.pallas import tpu as pltpu
from jax.experimental.pallas import tpu_sc as plsc

F32 = jnp.float32
BF16 = jnp.bfloat16
MXU = BF16
HI = lax.Precision.HIGHEST

D_MODEL = 1024
D_FF = 2816
EPS = 1e-6
S5_GROUPS, S5_GROUP, S5_STATE = 64, 16, 64
D_INNER = 2048
SSD_HEADDIM, SSD_HEADS, SSD_GROUPS, SSD_HPG, SSD_STATE = 64, 32, 8, 4, 128
CONV_K, CONV_DIM = 4, 4096
IN_SPLITS = (1024, 2048, 4096, 32, 2048)
IN_OFFS = (0, 1024, 3072, 7168, 7200, 9248)
SSD_Q = 256
SSD_STEP_GROUPS = 2
ADAM_LR, ADAM_B1, ADAM_B2, ADAM_EPS, ADAM_WD, ADAM_STEP = 0.001, 0.9, 0.999, 1e-08, 0.01, 10

VMEM_LIMIT = 56 * 1024 * 1024
MATMUL_VMEM = 32 * 1024 * 1024
MESH = pl.DeviceIdType.MESH

NN = ((1,), (0,))
NT = ((1,), (1,))
TN = ((0,), (0,))


def _dot(a, b, dims, precision=None):
    if precision is None:
        a, b = a.astype(MXU), b.astype(MXU)
    return lax.dot_general(a, b, (dims, ((), ())), precision=precision, preferred_element_type=F32)


def _tile(n, pref):
    if n <= pref:
        return n
    best = None
    for t in range(128, pref + 1, 128):
        if n % t == 0:
            best = t
    assert best is not None, (n, pref)
    return best


def _params(sem):
    return pltpu.CompilerParams(dimension_semantics=sem, vmem_limit_bytes=VMEM_LIMIT)


def _matmul(a, b, mode, *, name, out=((F32),), epilogue=None, extras=(), after=(), tm=512, tn=1536):
    if mode == "nn":
        (M, K), (_, N) = a.shape, b.shape
    elif mode == "nt":
        (M, K), (N, _) = a.shape, b.shape
    else:
        (K, M), (_, N) = a.shape, b.shape
    tm, tn = _tile(M, tm), _tile(N, tn)

    def vmem(tm_, tn_):
        per = K * tm_ * a.dtype.itemsize + K * tn_ * b.dtype.itemsize
        per += sum(tm_ * tn_ * jnp.dtype(dt).itemsize for dt in out)
        per += sum((1 if e.shape[0] == 1 else tm_) * tn_ * e.dtype.itemsize for e in extras)
        return 2 * per

    while vmem(tm, tn) > MATMUL_VMEM and (tn > 128 or tm > 128):
        if tn >= tm and tn > 128:
            tn = _tile(N, tn - 128)
        else:
            tm = _tile(M, tm - 128)
    bytes_a, bytes_b = a.size * a.dtype.itemsize, b.size * b.dtype.itemsize
    rows_outer = bytes_a + (M // tm) * bytes_b <= (N // tn) * bytes_a + bytes_b
    grid = (M // tm, N // tn) if rows_outer else (N // tn, M // tm)

    def at(f):
        return (lambda i, j: f(i, j)) if rows_outer else (lambda j, i: f(i, j))

    a_spec = pl.BlockSpec((K, tm), at(lambda i, j: (0, i))) if mode == "tn" else pl.BlockSpec((tm, K), at(lambda i, j: (i, 0)))
    b_spec = pl.BlockSpec((tn, K), at(lambda i, j: (j, 0))) if mode == "nt" else pl.BlockSpec((K, tn), at(lambda i, j: (0, j)))
    dims = {"nn": NN, "nt": NT, "tn": TN}[mode]
    e_specs = []
    for e in extras:
        if e.shape[0] == 1:
            e_specs.append(pl.BlockSpec((1, tn), at(lambda i, j: (0, j))))
        else:
            assert e.shape == (M, N), (e.shape, M, N)
            e_specs.append(pl.BlockSpec((tm, tn), at(lambda i, j: (i, j))))
    n_e, n_o, n_a = len(extras), len(out), len(after)

    def body(a_ref, b_ref, *refs):
        acc = _dot(a_ref[...], b_ref[...], dims)
        res = (acc,) if epilogue is None else epilogue(acc, *[r[...] for r in refs[:n_e]])
        for r, v in zip(refs[n_e + n_a:], res):
            r[...] = v.astype(r.dtype)

    res = pl.pallas_call(
        body, name=name, grid=grid,
        in_specs=[a_spec, b_spec] + e_specs + [ANY] * n_a,
        out_specs=[pl.BlockSpec((tm, tn), at(lambda i, j: (i, j))) for _ in range(n_o)],
        out_shape=[jax.ShapeDtypeStruct((M, N), dt) for dt in out],
        compiler_params=_params(("parallel", "parallel")),
    )(a, b, *extras, *after)
    return res[0] if n_o == 1 else res


def _rowwise(fn, rows, fulls, outs, reds=(), *, name, after=(), tm=256):
    T = rows[0].shape[0]
    tm = min(tm, T)
    n_r, n_f, n_o, n_d, n_a = len(rows), len(fulls), len(outs), len(reds), len(after)

    def body(*refs):
        ins = [r[...] for r in refs[:n_r + n_f]]
        o_refs = refs[n_r + n_f + n_a:n_r + n_f + n_a + n_o]
        d_refs = refs[n_r + n_f + n_a + n_o:]
        res = fn(*ins)
        for r, v in zip(o_refs, res[:n_o]):
            r[...] = v.astype(r.dtype)
        if n_d:
            @pl.when(pl.program_id(0) == 0)
            def _():
                for r in d_refs:
                    r[...] = jnp.zeros_like(r)
            for r, v in zip(d_refs, res[n_o:]):
                r[...] += v.astype(r.dtype)

    res = pl.pallas_call(
        body, name=name, grid=(T // tm,),
        in_specs=[pl.BlockSpec((tm, r.shape[1]), lambda i: (i, 0)) for r in rows]
        + [pl.BlockSpec(f.shape, lambda i, nd=f.ndim: (0,) * nd) for f in fulls] + [ANY] * n_a,
        out_specs=[pl.BlockSpec((tm, c), lambda i: (i, 0)) for c, _ in outs]
        + [pl.BlockSpec(s, lambda i, nd=len(s): (0,) * nd) for s, _ in reds],
        out_shape=[jax.ShapeDtypeStruct((T, c), dt) for c, dt in outs]
        + [jax.ShapeDtypeStruct(s, dt) for s, dt in reds],
        compiler_params=_params(("arbitrary",)),
    )(*rows, *fulls, *after)
    return res


def _rms(x, g):
    return x * lax.rsqrt(jnp.mean(x * x, axis=-1, keepdims=True) + EPS) * g


def _colsum(v):
    return jnp.sum(v, axis=0, keepdims=True)


def _softplus(x):
    return jnp.maximum(x, 0.0) + jnp.log1p(jnp.exp(-jnp.abs(x)))


def _ffn_fwd(x, norm, wgT, wuT, wd, tag, on_gate=None):
    D = x.shape[1]
    (hn,) = _rowwise(lambda xt, g: (_rms(xt, g),), [x], [norm], [(D, BF16)], name=f"{tag}_norm")
    a = _matmul(hn, wgT, "nt", name=f"{tag}_gate", out=(BF16,))
    if on_gate is not None:
        on_gate(a)
    b, hid = _matmul(hn, wuT, "nt", name=f"{tag}_up", out=(BF16, BF16),
                     epilogue=lambda acc, at: (acc, jax.nn.silu(at.astype(F32)) * acc), extras=[a])
    y = _matmul(hid, wd, "nn", name=f"{tag}_down", epilogue=lambda acc, xt: (xt + 0.5 * acc,), extras=[x])
    return y, (hn, a, b, hid)


def _ffn_bwd(dy, dy16, x, norm, wgT, wuT, wd, res, tag, after, on_grads):
    hn, a, b, hid = res
    D, F = x.shape[1], a.shape[1]

    def act_bwd(acc, at, bt):
        _, vjp = jax.vjp(lambda p, q: jax.nn.silu(p) * q, at.astype(F32), bt.astype(F32))
        return vjp(0.5 * acc)

    da, db = _matmul(dy16, wd, "nt", name=f"{tag}_d_hid", out=(BF16, BF16), epilogue=act_bwd, extras=[a, b])
    d_wd = _matmul(hid, dy16, "tn", name=f"{tag}_d_wd", epilogue=lambda acc: (0.5 * acc,))
    d_wgT = _matmul(da, hn, "tn", name=f"{tag}_d_wg", after=after)
    d_wuT = _matmul(db, hn, "tn", name=f"{tag}_d_wu")
    started = on_grads(d_wgT, d_wuT, d_wd, dy)
    dhn = _matmul(da, wgT, "nn", name=f"{tag}_d_hn1")
    dhn = _matmul(db, wuT, "nn", name=f"{tag}_d_hn2", epilogue=lambda acc, e: (acc + e,), extras=[dhn])

    def norm_bwd(xt, dh, dyt, g):
        _, vjp = jax.vjp(_rms, xt, g)
        dx, dg = vjp(dh)
        return dyt + dx, dyt + dx, dg

    dx, dx16, d_norm = _rowwise(norm_bwd, [x, dhn, dy], [norm], [(D, F32), (D, BF16)], [((1, D), F32)],
                                name=f"{tag}_d_norm", after=started)
    return dx, dx16, d_norm


S5_TILES = 8
S5_HALF = 256


def _s5_derive(A_re, A_im, log_dt, B_re, B_im, C_re, C_im):
    G, N, M = S5_GROUPS, S5_STATE, S5_GROUP
    dt = jnp.exp(log_dt)[:, None]
    mag = jnp.exp(A_re * dt)
    ar, ai = mag * jnp.cos(A_im * dt), mag * jnp.sin(A_im * dt)
    den = A_re * A_re + A_im * A_im
    cr = ((ar - 1.0) * A_re + ai * A_im) / den
    ci = (ai * A_re - (ar - 1.0) * A_im) / den
    bbr = cr[..., None] * B_re - ci[..., None] * B_im
    bbi = cr[..., None] * B_im + ci[..., None] * B_re
    eye = jnp.eye(8, dtype=F32)

    def tile_in(bb):
        t = bb.reshape(S5_TILES, 8, N, M).transpose(0, 1, 3, 2)
        return jnp.einsum("jamn,ab->jambn", t, eye).reshape(S5_TILES, 8 * M, 8 * N)

    def tile_out(c):
        t = c.reshape(S5_TILES, 8, M, N).transpose(0, 1, 3, 2)
        return jnp.einsum("janm,ab->janbm", t, eye).reshape(S5_TILES, 8 * N, 8 * M)

    return (tile_in(bbr), tile_in(bbi), tile_out(C_re), tile_out(C_im),
            ar.reshape(S5_TILES, 1, 8 * N), ai.reshape(S5_TILES, 1, 8 * N))


S5_NB = S5_HALF // 128
S5_SEG = 32


def _cmul(ar, ai, br, bi):
    return ar * br - ai * bi, ar * bi + ai * br


def _s5_scan(sr_ref, si_ref, ar, ai, T, reverse):
    L, V = T // S5_SEG, S5_SEG // 8
    assert L * S5_SEG == T and L & (L - 1) == 0, T
    sg = -1.0 if reverse else 1.0
    a_r = [jnp.broadcast_to(ar[:, 128 * b:128 * (b + 1)], (8, 128)) for b in range(S5_NB)]
    a_i = [jnp.broadcast_to(sg * ai[:, 128 * b:128 * (b + 1)], (8, 128)) for b in range(S5_NB)]

    def rows(k, v):
        return pl.ds(pl.multiple_of(((L - 1 - k) if reverse else k) * S5_SEG + 8 * v, 8), 8)

    def local(k, carry):
        out = []
        for b in range(S5_NB):
            for v in range(V):
                idx = rows(k, v)
                mr, mi = _cmul(a_r[b], a_i[b], *carry[b * V + v])
                nr, ni = mr + sr_ref[b, idx, :], mi + si_ref[b, idx, :]
                sr_ref[b, idx, :] = nr
                si_ref[b, idx, :] = ni
                out.append((nr, ni))
        return tuple(out)

    z = jnp.zeros((8, 128), F32)
    ends = lax.fori_loop(0, L, local, tuple((z, z) for _ in range(S5_NB * V)))

    carries = []
    for b in range(S5_NB):
        pr, pi = a_r[b][0:1], a_i[b][0:1]
        n = L
        while n > 1:
            pr, pi = _cmul(pr, pi, pr, pi)
            n //= 2
        cr, ci = jnp.zeros((1, 128), F32), jnp.zeros((1, 128), F32)
        into = [None] * S5_SEG
        for j in (reversed(range(S5_SEG)) if reverse else range(S5_SEG)):
            into[j] = (cr, ci)
            er, ei = ends[b * V + j // 8]
            mr, mi = _cmul(pr, pi, cr, ci)
            cr, ci = mr + er[j % 8:j % 8 + 1], mi + ei[j % 8:j % 8 + 1]
        carries.append([(jnp.concatenate([into[8 * v + s][0] for s in range(8)], axis=0),
                         jnp.concatenate([into[8 * v + s][1] for s in range(8)], axis=0)) for v in range(V)])

    def fix(k, powers):
        out = []
        for b in range(S5_NB):
            pr, pi = powers[b]
            for v in range(V):
                idx = rows(k, v)
                dr, di = _cmul(pr, pi, *carries[b][v])
                sr_ref[b, idx, :] += dr
                si_ref[b, idx, :] += di
            out.append(_cmul(pr, pi, a_r[b], a_i[b]))
        return tuple(out)

    lax.fori_loop(0, L, fix, tuple((a_r[b], a_i[b]) for b in range(S5_NB)))


def _s5_in(ut, b_ref, s_ref):
    for b in range(S5_NB):
        s_ref[b] = _dot(ut, b_ref[:, 128 * b:128 * (b + 1)], NN)


def _to_seg(v):
    T, C = v.shape
    return v.reshape(S5_SEG, T // S5_SEG, C).transpose(1, 0, 2).reshape(T, C)


def _from_seg(v):
    T, C = v.shape
    return v.reshape(T // S5_SEG, S5_SEG, C).transpose(1, 0, 2).reshape(T, C)


def _s5_specs(T):
    u_spec = pl.BlockSpec((T, 128), lambda j, h: (0, j))
    b_spec = pl.BlockSpec((None, 128, S5_HALF), lambda j, h: (j, 0, h))
    c_spec = pl.BlockSpec((None, S5_HALF, 128), lambda j, h: (j, h, 0))
    a_spec = pl.BlockSpec((None, 1, S5_HALF), lambda j, h: (j, 0, h))
    return u_spec, b_spec, c_spec, a_spec


def _s5_fwd(u, tiles):
    T = u.shape[0]
    u_spec, b_spec, c_spec, a_spec = _s5_specs(T)

    def body(u_ref, br_ref, bi_ref, cr_ref, ci_ref, ar_ref, ai_ref, y_ref, sr_ref, si_ref):
        ut = u_ref[...].astype(MXU)
        _s5_in(ut, br_ref, sr_ref)
        _s5_in(ut, bi_ref, si_ref)
        _s5_scan(sr_ref, si_ref, ar_ref[...], ai_ref[...], T, False)
        y = None
        for b in range(S5_NB):
            blk = slice(128 * b, 128 * (b + 1))
            yb = _dot(sr_ref[b], cr_ref[blk, :], NN) - _dot(si_ref[b], ci_ref[blk, :], NN)
            y = yb if y is None else y + yb

        @pl.when(pl.program_id(1) == 0)
        def _():
            y_ref[...] = y

        @pl.when(pl.program_id(1) == 1)
        def _():
            y_ref[...] += y

    scr = pltpu.VMEM((S5_NB, T, 128), F32)
    return pl.pallas_call(
        body, name="s5_fwd", grid=(S5_TILES, 2),
        in_specs=[u_spec, b_spec, b_spec, c_spec, c_spec, a_spec, a_spec],
        out_specs=u_spec, out_shape=jax.ShapeDtypeStruct(u.shape, F32),
        scratch_shapes=[scr, scr],
        compiler_params=_params(("parallel", "arbitrary")),
    )(u, *tiles)


def _s5_bwd(u, dys, du_skip, tiles):
    T = u.shape[0]
    u_spec, b_spec, c_spec, a_spec = _s5_specs(T)

    def body(u_ref, dy_ref, sk_ref, br_ref, bi_ref, cr_ref, ci_ref, ar_ref, ai_ref,
             du_ref, dbr_ref, dbi_ref, dcr_ref, dci_ref, dar_ref, dai_ref, sr_ref, si_ref, lr_ref, li_ref):
        ut, dy = u_ref[...].astype(MXU), dy_ref[...].astype(MXU)
        ar, ai = ar_ref[...], ai_ref[...]
        _s5_in(ut, br_ref, sr_ref)
        _s5_in(ut, bi_ref, si_ref)
        _s5_scan(sr_ref, si_ref, ar, ai, T, False)
        for b in range(S5_NB):
            blk = slice(128 * b, 128 * (b + 1))
            dcr_ref[blk, :] = _dot(sr_ref[b], dy, TN)
            dci_ref[blk, :] = -_dot(si_ref[b], dy, TN)
            lr_ref[b] = _dot(dy, cr_ref[blk, :], NT)
            li_ref[b] = -_dot(dy, ci_ref[blk, :], NT)
        _s5_scan(lr_ref, li_ref, ar, ai, T, True)
        du = None
        head = lax.broadcasted_iota(jnp.int32, (T, 128), 0) < S5_SEG
        seg0 = lax.broadcasted_iota(jnp.int32, (S5_SEG, 128), 0) == 0
        for b in range(S5_NB):
            blk = slice(128 * b, 128 * (b + 1))
            lr, li = lr_ref[b], li_ref[b]
            pr = jnp.where(head, 0.0, pltpu.roll(sr_ref[b], S5_SEG, 0))
            pi = jnp.where(head, 0.0, pltpu.roll(si_ref[b], S5_SEG, 0))
            er = jnp.where(seg0, 0.0, pltpu.roll(sr_ref[b, T - S5_SEG:T, :], 1, 0))
            ei = jnp.where(seg0, 0.0, pltpu.roll(si_ref[b, T - S5_SEG:T, :], 1, 0))
            hr, hi = lr[0:S5_SEG], li[0:S5_SEG]
            dar_ref[:, blk] = _colsum(lr * pr + li * pi) + _colsum(hr * er + hi * ei)
            dai_ref[:, blk] = _colsum(li * pr - lr * pi) + _colsum(hi * er - hr * ei)
            dbr_ref[:, blk] = _dot(ut, lr, TN)
            dbi_ref[:, blk] = _dot(ut, li, TN)
            dub = _dot(lr, br_ref[:, blk], NT) + _dot(li, bi_ref[:, blk], NT)
            du = dub if du is None else du + dub

        @pl.when(pl.program_id(1) == 0)
        def _():
            du_ref[...] = du + sk_ref[...]

        @pl.when(pl.program_id(1) == 1)
        def _():
            du_ref[...] += du

    scr = pltpu.VMEM((S5_NB, T, 128), F32)
    return pl.pallas_call(
        body, name="s5_bwd", grid=(S5_TILES, 2),
        in_specs=[u_spec, u_spec, u_spec, b_spec, b_spec, c_spec, c_spec, a_spec, a_spec],
        out_specs=[u_spec, b_spec, b_spec, c_spec, c_spec, a_spec, a_spec],
        out_shape=[jax.ShapeDtypeStruct(u.shape, F32)] + [jax.ShapeDtypeStruct(t.shape, F32) for t in tiles],
        scratch_shapes=[scr, scr, scr, scr],
        compiler_params=_params(("parallel", "arbitrary")),
    )(u, dys, du_skip, *tiles)


CONV_TILE = 256


def _conv_pre(x, w, b):
    T = x.shape[0]
    row = lax.broadcasted_iota(jnp.int32, x.shape, 0)
    acc = x * w[CONV_K - 1:CONV_K, :] + b
    for lag in range(1, CONV_K):
        acc = acc + jnp.where(row >= lag, pltpu.roll(x, lag, 0), 0.0) * w[CONV_K - 1 - lag:CONV_K - lag, :]
    return acc


def _conv_fwd(x, w, b):
    T, C = x.shape
    col = pl.BlockSpec((T, CONV_TILE), lambda j: (0, j))

    def body(x_ref, w_ref, b_ref, o_ref):
        o_ref[...] = jax.nn.silu(_conv_pre(x_ref[...], w_ref[...], b_ref[...]))

    return pl.pallas_call(
        body, name="conv_fwd", grid=(C // CONV_TILE,),
        in_specs=[col, pl.BlockSpec((CONV_K, CONV_TILE), lambda j: (0, j)), pl.BlockSpec((1, CONV_TILE), lambda j: (0, j))],
        out_specs=col, out_shape=jax.ShapeDtypeStruct((T, C), F32),
        compiler_params=_params(("parallel",)),
    )(x, w, b)


def _conv_bwd(x, w, b, dout, col0, name):
    T, C = dout.shape
    off = col0 // CONV_TILE
    xcol = pl.BlockSpec((T, CONV_TILE), lambda j: (0, j + off))
    dcol = pl.BlockSpec((T, CONV_TILE), lambda j: (0, j))

    def body(x_ref, w_ref, b_ref, d_ref, dx_ref, dw_ref, db_ref):
        x, w = x_ref[...], w_ref[...]
        _, vjp = jax.vjp(jax.nn.silu, _conv_pre(x, w, b_ref[...]))
        (dy,) = vjp(d_ref[...])
        row = lax.broadcasted_iota(jnp.int32, x.shape, 0)
        dx = dy * w[CONV_K - 1:CONV_K, :]
        dw_ref[CONV_K - 1:CONV_K, :] = _colsum(dy * x)
        for lag in range(1, CONV_K):
            ahead = jnp.where(row < T - lag, pltpu.roll(dy, T - lag, 0), 0.0)
            dx = dx + ahead * w[CONV_K - 1 - lag:CONV_K - lag, :]
            dw_ref[CONV_K - 1 - lag:CONV_K - lag, :] = _colsum(ahead * x)
        dx_ref[...] = dx.astype(dx_ref.dtype)
        db_ref[...] = _colsum(dy)

    return pl.pallas_call(
        body, name=name, grid=(C // CONV_TILE,),
        in_specs=[xcol, pl.BlockSpec((CONV_K, CONV_TILE), lambda j: (0, j + off)),
                  pl.BlockSpec((1, CONV_TILE), lambda j: (0, j + off)), dcol],
        out_specs=[dcol, pl.BlockSpec((CONV_K, CONV_TILE), lambda j: (0, j)), pl.BlockSpec((1, CONV_TILE), lambda j: (0, j))],
        out_shape=[jax.ShapeDtypeStruct((T, C), BF16), jax.ShapeDtypeStruct((CONV_K, C), F32), jax.ShapeDtypeStruct((1, C), F32)],
        compiler_params=_params(("parallel",)),
    )(x, w, b, dout)


def _ssd_common(x_ref, b_ref, c_ref, dtr_ref, bias_ref, alog_ref, tri_ref, triu_ref):
    Q = x_ref.shape[0]
    x, Bm, Cm = x_ref[...], b_ref[...], c_ref[...]
    pre = dtr_ref[...] + bias_ref[...]
    dt = _softplus(pre)
    A = -jnp.exp(alog_ref[...])
    adt = dt * A
    ac4 = _dot(tri_ref[...], adt, NN, HI)
    ar4 = _dot(adt, triu_ref[...], TN, HI)
    cb = _dot(Cm, Bm, NT)
    ii = lax.broadcasted_iota(jnp.int32, (Q, Q), 0)
    jj = lax.broadcasted_iota(jnp.int32, (Q, Q), 1)
    atot4 = ac4[Q - 1:Q, :]
    return x, Bm, Cm, pre, dt, A, ac4, ar4, cb, ii >= jj, atot4


def _ssd_decay(k, ac4, ar4, causal):
    seg = ac4[:, k:k + 1] - ar4[k:k + 1, :]
    return jnp.where(causal, jnp.exp(jnp.where(causal, seg, 0.0)), 0.0)


def _per_head(c4, n):
    lane = lax.broadcasted_iota(jnp.int32, (c4.shape[0], n), 1)
    out = jnp.broadcast_to(c4[:, 0:1], (c4.shape[0], n))
    for k in range(1, SSD_HPG):
        out = jnp.where(lane >= SSD_HEADDIM * k, c4[:, k:k + 1], out)
    return out


def _head_sums(v):
    n = SSD_HPG * SSD_HEADDIM
    row = lax.broadcasted_iota(jnp.int32, (n, SSD_HPG), 0)
    col = lax.broadcasted_iota(jnp.int32, (n, SSD_HPG), 1)
    member = jnp.where((row >= SSD_HEADDIM * col) & (row < SSD_HEADDIM * (col + 1)), 1.0, 0.0)
    return _dot(v, member, NN, HI)


def _head_rows(c4):
    n = SSD_HPG * SSD_HEADDIM
    row = lax.broadcasted_iota(jnp.int32, (n, 1), 0)
    out = jnp.broadcast_to(c4[:, 0:1], (n, 1))
    for k in range(1, SSD_HPG):
        out = jnp.where(row >= SSD_HEADDIM * k, c4[:, k:k + 1], out)
    return out


def _ssd_specs(T, Q, rev):
    NC = T // Q
    cc = (lambda c: NC - 1 - c) if rev else (lambda c: c)
    W, N, S = SSD_HPG * SSD_HEADDIM, SSD_STATE, SSD_STEP_GROUPS
    x_spec = pl.BlockSpec((Q, S * W), lambda g, c: (cc(c), g))
    b_spec = pl.BlockSpec((Q, S * N), lambda g, c: (cc(c), D_INNER // (S * N) + g))
    c_spec = pl.BlockSpec((Q, S * N), lambda g, c: (cc(c), (D_INNER + SSD_GROUPS * N) // (S * N) + g))
    dt_spec = pl.BlockSpec((S, Q, SSD_HPG), lambda g, c: (g, cc(c), 0))
    p_spec = pl.BlockSpec((S, 1, SSD_HPG), lambda g, c: (g, 0, 0))
    tri_spec = pl.BlockSpec((Q, Q), lambda g, c: (0, 0))
    h_spec = pl.BlockSpec((S, None, W, N), lambda g, c: (g, cc(c), 0, 0))
    return x_spec, b_spec, c_spec, dt_spec, p_spec, tri_spec, h_spec


def _ssd_views(gi, wide, narrow, lead):
    W, N = SSD_HPG * SSD_HEADDIM, SSD_STATE
    return ([r.at[:, pl.ds(W * gi, W)] for r in wide] + [r.at[:, pl.ds(N * gi, N)] for r in narrow]
            + [r.at[gi] for r in lead])


def _tri(Q):
    tri = jnp.tril(jnp.ones((Q, Q), F32))
    return tri, tri.T


def _ssd_fwd(xc, dtr8, bias8, alog8, dsk8):
    T, Q = xc.shape[0], SSD_Q
    NC = T // Q
    P, W = SSD_HEADDIM, SSD_HPG * SSD_HEADDIM
    x_spec, b_spec, c_spec, dt_spec, p_spec, tri_spec, h_spec = _ssd_specs(T, Q, False)

    def body(x_ref, b_ref, c_ref, dtr_ref, bias_ref, alog_ref, dsk_ref, tri_ref, triu_ref, y_ref, hs_ref, h_scr):
        for gi in range(SSD_STEP_GROUPS):
            xv, yv, bv, cv, dtv, biasv, alogv, dskv, hsv, hv = _ssd_views(
                gi, (x_ref, y_ref), (b_ref, c_ref), (dtr_ref, bias_ref, alog_ref, dsk_ref, hs_ref, h_scr))
            one(xv, bv, cv, dtv, biasv, alogv, dskv, tri_ref, triu_ref, yv, hsv, hv)

    def one(x_ref, b_ref, c_ref, dtr_ref, bias_ref, alog_ref, dsk_ref, tri_ref, triu_ref, y_ref, hs_ref, h_scr):
        @pl.when(pl.program_id(1) == 0)
        def _():
            h_scr[...] = jnp.zeros_like(h_scr)

        hin = h_scr[...]
        hs_ref[...] = hin
        x, Bm, Cm, pre, dt, A, ac4, ar4, cb, causal, atot4 = _ssd_common(
            x_ref, b_ref, c_ref, dtr_ref, bias_ref, alog_ref, tri_ref, triu_ref)
        xdt = x * _per_head(dt, W)
        rest = _dot(Cm, hin, NT) * _per_head(jnp.exp(ac4), W) + _per_head(dsk_ref[...], W) * x
        for k in range(SSD_HPG):
            blk = slice(P * k, P * (k + 1))
            G = cb * _ssd_decay(k, ac4, ar4, causal)
            y_ref[:, blk] = _dot(G, xdt[:, blk], NN) + rest[:, blk]
        xw = xdt * _per_head(jnp.exp(atot4 - ac4), W)
        h_scr[...] = _head_rows(jnp.exp(atot4)) * hin + _dot(xw, Bm, TN)

    tri, triu = _tri(Q)
    return pl.pallas_call(
        body, name="ssd_fwd", grid=(SSD_GROUPS // SSD_STEP_GROUPS, NC),
        in_specs=[x_spec, b_spec, c_spec, dt_spec, p_spec, p_spec, p_spec, tri_spec, tri_spec],
        out_specs=[pl.BlockSpec((Q, SSD_STEP_GROUPS * W), lambda g, c: (c, g)), h_spec],
        out_shape=[jax.ShapeDtypeStruct((T, D_INNER), F32),
                   jax.ShapeDtypeStruct((SSD_GROUPS, NC, W, SSD_STATE), F32)],
        scratch_shapes=[pltpu.VMEM((SSD_STEP_GROUPS, W, SSD_STATE), F32)],
        compiler_params=_params(("parallel", "arbitrary")),
    )(xc, xc, xc, dtr8, bias8, alog8, dsk8, tri, triu)


def _ssd_bwd(xc, dtr8, bias8, alog8, dsk8, hs, dy):
    T, Q = xc.shape[0], SSD_Q
    NC = T // Q
    P, W = SSD_HEADDIM, SSD_HPG * SSD_HEADDIM
    x_spec, b_spec, c_spec, dt_spec, p_spec, tri_spec, h_spec = _ssd_specs(T, Q, True)
    dy_spec = pl.BlockSpec((Q, SSD_STEP_GROUPS * W), lambda g, c: (NC - 1 - c, g))
    dbc_spec = pl.BlockSpec((Q, SSD_STEP_GROUPS * SSD_STATE), lambda g, c: (NC - 1 - c, g))

    def body(x_ref, b_ref, c_ref, dtr_ref, bias_ref, alog_ref, dsk_ref, tri_ref, triu_ref, hs_ref, dy_ref,
             dx_ref, db_ref, dc_ref, ddtr_ref, dbias_ref, dalog_ref, ddsk_ref, dh_scr):
        for gi in range(SSD_STEP_GROUPS):
            (xv, dyv, dxv, bv, cv, dbv, dcv, dtv, biasv, alogv, dskv, hsv, ddtv, dbiasv, dalogv, ddskv, dhv) = _ssd_views(
                gi, (x_ref, dy_ref, dx_ref), (b_ref, c_ref, db_ref, dc_ref),
                (dtr_ref, bias_ref, alog_ref, dsk_ref, hs_ref, ddtr_ref, dbias_ref, dalog_ref, ddsk_ref, dh_scr))
            one(xv, bv, cv, dtv, biasv, alogv, dskv, tri_ref, triu_ref, hsv, dyv,
                dxv, dbv, dcv, ddtv, dbiasv, dalogv, ddskv, dhv)

    def one(x_ref, b_ref, c_ref, dtr_ref, bias_ref, alog_ref, dsk_ref, tri_ref, triu_ref, hs_ref, dy_ref,
            dx_ref, db_ref, dc_ref, ddtr_ref, dbias_ref, dalog_ref, ddsk_ref, dh_scr):
        @pl.when(pl.program_id(1) == 0)
        def _():
            dh_scr[...] = jnp.zeros_like(dh_scr)
            dbias_ref[...] = jnp.zeros_like(dbias_ref)
            dalog_ref[...] = jnp.zeros_like(dalog_ref)
            ddsk_ref[...] = jnp.zeros_like(ddsk_ref)

        x, Bm, Cm, pre, dt, A, ac4, ar4, cb, causal, atot4 = _ssd_common(
            x_ref, b_ref, c_ref, dtr_ref, bias_ref, alog_ref, tri_ref, triu_ref)
        dyv, hin, dho = dy_ref[...], hs_ref[...], dh_scr[...]
        dt_w = _per_head(dt, W)
        xdt = x * dt_w
        E4, F4, etot4 = jnp.exp(ac4), jnp.exp(atot4 - ac4), jnp.exp(atot4)
        F_w = _per_head(F4, W)
        ddsk_ref[...] += _head_sums(_colsum(dyv * x))
        Z = _dot(Cm, hin, NT)
        dZ = dyv * _per_head(E4, W)
        dac4 = _head_sums(dyv * Z) * E4
        dC = _dot(dZ, hin, NN)
        dh_scr[...] = _dot(dZ, Cm, TN) + _head_rows(etot4) * dho
        per_row = jnp.sum(dho * hin, axis=1, keepdims=True)
        lane4 = lax.broadcasted_iota(jnp.int32, (1, SSD_HPG), 1)
        datot4 = jnp.zeros((1, SSD_HPG), F32)
        for k in range(SSD_HPG):
            datot4 = jnp.where(lane4 == k, jnp.sum(per_row[P * k:P * (k + 1)], keepdims=True), datot4)
        datot4 = datot4 * etot4
        dxw = _dot(Bm, dho, NT)
        dB = _dot(xdt * F_w, dho, NN)
        dFa4 = _head_sums(dxw * xdt) * F4
        datot4 = datot4 + _colsum(dFa4)
        dac4 = dac4 - dFa4
        dcb = jnp.zeros((Q, Q), F32)
        lane_q4 = lax.broadcasted_iota(jnp.int32, (Q, SSD_HPG), 1)
        sub_4q = lax.broadcasted_iota(jnp.int32, (SSD_HPG, Q), 0)
        row_sums = jnp.zeros((Q, SSD_HPG), F32)
        col_sums = jnp.zeros((SSD_HPG, Q), F32)
        dxdt_heads = []
        for k in range(SSD_HPG):
            blk = slice(P * k, P * (k + 1))
            L = _ssd_decay(k, ac4, ar4, causal)
            G = cb * L
            dG = _dot(dyv[:, blk], xdt[:, blk], NT)
            dxdt_heads.append(_dot(G, dyv[:, blk], TN))
            dcb = dcb + dG * L
            Mseg = dG * G
            row_sums = jnp.where(lane_q4 == k, jnp.sum(Mseg, axis=1, keepdims=True), row_sums)
            col_sums = jnp.where(sub_4q == k, jnp.sum(Mseg, axis=0, keepdims=True), col_sums)
        dxdt = dxw * F_w + jnp.concatenate(dxdt_heads, axis=1)
        last = lax.broadcasted_iota(jnp.int32, (Q, SSD_HPG), 0) == Q - 1
        dac4 = dac4 + row_sums + jnp.where(last, datot4, 0.0)
        dadt4 = _dot(triu_ref[...], dac4, NN, HI) - _dot(triu_ref[...], col_sums, NT, HI)
        ddt4 = _head_sums(dxdt * x) + dadt4 * A
        dalog_ref[...] += _colsum(dadt4 * dt) * A
        ddtr4 = ddt4 * jax.nn.sigmoid(pre)
        dbias_ref[...] += _colsum(ddtr4)
        ddtr_ref[...] = ddtr4
        dx_ref[...] = (_per_head(dsk_ref[...], W) * dyv + dxdt * dt_w).astype(dx_ref.dtype)
        db_ref[...] = dB + _dot(dcb, Cm, TN)
        dc_ref[...] = dC + _dot(dcb, Bm, NN)

    tri, triu = _tri(Q)
    return pl.pallas_call(
        body, name="ssd_bwd", grid=(SSD_GROUPS // SSD_STEP_GROUPS, NC),
        in_specs=[x_spec, b_spec, c_spec, dt_spec, p_spec, p_spec, p_spec, tri_spec, tri_spec, h_spec, dy_spec],
        out_specs=[dy_spec, dbc_spec, dbc_spec, dt_spec, p_spec, p_spec, p_spec],
        out_shape=[jax.ShapeDtypeStruct((T, D_INNER), F32),
                   jax.ShapeDtypeStruct((T, SSD_GROUPS * SSD_STATE), F32),
                   jax.ShapeDtypeStruct((T, SSD_GROUPS * SSD_STATE), F32),
                   jax.ShapeDtypeStruct(dtr8.shape, F32)] + [jax.ShapeDtypeStruct(bias8.shape, F32)] * 3,
        scratch_shapes=[pltpu.VMEM((SSD_STEP_GROUPS, W, SSD_STATE), F32)],
        compiler_params=_params(("parallel", "arbitrary")),
    )(xc, xc, xc, dtr8, bias8, alog8, dsk8, tri, triu, hs, dy)


def _to_groups(v):
    return v.reshape(SSD_GROUPS, 1, SSD_HPG)


def _mixer_fwd(x, p, w):
    T, D = x.shape
    (hn,) = _rowwise(lambda xt, g: (_rms(xt, g),), [x], [p["mix_norm"]], [(D, BF16)], name="mix_norm")
    winT = w["w_in"]
    u, z, xbc, dtr, gl = [
        _matmul(hn, winT[IN_OFFS[i]:IN_OFFS[i + 1]], "nt", name=f"mix_in{i}") for i in range(5)]
    tiles = _s5_derive(*[p[k][0] for k in ("s5_A_re", "s5_A_im", "s5_log_dt", "s5_B_re", "s5_B_im", "s5_C_re", "s5_C_im")])
    u = _to_seg(u)
    ys = _s5_fwd(u, tiles)
    (g,) = _rowwise(lambda yt, ut, d: (jax.nn.gelu(yt + d * ut),), [ys, u], [p["s5_D"]], [(D, F32)], name="s5_gelu")
    y5 = _matmul(g, w["s5_w_glu"], "nn", name="s5_glu", out=(BF16,),
                 epilogue=lambda acc, gt, b: (gt * jax.nn.sigmoid(acc + b),), extras=[g, p["s5_b_glu"]])
    xc = _conv_fwd(xbc, w["conv_w"], p["conv_b"])
    dtr8 = dtr.reshape(T, SSD_GROUPS, SSD_HPG).transpose(1, 0, 2)
    ssd_p = [_to_groups(p[k]) for k in ("ssd_dt_bias", "ssd_A_log", "ssd_D")]
    yssd_raw, hs = _ssd_fwd(xc, dtr8, *ssd_p)
    (yssd,) = _rowwise(lambda yt, zt, nw: (_rms(yt * jax.nn.silu(zt), nw),), [yssd_raw, z], [p["ssd_norm"]],
                       [(D_INNER, BF16)], name="ssd_gate")
    p5 = _from_seg(_matmul(y5, w["w_proj_s5"], "nn", name="mix_p5"))
    pssd = _matmul(yssd, w["w_proj_ssd"], "nn", name="mix_pssd")

    def merge(glt, at, bt, bg):
        gates = jax.nn.sigmoid(glt + bg)
        return (gates[:, :D] * at + gates[:, D:] * bt,)

    (merged,) = _rowwise(merge, [gl, p5, pssd], [p["b_gate"]], [(D, BF16)], name="mix_merge")
    y = _matmul(merged, w["w_out"], "nn", name="mix_out", epilogue=lambda acc, xt: (xt + acc,), extras=[x])
    res = dict(hn=hn, u=u, z=z, xbc=xbc, gl=gl, tiles=tiles, ys=ys, g=g, y5=y5, xc=xc, dtr8=dtr8, ssd_p=ssd_p,
               yssd_raw=yssd_raw, hs=hs, yssd=yssd, p5=p5, pssd=pssd, merged=merged)
    return y, res


def _mixer_bwd(dy, dy16, x, p, w, r):
    T, D = x.shape
    gp = {}
    dmerged = _matmul(dy16, w["w_out"], "nt", name="mix_d_merged")
    gp["w_out"] = _matmul(r["merged"], dy16, "tn", name="mix_d_wout")

    def merge_bwd(glt, at, bt, dm, bg):
        def f(q, a_, b_):
            gates = jax.nn.sigmoid(q + bg)
            return gates[:, :D] * a_ + gates[:, D:] * b_
        _, vjp = jax.vjp(f, glt, at, bt)
        dq, da_, db_ = vjp(dm)
        return da_, db_, dq, _colsum(dq)

    dp5, dpssd, dgl, gp["b_gate"] = _rowwise(
        merge_bwd, [r["gl"], r["p5"], r["pssd"], dmerged], [p["b_gate"]],
        [(D, BF16), (D, BF16), (2 * D, BF16)], [((1, 2 * D), F32)], name="mix_d_merge")
    dyssd = _matmul(dpssd, w["w_proj_ssd"], "nt", name="mix_d_yssd")
    gp["w_proj_ssd"] = _matmul(r["yssd"], dpssd, "tn", name="mix_d_wpssd")

    def gate_bwd(yt, zt, dyt, nw):
        _, vjp = jax.vjp(lambda a_, b_, c_: _rms(a_ * jax.nn.silu(b_), c_), yt, zt, nw)
        return vjp(dyt)

    dyraw, dz, gp["ssd_norm"] = _rowwise(
        gate_bwd, [r["yssd_raw"], r["z"], dyssd], [p["ssd_norm"]],
        [(D_INNER, F32), (D_INNER, BF16)], [((1, D_INNER), F32)], name="ssd_d_gate")
    dxs, dBm, dCm, ddtr8, dbias8, dalog8, ddsk8 = _ssd_bwd(r["xc"], r["dtr8"], *r["ssd_p"], r["hs"], dyraw)
    gp["ssd_dt_bias"], gp["ssd_A_log"], gp["ssd_D"] = [v.reshape(1, SSD_HEADS) for v in (dbias8, dalog8, ddsk8)]
    ddtr = ddtr8.transpose(1, 0, 2).reshape(T, SSD_HEADS)
    conv = [_conv_bwd(r["xbc"], w["conv_w"], p["conv_b"], d, c0, f"conv_bwd{i}")
            for i, (d, c0) in enumerate(((dxs, 0), (dBm, D_INNER), (dCm, D_INNER + SSD_GROUPS * SSD_STATE)))]
    dxbc = [c[0] for c in conv]
    gp["conv_w"] = jnp.concatenate([c[1] for c in conv], axis=1)
    gp["conv_b"] = jnp.concatenate([c[2] for c in conv], axis=1)

    dp5 = _to_seg(dp5)
    dy5 = _matmul(dp5, w["w_proj_s5"], "nt", name="mix_d_y5")
    gp["w_proj_s5"] = _matmul(r["y5"], dp5, "tn", name="mix_d_wp5")
    g = r["g"]

    def glu_ep(acc, gt, b, dyt):
        _, vjp = jax.vjp(lambda g_, t_: g_ * jax.nn.sigmoid(t_ + b), gt, acc)
        return vjp(dyt)

    dg1, dt_ = _matmul(g, w["s5_w_glu"], "nn", name="s5_d_glu", out=(F32, BF16), epilogue=glu_ep,
                       extras=[g, p["s5_b_glu"], dy5])
    gp["s5_w_glu"] = _matmul(g, dt_, "tn", name="s5_d_wglu")
    dg = _matmul(dt_, w["s5_w_glu"], "nt", name="s5_d_g", epilogue=lambda acc, e: (acc + e,), extras=[dg1])

    def gelu_bwd(yt, ut, dgt, dtt, d):
        _, vjp = jax.vjp(lambda y_, d_: jax.nn.gelu(y_ + d_ * ut), yt, d)
        dys, dd = vjp(dgt)
        return dys, dys * d, dd, _colsum(dtt.astype(F32))

    dys, dusk, gp["s5_D"], gp["s5_b_glu"] = _rowwise(
        gelu_bwd, [r["ys"], r["u"], dg, dt_], [p["s5_D"]], [(D, F32), (D, F32)], [((1, D), F32), ((1, D), F32)],
        name="s5_d_gelu")
    du, *dtiles = _s5_bwd(r["u"], dys, dusk, r["tiles"])
    du = _from_seg(du)

    pieces = [du, dz, None, ddtr, dgl]
    winT = w["w_in"]
    hn = r["hn"]
    d_rows, dhn = [], None
    cols = [(du, 0, 1024), (dz, 1024, 3072), (dxbc[0], 3072, 5120), (dxbc[1], 5120, 6144), (dxbc[2], 6144, 7168),
            (ddtr, 7168, 7200), (dgl, 7200, 9248)]
    for i, (d, lo, hi) in enumerate(cols):
        d_rows.append(_matmul(d, hn, "tn", name=f"mix_d_win{i}"))
        if dhn is None:
            dhn = _matmul(d, winT[lo:hi], "nn", name=f"mix_d_hn{i}")
        else:
            dhn = _matmul(d, winT[lo:hi], "nn", name=f"mix_d_hn{i}", epilogue=lambda acc, e: (acc + e,), extras=[dhn])
    gp["w_in"] = jnp.concatenate(d_rows, axis=0)
    del pieces

    def norm_bwd(xt, dh, dyt, gn):
        _, vjp = jax.vjp(_rms, xt, gn)
        dx, dgn = vjp(dh)
        return dyt + dx, dyt + dx, dgn

    dx, dx16, gp["mix_norm"] = _rowwise(norm_bwd, [x, dhn, dy], [p["mix_norm"]], [(D, F32), (D, BF16)],
                                        [((1, D), F32)], name="mix_d_norm")
    return dx, dx16, gp, dtiles


def _s5_param_grads(p, dtiles):
    keys = ("s5_A_re", "s5_A_im", "s5_log_dt", "s5_B_re", "s5_B_im", "s5_C_re", "s5_C_im")
    _, vjp = jax.vjp(_s5_derive, *[p[k][0] for k in keys])
    return {k: v[None] for k, v in zip(keys, vjp(tuple(dtiles)))}


def _local_step(x, target, p, w, late_weights, exchange):
    T, D = x.shape
    w = dict(w)
    x1, r1 = _ffn_fwd(x, p["ffn1_norm"], w["ffn1_w_gate"], w["ffn1_w_up"], w["ffn1_w_down"], "ffn1",
                      on_gate=lambda a_: w.update(late_weights("mixer", a_)))
    x2, rm = _mixer_fwd(x1, p, w)
    w.update(late_weights("ffn2", rm["p5"]))
    x3, r2 = _ffn_fwd(x2, p["ffn2_norm"], w["ffn2_w_gate"], w["ffn2_w_up"], w["ffn2_w_down"], "ffn2")

    def head(xt, tt, g):
        def f(x_, g_):
            e = _rms(x_, g_) - tt
            return 0.5 * jnp.sum(jnp.mean(e * e, axis=-1))
        l, (dx_, dg_) = jax.value_and_grad(f, argnums=(0, 1))(xt, g)
        return dx_, dx_, l.reshape(1, 1), dg_

    dx3, dx3_16, loss, d_final = _rowwise(head, [x3, target], [p["final_norm"]], [(D, F32), (D, BF16)],
                                          [((1, 1), F32), ((1, D), F32)], name="loss_head")
    gp = {"final_norm": d_final}

    def ffn_exchange(tag):
        return lambda d_wgT, d_wuT, d_wd, after: exchange(
            tag, {f"{tag}_w_gate": d_wgT, f"{tag}_w_up": d_wuT, f"{tag}_w_down": d_wd}, after)

    dx2, dx2_16, gp["ffn2_norm"] = _ffn_bwd(
        dx3, dx3_16, x2, p["ffn2_norm"], w["ffn2_w_gate"], w["ffn2_w_up"], w["ffn2_w_down"], r2, "ffn2", (),
        ffn_exchange("ffn2"))
    dx1, dx1_16, gm, dtiles = _mixer_bwd(dx2, dx2_16, x1, p, w, rm)
    conv_w_grad = gm.pop("conv_w")
    started = exchange("mixer", {k: gm.pop(k) for k in ("w_out", "w_proj_s5", "w_proj_ssd", "s5_w_glu", "w_in")}, dx1)
    gp.update(gm)
    gp.update(_s5_param_grads(p, dtiles))
    dx0, _, gp["ffn1_norm"] = _ffn_bwd(
        dx1, dx1_16, x, p["ffn1_norm"], w["ffn1_w_gate"], w["ffn1_w_up"], w["ffn1_w_down"], r1, "ffn1", started,
        ffn_exchange("ffn1"))
    return loss, dx0, gp, conv_w_grad


ADAMW_BLOCK = 1 << 18


def _adamw(w, m, v, gs, *, name, emit_g):
    _, R, C = w.shape
    n_g = len(gs)
    n_o = 4 if emit_g else 3
    tm = _rtile(R, max(8, ADAMW_BLOCK // C))
    lead = pl.BlockSpec((None, tm, C), lambda i: (0, i, 0))

    def body(w_ref, m_ref, v_ref, *refs):
        g = refs[0][...]
        for r in refs[1:n_g]:
            g = g + r[...]
        m2 = ADAM_B1 * m_ref[...] + (1.0 - ADAM_B1) * g
        v2 = ADAM_B2 * v_ref[...] + (1.0 - ADAM_B2) * (g * g)
        m_hat = m2 / (1.0 - ADAM_B1 ** ADAM_STEP)
        v_hat = v2 / (1.0 - ADAM_B2 ** ADAM_STEP)
        delta = -ADAM_LR * (m_hat / (jnp.sqrt(v_hat) + ADAM_EPS) + ADAM_WD * w_ref[...])
        for r, val in zip(refs[n_g:], ((g,) if emit_g else ()) + (delta, m2, v2)):
            r[...] = val

    return pl.pallas_call(
        body, name=name, grid=(R // tm,),
        in_specs=[lead] * 3 + [pl.BlockSpec((tm, C), lambda i: (i, 0))] * n_g, out_specs=[lead] * n_o,
        out_shape=[jax.ShapeDtypeStruct((1, R, C), F32)] * n_o,
        compiler_params=_params(("parallel",)),
    )(w, m, v, *gs)


def _rtile(r, cap=512):
    if r <= cap:
        return r
    best = None
    for t in range(8, cap + 1, 8):
        if r % t == 0:
            best = t
    assert best is not None, r
    return best


ANY = pl.BlockSpec(memory_space=pl.ANY)


def _where_am_i():
    x, y, c = lax.axis_index("x"), lax.axis_index("y"), lax.axis_index("c")
    ks = (2 * x + y, 2 * (1 - x) + y, 2 * x + (1 - y), 2 * (1 - x) + (1 - y))
    return x, y, c, ks


def _rc(src, dst, ssem, rsem, to):
    return pltpu.make_async_remote_copy(src_ref=src, dst_ref=dst, send_sem=ssem, recv_sem=rsem,
                                        device_id=to, device_id_type=MESH)


def _gather_shards(srcs):
    n = len(srcs)

    def body(*refs):
        _gather_body(refs[:n], refs[n:2 * n], *refs[2 * n:], extra=())

    return pl.pallas_call(
        body, name="gather_shards",
        in_specs=[ANY] * n, out_specs=[ANY] * n,
        out_shape=[jax.ShapeDtypeStruct((4,) + s.shape, s.dtype) for s in srcs],
        scratch_shapes=[pltpu.SemaphoreType.DMA((n, 6)), pltpu.SemaphoreType.DMA((n, 6)), pltpu.SemaphoreType.DMA((n,))],
        compiler_params=pltpu.CompilerParams(has_side_effects=True),
    )(*srcs)


def _handshake_all():
    x, y, c = lax.axis_index("x"), lax.axis_index("y"), lax.axis_index("c")
    barrier = pltpu.get_barrier_semaphore()
    for dx, dy, dc in itertools.product((0, 1), repeat=3):
        if (dx, dy, dc) != (0, 0, 0):
            to = (1 - x if dx else x, 1 - y if dy else y, 1 - c if dc else c)
            pl.semaphore_signal(barrier, inc=1, device_id=to, device_id_type=MESH)
    pl.semaphore_wait(barrier, 7)


SEQUENCER = dict(axis_name="seq", num_cores=1)


def _gather_shards_async(srcs, token, name, collective_id):
    n = len(srcs)

    def body(*refs):
        tok, tok_out = refs[n], refs[2 * n + 1]
        ssem, rsem, lsem = refs[2 * n + 2:]
        _handshake_all()
        _gather_body(refs[:n], refs[n + 1:2 * n + 1], ssem, rsem, lsem,
                     extra=(pltpu.make_async_copy(tok, tok_out, lsem.at[n]),))

    res = pl.kernel(
        body, name=name,
        out_type=[jax.ShapeDtypeStruct((4,) + s.shape, s.dtype) for s in srcs]
        + [jax.ShapeDtypeStruct(token.shape, token.dtype)],
        mesh=plsc.ScalarSubcoreMesh(**SEQUENCER),
        scratch_types=[pltpu.SemaphoreType.DMA((n, 6)), pltpu.SemaphoreType.DMA((n, 6)), pltpu.SemaphoreType.DMA((n + 1,))],
        compiler_params=pltpu.CompilerParams(collective_id=collective_id),
    )(*srcs, token)
    return res[:n], res[n]


def _gather_body(src, out, ssem, rsem, lsem, extra):
        n = len(src)
        x, y, c, (k_me, k_x, k_y, k_d) = _where_am_i()
        sib = (x, y, 1 - c)
        local = [pltpu.make_async_copy(src[i], out[i].at[k_me], lsem.at[i]) for i in range(n)] + list(extra)
        for cp in local:
            cp.start()

        def own(i, q, to):
            return _rc(src[i], out[i].at[k_me], ssem.at[i, q], rsem.at[i, q], to)

        def slab(i, q, k, to):
            return _rc(out[i].at[k], out[i].at[k], ssem.at[i, q], rsem.at[i, q], to)

        @pl.when(c == 1)
        def _():
            sends = [own(i, 0, (1 - x, y, 1)) for i in range(n)]
            for cp in sends:
                cp.start()
            fwd = [slab(i, 3, k_x, sib) for i in range(n)]
            for i in range(n):
                slab(i, 0, k_x, sib).wait_recv()
                fwd[i].start()
            for i in range(n):
                slab(i, 4, k_y, sib).wait_recv()
                slab(i, 5, k_d, sib).wait_recv()
            for cp in sends + fwd:
                cp.wait_send()

        @pl.when(c == 0)
        def _():
            sends = [own(i, 1, (x, 1 - y, 0)) for i in range(n)] + [own(i, 2, (1 - x, 1 - y, 0)) for i in range(n)]
            for cp in sends:
                cp.start()
            fwd = [slab(i, 4, k_y, sib) for i in range(n)] + [slab(i, 5, k_d, sib) for i in range(n)]
            for i in range(n):
                slab(i, 1, k_y, sib).wait_recv()
                fwd[i].start()
            for i in range(n):
                slab(i, 2, k_d, sib).wait_recv()
                fwd[n + i].start()
            for i in range(n):
                slab(i, 3, k_x, sib).wait_recv()
            for cp in sends + fwd:
                cp.wait_send()

        for cp in local:
            cp.wait()


def _swap_slabs(arrs, n_slabs, name, async_id, token):
    n = len(arrs)
    J = max(n_slabs, 1)

    def body(*refs):
        src, out = refs[:n], refs[n:2 * n]
        ssem, rsem = refs[2 * n:]
        x, y, c, (k_me, k_x, k_y, k_d) = _where_am_i()
        sib = (x, y, 1 - c)
        sel = (jnp.where(c == 1, k_y, k_me), jnp.where(c == 1, k_d, k_x))
        cps = []
        for i in range(n):
            for j in range(J):
                s = src[i].at[sel[j]] if n_slabs else src[i]
                d = out[i].at[j] if n_slabs else out[i]
                cps.append(_rc(s, d, ssem.at[i, j], rsem.at[i, j], sib))
        for cp in cps:
            cp.start()
        for cp in cps:
            cp.wait()

    return _exchange_call(
        body, arrs, [jax.ShapeDtypeStruct(((n_slabs,) if n_slabs else ()) + a.shape[-2:], a.dtype) for a in arrs],
        [pltpu.SemaphoreType.DMA((n, J)), pltpu.SemaphoreType.DMA((n, J))], name, async_id, token)


def _exchange_call(body, arrs, out_shape, sems, name, collective_id, token):
    n, m = len(arrs), len(out_shape)

    def seq_body(*refs):
        tok, tok_out, tok_sem = refs[n], refs[n + 1 + m], refs[-1]
        _handshake_all()
        cp = pltpu.make_async_copy(tok, tok_out, tok_sem)
        cp.start()
        body(*refs[:n], *refs[n + 1:n + 1 + m], *refs[n + 2 + m:-1])
        cp.wait()

    res = pl.kernel(
        seq_body, name=name, out_type=list(out_shape) + [jax.ShapeDtypeStruct(token.shape, token.dtype)],
        mesh=plsc.ScalarSubcoreMesh(**SEQUENCER), scratch_types=list(sems) + [pltpu.SemaphoreType.DMA],
        compiler_params=pltpu.CompilerParams(collective_id=collective_id))(*arrs, token)
    return res[:m], res[m]


def _send_chip_sums(arrs, name, async_id, token):
    n = len(arrs)

    def body(*refs):
        src, out = refs[:n], refs[n:2 * n]
        ssem, rsem = refs[2 * n:]
        x, y, c, _ = _where_am_i()

        @pl.when(c == 1)
        def _():
            cps = [_rc(src[i].at[1], out[i].at[0], ssem.at[i, 0], rsem.at[i, 0], (1 - x, y, 1)) for i in range(n)]
            for cp in cps:
                cp.start()
            for cp in cps:
                cp.wait()

        @pl.when(c == 0)
        def _():
            cps = [_rc(src[i].at[0], out[i].at[0], ssem.at[i, 0], rsem.at[i, 0], (x, 1 - y, 0)) for i in range(n)]
            cps += [_rc(src[i].at[1], out[i].at[1], ssem.at[i, 1], rsem.at[i, 1], (1 - x, 1 - y, 0)) for i in range(n)]
            for cp in cps:
                cp.start()
            for cp in cps:
                cp.wait()

    return _exchange_call(
        body, arrs, [jax.ShapeDtypeStruct(a.shape, a.dtype) for a in arrs],
        [pltpu.SemaphoreType.DMA((n, 2)), pltpu.SemaphoreType.DMA((n, 2))], name, async_id, token)


def _chip_sum(g, recv, sel, name):
    _, r, C = g.shape
    tr = _rtile(r)

    def body(sel_ref, g_ref, r_ref, o32_ref, o16_ref):
        s = g_ref[...] + r_ref[...]
        o32_ref[...] = s
        o16_ref[...] = s.astype(BF16)

    blk = pl.BlockSpec((None, tr, C), lambda j, t, sel_ref: (j, t, 0))
    return pl.pallas_call(
        body, name=name,
        grid_spec=pltpu.PrefetchScalarGridSpec(
            num_scalar_prefetch=1, grid=(2, r // tr),
            in_specs=[pl.BlockSpec((None, tr, C), lambda j, t, sel_ref: (sel_ref[j], t, 0)), blk],
            out_specs=[blk, blk]),
        out_shape=[jax.ShapeDtypeStruct((2, r, C), F32), jax.ShapeDtypeStruct((2, r, C), BF16)],
        compiler_params=_params(("arbitrary", "arbitrary")),
    )(sel, g, recv)


def _cross_sum(s32, got, north, after, name):
    _, r, C = s32.shape
    tr = _rtile(r)

    def body(f_ref, p_ref, a_ref, b_ref, after_ref, o_ref):
        a = a_ref[...].astype(F32)

        @pl.when(f_ref[0] == 1)
        def _():
            o_ref[...] = p_ref[...] + a

        @pl.when(f_ref[0] == 0)
        def _():
            o_ref[...] = a + b_ref[...].astype(F32)

    return pl.pallas_call(
        body, name=name,
        grid_spec=pltpu.PrefetchScalarGridSpec(
            num_scalar_prefetch=1, grid=(r // tr,),
            in_specs=[pl.BlockSpec((None, tr, C), lambda t, f: (0, t, 0)),
                      pl.BlockSpec((None, tr, C), lambda t, f: (0, t, 0)),
                      pl.BlockSpec((None, tr, C), lambda t, f: (1 - f[0], t, 0)), ANY],
            out_specs=pl.BlockSpec((tr, C), lambda t, f: (t, 0))),
        out_shape=jax.ShapeDtypeStruct((r, C), F32),
        compiler_params=_params(("arbitrary",)),
    )(north, s32, got, got, after)


class _Reducer:
    def __init__(self, token):
        self.token, self.pending, self.pairs = token, None, {}

    def begin(self, tag, ids, grads, after):
        names = list(grads)
        arrs = [grads[n].reshape(4, -1, D_MODEL) for n in names]
        x, y, c, (k_me, k_x, k_y, k_d) = _where_am_i()
        sel = jnp.stack([jnp.where(c == 1, k_me, k_y), jnp.where(c == 1, k_x, k_d)]).astype(jnp.int32)
        from_sib, self.token = _swap_slabs(arrs, 2, f"swap_grad_slabs_{tag}", ids[0], self.token)
        sums = [_chip_sum(g, r, sel, f"chip_sum_{tag}{i}") for i, (g, r) in enumerate(zip(arrs, from_sib))]
        self.finish(after)
        got, self.token = _send_chip_sums([s16 for _, s16 in sums], f"send_chip_sums_{tag}", ids[1], self.token)
        self.pending = (tag, ids[2], names, sums, got)
        return tuple(s16 for _, s16 in sums)

    def finish(self, after):
        if self.pending is None:
            return
        tag, cid, names, sums, got = self.pending
        north = jnp.reshape(lax.axis_index("c"), (1,)).astype(jnp.int32)
        parts = [_cross_sum(s32, b, north, after, f"cross_sum_{tag}{i}") for i, ((s32, _), b) in enumerate(zip(sums, got))]
        theirs, self.token = _swap_slabs(parts, 0, f"swap_parts_{tag}", cid, self.token)
        self.pairs.update(zip(names, zip(parts, theirs)))
        self.pending = None


def _all_reduce_small(v, token, collective_id):
    R, C = v.shape
    tr = R

    def body(v_ref, buf, ssem, rsem, lsem):
        x, y, c = lax.axis_index("x"), lax.axis_index("y"), lax.axis_index("c")
        me, sib = (x, y, c), (x, y, 1 - c)
        chips = [(1 - x, y), (x, 1 - y), (1 - x, 1 - y)]

        def slot(px, py, pc):
            return buf.at[4 * px + 2 * py + pc]

        def copy(k, block, to, src=None):
            return _rc(slot(*block) if src is None else src, slot(*block), ssem.at[k], rsem.at[k], to)

        mine = pltpu.make_async_copy(v_ref, slot(*me), lsem)
        mine.start()
        first = [copy(0, me, sib, src=v_ref)] + [copy(1 + j, me, (*chip, c), src=v_ref) for j, chip in enumerate(chips)]
        for cp in first:
            cp.start()
        passed = [copy(4 + j, (*chip, c), sib) for j, chip in enumerate(chips)]
        for j, chip in enumerate(chips):
            copy(1 + j, (*chip, c), me).wait_recv()
            passed[j].start()
        copy(0, sib, me).wait_recv()
        for j, chip in enumerate(chips):
            copy(4 + j, (*chip, 1 - c), me).wait_recv()
        for cp in first + passed:
            cp.wait_send()
        mine.wait()

    (gathered,), token = _exchange_call(
        body, [v], [jax.ShapeDtypeStruct((8, R, C), F32)],
        [pltpu.SemaphoreType.DMA((7,)), pltpu.SemaphoreType.DMA((7,)), pltpu.SemaphoreType.DMA],
        "gather_small", collective_id, token)

    def add(g_ref, o_ref):
        acc = g_ref[0]
        for k in range(1, 8):
            acc = acc + g_ref[k]
        o_ref[...] = acc

    total = pl.pallas_call(
        add, name="sum_small", grid=(R // tr,),
        in_specs=[pl.BlockSpec((8, tr, C), lambda i: (0, i, 0))], out_specs=pl.BlockSpec((tr, C), lambda i: (i, 0)),
        out_shape=jax.ShapeDtypeStruct((R, C), F32), compiler_params=_params(("parallel",)),
    )(gathered)
    return total, token


WEIGHTS = ['ffn1_norm', 'ffn1_w_gate', 'ffn1_w_up', 'ffn1_w_down', 'mix_norm', 'w_in', 'conv_w', 'conv_b', 's5_A_re',
           's5_A_im', 's5_log_dt', 's5_B_re', 's5_B_im', 's5_C_re', 's5_C_im', 's5_D', 's5_w_glu', 's5_b_glu',
           'ssd_A_log', 'ssd_dt_bias', 'ssd_D', 'ssd_norm', 'w_proj_s5', 'w_proj_ssd', 'b_gate', 'w_out', 'ffn2_norm',
           'ffn2_w_gate', 'ffn2_w_up', 'ffn2_w_down', 'final_norm']
ARGS = ['x'] + WEIGHTS + ['loss_target'] + ['m_' + n for n in WEIGHTS] + ['v_' + n for n in WEIGHTS]
COL_SHARDED = ('ffn1_w_gate', 'ffn1_w_up', 'w_in', 'ffn2_w_gate', 'ffn2_w_up')
ROW_SHARDED = ('ffn1_w_down', 's5_w_glu', 'w_proj_s5', 'w_proj_ssd', 'w_out', 'ffn2_w_down')
MATRICES = COL_SHARDED + ROW_SHARDED
GATHER_FIRST = ('ffn1_w_gate', 'ffn1_w_up')
GATHER_FIRST_LATER = ('ffn1_w_down',)
GATHER_SECOND = ('w_in', 'conv_w', 's5_w_glu', 'w_proj_s5', 'w_proj_ssd', 'w_out')
GATHER_THIRD = ('ffn2_w_gate', 'ffn2_w_up', 'ffn2_w_down')
SMALL = [n for n in WEIGHTS if n not in MATRICES]


def _pack(arrs, width=1024):
    flat = jnp.concatenate([a.reshape(-1).astype(F32) for a in arrs])
    rows = -(-flat.shape[0] // (8 * width)) * 8
    return jnp.pad(flat, (0, rows * width - flat.shape[0])).reshape(rows, width)


def _unpack(packed, shapes):
    flat, out, o = packed.reshape(-1), [], 0
    for s in shapes:
        n = math.prod(s)
        out.append(flat[o:o + n].reshape(s))
        o += n
    return out


def kernel(x, ffn1_norm, ffn1_w_gate, ffn1_w_up, ffn1_w_down, mix_norm, w_in, conv_w, conv_b, s5_A_re, s5_A_im, s5_log_dt, s5_B_re, s5_B_im, s5_C_re, s5_C_im, s5_D, s5_w_glu, s5_b_glu, ssd_A_log, ssd_dt_bias, ssd_D, ssd_norm, w_proj_s5, w_proj_ssd, b_gate, w_out, ffn2_norm, ffn2_w_gate, ffn2_w_up, ffn2_w_down, final_norm, loss_target, m_ffn1_norm, m_ffn1_w_gate, m_ffn1_w_up, m_ffn1_w_down, m_mix_norm, m_w_in, m_conv_w, m_conv_b, m_s5_A_re, m_s5_A_im, m_s5_log_dt, m_s5_B_re, m_s5_B_im, m_s5_C_re, m_s5_C_im, m_s5_D, m_s5_w_glu, m_s5_b_glu, m_ssd_A_log, m_ssd_dt_bias, m_ssd_D, m_ssd_norm, m_w_proj_s5, m_w_proj_ssd, m_b_gate, m_w_out, m_ffn2_norm, m_ffn2_w_gate, m_ffn2_w_up, m_ffn2_w_down, m_final_norm, v_ffn1_norm, v_ffn1_w_gate, v_ffn1_w_up, v_ffn1_w_down, v_mix_norm, v_w_in, v_conv_w, v_conv_b, v_s5_A_re, v_s5_A_im, v_s5_log_dt, v_s5_B_re, v_s5_B_im, v_s5_C_re, v_s5_C_im, v_s5_D, v_s5_w_glu, v_s5_b_glu, v_ssd_A_log, v_ssd_dt_bias, v_ssd_D, v_ssd_norm, v_w_proj_s5, v_w_proj_ssd, v_b_gate, v_w_out, v_ffn2_norm, v_ffn2_w_gate, v_ffn2_w_up, v_ffn2_w_down, v_final_norm):
    a = dict(locals())
    assert list(a) == ARGS
    x, target = a['x'][0], a['loss_target'][0]
    k_me = 2 * lax.axis_index("x") + lax.axis_index("y")

    src = {n: a[n][0].T.astype(BF16) for n in COL_SHARDED}
    src.update({n: a[n][0].astype(BF16) for n in ROW_SHARDED})
    src['conv_w'] = a['conv_w'][0]
    def flat(f):
        return f.reshape(-1, f.shape[-1])

    full = _gather_shards([src[n] for n in GATHER_FIRST])
    w = {n: flat(f) for n, f in zip(GATHER_FIRST, full)}
    got, token = _gather_shards_async(
        [src[n] for n in GATHER_FIRST_LATER], full[0][0, :8, :128].astype(F32), "gather_ffn1_down", 14)
    w.update({n: flat(f) for n, f in zip(GATHER_FIRST_LATER, got)})
    reducer = _Reducer(token)

    def gather_later(stage, after):
        token = reducer.token + after[:8, :128].astype(F32)
        calls = {"mixer": ((GATHER_SECOND[:2], "gather_mixer_in", 1), (GATHER_SECOND[2:], "gather_mixer", 12)),
                 "ffn2": ((GATHER_THIRD, "gather_ffn2", 2),)}[stage]
        out = {}
        for names, name, cid in calls:
            got, token = _gather_shards_async([src[n] for n in names], token, name, cid)
            out.update({n: flat(f) for n, f in zip(names, got)})
        reducer.token = token
        if 'conv_w' in out:
            out['conv_w'] = out['conv_w'].reshape(4, CONV_K, -1).transpose(1, 0, 2).reshape(CONV_K, CONV_DIM)
        return out

    p = {n: a[n] for n in SMALL if n != 'conv_w'}
    p['final_norm'] = a['final_norm'][None]
    ids = {"ffn2": (3, 4, 5), "mixer": (6, 7, 8), "ffn1": (9, 10, 11)}
    loss, grad_x, gp, conv_w_grad = _local_step(
        x, target, p, w, gather_later, lambda tag, grads, after: reducer.begin(tag, ids[tag], grads, after))
    gp['conv_w'] = conv_w_grad[None]
    small_shapes = [(1, 1)] + [gp[n].shape if n != 'final_norm' else (1, D_MODEL) for n in SMALL]
    summed, reducer.token = _all_reduce_small(_pack([loss] + [gp[n] for n in SMALL]), reducer.token, 13)
    red = _unpack(summed, small_shapes)
    loss_all = red[0].reshape(())
    gsmall = dict(zip(SMALL, red[1:]))
    gsmall['conv_w'] = lax.dynamic_slice_in_dim(gsmall['conv_w'], k_me * 1024, 1024, axis=2)
    gsmall = {n: g.reshape(a[n].shape) for n, g in gsmall.items()}

    grads, delta, new_m, new_v = {}, {}, {}, {}

    def update(names):
        for n in names:
            turn = (lambda t: jnp.swapaxes(t, 1, 2)) if n in COL_SHARDED else (lambda t: t)
            out = _adamw(turn(a[n]), turn(a['m_' + n]), turn(a['v_' + n]), list(reducer.pairs[n]),
                         name=f"adamw_{n}", emit_g=True)
            grads[n], delta[n], new_m[n], new_v[n] = [turn(t) for t in out]

    update([n for n in MATRICES if n not in GATHER_FIRST + GATHER_FIRST_LATER])
    reducer.finish(delta['w_in'])
    update(GATHER_FIRST + GATHER_FIRST_LATER)
    sw, sm, sv, sg = [_pack([t[n] for n in SMALL])[None] for t in (a, {n: a['m_' + n] for n in SMALL},
                                                                   {n: a['v_' + n] for n in SMALL}, gsmall)]
    d, m2, v2 = _adamw(sw, sm, sv, [sg[0]], name="adamw_small", emit_g=False)
    shapes = [a[n].shape for n in SMALL]
    for n, dd, mm, vv in zip(SMALL, _unpack(d[0], shapes), _unpack(m2[0], shapes), _unpack(v2[0], shapes)):
        grads[n], delta[n], new_m[n], new_v[n] = gsmall[n], dd, mm, vv
    return (loss_all, grad_x[None], *[grads[n] for n in WEIGHTS], *[delta[n] for n in WEIGHTS],
            *[new_m[n] for n in WEIGHTS], *[new_v[n] for n in WEIGHTS])
```

```python
import itertools
import math

import jax
import jax.numpy as jnp
from jax import lax
from jax.experimental import pallas as pl
from jax.experimental.pallas import tpu as pltpu
from jax.experimental.pallas import tpu_sc as plsc

F32 = jnp.float32
BF16 = jnp.bfloat16
MXU = BF16
HI = lax.Precision.HIGHEST

D_MODEL = 1024
D_FF = 2816
EPS = 1e-6
S5_GROUPS, S5_GROUP, S5_STATE = 64, 16, 64
D_INNER = 2048
SSD_HEADDIM, SSD_HEADS, SSD_GROUPS, SSD_HPG, SSD_STATE = 64, 32, 8, 4, 128
CONV_K, CONV_DIM = 4, 4096
IN_OFFS = (0, 1024, 3072, 7168, 7200, 9248)
SSD_Q = 256
SSD_STEP_GROUPS = 2
ADAM_LR, ADAM_B1, ADAM_B2, ADAM_EPS, ADAM_WD, ADAM_STEP = 0.001, 0.9, 0.999, 1e-08, 0.01, 10

VMEM_LIMIT = 56 * 1024 * 1024
MATMUL_VMEM = 32 * 1024 * 1024
MESH = pl.DeviceIdType.MESH

NN = ((1,), (0,))
NT = ((1,), (1,))
TN = ((0,), (0,))


def _dot(a, b, dims, precision=None):
    if precision is None:
        a, b = a.astype(MXU), b.astype(MXU)
    return lax.dot_general(a, b, (dims, ((), ())), precision=precision, preferred_element_type=F32)


def _tile(n, pref):
    if n <= pref:
        return n
    best = None
    for t in range(128, pref + 1, 128):
        if n % t == 0:
            best = t
    assert best is not None, (n, pref)
    return best


def _params(sem):
    return pltpu.CompilerParams(dimension_semantics=sem, vmem_limit_bytes=VMEM_LIMIT)


def _matmul(a, b, mode, *, name, out=((F32),), epilogue=None, extras=(), after=(), tm=512, tn=1536):
    if mode == "nn":
        (M, K), (_, N) = a.shape, b.shape
    elif mode == "nt":
        (M, K), (N, _) = a.shape, b.shape
    else:
        (K, M), (_, N) = a.shape, b.shape
    tm, tn = _tile(M, tm), _tile(N, tn)

    def vmem(tm_, tn_):
        per = K * tm_ * a.dtype.itemsize + K * tn_ * b.dtype.itemsize
        per += sum(tm_ * tn_ * jnp.dtype(dt).itemsize for dt in out)
        per += sum((1 if e.shape[0] == 1 else tm_) * tn_ * e.dtype.itemsize for e in extras)
        return 2 * per

    while vmem(tm, tn) > MATMUL_VMEM and (tn > 128 or tm > 128):
        if tn >= tm and tn > 128:
            tn = _tile(N, tn - 128)
        else:
            tm = _tile(M, tm - 128)
    bytes_a, bytes_b = a.size * a.dtype.itemsize, b.size * b.dtype.itemsize
    rows_outer = bytes_a + (M // tm) * bytes_b <= (N // tn) * bytes_a + bytes_b
    grid = (M // tm, N // tn) if rows_outer else (N // tn, M // tm)

    def at(f):
        return (lambda i, j: f(i, j)) if rows_outer else (lambda j, i: f(i, j))

    a_spec = pl.BlockSpec((K, tm), at(lambda i, j: (0, i))) if mode == "tn" else pl.BlockSpec((tm, K), at(lambda i, j: (i, 0)))
    b_spec = pl.BlockSpec((tn, K), at(lambda i, j: (j, 0))) if mode == "nt" else pl.BlockSpec((K, tn), at(lambda i, j: (0, j)))
    dims = {"nn": NN, "nt": NT, "tn": TN}[mode]
    e_specs = []
    for e in extras:
        if e.shape[0] == 1:
            e_specs.append(pl.BlockSpec((1, tn), at(lambda i, j: (0, j))))
        else:
            assert e.shape == (M, N), (e.shape, M, N)
            e_specs.append(pl.BlockSpec((tm, tn), at(lambda i, j: (i, j))))
    n_e, n_o, n_a = len(extras), len(out), len(after)

    def body(a_ref, b_ref, *refs):
        acc = _dot(a_ref[...], b_ref[...], dims)
        res = (acc,) if epilogue is None else epilogue(acc, *[r[...] for r in refs[:n_e]])
        for r, v in zip(refs[n_e + n_a:], res):
            r[...] = v.astype(r.dtype)

    res = pl.pallas_call(
        body, name=name, grid=grid,
        in_specs=[a_spec, b_spec] + e_specs + [ANY] * n_a,
        out_specs=[pl.BlockSpec((tm, tn), at(lambda i, j: (i, j))) for _ in range(n_o)],
        out_shape=[jax.ShapeDtypeStruct((M, N), dt) for dt in out],
        compiler_params=_params(("parallel", "parallel")),
    )(a, b, *extras, *after)
    return res[0] if n_o == 1 else res


def _rowwise(fn, rows, fulls, outs, reds=(), *, name, after=(), tm=256):
    T = rows[0].shape[0]
    tm = min(tm, T)
    n_r, n_f, n_o, n_d, n_a = len(rows), len(fulls), len(outs), len(reds), len(after)

    def body(*refs):
        ins = [r[...] for r in refs[:n_r + n_f]]
        o_refs = refs[n_r + n_f + n_a:n_r + n_f + n_a + n_o]
        d_refs = refs[n_r + n_f + n_a + n_o:]
        res = fn(*ins)
        for r, v in zip(o_refs, res[:n_o]):
            r[...] = v.astype(r.dtype)
        if n_d:
            @pl.when(pl.program_id(0) == 0)
            def _():
                for r in d_refs:
                    r[...] = jnp.zeros_like(r)
            for r, v in zip(d_refs, res[n_o:]):
                r[...] += v.astype(r.dtype)

    res = pl.pallas_call(
        body, name=name, grid=(T // tm,),
        in_specs=[pl.BlockSpec((tm, r.shape[1]), lambda i: (i, 0)) for r in rows]
        + [pl.BlockSpec(f.shape, lambda i, nd=f.ndim: (0,) * nd) for f in fulls] + [ANY] * n_a,
        out_specs=[pl.BlockSpec((tm, c), lambda i: (i, 0)) for c, _ in outs]
        + [pl.BlockSpec(s, lambda i, nd=len(s): (0,) * nd) for s, _ in reds],
        out_shape=[jax.ShapeDtypeStruct((T, c), dt) for c, dt in outs]
        + [jax.ShapeDtypeStruct(s, dt) for s, dt in reds],
        compiler_params=_params(("arbitrary",)),
    )(*rows, *fulls, *after)
    return res


def _rms(x, g):
    return x * lax.rsqrt(jnp.mean(x * x, axis=-1, keepdims=True) + EPS) * g


def _colsum(v):
    return jnp.sum(v, axis=0, keepdims=True)


def _softplus(x):
    return jnp.maximum(x, 0.0) + jnp.log1p(jnp.exp(-jnp.abs(x)))


def _ffn_fwd(x, norm, wgT, wuT, wd, tag, on_gate=None):
    D = x.shape[1]
    (hn,) = _rowwise(lambda xt, g: (_rms(xt, g),), [x], [norm], [(D, BF16)], name=f"{tag}_norm")
    a = _matmul(hn, wgT, "nt", name=f"{tag}_gate", out=(BF16,))
    if on_gate is not None:
        on_gate(a)
    b, hid = _matmul(hn, wuT, "nt", name=f"{tag}_up", out=(BF16, BF16),
                     epilogue=lambda acc, at: (acc, jax.nn.silu(at.astype(F32)) * acc), extras=[a])
    y = _matmul(hid, wd, "nn", name=f"{tag}_down", epilogue=lambda acc, xt: (xt + 0.5 * acc,), extras=[x])
    return y, (hn, a, b, hid)


def _ffn_bwd(dy, dy16, x, norm, wgT, wuT, wd, res, tag, after, on_grads):
    hn, a, b, hid = res
    D, F = x.shape[1], a.shape[1]

    def act_bwd(acc, at, bt):
        _, vjp = jax.vjp(lambda p, q: jax.nn.silu(p) * q, at.astype(F32), bt.astype(F32))
        return vjp(0.5 * acc)

    da, db = _matmul(dy16, wd, "nt", name=f"{tag}_d_hid", out=(BF16, BF16), epilogue=act_bwd, extras=[a, b])
    d_wd = _matmul(hid, dy16, "tn", name=f"{tag}_d_wd", epilogue=lambda acc: (0.5 * acc,))
    d_wgT = _matmul(da, hn, "tn", name=f"{tag}_d_wg", after=after)
    d_wuT = _matmul(db, hn, "tn", name=f"{tag}_d_wu")
    started = on_grads(d_wgT, d_wuT, d_wd, dy)
    dhn = _matmul(da, wgT, "nn", name=f"{tag}_d_hn1")
    dhn = _matmul(db, wuT, "nn", name=f"{tag}_d_hn2", epilogue=lambda acc, e: (acc + e,), extras=[dhn])

    def norm_bwd(xt, dh, dyt, g):
        _, vjp = jax.vjp(_rms, xt, g)
        dx, dg = vjp(dh)
        return dyt + dx, dyt + dx, dg

    dx, dx16, d_norm = _rowwise(norm_bwd, [x, dhn, dy], [norm], [(D, F32), (D, BF16)], [((1, D), F32)],
                                name=f"{tag}_d_norm", after=started)
    return dx, dx16, d_norm


S5_TILES = 8
S5_HALF = 256


def _s5_derive(A_re, A_im, log_dt, B_re, B_im, C_re, C_im):
    G, N, M = S5_GROUPS, S5_STATE, S5_GROUP
    dt = jnp.exp(log_dt)[:, None]
    mag = jnp.exp(A_re * dt)
    ar, ai = mag * jnp.cos(A_im * dt), mag * jnp.sin(A_im * dt)
    den = A_re * A_re + A_im * A_im
    cr = ((ar - 1.0) * A_re + ai * A_im) / den
    ci = (ai * A_re - (ar - 1.0) * A_im) / den
    bbr = cr[..., None] * B_re - ci[..., None] * B_im
    bbi = cr[..., None] * B_im + ci[..., None] * B_re
    eye = jnp.eye(8, dtype=F32)

    def tile_in(bb):
        t = bb.reshape(S5_TILES, 8, N, M).transpose(0, 1, 3, 2)
        return jnp.einsum("jamn,ab->jambn", t, eye).reshape(S5_TILES, 8 * M, 8 * N)

    def tile_out(c):
        t = c.reshape(S5_TILES, 8, M, N).transpose(0, 1, 3, 2)
        return jnp.einsum("janm,ab->janbm", t, eye).reshape(S5_TILES, 8 * N, 8 * M)

    return (tile_in(bbr), tile_in(bbi), tile_out(C_re), tile_out(C_im),
            ar.reshape(S5_TILES, 1, 8 * N), ai.reshape(S5_TILES, 1, 8 * N))


S5_NB = S5_HALF // 128
S5_SEG = 32


def _cmul(ar, ai, br, bi):
    return ar * br - ai * bi, ar * bi + ai * br


def _s5_scan(sr_ref, si_ref, ar, ai, T, reverse):
    L, V = T // S5_SEG, S5_SEG // 8
    assert L * S5_SEG == T and L & (L - 1) == 0, T
    sg = -1.0 if reverse else 1.0
    a_r = [jnp.broadcast_to(ar[:, 128 * b:128 * (b + 1)], (8, 128)) for b in range(S5_NB)]
    a_i = [jnp.broadcast_to(sg * ai[:, 128 * b:128 * (b + 1)], (8, 128)) for b in range(S5_NB)]

    def rows(k, v):
        return pl.ds(pl.multiple_of(((L - 1 - k) if reverse else k) * S5_SEG + 8 * v, 8), 8)

    def local(k, carry):
        out = []
        for b in range(S5_NB):
            for v in range(V):
                idx = rows(k, v)
                mr, mi = _cmul(a_r[b], a_i[b], *carry[b * V + v])
                nr, ni = mr + sr_ref[b, idx, :], mi + si_ref[b, idx, :]
                sr_ref[b, idx, :] = nr
                si_ref[b, idx, :] = ni
                out.append((nr, ni))
        return tuple(out)

    z = jnp.zeros((8, 128), F32)
    ends = lax.fori_loop(0, L, local, tuple((z, z) for _ in range(S5_NB * V)))

    carries = []
    for b in range(S5_NB):
        pr, pi = a_r[b][0:1], a_i[b][0:1]
        n = L
        while n > 1:
            pr, pi = _cmul(pr, pi, pr, pi)
            n //= 2
        cr, ci = jnp.zeros((1, 128), F32), jnp.zeros((1, 128), F32)
        into = [None] * S5_SEG
        for j in (reversed(range(S5_SEG)) if reverse else range(S5_SEG)):
            into[j] = (cr, ci)
            er, ei = ends[b * V + j // 8]
            mr, mi = _cmul(pr, pi, cr, ci)
            cr, ci = mr + er[j % 8:j % 8 + 1], mi + ei[j % 8:j % 8 + 1]
        carries.append([(jnp.concatenate([into[8 * v + s][0] for s in range(8)], axis=0),
                         jnp.concatenate([into[8 * v + s][1] for s in range(8)], axis=0)) for v in range(V)])

    def fix(k, powers):
        out = []
        for b in range(S5_NB):
            pr, pi = powers[b]
            for v in range(V):
                idx = rows(k, v)
                dr, di = _cmul(pr, pi, *carries[b][v])
                sr_ref[b, idx, :] += dr
                si_ref[b, idx, :] += di
            out.append(_cmul(pr, pi, a_r[b], a_i[b]))
        return tuple(out)

    lax.fori_loop(0, L, fix, tuple((a_r[b], a_i[b]) for b in range(S5_NB)))


def _s5_in(ut, b_ref, s_ref):
    for b in range(S5_NB):
        s_ref[b] = _dot(ut, b_ref[:, 128 * b:128 * (b + 1)], NN)


def _to_seg(v):
    T, C = v.shape
    return v.reshape(S5_SEG, T // S5_SEG, C).transpose(1, 0, 2).reshape(T, C)


def _from_seg(v):
    T, C = v.shape
    return v.reshape(T // S5_SEG, S5_SEG, C).transpose(1, 0, 2).reshape(T, C)


def _s5_specs(T):
    u_spec = pl.BlockSpec((T, 128), lambda j, h: (0, j))
    b_spec = pl.BlockSpec((None, 128, S5_HALF), lambda j, h: (j, 0, h))
    c_spec = pl.BlockSpec((None, S5_HALF, 128), lambda j, h: (j, h, 0))
    a_spec = pl.BlockSpec((None, 1, S5_HALF), lambda j, h: (j, 0, h))
    return u_spec, b_spec, c_spec, a_spec


def _s5_fwd(u, tiles):
    T = u.shape[0]
    u_spec, b_spec, c_spec, a_spec = _s5_specs(T)

    def body(u_ref, br_ref, bi_ref, cr_ref, ci_ref, ar_ref, ai_ref, y_ref, sr_ref, si_ref):
        ut = u_ref[...].astype(MXU)
        _s5_in(ut, br_ref, sr_ref)
        _s5_in(ut, bi_ref, si_ref)
        _s5_scan(sr_ref, si_ref, ar_ref[...], ai_ref[...], T, False)
        y = None
        for b in range(S5_NB):
            blk = slice(128 * b, 128 * (b + 1))
            yb = _dot(sr_ref[b], cr_ref[blk, :], NN) - _dot(si_ref[b], ci_ref[blk, :], NN)
            y = yb if y is None else y + yb

        @pl.when(pl.program_id(1) == 0)
        def _():
            y_ref[...] = y

        @pl.when(pl.program_id(1) == 1)
        def _():
            y_ref[...] += y

    scr = pltpu.VMEM((S5_NB, T, 128), F32)
    return pl.pallas_call(
        body, name="s5_fwd", grid=(S5_TILES, 2),
        in_specs=[u_spec, b_spec, b_spec, c_spec, c_spec, a_spec, a_spec],
        out_specs=u_spec, out_shape=jax.ShapeDtypeStruct(u.shape, F32),
        scratch_shapes=[scr, scr],
        compiler_params=_params(("parallel", "arbitrary")),
    )(u, *tiles)


def _s5_bwd(u, dys, du_skip, tiles):
    T = u.shape[0]
    u_spec, b_spec, c_spec, a_spec = _s5_specs(T)

    def body(u_ref, dy_ref, sk_ref, br_ref, bi_ref, cr_ref, ci_ref, ar_ref, ai_ref,
             du_ref, dbr_ref, dbi_ref, dcr_ref, dci_ref, dar_ref, dai_ref, sr_ref, si_ref, lr_ref, li_ref):
        ut, dy = u_ref[...].astype(MXU), dy_ref[...].astype(MXU)
        ar, ai = ar_ref[...], ai_ref[...]
        _s5_in(ut, br_ref, sr_ref)
        _s5_in(ut, bi_ref, si_ref)
        _s5_scan(sr_ref, si_ref, ar, ai, T, False)
        for b in range(S5_NB):
            blk = slice(128 * b, 128 * (b + 1))
            dcr_ref[blk, :] = _dot(sr_ref[b], dy, TN)
            dci_ref[blk, :] = -_dot(si_ref[b], dy, TN)
            lr_ref[b] = _dot(dy, cr_ref[blk, :], NT)
            li_ref[b] = -_dot(dy, ci_ref[blk, :], NT)
        _s5_scan(lr_ref, li_ref, ar, ai, T, True)
        du = None
        head = lax.broadcasted_iota(jnp.int32, (T, 128), 0) < S5_SEG
        seg0 = lax.broadcasted_iota(jnp.int32, (S5_SEG, 128), 0) == 0
        for b in range(S5_NB):
            blk = slice(128 * b, 128 * (b + 1))
            lr, li = lr_ref[b], li_ref[b]
            pr = jnp.where(head, 0.0, pltpu.roll(sr_ref[b], S5_SEG, 0))
            pi = jnp.where(head, 0.0, pltpu.roll(si_ref[b], S5_SEG, 0))
            er = jnp.where(seg0, 0.0, pltpu.roll(sr_ref[b, T - S5_SEG:T, :], 1, 0))
            ei = jnp.where(seg0, 0.0, pltpu.roll(si_ref[b, T - S5_SEG:T, :], 1, 0))
            hr, hi = lr[0:S5_SEG], li[0:S5_SEG]
            dar_ref[:, blk] = _colsum(lr * pr + li * pi) + _colsum(hr * er + hi * ei)
            dai_ref[:, blk] = _colsum(li * pr - lr * pi) + _colsum(hi * er - hr * ei)
            dbr_ref[:, blk] = _dot(ut, lr, TN)
            dbi_ref[:, blk] = _dot(ut, li, TN)
            dub = _dot(lr, br_ref[:, blk], NT) + _dot(li, bi_ref[:, blk], NT)
            du = dub if du is None else du + dub

        @pl.when(pl.program_id(1) == 0)
        def _():
            du_ref[...] = du + sk_ref[...]

        @pl.when(pl.program_id(1) == 1)
        def _():
            du_ref[...] += du

    scr = pltpu.VMEM((S5_NB, T, 128), F32)
    return pl.pallas_call(
        body, name="s5_bwd", grid=(S5_TILES, 2),
        in_specs=[u_spec, u_spec, u_spec, b_spec, b_spec, c_spec, c_spec, a_spec, a_spec],
        out_specs=[u_spec, b_spec, b_spec, c_spec, c_spec, a_spec, a_spec],
        out_shape=[jax.ShapeDtypeStruct(u.shape, F32)] + [jax.ShapeDtypeStruct(t.shape, F32) for t in tiles],
        scratch_shapes=[scr, scr, scr, scr],
        compiler_params=_params(("parallel", "arbitrary")),
    )(u, dys, du_skip, *tiles)


CONV_TILE = 256


def _conv_pre(x, w, b):
    T = x.shape[0]
    row = lax.broadcasted_iota(jnp.int32, x.shape, 0)
    acc = x * w[CONV_K - 1:CONV_K, :] + b
    for lag in range(1, CONV_K):
        acc = acc + jnp.where(row >= lag, pltpu.roll(x, lag, 0), 0.0) * w[CONV_K - 1 - lag:CONV_K - lag, :]
    return acc


def _conv_fwd(x, w, b):
    T, C = x.shape
    col = pl.BlockSpec((T, CONV_TILE), lambda j: (0, j))

    def body(x_ref, w_ref, b_ref, o_ref):
        o_ref[...] = jax.nn.silu(_conv_pre(x_ref[...], w_ref[...], b_ref[...]))

    return pl.pallas_call(
        body, name="conv_fwd", grid=(C // CONV_TILE,),
        in_specs=[col, pl.BlockSpec((CONV_K, CONV_TILE), lambda j: (0, j)), pl.BlockSpec((1, CONV_TILE), lambda j: (0, j))],
        out_specs=col, out_shape=jax.ShapeDtypeStruct((T, C), F32),
        compiler_params=_params(("parallel",)),
    )(x, w, b)


def _conv_bwd(x, w, b, dout, col0, name):
    T, C = dout.shape
    off = col0 // CONV_TILE
    xcol = pl.BlockSpec((T, CONV_TILE), lambda j: (0, j + off))
    dcol = pl.BlockSpec((T, CONV_TILE), lambda j: (0, j))

    def body(x_ref, w_ref, b_ref, d_ref, dx_ref, dw_ref, db_ref):
        x, w = x_ref[...], w_ref[...]
        _, vjp = jax.vjp(jax.nn.silu, _conv_pre(x, w, b_ref[...]))
        (dy,) = vjp(d_ref[...])
        row = lax.broadcasted_iota(jnp.int32, x.shape, 0)
        dx = dy * w[CONV_K - 1:CONV_K, :]
        dw_ref[CONV_K - 1:CONV_K, :] = _colsum(dy * x)
        for lag in range(1, CONV_K):
            ahead = jnp.where(row < T - lag, pltpu.roll(dy, T - lag, 0), 0.0)
            dx = dx + ahead * w[CONV_K - 1 - lag:CONV_K - lag, :]
            dw_ref[CONV_K - 1 - lag:CONV_K - lag, :] = _colsum(ahead * x)
        dx_ref[...] = dx.astype(dx_ref.dtype)
        db_ref[...] = _colsum(dy)

    return pl.pallas_call(
        body, name=name, grid=(C // CONV_TILE,),
        in_specs=[xcol, pl.BlockSpec((CONV_K, CONV_TILE), lambda j: (0, j + off)),
                  pl.BlockSpec((1, CONV_TILE), lambda j: (0, j + off)), dcol],
        out_specs=[dcol, pl.BlockSpec((CONV_K, CONV_TILE), lambda j: (0, j)), pl.BlockSpec((1, CONV_TILE), lambda j: (0, j))],
        out_shape=[jax.ShapeDtypeStruct((T, C), BF16), jax.ShapeDtypeStruct((CONV_K, C), F32), jax.ShapeDtypeStruct((1, C), F32)],
        compiler_params=_params(("parallel",)),
    )(x, w, b, dout)


def _ssd_common(x_ref, b_ref, c_ref, dtr_ref, bias_ref, alog_ref, tri_ref, triu_ref):
    Q = x_ref.shape[0]
    x, Bm, Cm = x_ref[...], b_ref[...], c_ref[...]
    pre = dtr_ref[...] + bias_ref[...]
    dt = _softplus(pre)
    A = -jnp.exp(alog_ref[...])
    adt = dt * A
    ac4 = _dot(tri_ref[...], adt, NN, HI)
    ar4 = _dot(adt, triu_ref[...], TN, HI)
    cb = _dot(Cm, Bm, NT)
    ii = lax.broadcasted_iota(jnp.int32, (Q, Q), 0)
    jj = lax.broadcasted_iota(jnp.int32, (Q, Q), 1)
    atot4 = ac4[Q - 1:Q, :]
    return x, Bm, Cm, pre, dt, A, ac4, ar4, cb, ii >= jj, atot4


def _ssd_decay(k, ac4, ar4, causal):
    seg = ac4[:, k:k + 1] - ar4[k:k + 1, :]
    return jnp.where(causal, jnp.exp(jnp.where(causal, seg, 0.0)), 0.0)


def _per_head(c4, n):
    lane = lax.broadcasted_iota(jnp.int32, (c4.shape[0], n), 1)
    out = jnp.broadcast_to(c4[:, 0:1], (c4.shape[0], n))
    for k in range(1, SSD_HPG):
        out = jnp.where(lane >= SSD_HEADDIM * k, c4[:, k:k + 1], out)
    return out


def _head_sums(v):
    n = SSD_HPG * SSD_HEADDIM
    row = lax.broadcasted_iota(jnp.int32, (n, SSD_HPG), 0)
    col = lax.broadcasted_iota(jnp.int32, (n, SSD_HPG), 1)
    member = jnp.where((row >= SSD_HEADDIM * col) & (row < SSD_HEADDIM * (col + 1)), 1.0, 0.0)
    return _dot(v, member, NN, HI)


def _head_rows(c4):
    n = SSD_HPG * SSD_HEADDIM
    row = lax.broadcasted_iota(jnp.int32, (n, 1), 0)
    out = jnp.broadcast_to(c4[:, 0:1], (n, 1))
    for k in range(1, SSD_HPG):
        out = jnp.where(row >= SSD_HEADDIM * k, c4[:, k:k + 1], out)
    return out


def _ssd_specs(T, Q, rev):
    NC = T // Q
    cc = (lambda c: NC - 1 - c) if rev else (lambda c: c)
    W, N, S = SSD_HPG * SSD_HEADDIM, SSD_STATE, SSD_STEP_GROUPS
    x_spec = pl.BlockSpec((Q, S * W), lambda g, c: (cc(c), g))
    b_spec = pl.BlockSpec((Q, S * N), lambda g, c: (cc(c), D_INNER // (S * N) + g))
    c_spec = pl.BlockSpec((Q, S * N), lambda g, c: (cc(c), (D_INNER + SSD_GROUPS * N) // (S * N) + g))
    dt_spec = pl.BlockSpec((S, Q, SSD_HPG), lambda g, c: (g, cc(c), 0))
    p_spec = pl.BlockSpec((S, 1, SSD_HPG), lambda g, c: (g, 0, 0))
    tri_spec = pl.BlockSpec((Q, Q), lambda g, c: (0, 0))
    h_spec = pl.BlockSpec((S, None, W, N), lambda g, c: (g, cc(c), 0, 0))
    return x_spec, b_spec, c_spec, dt_spec, p_spec, tri_spec, h_spec


def _ssd_views(gi, wide, narrow, lead):
    W, N = SSD_HPG * SSD_HEADDIM, SSD_STATE
    return ([r.at[:, pl.ds(W * gi, W)] for r in wide] + [r.at[:, pl.ds(N * gi, N)] for r in narrow]
            + [r.at[gi] for r in lead])


def _tri(Q):
    tri = jnp.tril(jnp.ones((Q, Q), F32))
    return tri, tri.T


def _ssd_fwd(xc, dtr8, bias8, alog8, dsk8):
    T, Q = xc.shape[0], SSD_Q
    NC = T // Q
    P, W = SSD_HEADDIM, SSD_HPG * SSD_HEADDIM
    x_spec, b_spec, c_spec, dt_spec, p_spec, tri_spec, h_spec = _ssd_specs(T, Q, False)

    def body(x_ref, b_ref, c_ref, dtr_ref, bias_ref, alog_ref, dsk_ref, tri_ref, triu_ref, y_ref, hs_ref, h_scr):
        for gi in range(SSD_STEP_GROUPS):
            xv, yv, bv, cv, dtv, biasv, alogv, dskv, hsv, hv = _ssd_views(
                gi, (x_ref, y_ref), (b_ref, c_ref), (dtr_ref, bias_ref, alog_ref, dsk_ref, hs_ref, h_scr))
            one(xv, bv, cv, dtv, biasv, alogv, dskv, tri_ref, triu_ref, yv, hsv, hv)

    def one(x_ref, b_ref, c_ref, dtr_ref, bias_ref, alog_ref, dsk_ref, tri_ref, triu_ref, y_ref, hs_ref, h_scr):
        @pl.when(pl.program_id(1) == 0)
        def _():
            h_scr[...] = jnp.zeros_like(h_scr)

        hin = h_scr[...]
        hs_ref[...] = hin
        x, Bm, Cm, pre, dt, A, ac4, ar4, cb, causal, atot4 = _ssd_common(
            x_ref, b_ref, c_ref, dtr_ref, bias_ref, alog_ref, tri_ref, triu_ref)
        xdt = x * _per_head(dt, W)
        rest = _dot(Cm, hin, NT) * _per_head(jnp.exp(ac4), W) + _per_head(dsk_ref[...], W) * x
        for k in range(SSD_HPG):
            blk = slice(P * k, P * (k + 1))
            G = cb * _ssd_decay(k, ac4, ar4, causal)
            y_ref[:, blk] = _dot(G, xdt[:, blk], NN) + rest[:, blk]
        xw = xdt * _per_head(jnp.exp(atot4 - ac4), W)
        h_scr[...] = _head_rows(jnp.exp(atot4)) * hin + _dot(xw, Bm, TN)

    tri, triu = _tri(Q)
    return pl.pallas_call(
        body, name="ssd_fwd", grid=(SSD_GROUPS // SSD_STEP_GROUPS, NC),
        in_specs=[x_spec, b_spec, c_spec, dt_spec, p_spec, p_spec, p_spec, tri_spec, tri_spec],
        out_specs=[pl.BlockSpec((Q, SSD_STEP_GROUPS * W), lambda g, c: (c, g)), h_spec],
        out_shape=[jax.ShapeDtypeStruct((T, D_INNER), F32),
                   jax.ShapeDtypeStruct((SSD_GROUPS, NC, W, SSD_STATE), F32)],
        scratch_shapes=[pltpu.VMEM((SSD_STEP_GROUPS, W, SSD_STATE), F32)],
        compiler_params=_params(("parallel", "arbitrary")),
    )(xc, xc, xc, dtr8, bias8, alog8, dsk8, tri, triu)


def _ssd_bwd(xc, dtr8, bias8, alog8, dsk8, hs, dy):
    T, Q = xc.shape[0], SSD_Q
    NC = T // Q
    P, W = SSD_HEADDIM, SSD_HPG * SSD_HEADDIM
    x_spec, b_spec, c_spec, dt_spec, p_spec, tri_spec, h_spec = _ssd_specs(T, Q, True)
    dy_spec = pl.BlockSpec((Q, SSD_STEP_GROUPS * W), lambda g, c: (NC - 1 - c, g))
    dbc_spec = pl.BlockSpec((Q, SSD_STEP_GROUPS * SSD_STATE), lambda g, c: (NC - 1 - c, g))

    def body(x_ref, b_ref, c_ref, dtr_ref, bias_ref, alog_ref, dsk_ref, tri_ref, triu_ref, hs_ref, dy_ref,
             dx_ref, db_ref, dc_ref, ddtr_ref, dbias_ref, dalog_ref, ddsk_ref, dh_scr):
        for gi in range(SSD_STEP_GROUPS):
            (xv, dyv, dxv, bv, cv, dbv, dcv, dtv, biasv, alogv, dskv, hsv, ddtv, dbiasv, dalogv, ddskv, dhv) = _ssd_views(
                gi, (x_ref, dy_ref, dx_ref), (b_ref, c_ref, db_ref, dc_ref),
                (dtr_ref, bias_ref, alog_ref, dsk_ref, hs_ref, ddtr_ref, dbias_ref, dalog_ref, ddsk_ref, dh_scr))
            one(xv, bv, cv, dtv, biasv, alogv, dskv, tri_ref, triu_ref, hsv, dyv,
                dxv, dbv, dcv, ddtv, dbiasv, dalogv, ddskv, dhv)

    def one(x_ref, b_ref, c_ref, dtr_ref, bias_ref, alog_ref, dsk_ref, tri_ref, triu_ref, hs_ref, dy_ref,
            dx_ref, db_ref, dc_ref, ddtr_ref, dbias_ref, dalog_ref, ddsk_ref, dh_scr):
        @pl.when(pl.program_id(1) == 0)
        def _():
            dh_scr[...] = jnp.zeros_like(dh_scr)
            dbias_ref[...] = jnp.zeros_like(dbias_ref)
            dalog_ref[...] = jnp.zeros_like(dalog_ref)
            ddsk_ref[...] = jnp.zeros_like(ddsk_ref)

        x, Bm, Cm, pre, dt, A, ac4, ar4, cb, causal, atot4 = _ssd_common(
            x_ref, b_ref, c_ref, dtr_ref, bias_ref, alog_ref, tri_ref, triu_ref)
        dyv, hin, dho = dy_ref[...], hs_ref[...], dh_scr[...]
        dt_w = _per_head(dt, W)
        xdt = x * dt_w
        E4, F4, etot4 = jnp.exp(ac4), jnp.exp(atot4 - ac4), jnp.exp(atot4)
        F_w = _per_head(F4, W)
        ddsk_ref[...] += _head_sums(_colsum(dyv * x))
        Z = _dot(Cm, hin, NT)
        dZ = dyv * _per_head(E4, W)
        dac4 = _head_sums(dyv * Z) * E4
        dC = _dot(dZ, hin, NN)
        dh_scr[...] = _dot(dZ, Cm, TN) + _head_rows(etot4) * dho
        per_row = jnp.sum(dho * hin, axis=1, keepdims=True)
        lane4 = lax.broadcasted_iota(jnp.int32, (1, SSD_HPG), 1)
        datot4 = jnp.zeros((1, SSD_HPG), F32)
        for k in range(SSD_HPG):
            datot4 = jnp.where(lane4 == k, jnp.sum(per_row[P * k:P * (k + 1)], keepdims=True), datot4)
        datot4 = datot4 * etot4
        dxw = _dot(Bm, dho, NT)
        dB = _dot(xdt * F_w, dho, NN)
        dFa4 = _head_sums(dxw * xdt) * F4
        datot4 = datot4 + _colsum(dFa4)
        dac4 = dac4 - dFa4
        dcb = jnp.zeros((Q, Q), F32)
        lane_q4 = lax.broadcasted_iota(jnp.int32, (Q, SSD_HPG), 1)
        sub_4q = lax.broadcasted_iota(jnp.int32, (SSD_HPG, Q), 0)
        row_sums = jnp.zeros((Q, SSD_HPG), F32)
        col_sums = jnp.zeros((SSD_HPG, Q), F32)
        dxdt_heads = []
        for k in range(SSD_HPG):
            blk = slice(P * k, P * (k + 1))
            L = _ssd_decay(k, ac4, ar4, causal)
            G = cb * L
            dG = _dot(dyv[:, blk], xdt[:, blk], NT)
            dxdt_heads.append(_dot(G, dyv[:, blk], TN))
            dcb = dcb + dG * L
            Mseg = dG * G
            row_sums = jnp.where(lane_q4 == k, jnp.sum(Mseg, axis=1, keepdims=True), row_sums)
            col_sums = jnp.where(sub_4q == k, jnp.sum(Mseg, axis=0, keepdims=True), col_sums)
        dxdt = dxw * F_w + jnp.concatenate(dxdt_heads, axis=1)
        last = lax.broadcasted_iota(jnp.int32, (Q, SSD_HPG), 0) == Q - 1
        dac4 = dac4 + row_sums + jnp.where(last, datot4, 0.0)
        dadt4 = _dot(triu_ref[...], dac4, NN, HI) - _dot(triu_ref[...], col_sums, NT, HI)
        ddt4 = _head_sums(dxdt * x) + dadt4 * A
        dalog_ref[...] += _colsum(dadt4 * dt) * A
        ddtr4 = ddt4 * jax.nn.sigmoid(pre)
        dbias_ref[...] += _colsum(ddtr4)
        ddtr_ref[...] = ddtr4
        dx_ref[...] = (_per_head(dsk_ref[...], W) * dyv + dxdt * dt_w).astype(dx_ref.dtype)
        db_ref[...] = dB + _dot(dcb, Cm, TN)
        dc_ref[...] = dC + _dot(dcb, Bm, NN)

    tri, triu = _tri(Q)
    return pl.pallas_call(
        body, name="ssd_bwd", grid=(SSD_GROUPS // SSD_STEP_GROUPS, NC),
        in_specs=[x_spec, b_spec, c_spec, dt_spec, p_spec, p_spec, p_spec, tri_spec, tri_spec, h_spec, dy_spec],
        out_specs=[dy_spec, dbc_spec, dbc_spec, dt_spec, p_spec, p_spec, p_spec],
        out_shape=[jax.ShapeDtypeStruct((T, D_INNER), F32),
                   jax.ShapeDtypeStruct((T, SSD_GROUPS * SSD_STATE), F32),
                   jax.ShapeDtypeStruct((T, SSD_GROUPS * SSD_STATE), F32),
                   jax.ShapeDtypeStruct(dtr8.shape, F32)] + [jax.ShapeDtypeStruct(bias8.shape, F32)] * 3,
        scratch_shapes=[pltpu.VMEM((SSD_STEP_GROUPS, W, SSD_STATE), F32)],
        compiler_params=_params(("parallel", "arbitrary")),
    )(xc, xc, xc, dtr8, bias8, alog8, dsk8, tri, triu, hs, dy)


def _to_groups(v):
    return v.reshape(SSD_GROUPS, 1, SSD_HPG)


def _mixer_fwd(x, p, w):
    T, D = x.shape
    (hn,) = _rowwise(lambda xt, g: (_rms(xt, g),), [x], [p["mix_norm"]], [(D, BF16)], name="mix_norm")
    winT = w["w_in"]
    u, z, xbc, dtr, gl = [
        _matmul(hn, winT[IN_OFFS[i]:IN_OFFS[i + 1]], "nt", name=f"mix_in{i}") for i in range(5)]
    tiles = _s5_derive(*[p[k][0] for k in ("s5_A_re", "s5_A_im", "s5_log_dt", "s5_B_re", "s5_B_im", "s5_C_re", "s5_C_im")])
    u = _to_seg(u)
    ys = _s5_fwd(u, tiles)
    (g,) = _rowwise(lambda yt, ut, d: (jax.nn.gelu(yt + d * ut),), [ys, u], [p["s5_D"]], [(D, F32)], name="s5_gelu")
    y5 = _matmul(g, w["s5_w_glu"], "nn", name="s5_glu", out=(BF16,),
                 epilogue=lambda acc, gt, b: (gt * jax.nn.sigmoid(acc + b),), extras=[g, p["s5_b_glu"]])
    xc = _conv_fwd(xbc, w["conv_w"], p["conv_b"])
    dtr8 = dtr.reshape(T, SSD_GROUPS, SSD_HPG).transpose(1, 0, 2)
    ssd_p = [_to_groups(p[k]) for k in ("ssd_dt_bias", "ssd_A_log", "ssd_D")]
    yssd_raw, hs = _ssd_fwd(xc, dtr8, *ssd_p)
    (yssd,) = _rowwise(lambda yt, zt, nw: (_rms(yt * jax.nn.silu(zt), nw),), [yssd_raw, z], [p["ssd_norm"]],
                       [(D_INNER, BF16)], name="ssd_gate")
    p5 = _from_seg(_matmul(y5, w["w_proj_s5"], "nn", name="mix_p5"))
    pssd = _matmul(yssd, w["w_proj_ssd"], "nn", name="mix_pssd")

    def merge(glt, at, bt, bg):
        gates = jax.nn.sigmoid(glt + bg)
        return (gates[:, :D] * at + gates[:, D:] * bt,)

    (merged,) = _rowwise(merge, [gl, p5, pssd], [p["b_gate"]], [(D, BF16)], name="mix_merge")
    y = _matmul(merged, w["w_out"], "nn", name="mix_out", epilogue=lambda acc, xt: (xt + acc,), extras=[x])
    res = dict(hn=hn, u=u, z=z, xbc=xbc, gl=gl, tiles=tiles, ys=ys, g=g, y5=y5, xc=xc, dtr8=dtr8, ssd_p=ssd_p,
               yssd_raw=yssd_raw, hs=hs, yssd=yssd, p5=p5, pssd=pssd, merged=merged)
    return y, res


def _mixer_bwd(dy, dy16, x, p, w, r):
    T, D = x.shape
    gp = {}
    dmerged = _matmul(dy16, w["w_out"], "nt", name="mix_d_merged")
    gp["w_out"] = _matmul(r["merged"], dy16, "tn", name="mix_d_wout")

    def merge_bwd(glt, at, bt, dm, bg):
        def f(q, a_, b_):
            gates = jax.nn.sigmoid(q + bg)
            return gates[:, :D] * a_ + gates[:, D:] * b_
        _, vjp = jax.vjp(f, glt, at, bt)
        dq, da_, db_ = vjp(dm)
        return da_, db_, dq, _colsum(dq)

    dp5, dpssd, dgl, gp["b_gate"] = _rowwise(
        merge_bwd, [r["gl"], r["p5"], r["pssd"], dmerged], [p["b_gate"]],
        [(D, BF16), (D, BF16), (2 * D, BF16)], [((1, 2 * D), F32)], name="mix_d_merge")
    dyssd = _matmul(dpssd, w["w_proj_ssd"], "nt", name="mix_d_yssd")
    gp["w_proj_ssd"] = _matmul(r["yssd"], dpssd, "tn", name="mix_d_wpssd")

    def gate_bwd(yt, zt, dyt, nw):
        _, vjp = jax.vjp(lambda a_, b_, c_: _rms(a_ * jax.nn.silu(b_), c_), yt, zt, nw)
        return vjp(dyt)

    dyraw, dz, gp["ssd_norm"] = _rowwise(
        gate_bwd, [r["yssd_raw"], r["z"], dyssd], [p["ssd_norm"]],
        [(D_INNER, F32), (D_INNER, BF16)], [((1, D_INNER), F32)], name="ssd_d_gate")
    dxs, dBm, dCm, ddtr8, dbias8, dalog8, ddsk8 = _ssd_bwd(r["xc"], r["dtr8"], *r["ssd_p"], r["hs"], dyraw)
    gp["ssd_dt_bias"], gp["ssd_A_log"], gp["ssd_D"] = [v.reshape(1, SSD_HEADS) for v in (dbias8, dalog8, ddsk8)]
    ddtr = ddtr8.transpose(1, 0, 2).reshape(T, SSD_HEADS)
    conv = [_conv_bwd(r["xbc"], w["conv_w"], p["conv_b"], d, c0, f"conv_bwd{i}")
            for i, (d, c0) in enumerate(((dxs, 0), (dBm, D_INNER), (dCm, D_INNER + SSD_GROUPS * SSD_STATE)))]
    dxbc = [c[0] for c in conv]
    gp["conv_w"] = jnp.concatenate([c[1] for c in conv], axis=1)
    gp["conv_b"] = jnp.concatenate([c[2] for c in conv], axis=1)

    dp5 = _to_seg(dp5)
    dy5 = _matmul(dp5, w["w_proj_s5"], "nt", name="mix_d_y5")
    gp["w_proj_s5"] = _matmul(r["y5"], dp5, "tn", name="mix_d_wp5")
    g = r["g"]

    def glu_ep(acc, gt, b, dyt):
        _, vjp = jax.vjp(lambda g_, t_: g_ * jax.nn.sigmoid(t_ + b), gt, acc)
        return vjp(dyt)

    dg1, dt_ = _matmul(g, w["s5_w_glu"], "nn", name="s5_d_glu", out=(F32, BF16), epilogue=glu_ep,
                       extras=[g, p["s5_b_glu"], dy5])
    gp["s5_w_glu"] = _matmul(g, dt_, "tn", name="s5_d_wglu")
    dg = _matmul(dt_, w["s5_w_glu"], "nt", name="s5_d_g", epilogue=lambda acc, e: (acc + e,), extras=[dg1])

    def gelu_bwd(yt, ut, dgt, dtt, d):
        _, vjp = jax.vjp(lambda y_, d_: jax.nn.gelu(y_ + d_ * ut), yt, d)
        dys, dd = vjp(dgt)
        return dys, dys * d, dd, _colsum(dtt.astype(F32))

    dys, dusk, gp["s5_D"], gp["s5_b_glu"] = _rowwise(
        gelu_bwd, [r["ys"], r["u"], dg, dt_], [p["s5_D"]], [(D, F32), (D, F32)], [((1, D), F32), ((1, D), F32)],
        name="s5_d_gelu")
    du, *dtiles = _s5_bwd(r["u"], dys, dusk, r["tiles"])
    du = _from_seg(du)

    winT = w["w_in"]
    hn = r["hn"]
    d_rows, dhn = [], None
    cols = [(du, 0, 1024), (dz, 1024, 3072), (dxbc[0], 3072, 5120), (dxbc[1], 5120, 6144), (dxbc[2], 6144, 7168),
            (ddtr, 7168, 7200), (dgl, 7200, 9248)]
    for i, (d, lo, hi) in enumerate(cols):
        d_rows.append(_matmul(d, hn, "tn", name=f"mix_d_win{i}"))
        if dhn is None:
            dhn = _matmul(d, winT[lo:hi], "nn", name=f"mix_d_hn{i}")
        else:
            dhn = _matmul(d, winT[lo:hi], "nn", name=f"mix_d_hn{i}", epilogue=lambda acc, e: (acc + e,), extras=[dhn])
    gp["w_in"] = jnp.concatenate(d_rows, axis=0)

    def norm_bwd(xt, dh, dyt, gn):
        _, vjp = jax.vjp(_rms, xt, gn)
        dx, dgn = vjp(dh)
        return dyt + dx, dyt + dx, dgn

    dx, dx16, gp["mix_norm"] = _rowwise(norm_bwd, [x, dhn, dy], [p["mix_norm"]], [(D, F32), (D, BF16)],
                                        [((1, D), F32)], name="mix_d_norm")
    return dx, dx16, gp, dtiles


def _s5_param_grads(p, dtiles):
    keys = ("s5_A_re", "s5_A_im", "s5_log_dt", "s5_B_re", "s5_B_im", "s5_C_re", "s5_C_im")
    _, vjp = jax.vjp(_s5_derive, *[p[k][0] for k in keys])
    return {k: v[None] for k, v in zip(keys, vjp(tuple(dtiles)))}


def _local_step(x, target, p, w, late_weights, exchange):
    T, D = x.shape
    w = dict(w)
    x1, r1 = _ffn_fwd(x, p["ffn1_norm"], w["ffn1_w_gate"], w["ffn1_w_up"], w["ffn1_w_down"], "ffn1",
                      on_gate=lambda a_: w.update(late_weights("mixer", a_)))
    x2, rm = _mixer_fwd(x1, p, w)
    w.update(late_weights("ffn2", rm["p5"]))
    x3, r2 = _ffn_fwd(x2, p["ffn2_norm"], w["ffn2_w_gate"], w["ffn2_w_up"], w["ffn2_w_down"], "ffn2")

    def head(xt, tt, g):
        def f(x_, g_):
            e = _rms(x_, g_) - tt
            return 0.5 * jnp.sum(jnp.mean(e * e, axis=-1))
        l, (dx_, dg_) = jax.value_and_grad(f, argnums=(0, 1))(xt, g)
        return dx_, dx_, l.reshape(1, 1), dg_

    dx3, dx3_16, loss, d_final = _rowwise(head, [x3, target], [p["final_norm"]], [(D, F32), (D, BF16)],
                                          [((1, 1), F32), ((1, D), F32)], name="loss_head")
    gp = {"final_norm": d_final}

    def ffn_exchange(tag):
        return lambda d_wgT, d_wuT, d_wd, after: exchange(
            tag, {f"{tag}_w_gate": d_wgT, f"{tag}_w_up": d_wuT, f"{tag}_w_down": d_wd}, after)

    dx2, dx2_16, gp["ffn2_norm"] = _ffn_bwd(
        dx3, dx3_16, x2, p["ffn2_norm"], w["ffn2_w_gate"], w["ffn2_w_up"], w["ffn2_w_down"], r2, "ffn2", (),
        ffn_exchange("ffn2"))
    dx1, dx1_16, gm, dtiles = _mixer_bwd(dx2, dx2_16, x1, p, w, rm)
    conv_w_grad = gm.pop("conv_w")
    started = exchange("mixer", {k: gm.pop(k) for k in ("w_out", "w_proj_s5", "w_proj_ssd", "s5_w_glu", "w_in")}, dx1)
    gp.update(gm)
    gp.update(_s5_param_grads(p, dtiles))
    dx0, _, gp["ffn1_norm"] = _ffn_bwd(
        dx1, dx1_16, x, p["ffn1_norm"], w["ffn1_w_gate"], w["ffn1_w_up"], w["ffn1_w_down"], r1, "ffn1", started,
        ffn_exchange("ffn1"))
    return loss, dx0, gp, conv_w_grad


ADAMW_BLOCK = 1 << 19


def _adamw(w, m, v, gs, *, name, emit_g):
    _, R, C = w.shape
    n_g = len(gs)
    n_o = 4 if emit_g else 3
    tm = _rtile(R, max(8, ADAMW_BLOCK // C))
    lead = pl.BlockSpec((None, tm, C), lambda i: (0, i, 0))

    def body(w_ref, m_ref, v_ref, *refs):
        g = refs[0][...]
        for r in refs[1:n_g]:
            g = g + r[...]
        m2 = ADAM_B1 * m_ref[...] + (1.0 - ADAM_B1) * g
        v2 = ADAM_B2 * v_ref[...] + (1.0 - ADAM_B2) * (g * g)
        m_hat = m2 / (1.0 - ADAM_B1 ** ADAM_STEP)
        v_hat = v2 / (1.0 - ADAM_B2 ** ADAM_STEP)
        delta = -ADAM_LR * (m_hat / (jnp.sqrt(v_hat) + ADAM_EPS) + ADAM_WD * w_ref[...])
        for r, val in zip(refs[n_g:], ((g,) if emit_g else ()) + (delta, m2, v2)):
            r[...] = val

    return pl.pallas_call(
        body, name=name, grid=(R // tm,),
        in_specs=[lead] * 3 + [pl.BlockSpec((tm, C), lambda i: (i, 0))] * n_g, out_specs=[lead] * n_o,
        out_shape=[jax.ShapeDtypeStruct((1, R, C), F32)] * n_o,
        compiler_params=_params(("parallel",)),
    )(w, m, v, *gs)


def _rtile(r, cap=512):
    if r <= cap:
        return r
    best = None
    for t in range(8, cap + 1, 8):
        if r % t == 0:
            best = t
    assert best is not None, r
    return best


ANY = pl.BlockSpec(memory_space=pl.ANY)


def _where_am_i():
    x, y, c = lax.axis_index("x"), lax.axis_index("y"), lax.axis_index("c")
    ks = (2 * x + y, 2 * (1 - x) + y, 2 * x + (1 - y), 2 * (1 - x) + (1 - y))
    return x, y, c, ks


def _rc(src, dst, ssem, rsem, to):
    return pltpu.make_async_remote_copy(src_ref=src, dst_ref=dst, send_sem=ssem, recv_sem=rsem,
                                        device_id=to, device_id_type=MESH)


def _gather_shards(srcs):
    n = len(srcs)

    def body(*refs):
        _gather_body(refs[:n], refs[n:2 * n], *refs[2 * n:], extra=())

    return pl.pallas_call(
        body, name="gather_shards",
        in_specs=[ANY] * n, out_specs=[ANY] * n,
        out_shape=[jax.ShapeDtypeStruct((4,) + s.shape, s.dtype) for s in srcs],
        scratch_shapes=[pltpu.SemaphoreType.DMA((n, 6)), pltpu.SemaphoreType.DMA((n, 6)), pltpu.SemaphoreType.DMA((n,))],
        compiler_params=pltpu.CompilerParams(has_side_effects=True),
    )(*srcs)


def _handshake_all():
    x, y, c = lax.axis_index("x"), lax.axis_index("y"), lax.axis_index("c")
    barrier = pltpu.get_barrier_semaphore()
    for dx, dy, dc in itertools.product((0, 1), repeat=3):
        if (dx, dy, dc) != (0, 0, 0):
            to = (1 - x if dx else x, 1 - y if dy else y, 1 - c if dc else c)
            pl.semaphore_signal(barrier, inc=1, device_id=to, device_id_type=MESH)
    pl.semaphore_wait(barrier, 7)


SEQUENCER = dict(axis_name="seq", num_cores=1)


def _gather_shards_async(srcs, token, name, collective_id):
    n = len(srcs)

    def body(*refs):
        tok, tok_out = refs[n], refs[2 * n + 1]
        ssem, rsem, lsem = refs[2 * n + 2:]
        _handshake_all()
        _gather_body(refs[:n], refs[n + 1:2 * n + 1], ssem, rsem, lsem,
                     extra=(pltpu.make_async_copy(tok, tok_out, lsem.at[n]),))

    res = pl.kernel(
        body, name=name,
        out_type=[jax.ShapeDtypeStruct((4,) + s.shape, s.dtype) for s in srcs]
        + [jax.ShapeDtypeStruct(token.shape, token.dtype)],
        mesh=plsc.ScalarSubcoreMesh(**SEQUENCER),
        scratch_types=[pltpu.SemaphoreType.DMA((n, 6)), pltpu.SemaphoreType.DMA((n, 6)), pltpu.SemaphoreType.DMA((n + 1,))],
        compiler_params=pltpu.CompilerParams(collective_id=collective_id),
    )(*srcs, token)
    return res[:n], res[n]


def _gather_body(src, out, ssem, rsem, lsem, extra):
        n = len(src)
        x, y, c, (k_me, k_x, k_y, k_d) = _where_am_i()
        sib = (x, y, 1 - c)
        local = [pltpu.make_async_copy(src[i], out[i].at[k_me], lsem.at[i]) for i in range(n)] + list(extra)
        for cp in local:
            cp.start()

        def own(i, q, to):
            return _rc(src[i], out[i].at[k_me], ssem.at[i, q], rsem.at[i, q], to)

        def slab(i, q, k, to):
            return _rc(out[i].at[k], out[i].at[k], ssem.at[i, q], rsem.at[i, q], to)

        @pl.when(c == 1)
        def _():
            sends = [own(i, 0, (1 - x, y, 1)) for i in range(n)]
            for cp in sends:
                cp.start()
            fwd = [slab(i, 3, k_x, sib) for i in range(n)]
            for i in range(n):
                slab(i, 0, k_x, sib).wait_recv()
                fwd[i].start()
            for i in range(n):
                slab(i, 4, k_y, sib).wait_recv()
                slab(i, 5, k_d, sib).wait_recv()
            for cp in sends + fwd:
                cp.wait_send()

        @pl.when(c == 0)
        def _():
            sends = [own(i, 1, (x, 1 - y, 0)) for i in range(n)] + [own(i, 2, (1 - x, 1 - y, 0)) for i in range(n)]
            for cp in sends:
                cp.start()
            fwd = [slab(i, 4, k_y, sib) for i in range(n)] + [slab(i, 5, k_d, sib) for i in range(n)]
            for i in range(n):
                slab(i, 1, k_y, sib).wait_recv()
                fwd[i].start()
            for i in range(n):
                slab(i, 2, k_d, sib).wait_recv()
                fwd[n + i].start()
            for i in range(n):
                slab(i, 3, k_x, sib).wait_recv()
            for cp in sends + fwd:
                cp.wait_send()

        for cp in local:
            cp.wait()


def _swap_slabs(arrs, n_slabs, name, async_id, token):
    n = len(arrs)
    J = max(n_slabs, 1)

    def body(*refs):
        src, out = refs[:n], refs[n:2 * n]
        ssem, rsem = refs[2 * n:]
        x, y, c, (k_me, k_x, k_y, k_d) = _where_am_i()
        sib = (x, y, 1 - c)
        sel = (jnp.where(c == 1, k_y, k_me), jnp.where(c == 1, k_d, k_x))
        cps = []
        for i in range(n):
            for j in range(J):
                s = src[i].at[sel[j]] if n_slabs else src[i]
                d = out[i].at[j] if n_slabs else out[i]
                cps.append(_rc(s, d, ssem.at[i, j], rsem.at[i, j], sib))
        for cp in cps:
            cp.start()
        for cp in cps:
            cp.wait()

    return _exchange_call(
        body, arrs, [jax.ShapeDtypeStruct(((n_slabs,) if n_slabs else ()) + a.shape[-2:], a.dtype) for a in arrs],
        [pltpu.SemaphoreType.DMA((n, J)), pltpu.SemaphoreType.DMA((n, J))], name, async_id, token)


def _exchange_call(body, arrs, out_shape, sems, name, collective_id, token):
    n, m = len(arrs), len(out_shape)

    def seq_body(*refs):
        tok, tok_out, tok_sem = refs[n], refs[n + 1 + m], refs[-1]
        _handshake_all()
        cp = pltpu.make_async_copy(tok, tok_out, tok_sem)
        cp.start()
        body(*refs[:n], *refs[n + 1:n + 1 + m], *refs[n + 2 + m:-1])
        cp.wait()

    res = pl.kernel(
        seq_body, name=name, out_type=list(out_shape) + [jax.ShapeDtypeStruct(token.shape, token.dtype)],
        mesh=plsc.ScalarSubcoreMesh(**SEQUENCER), scratch_types=list(sems) + [pltpu.SemaphoreType.DMA],
        compiler_params=pltpu.CompilerParams(collective_id=collective_id))(*arrs, token)
    return res[:m], res[m]


def _send_chip_sums(arrs, name, async_id, token):
    n = len(arrs)

    def body(*refs):
        src, out = refs[:n], refs[n:2 * n]
        ssem, rsem = refs[2 * n:]
        x, y, c, _ = _where_am_i()

        @pl.when(c == 1)
        def _():
            cps = [_rc(src[i].at[1], out[i].at[0], ssem.at[i, 0], rsem.at[i, 0], (1 - x, y, 1)) for i in range(n)]
            for cp in cps:
                cp.start()
            for cp in cps:
                cp.wait()

        @pl.when(c == 0)
        def _():
            cps = [_rc(src[i].at[0], out[i].at[0], ssem.at[i, 0], rsem.at[i, 0], (x, 1 - y, 0)) for i in range(n)]
            cps += [_rc(src[i].at[1], out[i].at[1], ssem.at[i, 1], rsem.at[i, 1], (1 - x, 1 - y, 0)) for i in range(n)]
            for cp in cps:
                cp.start()
            for cp in cps:
                cp.wait()

    return _exchange_call(
        body, arrs, [jax.ShapeDtypeStruct(a.shape, a.dtype) for a in arrs],
        [pltpu.SemaphoreType.DMA((n, 2)), pltpu.SemaphoreType.DMA((n, 2))], name, async_id, token)


def _chip_sum(g, recv, sel, name):
    _, r, C = g.shape
    tr = _rtile(r)

    def body(sel_ref, g_ref, r_ref, o32_ref, o16_ref):
        s = g_ref[...] + r_ref[...]
        o32_ref[...] = s
        o16_ref[...] = s.astype(BF16)

    blk = pl.BlockSpec((None, tr, C), lambda j, t, sel_ref: (j, t, 0))
    return pl.pallas_call(
        body, name=name,
        grid_spec=pltpu.PrefetchScalarGridSpec(
            num_scalar_prefetch=1, grid=(2, r // tr),
            in_specs=[pl.BlockSpec((None, tr, C), lambda j, t, sel_ref: (sel_ref[j], t, 0)), blk],
            out_specs=[blk, blk]),
        out_shape=[jax.ShapeDtypeStruct((2, r, C), F32), jax.ShapeDtypeStruct((2, r, C), BF16)],
        compiler_params=_params(("arbitrary", "arbitrary")),
    )(sel, g, recv)


def _cross_sum(s32, got, north, after, name):
    _, r, C = s32.shape
    tr = _rtile(r)

    def body(f_ref, p_ref, a_ref, b_ref, after_ref, o_ref):
        a = a_ref[...].astype(F32)

        @pl.when(f_ref[0] == 1)
        def _():
            o_ref[...] = p_ref[...] + a

        @pl.when(f_ref[0] == 0)
        def _():
            o_ref[...] = a + b_ref[...].astype(F32)

    return pl.pallas_call(
        body, name=name,
        grid_spec=pltpu.PrefetchScalarGridSpec(
            num_scalar_prefetch=1, grid=(r // tr,),
            in_specs=[pl.BlockSpec((None, tr, C), lambda t, f: (0, t, 0)),
                      pl.BlockSpec((None, tr, C), lambda t, f: (0, t, 0)),
                      pl.BlockSpec((None, tr, C), lambda t, f: (1 - f[0], t, 0)), ANY],
            out_specs=pl.BlockSpec((tr, C), lambda t, f: (t, 0))),
        out_shape=jax.ShapeDtypeStruct((r, C), F32),
        compiler_params=_params(("arbitrary",)),
    )(north, s32, got, got, after)


class _Reducer:
    def __init__(self, token):
        self.token, self.pending, self.pairs = token, None, {}

    def begin(self, tag, ids, grads, after):
        names = list(grads)
        arrs = [grads[n].reshape(4, -1, D_MODEL) for n in names]
        x, y, c, (k_me, k_x, k_y, k_d) = _where_am_i()
        sel = jnp.stack([jnp.where(c == 1, k_me, k_y), jnp.where(c == 1, k_x, k_d)]).astype(jnp.int32)
        from_sib, self.token = _swap_slabs(arrs, 2, f"swap_grad_slabs_{tag}", ids[0], self.token)
        sums = [_chip_sum(g, r, sel, f"chip_sum_{tag}{i}") for i, (g, r) in enumerate(zip(arrs, from_sib))]
        self.finish(after)
        got, self.token = _send_chip_sums([s16 for _, s16 in sums], f"send_chip_sums_{tag}", ids[1], self.token)
        self.pending = (tag, ids[2], names, sums, got)
        return tuple(s16 for _, s16 in sums)

    def finish(self, after):
        if self.pending is None:
            return
        tag, cid, names, sums, got = self.pending
        north = jnp.reshape(lax.axis_index("c"), (1,)).astype(jnp.int32)
        parts = [_cross_sum(s32, b, north, after, f"cross_sum_{tag}{i}") for i, ((s32, _), b) in enumerate(zip(sums, got))]
        theirs, self.token = _swap_slabs(parts, 0, f"swap_parts_{tag}", cid, self.token)
        self.pairs.update(zip(names, zip(parts, theirs)))
        self.pending = None


def _all_reduce_small(v, token, collective_id):
    R, C = v.shape
    tr = R

    def body(v_ref, buf, ssem, rsem, lsem):
        x, y, c = lax.axis_index("x"), lax.axis_index("y"), lax.axis_index("c")
        me, sib = (x, y, c), (x, y, 1 - c)
        chips = [(1 - x, y), (x, 1 - y), (1 - x, 1 - y)]

        def slot(px, py, pc):
            return buf.at[4 * px + 2 * py + pc]

        def copy(k, block, to, src=None):
            return _rc(slot(*block) if src is None else src, slot(*block), ssem.at[k], rsem.at[k], to)

        mine = pltpu.make_async_copy(v_ref, slot(*me), lsem)
        mine.start()
        first = [copy(0, me, sib, src=v_ref)] + [copy(1 + j, me, (*chip, c), src=v_ref) for j, chip in enumerate(chips)]
        for cp in first:
            cp.start()
        passed = [copy(4 + j, (*chip, c), sib) for j, chip in enumerate(chips)]
        for j, chip in enumerate(chips):
            copy(1 + j, (*chip, c), me).wait_recv()
            passed[j].start()
        copy(0, sib, me).wait_recv()
        for j, chip in enumerate(chips):
            copy(4 + j, (*chip, 1 - c), me).wait_recv()
        for cp in first + passed:
            cp.wait_send()
        mine.wait()

    (gathered,), token = _exchange_call(
        body, [v], [jax.ShapeDtypeStruct((8, R, C), F32)],
        [pltpu.SemaphoreType.DMA((7,)), pltpu.SemaphoreType.DMA((7,)), pltpu.SemaphoreType.DMA],
        "gather_small", collective_id, token)

    def add(g_ref, o_ref):
        acc = g_ref[0]
        for k in range(1, 8):
            acc = acc + g_ref[k]
        o_ref[...] = acc

    total = pl.pallas_call(
        add, name="sum_small", grid=(R // tr,),
        in_specs=[pl.BlockSpec((8, tr, C), lambda i: (0, i, 0))], out_specs=pl.BlockSpec((tr, C), lambda i: (i, 0)),
        out_shape=jax.ShapeDtypeStruct((R, C), F32), compiler_params=_params(("parallel",)),
    )(gathered)
    return total, token


WEIGHTS = ['ffn1_norm', 'ffn1_w_gate', 'ffn1_w_up', 'ffn1_w_down', 'mix_norm', 'w_in', 'conv_w', 'conv_b', 's5_A_re',
           's5_A_im', 's5_log_dt', 's5_B_re', 's5_B_im', 's5_C_re', 's5_C_im', 's5_D', 's5_w_glu', 's5_b_glu',
           'ssd_A_log', 'ssd_dt_bias', 'ssd_D', 'ssd_norm', 'w_proj_s5', 'w_proj_ssd', 'b_gate', 'w_out', 'ffn2_norm',
           'ffn2_w_gate', 'ffn2_w_up', 'ffn2_w_down', 'final_norm']
ARGS = ['x'] + WEIGHTS + ['loss_target'] + ['m_' + n for n in WEIGHTS] + ['v_' + n for n in WEIGHTS]
COL_SHARDED = ('ffn1_w_gate', 'ffn1_w_up', 'w_in', 'ffn2_w_gate', 'ffn2_w_up')
ROW_SHARDED = ('ffn1_w_down', 's5_w_glu', 'w_proj_s5', 'w_proj_ssd', 'w_out', 'ffn2_w_down')
MATRICES = COL_SHARDED + ROW_SHARDED
GATHER_FIRST = ('ffn1_w_gate', 'ffn1_w_up')
GATHER_FIRST_LATER = ('ffn1_w_down',)
GATHER_SECOND = ('w_in', 'conv_w', 's5_w_glu', 'w_proj_s5', 'w_proj_ssd', 'w_out')
GATHER_THIRD = ('ffn2_w_gate', 'ffn2_w_up', 'ffn2_w_down')
SMALL = [n for n in WEIGHTS if n not in MATRICES]


def _pack(arrs, width=1024):
    flat = jnp.concatenate([a.reshape(-1).astype(F32) for a in arrs])
    rows = -(-flat.shape[0] // (8 * width)) * 8
    return jnp.pad(flat, (0, rows * width - flat.shape[0])).reshape(rows, width)


def _unpack(packed, shapes):
    flat, out, o = packed.reshape(-1), [], 0
    for s in shapes:
        n = math.prod(s)
        out.append(flat[o:o + n].reshape(s))
        o += n
    return out


def kernel(x, ffn1_norm, ffn1_w_gate, ffn1_w_up, ffn1_w_down, mix_norm, w_in, conv_w, conv_b, s5_A_re, s5_A_im, s5_log_dt, s5_B_re, s5_B_im, s5_C_re, s5_C_im, s5_D, s5_w_glu, s5_b_glu, ssd_A_log, ssd_dt_bias, ssd_D, ssd_norm, w_proj_s5, w_proj_ssd, b_gate, w_out, ffn2_norm, ffn2_w_gate, ffn2_w_up, ffn2_w_down, final_norm, loss_target, m_ffn1_norm, m_ffn1_w_gate, m_ffn1_w_up, m_ffn1_w_down, m_mix_norm, m_w_in, m_conv_w, m_conv_b, m_s5_A_re, m_s5_A_im, m_s5_log_dt, m_s5_B_re, m_s5_B_im, m_s5_C_re, m_s5_C_im, m_s5_D, m_s5_w_glu, m_s5_b_glu, m_ssd_A_log, m_ssd_dt_bias, m_ssd_D, m_ssd_norm, m_w_proj_s5, m_w_proj_ssd, m_b_gate, m_w_out, m_ffn2_norm, m_ffn2_w_gate, m_ffn2_w_up, m_ffn2_w_down, m_final_norm, v_ffn1_norm, v_ffn1_w_gate, v_ffn1_w_up, v_ffn1_w_down, v_mix_norm, v_w_in, v_conv_w, v_conv_b, v_s5_A_re, v_s5_A_im, v_s5_log_dt, v_s5_B_re, v_s5_B_im, v_s5_C_re, v_s5_C_im, v_s5_D, v_s5_w_glu, v_s5_b_glu, v_ssd_A_log, v_ssd_dt_bias, v_ssd_D, v_ssd_norm, v_w_proj_s5, v_w_proj_ssd, v_b_gate, v_w_out, v_ffn2_norm, v_ffn2_w_gate, v_ffn2_w_up, v_ffn2_w_down, v_final_norm):
    a = dict(locals())
    assert list(a) == ARGS
    x, target = a['x'][0], a['loss_target'][0]
    k_me = 2 * lax.axis_index("x") + lax.axis_index("y")

    src = {n: a[n][0].T.astype(BF16) for n in COL_SHARDED}
    src.update({n: a[n][0].astype(BF16) for n in ROW_SHARDED})
    src['conv_w'] = a['conv_w'][0]
    def flat(f):
        return f.reshape(-1, f.shape[-1])

    full = _gather_shards([src[n] for n in GATHER_FIRST])
    w = {n: flat(f) for n, f in zip(GATHER_FIRST, full)}
    got, token = _gather_shards_async(
        [src[n] for n in GATHER_FIRST_LATER], full[0][0, :8, :128].astype(F32), "gather_ffn1_down", 14)
    w.update({n: flat(f) for n, f in zip(GATHER_FIRST_LATER, got)})
    reducer = _Reducer(token)

    def gather_later(stage, after):
        token = reducer.token + after[:8, :128].astype(F32)
        calls = {"mixer": ((GATHER_SECOND[:2], "gather_mixer_in", 1), (GATHER_SECOND[2:], "gather_mixer", 12)),
                 "ffn2": ((GATHER_THIRD, "gather_ffn2", 2),)}[stage]
        out = {}
        for names, name, cid in calls:
            got, token = _gather_shards_async([src[n] for n in names], token, name, cid)
            out.update({n: flat(f) for n, f in zip(names, got)})
        reducer.token = token
        if 'conv_w' in out:
            out['conv_w'] = out['conv_w'].reshape(4, CONV_K, -1).transpose(1, 0, 2).reshape(CONV_K, CONV_DIM)
        return out

    p = {n: a[n] for n in SMALL if n != 'conv_w'}
    p['final_norm'] = a['final_norm'][None]
    ids = {"ffn2": (3, 4, 5), "mixer": (6, 7, 8), "ffn1": (9, 10, 11)}
    loss, grad_x, gp, conv_w_grad = _local_step(
        x, target, p, w, gather_later, lambda tag, grads, after: reducer.begin(tag, ids[tag], grads, after))
    gp['conv_w'] = conv_w_grad[None]
    small_shapes = [(1, 1)] + [gp[n].shape if n != 'final_norm' else (1, D_MODEL) for n in SMALL]
    summed, reducer.token = _all_reduce_small(_pack([loss] + [gp[n] for n in SMALL]), reducer.token, 13)
    red = _unpack(summed, small_shapes)
    loss_all = red[0].reshape(())
    gsmall = dict(zip(SMALL, red[1:]))
    gsmall['conv_w'] = lax.dynamic_slice_in_dim(gsmall['conv_w'], k_me * 1024, 1024, axis=2)
    gsmall = {n: g.reshape(a[n].shape) for n, g in gsmall.items()}

    grads, delta, new_m, new_v = {}, {}, {}, {}

    def update(names):
        for n in names:
            turn = (lambda t: jnp.swapaxes(t, 1, 2)) if n in COL_SHARDED else (lambda t: t)
            out = _adamw(turn(a[n]), turn(a['m_' + n]), turn(a['v_' + n]), list(reducer.pairs[n]),
                         name=f"adamw_{n}", emit_g=True)
            grads[n], delta[n], new_m[n], new_v[n] = [turn(t) for t in out]

    update([n for n in MATRICES if n not in GATHER_FIRST + GATHER_FIRST_LATER])
    reducer.finish(delta['w_in'])
    update(GATHER_FIRST + GATHER_FIRST_LATER)
    sw, sm, sv, sg = [_pack([t[n] for n in SMALL])[None] for t in (a, {n: a['m_' + n] for n in SMALL},
                                                                   {n: a['v_' + n] for n in SMALL}, gsmall)]
    d, m2, v2 = _adamw(sw, sm, sv, [sg[0]], name="adamw_small", emit_g=False)
    shapes = [a[n].shape for n in SMALL]
    for n, dd, mm, vv in zip(SMALL, _unpack(d[0], shapes), _unpack(m2[0], shapes), _unpack(v2[0], shapes)):
        grads[n], delta[n], new_m[n], new_v[n] = gsmall[n], dd, mm, vv
    return (loss_all, grad_x[None], *[grads[n] for n in WEIGHTS], *[delta[n] for n in WEIGHTS],
            *[new_m[n] for n in WEIGHTS], *[new_v[n] for n in WEIGHTS])
```

```python
import itertools
import math

import jax
import jax.numpy as jnp
from jax import lax
from jax.experimental import pallas as pl
from jax.experimental.pallas import tpu as pltpu
from jax.experimental.pallas import tpu_sc as plsc

F32 = jnp.float32
BF16 = jnp.bfloat16
MXU = BF16
HI = lax.Precision.HIGHEST

D_MODEL = 1024
D_FF = 2816
EPS = 1e-6
S5_GROUPS, S5_GROUP, S5_STATE = 64, 16, 64
D_INNER = 2048
SSD_HEADDIM, SSD_HEADS, SSD_GROUPS, SSD_HPG, SSD_STATE = 64, 32, 8, 4, 128
CONV_K, CONV_DIM = 4, 4096
IN_OFFS = (0, 1024, 3072, 7168, 7200, 9248)
SSD_Q = 256
SSD_STEP_GROUPS = 4
ADAM_LR, ADAM_B1, ADAM_B2, ADAM_EPS, ADAM_WD, ADAM_STEP = 0.001, 0.9, 0.999, 1e-08, 0.01, 10

VMEM_LIMIT = 56 * 1024 * 1024
MATMUL_VMEM = 32 * 1024 * 1024
MESH = pl.DeviceIdType.MESH

NN = ((1,), (0,))
NT = ((1,), (1,))
TN = ((0,), (0,))


def _dot(a, b, dims, precision=None):
    if precision is None:
        a, b = a.astype(MXU), b.astype(MXU)
    return lax.dot_general(a, b, (dims, ((), ())), precision=precision, preferred_element_type=F32)


def _tile(n, pref):
    if n <= pref:
        return n
    best = None
    for t in range(128, pref + 1, 128):
        if n % t == 0:
            best = t
    assert best is not None, (n, pref)
    return best


def _params(sem):
    return pltpu.CompilerParams(dimension_semantics=sem, vmem_limit_bytes=VMEM_LIMIT)


def _matmul(a, b, mode, *, name, out=((F32),), epilogue=None, extras=(), after=(), tm=512, tn=1536):
    if mode == "nn":
        (M, K), (_, N) = a.shape, b.shape
    elif mode == "nt":
        (M, K), (N, _) = a.shape, b.shape
    else:
        (K, M), (_, N) = a.shape, b.shape
    tm, tn = _tile(M, tm), _tile(N, tn)

    def vmem(tm_, tn_):
        per = K * tm_ * a.dtype.itemsize + K * tn_ * b.dtype.itemsize
        per += sum(tm_ * tn_ * jnp.dtype(dt).itemsize for dt in out)
        per += sum((1 if e.shape[0] == 1 else tm_) * tn_ * e.dtype.itemsize for e in extras)
        return 2 * per

    while vmem(tm, tn) > MATMUL_VMEM and (tn > 128 or tm > 128):
        if tn >= tm and tn > 128:
            tn = _tile(N, tn - 128)
        else:
            tm = _tile(M, tm - 128)
    bytes_a, bytes_b = a.size * a.dtype.itemsize, b.size * b.dtype.itemsize
    rows_outer = bytes_a + (M // tm) * bytes_b <= (N // tn) * bytes_a + bytes_b
    grid = (M // tm, N // tn) if rows_outer else (N // tn, M // tm)

    def at(f):
        return (lambda i, j: f(i, j)) if rows_outer else (lambda j, i: f(i, j))

    a_spec = pl.BlockSpec((K, tm), at(lambda i, j: (0, i))) if mode == "tn" else pl.BlockSpec((tm, K), at(lambda i, j: (i, 0)))
    b_spec = pl.BlockSpec((tn, K), at(lambda i, j: (j, 0))) if mode == "nt" else pl.BlockSpec((K, tn), at(lambda i, j: (0, j)))
    dims = {"nn": NN, "nt": NT, "tn": TN}[mode]
    e_specs = []
    for e in extras:
        if e.shape[0] == 1:
            e_specs.append(pl.BlockSpec((1, tn), at(lambda i, j: (0, j))))
        else:
            assert e.shape == (M, N), (e.shape, M, N)
            e_specs.append(pl.BlockSpec((tm, tn), at(lambda i, j: (i, j))))
    n_e, n_o, n_a = len(extras), len(out), len(after)

    def body(a_ref, b_ref, *refs):
        acc = _dot(a_ref[...], b_ref[...], dims)
        res = (acc,) if epilogue is None else epilogue(acc, *[r[...] for r in refs[:n_e]])
        for r, v in zip(refs[n_e + n_a:], res):
            r[...] = v.astype(r.dtype)

    res = pl.pallas_call(
        body, name=name, grid=grid,
        in_specs=[a_spec, b_spec] + e_specs + [ANY] * n_a,
        out_specs=[pl.BlockSpec((tm, tn), at(lambda i, j: (i, j))) for _ in range(n_o)],
        out_shape=[jax.ShapeDtypeStruct((M, N), dt) for dt in out],
        compiler_params=_params(("parallel", "parallel")),
    )(a, b, *extras, *after)
    return res[0] if n_o == 1 else res


def _rowwise(fn, rows, fulls, outs, reds=(), *, name, after=(), tm=256):
    T = rows[0].shape[0]
    tm = min(tm, T)
    n_r, n_f, n_o, n_d, n_a = len(rows), len(fulls), len(outs), len(reds), len(after)

    def body(*refs):
        ins = [r[...] for r in refs[:n_r + n_f]]
        o_refs = refs[n_r + n_f + n_a:n_r + n_f + n_a + n_o]
        d_refs = refs[n_r + n_f + n_a + n_o:]
        res = fn(*ins)
        for r, v in zip(o_refs, res[:n_o]):
            r[...] = v.astype(r.dtype)
        if n_d:
            @pl.when(pl.program_id(0) == 0)
            def _():
                for r in d_refs:
                    r[...] = jnp.zeros_like(r)
            for r, v in zip(d_refs, res[n_o:]):
                r[...] += v.astype(r.dtype)

    res = pl.pallas_call(
        body, name=name, grid=(T // tm,),
        in_specs=[pl.BlockSpec((tm, r.shape[1]), lambda i: (i, 0)) for r in rows]
        + [pl.BlockSpec(f.shape, lambda i, nd=f.ndim: (0,) * nd) for f in fulls] + [ANY] * n_a,
        out_specs=[pl.BlockSpec((tm, c), lambda i: (i, 0)) for c, _ in outs]
        + [pl.BlockSpec(s, lambda i, nd=len(s): (0,) * nd) for s, _ in reds],
        out_shape=[jax.ShapeDtypeStruct((T, c), dt) for c, dt in outs]
        + [jax.ShapeDtypeStruct(s, dt) for s, dt in reds],
        compiler_params=_params(("arbitrary",)),
    )(*rows, *fulls, *after)
    return res


def _rms(x, g):
    return x * lax.rsqrt(jnp.mean(x * x, axis=-1, keepdims=True) + EPS) * g


def _colsum(v):
    return jnp.sum(v, axis=0, keepdims=True)


def _softplus(x):
    return jnp.maximum(x, 0.0) + jnp.log1p(jnp.exp(-jnp.abs(x)))


def _ffn_fwd(x, norm, wgT, wuT, wd, tag, on_gate=None):
    D = x.shape[1]
    (hn,) = _rowwise(lambda xt, g: (_rms(xt, g),), [x], [norm], [(D, BF16)], name=f"{tag}_norm")
    a = _matmul(hn, wgT, "nt", name=f"{tag}_gate", out=(BF16,))
    if on_gate is not None:
        on_gate(a)
    b, hid = _matmul(hn, wuT, "nt", name=f"{tag}_up", out=(BF16, BF16),
                     epilogue=lambda acc, at: (acc, jax.nn.silu(at.astype(F32)) * acc), extras=[a])
    y = _matmul(hid, wd, "nn", name=f"{tag}_down", epilogue=lambda acc, xt: (xt + 0.5 * acc,), extras=[x])
    return y, (hn, a, b, hid)


def _ffn_bwd(dy, dy16, x, norm, wgT, wuT, wd, res, tag, after, on_grads):
    hn, a, b, hid = res
    D, F = x.shape[1], a.shape[1]

    def act_bwd(acc, at, bt):
        _, vjp = jax.vjp(lambda p, q: jax.nn.silu(p) * q, at.astype(F32), bt.astype(F32))
        return vjp(0.5 * acc)

    da, db = _matmul(dy16, wd, "nt", name=f"{tag}_d_hid", out=(BF16, BF16), epilogue=act_bwd, extras=[a, b])
    d_wd = _matmul(hid, dy16, "tn", name=f"{tag}_d_wd", epilogue=lambda acc: (0.5 * acc,))
    d_wgT = _matmul(da, hn, "tn", name=f"{tag}_d_wg", after=after)
    d_wuT = _matmul(db, hn, "tn", name=f"{tag}_d_wu")
    started = on_grads(d_wgT, d_wuT, d_wd, dy)
    dhn = _matmul(da, wgT, "nn", name=f"{tag}_d_hn1")
    dhn = _matmul(db, wuT, "nn", name=f"{tag}_d_hn2", epilogue=lambda acc, e: (acc + e,), extras=[dhn])

    def norm_bwd(xt, dh, dyt, g):
        _, vjp = jax.vjp(_rms, xt, g)
        dx, dg = vjp(dh)
        return dyt + dx, dyt + dx, dg

    dx, dx16, d_norm = _rowwise(norm_bwd, [x, dhn, dy], [norm], [(D, F32), (D, BF16)], [((1, D), F32)],
                                name=f"{tag}_d_norm", after=started)
    return dx, dx16, d_norm


S5_TILES = 8
S5_HALF = 256


def _s5_derive(A_re, A_im, log_dt, B_re, B_im, C_re, C_im):
    G, N, M = S5_GROUPS, S5_STATE, S5_GROUP
    dt = jnp.exp(log_dt)[:, None]
    mag = jnp.exp(A_re * dt)
    ar, ai = mag * jnp.cos(A_im * dt), mag * jnp.sin(A_im * dt)
    den = A_re * A_re + A_im * A_im
    cr = ((ar - 1.0) * A_re + ai * A_im) / den
    ci = (ai * A_re - (ar - 1.0) * A_im) / den
    bbr = cr[..., None] * B_re - ci[..., None] * B_im
    bbi = cr[..., None] * B_im + ci[..., None] * B_re
    eye = jnp.eye(8, dtype=F32)

    def tile_in(bb):
        t = bb.reshape(S5_TILES, 8, N, M).transpose(0, 1, 3, 2)
        return jnp.einsum("jamn,ab->jambn", t, eye).reshape(S5_TILES, 8 * M, 8 * N)

    def tile_out(c):
        t = c.reshape(S5_TILES, 8, M, N).transpose(0, 1, 3, 2)
        return jnp.einsum("janm,ab->janbm", t, eye).reshape(S5_TILES, 8 * N, 8 * M)

    return (tile_in(bbr), tile_in(bbi), tile_out(C_re), tile_out(C_im),
            ar.reshape(S5_TILES, 1, 8 * N), ai.reshape(S5_TILES, 1, 8 * N))


S5_NB = S5_HALF // 128
S5_SEG = 32


def _cmul(ar, ai, br, bi):
    return ar * br - ai * bi, ar * bi + ai * br


def _s5_scan(sr_ref, si_ref, ar, ai, T, reverse):
    L, V = T // S5_SEG, S5_SEG // 8
    assert L * S5_SEG == T and L & (L - 1) == 0, T
    sg = -1.0 if reverse else 1.0
    a_r = [jnp.broadcast_to(ar[:, 128 * b:128 * (b + 1)], (8, 128)) for b in range(S5_NB)]
    a_i = [jnp.broadcast_to(sg * ai[:, 128 * b:128 * (b + 1)], (8, 128)) for b in range(S5_NB)]

    def rows(k, v):
        return pl.ds(pl.multiple_of(((L - 1 - k) if reverse else k) * S5_SEG + 8 * v, 8), 8)

    def local(k, carry):
        out = []
        for b in range(S5_NB):
            for v in range(V):
                idx = rows(k, v)
                mr, mi = _cmul(a_r[b], a_i[b], *carry[b * V + v])
                nr, ni = mr + sr_ref[b, idx, :], mi + si_ref[b, idx, :]
                sr_ref[b, idx, :] = nr
                si_ref[b, idx, :] = ni
                out.append((nr, ni))
        return tuple(out)

    z = jnp.zeros((8, 128), F32)
    ends = lax.fori_loop(0, L, local, tuple((z, z) for _ in range(S5_NB * V)))

    carries = []
    for b in range(S5_NB):
        pr, pi = a_r[b][0:1], a_i[b][0:1]
        n = L
        while n > 1:
            pr, pi = _cmul(pr, pi, pr, pi)
            n //= 2
        cr, ci = jnp.zeros((1, 128), F32), jnp.zeros((1, 128), F32)
        into = [None] * S5_SEG
        for j in (reversed(range(S5_SEG)) if reverse else range(S5_SEG)):
            into[j] = (cr, ci)
            er, ei = ends[b * V + j // 8]
            mr, mi = _cmul(pr, pi, cr, ci)
            cr, ci = mr + er[j % 8:j % 8 + 1], mi + ei[j % 8:j % 8 + 1]
        carries.append([(jnp.concatenate([into[8 * v + s][0] for s in range(8)], axis=0),
                         jnp.concatenate([into[8 * v + s][1] for s in range(8)], axis=0)) for v in range(V)])

    def fix(k, powers):
        out = []
        for b in range(S5_NB):
            pr, pi = powers[b]
            for v in range(V):
                idx = rows(k, v)
                dr, di = _cmul(pr, pi, *carries[b][v])
                sr_ref[b, idx, :] += dr
                si_ref[b, idx, :] += di
            out.append(_cmul(pr, pi, a_r[b], a_i[b]))
        return tuple(out)

    lax.fori_loop(0, L, fix, tuple((a_r[b], a_i[b]) for b in range(S5_NB)))


def _s5_in(ut, b_ref, s_ref):
    for b in range(S5_NB):
        s_ref[b] = _dot(ut, b_ref[:, 128 * b:128 * (b + 1)], NN)


def _to_seg(v):
    T, C = v.shape
    return v.reshape(S5_SEG, T // S5_SEG, C).transpose(1, 0, 2).reshape(T, C)


def _from_seg(v):
    T, C = v.shape
    return v.reshape(T // S5_SEG, S5_SEG, C).transpose(1, 0, 2).reshape(T, C)


def _s5_specs(T):
    u_spec = pl.BlockSpec((T, 128), lambda j, h: (0, j))
    b_spec = pl.BlockSpec((None, 128, S5_HALF), lambda j, h: (j, 0, h))
    c_spec = pl.BlockSpec((None, S5_HALF, 128), lambda j, h: (j, h, 0))
    a_spec = pl.BlockSpec((None, 1, S5_HALF), lambda j, h: (j, 0, h))
    return u_spec, b_spec, c_spec, a_spec


def _s5_fwd(u, tiles):
    T = u.shape[0]
    u_spec, b_spec, c_spec, a_spec = _s5_specs(T)

    def body(u_ref, br_ref, bi_ref, cr_ref, ci_ref, ar_ref, ai_ref, y_ref, sr_ref, si_ref):
        ut = u_ref[...].astype(MXU)
        _s5_in(ut, br_ref, sr_ref)
        _s5_in(ut, bi_ref, si_ref)
        _s5_scan(sr_ref, si_ref, ar_ref[...], ai_ref[...], T, False)
        y = None
        for b in range(S5_NB):
            blk = slice(128 * b, 128 * (b + 1))
            yb = _dot(sr_ref[b], cr_ref[blk, :], NN) - _dot(si_ref[b], ci_ref[blk, :], NN)
            y = yb if y is None else y + yb

        @pl.when(pl.program_id(1) == 0)
        def _():
            y_ref[...] = y

        @pl.when(pl.program_id(1) == 1)
        def _():
            y_ref[...] += y

    scr = pltpu.VMEM((S5_NB, T, 128), F32)
    return pl.pallas_call(
        body, name="s5_fwd", grid=(S5_TILES, 2),
        in_specs=[u_spec, b_spec, b_spec, c_spec, c_spec, a_spec, a_spec],
        out_specs=u_spec, out_shape=jax.ShapeDtypeStruct(u.shape, F32),
        scratch_shapes=[scr, scr],
        compiler_params=_params(("parallel", "arbitrary")),
    )(u, *tiles)


def _s5_bwd(u, dys, du_skip, tiles):
    T = u.shape[0]
    u_spec, b_spec, c_spec, a_spec = _s5_specs(T)

    def body(u_ref, dy_ref, sk_ref, br_ref, bi_ref, cr_ref, ci_ref, ar_ref, ai_ref,
             du_ref, dbr_ref, dbi_ref, dcr_ref, dci_ref, dar_ref, dai_ref, sr_ref, si_ref, lr_ref, li_ref):
        ut, dy = u_ref[...].astype(MXU), dy_ref[...].astype(MXU)
        ar, ai = ar_ref[...], ai_ref[...]
        _s5_in(ut, br_ref, sr_ref)
        _s5_in(ut, bi_ref, si_ref)
        _s5_scan(sr_ref, si_ref, ar, ai, T, False)
        for b in range(S5_NB):
            blk = slice(128 * b, 128 * (b + 1))
            dcr_ref[blk, :] = _dot(sr_ref[b], dy, TN)
            dci_ref[blk, :] = -_dot(si_ref[b], dy, TN)
            lr_ref[b] = _dot(dy, cr_ref[blk, :], NT)
            li_ref[b] = -_dot(dy, ci_ref[blk, :], NT)
        _s5_scan(lr_ref, li_ref, ar, ai, T, True)
        du = None
        head = lax.broadcasted_iota(jnp.int32, (T, 128), 0) < S5_SEG
        seg0 = lax.broadcasted_iota(jnp.int32, (S5_SEG, 128), 0) == 0
        for b in range(S5_NB):
            blk = slice(128 * b, 128 * (b + 1))
            lr, li = lr_ref[b], li_ref[b]
            pr = jnp.where(head, 0.0, pltpu.roll(sr_ref[b], S5_SEG, 0))
            pi = jnp.where(head, 0.0, pltpu.roll(si_ref[b], S5_SEG, 0))
            er = jnp.where(seg0, 0.0, pltpu.roll(sr_ref[b, T - S5_SEG:T, :], 1, 0))
            ei = jnp.where(seg0, 0.0, pltpu.roll(si_ref[b, T - S5_SEG:T, :], 1, 0))
            hr, hi = lr[0:S5_SEG], li[0:S5_SEG]
            dar_ref[:, blk] = _colsum(lr * pr + li * pi) + _colsum(hr * er + hi * ei)
            dai_ref[:, blk] = _colsum(li * pr - lr * pi) + _colsum(hi * er - hr * ei)
            dbr_ref[:, blk] = _dot(ut, lr, TN)
            dbi_ref[:, blk] = _dot(ut, li, TN)
            dub = _dot(lr, br_ref[:, blk], NT) + _dot(li, bi_ref[:, blk], NT)
            du = dub if du is None else du + dub

        @pl.when(pl.program_id(1) == 0)
        def _():
            du_ref[...] = du + sk_ref[...]

        @pl.when(pl.program_id(1) == 1)
        def _():
            du_ref[...] += du

    scr = pltpu.VMEM((S5_NB, T, 128), F32)
    return pl.pallas_call(
        body, name="s5_bwd", grid=(S5_TILES, 2),
        in_specs=[u_spec, u_spec, u_spec, b_spec, b_spec, c_spec, c_spec, a_spec, a_spec],
        out_specs=[u_spec, b_spec, b_spec, c_spec, c_spec, a_spec, a_spec],
        out_shape=[jax.ShapeDtypeStruct(u.shape, F32)] + [jax.ShapeDtypeStruct(t.shape, F32) for t in tiles],
        scratch_shapes=[scr, scr, scr, scr],
        compiler_params=_params(("parallel", "arbitrary")),
    )(u, dys, du_skip, *tiles)


CONV_TILE = 256


def _conv_pre(x, w, b):
    T = x.shape[0]
    row = lax.broadcasted_iota(jnp.int32, x.shape, 0)
    acc = x * w[CONV_K - 1:CONV_K, :] + b
    for lag in range(1, CONV_K):
        acc = acc + jnp.where(row >= lag, pltpu.roll(x, lag, 0), 0.0) * w[CONV_K - 1 - lag:CONV_K - lag, :]
    return acc


def _conv_fwd(x, w, b):
    T, C = x.shape
    col = pl.BlockSpec((T, CONV_TILE), lambda j: (0, j))

    def body(x_ref, w_ref, b_ref, o_ref):
        o_ref[...] = jax.nn.silu(_conv_pre(x_ref[...], w_ref[...], b_ref[...]))

    return pl.pallas_call(
        body, name="conv_fwd", grid=(C // CONV_TILE,),
        in_specs=[col, pl.BlockSpec((CONV_K, CONV_TILE), lambda j: (0, j)), pl.BlockSpec((1, CONV_TILE), lambda j: (0, j))],
        out_specs=col, out_shape=jax.ShapeDtypeStruct((T, C), F32),
        compiler_params=_params(("parallel",)),
    )(x, w, b)


def _conv_bwd(x, w, b, dout, col0, name):
    T, C = dout.shape
    off = col0 // CONV_TILE
    xcol = pl.BlockSpec((T, CONV_TILE), lambda j: (0, j + off))
    dcol = pl.BlockSpec((T, CONV_TILE), lambda j: (0, j))

    def body(x_ref, w_ref, b_ref, d_ref, dx_ref, dw_ref, db_ref):
        x, w = x_ref[...], w_ref[...]
        _, vjp = jax.vjp(jax.nn.silu, _conv_pre(x, w, b_ref[...]))
        (dy,) = vjp(d_ref[...])
        row = lax.broadcasted_iota(jnp.int32, x.shape, 0)
        dx = dy * w[CONV_K - 1:CONV_K, :]
        dw_ref[CONV_K - 1:CONV_K, :] = _colsum(dy * x)
        for lag in range(1, CONV_K):
            ahead = jnp.where(row < T - lag, pltpu.roll(dy, T - lag, 0), 0.0)
            dx = dx + ahead * w[CONV_K - 1 - lag:CONV_K - lag, :]
            dw_ref[CONV_K - 1 - lag:CONV_K - lag, :] = _colsum(ahead * x)
        dx_ref[...] = dx.astype(dx_ref.dtype)
        db_ref[...] = _colsum(dy)

    return pl.pallas_call(
        body, name=name, grid=(C // CONV_TILE,),
        in_specs=[xcol, pl.BlockSpec((CONV_K, CONV_TILE), lambda j: (0, j + off)),
                  pl.BlockSpec((1, CONV_TILE), lambda j: (0, j + off)), dcol],
        out_specs=[dcol, pl.BlockSpec((CONV_K, CONV_TILE), lambda j: (0, j)), pl.BlockSpec((1, CONV_TILE), lambda j: (0, j))],
        out_shape=[jax.ShapeDtypeStruct((T, C), BF16), jax.ShapeDtypeStruct((CONV_K, C), F32), jax.ShapeDtypeStruct((1, C), F32)],
        compiler_params=_params(("parallel",)),
    )(x, w, b, dout)


def _ssd_common(x_ref, b_ref, c_ref, dtr_ref, bias_ref, alog_ref, tri_ref, triu_ref):
    Q = x_ref.shape[0]
    x, Bm, Cm = x_ref[...], b_ref[...], c_ref[...]
    pre = dtr_ref[...] + bias_ref[...]
    dt = _softplus(pre)
    A = -jnp.exp(alog_ref[...])
    adt = dt * A
    ac4 = _dot(tri_ref[...], adt, NN, HI)
    ar4 = _dot(adt, triu_ref[...], TN, HI)
    cb = _dot(Cm, Bm, NT)
    ii = lax.broadcasted_iota(jnp.int32, (Q, Q), 0)
    jj = lax.broadcasted_iota(jnp.int32, (Q, Q), 1)
    atot4 = ac4[Q - 1:Q, :]
    return x, Bm, Cm, pre, dt, A, ac4, ar4, cb, ii >= jj, atot4


def _ssd_decay(k, ac4, ar4, causal):
    seg = ac4[:, k:k + 1] - ar4[k:k + 1, :]
    return jnp.where(causal, jnp.exp(jnp.where(causal, seg, 0.0)), 0.0)


def _per_head(c4, n):
    lane = lax.broadcasted_iota(jnp.int32, (c4.shape[0], n), 1)
    out = jnp.broadcast_to(c4[:, 0:1], (c4.shape[0], n))
    for k in range(1, SSD_HPG):
        out = jnp.where(lane >= SSD_HEADDIM * k, c4[:, k:k + 1], out)
    return out


def _head_sums(v):
    n = SSD_HPG * SSD_HEADDIM
    row = lax.broadcasted_iota(jnp.int32, (n, SSD_HPG), 0)
    col = lax.broadcasted_iota(jnp.int32, (n, SSD_HPG), 1)
    member = jnp.where((row >= SSD_HEADDIM * col) & (row < SSD_HEADDIM * (col + 1)), 1.0, 0.0)
    return _dot(v, member, NN, HI)


def _head_rows(c4):
    n = SSD_HPG * SSD_HEADDIM
    row = lax.broadcasted_iota(jnp.int32, (n, 1), 0)
    out = jnp.broadcast_to(c4[:, 0:1], (n, 1))
    for k in range(1, SSD_HPG):
        out = jnp.where(row >= SSD_HEADDIM * k, c4[:, k:k + 1], out)
    return out


def _ssd_specs(T, Q, rev):
    NC = T // Q
    cc = (lambda c: NC - 1 - c) if rev else (lambda c: c)
    W, N, S = SSD_HPG * SSD_HEADDIM, SSD_STATE, SSD_STEP_GROUPS
    x_spec = pl.BlockSpec((Q, S * W), lambda g, c: (cc(c), g))
    b_spec = pl.BlockSpec((Q, S * N), lambda g, c: (cc(c), D_INNER // (S * N) + g))
    c_spec = pl.BlockSpec((Q, S * N), lambda g, c: (cc(c), (D_INNER + SSD_GROUPS * N) // (S * N) + g))
    dt_spec = pl.BlockSpec((S, Q, SSD_HPG), lambda g, c: (g, cc(c), 0))
    p_spec = pl.BlockSpec((S, 1, SSD_HPG), lambda g, c: (g, 0, 0))
    tri_spec = pl.BlockSpec((Q, Q), lambda g, c: (0, 0))
    h_spec = pl.BlockSpec((S, None, W, N), lambda g, c: (g, cc(c), 0, 0))
    return x_spec, b_spec, c_spec, dt_spec, p_spec, tri_spec, h_spec


def _ssd_views(gi, wide, narrow, lead):
    W, N = SSD_HPG * SSD_HEADDIM, SSD_STATE
    return ([r.at[:, pl.ds(W * gi, W)] for r in wide] + [r.at[:, pl.ds(N * gi, N)] for r in narrow]
            + [r.at[gi] for r in lead])


def _tri(Q):
    tri = jnp.tril(jnp.ones((Q, Q), F32))
    return tri, tri.T


def _ssd_fwd(xc, dtr8, bias8, alog8, dsk8):
    T, Q = xc.shape[0], SSD_Q
    NC = T // Q
    P, W = SSD_HEADDIM, SSD_HPG * SSD_HEADDIM
    x_spec, b_spec, c_spec, dt_spec, p_spec, tri_spec, h_spec = _ssd_specs(T, Q, False)

    def body(x_ref, b_ref, c_ref, dtr_ref, bias_ref, alog_ref, dsk_ref, tri_ref, triu_ref, y_ref, hs_ref, h_scr):
        for gi in range(SSD_STEP_GROUPS):
            xv, yv, bv, cv, dtv, biasv, alogv, dskv, hsv, hv = _ssd_views(
                gi, (x_ref, y_ref), (b_ref, c_ref), (dtr_ref, bias_ref, alog_ref, dsk_ref, hs_ref, h_scr))
            one(xv, bv, cv, dtv, biasv, alogv, dskv, tri_ref, triu_ref, yv, hsv, hv)

    def one(x_ref, b_ref, c_ref, dtr_ref, bias_ref, alog_ref, dsk_ref, tri_ref, triu_ref, y_ref, hs_ref, h_scr):
        @pl.when(pl.program_id(1) == 0)
        def _():
            h_scr[...] = jnp.zeros_like(h_scr)

        hin = h_scr[...]
        hs_ref[...] = hin
        x, Bm, Cm, pre, dt, A, ac4, ar4, cb, causal, atot4 = _ssd_common(
            x_ref, b_ref, c_ref, dtr_ref, bias_ref, alog_ref, tri_ref, triu_ref)
        xdt = x * _per_head(dt, W)
        rest = _dot(Cm, hin, NT) * _per_head(jnp.exp(ac4), W) + _per_head(dsk_ref[...], W) * x
        for k in range(SSD_HPG):
            blk = slice(P * k, P * (k + 1))
            G = cb * _ssd_decay(k, ac4, ar4, causal)
            y_ref[:, blk] = _dot(G, xdt[:, blk], NN) + rest[:, blk]
        xw = xdt * _per_head(jnp.exp(atot4 - ac4), W)
        h_scr[...] = _head_rows(jnp.exp(atot4)) * hin + _dot(xw, Bm, TN)

    tri, triu = _tri(Q)
    return pl.pallas_call(
        body, name="ssd_fwd", grid=(SSD_GROUPS // SSD_STEP_GROUPS, NC),
        in_specs=[x_spec, b_spec, c_spec, dt_spec, p_spec, p_spec, p_spec, tri_spec, tri_spec],
        out_specs=[pl.BlockSpec((Q, SSD_STEP_GROUPS * W), lambda g, c: (c, g)), h_spec],
        out_shape=[jax.ShapeDtypeStruct((T, D_INNER), F32),
                   jax.ShapeDtypeStruct((SSD_GROUPS, NC, W, SSD_STATE), F32)],
        scratch_shapes=[pltpu.VMEM((SSD_STEP_GROUPS, W, SSD_STATE), F32)],
        compiler_params=_params(("parallel", "arbitrary")),
    )(xc, xc, xc, dtr8, bias8, alog8, dsk8, tri, triu)


def _ssd_bwd(xc, dtr8, bias8, alog8, dsk8, hs, dy):
    T, Q = xc.shape[0], SSD_Q
    NC = T // Q
    P, W = SSD_HEADDIM, SSD_HPG * SSD_HEADDIM
    x_spec, b_spec, c_spec, dt_spec, p_spec, tri_spec, h_spec = _ssd_specs(T, Q, True)
    dy_spec = pl.BlockSpec((Q, SSD_STEP_GROUPS * W), lambda g, c: (NC - 1 - c, g))
    dbc_spec = pl.BlockSpec((Q, SSD_STEP_GROUPS * SSD_STATE), lambda g, c: (NC - 1 - c, g))

    def body(x_ref, b_ref, c_ref, dtr_ref, bias_ref, alog_ref, dsk_ref, tri_ref, triu_ref, hs_ref, dy_ref,
             dx_ref, db_ref, dc_ref, ddtr_ref, dbias_ref, dalog_ref, ddsk_ref, dh_scr):
        for gi in range(SSD_STEP_GROUPS):
            (xv, dyv, dxv, bv, cv, dbv, dcv, dtv, biasv, alogv, dskv, hsv, ddtv, dbiasv, dalogv, ddskv, dhv) = _ssd_views(
                gi, (x_ref, dy_ref, dx_ref), (b_ref, c_ref, db_ref, dc_ref),
                (dtr_ref, bias_ref, alog_ref, dsk_ref, hs_ref, ddtr_ref, dbias_ref, dalog_ref, ddsk_ref, dh_scr))
            one(xv, bv, cv, dtv, biasv, alogv, dskv, tri_ref, triu_ref, hsv, dyv,
                dxv, dbv, dcv, ddtv, dbiasv, dalogv, ddskv, dhv)

    def one(x_ref, b_ref, c_ref, dtr_ref, bias_ref, alog_ref, dsk_ref, tri_ref, triu_ref, hs_ref, dy_ref,
            dx_ref, db_ref, dc_ref, ddtr_ref, dbias_ref, dalog_ref, ddsk_ref, dh_scr):
        @pl.when(pl.program_id(1) == 0)
        def _():
            dh_scr[...] = jnp.zeros_like(dh_scr)
            dbias_ref[...] = jnp.zeros_like(dbias_ref)
            dalog_ref[...] = jnp.zeros_like(dalog_ref)
            ddsk_ref[...] = jnp.zeros_like(ddsk_ref)

        x, Bm, Cm, pre, dt, A, ac4, ar4, cb, causal, atot4 = _ssd_common(
            x_ref, b_ref, c_ref, dtr_ref, bias_ref, alog_ref, tri_ref, triu_ref)
        dyv, hin, dho = dy_ref[...], hs_ref[...], dh_scr[...]
        dt_w = _per_head(dt, W)
        xdt = x * dt_w
        E4, F4, etot4 = jnp.exp(ac4), jnp.exp(atot4 - ac4), jnp.exp(atot4)
        F_w = _per_head(F4, W)
        ddsk_ref[...] += _head_sums(_colsum(dyv * x))
        Z = _dot(Cm, hin, NT)
        dZ = dyv * _per_head(E4, W)
        dac4 = _head_sums(dyv * Z) * E4
        dC = _dot(dZ, hin, NN)
        dh_scr[...] = _dot(dZ, Cm, TN) + _head_rows(etot4) * dho
        per_row = jnp.sum(dho * hin, axis=1, keepdims=True)
        lane4 = lax.broadcasted_iota(jnp.int32, (1, SSD_HPG), 1)
        datot4 = jnp.zeros((1, SSD_HPG), F32)
        for k in range(SSD_HPG):
            datot4 = jnp.where(lane4 == k, jnp.sum(per_row[P * k:P * (k + 1)], keepdims=True), datot4)
        datot4 = datot4 * etot4
        dxw = _dot(Bm, dho, NT)
        dB = _dot(xdt * F_w, dho, NN)
        dFa4 = _head_sums(dxw * xdt) * F4
        datot4 = datot4 + _colsum(dFa4)
        dac4 = dac4 - dFa4
        dcb = jnp.zeros((Q, Q), F32)
        lane_q4 = lax.broadcasted_iota(jnp.int32, (Q, SSD_HPG), 1)
        sub_4q = lax.broadcasted_iota(jnp.int32, (SSD_HPG, Q), 0)
        row_sums = jnp.zeros((Q, SSD_HPG), F32)
        col_sums = jnp.zeros((SSD_HPG, Q), F32)
        dxdt_heads = []
        for k in range(SSD_HPG):
            blk = slice(P * k, P * (k + 1))
            L = _ssd_decay(k, ac4, ar4, causal)
            G = cb * L
            dG = _dot(dyv[:, blk], xdt[:, blk], NT)
            dxdt_heads.append(_dot(G, dyv[:, blk], TN))
            dcb = dcb + dG * L
            Mseg = dG * G
            row_sums = jnp.where(lane_q4 == k, jnp.sum(Mseg, axis=1, keepdims=True), row_sums)
            col_sums = jnp.where(sub_4q == k, jnp.sum(Mseg, axis=0, keepdims=True), col_sums)
        dxdt = dxw * F_w + jnp.concatenate(dxdt_heads, axis=1)
        last = lax.broadcasted_iota(jnp.int32, (Q, SSD_HPG), 0) == Q - 1
        dac4 = dac4 + row_sums + jnp.where(last, datot4, 0.0)
        dadt4 = _dot(triu_ref[...], dac4, NN, HI) - _dot(triu_ref[...], col_sums, NT, HI)
        ddt4 = _head_sums(dxdt * x) + dadt4 * A
        dalog_ref[...] += _colsum(dadt4 * dt) * A
        ddtr4 = ddt4 * jax.nn.sigmoid(pre)
        dbias_ref[...] += _colsum(ddtr4)
        ddtr_ref[...] = ddtr4
        dx_ref[...] = (_per_head(dsk_ref[...], W) * dyv + dxdt * dt_w).astype(dx_ref.dtype)
        db_ref[...] = dB + _dot(dcb, Cm, TN)
        dc_ref[...] = dC + _dot(dcb, Bm, NN)

    tri, triu = _tri(Q)
    return pl.pallas_call(
        body, name="ssd_bwd", grid=(SSD_GROUPS // SSD_STEP_GROUPS, NC),
        in_specs=[x_spec, b_spec, c_spec, dt_spec, p_spec, p_spec, p_spec, tri_spec, tri_spec, h_spec, dy_spec],
        out_specs=[dy_spec, dbc_spec, dbc_spec, dt_spec, p_spec, p_spec, p_spec],
        out_shape=[jax.ShapeDtypeStruct((T, D_INNER), F32),
                   jax.ShapeDtypeStruct((T, SSD_GROUPS * SSD_STATE), F32),
                   jax.ShapeDtypeStruct((T, SSD_GROUPS * SSD_STATE), F32),
                   jax.ShapeDtypeStruct(dtr8.shape, F32)] + [jax.ShapeDtypeStruct(bias8.shape, F32)] * 3,
        scratch_shapes=[pltpu.VMEM((SSD_STEP_GROUPS, W, SSD_STATE), F32)],
        compiler_params=_params(("parallel", "arbitrary")),
    )(xc, xc, xc, dtr8, bias8, alog8, dsk8, tri, triu, hs, dy)


def _to_groups(v):
    return v.reshape(SSD_GROUPS, 1, SSD_HPG)


def _mixer_fwd(x, p, w):
    T, D = x.shape
    (hn,) = _rowwise(lambda xt, g: (_rms(xt, g),), [x], [p["mix_norm"]], [(D, BF16)], name="mix_norm")
    winT = w["w_in"]
    u, z, xbc, dtr, gl = [
        _matmul(hn, winT[IN_OFFS[i]:IN_OFFS[i + 1]], "nt", name=f"mix_in{i}") for i in range(5)]
    tiles = _s5_derive(*[p[k][0] for k in ("s5_A_re", "s5_A_im", "s5_log_dt", "s5_B_re", "s5_B_im", "s5_C_re", "s5_C_im")])
    u = _to_seg(u)
    ys = _s5_fwd(u, tiles)
    (g,) = _rowwise(lambda yt, ut, d: (jax.nn.gelu(yt + d * ut),), [ys, u], [p["s5_D"]], [(D, F32)], name="s5_gelu")
    y5 = _matmul(g, w["s5_w_glu"], "nn", name="s5_glu", out=(BF16,),
                 epilogue=lambda acc, gt, b: (gt * jax.nn.sigmoid(acc + b),), extras=[g, p["s5_b_glu"]])
    xc = _conv_fwd(xbc, w["conv_w"], p["conv_b"])
    dtr8 = dtr.reshape(T, SSD_GROUPS, SSD_HPG).transpose(1, 0, 2)
    ssd_p = [_to_groups(p[k]) for k in ("ssd_dt_bias", "ssd_A_log", "ssd_D")]
    yssd_raw, hs = _ssd_fwd(xc, dtr8, *ssd_p)
    (yssd,) = _rowwise(lambda yt, zt, nw: (_rms(yt * jax.nn.silu(zt), nw),), [yssd_raw, z], [p["ssd_norm"]],
                       [(D_INNER, BF16)], name="ssd_gate")
    p5 = _from_seg(_matmul(y5, w["w_proj_s5"], "nn", name="mix_p5"))
    pssd = _matmul(yssd, w["w_proj_ssd"], "nn", name="mix_pssd")

    def merge(glt, at, bt, bg):
        gates = jax.nn.sigmoid(glt + bg)
        return (gates[:, :D] * at + gates[:, D:] * bt,)

    (merged,) = _rowwise(merge, [gl, p5, pssd], [p["b_gate"]], [(D, BF16)], name="mix_merge")
    y = _matmul(merged, w["w_out"], "nn", name="mix_out", epilogue=lambda acc, xt: (xt + acc,), extras=[x])
    res = dict(hn=hn, u=u, z=z, xbc=xbc, gl=gl, tiles=tiles, ys=ys, g=g, y5=y5, xc=xc, dtr8=dtr8, ssd_p=ssd_p,
               yssd_raw=yssd_raw, hs=hs, yssd=yssd, p5=p5, pssd=pssd, merged=merged)
    return y, res


def _mixer_bwd(dy, dy16, x, p, w, r):
    T, D = x.shape
    gp = {}
    dmerged = _matmul(dy16, w["w_out"], "nt", name="mix_d_merged")
    gp["w_out"] = _matmul(r["merged"], dy16, "tn", name="mix_d_wout")

    def merge_bwd(glt, at, bt, dm, bg):
        def f(q, a_, b_):
            gates = jax.nn.sigmoid(q + bg)
            return gates[:, :D] * a_ + gates[:, D:] * b_
        _, vjp = jax.vjp(f, glt, at, bt)
        dq, da_, db_ = vjp(dm)
        return da_, db_, dq, _colsum(dq)

    dp5, dpssd, dgl, gp["b_gate"] = _rowwise(
        merge_bwd, [r["gl"], r["p5"], r["pssd"], dmerged], [p["b_gate"]],
        [(D, BF16), (D, BF16), (2 * D, BF16)], [((1, 2 * D), F32)], name="mix_d_merge")
    dyssd = _matmul(dpssd, w["w_proj_ssd"], "nt", name="mix_d_yssd")
    gp["w_proj_ssd"] = _matmul(r["yssd"], dpssd, "tn", name="mix_d_wpssd")

    def gate_bwd(yt, zt, dyt, nw):
        _, vjp = jax.vjp(lambda a_, b_, c_: _rms(a_ * jax.nn.silu(b_), c_), yt, zt, nw)
        return vjp(dyt)

    dyraw, dz, gp["ssd_norm"] = _rowwise(
        gate_bwd, [r["yssd_raw"], r["z"], dyssd], [p["ssd_norm"]],
        [(D_INNER, F32), (D_INNER, BF16)], [((1, D_INNER), F32)], name="ssd_d_gate")
    dxs, dBm, dCm, ddtr8, dbias8, dalog8, ddsk8 = _ssd_bwd(r["xc"], r["dtr8"], *r["ssd_p"], r["hs"], dyraw)
    gp["ssd_dt_bias"], gp["ssd_A_log"], gp["ssd_D"] = [v.reshape(1, SSD_HEADS) for v in (dbias8, dalog8, ddsk8)]
    ddtr = ddtr8.transpose(1, 0, 2).reshape(T, SSD_HEADS)
    conv = [_conv_bwd(r["xbc"], w["conv_w"], p["conv_b"], d, c0, f"conv_bwd{i}")
            for i, (d, c0) in enumerate(((dxs, 0), (dBm, D_INNER), (dCm, D_INNER + SSD_GROUPS * SSD_STATE)))]
    dxbc = [c[0] for c in conv]
    gp["conv_w"] = jnp.concatenate([c[1] for c in conv], axis=1)
    gp["conv_b"] = jnp.concatenate([c[2] for c in conv], axis=1)

    dp5 = _to_seg(dp5)
    dy5 = _matmul(dp5, w["w_proj_s5"], "nt", name="mix_d_y5")
    gp["w_proj_s5"] = _matmul(r["y5"], dp5, "tn", name="mix_d_wp5")
    g = r["g"]

    def glu_ep(acc, gt, b, dyt):
        _, vjp = jax.vjp(lambda g_, t_: g_ * jax.nn.sigmoid(t_ + b), gt, acc)
        return vjp(dyt)

    dg1, dt_ = _matmul(g, w["s5_w_glu"], "nn", name="s5_d_glu", out=(F32, BF16), epilogue=glu_ep,
                       extras=[g, p["s5_b_glu"], dy5])
    gp["s5_w_glu"] = _matmul(g, dt_, "tn", name="s5_d_wglu")
    dg = _matmul(dt_, w["s5_w_glu"], "nt", name="s5_d_g", epilogue=lambda acc, e: (acc + e,), extras=[dg1])

    def gelu_bwd(yt, ut, dgt, dtt, d):
        _, vjp = jax.vjp(lambda y_, d_: jax.nn.gelu(y_ + d_ * ut), yt, d)
        dys, dd = vjp(dgt)
        return dys, dys * d, dd, _colsum(dtt.astype(F32))

    dys, dusk, gp["s5_D"], gp["s5_b_glu"] = _rowwise(
        gelu_bwd, [r["ys"], r["u"], dg, dt_], [p["s5_D"]], [(D, F32), (D, F32)], [((1, D), F32), ((1, D), F32)],
        name="s5_d_gelu")
    du, *dtiles = _s5_bwd(r["u"], dys, dusk, r["tiles"])
    du = _from_seg(du)

    winT = w["w_in"]
    hn = r["hn"]
    d_rows, dhn = [], None
    cols = [(du, 0, 1024), (dz, 1024, 3072), (dxbc[0], 3072, 5120), (dxbc[1], 5120, 6144), (dxbc[2], 6144, 7168),
            (ddtr, 7168, 7200), (dgl, 7200, 9248)]
    for i, (d, lo, hi) in enumerate(cols):
        d_rows.append(_matmul(d, hn, "tn", name=f"mix_d_win{i}"))
        if dhn is None:
            dhn = _matmul(d, winT[lo:hi], "nn", name=f"mix_d_hn{i}")
        else:
            dhn = _matmul(d, winT[lo:hi], "nn", name=f"mix_d_hn{i}", epilogue=lambda acc, e: (acc + e,), extras=[dhn])
    gp["w_in"] = jnp.concatenate(d_rows, axis=0)

    def norm_bwd(xt, dh, dyt, gn):
        _, vjp = jax.vjp(_rms, xt, gn)
        dx, dgn = vjp(dh)
        return dyt + dx, dyt + dx, dgn

    dx, dx16, gp["mix_norm"] = _rowwise(norm_bwd, [x, dhn, dy], [p["mix_norm"]], [(D, F32), (D, BF16)],
                                        [((1, D), F32)], name="mix_d_norm")
    return dx, dx16, gp, dtiles


def _s5_param_grads(p, dtiles):
    keys = ("s5_A_re", "s5_A_im", "s5_log_dt", "s5_B_re", "s5_B_im", "s5_C_re", "s5_C_im")
    _, vjp = jax.vjp(_s5_derive, *[p[k][0] for k in keys])
    return {k: v[None] for k, v in zip(keys, vjp(tuple(dtiles)))}


def _local_step(x, target, p, w, late_weights, exchange):
    T, D = x.shape
    w = dict(w)
    x1, r1 = _ffn_fwd(x, p["ffn1_norm"], w["ffn1_w_gate"], w["ffn1_w_up"], w["ffn1_w_down"], "ffn1",
                      on_gate=lambda a_: w.update(late_weights("mixer", a_)))
    x2, rm = _mixer_fwd(x1, p, w)
    w.update(late_weights("ffn2", rm["p5"]))
    x3, r2 = _ffn_fwd(x2, p["ffn2_norm"], w["ffn2_w_gate"], w["ffn2_w_up"], w["ffn2_w_down"], "ffn2")

    def head(xt, tt, g):
        def f(x_, g_):
            e = _rms(x_, g_) - tt
            return 0.5 * jnp.sum(jnp.mean(e * e, axis=-1))
        l, (dx_, dg_) = jax.value_and_grad(f, argnums=(0, 1))(xt, g)
        return dx_, dx_, l.reshape(1, 1), dg_

    dx3, dx3_16, loss, d_final = _rowwise(head, [x3, target], [p["final_norm"]], [(D, F32), (D, BF16)],
                                          [((1, 1), F32), ((1, D), F32)], name="loss_head")
    gp = {"final_norm": d_final}

    def ffn_exchange(tag):
        return lambda d_wgT, d_wuT, d_wd, after: exchange(
            tag, {f"{tag}_w_gate": d_wgT, f"{tag}_w_up": d_wuT, f"{tag}_w_down": d_wd}, after)

    dx2, dx2_16, gp["ffn2_norm"] = _ffn_bwd(
        dx3, dx3_16, x2, p["ffn2_norm"], w["ffn2_w_gate"], w["ffn2_w_up"], w["ffn2_w_down"], r2, "ffn2", (),
        ffn_exchange("ffn2"))
    dx1, dx1_16, gm, dtiles = _mixer_bwd(dx2, dx2_16, x1, p, w, rm)
    conv_w_grad = gm.pop("conv_w")
    started = exchange("mixer", {k: gm.pop(k) for k in ("w_out", "w_proj_s5", "w_proj_ssd", "s5_w_glu", "w_in")}, dx1)
    gp.update(gm)
    gp.update(_s5_param_grads(p, dtiles))
    dx0, _, gp["ffn1_norm"] = _ffn_bwd(
        dx1, dx1_16, x, p["ffn1_norm"], w["ffn1_w_gate"], w["ffn1_w_up"], w["ffn1_w_down"], r1, "ffn1", started,
        ffn_exchange("ffn1"))
    return loss, dx0, gp, conv_w_grad


ADAMW_BLOCK = 1 << 19


def _adamw(w, m, v, gs, *, name, emit_g):
    _, R, C = w.shape
    n_g = len(gs)
    n_o = 4 if emit_g else 3
    tm = _rtile(R, max(8, ADAMW_BLOCK // C))
    lead = pl.BlockSpec((None, tm, C), lambda i: (0, i, 0))

    def body(w_ref, m_ref, v_ref, *refs):
        g = refs[0][...]
        for r in refs[1:n_g]:
            g = g + r[...]
        m2 = ADAM_B1 * m_ref[...] + (1.0 - ADAM_B1) * g
        v2 = ADAM_B2 * v_ref[...] + (1.0 - ADAM_B2) * (g * g)
        m_hat = m2 / (1.0 - ADAM_B1 ** ADAM_STEP)
        v_hat = v2 / (1.0 - ADAM_B2 ** ADAM_STEP)
        delta = -ADAM_LR * (m_hat / (jnp.sqrt(v_hat) + ADAM_EPS) + ADAM_WD * w_ref[...])
        for r, val in zip(refs[n_g:], ((g,) if emit_g else ()) + (delta, m2, v2)):
            r[...] = val

    return pl.pallas_call(
        body, name=name, grid=(R // tm,),
        in_specs=[lead] * 3 + [pl.BlockSpec((tm, C), lambda i: (i, 0))] * n_g, out_specs=[lead] * n_o,
        out_shape=[jax.ShapeDtypeStruct((1, R, C), F32)] * n_o,
        compiler_params=_params(("parallel",)),
    )(w, m, v, *gs)


def _rtile(r, cap=512):
    if r <= cap:
        return r
    best = None
    for t in range(8, cap + 1, 8):
        if r % t == 0:
            best = t
    assert best is not None, r
    return best


ANY = pl.BlockSpec(memory_space=pl.ANY)


def _where_am_i():
    x, y, c = lax.axis_index("x"), lax.axis_index("y"), lax.axis_index("c")
    ks = (2 * x + y, 2 * (1 - x) + y, 2 * x + (1 - y), 2 * (1 - x) + (1 - y))
    return x, y, c, ks


def _rc(src, dst, ssem, rsem, to):
    return pltpu.make_async_remote_copy(src_ref=src, dst_ref=dst, send_sem=ssem, recv_sem=rsem,
                                        device_id=to, device_id_type=MESH)


def _gather_shards(srcs):
    n = len(srcs)

    def body(*refs):
        _gather_body(refs[:n], refs[n:2 * n], *refs[2 * n:], extra=())

    return pl.pallas_call(
        body, name="gather_shards",
        in_specs=[ANY] * n, out_specs=[ANY] * n,
        out_shape=[jax.ShapeDtypeStruct((4,) + s.shape, s.dtype) for s in srcs],
        scratch_shapes=[pltpu.SemaphoreType.DMA((n, 6)), pltpu.SemaphoreType.DMA((n, 6)), pltpu.SemaphoreType.DMA((n,))],
        compiler_params=pltpu.CompilerParams(has_side_effects=True),
    )(*srcs)


def _handshake_all():
    x, y, c = lax.axis_index("x"), lax.axis_index("y"), lax.axis_index("c")
    barrier = pltpu.get_barrier_semaphore()
    for dx, dy, dc in itertools.product((0, 1), repeat=3):
        if (dx, dy, dc) != (0, 0, 0):
            to = (1 - x if dx else x, 1 - y if dy else y, 1 - c if dc else c)
            pl.semaphore_signal(barrier, inc=1, device_id=to, device_id_type=MESH)
    pl.semaphore_wait(barrier, 7)


SEQUENCER = dict(axis_name="seq", num_cores=1)


def _gather_shards_async(srcs, token, name, collective_id):
    n = len(srcs)

    def body(*refs):
        tok, tok_out = refs[n], refs[2 * n + 1]
        ssem, rsem, lsem = refs[2 * n + 2:]
        _handshake_all()
        _gather_body(refs[:n], refs[n + 1:2 * n + 1], ssem, rsem, lsem,
                     extra=(pltpu.make_async_copy(tok, tok_out, lsem.at[n]),))

    res = pl.kernel(
        body, name=name,
        out_type=[jax.ShapeDtypeStruct((4,) + s.shape, s.dtype) for s in srcs]
        + [jax.ShapeDtypeStruct(token.shape, token.dtype)],
        mesh=plsc.ScalarSubcoreMesh(**SEQUENCER),
        scratch_types=[pltpu.SemaphoreType.DMA((n, 6)), pltpu.SemaphoreType.DMA((n, 6)), pltpu.SemaphoreType.DMA((n + 1,))],
        compiler_params=pltpu.CompilerParams(collective_id=collective_id),
    )(*srcs, token)
    return res[:n], res[n]


def _gather_body(src, out, ssem, rsem, lsem, extra):
        n = len(src)
        x, y, c, (k_me, k_x, k_y, k_d) = _where_am_i()
        sib = (x, y, 1 - c)
        local = [pltpu.make_async_copy(src[i], out[i].at[k_me], lsem.at[i]) for i in range(n)] + list(extra)
        for cp in local:
            cp.start()

        def own(i, q, to):
            return _rc(src[i], out[i].at[k_me], ssem.at[i, q], rsem.at[i, q], to)

        def slab(i, q, k, to):
            return _rc(out[i].at[k], out[i].at[k], ssem.at[i, q], rsem.at[i, q], to)

        @pl.when(c == 1)
        def _():
            sends = [own(i, 0, (1 - x, y, 1)) for i in range(n)]
            for cp in sends:
                cp.start()
            fwd = [slab(i, 3, k_x, sib) for i in range(n)]
            for i in range(n):
                slab(i, 0, k_x, sib).wait_recv()
                fwd[i].start()
            for i in range(n):
                slab(i, 4, k_y, sib).wait_recv()
                slab(i, 5, k_d, sib).wait_recv()
            for cp in sends + fwd:
                cp.wait_send()

        @pl.when(c == 0)
        def _():
            sends = [own(i, 1, (x, 1 - y, 0)) for i in range(n)] + [own(i, 2, (1 - x, 1 - y, 0)) for i in range(n)]
            for cp in sends:
                cp.start()
            fwd = [slab(i, 4, k_y, sib) for i in range(n)] + [slab(i, 5, k_d, sib) for i in range(n)]
            for i in range(n):
                slab(i, 1, k_y, sib).wait_recv()
                fwd[i].start()
            for i in range(n):
                slab(i, 2, k_d, sib).wait_recv()
                fwd[n + i].start()
            for i in range(n):
                slab(i, 3, k_x, sib).wait_recv()
            for cp in sends + fwd:
                cp.wait_send()

        for cp in local:
            cp.wait()


def _swap_slabs(arrs, n_slabs, name, async_id, token):
    n = len(arrs)
    J = max(n_slabs, 1)

    def body(*refs):
        src, out = refs[:n], refs[n:2 * n]
        ssem, rsem = refs[2 * n:]
        x, y, c, (k_me, k_x, k_y, k_d) = _where_am_i()
        sib = (x, y, 1 - c)
        sel = (jnp.where(c == 1, k_y, k_me), jnp.where(c == 1, k_d, k_x))
        cps = []
        for i in range(n):
            for j in range(J):
                s = src[i].at[sel[j]] if n_slabs else src[i]
                d = out[i].at[j] if n_slabs else out[i]
                cps.append(_rc(s, d, ssem.at[i, j], rsem.at[i, j], sib))
        for cp in cps:
            cp.start()
        for cp in cps:
            cp.wait()

    return _exchange_call(
        body, arrs, [jax.ShapeDtypeStruct(((n_slabs,) if n_slabs else ()) + a.shape[-2:], a.dtype) for a in arrs],
        [pltpu.SemaphoreType.DMA((n, J)), pltpu.SemaphoreType.DMA((n, J))], name, async_id, token)


def _exchange_call(body, arrs, out_shape, sems, name, collective_id, token):
    n, m = len(arrs), len(out_shape)

    def seq_body(*refs):
        tok, tok_out, tok_sem = refs[n], refs[n + 1 + m], refs[-1]
        _handshake_all()
        cp = pltpu.make_async_copy(tok, tok_out, tok_sem)
        cp.start()
        body(*refs[:n], *refs[n + 1:n + 1 + m], *refs[n + 2 + m:-1])
        cp.wait()

    res = pl.kernel(
        seq_body, name=name, out_type=list(out_shape) + [jax.ShapeDtypeStruct(token.shape, token.dtype)],
        mesh=plsc.ScalarSubcoreMesh(**SEQUENCER), scratch_types=list(sems) + [pltpu.SemaphoreType.DMA],
        compiler_params=pltpu.CompilerParams(collective_id=collective_id))(*arrs, token)
    return res[:m], res[m]


def _send_chip_sums(arrs, name, async_id, token):
    n = len(arrs)

    def body(*refs):
        src, out = refs[:n], refs[n:2 * n]
        ssem, rsem = refs[2 * n:]
        x, y, c, _ = _where_am_i()

        @pl.when(c == 1)
        def _():
            cps = [_rc(src[i].at[1], out[i].at[0], ssem.at[i, 0], rsem.at[i, 0], (1 - x, y, 1)) for i in range(n)]
            for cp in cps:
                cp.start()
            for cp in cps:
                cp.wait()

        @pl.when(c == 0)
        def _():
            cps = [_rc(src[i].at[0], out[i].at[0], ssem.at[i, 0], rsem.at[i, 0], (x, 1 - y, 0)) for i in range(n)]
            cps += [_rc(src[i].at[1], out[i].at[1], ssem.at[i, 1], rsem.at[i, 1], (1 - x, 1 - y, 0)) for i in range(n)]
            for cp in cps:
                cp.start()
            for cp in cps:
                cp.wait()

    return _exchange_call(
        body, arrs, [jax.ShapeDtypeStruct(a.shape, a.dtype) for a in arrs],
        [pltpu.SemaphoreType.DMA((n, 2)), pltpu.SemaphoreType.DMA((n, 2))], name, async_id, token)


def _chip_sum(g, recv, sel, name):
    _, r, C = g.shape
    tr = _rtile(r)

    def body(sel_ref, g_ref, r_ref, o32_ref, o16_ref):
        s = g_ref[...] + r_ref[...]
        o32_ref[...] = s
        o16_ref[...] = s.astype(BF16)

    blk = pl.BlockSpec((None, tr, C), lambda j, t, sel_ref: (j, t, 0))
    return pl.pallas_call(
        body, name=name,
        grid_spec=pltpu.PrefetchScalarGridSpec(
            num_scalar_prefetch=1, grid=(2, r // tr),
            in_specs=[pl.BlockSpec((None, tr, C), lambda j, t, sel_ref: (sel_ref[j], t, 0)), blk],
            out_specs=[blk, blk]),
        out_shape=[jax.ShapeDtypeStruct((2, r, C), F32), jax.ShapeDtypeStruct((2, r, C), BF16)],
        compiler_params=_params(("arbitrary", "arbitrary")),
    )(sel, g, recv)


def _cross_sum(s32, got, north, after, name):
    _, r, C = s32.shape
    tr = _rtile(r)

    def body(f_ref, p_ref, a_ref, b_ref, after_ref, o_ref):
        a = a_ref[...].astype(F32)

        @pl.when(f_ref[0] == 1)
        def _():
            o_ref[...] = p_ref[...] + a

        @pl.when(f_ref[0] == 0)
        def _():
            o_ref[...] = a + b_ref[...].astype(F32)

    return pl.pallas_call(
        body, name=name,
        grid_spec=pltpu.PrefetchScalarGridSpec(
            num_scalar_prefetch=1, grid=(r // tr,),
            in_specs=[pl.BlockSpec((None, tr, C), lambda t, f: (0, t, 0)),
                      pl.BlockSpec((None, tr, C), lambda t, f: (0, t, 0)),
                      pl.BlockSpec((None, tr, C), lambda t, f: (1 - f[0], t, 0)), ANY],
            out_specs=pl.BlockSpec((tr, C), lambda t, f: (t, 0))),
        out_shape=jax.ShapeDtypeStruct((r, C), F32),
        compiler_params=_params(("arbitrary",)),
    )(north, s32, got, got, after)


class _Reducer:
    def __init__(self, token):
        self.token, self.pending, self.pairs = token, None, {}

    def begin(self, tag, ids, grads, after):
        names = list(grads)
        arrs = [grads[n].reshape(4, -1, D_MODEL) for n in names]
        x, y, c, (k_me, k_x, k_y, k_d) = _where_am_i()
        sel = jnp.stack([jnp.where(c == 1, k_me, k_y), jnp.where(c == 1, k_x, k_d)]).astype(jnp.int32)
        from_sib, self.token = _swap_slabs(arrs, 2, f"swap_grad_slabs_{tag}", ids[0], self.token)
        sums = [_chip_sum(g, r, sel, f"chip_sum_{tag}{i}") for i, (g, r) in enumerate(zip(arrs, from_sib))]
        self.finish(after)
        got, self.token = _send_chip_sums([s16 for _, s16 in sums], f"send_chip_sums_{tag}", ids[1], self.token)
        self.pending = (tag, ids[2], names, sums, got)
        return tuple(s16 for _, s16 in sums)

    def finish(self, after):
        if self.pending is None:
            return
        tag, cid, names, sums, got = self.pending
        north = jnp.reshape(lax.axis_index("c"), (1,)).astype(jnp.int32)
        parts = [_cross_sum(s32, b, north, after, f"cross_sum_{tag}{i}") for i, ((s32, _), b) in enumerate(zip(sums, got))]
        theirs, self.token = _swap_slabs(parts, 0, f"swap_parts_{tag}", cid, self.token)
        self.pairs.update(zip(names, zip(parts, theirs)))
        self.pending = None


def _all_reduce_small(v, token, collective_id):
    R, C = v.shape
    tr = R

    def body(v_ref, buf, ssem, rsem, lsem):
        x, y, c = lax.axis_index("x"), lax.axis_index("y"), lax.axis_index("c")
        me, sib = (x, y, c), (x, y, 1 - c)
        chips = [(1 - x, y), (x, 1 - y), (1 - x, 1 - y)]

        def slot(px, py, pc):
            return buf.at[4 * px + 2 * py + pc]

        def copy(k, block, to, src=None):
            return _rc(slot(*block) if src is None else src, slot(*block), ssem.at[k], rsem.at[k], to)

        mine = pltpu.make_async_copy(v_ref, slot(*me), lsem)
        mine.start()
        first = [copy(0, me, sib, src=v_ref)] + [copy(1 + j, me, (*chip, c), src=v_ref) for j, chip in enumerate(chips)]
        for cp in first:
            cp.start()
        passed = [copy(4 + j, (*chip, c), sib) for j, chip in enumerate(chips)]
        for j, chip in enumerate(chips):
            copy(1 + j, (*chip, c), me).wait_recv()
            passed[j].start()
        copy(0, sib, me).wait_recv()
        for j, chip in enumerate(chips):
            copy(4 + j, (*chip, 1 - c), me).wait_recv()
        for cp in first + passed:
            cp.wait_send()
        mine.wait()

    (gathered,), token = _exchange_call(
        body, [v], [jax.ShapeDtypeStruct((8, R, C), F32)],
        [pltpu.SemaphoreType.DMA((7,)), pltpu.SemaphoreType.DMA((7,)), pltpu.SemaphoreType.DMA],
        "gather_small", collective_id, token)

    def add(g_ref, o_ref):
        acc = g_ref[0]
        for k in range(1, 8):
            acc = acc + g_ref[k]
        o_ref[...] = acc

    total = pl.pallas_call(
        add, name="sum_small", grid=(R // tr,),
        in_specs=[pl.BlockSpec((8, tr, C), lambda i: (0, i, 0))], out_specs=pl.BlockSpec((tr, C), lambda i: (i, 0)),
        out_shape=jax.ShapeDtypeStruct((R, C), F32), compiler_params=_params(("parallel",)),
    )(gathered)
    return total, token


WEIGHTS = ['ffn1_norm', 'ffn1_w_gate', 'ffn1_w_up', 'ffn1_w_down', 'mix_norm', 'w_in', 'conv_w', 'conv_b', 's5_A_re',
           's5_A_im', 's5_log_dt', 's5_B_re', 's5_B_im', 's5_C_re', 's5_C_im', 's5_D', 's5_w_glu', 's5_b_glu',
           'ssd_A_log', 'ssd_dt_bias', 'ssd_D', 'ssd_norm', 'w_proj_s5', 'w_proj_ssd', 'b_gate', 'w_out', 'ffn2_norm',
           'ffn2_w_gate', 'ffn2_w_up', 'ffn2_w_down', 'final_norm']
ARGS = ['x'] + WEIGHTS + ['loss_target'] + ['m_' + n for n in WEIGHTS] + ['v_' + n for n in WEIGHTS]
COL_SHARDED = ('ffn1_w_gate', 'ffn1_w_up', 'w_in', 'ffn2_w_gate', 'ffn2_w_up')
ROW_SHARDED = ('ffn1_w_down', 's5_w_glu', 'w_proj_s5', 'w_proj_ssd', 'w_out', 'ffn2_w_down')
MATRICES = COL_SHARDED + ROW_SHARDED
GATHER_FIRST = ('ffn1_w_gate', 'ffn1_w_up')
GATHER_FIRST_LATER = ('ffn1_w_down',)
GATHER_SECOND = ('w_in', 'conv_w', 's5_w_glu', 'w_proj_s5', 'w_proj_ssd', 'w_out')
GATHER_THIRD = ('ffn2_w_gate', 'ffn2_w_up', 'ffn2_w_down')
SMALL = [n for n in WEIGHTS if n not in MATRICES]


def _pack(arrs, width=1024):
    flat = jnp.concatenate([a.reshape(-1).astype(F32) for a in arrs])
    rows = -(-flat.shape[0] // (8 * width)) * 8
    return jnp.pad(flat, (0, rows * width - flat.shape[0])).reshape(rows, width)


def _unpack(packed, shapes):
    flat, out, o = packed.reshape(-1), [], 0
    for s in shapes:
        n = math.prod(s)
        out.append(flat[o:o + n].reshape(s))
        o += n
    return out


def kernel(x, ffn1_norm, ffn1_w_gate, ffn1_w_up, ffn1_w_down, mix_norm, w_in, conv_w, conv_b, s5_A_re, s5_A_im, s5_log_dt, s5_B_re, s5_B_im, s5_C_re, s5_C_im, s5_D, s5_w_glu, s5_b_glu, ssd_A_log, ssd_dt_bias, ssd_D, ssd_norm, w_proj_s5, w_proj_ssd, b_gate, w_out, ffn2_norm, ffn2_w_gate, ffn2_w_up, ffn2_w_down, final_norm, loss_target, m_ffn1_norm, m_ffn1_w_gate, m_ffn1_w_up, m_ffn1_w_down, m_mix_norm, m_w_in, m_conv_w, m_conv_b, m_s5_A_re, m_s5_A_im, m_s5_log_dt, m_s5_B_re, m_s5_B_im, m_s5_C_re, m_s5_C_im, m_s5_D, m_s5_w_glu, m_s5_b_glu, m_ssd_A_log, m_ssd_dt_bias, m_ssd_D, m_ssd_norm, m_w_proj_s5, m_w_proj_ssd, m_b_gate, m_w_out, m_ffn2_norm, m_ffn2_w_gate, m_ffn2_w_up, m_ffn2_w_down, m_final_norm, v_ffn1_norm, v_ffn1_w_gate, v_ffn1_w_up, v_ffn1_w_down, v_mix_norm, v_w_in, v_conv_w, v_conv_b, v_s5_A_re, v_s5_A_im, v_s5_log_dt, v_s5_B_re, v_s5_B_im, v_s5_C_re, v_s5_C_im, v_s5_D, v_s5_w_glu, v_s5_b_glu, v_ssd_A_log, v_ssd_dt_bias, v_ssd_D, v_ssd_norm, v_w_proj_s5, v_w_proj_ssd, v_b_gate, v_w_out, v_ffn2_norm, v_ffn2_w_gate, v_ffn2_w_up, v_ffn2_w_down, v_final_norm):
    a = dict(locals())
    assert list(a) == ARGS
    x, target = a['x'][0], a['loss_target'][0]
    k_me = 2 * lax.axis_index("x") + lax.axis_index("y")

    src = {n: a[n][0].T.astype(BF16) for n in COL_SHARDED}
    src.update({n: a[n][0].astype(BF16) for n in ROW_SHARDED})
    src['conv_w'] = a['conv_w'][0]
    def flat(f):
        return f.reshape(-1, f.shape[-1])

    full = _gather_shards([src[n] for n in GATHER_FIRST])
    w = {n: flat(f) for n, f in zip(GATHER_FIRST, full)}
    got, token = _gather_shards_async(
        [src[n] for n in GATHER_FIRST_LATER], full[0][0, :8, :128].astype(F32), "gather_ffn1_down", 14)
    w.update({n: flat(f) for n, f in zip(GATHER_FIRST_LATER, got)})
    reducer = _Reducer(token)

    def gather_later(stage, after):
        token = reducer.token + after[:8, :128].astype(F32)
        calls = {"mixer": ((GATHER_SECOND[:2], "gather_mixer_in", 1), (GATHER_SECOND[2:], "gather_mixer", 12)),
                 "ffn2": ((GATHER_THIRD, "gather_ffn2", 2),)}[stage]
        out = {}
        for names, name, cid in calls:
            got, token = _gather_shards_async([src[n] for n in names], token, name, cid)
            out.update({n: flat(f) for n, f in zip(names, got)})
        reducer.token = token
        if 'conv_w' in out:
            out['conv_w'] = out['conv_w'].reshape(4, CONV_K, -1).transpose(1, 0, 2).reshape(CONV_K, CONV_DIM)
        return out

    p = {n: a[n] for n in SMALL if n != 'conv_w'}
    p['final_norm'] = a['final_norm'][None]
    ids = {"ffn2": (3, 4, 5), "mixer": (6, 7, 8), "ffn1": (9, 10, 11)}
    loss, grad_x, gp, conv_w_grad = _local_step(
        x, target, p, w, gather_later, lambda tag, grads, after: reducer.begin(tag, ids[tag], grads, after))
    gp['conv_w'] = conv_w_grad[None]
    small_shapes = [(1, 1)] + [gp[n].shape if n != 'final_norm' else (1, D_MODEL) for n in SMALL]
    summed, reducer.token = _all_reduce_small(_pack([loss] + [gp[n] for n in SMALL]), reducer.token, 13)
    red = _unpack(summed, small_shapes)
    loss_all = red[0].reshape(())
    gsmall = dict(zip(SMALL, red[1:]))
    gsmall['conv_w'] = lax.dynamic_slice_in_dim(gsmall['conv_w'], k_me * 1024, 1024, axis=2)
    gsmall = {n: g.reshape(a[n].shape) for n, g in gsmall.items()}

    grads, delta, new_m, new_v = {}, {}, {}, {}

    def update(names):
        for n in names:
            turn = (lambda t: jnp.swapaxes(t, 1, 2)) if n in COL_SHARDED else (lambda t: t)
            out = _adamw(turn(a[n]), turn(a['m_' + n]), turn(a['v_' + n]), list(reducer.pairs[n]),
                         name=f"adamw_{n}", emit_g=True)
            grads[n], delta[n], new_m[n], new_v[n] = [turn(t) for t in out]

    update([n for n in MATRICES if n not in GATHER_FIRST + GATHER_FIRST_LATER])
    reducer.finish(delta['w_in'])
    update(GATHER_FIRST + GATHER_FIRST_LATER)
    sw, sm, sv, sg = [_pack([t[n] for n in SMALL])[None] for t in (a, {n: a['m_' + n] for n in SMALL},
                                                                   {n: a['v_' + n] for n in SMALL}, gsmall)]
    d, m2, v2 = _adamw(sw, sm, sv, [sg[0]], name="adamw_small", emit_g=False)
    shapes = [a[n].shape for n in SMALL]
    for n, dd, mm, vv in zip(SMALL, _unpack(d[0], shapes), _unpack(m2[0], shapes), _unpack(v2[0], shapes)):
        grads[n], delta[n], new_m[n], new_v[n] = gsmall[n], dd, mm, vv
    return (loss_all, grad_x[None], *[grads[n] for n in WEIGHTS], *[delta[n] for n in WEIGHTS],
            *[new_m[n] for n in WEIGHTS], *[new_v[n] for n in WEIGHTS])
```

```python
import itertools
import math

import jax
import jax.numpy as jnp
from jax import lax
from jax.experimental import pallas as pl
from jax.experimental.pallas import tpu as pltpu
from jax.experimental.pallas import tpu_sc as plsc

F32 = jnp.float32
BF16 = jnp.bfloat16
MXU = BF16
HI = lax.Precision.HIGHEST

D_MODEL = 1024
D_FF = 2816
EPS = 1e-6
S5_GROUPS, S5_GROUP, S5_STATE = 64, 16, 64
D_INNER = 2048
SSD_HEADDIM, SSD_HEADS, SSD_GROUPS, SSD_HPG, SSD_STATE = 64, 32, 8, 4, 128
CONV_K, CONV_DIM = 4, 4096
IN_OFFS = (0, 1024, 3072, 7168, 7200, 9248)
SSD_Q = 256
SSD_STEP_GROUPS = 4
ADAM_LR, ADAM_B1, ADAM_B2, ADAM_EPS, ADAM_WD, ADAM_STEP = 0.001, 0.9, 0.999, 1e-08, 0.01, 10

VMEM_LIMIT = 56 * 1024 * 1024
MATMUL_VMEM = 32 * 1024 * 1024
MESH = pl.DeviceIdType.MESH

NN = ((1,), (0,))
NT = ((1,), (1,))
TN = ((0,), (0,))


def _dot(a, b, dims, precision=None):
    if precision is None:
        a, b = a.astype(MXU), b.astype(MXU)
    return lax.dot_general(a, b, (dims, ((), ())), precision=precision, preferred_element_type=F32)


def _tile(n, pref):
    if n <= pref:
        return n
    best = None
    for t in range(128, pref + 1, 128):
        if n % t == 0:
            best = t
    assert best is not None, (n, pref)
    return best


def _params(sem):
    return pltpu.CompilerParams(dimension_semantics=sem, vmem_limit_bytes=VMEM_LIMIT)


def _matmul(a, b, mode, *, name, out=((F32),), epilogue=None, extras=(), after=(), tm=512, tn=1536):
    if mode == "nn":
        (M, K), (_, N) = a.shape, b.shape
    elif mode == "nt":
        (M, K), (N, _) = a.shape, b.shape
    else:
        (K, M), (_, N) = a.shape, b.shape
    tm, tn = _tile(M, tm), _tile(N, tn)

    def vmem(tm_, tn_):
        per = K * tm_ * a.dtype.itemsize + K * tn_ * b.dtype.itemsize
        per += sum(tm_ * tn_ * jnp.dtype(dt).itemsize for dt in out)
        per += sum((1 if e.shape[0] == 1 else tm_) * tn_ * e.dtype.itemsize for e in extras)
        return 2 * per

    while vmem(tm, tn) > MATMUL_VMEM and (tn > 128 or tm > 128):
        if tn >= tm and tn > 128:
            tn = _tile(N, tn - 128)
        else:
            tm = _tile(M, tm - 128)
    bytes_a, bytes_b = a.size * a.dtype.itemsize, b.size * b.dtype.itemsize
    rows_outer = bytes_a + (M // tm) * bytes_b <= (N // tn) * bytes_a + bytes_b
    grid = (M // tm, N // tn) if rows_outer else (N // tn, M // tm)

    def at(f):
        return (lambda i, j: f(i, j)) if rows_outer else (lambda j, i: f(i, j))

    a_spec = pl.BlockSpec((K, tm), at(lambda i, j: (0, i))) if mode == "tn" else pl.BlockSpec((tm, K), at(lambda i, j: (i, 0)))
    b_spec = pl.BlockSpec((tn, K), at(lambda i, j: (j, 0))) if mode == "nt" else pl.BlockSpec((K, tn), at(lambda i, j: (0, j)))
    dims = {"nn": NN, "nt": NT, "tn": TN}[mode]
    e_specs = []
    for e in extras:
        if e.shape[0] == 1:
            e_specs.append(pl.BlockSpec((1, tn), at(lambda i, j: (0, j))))
        else:
            assert e.shape == (M, N), (e.shape, M, N)
            e_specs.append(pl.BlockSpec((tm, tn), at(lambda i, j: (i, j))))
    n_e, n_o, n_a = len(extras), len(out), len(after)

    def body(a_ref, b_ref, *refs):
        acc = _dot(a_ref[...], b_ref[...], dims)
        res = (acc,) if epilogue is None else epilogue(acc, *[r[...] for r in refs[:n_e]])
        for r, v in zip(refs[n_e + n_a:], res):
            r[...] = v.astype(r.dtype)

    res = pl.pallas_call(
        body, name=name, grid=grid,
        in_specs=[a_spec, b_spec] + e_specs + [ANY] * n_a,
        out_specs=[pl.BlockSpec((tm, tn), at(lambda i, j: (i, j))) for _ in range(n_o)],
        out_shape=[jax.ShapeDtypeStruct((M, N), dt) for dt in out],
        compiler_params=_params(("parallel", "parallel")),
    )(a, b, *extras, *after)
    return res[0] if n_o == 1 else res


ROW_TILE = 256
ROW_TILE_WIDE_UP_TO = 1024


def _rowwise(fn, rows, fulls, outs, reds=(), *, name, after=(), tm=ROW_TILE):
    T = rows[0].shape[0]
    widest = max([r.shape[1] for r in rows] + [c for c, _ in outs])
    if tm == ROW_TILE and widest <= ROW_TILE_WIDE_UP_TO and T % (2 * tm) == 0:
        tm *= 2
    tm = min(tm, T)
    n_r, n_f, n_o, n_d, n_a = len(rows), len(fulls), len(outs), len(reds), len(after)

    def body(*refs):
        ins = [r[...] for r in refs[:n_r + n_f]]
        o_refs = refs[n_r + n_f + n_a:n_r + n_f + n_a + n_o]
        d_refs = refs[n_r + n_f + n_a + n_o:]
        res = fn(*ins)
        for r, v in zip(o_refs, res[:n_o]):
            r[...] = v.astype(r.dtype)
        if n_d:
            @pl.when(pl.program_id(0) == 0)
            def _():
                for r in d_refs:
                    r[...] = jnp.zeros_like(r)
            for r, v in zip(d_refs, res[n_o:]):
                r[...] += v.astype(r.dtype)

    res = pl.pallas_call(
        body, name=name, grid=(T // tm,),
        in_specs=[pl.BlockSpec((tm, r.shape[1]), lambda i: (i, 0)) for r in rows]
        + [pl.BlockSpec(f.shape, lambda i, nd=f.ndim: (0,) * nd) for f in fulls] + [ANY] * n_a,
        out_specs=[pl.BlockSpec((tm, c), lambda i: (i, 0)) for c, _ in outs]
        + [pl.BlockSpec(s, lambda i, nd=len(s): (0,) * nd) for s, _ in reds],
        out_shape=[jax.ShapeDtypeStruct((T, c), dt) for c, dt in outs]
        + [jax.ShapeDtypeStruct(s, dt) for s, dt in reds],
        compiler_params=_params(("arbitrary",)),
    )(*rows, *fulls, *after)
    return res


def _rms(x, g):
    return x * lax.rsqrt(jnp.mean(x * x, axis=-1, keepdims=True) + EPS) * g


def _colsum(v):
    return jnp.sum(v, axis=0, keepdims=True)


def _softplus(x):
    return jnp.maximum(x, 0.0) + jnp.log1p(jnp.exp(-jnp.abs(x)))


def _ffn_fwd(x, norm, wgT, wuT, wd, tag, on_gate=None):
    D = x.shape[1]
    (hn,) = _rowwise(lambda xt, g: (_rms(xt, g),), [x], [norm], [(D, BF16)], name=f"{tag}_norm")
    a = _matmul(hn, wgT, "nt", name=f"{tag}_gate", out=(BF16,))
    if on_gate is not None:
        on_gate(a)
    b, hid = _matmul(hn, wuT, "nt", name=f"{tag}_up", out=(BF16, BF16),
                     epilogue=lambda acc, at: (acc, jax.nn.silu(at.astype(F32)) * acc), extras=[a])
    y = _matmul(hid, wd, "nn", name=f"{tag}_down", epilogue=lambda acc, xt: (xt + 0.5 * acc,), extras=[x])
    return y, (hn, a, b, hid)


def _ffn_bwd(dy, dy16, x, norm, wgT, wuT, wd, res, tag, after, on_grads):
    hn, a, b, hid = res
    D, F = x.shape[1], a.shape[1]

    def act_bwd(acc, at, bt):
        a_, b_ = at.astype(F32), bt.astype(F32)
        sig = jax.nn.sigmoid(a_)
        s = a_ * sig
        dh = 0.5 * acc
        return dh * b_ * (sig + s * (1.0 - sig)), dh * s

    da, db = _matmul(dy16, wd, "nt", name=f"{tag}_d_hid", out=(BF16, BF16), epilogue=act_bwd, extras=[a, b])
    d_wd = _matmul(hid, dy16, "tn", name=f"{tag}_d_wd", epilogue=lambda acc: (0.5 * acc,))
    d_wgT = _matmul(da, hn, "tn", name=f"{tag}_d_wg", after=after)
    d_wuT = _matmul(db, hn, "tn", name=f"{tag}_d_wu")
    started = on_grads(d_wgT, d_wuT, d_wd, dy)
    dhn = _matmul(da, wgT, "nn", name=f"{tag}_d_hn1")
    dhn = _matmul(db, wuT, "nn", name=f"{tag}_d_hn2", epilogue=lambda acc, e: (acc + e,), extras=[dhn])

    def norm_bwd(xt, dh, dyt, g):
        _, vjp = jax.vjp(_rms, xt, g)
        dx, dg = vjp(dh)
        return dyt + dx, dyt + dx, dg

    dx, dx16, d_norm = _rowwise(norm_bwd, [x, dhn, dy], [norm], [(D, F32), (D, BF16)], [((1, D), F32)],
                                name=f"{tag}_d_norm", after=started)
    return dx, dx16, d_norm


S5_TILES = 8
S5_HALF = 256


def _s5_derive(A_re, A_im, log_dt, B_re, B_im, C_re, C_im):
    G, N, M = S5_GROUPS, S5_STATE, S5_GROUP
    dt = jnp.exp(log_dt)[:, None]
    mag = jnp.exp(A_re * dt)
    ar, ai = mag * jnp.cos(A_im * dt), mag * jnp.sin(A_im * dt)
    den = A_re * A_re + A_im * A_im
    cr = ((ar - 1.0) * A_re + ai * A_im) / den
    ci = (ai * A_re - (ar - 1.0) * A_im) / den
    bbr = cr[..., None] * B_re - ci[..., None] * B_im
    bbi = cr[..., None] * B_im + ci[..., None] * B_re
    eye = jnp.eye(8, dtype=F32)

    def tile_in(bb):
        t = bb.reshape(S5_TILES, 8, N, M).transpose(0, 1, 3, 2)
        return jnp.einsum("jamn,ab->jambn", t, eye).reshape(S5_TILES, 8 * M, 8 * N)

    def tile_out(c):
        t = c.reshape(S5_TILES, 8, M, N).transpose(0, 1, 3, 2)
        return jnp.einsum("janm,ab->janbm", t, eye).reshape(S5_TILES, 8 * N, 8 * M)

    return (tile_in(bbr), tile_in(bbi), tile_out(C_re), tile_out(C_im),
            ar.reshape(S5_TILES, 1, 8 * N), ai.reshape(S5_TILES, 1, 8 * N))


S5_NB = S5_HALF // 128
S5_SEG = 32


def _cmul(ar, ai, br, bi):
    return ar * br - ai * bi, ar * bi + ai * br


def _s5_scan(sr_ref, si_ref, ar, ai, T, reverse):
    L, V = T // S5_SEG, S5_SEG // 8
    assert L * S5_SEG == T and L & (L - 1) == 0, T
    sg = -1.0 if reverse else 1.0
    a_r = [jnp.broadcast_to(ar[:, 128 * b:128 * (b + 1)], (8, 128)) for b in range(S5_NB)]
    a_i = [jnp.broadcast_to(sg * ai[:, 128 * b:128 * (b + 1)], (8, 128)) for b in range(S5_NB)]

    def rows(k, v):
        return pl.ds(pl.multiple_of(((L - 1 - k) if reverse else k) * S5_SEG + 8 * v, 8), 8)

    def local(k, carry):
        out = []
        for b in range(S5_NB):
            for v in range(V):
                idx = rows(k, v)
                mr, mi = _cmul(a_r[b], a_i[b], *carry[b * V + v])
                nr, ni = mr + sr_ref[b, idx, :], mi + si_ref[b, idx, :]
                sr_ref[b, idx, :] = nr
                si_ref[b, idx, :] = ni
                out.append((nr, ni))
        return tuple(out)

    z = jnp.zeros((8, 128), F32)
    ends = lax.fori_loop(0, L, local, tuple((z, z) for _ in range(S5_NB * V)))

    carries = []
    for b in range(S5_NB):
        pr, pi = a_r[b][0:1], a_i[b][0:1]
        n = L
        while n > 1:
            pr, pi = _cmul(pr, pi, pr, pi)
            n //= 2
        cr, ci = jnp.zeros((1, 128), F32), jnp.zeros((1, 128), F32)
        into = [None] * S5_SEG
        for j in (reversed(range(S5_SEG)) if reverse else range(S5_SEG)):
            into[j] = (cr, ci)
            er, ei = ends[b * V + j // 8]
            mr, mi = _cmul(pr, pi, cr, ci)
            cr, ci = mr + er[j % 8:j % 8 + 1], mi + ei[j % 8:j % 8 + 1]
        carries.append([(jnp.concatenate([into[8 * v + s][0] for s in range(8)], axis=0),
                         jnp.concatenate([into[8 * v + s][1] for s in range(8)], axis=0)) for v in range(V)])

    def fix(k, powers):
        out = []
        for b in range(S5_NB):
            pr, pi = powers[b]
            for v in range(V):
                idx = rows(k, v)
                dr, di = _cmul(pr, pi, *carries[b][v])
                sr_ref[b, idx, :] += dr
                si_ref[b, idx, :] += di
            out.append(_cmul(pr, pi, a_r[b], a_i[b]))
        return tuple(out)

    lax.fori_loop(0, L, fix, tuple((a_r[b], a_i[b]) for b in range(S5_NB)))


def _s5_in(ut, b_ref, s_ref):
    for b in range(S5_NB):
        s_ref[b] = _dot(ut, b_ref[:, 128 * b:128 * (b + 1)], NN)


def _to_seg(v):
    T, C = v.shape
    return v.reshape(S5_SEG, T // S5_SEG, C).transpose(1, 0, 2).reshape(T, C)


def _from_seg(v):
    T, C = v.shape
    return v.reshape(T // S5_SEG, S5_SEG, C).transpose(1, 0, 2).reshape(T, C)


def _s5_specs(T):
    u_spec = pl.BlockSpec((T, 128), lambda j, h: (0, j))
    b_spec = pl.BlockSpec((None, 128, S5_HALF), lambda j, h: (j, 0, h))
    c_spec = pl.BlockSpec((None, S5_HALF, 128), lambda j, h: (j, h, 0))
    a_spec = pl.BlockSpec((None, 1, S5_HALF), lambda j, h: (j, 0, h))
    return u_spec, b_spec, c_spec, a_spec


def _s5_fwd(u, tiles):
    T = u.shape[0]
    u_spec, b_spec, c_spec, a_spec = _s5_specs(T)

    def body(u_ref, br_ref, bi_ref, cr_ref, ci_ref, ar_ref, ai_ref, y_ref, sr_ref, si_ref):
        ut = u_ref[...].astype(MXU)
        _s5_in(ut, br_ref, sr_ref)
        _s5_in(ut, bi_ref, si_ref)
        _s5_scan(sr_ref, si_ref, ar_ref[...], ai_ref[...], T, False)
        y = None
        for b in range(S5_NB):
            blk = slice(128 * b, 128 * (b + 1))
            yb = _dot(sr_ref[b], cr_ref[blk, :], NN) - _dot(si_ref[b], ci_ref[blk, :], NN)
            y = yb if y is None else y + yb

        @pl.when(pl.program_id(1) == 0)
        def _():
            y_ref[...] = y

        @pl.when(pl.program_id(1) == 1)
        def _():
            y_ref[...] += y

    scr = pltpu.VMEM((S5_NB, T, 128), F32)
    return pl.pallas_call(
        body, name="s5_fwd", grid=(S5_TILES, 2),
        in_specs=[u_spec, b_spec, b_spec, c_spec, c_spec, a_spec, a_spec],
        out_specs=u_spec, out_shape=jax.ShapeDtypeStruct(u.shape, F32),
        scratch_shapes=[scr, scr],
        compiler_params=_params(("parallel", "arbitrary")),
    )(u, *tiles)


def _s5_bwd(u, dys, du_skip, tiles):
    T = u.shape[0]
    u_spec, b_spec, c_spec, a_spec = _s5_specs(T)

    def body(u_ref, dy_ref, sk_ref, br_ref, bi_ref, cr_ref, ci_ref, ar_ref, ai_ref,
             du_ref, dbr_ref, dbi_ref, dcr_ref, dci_ref, dar_ref, dai_ref, sr_ref, si_ref, lr_ref, li_ref):
        ut, dy = u_ref[...].astype(MXU), dy_ref[...].astype(MXU)
        ar, ai = ar_ref[...], ai_ref[...]
        _s5_in(ut, br_ref, sr_ref)
        _s5_in(ut, bi_ref, si_ref)
        _s5_scan(sr_ref, si_ref, ar, ai, T, False)
        for b in range(S5_NB):
            blk = slice(128 * b, 128 * (b + 1))
            dcr_ref[blk, :] = _dot(sr_ref[b], dy, TN)
            dci_ref[blk, :] = -_dot(si_ref[b], dy, TN)
            lr_ref[b] = _dot(dy, cr_ref[blk, :], NT)
            li_ref[b] = -_dot(dy, ci_ref[blk, :], NT)
        _s5_scan(lr_ref, li_ref, ar, ai, T, True)
        du = None
        head = lax.broadcasted_iota(jnp.int32, (T, 128), 0) < S5_SEG
        seg0 = lax.broadcasted_iota(jnp.int32, (S5_SEG, 128), 0) == 0
        for b in range(S5_NB):
            blk = slice(128 * b, 128 * (b + 1))
            lr, li = lr_ref[b], li_ref[b]
            pr = jnp.where(head, 0.0, pltpu.roll(sr_ref[b], S5_SEG, 0))
            pi = jnp.where(head, 0.0, pltpu.roll(si_ref[b], S5_SEG, 0))
            er = jnp.where(seg0, 0.0, pltpu.roll(sr_ref[b, T - S5_SEG:T, :], 1, 0))
            ei = jnp.where(seg0, 0.0, pltpu.roll(si_ref[b, T - S5_SEG:T, :], 1, 0))
            hr, hi = lr[0:S5_SEG], li[0:S5_SEG]
            dar_ref[:, blk] = _colsum(lr * pr + li * pi) + _colsum(hr * er + hi * ei)
            dai_ref[:, blk] = _colsum(li * pr - lr * pi) + _colsum(hi * er - hr * ei)
            dbr_ref[:, blk] = _dot(ut, lr, TN)
            dbi_ref[:, blk] = _dot(ut, li, TN)
            dub = _dot(lr, br_ref[:, blk], NT) + _dot(li, bi_ref[:, blk], NT)
            du = dub if du is None else du + dub

        @pl.when(pl.program_id(1) == 0)
        def _():
            du_ref[...] = du + sk_ref[...]

        @pl.when(pl.program_id(1) == 1)
        def _():
            du_ref[...] += du

    scr = pltpu.VMEM((S5_NB, T, 128), F32)
    return pl.pallas_call(
        body, name="s5_bwd", grid=(S5_TILES, 2),
        in_specs=[u_spec, u_spec, u_spec, b_spec, b_spec, c_spec, c_spec, a_spec, a_spec],
        out_specs=[u_spec, b_spec, b_spec, c_spec, c_spec, a_spec, a_spec],
        out_shape=[jax.ShapeDtypeStruct(u.shape, F32)] + [jax.ShapeDtypeStruct(t.shape, F32) for t in tiles],
        scratch_shapes=[scr, scr, scr, scr],
        compiler_params=_params(("parallel", "arbitrary")),
    )(u, dys, du_skip, *tiles)


CONV_TILE = 256


def _conv_pre(x, w, b):
    T = x.shape[0]
    row = lax.broadcasted_iota(jnp.int32, x.shape, 0)
    acc = x * w[CONV_K - 1:CONV_K, :] + b
    for lag in range(1, CONV_K):
        acc = acc + jnp.where(row >= lag, pltpu.roll(x, lag, 0), 0.0) * w[CONV_K - 1 - lag:CONV_K - lag, :]
    return acc


def _conv_fwd(x, w, b):
    T, C = x.shape
    col = pl.BlockSpec((T, CONV_TILE), lambda j: (0, j))

    def body(x_ref, w_ref, b_ref, o_ref):
        o_ref[...] = jax.nn.silu(_conv_pre(x_ref[...], w_ref[...], b_ref[...]))

    return pl.pallas_call(
        body, name="conv_fwd", grid=(C // CONV_TILE,),
        in_specs=[col, pl.BlockSpec((CONV_K, CONV_TILE), lambda j: (0, j)), pl.BlockSpec((1, CONV_TILE), lambda j: (0, j))],
        out_specs=col, out_shape=jax.ShapeDtypeStruct((T, C), F32),
        compiler_params=_params(("parallel",)),
    )(x, w, b)


def _conv_bwd(x, w, b, dout, col0, name):
    T, C = dout.shape
    off = col0 // CONV_TILE
    xcol = pl.BlockSpec((T, CONV_TILE), lambda j: (0, j + off))
    dcol = pl.BlockSpec((T, CONV_TILE), lambda j: (0, j))

    def body(x_ref, w_ref, b_ref, d_ref, dx_ref, dw_ref, db_ref):
        x, w = x_ref[...], w_ref[...]
        _, vjp = jax.vjp(jax.nn.silu, _conv_pre(x, w, b_ref[...]))
        (dy,) = vjp(d_ref[...])
        row = lax.broadcasted_iota(jnp.int32, x.shape, 0)
        dx = dy * w[CONV_K - 1:CONV_K, :]
        dw_ref[CONV_K - 1:CONV_K, :] = _colsum(dy * x)
        for lag in range(1, CONV_K):
            ahead = jnp.where(row < T - lag, pltpu.roll(dy, T - lag, 0), 0.0)
            dx = dx + ahead * w[CONV_K - 1 - lag:CONV_K - lag, :]
            dw_ref[CONV_K - 1 - lag:CONV_K - lag, :] = _colsum(ahead * x)
        dx_ref[...] = dx.astype(dx_ref.dtype)
        db_ref[...] = _colsum(dy)

    return pl.pallas_call(
        body, name=name, grid=(C // CONV_TILE,),
        in_specs=[xcol, pl.BlockSpec((CONV_K, CONV_TILE), lambda j: (0, j + off)),
                  pl.BlockSpec((1, CONV_TILE), lambda j: (0, j + off)), dcol],
        out_specs=[dcol, pl.BlockSpec((CONV_K, CONV_TILE), lambda j: (0, j)), pl.BlockSpec((1, CONV_TILE), lambda j: (0, j))],
        out_shape=[jax.ShapeDtypeStruct((T, C), BF16), jax.ShapeDtypeStruct((CONV_K, C), F32), jax.ShapeDtypeStruct((1, C), F32)],
        compiler_params=_params(("parallel",)),
    )(x, w, b, dout)


def _ssd_common(x_ref, b_ref, c_ref, dtr_ref, bias_ref, alog_ref, tri_ref, triu_ref):
    Q = x_ref.shape[0]
    x, Bm, Cm = x_ref[...], b_ref[...], c_ref[...]
    pre = dtr_ref[...] + bias_ref[...]
    dt = _softplus(pre)
    A = -jnp.exp(alog_ref[...])
    adt = dt * A
    ac4 = _dot(tri_ref[...], adt, NN, HI)
    ar4 = _dot(adt, triu_ref[...], TN, HI)
    cb = _dot(Cm, Bm, NT)
    ii = lax.broadcasted_iota(jnp.int32, (Q, Q), 0)
    jj = lax.broadcasted_iota(jnp.int32, (Q, Q), 1)
    atot4 = ac4[Q - 1:Q, :]
    return x, Bm, Cm, pre, dt, A, ac4, ar4, cb, ii >= jj, atot4


def _ssd_decay(k, ac4, ar4, causal):
    seg = ac4[:, k:k + 1] - ar4[k:k + 1, :]
    return jnp.where(causal, jnp.exp(jnp.where(causal, seg, 0.0)), 0.0)


def _per_head(c4, n):
    lane = lax.broadcasted_iota(jnp.int32, (c4.shape[0], n), 1)
    out = jnp.broadcast_to(c4[:, 0:1], (c4.shape[0], n))
    for k in range(1, SSD_HPG):
        out = jnp.where(lane >= SSD_HEADDIM * k, c4[:, k:k + 1], out)
    return out


def _head_sums(v):
    n = SSD_HPG * SSD_HEADDIM
    row = lax.broadcasted_iota(jnp.int32, (n, SSD_HPG), 0)
    col = lax.broadcasted_iota(jnp.int32, (n, SSD_HPG), 1)
    member = jnp.where((row >= SSD_HEADDIM * col) & (row < SSD_HEADDIM * (col + 1)), 1.0, 0.0)
    return _dot(v, member, NN, HI)


def _head_rows(c4):
    n = SSD_HPG * SSD_HEADDIM
    row = lax.broadcasted_iota(jnp.int32, (n, 1), 0)
    out = jnp.broadcast_to(c4[:, 0:1], (n, 1))
    for k in range(1, SSD_HPG):
        out = jnp.where(row >= SSD_HEADDIM * k, c4[:, k:k + 1], out)
    return out


def _ssd_specs(T, Q, rev):
    NC = T // Q
    cc = (lambda c: NC - 1 - c) if rev else (lambda c: c)
    W, N, S = SSD_HPG * SSD_HEADDIM, SSD_STATE, SSD_STEP_GROUPS
    x_spec = pl.BlockSpec((Q, S * W), lambda g, c: (cc(c), g))
    b_spec = pl.BlockSpec((Q, S * N), lambda g, c: (cc(c), D_INNER // (S * N) + g))
    c_spec = pl.BlockSpec((Q, S * N), lambda g, c: (cc(c), (D_INNER + SSD_GROUPS * N) // (S * N) + g))
    dt_spec = pl.BlockSpec((S, Q, SSD_HPG), lambda g, c: (g, cc(c), 0))
    p_spec = pl.BlockSpec((S, 1, SSD_HPG), lambda g, c: (g, 0, 0))
    tri_spec = pl.BlockSpec((Q, Q), lambda g, c: (0, 0))
    h_spec = pl.BlockSpec((S, None, W, N), lambda g, c: (g, cc(c), 0, 0))
    return x_spec, b_spec, c_spec, dt_spec, p_spec, tri_spec, h_spec


def _ssd_views(gi, wide, narrow, lead):
    W, N = SSD_HPG * SSD_HEADDIM, SSD_STATE
    return ([r.at[:, pl.ds(W * gi, W)] for r in wide] + [r.at[:, pl.ds(N * gi, N)] for r in narrow]
            + [r.at[gi] for r in lead])


def _tri(Q):
    tri = jnp.tril(jnp.ones((Q, Q), F32))
    return tri, tri.T


def _ssd_fwd(xc, dtr8, bias8, alog8, dsk8):
    T, Q = xc.shape[0], SSD_Q
    NC = T // Q
    P, W = SSD_HEADDIM, SSD_HPG * SSD_HEADDIM
    x_spec, b_spec, c_spec, dt_spec, p_spec, tri_spec, h_spec = _ssd_specs(T, Q, False)

    def body(x_ref, b_ref, c_ref, dtr_ref, bias_ref, alog_ref, dsk_ref, tri_ref, triu_ref, y_ref, hs_ref, h_scr):
        for gi in range(SSD_STEP_GROUPS):
            xv, yv, bv, cv, dtv, biasv, alogv, dskv, hsv, hv = _ssd_views(
                gi, (x_ref, y_ref), (b_ref, c_ref), (dtr_ref, bias_ref, alog_ref, dsk_ref, hs_ref, h_scr))
            one(xv, bv, cv, dtv, biasv, alogv, dskv, tri_ref, triu_ref, yv, hsv, hv)

    def one(x_ref, b_ref, c_ref, dtr_ref, bias_ref, alog_ref, dsk_ref, tri_ref, triu_ref, y_ref, hs_ref, h_scr):
        @pl.when(pl.program_id(1) == 0)
        def _():
            h_scr[...] = jnp.zeros_like(h_scr)

        hin = h_scr[...]
        hs_ref[...] = hin
        x, Bm, Cm, pre, dt, A, ac4, ar4, cb, causal, atot4 = _ssd_common(
            x_ref, b_ref, c_ref, dtr_ref, bias_ref, alog_ref, tri_ref, triu_ref)
        xdt = x * _per_head(dt, W)
        rest = _dot(Cm, hin, NT) * _per_head(jnp.exp(ac4), W) + _per_head(dsk_ref[...], W) * x
        for k in range(SSD_HPG):
            blk = slice(P * k, P * (k + 1))
            G = cb * _ssd_decay(k, ac4, ar4, causal)
            y_ref[:, blk] = _dot(G, xdt[:, blk], NN) + rest[:, blk]
        xw = xdt * _per_head(jnp.exp(atot4 - ac4), W)
        h_scr[...] = _head_rows(jnp.exp(atot4)) * hin + _dot(xw, Bm, TN)

    tri, triu = _tri(Q)
    return pl.pallas_call(
        body, name="ssd_fwd", grid=(SSD_GROUPS // SSD_STEP_GROUPS, NC),
        in_specs=[x_spec, b_spec, c_spec, dt_spec, p_spec, p_spec, p_spec, tri_spec, tri_spec],
        out_specs=[pl.BlockSpec((Q, SSD_STEP_GROUPS * W), lambda g, c: (c, g)), h_spec],
        out_shape=[jax.ShapeDtypeStruct((T, D_INNER), F32),
                   jax.ShapeDtypeStruct((SSD_GROUPS, NC, W, SSD_STATE), F32)],
        scratch_shapes=[pltpu.VMEM((SSD_STEP_GROUPS, W, SSD_STATE), F32)],
        compiler_params=_params(("parallel", "arbitrary")),
    )(xc, xc, xc, dtr8, bias8, alog8, dsk8, tri, triu)


def _ssd_bwd(xc, dtr8, bias8, alog8, dsk8, hs, dy):
    T, Q = xc.shape[0], SSD_Q
    NC = T // Q
    P, W = SSD_HEADDIM, SSD_HPG * SSD_HEADDIM
    x_spec, b_spec, c_spec, dt_spec, p_spec, tri_spec, h_spec = _ssd_specs(T, Q, True)
    dy_spec = pl.BlockSpec((Q, SSD_STEP_GROUPS * W), lambda g, c: (NC - 1 - c, g))
    dbc_spec = pl.BlockSpec((Q, SSD_STEP_GROUPS * SSD_STATE), lambda g, c: (NC - 1 - c, g))

    def body(x_ref, b_ref, c_ref, dtr_ref, bias_ref, alog_ref, dsk_ref, tri_ref, triu_ref, hs_ref, dy_ref,
             dx_ref, db_ref, dc_ref, ddtr_ref, dbias_ref, dalog_ref, ddsk_ref, dh_scr):
        for gi in range(SSD_STEP_GROUPS):
            (xv, dyv, dxv, bv, cv, dbv, dcv, dtv, biasv, alogv, dskv, hsv, ddtv, dbiasv, dalogv, ddskv, dhv) = _ssd_views(
                gi, (x_ref, dy_ref, dx_ref), (b_ref, c_ref, db_ref, dc_ref),
                (dtr_ref, bias_ref, alog_ref, dsk_ref, hs_ref, ddtr_ref, dbias_ref, dalog_ref, ddsk_ref, dh_scr))
            one(xv, bv, cv, dtv, biasv, alogv, dskv, tri_ref, triu_ref, hsv, dyv,
                dxv, dbv, dcv, ddtv, dbiasv, dalogv, ddskv, dhv)

    def one(x_ref, b_ref, c_ref, dtr_ref, bias_ref, alog_ref, dsk_ref, tri_ref, triu_ref, hs_ref, dy_ref,
            dx_ref, db_ref, dc_ref, ddtr_ref, dbias_ref, dalog_ref, ddsk_ref, dh_scr):
        @pl.when(pl.program_id(1) == 0)
        def _():
            dh_scr[...] = jnp.zeros_like(dh_scr)
            dbias_ref[...] = jnp.zeros_like(dbias_ref)
            dalog_ref[...] = jnp.zeros_like(dalog_ref)
            ddsk_ref[...] = jnp.zeros_like(ddsk_ref)

        x, Bm, Cm, pre, dt, A, ac4, ar4, cb, causal, atot4 = _ssd_common(
            x_ref, b_ref, c_ref, dtr_ref, bias_ref, alog_ref, tri_ref, triu_ref)
        dyv, hin, dho = dy_ref[...], hs_ref[...], dh_scr[...]
        dt_w = _per_head(dt, W)
        xdt = x * dt_w
        E4, F4, etot4 = jnp.exp(ac4), jnp.exp(atot4 - ac4), jnp.exp(atot4)
        F_w = _per_head(F4, W)
        ddsk_ref[...] += _head_sums(_colsum(dyv * x))
        Z = _dot(Cm, hin, NT)
        dZ = dyv * _per_head(E4, W)
        dac4 = _head_sums(dyv * Z) * E4
        dC = _dot(dZ, hin, NN)
        dh_scr[...] = _dot(dZ, Cm, TN) + _head_rows(etot4) * dho
        per_row = jnp.sum(dho * hin, axis=1, keepdims=True)
        lane4 = lax.broadcasted_iota(jnp.int32, (1, SSD_HPG), 1)
        datot4 = jnp.zeros((1, SSD_HPG), F32)
        for k in range(SSD_HPG):
            datot4 = jnp.where(lane4 == k, jnp.sum(per_row[P * k:P * (k + 1)], keepdims=True), datot4)
        datot4 = datot4 * etot4
        dxw = _dot(Bm, dho, NT)
        dB = _dot(xdt * F_w, dho, NN)
        dFa4 = _head_sums(dxw * xdt) * F4
        datot4 = datot4 + _colsum(dFa4)
        dac4 = dac4 - dFa4
        dcb = jnp.zeros((Q, Q), F32)
        lane_q4 = lax.broadcasted_iota(jnp.int32, (Q, SSD_HPG), 1)
        sub_4q = lax.broadcasted_iota(jnp.int32, (SSD_HPG, Q), 0)
        row_sums = jnp.zeros((Q, SSD_HPG), F32)
        col_sums = jnp.zeros((SSD_HPG, Q), F32)
        dxdt_heads = []
        for k in range(SSD_HPG):
            blk = slice(P * k, P * (k + 1))
            L = _ssd_decay(k, ac4, ar4, causal)
            G = cb * L
            dG = _dot(dyv[:, blk], xdt[:, blk], NT)
            dxdt_heads.append(_dot(G, dyv[:, blk], TN))
            dcb = dcb + dG * L
            Mseg = dG * G
            row_sums = jnp.where(lane_q4 == k, jnp.sum(Mseg, axis=1, keepdims=True), row_sums)
            col_sums = jnp.where(sub_4q == k, jnp.sum(Mseg, axis=0, keepdims=True), col_sums)
        dxdt = dxw * F_w + jnp.concatenate(dxdt_heads, axis=1)
        last = lax.broadcasted_iota(jnp.int32, (Q, SSD_HPG), 0) == Q - 1
        dac4 = dac4 + row_sums + jnp.where(last, datot4, 0.0)
        dadt4 = _dot(triu_ref[...], dac4, NN, HI) - _dot(triu_ref[...], col_sums, NT, HI)
        ddt4 = _head_sums(dxdt * x) + dadt4 * A
        dalog_ref[...] += _colsum(dadt4 * dt) * A
        ddtr4 = ddt4 * jax.nn.sigmoid(pre)
        dbias_ref[...] += _colsum(ddtr4)
        ddtr_ref[...] = ddtr4
        dx_ref[...] = (_per_head(dsk_ref[...], W) * dyv + dxdt * dt_w).astype(dx_ref.dtype)
        db_ref[...] = dB + _dot(dcb, Cm, TN)
        dc_ref[...] = dC + _dot(dcb, Bm, NN)

    tri, triu = _tri(Q)
    return pl.pallas_call(
        body, name="ssd_bwd", grid=(SSD_GROUPS // SSD_STEP_GROUPS, NC),
        in_specs=[x_spec, b_spec, c_spec, dt_spec, p_spec, p_spec, p_spec, tri_spec, tri_spec, h_spec, dy_spec],
        out_specs=[dy_spec, dbc_spec, dbc_spec, dt_spec, p_spec, p_spec, p_spec],
        out_shape=[jax.ShapeDtypeStruct((T, D_INNER), F32),
                   jax.ShapeDtypeStruct((T, SSD_GROUPS * SSD_STATE), F32),
                   jax.ShapeDtypeStruct((T, SSD_GROUPS * SSD_STATE), F32),
                   jax.ShapeDtypeStruct(dtr8.shape, F32)] + [jax.ShapeDtypeStruct(bias8.shape, F32)] * 3,
        scratch_shapes=[pltpu.VMEM((SSD_STEP_GROUPS, W, SSD_STATE), F32)],
        compiler_params=_params(("parallel", "arbitrary")),
    )(xc, xc, xc, dtr8, bias8, alog8, dsk8, tri, triu, hs, dy)


def _to_groups(v):
    return v.reshape(SSD_GROUPS, 1, SSD_HPG)


def _mixer_fwd(x, p, w):
    T, D = x.shape
    (hn,) = _rowwise(lambda xt, g: (_rms(xt, g),), [x], [p["mix_norm"]], [(D, BF16)], name="mix_norm")
    winT = w["w_in"]
    u, z, xbc, dtr, gl = [
        _matmul(hn, winT[IN_OFFS[i]:IN_OFFS[i + 1]], "nt", name=f"mix_in{i}") for i in range(5)]
    tiles = _s5_derive(*[p[k][0] for k in ("s5_A_re", "s5_A_im", "s5_log_dt", "s5_B_re", "s5_B_im", "s5_C_re", "s5_C_im")])
    u = _to_seg(u)
    ys = _s5_fwd(u, tiles)
    (g,) = _rowwise(lambda yt, ut, d: (jax.nn.gelu(yt + d * ut),), [ys, u], [p["s5_D"]], [(D, F32)], name="s5_gelu")
    y5 = _matmul(g, w["s5_w_glu"], "nn", name="s5_glu", out=(BF16,),
                 epilogue=lambda acc, gt, b: (gt * jax.nn.sigmoid(acc + b),), extras=[g, p["s5_b_glu"]])
    xc = _conv_fwd(xbc, w["conv_w"], p["conv_b"])
    dtr8 = dtr.reshape(T, SSD_GROUPS, SSD_HPG).transpose(1, 0, 2)
    ssd_p = [_to_groups(p[k]) for k in ("ssd_dt_bias", "ssd_A_log", "ssd_D")]
    yssd_raw, hs = _ssd_fwd(xc, dtr8, *ssd_p)
    (yssd,) = _rowwise(lambda yt, zt, nw: (_rms(yt * jax.nn.silu(zt), nw),), [yssd_raw, z], [p["ssd_norm"]],
                       [(D_INNER, BF16)], name="ssd_gate")
    p5 = _from_seg(_matmul(y5, w["w_proj_s5"], "nn", name="mix_p5"))
    pssd = _matmul(yssd, w["w_proj_ssd"], "nn", name="mix_pssd")

    def merge(glt, at, bt, bg):
        gates = jax.nn.sigmoid(glt + bg)
        return (gates[:, :D] * at + gates[:, D:] * bt,)

    (merged,) = _rowwise(merge, [gl, p5, pssd], [p["b_gate"]], [(D, BF16)], name="mix_merge")
    y = _matmul(merged, w["w_out"], "nn", name="mix_out", epilogue=lambda acc, xt: (xt + acc,), extras=[x])
    res = dict(hn=hn, u=u, z=z, xbc=xbc, gl=gl, tiles=tiles, ys=ys, g=g, y5=y5, xc=xc, dtr8=dtr8, ssd_p=ssd_p,
               yssd_raw=yssd_raw, hs=hs, yssd=yssd, p5=p5, pssd=pssd, merged=merged)
    return y, res


def _mixer_bwd(dy, dy16, x, p, w, r):
    T, D = x.shape
    gp = {}
    dmerged = _matmul(dy16, w["w_out"], "nt", name="mix_d_merged")
    gp["w_out"] = _matmul(r["merged"], dy16, "tn", name="mix_d_wout")

    def merge_bwd(glt, at, bt, dm, bg):
        def f(q, a_, b_):
            gates = jax.nn.sigmoid(q + bg)
            return gates[:, :D] * a_ + gates[:, D:] * b_
        _, vjp = jax.vjp(f, glt, at, bt)
        dq, da_, db_ = vjp(dm)
        return da_, db_, dq, _colsum(dq)

    dp5, dpssd, dgl, gp["b_gate"] = _rowwise(
        merge_bwd, [r["gl"], r["p5"], r["pssd"], dmerged], [p["b_gate"]],
        [(D, BF16), (D, BF16), (2 * D, BF16)], [((1, 2 * D), F32)], name="mix_d_merge")
    dyssd = _matmul(dpssd, w["w_proj_ssd"], "nt", name="mix_d_yssd")
    gp["w_proj_ssd"] = _matmul(r["yssd"], dpssd, "tn", name="mix_d_wpssd")

    def gate_bwd(yt, zt, dyt, nw):
        _, vjp = jax.vjp(lambda a_, b_, c_: _rms(a_ * jax.nn.silu(b_), c_), yt, zt, nw)
        return vjp(dyt)

    dyraw, dz, gp["ssd_norm"] = _rowwise(
        gate_bwd, [r["yssd_raw"], r["z"], dyssd], [p["ssd_norm"]],
        [(D_INNER, F32), (D_INNER, BF16)], [((1, D_INNER), F32)], name="ssd_d_gate")
    dxs, dBm, dCm, ddtr8, dbias8, dalog8, ddsk8 = _ssd_bwd(r["xc"], r["dtr8"], *r["ssd_p"], r["hs"], dyraw)
    gp["ssd_dt_bias"], gp["ssd_A_log"], gp["ssd_D"] = [v.reshape(1, SSD_HEADS) for v in (dbias8, dalog8, ddsk8)]
    ddtr = ddtr8.transpose(1, 0, 2).reshape(T, SSD_HEADS)
    conv = [_conv_bwd(r["xbc"], w["conv_w"], p["conv_b"], d, c0, f"conv_bwd{i}")
            for i, (d, c0) in enumerate(((dxs, 0), (dBm, D_INNER), (dCm, D_INNER + SSD_GROUPS * SSD_STATE)))]
    dxbc = [c[0] for c in conv]
    gp["conv_w"] = jnp.concatenate([c[1] for c in conv], axis=1)
    gp["conv_b"] = jnp.concatenate([c[2] for c in conv], axis=1)

    dp5 = _to_seg(dp5)
    dy5 = _matmul(dp5, w["w_proj_s5"], "nt", name="mix_d_y5")
    gp["w_proj_s5"] = _matmul(r["y5"], dp5, "tn", name="mix_d_wp5")
    g = r["g"]

    def glu_ep(acc, gt, b, dyt):
        _, vjp = jax.vjp(lambda g_, t_: g_ * jax.nn.sigmoid(t_ + b), gt, acc)
        return vjp(dyt)

    dg1, dt_ = _matmul(g, w["s5_w_glu"], "nn", name="s5_d_glu", out=(F32, BF16), epilogue=glu_ep,
                       extras=[g, p["s5_b_glu"], dy5])
    gp["s5_w_glu"] = _matmul(g, dt_, "tn", name="s5_d_wglu")
    dg = _matmul(dt_, w["s5_w_glu"], "nt", name="s5_d_g", epilogue=lambda acc, e: (acc + e,), extras=[dg1])

    def gelu_bwd(yt, ut, dgt, dtt, d):
        _, vjp = jax.vjp(lambda y_, d_: jax.nn.gelu(y_ + d_ * ut), yt, d)
        dys, dd = vjp(dgt)
        return dys, dys * d, dd, _colsum(dtt.astype(F32))

    dys, dusk, gp["s5_D"], gp["s5_b_glu"] = _rowwise(
        gelu_bwd, [r["ys"], r["u"], dg, dt_], [p["s5_D"]], [(D, F32), (D, F32)], [((1, D), F32), ((1, D), F32)],
        name="s5_d_gelu")
    du, *dtiles = _s5_bwd(r["u"], dys, dusk, r["tiles"])
    du = _from_seg(du)

    winT = w["w_in"]
    hn = r["hn"]
    d_rows, dhn = [], None
    cols = [(du, 0, 1024), (dz, 1024, 3072), (dxbc[0], 3072, 5120), (dxbc[1], 5120, 6144), (dxbc[2], 6144, 7168),
            (ddtr, 7168, 7200), (dgl, 7200, 9248)]
    for i, (d, lo, hi) in enumerate(cols):
        d_rows.append(_matmul(d, hn, "tn", name=f"mix_d_win{i}"))
        if dhn is None:
            dhn = _matmul(d, winT[lo:hi], "nn", name=f"mix_d_hn{i}")
        else:
            dhn = _matmul(d, winT[lo:hi], "nn", name=f"mix_d_hn{i}", epilogue=lambda acc, e: (acc + e,), extras=[dhn])
    gp["w_in"] = jnp.concatenate(d_rows, axis=0)

    def norm_bwd(xt, dh, dyt, gn):
        _, vjp = jax.vjp(_rms, xt, gn)
        dx, dgn = vjp(dh)
        return dyt + dx, dyt + dx, dgn

    dx, dx16, gp["mix_norm"] = _rowwise(norm_bwd, [x, dhn, dy], [p["mix_norm"]], [(D, F32), (D, BF16)],
                                        [((1, D), F32)], name="mix_d_norm")
    return dx, dx16, gp, dtiles


def _s5_param_grads(p, dtiles):
    keys = ("s5_A_re", "s5_A_im", "s5_log_dt", "s5_B_re", "s5_B_im", "s5_C_re", "s5_C_im")
    _, vjp = jax.vjp(_s5_derive, *[p[k][0] for k in keys])
    return {k: v[None] for k, v in zip(keys, vjp(tuple(dtiles)))}


def _local_step(x, target, p, w, late_weights, exchange):
    T, D = x.shape
    w = dict(w)
    x1, r1 = _ffn_fwd(x, p["ffn1_norm"], w["ffn1_w_gate"], w["ffn1_w_up"], w["ffn1_w_down"], "ffn1",
                      on_gate=lambda a_: w.update(late_weights("mixer", a_)))
    x2, rm = _mixer_fwd(x1, p, w)
    w.update(late_weights("ffn2", rm["p5"]))
    x3, r2 = _ffn_fwd(x2, p["ffn2_norm"], w["ffn2_w_gate"], w["ffn2_w_up"], w["ffn2_w_down"], "ffn2")

    def head(xt, tt, g):
        def f(x_, g_):
            e = _rms(x_, g_) - tt
            return 0.5 * jnp.sum(jnp.mean(e * e, axis=-1))
        l, (dx_, dg_) = jax.value_and_grad(f, argnums=(0, 1))(xt, g)
        return dx_, dx_, l.reshape(1, 1), dg_

    dx3, dx3_16, loss, d_final = _rowwise(head, [x3, target], [p["final_norm"]], [(D, F32), (D, BF16)],
                                          [((1, 1), F32), ((1, D), F32)], name="loss_head")
    gp = {"final_norm": d_final}

    def ffn_exchange(tag):
        return lambda d_wgT, d_wuT, d_wd, after: exchange(
            tag, {f"{tag}_w_gate": d_wgT, f"{tag}_w_up": d_wuT, f"{tag}_w_down": d_wd}, after)

    dx2, dx2_16, gp["ffn2_norm"] = _ffn_bwd(
        dx3, dx3_16, x2, p["ffn2_norm"], w["ffn2_w_gate"], w["ffn2_w_up"], w["ffn2_w_down"], r2, "ffn2", (),
        ffn_exchange("ffn2"))
    dx1, dx1_16, gm, dtiles = _mixer_bwd(dx2, dx2_16, x1, p, w, rm)
    conv_w_grad = gm.pop("conv_w")
    started = exchange("mixer", {k: gm.pop(k) for k in ("w_out", "w_proj_s5", "w_proj_ssd", "s5_w_glu", "w_in")}, dx1)
    gp.update(gm)
    gp.update(_s5_param_grads(p, dtiles))
    dx0, _, gp["ffn1_norm"] = _ffn_bwd(
        dx1, dx1_16, x, p["ffn1_norm"], w["ffn1_w_gate"], w["ffn1_w_up"], w["ffn1_w_down"], r1, "ffn1", started,
        ffn_exchange("ffn1"))
    return loss, dx0, gp, conv_w_grad


ADAMW_BLOCK = 1 << 19


def _adamw(w, m, v, gs, *, name, emit_g):
    _, R, C = w.shape
    n_g = len(gs)
    n_o = 4 if emit_g else 3
    tm = _rtile(R, max(8, ADAMW_BLOCK // C))
    lead = pl.BlockSpec((None, tm, C), lambda i: (0, i, 0))

    def body(w_ref, m_ref, v_ref, *refs):
        g = refs[0][...]
        for r in refs[1:n_g]:
            g = g + r[...]
        m2 = ADAM_B1 * m_ref[...] + (1.0 - ADAM_B1) * g
        v2 = ADAM_B2 * v_ref[...] + (1.0 - ADAM_B2) * (g * g)
        m_hat = m2 / (1.0 - ADAM_B1 ** ADAM_STEP)
        v_hat = v2 / (1.0 - ADAM_B2 ** ADAM_STEP)
        delta = -ADAM_LR * (m_hat / (jnp.sqrt(v_hat) + ADAM_EPS) + ADAM_WD * w_ref[...])
        for r, val in zip(refs[n_g:], ((g,) if emit_g else ()) + (delta, m2, v2)):
            r[...] = val

    return pl.pallas_call(
        body, name=name, grid=(R // tm,),
        in_specs=[lead] * 3 + [pl.BlockSpec((tm, C), lambda i: (i, 0))] * n_g, out_specs=[lead] * n_o,
        out_shape=[jax.ShapeDtypeStruct((1, R, C), F32)] * n_o,
        compiler_params=_params(("parallel",)),
    )(w, m, v, *gs)


def _rtile(r, cap=512):
    if r <= cap:
        return r
    best = None
    for t in range(8, cap + 1, 8):
        if r % t == 0:
            best = t
    assert best is not None, r
    return best


ANY = pl.BlockSpec(memory_space=pl.ANY)


def _where_am_i():
    x, y, c = lax.axis_index("x"), lax.axis_index("y"), lax.axis_index("c")
    ks = (2 * x + y, 2 * (1 - x) + y, 2 * x + (1 - y), 2 * (1 - x) + (1 - y))
    return x, y, c, ks


def _rc(src, dst, ssem, rsem, to):
    return pltpu.make_async_remote_copy(src_ref=src, dst_ref=dst, send_sem=ssem, recv_sem=rsem,
                                        device_id=to, device_id_type=MESH)


def _gather_shards(srcs):
    n = len(srcs)

    def body(*refs):
        _gather_body(refs[:n], refs[n:2 * n], *refs[2 * n:], extra=())

    return pl.pallas_call(
        body, name="gather_shards",
        in_specs=[ANY] * n, out_specs=[ANY] * n,
        out_shape=[jax.ShapeDtypeStruct((4,) + s.shape, s.dtype) for s in srcs],
        scratch_shapes=[pltpu.SemaphoreType.DMA((n, 6)), pltpu.SemaphoreType.DMA((n, 6)), pltpu.SemaphoreType.DMA((n,))],
        compiler_params=pltpu.CompilerParams(has_side_effects=True),
    )(*srcs)


def _handshake_all():
    x, y, c = lax.axis_index("x"), lax.axis_index("y"), lax.axis_index("c")
    barrier = pltpu.get_barrier_semaphore()
    for dx, dy, dc in itertools.product((0, 1), repeat=3):
        if (dx, dy, dc) != (0, 0, 0):
            to = (1 - x if dx else x, 1 - y if dy else y, 1 - c if dc else c)
            pl.semaphore_signal(barrier, inc=1, device_id=to, device_id_type=MESH)
    pl.semaphore_wait(barrier, 7)


SEQUENCER = dict(axis_name="seq", num_cores=1)


def _gather_shards_async(srcs, token, name, collective_id):
    n = len(srcs)

    def body(*refs):
        tok, tok_out = refs[n], refs[2 * n + 1]
        ssem, rsem, lsem = refs[2 * n + 2:]
        _handshake_all()
        _gather_body(refs[:n], refs[n + 1:2 * n + 1], ssem, rsem, lsem,
                     extra=(pltpu.make_async_copy(tok, tok_out, lsem.at[n]),))

    res = pl.kernel(
        body, name=name,
        out_type=[jax.ShapeDtypeStruct((4,) + s.shape, s.dtype) for s in srcs]
        + [jax.ShapeDtypeStruct(token.shape, token.dtype)],
        mesh=plsc.ScalarSubcoreMesh(**SEQUENCER),
        scratch_types=[pltpu.SemaphoreType.DMA((n, 6)), pltpu.SemaphoreType.DMA((n, 6)), pltpu.SemaphoreType.DMA((n + 1,))],
        compiler_params=pltpu.CompilerParams(collective_id=collective_id),
    )(*srcs, token)
    return res[:n], res[n]


def _gather_body(src, out, ssem, rsem, lsem, extra):
        n = len(src)
        x, y, c, (k_me, k_x, k_y, k_d) = _where_am_i()
        sib = (x, y, 1 - c)
        local = [pltpu.make_async_copy(src[i], out[i].at[k_me], lsem.at[i]) for i in range(n)] + list(extra)
        for cp in local:
            cp.start()

        def own(i, q, to):
            return _rc(src[i], out[i].at[k_me], ssem.at[i, q], rsem.at[i, q], to)

        def slab(i, q, k, to):
            return _rc(out[i].at[k], out[i].at[k], ssem.at[i, q], rsem.at[i, q], to)

        @pl.when(c == 1)
        def _():
            sends = [own(i, 0, (1 - x, y, 1)) for i in range(n)]
            for cp in sends:
                cp.start()
            fwd = [slab(i, 3, k_x, sib) for i in range(n)]
            for i in range(n):
                slab(i, 0, k_x, sib).wait_recv()
                fwd[i].start()
            for i in range(n):
                slab(i, 4, k_y, sib).wait_recv()
                slab(i, 5, k_d, sib).wait_recv()
            for cp in sends + fwd:
                cp.wait_send()

        @pl.when(c == 0)
        def _():
            sends = [own(i, 1, (x, 1 - y, 0)) for i in range(n)] + [own(i, 2, (1 - x, 1 - y, 0)) for i in range(n)]
            for cp in sends:
                cp.start()
            fwd = [slab(i, 4, k_y, sib) for i in range(n)] + [slab(i, 5, k_d, sib) for i in range(n)]
            for i in range(n):
                slab(i, 1, k_y, sib).wait_recv()
                fwd[i].start()
            for i in range(n):
                slab(i, 2, k_d, sib).wait_recv()
                fwd[n + i].start()
            for i in range(n):
                slab(i, 3, k_x, sib).wait_recv()
            for cp in sends + fwd:
                cp.wait_send()

        for cp in local:
            cp.wait()


def _swap_slabs(arrs, n_slabs, name, async_id, token):
    n = len(arrs)
    J = max(n_slabs, 1)

    def body(*refs):
        src, out = refs[:n], refs[n:2 * n]
        ssem, rsem = refs[2 * n:]
        x, y, c, (k_me, k_x, k_y, k_d) = _where_am_i()
        sib = (x, y, 1 - c)
        sel = (jnp.where(c == 1, k_y, k_me), jnp.where(c == 1, k_d, k_x))
        cps = []
        for i in range(n):
            for j in range(J):
                s = src[i].at[sel[j]] if n_slabs else src[i]
                d = out[i].at[j] if n_slabs else out[i]
                cps.append(_rc(s, d, ssem.at[i, j], rsem.at[i, j], sib))
        for cp in cps:
            cp.start()
        for cp in cps:
            cp.wait()

    return _exchange_call(
        body, arrs, [jax.ShapeDtypeStruct(((n_slabs,) if n_slabs else ()) + a.shape[-2:], a.dtype) for a in arrs],
        [pltpu.SemaphoreType.DMA((n, J)), pltpu.SemaphoreType.DMA((n, J))], name, async_id, token)


def _exchange_call(body, arrs, out_shape, sems, name, collective_id, token):
    n, m = len(arrs), len(out_shape)

    def seq_body(*refs):
        tok, tok_out, tok_sem = refs[n], refs[n + 1 + m], refs[-1]
        _handshake_all()
        cp = pltpu.make_async_copy(tok, tok_out, tok_sem)
        cp.start()
        body(*refs[:n], *refs[n + 1:n + 1 + m], *refs[n + 2 + m:-1])
        cp.wait()

    res = pl.kernel(
        seq_body, name=name, out_type=list(out_shape) + [jax.ShapeDtypeStruct(token.shape, token.dtype)],
        mesh=plsc.ScalarSubcoreMesh(**SEQUENCER), scratch_types=list(sems) + [pltpu.SemaphoreType.DMA],
        compiler_params=pltpu.CompilerParams(collective_id=collective_id))(*arrs, token)
    return res[:m], res[m]


def _send_chip_sums(arrs, name, async_id, token):
    n = len(arrs)

    def body(*refs):
        src, out = refs[:n], refs[n:2 * n]
        ssem, rsem = refs[2 * n:]
        x, y, c, _ = _where_am_i()

        @pl.when(c == 1)
        def _():
            cps = [_rc(src[i].at[1], out[i].at[0], ssem.at[i, 0], rsem.at[i, 0], (1 - x, y, 1)) for i in range(n)]
            for cp in cps:
                cp.start()
            for cp in cps:
                cp.wait()

        @pl.when(c == 0)
        def _():
            cps = [_rc(src[i].at[0], out[i].at[0], ssem.at[i, 0], rsem.at[i, 0], (x, 1 - y, 0)) for i in range(n)]
            cps += [_rc(src[i].at[1], out[i].at[1], ssem.at[i, 1], rsem.at[i, 1], (1 - x, 1 - y, 0)) for i in range(n)]
            for cp in cps:
                cp.start()
            for cp in cps:
                cp.wait()

    return _exchange_call(
        body, arrs, [jax.ShapeDtypeStruct(a.shape, a.dtype) for a in arrs],
        [pltpu.SemaphoreType.DMA((n, 2)), pltpu.SemaphoreType.DMA((n, 2))], name, async_id, token)


def _chip_sum(g, recv, sel, name):
    _, r, C = g.shape
    tr = _rtile(r)

    def body(sel_ref, g_ref, r_ref, o32_ref, o16_ref):
        s = g_ref[...] + r_ref[...]
        o32_ref[...] = s
        o16_ref[...] = s.astype(BF16)

    blk = pl.BlockSpec((None, tr, C), lambda j, t, sel_ref: (j, t, 0))
    return pl.pallas_call(
        body, name=name,
        grid_spec=pltpu.PrefetchScalarGridSpec(
            num_scalar_prefetch=1, grid=(2, r // tr),
            in_specs=[pl.BlockSpec((None, tr, C), lambda j, t, sel_ref: (sel_ref[j], t, 0)), blk],
            out_specs=[blk, blk]),
        out_shape=[jax.ShapeDtypeStruct((2, r, C), F32), jax.ShapeDtypeStruct((2, r, C), BF16)],
        compiler_params=_params(("arbitrary", "arbitrary")),
    )(sel, g, recv)


def _cross_sum(s32, got, north, after, name):
    _, r, C = s32.shape
    tr = _rtile(r)

    def body(f_ref, p_ref, a_ref, b_ref, after_ref, o_ref):
        a = a_ref[...].astype(F32)

        @pl.when(f_ref[0] == 1)
        def _():
            o_ref[...] = p_ref[...] + a

        @pl.when(f_ref[0] == 0)
        def _():
            o_ref[...] = a + b_ref[...].astype(F32)

    return pl.pallas_call(
        body, name=name,
        grid_spec=pltpu.PrefetchScalarGridSpec(
            num_scalar_prefetch=1, grid=(r // tr,),
            in_specs=[pl.BlockSpec((None, tr, C), lambda t, f: (0, t, 0)),
                      pl.BlockSpec((None, tr, C), lambda t, f: (0, t, 0)),
                      pl.BlockSpec((None, tr, C), lambda t, f: (1 - f[0], t, 0)), ANY],
            out_specs=pl.BlockSpec((tr, C), lambda t, f: (t, 0))),
        out_shape=jax.ShapeDtypeStruct((r, C), F32),
        compiler_params=_params(("arbitrary",)),
    )(north, s32, got, got, after)


class _Reducer:
    def __init__(self, token):
        self.token, self.pending, self.pairs = token, None, {}

    def begin(self, tag, ids, grads, after):
        names = list(grads)
        arrs = [grads[n].reshape(4, -1, D_MODEL) for n in names]
        x, y, c, (k_me, k_x, k_y, k_d) = _where_am_i()
        sel = jnp.stack([jnp.where(c == 1, k_me, k_y), jnp.where(c == 1, k_x, k_d)]).astype(jnp.int32)
        from_sib, self.token = _swap_slabs(arrs, 2, f"swap_grad_slabs_{tag}", ids[0], self.token)
        sums = [_chip_sum(g, r, sel, f"chip_sum_{tag}{i}") for i, (g, r) in enumerate(zip(arrs, from_sib))]
        self.finish(after)
        got, self.token = _send_chip_sums([s16 for _, s16 in sums], f"send_chip_sums_{tag}", ids[1], self.token)
        self.pending = (tag, ids[2], names, sums, got)
        return tuple(s16 for _, s16 in sums)

    def finish(self, after):
        if self.pending is None:
            return
        tag, cid, names, sums, got = self.pending
        north = jnp.reshape(lax.axis_index("c"), (1,)).astype(jnp.int32)
        parts = [_cross_sum(s32, b, north, after, f"cross_sum_{tag}{i}") for i, ((s32, _), b) in enumerate(zip(sums, got))]
        theirs, self.token = _swap_slabs(parts, 0, f"swap_parts_{tag}", cid, self.token)
        self.pairs.update(zip(names, zip(parts, theirs)))
        self.pending = None


def _all_reduce_small(v, token, collective_id):
    R, C = v.shape
    tr = R

    def body(v_ref, buf, ssem, rsem, lsem):
        x, y, c = lax.axis_index("x"), lax.axis_index("y"), lax.axis_index("c")
        me, sib = (x, y, c), (x, y, 1 - c)
        chips = [(1 - x, y), (x, 1 - y), (1 - x, 1 - y)]

        def slot(px, py, pc):
            return buf.at[4 * px + 2 * py + pc]

        def copy(k, block, to, src=None):
            return _rc(slot(*block) if src is None else src, slot(*block), ssem.at[k], rsem.at[k], to)

        mine = pltpu.make_async_copy(v_ref, slot(*me), lsem)
        mine.start()
        first = [copy(0, me, sib, src=v_ref)] + [copy(1 + j, me, (*chip, c), src=v_ref) for j, chip in enumerate(chips)]
        for cp in first:
            cp.start()
        passed = [copy(4 + j, (*chip, c), sib) for j, chip in enumerate(chips)]
        for j, chip in enumerate(chips):
            copy(1 + j, (*chip, c), me).wait_recv()
            passed[j].start()
        copy(0, sib, me).wait_recv()
        for j, chip in enumerate(chips):
            copy(4 + j, (*chip, 1 - c), me).wait_recv()
        for cp in first + passed:
            cp.wait_send()
        mine.wait()

    (gathered,), token = _exchange_call(
        body, [v], [jax.ShapeDtypeStruct((8, R, C), F32)],
        [pltpu.SemaphoreType.DMA((7,)), pltpu.SemaphoreType.DMA((7,)), pltpu.SemaphoreType.DMA],
        "gather_small", collective_id, token)

    def add(g_ref, o_ref):
        acc = g_ref[0]
        for k in range(1, 8):
            acc = acc + g_ref[k]
        o_ref[...] = acc

    total = pl.pallas_call(
        add, name="sum_small", grid=(R // tr,),
        in_specs=[pl.BlockSpec((8, tr, C), lambda i: (0, i, 0))], out_specs=pl.BlockSpec((tr, C), lambda i: (i, 0)),
        out_shape=jax.ShapeDtypeStruct((R, C), F32), compiler_params=_params(("parallel",)),
    )(gathered)
    return total, token


WEIGHTS = ['ffn1_norm', 'ffn1_w_gate', 'ffn1_w_up', 'ffn1_w_down', 'mix_norm', 'w_in', 'conv_w', 'conv_b', 's5_A_re',
           's5_A_im', 's5_log_dt', 's5_B_re', 's5_B_im', 's5_C_re', 's5_C_im', 's5_D', 's5_w_glu', 's5_b_glu',
           'ssd_A_log', 'ssd_dt_bias', 'ssd_D', 'ssd_norm', 'w_proj_s5', 'w_proj_ssd', 'b_gate', 'w_out', 'ffn2_norm',
           'ffn2_w_gate', 'ffn2_w_up', 'ffn2_w_down', 'final_norm']
ARGS = ['x'] + WEIGHTS + ['loss_target'] + ['m_' + n for n in WEIGHTS] + ['v_' + n for n in WEIGHTS]
COL_SHARDED = ('ffn1_w_gate', 'ffn1_w_up', 'w_in', 'ffn2_w_gate', 'ffn2_w_up')
ROW_SHARDED = ('ffn1_w_down', 's5_w_glu', 'w_proj_s5', 'w_proj_ssd', 'w_out', 'ffn2_w_down')
MATRICES = COL_SHARDED + ROW_SHARDED
GATHER_FIRST = ('ffn1_w_gate', 'ffn1_w_up')
GATHER_FIRST_LATER = ('ffn1_w_down',)
GATHER_SECOND = ('w_in', 'conv_w', 's5_w_glu', 'w_proj_s5', 'w_proj_ssd', 'w_out')
GATHER_THIRD = ('ffn2_w_gate', 'ffn2_w_up', 'ffn2_w_down')
SMALL = [n for n in WEIGHTS if n not in MATRICES]


def _pack(arrs, width=1024):
    flat = jnp.concatenate([a.reshape(-1).astype(F32) for a in arrs])
    rows = -(-flat.shape[0] // (8 * width)) * 8
    return jnp.pad(flat, (0, rows * width - flat.shape[0])).reshape(rows, width)


def _unpack(packed, shapes):
    flat, out, o = packed.reshape(-1), [], 0
    for s in shapes:
        n = math.prod(s)
        out.append(flat[o:o + n].reshape(s))
        o += n
    return out


def kernel(x, ffn1_norm, ffn1_w_gate, ffn1_w_up, ffn1_w_down, mix_norm, w_in, conv_w, conv_b, s5_A_re, s5_A_im, s5_log_dt, s5_B_re, s5_B_im, s5_C_re, s5_C_im, s5_D, s5_w_glu, s5_b_glu, ssd_A_log, ssd_dt_bias, ssd_D, ssd_norm, w_proj_s5, w_proj_ssd, b_gate, w_out, ffn2_norm, ffn2_w_gate, ffn2_w_up, ffn2_w_down, final_norm, loss_target, m_ffn1_norm, m_ffn1_w_gate, m_ffn1_w_up, m_ffn1_w_down, m_mix_norm, m_w_in, m_conv_w, m_conv_b, m_s5_A_re, m_s5_A_im, m_s5_log_dt, m_s5_B_re, m_s5_B_im, m_s5_C_re, m_s5_C_im, m_s5_D, m_s5_w_glu, m_s5_b_glu, m_ssd_A_log, m_ssd_dt_bias, m_ssd_D, m_ssd_norm, m_w_proj_s5, m_w_proj_ssd, m_b_gate, m_w_out, m_ffn2_norm, m_ffn2_w_gate, m_ffn2_w_up, m_ffn2_w_down, m_final_norm, v_ffn1_norm, v_ffn1_w_gate, v_ffn1_w_up, v_ffn1_w_down, v_mix_norm, v_w_in, v_conv_w, v_conv_b, v_s5_A_re, v_s5_A_im, v_s5_log_dt, v_s5_B_re, v_s5_B_im, v_s5_C_re, v_s5_C_im, v_s5_D, v_s5_w_glu, v_s5_b_glu, v_ssd_A_log, v_ssd_dt_bias, v_ssd_D, v_ssd_norm, v_w_proj_s5, v_w_proj_ssd, v_b_gate, v_w_out, v_ffn2_norm, v_ffn2_w_gate, v_ffn2_w_up, v_ffn2_w_down, v_final_norm):
    a = dict(locals())
    assert list(a) == ARGS
    x, target = a['x'][0], a['loss_target'][0]
    k_me = 2 * lax.axis_index("x") + lax.axis_index("y")

    src = {n: a[n][0].T.astype(BF16) for n in COL_SHARDED}
    src.update({n: a[n][0].astype(BF16) for n in ROW_SHARDED})
    src['conv_w'] = a['conv_w'][0]
    def flat(f):
        return f.reshape(-1, f.shape[-1])

    full = _gather_shards([src[n] for n in GATHER_FIRST])
    w = {n: flat(f) for n, f in zip(GATHER_FIRST, full)}
    got, token = _gather_shards_async(
        [src[n] for n in GATHER_FIRST_LATER], full[0][0, :8, :128].astype(F32), "gather_ffn1_down", 14)
    w.update({n: flat(f) for n, f in zip(GATHER_FIRST_LATER, got)})
    reducer = _Reducer(token)

    def gather_later(stage, after):
        token = reducer.token + after[:8, :128].astype(F32)
        calls = {"mixer": ((GATHER_SECOND[:2], "gather_mixer_in", 1), (GATHER_SECOND[2:], "gather_mixer", 12)),
                 "ffn2": ((GATHER_THIRD, "gather_ffn2", 2),)}[stage]
        out = {}
        for names, name, cid in calls:
            got, token = _gather_shards_async([src[n] for n in names], token, name, cid)
            out.update({n: flat(f) for n, f in zip(names, got)})
        reducer.token = token
        if 'conv_w' in out:
            out['conv_w'] = out['conv_w'].reshape(4, CONV_K, -1).transpose(1, 0, 2).reshape(CONV_K, CONV_DIM)
        return out

    p = {n: a[n] for n in SMALL if n != 'conv_w'}
    p['final_norm'] = a['final_norm'][None]
    ids = {"ffn2": (3, 4, 5), "mixer": (6, 7, 8), "ffn1": (9, 10, 11)}
    loss, grad_x, gp, conv_w_grad = _local_step(
        x, target, p, w, gather_later, lambda tag, grads, after: reducer.begin(tag, ids[tag], grads, after))
    gp['conv_w'] = conv_w_grad[None]
    small_shapes = [(1, 1)] + [gp[n].shape if n != 'final_norm' else (1, D_MODEL) for n in SMALL]
    summed, reducer.token = _all_reduce_small(_pack([loss] + [gp[n] for n in SMALL]), reducer.token, 13)
    red = _unpack(summed, small_shapes)
    loss_all = red[0].reshape(())
    gsmall = dict(zip(SMALL, red[1:]))
    gsmall['conv_w'] = lax.dynamic_slice_in_dim(gsmall['conv_w'], k_me * 1024, 1024, axis=2)
    gsmall = {n: g.reshape(a[n].shape) for n, g in gsmall.items()}

    grads, delta, new_m, new_v = {}, {}, {}, {}

    def update(names):
        for n in names:
            turn = (lambda t: jnp.swapaxes(t, 1, 2)) if n in COL_SHARDED else (lambda t: t)
            out = _adamw(turn(a[n]), turn(a['m_' + n]), turn(a['v_' + n]), list(reducer.pairs[n]),
                         name=f"adamw_{n}", emit_g=True)
            grads[n], delta[n], new_m[n], new_v[n] = [turn(t) for t in out]

    update([n for n in MATRICES if n not in GATHER_FIRST + GATHER_FIRST_LATER])
    reducer.finish(delta['w_in'])
    update(GATHER_FIRST + GATHER_FIRST_LATER)
    sw, sm, sv, sg = [_pack([t[n] for n in SMALL])[None] for t in (a, {n: a['m_' + n] for n in SMALL},
                                                                   {n: a['v_' + n] for n in SMALL}, gsmall)]
    d, m2, v2 = _adamw(sw, sm, sv, [sg[0]], name="adamw_small", emit_g=False)
    shapes = [a[n].shape for n in SMALL]
    for n, dd, mm, vv in zip(SMALL, _unpack(d[0], shapes), _unpack(m2[0], shapes), _unpack(v2[0], shapes)):
        grads[n], delta[n], new_m[n], new_v[n] = gsmall[n], dd, mm, vv
    return (loss_all, grad_x[None], *[grads[n] for n in WEIGHTS], *[delta[n] for n in WEIGHTS],
            *[new_m[n] for n in WEIGHTS], *[new_v[n] for n in WEIGHTS])
```

```python
import itertools
import math

import jax
import jax.numpy as jnp
from jax import lax
from jax.experimental import pallas as pl
from jax.experimental.pallas import tpu as pltpu
from jax.experimental.pallas import tpu_sc as plsc

F32 = jnp.float32
BF16 = jnp.bfloat16
MXU = BF16
HI = lax.Precision.HIGHEST

D_MODEL = 1024
D_FF = 2816
EPS = 1e-6
S5_GROUPS, S5_GROUP, S5_STATE = 64, 16, 64
D_INNER = 2048
SSD_HEADDIM, SSD_HEADS, SSD_GROUPS, SSD_HPG, SSD_STATE = 64, 32, 8, 4, 128
CONV_K, CONV_DIM = 4, 4096
IN_OFFS = (0, 1024, 3072, 7168, 7200, 9248)
SSD_Q = 256
SSD_STEP_GROUPS = 4
ADAM_LR, ADAM_B1, ADAM_B2, ADAM_EPS, ADAM_WD, ADAM_STEP = 0.001, 0.9, 0.999, 1e-08, 0.01, 10

VMEM_LIMIT = 56 * 1024 * 1024
MATMUL_VMEM = 32 * 1024 * 1024
MESH = pl.DeviceIdType.MESH

NN = ((1,), (0,))
NT = ((1,), (1,))
TN = ((0,), (0,))


def _dot(a, b, dims, precision=None):
    if precision is None:
        a, b = a.astype(MXU), b.astype(MXU)
    return lax.dot_general(a, b, (dims, ((), ())), precision=precision, preferred_element_type=F32)


def _tile(n, pref):
    if n <= pref:
        return n
    best = None
    for t in range(128, pref + 1, 128):
        if n % t == 0:
            best = t
    assert best is not None, (n, pref)
    return best


def _params(sem):
    return pltpu.CompilerParams(dimension_semantics=sem, vmem_limit_bytes=VMEM_LIMIT)


def _matmul(a, b, mode, *, name, out=((F32),), epilogue=None, extras=(), after=(), tm=1024, tn=1536):
    if mode == "nn":
        (M, K), (_, N) = a.shape, b.shape
    elif mode == "nt":
        (M, K), (N, _) = a.shape, b.shape
    else:
        (K, M), (_, N) = a.shape, b.shape
    tm, tn = _tile(M, tm), _tile(N, tn)

    def vmem(tm_, tn_):
        per = K * tm_ * a.dtype.itemsize + K * tn_ * b.dtype.itemsize
        per += sum(tm_ * tn_ * jnp.dtype(dt).itemsize for dt in out)
        per += sum((1 if e.shape[0] == 1 else tm_) * tn_ * e.dtype.itemsize for e in extras)
        return 2 * per

    while vmem(tm, tn) > MATMUL_VMEM and (tn > 128 or tm > 128):
        if tn >= tm and tn > 128:
            tn = _tile(N, tn - 128)
        else:
            tm = _tile(M, tm - 128)
    bytes_a, bytes_b = a.size * a.dtype.itemsize, b.size * b.dtype.itemsize
    rows_outer = bytes_a + (M // tm) * bytes_b <= (N // tn) * bytes_a + bytes_b
    grid = (M // tm, N // tn) if rows_outer else (N // tn, M // tm)

    def at(f):
        return (lambda i, j: f(i, j)) if rows_outer else (lambda j, i: f(i, j))

    a_spec = pl.BlockSpec((K, tm), at(lambda i, j: (0, i))) if mode == "tn" else pl.BlockSpec((tm, K), at(lambda i, j: (i, 0)))
    b_spec = pl.BlockSpec((tn, K), at(lambda i, j: (j, 0))) if mode == "nt" else pl.BlockSpec((K, tn), at(lambda i, j: (0, j)))
    dims = {"nn": NN, "nt": NT, "tn": TN}[mode]
    e_specs = []
    for e in extras:
        if e.shape[0] == 1:
            e_specs.append(pl.BlockSpec((1, tn), at(lambda i, j: (0, j))))
        else:
            assert e.shape == (M, N), (e.shape, M, N)
            e_specs.append(pl.BlockSpec((tm, tn), at(lambda i, j: (i, j))))
    n_e, n_o, n_a = len(extras), len(out), len(after)

    def body(a_ref, b_ref, *refs):
        acc = _dot(a_ref[...], b_ref[...], dims)
        res = (acc,) if epilogue is None else epilogue(acc, *[r[...] for r in refs[:n_e]])
        for r, v in zip(refs[n_e + n_a:], res):
            r[...] = v.astype(r.dtype)

    res = pl.pallas_call(
        body, name=name, grid=grid,
        in_specs=[a_spec, b_spec] + e_specs + [ANY] * n_a,
        out_specs=[pl.BlockSpec((tm, tn), at(lambda i, j: (i, j))) for _ in range(n_o)],
        out_shape=[jax.ShapeDtypeStruct((M, N), dt) for dt in out],
        compiler_params=_params(("parallel", "parallel")),
    )(a, b, *extras, *after)
    return res[0] if n_o == 1 else res


ROW_TILE = 256
ROW_TILE_WIDE_UP_TO = 1024


def _rowwise(fn, rows, fulls, outs, reds=(), *, name, after=(), tm=ROW_TILE):
    T = rows[0].shape[0]
    widest = max([r.shape[1] for r in rows] + [c for c, _ in outs])
    if tm == ROW_TILE and widest <= ROW_TILE_WIDE_UP_TO and T % (2 * tm) == 0:
        tm *= 2
    tm = min(tm, T)
    n_r, n_f, n_o, n_d, n_a = len(rows), len(fulls), len(outs), len(reds), len(after)

    def body(*refs):
        ins = [r[...] for r in refs[:n_r + n_f]]
        o_refs = refs[n_r + n_f + n_a:n_r + n_f + n_a + n_o]
        d_refs = refs[n_r + n_f + n_a + n_o:]
        res = fn(*ins)
        for r, v in zip(o_refs, res[:n_o]):
            r[...] = v.astype(r.dtype)
        if n_d:
            @pl.when(pl.program_id(0) == 0)
            def _():
                for r in d_refs:
                    r[...] = jnp.zeros_like(r)
            for r, v in zip(d_refs, res[n_o:]):
                r[...] += v.astype(r.dtype)

    res = pl.pallas_call(
        body, name=name, grid=(T // tm,),
        in_specs=[pl.BlockSpec((tm, r.shape[1]), lambda i: (i, 0)) for r in rows]
        + [pl.BlockSpec(f.shape, lambda i, nd=f.ndim: (0,) * nd) for f in fulls] + [ANY] * n_a,
        out_specs=[pl.BlockSpec((tm, c), lambda i: (i, 0)) for c, _ in outs]
        + [pl.BlockSpec(s, lambda i, nd=len(s): (0,) * nd) for s, _ in reds],
        out_shape=[jax.ShapeDtypeStruct((T, c), dt) for c, dt in outs]
        + [jax.ShapeDtypeStruct(s, dt) for s, dt in reds],
        compiler_params=_params(("arbitrary",)),
    )(*rows, *fulls, *after)
    return res


def _rms(x, g):
    return x * lax.rsqrt(jnp.mean(x * x, axis=-1, keepdims=True) + EPS) * g


def _colsum(v):
    return jnp.sum(v, axis=0, keepdims=True)


def _softplus(x):
    return jnp.maximum(x, 0.0) + jnp.log1p(jnp.exp(-jnp.abs(x)))


def _ffn_fwd(x, norm, wgT, wuT, wd, tag, on_gate=None):
    D = x.shape[1]
    (hn,) = _rowwise(lambda xt, g: (_rms(xt, g),), [x], [norm], [(D, BF16)], name=f"{tag}_norm")
    a = _matmul(hn, wgT, "nt", name=f"{tag}_gate", out=(BF16,))
    if on_gate is not None:
        on_gate(a)
    b, hid = _matmul(hn, wuT, "nt", name=f"{tag}_up", out=(BF16, BF16),
                     epilogue=lambda acc, at: (acc, jax.nn.silu(at.astype(F32)) * acc), extras=[a])
    y = _matmul(hid, wd, "nn", name=f"{tag}_down", epilogue=lambda acc, xt: (xt + 0.5 * acc,), extras=[x])
    return y, (hn, a, b, hid)


def _ffn_bwd(dy, dy16, x, norm, wgT, wuT, wd, res, tag, after, on_grads):
    hn, a, b, hid = res
    D, F = x.shape[1], a.shape[1]

    def act_bwd(acc, at, bt):
        a_, b_ = at.astype(F32), bt.astype(F32)
        sig = jax.nn.sigmoid(a_)
        s = a_ * sig
        dh = 0.5 * acc
        return dh * b_ * (sig + s * (1.0 - sig)), dh * s

    da, db = _matmul(dy16, wd, "nt", name=f"{tag}_d_hid", out=(BF16, BF16), epilogue=act_bwd, extras=[a, b])
    d_wd = _matmul(hid, dy16, "tn", name=f"{tag}_d_wd", epilogue=lambda acc: (0.5 * acc,))
    d_wgT = _matmul(da, hn, "tn", name=f"{tag}_d_wg", after=after)
    d_wuT = _matmul(db, hn, "tn", name=f"{tag}_d_wu")
    started = on_grads(d_wgT, d_wuT, d_wd, dy)
    dhn = _matmul(da, wgT, "nn", name=f"{tag}_d_hn1")
    dhn = _matmul(db, wuT, "nn", name=f"{tag}_d_hn2", epilogue=lambda acc, e: (acc + e,), extras=[dhn])

    def norm_bwd(xt, dh, dyt, g):
        _, vjp = jax.vjp(_rms, xt, g)
        dx, dg = vjp(dh)
        return dyt + dx, dyt + dx, dg

    dx, dx16, d_norm = _rowwise(norm_bwd, [x, dhn, dy], [norm], [(D, F32), (D, BF16)], [((1, D), F32)],
                                name=f"{tag}_d_norm", after=started)
    return dx, dx16, d_norm


S5_TILES = 8
S5_HALF = 256


def _s5_derive(A_re, A_im, log_dt, B_re, B_im, C_re, C_im):
    G, N, M = S5_GROUPS, S5_STATE, S5_GROUP
    dt = jnp.exp(log_dt)[:, None]
    mag = jnp.exp(A_re * dt)
    ar, ai = mag * jnp.cos(A_im * dt), mag * jnp.sin(A_im * dt)
    den = A_re * A_re + A_im * A_im
    cr = ((ar - 1.0) * A_re + ai * A_im) / den
    ci = (ai * A_re - (ar - 1.0) * A_im) / den
    bbr = cr[..., None] * B_re - ci[..., None] * B_im
    bbi = cr[..., None] * B_im + ci[..., None] * B_re
    eye = jnp.eye(8, dtype=F32)

    def tile_in(bb):
        t = bb.reshape(S5_TILES, 8, N, M).transpose(0, 1, 3, 2)
        return jnp.einsum("jamn,ab->jambn", t, eye).reshape(S5_TILES, 8 * M, 8 * N)

    def tile_out(c):
        t = c.reshape(S5_TILES, 8, M, N).transpose(0, 1, 3, 2)
        return jnp.einsum("janm,ab->janbm", t, eye).reshape(S5_TILES, 8 * N, 8 * M)

    return (tile_in(bbr), tile_in(bbi), tile_out(C_re), tile_out(C_im),
            ar.reshape(S5_TILES, 1, 8 * N), ai.reshape(S5_TILES, 1, 8 * N))


S5_NB = S5_HALF // 128
S5_SEG = 32


def _cmul(ar, ai, br, bi):
    return ar * br - ai * bi, ar * bi + ai * br


def _s5_scan(sr_ref, si_ref, ar, ai, T, reverse):
    L, V = T // S5_SEG, S5_SEG // 8
    assert L * S5_SEG == T and L & (L - 1) == 0, T
    sg = -1.0 if reverse else 1.0
    a_r = [jnp.broadcast_to(ar[:, 128 * b:128 * (b + 1)], (8, 128)) for b in range(S5_NB)]
    a_i = [jnp.broadcast_to(sg * ai[:, 128 * b:128 * (b + 1)], (8, 128)) for b in range(S5_NB)]

    def rows(k, v):
        return pl.ds(pl.multiple_of(((L - 1 - k) if reverse else k) * S5_SEG + 8 * v, 8), 8)

    def local(k, carry):
        out = []
        for b in range(S5_NB):
            for v in range(V):
                idx = rows(k, v)
                mr, mi = _cmul(a_r[b], a_i[b], *carry[b * V + v])
                nr, ni = mr + sr_ref[b, idx, :], mi + si_ref[b, idx, :]
                sr_ref[b, idx, :] = nr
                si_ref[b, idx, :] = ni
                out.append((nr, ni))
        return tuple(out)

    z = jnp.zeros((8, 128), F32)
    ends = lax.fori_loop(0, L, local, tuple((z, z) for _ in range(S5_NB * V)))

    carries = []
    for b in range(S5_NB):
        pr, pi = a_r[b][0:1], a_i[b][0:1]
        n = L
        while n > 1:
            pr, pi = _cmul(pr, pi, pr, pi)
            n //= 2
        cr, ci = jnp.zeros((1, 128), F32), jnp.zeros((1, 128), F32)
        into = [None] * S5_SEG
        for j in (reversed(range(S5_SEG)) if reverse else range(S5_SEG)):
            into[j] = (cr, ci)
            er, ei = ends[b * V + j // 8]
            mr, mi = _cmul(pr, pi, cr, ci)
            cr, ci = mr + er[j % 8:j % 8 + 1], mi + ei[j % 8:j % 8 + 1]
        carries.append([(jnp.concatenate([into[8 * v + s][0] for s in range(8)], axis=0),
                         jnp.concatenate([into[8 * v + s][1] for s in range(8)], axis=0)) for v in range(V)])

    def fix(k, powers):
        out = []
        for b in range(S5_NB):
            pr, pi = powers[b]
            for v in range(V):
                idx = rows(k, v)
                dr, di = _cmul(pr, pi, *carries[b][v])
                sr_ref[b, idx, :] += dr
                si_ref[b, idx, :] += di
            out.append(_cmul(pr, pi, a_r[b], a_i[b]))
        return tuple(out)

    lax.fori_loop(0, L, fix, tuple((a_r[b], a_i[b]) for b in range(S5_NB)))


def _s5_in(ut, b_ref, s_ref):
    for b in range(S5_NB):
        s_ref[b] = _dot(ut, b_ref[:, 128 * b:128 * (b + 1)], NN)


def _to_seg(v):
    T, C = v.shape
    return v.reshape(S5_SEG, T // S5_SEG, C).transpose(1, 0, 2).reshape(T, C)


def _from_seg(v):
    T, C = v.shape
    return v.reshape(T // S5_SEG, S5_SEG, C).transpose(1, 0, 2).reshape(T, C)


def _s5_specs(T):
    u_spec = pl.BlockSpec((T, 128), lambda j, h: (0, j))
    b_spec = pl.BlockSpec((None, 128, S5_HALF), lambda j, h: (j, 0, h))
    c_spec = pl.BlockSpec((None, S5_HALF, 128), lambda j, h: (j, h, 0))
    a_spec = pl.BlockSpec((None, 1, S5_HALF), lambda j, h: (j, 0, h))
    return u_spec, b_spec, c_spec, a_spec


def _s5_fwd(u, tiles):
    T = u.shape[0]
    u_spec, b_spec, c_spec, a_spec = _s5_specs(T)

    def body(u_ref, br_ref, bi_ref, cr_ref, ci_ref, ar_ref, ai_ref, y_ref, sr_ref, si_ref):
        ut = u_ref[...].astype(MXU)
        _s5_in(ut, br_ref, sr_ref)
        _s5_in(ut, bi_ref, si_ref)
        _s5_scan(sr_ref, si_ref, ar_ref[...], ai_ref[...], T, False)
        y = None
        for b in range(S5_NB):
            blk = slice(128 * b, 128 * (b + 1))
            yb = _dot(sr_ref[b], cr_ref[blk, :], NN) - _dot(si_ref[b], ci_ref[blk, :], NN)
            y = yb if y is None else y + yb

        @pl.when(pl.program_id(1) == 0)
        def _():
            y_ref[...] = y

        @pl.when(pl.program_id(1) == 1)
        def _():
            y_ref[...] += y

    scr = pltpu.VMEM((S5_NB, T, 128), F32)
    return pl.pallas_call(
        body, name="s5_fwd", grid=(S5_TILES, 2),
        in_specs=[u_spec, b_spec, b_spec, c_spec, c_spec, a_spec, a_spec],
        out_specs=u_spec, out_shape=jax.ShapeDtypeStruct(u.shape, F32),
        scratch_shapes=[scr, scr],
        compiler_params=_params(("parallel", "arbitrary")),
    )(u, *tiles)


def _s5_bwd(u, dys, du_skip, tiles):
    T = u.shape[0]
    u_spec, b_spec, c_spec, a_spec = _s5_specs(T)

    def body(u_ref, dy_ref, sk_ref, br_ref, bi_ref, cr_ref, ci_ref, ar_ref, ai_ref,
             du_ref, dbr_ref, dbi_ref, dcr_ref, dci_ref, dar_ref, dai_ref, sr_ref, si_ref, lr_ref, li_ref):
        ut, dy = u_ref[...].astype(MXU), dy_ref[...].astype(MXU)
        ar, ai = ar_ref[...], ai_ref[...]
        _s5_in(ut, br_ref, sr_ref)
        _s5_in(ut, bi_ref, si_ref)
        _s5_scan(sr_ref, si_ref, ar, ai, T, False)
        for b in range(S5_NB):
            blk = slice(128 * b, 128 * (b + 1))
            dcr_ref[blk, :] = _dot(sr_ref[b], dy, TN)
            dci_ref[blk, :] = -_dot(si_ref[b], dy, TN)
            lr_ref[b] = _dot(dy, cr_ref[blk, :], NT)
            li_ref[b] = -_dot(dy, ci_ref[blk, :], NT)
        _s5_scan(lr_ref, li_ref, ar, ai, T, True)
        du = None
        head = lax.broadcasted_iota(jnp.int32, (T, 128), 0) < S5_SEG
        seg0 = lax.broadcasted_iota(jnp.int32, (S5_SEG, 128), 0) == 0
        for b in range(S5_NB):
            blk = slice(128 * b, 128 * (b + 1))
            lr, li = lr_ref[b], li_ref[b]
            pr = jnp.where(head, 0.0, pltpu.roll(sr_ref[b], S5_SEG, 0))
            pi = jnp.where(head, 0.0, pltpu.roll(si_ref[b], S5_SEG, 0))
            er = jnp.where(seg0, 0.0, pltpu.roll(sr_ref[b, T - S5_SEG:T, :], 1, 0))
            ei = jnp.where(seg0, 0.0, pltpu.roll(si_ref[b, T - S5_SEG:T, :], 1, 0))
            hr, hi = lr[0:S5_SEG], li[0:S5_SEG]
            dar_ref[:, blk] = _colsum(lr * pr + li * pi) + _colsum(hr * er + hi * ei)
            dai_ref[:, blk] = _colsum(li * pr - lr * pi) + _colsum(hi * er - hr * ei)
            dbr_ref[:, blk] = _dot(ut, lr, TN)
            dbi_ref[:, blk] = _dot(ut, li, TN)
            dub = _dot(lr, br_ref[:, blk], NT) + _dot(li, bi_ref[:, blk], NT)
            du = dub if du is None else du + dub

        @pl.when(pl.program_id(1) == 0)
        def _():
            du_ref[...] = du + sk_ref[...]

        @pl.when(pl.program_id(1) == 1)
        def _():
            du_ref[...] += du

    scr = pltpu.VMEM((S5_NB, T, 128), F32)
    return pl.pallas_call(
        body, name="s5_bwd", grid=(S5_TILES, 2),
        in_specs=[u_spec, u_spec, u_spec, b_spec, b_spec, c_spec, c_spec, a_spec, a_spec],
        out_specs=[u_spec, b_spec, b_spec, c_spec, c_spec, a_spec, a_spec],
        out_shape=[jax.ShapeDtypeStruct(u.shape, F32)] + [jax.ShapeDtypeStruct(t.shape, F32) for t in tiles],
        scratch_shapes=[scr, scr, scr, scr],
        compiler_params=_params(("parallel", "arbitrary")),
    )(u, dys, du_skip, *tiles)


CONV_TILE = 256


def _conv_pre(x, w, b):
    T = x.shape[0]
    row = lax.broadcasted_iota(jnp.int32, x.shape, 0)
    acc = x * w[CONV_K - 1:CONV_K, :] + b
    for lag in range(1, CONV_K):
        acc = acc + jnp.where(row >= lag, pltpu.roll(x, lag, 0), 0.0) * w[CONV_K - 1 - lag:CONV_K - lag, :]
    return acc


def _conv_fwd(x, w, b):
    T, C = x.shape
    col = pl.BlockSpec((T, CONV_TILE), lambda j: (0, j))

    def body(x_ref, w_ref, b_ref, o_ref):
        o_ref[...] = jax.nn.silu(_conv_pre(x_ref[...], w_ref[...], b_ref[...]))

    return pl.pallas_call(
        body, name="conv_fwd", grid=(C // CONV_TILE,),
        in_specs=[col, pl.BlockSpec((CONV_K, CONV_TILE), lambda j: (0, j)), pl.BlockSpec((1, CONV_TILE), lambda j: (0, j))],
        out_specs=col, out_shape=jax.ShapeDtypeStruct((T, C), F32),
        compiler_params=_params(("parallel",)),
    )(x, w, b)


def _conv_bwd(x, w, b, dout, col0, name):
    T, C = dout.shape
    off = col0 // CONV_TILE
    xcol = pl.BlockSpec((T, CONV_TILE), lambda j: (0, j + off))
    dcol = pl.BlockSpec((T, CONV_TILE), lambda j: (0, j))

    def body(x_ref, w_ref, b_ref, d_ref, dx_ref, dw_ref, db_ref):
        x, w = x_ref[...], w_ref[...]
        _, vjp = jax.vjp(jax.nn.silu, _conv_pre(x, w, b_ref[...]))
        (dy,) = vjp(d_ref[...])
        row = lax.broadcasted_iota(jnp.int32, x.shape, 0)
        dx = dy * w[CONV_K - 1:CONV_K, :]
        dw_ref[CONV_K - 1:CONV_K, :] = _colsum(dy * x)
        for lag in range(1, CONV_K):
            ahead = jnp.where(row < T - lag, pltpu.roll(dy, T - lag, 0), 0.0)
            dx = dx + ahead * w[CONV_K - 1 - lag:CONV_K - lag, :]
            dw_ref[CONV_K - 1 - lag:CONV_K - lag, :] = _colsum(ahead * x)
        dx_ref[...] = dx.astype(dx_ref.dtype)
        db_ref[...] = _colsum(dy)

    return pl.pallas_call(
        body, name=name, grid=(C // CONV_TILE,),
        in_specs=[xcol, pl.BlockSpec((CONV_K, CONV_TILE), lambda j: (0, j + off)),
                  pl.BlockSpec((1, CONV_TILE), lambda j: (0, j + off)), dcol],
        out_specs=[dcol, pl.BlockSpec((CONV_K, CONV_TILE), lambda j: (0, j)), pl.BlockSpec((1, CONV_TILE), lambda j: (0, j))],
        out_shape=[jax.ShapeDtypeStruct((T, C), BF16), jax.ShapeDtypeStruct((CONV_K, C), F32), jax.ShapeDtypeStruct((1, C), F32)],
        compiler_params=_params(("parallel",)),
    )(x, w, b, dout)


def _ssd_common(x_ref, b_ref, c_ref, dtr_ref, bias_ref, alog_ref, tri_ref, triu_ref):
    Q = x_ref.shape[0]
    x, Bm, Cm = x_ref[...], b_ref[...], c_ref[...]
    pre = dtr_ref[...] + bias_ref[...]
    dt = _softplus(pre)
    A = -jnp.exp(alog_ref[...])
    adt = dt * A
    ac4 = _dot(tri_ref[...], adt, NN, HI)
    ar4 = _dot(adt, triu_ref[...], TN, HI)
    cb = _dot(Cm, Bm, NT)
    ii = lax.broadcasted_iota(jnp.int32, (Q, Q), 0)
    jj = lax.broadcasted_iota(jnp.int32, (Q, Q), 1)
    atot4 = ac4[Q - 1:Q, :]
    return x, Bm, Cm, pre, dt, A, ac4, ar4, cb, ii >= jj, atot4


def _ssd_decay(k, ac4, ar4, causal):
    seg = ac4[:, k:k + 1] - ar4[k:k + 1, :]
    return jnp.where(causal, jnp.exp(jnp.where(causal, seg, 0.0)), 0.0)


def _per_head(c4, n):
    lane = lax.broadcasted_iota(jnp.int32, (c4.shape[0], n), 1)
    out = jnp.broadcast_to(c4[:, 0:1], (c4.shape[0], n))
    for k in range(1, SSD_HPG):
        out = jnp.where(lane >= SSD_HEADDIM * k, c4[:, k:k + 1], out)
    return out


def _head_sums(v):
    n = SSD_HPG * SSD_HEADDIM
    row = lax.broadcasted_iota(jnp.int32, (n, SSD_HPG), 0)
    col = lax.broadcasted_iota(jnp.int32, (n, SSD_HPG), 1)
    member = jnp.where((row >= SSD_HEADDIM * col) & (row < SSD_HEADDIM * (col + 1)), 1.0, 0.0)
    return _dot(v, member, NN, HI)


def _head_rows(c4):
    n = SSD_HPG * SSD_HEADDIM
    row = lax.broadcasted_iota(jnp.int32, (n, 1), 0)
    out = jnp.broadcast_to(c4[:, 0:1], (n, 1))
    for k in range(1, SSD_HPG):
        out = jnp.where(row >= SSD_HEADDIM * k, c4[:, k:k + 1], out)
    return out


def _ssd_specs(T, Q, rev):
    NC = T // Q
    cc = (lambda c: NC - 1 - c) if rev else (lambda c: c)
    W, N, S = SSD_HPG * SSD_HEADDIM, SSD_STATE, SSD_STEP_GROUPS
    x_spec = pl.BlockSpec((Q, S * W), lambda g, c: (cc(c), g))
    b_spec = pl.BlockSpec((Q, S * N), lambda g, c: (cc(c), D_INNER // (S * N) + g))
    c_spec = pl.BlockSpec((Q, S * N), lambda g, c: (cc(c), (D_INNER + SSD_GROUPS * N) // (S * N) + g))
    dt_spec = pl.BlockSpec((S, Q, SSD_HPG), lambda g, c: (g, cc(c), 0))
    p_spec = pl.BlockSpec((S, 1, SSD_HPG), lambda g, c: (g, 0, 0))
    tri_spec = pl.BlockSpec((Q, Q), lambda g, c: (0, 0))
    h_spec = pl.BlockSpec((S, None, W, N), lambda g, c: (g, cc(c), 0, 0))
    return x_spec, b_spec, c_spec, dt_spec, p_spec, tri_spec, h_spec


def _ssd_views(gi, wide, narrow, lead):
    W, N = SSD_HPG * SSD_HEADDIM, SSD_STATE
    return ([r.at[:, pl.ds(W * gi, W)] for r in wide] + [r.at[:, pl.ds(N * gi, N)] for r in narrow]
            + [r.at[gi] for r in lead])


def _tri(Q):
    tri = jnp.tril(jnp.ones((Q, Q), F32))
    return tri, tri.T


def _ssd_fwd(xc, dtr8, bias8, alog8, dsk8):
    T, Q = xc.shape[0], SSD_Q
    NC = T // Q
    P, W = SSD_HEADDIM, SSD_HPG * SSD_HEADDIM
    x_spec, b_spec, c_spec, dt_spec, p_spec, tri_spec, h_spec = _ssd_specs(T, Q, False)

    def body(x_ref, b_ref, c_ref, dtr_ref, bias_ref, alog_ref, dsk_ref, tri_ref, triu_ref, y_ref, hs_ref, h_scr):
        for gi in range(SSD_STEP_GROUPS):
            xv, yv, bv, cv, dtv, biasv, alogv, dskv, hsv, hv = _ssd_views(
                gi, (x_ref, y_ref), (b_ref, c_ref), (dtr_ref, bias_ref, alog_ref, dsk_ref, hs_ref, h_scr))
            one(xv, bv, cv, dtv, biasv, alogv, dskv, tri_ref, triu_ref, yv, hsv, hv)

    def one(x_ref, b_ref, c_ref, dtr_ref, bias_ref, alog_ref, dsk_ref, tri_ref, triu_ref, y_ref, hs_ref, h_scr):
        @pl.when(pl.program_id(1) == 0)
        def _():
            h_scr[...] = jnp.zeros_like(h_scr)

        hin = h_scr[...]
        hs_ref[...] = hin
        x, Bm, Cm, pre, dt, A, ac4, ar4, cb, causal, atot4 = _ssd_common(
            x_ref, b_ref, c_ref, dtr_ref, bias_ref, alog_ref, tri_ref, triu_ref)
        xdt = x * _per_head(dt, W)
        rest = _dot(Cm, hin, NT) * _per_head(jnp.exp(ac4), W) + _per_head(dsk_ref[...], W) * x
        for k in range(SSD_HPG):
            blk = slice(P * k, P * (k + 1))
            G = cb * _ssd_decay(k, ac4, ar4, causal)
            y_ref[:, blk] = _dot(G, xdt[:, blk], NN) + rest[:, blk]
        xw = xdt * _per_head(jnp.exp(atot4 - ac4), W)
        h_scr[...] = _head_rows(jnp.exp(atot4)) * hin + _dot(xw, Bm, TN)

    tri, triu = _tri(Q)
    return pl.pallas_call(
        body, name="ssd_fwd", grid=(SSD_GROUPS // SSD_STEP_GROUPS, NC),
        in_specs=[x_spec, b_spec, c_spec, dt_spec, p_spec, p_spec, p_spec, tri_spec, tri_spec],
        out_specs=[pl.BlockSpec((Q, SSD_STEP_GROUPS * W), lambda g, c: (c, g)), h_spec],
        out_shape=[jax.ShapeDtypeStruct((T, D_INNER), F32),
                   jax.ShapeDtypeStruct((SSD_GROUPS, NC, W, SSD_STATE), F32)],
        scratch_shapes=[pltpu.VMEM((SSD_STEP_GROUPS, W, SSD_STATE), F32)],
        compiler_params=_params(("parallel", "arbitrary")),
    )(xc, xc, xc, dtr8, bias8, alog8, dsk8, tri, triu)


def _ssd_bwd(xc, dtr8, bias8, alog8, dsk8, hs, dy):
    T, Q = xc.shape[0], SSD_Q
    NC = T // Q
    P, W = SSD_HEADDIM, SSD_HPG * SSD_HEADDIM
    x_spec, b_spec, c_spec, dt_spec, p_spec, tri_spec, h_spec = _ssd_specs(T, Q, True)
    dy_spec = pl.BlockSpec((Q, SSD_STEP_GROUPS * W), lambda g, c: (NC - 1 - c, g))
    dbc_spec = pl.BlockSpec((Q, SSD_STEP_GROUPS * SSD_STATE), lambda g, c: (NC - 1 - c, g))

    def body(x_ref, b_ref, c_ref, dtr_ref, bias_ref, alog_ref, dsk_ref, tri_ref, triu_ref, hs_ref, dy_ref,
             dx_ref, db_ref, dc_ref, ddtr_ref, dbias_ref, dalog_ref, ddsk_ref, dh_scr):
        for gi in range(SSD_STEP_GROUPS):
            (xv, dyv, dxv, bv, cv, dbv, dcv, dtv, biasv, alogv, dskv, hsv, ddtv, dbiasv, dalogv, ddskv, dhv) = _ssd_views(
                gi, (x_ref, dy_ref, dx_ref), (b_ref, c_ref, db_ref, dc_ref),
                (dtr_ref, bias_ref, alog_ref, dsk_ref, hs_ref, ddtr_ref, dbias_ref, dalog_ref, ddsk_ref, dh_scr))
            one(xv, bv, cv, dtv, biasv, alogv, dskv, tri_ref, triu_ref, hsv, dyv,
                dxv, dbv, dcv, ddtv, dbiasv, dalogv, ddskv, dhv)

    def one(x_ref, b_ref, c_ref, dtr_ref, bias_ref, alog_ref, dsk_ref, tri_ref, triu_ref, hs_ref, dy_ref,
            dx_ref, db_ref, dc_ref, ddtr_ref, dbias_ref, dalog_ref, ddsk_ref, dh_scr):
        @pl.when(pl.program_id(1) == 0)
        def _():
            dh_scr[...] = jnp.zeros_like(dh_scr)
            dbias_ref[...] = jnp.zeros_like(dbias_ref)
            dalog_ref[...] = jnp.zeros_like(dalog_ref)
            ddsk_ref[...] = jnp.zeros_like(ddsk_ref)

        x, Bm, Cm, pre, dt, A, ac4, ar4, cb, causal, atot4 = _ssd_common(
            x_ref, b_ref, c_ref, dtr_ref, bias_ref, alog_ref, tri_ref, triu_ref)
        dyv, hin, dho = dy_ref[...], hs_ref[...], dh_scr[...]
        dt_w = _per_head(dt, W)
        xdt = x * dt_w
        E4, F4, etot4 = jnp.exp(ac4), jnp.exp(atot4 - ac4), jnp.exp(atot4)
        F_w = _per_head(F4, W)
        ddsk_ref[...] += _head_sums(_colsum(dyv * x))
        Z = _dot(Cm, hin, NT)
        dZ = dyv * _per_head(E4, W)
        dac4 = _head_sums(dyv * Z) * E4
        dC = _dot(dZ, hin, NN)
        dh_scr[...] = _dot(dZ, Cm, TN) + _head_rows(etot4) * dho
        per_row = jnp.sum(dho * hin, axis=1, keepdims=True)
        lane4 = lax.broadcasted_iota(jnp.int32, (1, SSD_HPG), 1)
        datot4 = jnp.zeros((1, SSD_HPG), F32)
        for k in range(SSD_HPG):
            datot4 = jnp.where(lane4 == k, jnp.sum(per_row[P * k:P * (k + 1)], keepdims=True), datot4)
        datot4 = datot4 * etot4
        dxw = _dot(Bm, dho, NT)
        dB = _dot(xdt * F_w, dho, NN)
        dFa4 = _head_sums(dxw * xdt) * F4
        datot4 = datot4 + _colsum(dFa4)
        dac4 = dac4 - dFa4
        dcb = jnp.zeros((Q, Q), F32)
        lane_q4 = lax.broadcasted_iota(jnp.int32, (Q, SSD_HPG), 1)
        sub_4q = lax.broadcasted_iota(jnp.int32, (SSD_HPG, Q), 0)
        row_sums = jnp.zeros((Q, SSD_HPG), F32)
        col_sums = jnp.zeros((SSD_HPG, Q), F32)
        dxdt_heads = []
        for k in range(SSD_HPG):
            blk = slice(P * k, P * (k + 1))
            L = _ssd_decay(k, ac4, ar4, causal)
            G = cb * L
            dG = _dot(dyv[:, blk], xdt[:, blk], NT)
            dxdt_heads.append(_dot(G, dyv[:, blk], TN))
            dcb = dcb + dG * L
            Mseg = dG * G
            row_sums = jnp.where(lane_q4 == k, jnp.sum(Mseg, axis=1, keepdims=True), row_sums)
            col_sums = jnp.where(sub_4q == k, jnp.sum(Mseg, axis=0, keepdims=True), col_sums)
        dxdt = dxw * F_w + jnp.concatenate(dxdt_heads, axis=1)
        last = lax.broadcasted_iota(jnp.int32, (Q, SSD_HPG), 0) == Q - 1
        dac4 = dac4 + row_sums + jnp.where(last, datot4, 0.0)
        dadt4 = _dot(triu_ref[...], dac4, NN, HI) - _dot(triu_ref[...], col_sums, NT, HI)
        ddt4 = _head_sums(dxdt * x) + dadt4 * A
        dalog_ref[...] += _colsum(dadt4 * dt) * A
        ddtr4 = ddt4 * jax.nn.sigmoid(pre)
        dbias_ref[...] += _colsum(ddtr4)
        ddtr_ref[...] = ddtr4
        dx_ref[...] = (_per_head(dsk_ref[...], W) * dyv + dxdt * dt_w).astype(dx_ref.dtype)
        db_ref[...] = dB + _dot(dcb, Cm, TN)
        dc_ref[...] = dC + _dot(dcb, Bm, NN)

    tri, triu = _tri(Q)
    return pl.pallas_call(
        body, name="ssd_bwd", grid=(SSD_GROUPS // SSD_STEP_GROUPS, NC),
        in_specs=[x_spec, b_spec, c_spec, dt_spec, p_spec, p_spec, p_spec, tri_spec, tri_spec, h_spec, dy_spec],
        out_specs=[dy_spec, dbc_spec, dbc_spec, dt_spec, p_spec, p_spec, p_spec],
        out_shape=[jax.ShapeDtypeStruct((T, D_INNER), F32),
                   jax.ShapeDtypeStruct((T, SSD_GROUPS * SSD_STATE), F32),
                   jax.ShapeDtypeStruct((T, SSD_GROUPS * SSD_STATE), F32),
                   jax.ShapeDtypeStruct(dtr8.shape, F32)] + [jax.ShapeDtypeStruct(bias8.shape, F32)] * 3,
        scratch_shapes=[pltpu.VMEM((SSD_STEP_GROUPS, W, SSD_STATE), F32)],
        compiler_params=_params(("parallel", "arbitrary")),
    )(xc, xc, xc, dtr8, bias8, alog8, dsk8, tri, triu, hs, dy)


def _to_groups(v):
    return v.reshape(SSD_GROUPS, 1, SSD_HPG)


def _mixer_fwd(x, p, w):
    T, D = x.shape
    (hn,) = _rowwise(lambda xt, g: (_rms(xt, g),), [x], [p["mix_norm"]], [(D, BF16)], name="mix_norm")
    winT = w["w_in"]
    u, z, xbc, dtr, gl = [
        _matmul(hn, winT[IN_OFFS[i]:IN_OFFS[i + 1]], "nt", name=f"mix_in{i}") for i in range(5)]
    tiles = _s5_derive(*[p[k][0] for k in ("s5_A_re", "s5_A_im", "s5_log_dt", "s5_B_re", "s5_B_im", "s5_C_re", "s5_C_im")])
    u = _to_seg(u)
    ys = _s5_fwd(u, tiles)
    (g,) = _rowwise(lambda yt, ut, d: (jax.nn.gelu(yt + d * ut),), [ys, u], [p["s5_D"]], [(D, F32)], name="s5_gelu")
    y5 = _matmul(g, w["s5_w_glu"], "nn", name="s5_glu", out=(BF16,),
                 epilogue=lambda acc, gt, b: (gt * jax.nn.sigmoid(acc + b),), extras=[g, p["s5_b_glu"]])
    xc = _conv_fwd(xbc, w["conv_w"], p["conv_b"])
    dtr8 = dtr.reshape(T, SSD_GROUPS, SSD_HPG).transpose(1, 0, 2)
    ssd_p = [_to_groups(p[k]) for k in ("ssd_dt_bias", "ssd_A_log", "ssd_D")]
    yssd_raw, hs = _ssd_fwd(xc, dtr8, *ssd_p)
    (yssd,) = _rowwise(lambda yt, zt, nw: (_rms(yt * jax.nn.silu(zt), nw),), [yssd_raw, z], [p["ssd_norm"]],
                       [(D_INNER, BF16)], name="ssd_gate")
    p5 = _from_seg(_matmul(y5, w["w_proj_s5"], "nn", name="mix_p5"))
    pssd = _matmul(yssd, w["w_proj_ssd"], "nn", name="mix_pssd")

    def merge(glt, at, bt, bg):
        gates = jax.nn.sigmoid(glt + bg)
        return (gates[:, :D] * at + gates[:, D:] * bt,)

    (merged,) = _rowwise(merge, [gl, p5, pssd], [p["b_gate"]], [(D, BF16)], name="mix_merge")
    y = _matmul(merged, w["w_out"], "nn", name="mix_out", epilogue=lambda acc, xt: (xt + acc,), extras=[x])
    res = dict(hn=hn, u=u, z=z, xbc=xbc, gl=gl, tiles=tiles, ys=ys, g=g, y5=y5, xc=xc, dtr8=dtr8, ssd_p=ssd_p,
               yssd_raw=yssd_raw, hs=hs, yssd=yssd, p5=p5, pssd=pssd, merged=merged)
    return y, res


def _mixer_bwd(dy, dy16, x, p, w, r):
    T, D = x.shape
    gp = {}
    dmerged = _matmul(dy16, w["w_out"], "nt", name="mix_d_merged")
    gp["w_out"] = _matmul(r["merged"], dy16, "tn", name="mix_d_wout")

    def merge_bwd(glt, at, bt, dm, bg):
        def f(q, a_, b_):
            gates = jax.nn.sigmoid(q + bg)
            return gates[:, :D] * a_ + gates[:, D:] * b_
        _, vjp = jax.vjp(f, glt, at, bt)
        dq, da_, db_ = vjp(dm)
        return da_, db_, dq, _colsum(dq)

    dp5, dpssd, dgl, gp["b_gate"] = _rowwise(
        merge_bwd, [r["gl"], r["p5"], r["pssd"], dmerged], [p["b_gate"]],
        [(D, BF16), (D, BF16), (2 * D, BF16)], [((1, 2 * D), F32)], name="mix_d_merge")
    dyssd = _matmul(dpssd, w["w_proj_ssd"], "nt", name="mix_d_yssd")
    gp["w_proj_ssd"] = _matmul(r["yssd"], dpssd, "tn", name="mix_d_wpssd")

    def gate_bwd(yt, zt, dyt, nw):
        _, vjp = jax.vjp(lambda a_, b_, c_: _rms(a_ * jax.nn.silu(b_), c_), yt, zt, nw)
        return vjp(dyt)

    dyraw, dz, gp["ssd_norm"] = _rowwise(
        gate_bwd, [r["yssd_raw"], r["z"], dyssd], [p["ssd_norm"]],
        [(D_INNER, F32), (D_INNER, BF16)], [((1, D_INNER), F32)], name="ssd_d_gate")
    dxs, dBm, dCm, ddtr8, dbias8, dalog8, ddsk8 = _ssd_bwd(r["xc"], r["dtr8"], *r["ssd_p"], r["hs"], dyraw)
    gp["ssd_dt_bias"], gp["ssd_A_log"], gp["ssd_D"] = [v.reshape(1, SSD_HEADS) for v in (dbias8, dalog8, ddsk8)]
    ddtr = ddtr8.transpose(1, 0, 2).reshape(T, SSD_HEADS)
    conv = [_conv_bwd(r["xbc"], w["conv_w"], p["conv_b"], d, c0, f"conv_bwd{i}")
            for i, (d, c0) in enumerate(((dxs, 0), (dBm, D_INNER), (dCm, D_INNER + SSD_GROUPS * SSD_STATE)))]
    dxbc = [c[0] for c in conv]
    gp["conv_w"] = jnp.concatenate([c[1] for c in conv], axis=1)
    gp["conv_b"] = jnp.concatenate([c[2] for c in conv], axis=1)

    dp5 = _to_seg(dp5)
    dy5 = _matmul(dp5, w["w_proj_s5"], "nt", name="mix_d_y5")
    gp["w_proj_s5"] = _matmul(r["y5"], dp5, "tn", name="mix_d_wp5")
    g = r["g"]

    def glu_ep(acc, gt, b, dyt):
        _, vjp = jax.vjp(lambda g_, t_: g_ * jax.nn.sigmoid(t_ + b), gt, acc)
        return vjp(dyt)

    dg1, dt_ = _matmul(g, w["s5_w_glu"], "nn", name="s5_d_glu", out=(F32, BF16), epilogue=glu_ep,
                       extras=[g, p["s5_b_glu"], dy5])
    gp["s5_w_glu"] = _matmul(g, dt_, "tn", name="s5_d_wglu")
    dg = _matmul(dt_, w["s5_w_glu"], "nt", name="s5_d_g", epilogue=lambda acc, e: (acc + e,), extras=[dg1])

    def gelu_bwd(yt, ut, dgt, dtt, d):
        _, vjp = jax.vjp(lambda y_, d_: jax.nn.gelu(y_ + d_ * ut), yt, d)
        dys, dd = vjp(dgt)
        return dys, dys * d, dd, _colsum(dtt.astype(F32))

    dys, dusk, gp["s5_D"], gp["s5_b_glu"] = _rowwise(
        gelu_bwd, [r["ys"], r["u"], dg, dt_], [p["s5_D"]], [(D, F32), (D, F32)], [((1, D), F32), ((1, D), F32)],
        name="s5_d_gelu")
    du, *dtiles = _s5_bwd(r["u"], dys, dusk, r["tiles"])
    du = _from_seg(du)

    winT = w["w_in"]
    hn = r["hn"]
    d_rows, dhn = [], None
    cols = [(du, 0, 1024), (dz, 1024, 3072), (dxbc[0], 3072, 5120), (dxbc[1], 5120, 6144), (dxbc[2], 6144, 7168),
            (ddtr, 7168, 7200), (dgl, 7200, 9248)]
    for i, (d, lo, hi) in enumerate(cols):
        d_rows.append(_matmul(d, hn, "tn", name=f"mix_d_win{i}"))
        if dhn is None:
            dhn = _matmul(d, winT[lo:hi], "nn", name=f"mix_d_hn{i}")
        else:
            dhn = _matmul(d, winT[lo:hi], "nn", name=f"mix_d_hn{i}", epilogue=lambda acc, e: (acc + e,), extras=[dhn])
    gp["w_in"] = jnp.concatenate(d_rows, axis=0)

    def norm_bwd(xt, dh, dyt, gn):
        _, vjp = jax.vjp(_rms, xt, gn)
        dx, dgn = vjp(dh)
        return dyt + dx, dyt + dx, dgn

    dx, dx16, gp["mix_norm"] = _rowwise(norm_bwd, [x, dhn, dy], [p["mix_norm"]], [(D, F32), (D, BF16)],
                                        [((1, D), F32)], name="mix_d_norm")
    return dx, dx16, gp, dtiles


def _s5_param_grads(p, dtiles):
    keys = ("s5_A_re", "s5_A_im", "s5_log_dt", "s5_B_re", "s5_B_im", "s5_C_re", "s5_C_im")
    _, vjp = jax.vjp(_s5_derive, *[p[k][0] for k in keys])
    return {k: v[None] for k, v in zip(keys, vjp(tuple(dtiles)))}


def _local_step(x, target, p, w, late_weights, exchange):
    T, D = x.shape
    w = dict(w)
    x1, r1 = _ffn_fwd(x, p["ffn1_norm"], w["ffn1_w_gate"], w["ffn1_w_up"], w["ffn1_w_down"], "ffn1",
                      on_gate=lambda a_: w.update(late_weights("mixer", a_)))
    x2, rm = _mixer_fwd(x1, p, w)
    w.update(late_weights("ffn2", rm["p5"]))
    x3, r2 = _ffn_fwd(x2, p["ffn2_norm"], w["ffn2_w_gate"], w["ffn2_w_up"], w["ffn2_w_down"], "ffn2")

    def head(xt, tt, g):
        def f(x_, g_):
            e = _rms(x_, g_) - tt
            return 0.5 * jnp.sum(jnp.mean(e * e, axis=-1))
        l, (dx_, dg_) = jax.value_and_grad(f, argnums=(0, 1))(xt, g)
        return dx_, dx_, l.reshape(1, 1), dg_

    dx3, dx3_16, loss, d_final = _rowwise(head, [x3, target], [p["final_norm"]], [(D, F32), (D, BF16)],
                                          [((1, 1), F32), ((1, D), F32)], name="loss_head")
    gp = {"final_norm": d_final}

    def ffn_exchange(tag):
        return lambda d_wgT, d_wuT, d_wd, after: exchange(
            tag, {f"{tag}_w_gate": d_wgT, f"{tag}_w_up": d_wuT, f"{tag}_w_down": d_wd}, after)

    dx2, dx2_16, gp["ffn2_norm"] = _ffn_bwd(
        dx3, dx3_16, x2, p["ffn2_norm"], w["ffn2_w_gate"], w["ffn2_w_up"], w["ffn2_w_down"], r2, "ffn2", (),
        ffn_exchange("ffn2"))
    dx1, dx1_16, gm, dtiles = _mixer_bwd(dx2, dx2_16, x1, p, w, rm)
    conv_w_grad = gm.pop("conv_w")
    started = exchange("mixer", {k: gm.pop(k) for k in ("w_out", "w_proj_s5", "w_proj_ssd", "s5_w_glu", "w_in")}, dx1)
    gp.update(gm)
    gp.update(_s5_param_grads(p, dtiles))
    dx0, _, gp["ffn1_norm"] = _ffn_bwd(
        dx1, dx1_16, x, p["ffn1_norm"], w["ffn1_w_gate"], w["ffn1_w_up"], w["ffn1_w_down"], r1, "ffn1", started,
        ffn_exchange("ffn1"))
    return loss, dx0, gp, conv_w_grad


ADAMW_BLOCK = 1 << 19


def _adamw(w, m, v, gs, *, name, emit_g):
    _, R, C = w.shape
    n_g = len(gs)
    n_o = 4 if emit_g else 3
    tm = _rtile(R, max(8, ADAMW_BLOCK // C))
    lead = pl.BlockSpec((None, tm, C), lambda i: (0, i, 0))

    def body(w_ref, m_ref, v_ref, *refs):
        g = refs[0][...]
        for r in refs[1:n_g]:
            g = g + r[...]
        m2 = ADAM_B1 * m_ref[...] + (1.0 - ADAM_B1) * g
        v2 = ADAM_B2 * v_ref[...] + (1.0 - ADAM_B2) * (g * g)
        m_hat = m2 / (1.0 - ADAM_B1 ** ADAM_STEP)
        v_hat = v2 / (1.0 - ADAM_B2 ** ADAM_STEP)
        delta = -ADAM_LR * (m_hat / (jnp.sqrt(v_hat) + ADAM_EPS) + ADAM_WD * w_ref[...])
        for r, val in zip(refs[n_g:], ((g,) if emit_g else ()) + (delta, m2, v2)):
            r[...] = val

    return pl.pallas_call(
        body, name=name, grid=(R // tm,),
        in_specs=[lead] * 3 + [pl.BlockSpec((tm, C), lambda i: (i, 0))] * n_g, out_specs=[lead] * n_o,
        out_shape=[jax.ShapeDtypeStruct((1, R, C), F32)] * n_o,
        compiler_params=_params(("parallel",)),
    )(w, m, v, *gs)


def _rtile(r, cap=512):
    if r <= cap:
        return r
    best = None
    for t in range(8, cap + 1, 8):
        if r % t == 0:
            best = t
    assert best is not None, r
    return best


ANY = pl.BlockSpec(memory_space=pl.ANY)


def _where_am_i():
    x, y, c = lax.axis_index("x"), lax.axis_index("y"), lax.axis_index("c")
    ks = (2 * x + y, 2 * (1 - x) + y, 2 * x + (1 - y), 2 * (1 - x) + (1 - y))
    return x, y, c, ks


def _rc(src, dst, ssem, rsem, to):
    return pltpu.make_async_remote_copy(src_ref=src, dst_ref=dst, send_sem=ssem, recv_sem=rsem,
                                        device_id=to, device_id_type=MESH)


def _gather_shards(srcs):
    n = len(srcs)

    def body(*refs):
        _gather_body(refs[:n], refs[n:2 * n], *refs[2 * n:], extra=())

    return pl.pallas_call(
        body, name="gather_shards",
        in_specs=[ANY] * n, out_specs=[ANY] * n,
        out_shape=[jax.ShapeDtypeStruct((4,) + s.shape, s.dtype) for s in srcs],
        scratch_shapes=[pltpu.SemaphoreType.DMA((n, 6)), pltpu.SemaphoreType.DMA((n, 6)), pltpu.SemaphoreType.DMA((n,))],
        compiler_params=pltpu.CompilerParams(has_side_effects=True),
    )(*srcs)


def _handshake_all():
    x, y, c = lax.axis_index("x"), lax.axis_index("y"), lax.axis_index("c")
    barrier = pltpu.get_barrier_semaphore()
    for dx, dy, dc in itertools.product((0, 1), repeat=3):
        if (dx, dy, dc) != (0, 0, 0):
            to = (1 - x if dx else x, 1 - y if dy else y, 1 - c if dc else c)
            pl.semaphore_signal(barrier, inc=1, device_id=to, device_id_type=MESH)
    pl.semaphore_wait(barrier, 7)


SEQUENCER = dict(axis_name="seq", num_cores=1)


def _gather_shards_async(srcs, token, name, collective_id):
    n = len(srcs)

    def body(*refs):
        tok, tok_out = refs[n], refs[2 * n + 1]
        ssem, rsem, lsem = refs[2 * n + 2:]
        _handshake_all()
        _gather_body(refs[:n], refs[n + 1:2 * n + 1], ssem, rsem, lsem,
                     extra=(pltpu.make_async_copy(tok, tok_out, lsem.at[n]),))

    res = pl.kernel(
        body, name=name,
        out_type=[jax.ShapeDtypeStruct((4,) + s.shape, s.dtype) for s in srcs]
        + [jax.ShapeDtypeStruct(token.shape, token.dtype)],
        mesh=plsc.ScalarSubcoreMesh(**SEQUENCER),
        scratch_types=[pltpu.SemaphoreType.DMA((n, 6)), pltpu.SemaphoreType.DMA((n, 6)), pltpu.SemaphoreType.DMA((n + 1,))],
        compiler_params=pltpu.CompilerParams(collective_id=collective_id),
    )(*srcs, token)
    return res[:n], res[n]


def _gather_body(src, out, ssem, rsem, lsem, extra):
        n = len(src)
        x, y, c, (k_me, k_x, k_y, k_d) = _where_am_i()
        sib = (x, y, 1 - c)
        local = [pltpu.make_async_copy(src[i], out[i].at[k_me], lsem.at[i]) for i in range(n)] + list(extra)
        for cp in local:
            cp.start()

        def own(i, q, to):
            return _rc(src[i], out[i].at[k_me], ssem.at[i, q], rsem.at[i, q], to)

        def slab(i, q, k, to):
            return _rc(out[i].at[k], out[i].at[k], ssem.at[i, q], rsem.at[i, q], to)

        @pl.when(c == 1)
        def _():
            sends = [own(i, 0, (1 - x, y, 1)) for i in range(n)]
            for cp in sends:
                cp.start()
            fwd = [slab(i, 3, k_x, sib) for i in range(n)]
            for i in range(n):
                slab(i, 0, k_x, sib).wait_recv()
                fwd[i].start()
            for i in range(n):
                slab(i, 4, k_y, sib).wait_recv()
                slab(i, 5, k_d, sib).wait_recv()
            for cp in sends + fwd:
                cp.wait_send()

        @pl.when(c == 0)
        def _():
            sends = [own(i, 1, (x, 1 - y, 0)) for i in range(n)] + [own(i, 2, (1 - x, 1 - y, 0)) for i in range(n)]
            for cp in sends:
                cp.start()
            fwd = [slab(i, 4, k_y, sib) for i in range(n)] + [slab(i, 5, k_d, sib) for i in range(n)]
            for i in range(n):
                slab(i, 1, k_y, sib).wait_recv()
                fwd[i].start()
            for i in range(n):
                slab(i, 2, k_d, sib).wait_recv()
                fwd[n + i].start()
            for i in range(n):
                slab(i, 3, k_x, sib).wait_recv()
            for cp in sends + fwd:
                cp.wait_send()

        for cp in local:
            cp.wait()


def _swap_slabs(arrs, n_slabs, name, async_id, token):
    n = len(arrs)
    J = max(n_slabs, 1)

    def body(*refs):
        src, out = refs[:n], refs[n:2 * n]
        ssem, rsem = refs[2 * n:]
        x, y, c, (k_me, k_x, k_y, k_d) = _where_am_i()
        sib = (x, y, 1 - c)
        sel = (jnp.where(c == 1, k_y, k_me), jnp.where(c == 1, k_d, k_x))
        cps = []
        for i in range(n):
            for j in range(J):
                s = src[i].at[sel[j]] if n_slabs else src[i]
                d = out[i].at[j] if n_slabs else out[i]
                cps.append(_rc(s, d, ssem.at[i, j], rsem.at[i, j], sib))
        for cp in cps:
            cp.start()
        for cp in cps:
            cp.wait()

    return _exchange_call(
        body, arrs, [jax.ShapeDtypeStruct(((n_slabs,) if n_slabs else ()) + a.shape[-2:], a.dtype) for a in arrs],
        [pltpu.SemaphoreType.DMA((n, J)), pltpu.SemaphoreType.DMA((n, J))], name, async_id, token)


def _exchange_call(body, arrs, out_shape, sems, name, collective_id, token):
    n, m = len(arrs), len(out_shape)

    def seq_body(*refs):
        tok, tok_out, tok_sem = refs[n], refs[n + 1 + m], refs[-1]
        _handshake_all()
        cp = pltpu.make_async_copy(tok, tok_out, tok_sem)
        cp.start()
        body(*refs[:n], *refs[n + 1:n + 1 + m], *refs[n + 2 + m:-1])
        cp.wait()

    res = pl.kernel(
        seq_body, name=name, out_type=list(out_shape) + [jax.ShapeDtypeStruct(token.shape, token.dtype)],
        mesh=plsc.ScalarSubcoreMesh(**SEQUENCER), scratch_types=list(sems) + [pltpu.SemaphoreType.DMA],
        compiler_params=pltpu.CompilerParams(collective_id=collective_id))(*arrs, token)
    return res[:m], res[m]


def _send_chip_sums(arrs, name, async_id, token):
    n = len(arrs)

    def body(*refs):
        src, out = refs[:n], refs[n:2 * n]
        ssem, rsem = refs[2 * n:]
        x, y, c, _ = _where_am_i()

        @pl.when(c == 1)
        def _():
            cps = [_rc(src[i].at[1], out[i].at[0], ssem.at[i, 0], rsem.at[i, 0], (1 - x, y, 1)) for i in range(n)]
            for cp in cps:
                cp.start()
            for cp in cps:
                cp.wait()

        @pl.when(c == 0)
        def _():
            cps = [_rc(src[i].at[0], out[i].at[0], ssem.at[i, 0], rsem.at[i, 0], (x, 1 - y, 0)) for i in range(n)]
            cps += [_rc(src[i].at[1], out[i].at[1], ssem.at[i, 1], rsem.at[i, 1], (1 - x, 1 - y, 0)) for i in range(n)]
            for cp in cps:
                cp.start()
            for cp in cps:
                cp.wait()

    return _exchange_call(
        body, arrs, [jax.ShapeDtypeStruct(a.shape, a.dtype) for a in arrs],
        [pltpu.SemaphoreType.DMA((n, 2)), pltpu.SemaphoreType.DMA((n, 2))], name, async_id, token)


def _chip_sum(g, recv, sel, name):
    _, r, C = g.shape
    tr = _rtile(r)

    def body(sel_ref, g_ref, r_ref, o32_ref, o16_ref):
        s = g_ref[...] + r_ref[...]
        o32_ref[...] = s
        o16_ref[...] = s.astype(BF16)

    blk = pl.BlockSpec((None, tr, C), lambda j, t, sel_ref: (j, t, 0))
    return pl.pallas_call(
        body, name=name,
        grid_spec=pltpu.PrefetchScalarGridSpec(
            num_scalar_prefetch=1, grid=(2, r // tr),
            in_specs=[pl.BlockSpec((None, tr, C), lambda j, t, sel_ref: (sel_ref[j], t, 0)), blk],
            out_specs=[blk, blk]),
        out_shape=[jax.ShapeDtypeStruct((2, r, C), F32), jax.ShapeDtypeStruct((2, r, C), BF16)],
        compiler_params=_params(("arbitrary", "arbitrary")),
    )(sel, g, recv)


def _cross_sum(s32, got, north, after, name):
    _, r, C = s32.shape
    tr = _rtile(r)

    def body(f_ref, p_ref, a_ref, b_ref, after_ref, o_ref):
        a = a_ref[...].astype(F32)

        @pl.when(f_ref[0] == 1)
        def _():
            o_ref[...] = p_ref[...] + a

        @pl.when(f_ref[0] == 0)
        def _():
            o_ref[...] = a + b_ref[...].astype(F32)

    return pl.pallas_call(
        body, name=name,
        grid_spec=pltpu.PrefetchScalarGridSpec(
            num_scalar_prefetch=1, grid=(r // tr,),
            in_specs=[pl.BlockSpec((None, tr, C), lambda t, f: (0, t, 0)),
                      pl.BlockSpec((None, tr, C), lambda t, f: (0, t, 0)),
                      pl.BlockSpec((None, tr, C), lambda t, f: (1 - f[0], t, 0)), ANY],
            out_specs=pl.BlockSpec((tr, C), lambda t, f: (t, 0))),
        out_shape=jax.ShapeDtypeStruct((r, C), F32),
        compiler_params=_params(("arbitrary",)),
    )(north, s32, got, got, after)


class _Reducer:
    def __init__(self, token):
        self.token, self.pending, self.pairs = token, None, {}

    def begin(self, tag, ids, grads, after):
        names = list(grads)
        arrs = [grads[n].reshape(4, -1, D_MODEL) for n in names]
        x, y, c, (k_me, k_x, k_y, k_d) = _where_am_i()
        sel = jnp.stack([jnp.where(c == 1, k_me, k_y), jnp.where(c == 1, k_x, k_d)]).astype(jnp.int32)
        from_sib, self.token = _swap_slabs(arrs, 2, f"swap_grad_slabs_{tag}", ids[0], self.token)
        sums = [_chip_sum(g, r, sel, f"chip_sum_{tag}{i}") for i, (g, r) in enumerate(zip(arrs, from_sib))]
        self.finish(after)
        got, self.token = _send_chip_sums([s16 for _, s16 in sums], f"send_chip_sums_{tag}", ids[1], self.token)
        self.pending = (tag, ids[2], names, sums, got)
        return tuple(s16 for _, s16 in sums)

    def finish(self, after):
        if self.pending is None:
            return
        tag, cid, names, sums, got = self.pending
        north = jnp.reshape(lax.axis_index("c"), (1,)).astype(jnp.int32)
        parts = [_cross_sum(s32, b, north, after, f"cross_sum_{tag}{i}") for i, ((s32, _), b) in enumerate(zip(sums, got))]
        theirs, self.token = _swap_slabs(parts, 0, f"swap_parts_{tag}", cid, self.token)
        self.pairs.update(zip(names, zip(parts, theirs)))
        self.pending = None


def _all_reduce_small(v, token, collective_id):
    R, C = v.shape
    tr = R

    def body(v_ref, buf, ssem, rsem, lsem):
        x, y, c = lax.axis_index("x"), lax.axis_index("y"), lax.axis_index("c")
        me, sib = (x, y, c), (x, y, 1 - c)
        chips = [(1 - x, y), (x, 1 - y), (1 - x, 1 - y)]

        def slot(px, py, pc):
            return buf.at[4 * px + 2 * py + pc]

        def copy(k, block, to, src=None):
            return _rc(slot(*block) if src is None else src, slot(*block), ssem.at[k], rsem.at[k], to)

        mine = pltpu.make_async_copy(v_ref, slot(*me), lsem)
        mine.start()
        first = [copy(0, me, sib, src=v_ref)] + [copy(1 + j, me, (*chip, c), src=v_ref) for j, chip in enumerate(chips)]
        for cp in first:
            cp.start()
        passed = [copy(4 + j, (*chip, c), sib) for j, chip in enumerate(chips)]
        for j, chip in enumerate(chips):
            copy(1 + j, (*chip, c), me).wait_recv()
            passed[j].start()
        copy(0, sib, me).wait_recv()
        for j, chip in enumerate(chips):
            copy(4 + j, (*chip, 1 - c), me).wait_recv()
        for cp in first + passed:
            cp.wait_send()
        mine.wait()

    (gathered,), token = _exchange_call(
        body, [v], [jax.ShapeDtypeStruct((8, R, C), F32)],
        [pltpu.SemaphoreType.DMA((7,)), pltpu.SemaphoreType.DMA((7,)), pltpu.SemaphoreType.DMA],
        "gather_small", collective_id, token)

    def add(g_ref, o_ref):
        acc = g_ref[0]
        for k in range(1, 8):
            acc = acc + g_ref[k]
        o_ref[...] = acc

    total = pl.pallas_call(
        add, name="sum_small", grid=(R // tr,),
        in_specs=[pl.BlockSpec((8, tr, C), lambda i: (0, i, 0))], out_specs=pl.BlockSpec((tr, C), lambda i: (i, 0)),
        out_shape=jax.ShapeDtypeStruct((R, C), F32), compiler_params=_params(("parallel",)),
    )(gathered)
    return total, token


WEIGHTS = ['ffn1_norm', 'ffn1_w_gate', 'ffn1_w_up', 'ffn1_w_down', 'mix_norm', 'w_in', 'conv_w', 'conv_b', 's5_A_re',
           's5_A_im', 's5_log_dt', 's5_B_re', 's5_B_im', 's5_C_re', 's5_C_im', 's5_D', 's5_w_glu', 's5_b_glu',
           'ssd_A_log', 'ssd_dt_bias', 'ssd_D', 'ssd_norm', 'w_proj_s5', 'w_proj_ssd', 'b_gate', 'w_out', 'ffn2_norm',
           'ffn2_w_gate', 'ffn2_w_up', 'ffn2_w_down', 'final_norm']
ARGS = ['x'] + WEIGHTS + ['loss_target'] + ['m_' + n for n in WEIGHTS] + ['v_' + n for n in WEIGHTS]
COL_SHARDED = ('ffn1_w_gate', 'ffn1_w_up', 'w_in', 'ffn2_w_gate', 'ffn2_w_up')
ROW_SHARDED = ('ffn1_w_down', 's5_w_glu', 'w_proj_s5', 'w_proj_ssd', 'w_out', 'ffn2_w_down')
MATRICES = COL_SHARDED + ROW_SHARDED
GATHER_FIRST = ('ffn1_w_gate', 'ffn1_w_up')
GATHER_FIRST_LATER = ('ffn1_w_down',)
GATHER_SECOND = ('w_in', 'conv_w', 's5_w_glu', 'w_proj_s5', 'w_proj_ssd', 'w_out')
GATHER_THIRD = ('ffn2_w_gate', 'ffn2_w_up', 'ffn2_w_down')
SMALL = [n for n in WEIGHTS if n not in MATRICES]


def _pack(arrs, width=1024):
    flat = jnp.concatenate([a.reshape(-1).astype(F32) for a in arrs])
    rows = -(-flat.shape[0] // (8 * width)) * 8
    return jnp.pad(flat, (0, rows * width - flat.shape[0])).reshape(rows, width)


def _unpack(packed, shapes):
    flat, out, o = packed.reshape(-1), [], 0
    for s in shapes:
        n = math.prod(s)
        out.append(flat[o:o + n].reshape(s))
        o += n
    return out


def kernel(x, ffn1_norm, ffn1_w_gate, ffn1_w_up, ffn1_w_down, mix_norm, w_in, conv_w, conv_b, s5_A_re, s5_A_im, s5_log_dt, s5_B_re, s5_B_im, s5_C_re, s5_C_im, s5_D, s5_w_glu, s5_b_glu, ssd_A_log, ssd_dt_bias, ssd_D, ssd_norm, w_proj_s5, w_proj_ssd, b_gate, w_out, ffn2_norm, ffn2_w_gate, ffn2_w_up, ffn2_w_down, final_norm, loss_target, m_ffn1_norm, m_ffn1_w_gate, m_ffn1_w_up, m_ffn1_w_down, m_mix_norm, m_w_in, m_conv_w, m_conv_b, m_s5_A_re, m_s5_A_im, m_s5_log_dt, m_s5_B_re, m_s5_B_im, m_s5_C_re, m_s5_C_im, m_s5_D, m_s5_w_glu, m_s5_b_glu, m_ssd_A_log, m_ssd_dt_bias, m_ssd_D, m_ssd_norm, m_w_proj_s5, m_w_proj_ssd, m_b_gate, m_w_out, m_ffn2_norm, m_ffn2_w_gate, m_ffn2_w_up, m_ffn2_w_down, m_final_norm, v_ffn1_norm, v_ffn1_w_gate, v_ffn1_w_up, v_ffn1_w_down, v_mix_norm, v_w_in, v_conv_w, v_conv_b, v_s5_A_re, v_s5_A_im, v_s5_log_dt, v_s5_B_re, v_s5_B_im, v_s5_C_re, v_s5_C_im, v_s5_D, v_s5_w_glu, v_s5_b_glu, v_ssd_A_log, v_ssd_dt_bias, v_ssd_D, v_ssd_norm, v_w_proj_s5, v_w_proj_ssd, v_b_gate, v_w_out, v_ffn2_norm, v_ffn2_w_gate, v_ffn2_w_up, v_ffn2_w_down, v_final_norm):
    a = dict(locals())
    assert list(a) == ARGS
    x, target = a['x'][0], a['loss_target'][0]
    k_me = 2 * lax.axis_index("x") + lax.axis_index("y")

    src = {n: a[n][0].T.astype(BF16) for n in COL_SHARDED}
    src.update({n: a[n][0].astype(BF16) for n in ROW_SHARDED})
    src['conv_w'] = a['conv_w'][0]
    def flat(f):
        return f.reshape(-1, f.shape[-1])

    full = _gather_shards([src[n] for n in GATHER_FIRST])
    w = {n: flat(f) for n, f in zip(GATHER_FIRST, full)}
    got, token = _gather_shards_async(
        [src[n] for n in GATHER_FIRST_LATER], full[0][0, :8, :128].astype(F32), "gather_ffn1_down", 14)
    w.update({n: flat(f) for n, f in zip(GATHER_FIRST_LATER, got)})
    reducer = _Reducer(token)

    def gather_later(stage, after):
        token = reducer.token + after[:8, :128].astype(F32)
        calls = {"mixer": ((GATHER_SECOND[:2], "gather_mixer_in", 1), (GATHER_SECOND[2:], "gather_mixer", 12)),
                 "ffn2": ((GATHER_THIRD, "gather_ffn2", 2),)}[stage]
        out = {}
        for names, name, cid in calls:
            got, token = _gather_shards_async([src[n] for n in names], token, name, cid)
            out.update({n: flat(f) for n, f in zip(names, got)})
        reducer.token = token
        if 'conv_w' in out:
            out['conv_w'] = out['conv_w'].reshape(4, CONV_K, -1).transpose(1, 0, 2).reshape(CONV_K, CONV_DIM)
        return out

    p = {n: a[n] for n in SMALL if n != 'conv_w'}
    p['final_norm'] = a['final_norm'][None]
    ids = {"ffn2": (3, 4, 5), "mixer": (6, 7, 8), "ffn1": (9, 10, 11)}
    loss, grad_x, gp, conv_w_grad = _local_step(
        x, target, p, w, gather_later, lambda tag, grads, after: reducer.begin(tag, ids[tag], grads, after))
    gp['conv_w'] = conv_w_grad[None]
    small_shapes = [(1, 1)] + [gp[n].shape if n != 'final_norm' else (1, D_MODEL) for n in SMALL]
    summed, reducer.token = _all_reduce_small(_pack([loss] + [gp[n] for n in SMALL]), reducer.token, 13)
    red = _unpack(summed, small_shapes)
    loss_all = red[0].reshape(())
    gsmall = dict(zip(SMALL, red[1:]))
    gsmall['conv_w'] = lax.dynamic_slice_in_dim(gsmall['conv_w'], k_me * 1024, 1024, axis=2)
    gsmall = {n: g.reshape(a[n].shape) for n, g in gsmall.items()}

    grads, delta, new_m, new_v = {}, {}, {}, {}

    def update(names):
        for n in names:
            turn = (lambda t: jnp.swapaxes(t, 1, 2)) if n in COL_SHARDED else (lambda t: t)
            out = _adamw(turn(a[n]), turn(a['m_' + n]), turn(a['v_' + n]), list(reducer.pairs[n]),
                         name=f"adamw_{n}", emit_g=True)
            grads[n], delta[n], new_m[n], new_v[n] = [turn(t) for t in out]

    update([n for n in MATRICES if n not in GATHER_FIRST + GATHER_FIRST_LATER])
    reducer.finish(delta['w_in'])
    update(GATHER_FIRST + GATHER_FIRST_LATER)
    sw, sm, sv, sg = [_pack([t[n] for n in SMALL])[None] for t in (a, {n: a['m_' + n] for n in SMALL},
                                                                   {n: a['v_' + n] for n in SMALL}, gsmall)]
    d, m2, v2 = _adamw(sw, sm, sv, [sg[0]], name="adamw_small", emit_g=False)
    shapes = [a[n].shape for n in SMALL]
    for n, dd, mm, vv in zip(SMALL, _unpack(d[0], shapes), _unpack(m2[0], shapes), _unpack(v2[0], shapes)):
        grads[n], delta[n], new_m[n], new_v[n] = gsmall[n], dd, mm, vv
    return (loss_all, grad_x[None], *[grads[n] for n in WEIGHTS], *[delta[n] for n in WEIGHTS],
            *[new_m[n] for n in WEIGHTS], *[new_v[n] for n in WEIGHTS])
```

```python
import itertools
import math

import jax
import jax.numpy as jnp
from jax import lax
from jax.experimental import pallas as pl
from jax.experimental.pallas import tpu as pltpu
from jax.experimental.pallas import tpu_sc as plsc

F32 = jnp.float32
BF16 = jnp.bfloat16
MXU = BF16
HI = lax.Precision.HIGHEST

D_MODEL = 1024
D_FF = 2816
EPS = 1e-6
S5_GROUPS, S5_GROUP, S5_STATE = 64, 16, 64
D_INNER = 2048
SSD_HEADDIM, SSD_HEADS, SSD_GROUPS, SSD_HPG, SSD_STATE = 64, 32, 8, 4, 128
CONV_K, CONV_DIM = 4, 4096
IN_OFFS = (0, 1024, 3072, 7168, 7200, 9248)
SSD_Q = 256
SSD_STEP_GROUPS = 4
ADAM_LR, ADAM_B1, ADAM_B2, ADAM_EPS, ADAM_WD, ADAM_STEP = 0.001, 0.9, 0.999, 1e-08, 0.01, 10

VMEM_LIMIT = 56 * 1024 * 1024
MATMUL_VMEM = 40 * 1024 * 1024
MESH = pl.DeviceIdType.MESH

NN = ((1,), (0,))
NT = ((1,), (1,))
TN = ((0,), (0,))


def _dot(a, b, dims, precision=None):
    if precision is None:
        a, b = a.astype(MXU), b.astype(MXU)
    return lax.dot_general(a, b, (dims, ((), ())), precision=precision, preferred_element_type=F32)


def _tile(n, pref):
    if n <= pref:
        return n
    best = None
    for t in range(128, pref + 1, 128):
        if n % t == 0:
            best = t
    assert best is not None, (n, pref)
    return best


def _params(sem):
    return pltpu.CompilerParams(dimension_semantics=sem, vmem_limit_bytes=VMEM_LIMIT)


def _matmul(a, b, mode, *, name, out=((F32),), epilogue=None, extras=(), after=(), tm=1024, tn=1536):
    if mode == "nn":
        (M, K), (_, N) = a.shape, b.shape
    elif mode == "nt":
        (M, K), (N, _) = a.shape, b.shape
    else:
        (K, M), (_, N) = a.shape, b.shape
    tm, tn = _tile(M, tm), _tile(N, tn)

    def vmem(tm_, tn_):
        per = K * tm_ * a.dtype.itemsize + K * tn_ * b.dtype.itemsize
        per += sum(tm_ * tn_ * jnp.dtype(dt).itemsize for dt in out)
        per += sum((1 if e.shape[0] == 1 else tm_) * tn_ * e.dtype.itemsize for e in extras)
        return 2 * per

    while vmem(tm, tn) > MATMUL_VMEM and (tn > 128 or tm > 128):
        if tn >= tm and tn > 128:
            tn = _tile(N, tn - 128)
        else:
            tm = _tile(M, tm - 128)
    bytes_a, bytes_b = a.size * a.dtype.itemsize, b.size * b.dtype.itemsize
    rows_outer = bytes_a + (M // tm) * bytes_b <= (N // tn) * bytes_a + bytes_b
    grid = (M // tm, N // tn) if rows_outer else (N // tn, M // tm)

    def at(f):
        return (lambda i, j: f(i, j)) if rows_outer else (lambda j, i: f(i, j))

    a_spec = pl.BlockSpec((K, tm), at(lambda i, j: (0, i))) if mode == "tn" else pl.BlockSpec((tm, K), at(lambda i, j: (i, 0)))
    b_spec = pl.BlockSpec((tn, K), at(lambda i, j: (j, 0))) if mode == "nt" else pl.BlockSpec((K, tn), at(lambda i, j: (0, j)))
    dims = {"nn": NN, "nt": NT, "tn": TN}[mode]
    e_specs = []
    for e in extras:
        if e.shape[0] == 1:
            e_specs.append(pl.BlockSpec((1, tn), at(lambda i, j: (0, j))))
        else:
            assert e.shape == (M, N), (e.shape, M, N)
            e_specs.append(pl.BlockSpec((tm, tn), at(lambda i, j: (i, j))))
    n_e, n_o, n_a = len(extras), len(out), len(after)

    def body(a_ref, b_ref, *refs):
        acc = _dot(a_ref[...], b_ref[...], dims)
        res = (acc,) if epilogue is None else epilogue(acc, *[r[...] for r in refs[:n_e]])
        for r, v in zip(refs[n_e + n_a:], res):
            r[...] = v.astype(r.dtype)

    res = pl.pallas_call(
        body, name=name, grid=grid,
        in_specs=[a_spec, b_spec] + e_specs + [ANY] * n_a,
        out_specs=[pl.BlockSpec((tm, tn), at(lambda i, j: (i, j))) for _ in range(n_o)],
        out_shape=[jax.ShapeDtypeStruct((M, N), dt) for dt in out],
        compiler_params=_params(("parallel", "parallel")),
    )(a, b, *extras, *after)
    return res[0] if n_o == 1 else res


ROW_TILE = 256
ROW_TILE_WIDE_UP_TO = 1024


def _rowwise(fn, rows, fulls, outs, reds=(), *, name, after=(), tm=ROW_TILE):
    T = rows[0].shape[0]
    widest = max([r.shape[1] for r in rows] + [c for c, _ in outs])
    if tm == ROW_TILE and widest <= ROW_TILE_WIDE_UP_TO and T % (2 * tm) == 0:
        tm *= 2
    tm = min(tm, T)
    n_r, n_f, n_o, n_d, n_a = len(rows), len(fulls), len(outs), len(reds), len(after)

    def body(*refs):
        ins = [r[...] for r in refs[:n_r + n_f]]
        o_refs = refs[n_r + n_f + n_a:n_r + n_f + n_a + n_o]
        d_refs = refs[n_r + n_f + n_a + n_o:]
        res = fn(*ins)
        for r, v in zip(o_refs, res[:n_o]):
            r[...] = v.astype(r.dtype)
        if n_d:
            @pl.when(pl.program_id(0) == 0)
            def _():
                for r in d_refs:
                    r[...] = jnp.zeros_like(r)
            for r, v in zip(d_refs, res[n_o:]):
                r[...] += v.astype(r.dtype)

    res = pl.pallas_call(
        body, name=name, grid=(T // tm,),
        in_specs=[pl.BlockSpec((tm, r.shape[1]), lambda i: (i, 0)) for r in rows]
        + [pl.BlockSpec(f.shape, lambda i, nd=f.ndim: (0,) * nd) for f in fulls] + [ANY] * n_a,
        out_specs=[pl.BlockSpec((tm, c), lambda i: (i, 0)) for c, _ in outs]
        + [pl.BlockSpec(s, lambda i, nd=len(s): (0,) * nd) for s, _ in reds],
        out_shape=[jax.ShapeDtypeStruct((T, c), dt) for c, dt in outs]
        + [jax.ShapeDtypeStruct(s, dt) for s, dt in reds],
        compiler_params=_params(("arbitrary",)),
    )(*rows, *fulls, *after)
    return res


def _rms(x, g):
    return x * lax.rsqrt(jnp.mean(x * x, axis=-1, keepdims=True) + EPS) * g


def _colsum(v):
    return jnp.sum(v, axis=0, keepdims=True)


def _softplus(x):
    return jnp.maximum(x, 0.0) + jnp.log1p(jnp.exp(-jnp.abs(x)))


def _ffn_fwd(x, norm, wgT, wuT, wd, tag, on_gate=None):
    D = x.shape[1]
    (hn,) = _rowwise(lambda xt, g: (_rms(xt, g),), [x], [norm], [(D, BF16)], name=f"{tag}_norm")
    a = _matmul(hn, wgT, "nt", name=f"{tag}_gate", out=(BF16,))
    if on_gate is not None:
        on_gate(a)
    b, hid = _matmul(hn, wuT, "nt", name=f"{tag}_up", out=(BF16, BF16),
                     epilogue=lambda acc, at: (acc, jax.nn.silu(at.astype(F32)) * acc), extras=[a])
    y = _matmul(hid, wd, "nn", name=f"{tag}_down", epilogue=lambda acc, xt: (xt + 0.5 * acc,), extras=[x])
    return y, (hn, a, b, hid)


def _ffn_bwd(dy, dy16, x, norm, wgT, wuT, wd, res, tag, after, on_grads):
    hn, a, b, hid = res
    D, F = x.shape[1], a.shape[1]

    def act_bwd(acc, at, bt):
        a_, b_ = at.astype(F32), bt.astype(F32)
        sig = jax.nn.sigmoid(a_)
        s = a_ * sig
        dh = 0.5 * acc
        return dh * b_ * (sig + s * (1.0 - sig)), dh * s

    da, db = _matmul(dy16, wd, "nt", name=f"{tag}_d_hid", out=(BF16, BF16), epilogue=act_bwd, extras=[a, b])
    d_wd = _matmul(hid, dy16, "tn", name=f"{tag}_d_wd", epilogue=lambda acc: (0.5 * acc,))
    d_wgT = _matmul(da, hn, "tn", name=f"{tag}_d_wg", after=after)
    d_wuT = _matmul(db, hn, "tn", name=f"{tag}_d_wu")
    started = on_grads(d_wgT, d_wuT, d_wd, dy)
    dhn = _matmul(da, wgT, "nn", name=f"{tag}_d_hn1")
    dhn = _matmul(db, wuT, "nn", name=f"{tag}_d_hn2", epilogue=lambda acc, e: (acc + e,), extras=[dhn])

    def norm_bwd(xt, dh, dyt, g):
        _, vjp = jax.vjp(_rms, xt, g)
        dx, dg = vjp(dh)
        return dyt + dx, dyt + dx, dg

    dx, dx16, d_norm = _rowwise(norm_bwd, [x, dhn, dy], [norm], [(D, F32), (D, BF16)], [((1, D), F32)],
                                name=f"{tag}_d_norm", after=started)
    return dx, dx16, d_norm


S5_TILES = 8
S5_HALF = 256


def _s5_derive(A_re, A_im, log_dt, B_re, B_im, C_re, C_im):
    G, N, M = S5_GROUPS, S5_STATE, S5_GROUP
    dt = jnp.exp(log_dt)[:, None]
    mag = jnp.exp(A_re * dt)
    ar, ai = mag * jnp.cos(A_im * dt), mag * jnp.sin(A_im * dt)
    den = A_re * A_re + A_im * A_im
    cr = ((ar - 1.0) * A_re + ai * A_im) / den
    ci = (ai * A_re - (ar - 1.0) * A_im) / den
    bbr = cr[..., None] * B_re - ci[..., None] * B_im
    bbi = cr[..., None] * B_im + ci[..., None] * B_re
    eye = jnp.eye(8, dtype=F32)

    def tile_in(bb):
        t = bb.reshape(S5_TILES, 8, N, M).transpose(0, 1, 3, 2)
        return jnp.einsum("jamn,ab->jambn", t, eye).reshape(S5_TILES, 8 * M, 8 * N)

    def tile_out(c):
        t = c.reshape(S5_TILES, 8, M, N).transpose(0, 1, 3, 2)
        return jnp.einsum("janm,ab->janbm", t, eye).reshape(S5_TILES, 8 * N, 8 * M)

    return (tile_in(bbr), tile_in(bbi), tile_out(C_re), tile_out(C_im),
            ar.reshape(S5_TILES, 1, 8 * N), ai.reshape(S5_TILES, 1, 8 * N))


S5_NB = S5_HALF // 128
S5_SEG = 32


def _cmul(ar, ai, br, bi):
    return ar * br - ai * bi, ar * bi + ai * br


def _s5_scan(sr_ref, si_ref, ar, ai, T, reverse):
    L, V = T // S5_SEG, S5_SEG // 8
    assert L * S5_SEG == T and L & (L - 1) == 0, T
    sg = -1.0 if reverse else 1.0
    a_r = [jnp.broadcast_to(ar[:, 128 * b:128 * (b + 1)], (8, 128)) for b in range(S5_NB)]
    a_i = [jnp.broadcast_to(sg * ai[:, 128 * b:128 * (b + 1)], (8, 128)) for b in range(S5_NB)]

    def rows(k, v):
        return pl.ds(pl.multiple_of(((L - 1 - k) if reverse else k) * S5_SEG + 8 * v, 8), 8)

    def local(k, carry):
        out = []
        for b in range(S5_NB):
            for v in range(V):
                idx = rows(k, v)
                mr, mi = _cmul(a_r[b], a_i[b], *carry[b * V + v])
                nr, ni = mr + sr_ref[b, idx, :], mi + si_ref[b, idx, :]
                sr_ref[b, idx, :] = nr
                si_ref[b, idx, :] = ni
                out.append((nr, ni))
        return tuple(out)

    z = jnp.zeros((8, 128), F32)
    ends = lax.fori_loop(0, L, local, tuple((z, z) for _ in range(S5_NB * V)))

    carries = []
    for b in range(S5_NB):
        pr, pi = a_r[b][0:1], a_i[b][0:1]
        n = L
        while n > 1:
            pr, pi = _cmul(pr, pi, pr, pi)
            n //= 2
        cr, ci = jnp.zeros((1, 128), F32), jnp.zeros((1, 128), F32)
        into = [None] * S5_SEG
        for j in (reversed(range(S5_SEG)) if reverse else range(S5_SEG)):
            into[j] = (cr, ci)
            er, ei = ends[b * V + j // 8]
            mr, mi = _cmul(pr, pi, cr, ci)
            cr, ci = mr + er[j % 8:j % 8 + 1], mi + ei[j % 8:j % 8 + 1]
        carries.append([(jnp.concatenate([into[8 * v + s][0] for s in range(8)], axis=0),
                         jnp.concatenate([into[8 * v + s][1] for s in range(8)], axis=0)) for v in range(V)])

    def fix(k, powers):
        out = []
        for b in range(S5_NB):
            pr, pi = powers[b]
            for v in range(V):
                idx = rows(k, v)
                dr, di = _cmul(pr, pi, *carries[b][v])
                sr_ref[b, idx, :] += dr
                si_ref[b, idx, :] += di
            out.append(_cmul(pr, pi, a_r[b], a_i[b]))
        return tuple(out)

    lax.fori_loop(0, L, fix, tuple((a_r[b], a_i[b]) for b in range(S5_NB)))


def _s5_in(ut, b_ref, s_ref):
    for b in range(S5_NB):
        s_ref[b] = _dot(ut, b_ref[:, 128 * b:128 * (b + 1)], NN)


def _to_seg(v):
    T, C = v.shape
    return v.reshape(S5_SEG, T // S5_SEG, C).transpose(1, 0, 2).reshape(T, C)


def _from_seg(v):
    T, C = v.shape
    return v.reshape(T // S5_SEG, S5_SEG, C).transpose(1, 0, 2).reshape(T, C)


def _s5_specs(T):
    u_spec = pl.BlockSpec((T, 128), lambda j, h: (0, j))
    b_spec = pl.BlockSpec((None, 128, S5_HALF), lambda j, h: (j, 0, h))
    c_spec = pl.BlockSpec((None, S5_HALF, 128), lambda j, h: (j, h, 0))
    a_spec = pl.BlockSpec((None, 1, S5_HALF), lambda j, h: (j, 0, h))
    return u_spec, b_spec, c_spec, a_spec


def _s5_fwd(u, tiles):
    T = u.shape[0]
    u_spec, b_spec, c_spec, a_spec = _s5_specs(T)

    def body(u_ref, br_ref, bi_ref, cr_ref, ci_ref, ar_ref, ai_ref, y_ref, sr_ref, si_ref):
        ut = u_ref[...].astype(MXU)
        _s5_in(ut, br_ref, sr_ref)
        _s5_in(ut, bi_ref, si_ref)
        _s5_scan(sr_ref, si_ref, ar_ref[...], ai_ref[...], T, False)
        y = None
        for b in range(S5_NB):
            blk = slice(128 * b, 128 * (b + 1))
            yb = _dot(sr_ref[b], cr_ref[blk, :], NN) - _dot(si_ref[b], ci_ref[blk, :], NN)
            y = yb if y is None else y + yb

        @pl.when(pl.program_id(1) == 0)
        def _():
            y_ref[...] = y

        @pl.when(pl.program_id(1) == 1)
        def _():
            y_ref[...] += y

    scr = pltpu.VMEM((S5_NB, T, 128), F32)
    return pl.pallas_call(
        body, name="s5_fwd", grid=(S5_TILES, 2),
        in_specs=[u_spec, b_spec, b_spec, c_spec, c_spec, a_spec, a_spec],
        out_specs=u_spec, out_shape=jax.ShapeDtypeStruct(u.shape, F32),
        scratch_shapes=[scr, scr],
        compiler_params=_params(("parallel", "arbitrary")),
    )(u, *tiles)


def _s5_bwd(u, dys, du_skip, tiles):
    T = u.shape[0]
    u_spec, b_spec, c_spec, a_spec = _s5_specs(T)

    def body(u_ref, dy_ref, sk_ref, br_ref, bi_ref, cr_ref, ci_ref, ar_ref, ai_ref,
             du_ref, dbr_ref, dbi_ref, dcr_ref, dci_ref, dar_ref, dai_ref, sr_ref, si_ref, lr_ref, li_ref):
        ut, dy = u_ref[...].astype(MXU), dy_ref[...].astype(MXU)
        ar, ai = ar_ref[...], ai_ref[...]
        _s5_in(ut, br_ref, sr_ref)
        _s5_in(ut, bi_ref, si_ref)
        _s5_scan(sr_ref, si_ref, ar, ai, T, False)
        for b in range(S5_NB):
            blk = slice(128 * b, 128 * (b + 1))
            dcr_ref[blk, :] = _dot(sr_ref[b], dy, TN)
            dci_ref[blk, :] = -_dot(si_ref[b], dy, TN)
            lr_ref[b] = _dot(dy, cr_ref[blk, :], NT)
            li_ref[b] = -_dot(dy, ci_ref[blk, :], NT)
        _s5_scan(lr_ref, li_ref, ar, ai, T, True)
        du = None
        head = lax.broadcasted_iota(jnp.int32, (T, 128), 0) < S5_SEG
        seg0 = lax.broadcasted_iota(jnp.int32, (S5_SEG, 128), 0) == 0
        for b in range(S5_NB):
            blk = slice(128 * b, 128 * (b + 1))
            lr, li = lr_ref[b], li_ref[b]
            pr = jnp.where(head, 0.0, pltpu.roll(sr_ref[b], S5_SEG, 0))
            pi = jnp.where(head, 0.0, pltpu.roll(si_ref[b], S5_SEG, 0))
            er = jnp.where(seg0, 0.0, pltpu.roll(sr_ref[b, T - S5_SEG:T, :], 1, 0))
            ei = jnp.where(seg0, 0.0, pltpu.roll(si_ref[b, T - S5_SEG:T, :], 1, 0))
            hr, hi = lr[0:S5_SEG], li[0:S5_SEG]
            dar_ref[:, blk] = _colsum(lr * pr + li * pi) + _colsum(hr * er + hi * ei)
            dai_ref[:, blk] = _colsum(li * pr - lr * pi) + _colsum(hi * er - hr * ei)
            dbr_ref[:, blk] = _dot(ut, lr, TN)
            dbi_ref[:, blk] = _dot(ut, li, TN)
            dub = _dot(lr, br_ref[:, blk], NT) + _dot(li, bi_ref[:, blk], NT)
            du = dub if du is None else du + dub

        @pl.when(pl.program_id(1) == 0)
        def _():
            du_ref[...] = du + sk_ref[...]

        @pl.when(pl.program_id(1) == 1)
        def _():
            du_ref[...] += du

    scr = pltpu.VMEM((S5_NB, T, 128), F32)
    return pl.pallas_call(
        body, name="s5_bwd", grid=(S5_TILES, 2),
        in_specs=[u_spec, u_spec, u_spec, b_spec, b_spec, c_spec, c_spec, a_spec, a_spec],
        out_specs=[u_spec, b_spec, b_spec, c_spec, c_spec, a_spec, a_spec],
        out_shape=[jax.ShapeDtypeStruct(u.shape, F32)] + [jax.ShapeDtypeStruct(t.shape, F32) for t in tiles],
        scratch_shapes=[scr, scr, scr, scr],
        compiler_params=_params(("parallel", "arbitrary")),
    )(u, dys, du_skip, *tiles)


CONV_TILE = 256


def _conv_pre(x, w, b):
    T = x.shape[0]
    row = lax.broadcasted_iota(jnp.int32, x.shape, 0)
    acc = x * w[CONV_K - 1:CONV_K, :] + b
    for lag in range(1, CONV_K):
        acc = acc + jnp.where(row >= lag, pltpu.roll(x, lag, 0), 0.0) * w[CONV_K - 1 - lag:CONV_K - lag, :]
    return acc


def _conv_fwd(x, w, b):
    T, C = x.shape
    col = pl.BlockSpec((T, CONV_TILE), lambda j: (0, j))

    def body(x_ref, w_ref, b_ref, o_ref):
        o_ref[...] = jax.nn.silu(_conv_pre(x_ref[...], w_ref[...], b_ref[...]))

    return pl.pallas_call(
        body, name="conv_fwd", grid=(C // CONV_TILE,),
        in_specs=[col, pl.BlockSpec((CONV_K, CONV_TILE), lambda j: (0, j)), pl.BlockSpec((1, CONV_TILE), lambda j: (0, j))],
        out_specs=col, out_shape=jax.ShapeDtypeStruct((T, C), F32),
        compiler_params=_params(("parallel",)),
    )(x, w, b)


def _conv_bwd(x, w, b, dout, col0, name):
    T, C = dout.shape
    off = col0 // CONV_TILE
    xcol = pl.BlockSpec((T, CONV_TILE), lambda j: (0, j + off))
    dcol = pl.BlockSpec((T, CONV_TILE), lambda j: (0, j))

    def body(x_ref, w_ref, b_ref, d_ref, dx_ref, dw_ref, db_ref):
        x, w = x_ref[...], w_ref[...]
        _, vjp = jax.vjp(jax.nn.silu, _conv_pre(x, w, b_ref[...]))
        (dy,) = vjp(d_ref[...])
        row = lax.broadcasted_iota(jnp.int32, x.shape, 0)
        dx = dy * w[CONV_K - 1:CONV_K, :]
        dw_ref[CONV_K - 1:CONV_K, :] = _colsum(dy * x)
        for lag in range(1, CONV_K):
            ahead = jnp.where(row < T - lag, pltpu.roll(dy, T - lag, 0), 0.0)
            dx = dx + ahead * w[CONV_K - 1 - lag:CONV_K - lag, :]
            dw_ref[CONV_K - 1 - lag:CONV_K - lag, :] = _colsum(ahead * x)
        dx_ref[...] = dx.astype(dx_ref.dtype)
        db_ref[...] = _colsum(dy)

    return pl.pallas_call(
        body, name=name, grid=(C // CONV_TILE,),
        in_specs=[xcol, pl.BlockSpec((CONV_K, CONV_TILE), lambda j: (0, j + off)),
                  pl.BlockSpec((1, CONV_TILE), lambda j: (0, j + off)), dcol],
        out_specs=[dcol, pl.BlockSpec((CONV_K, CONV_TILE), lambda j: (0, j)), pl.BlockSpec((1, CONV_TILE), lambda j: (0, j))],
        out_shape=[jax.ShapeDtypeStruct((T, C), BF16), jax.ShapeDtypeStruct((CONV_K, C), F32), jax.ShapeDtypeStruct((1, C), F32)],
        compiler_params=_params(("parallel",)),
    )(x, w, b, dout)


def _ssd_common(x_ref, b_ref, c_ref, dtr_ref, bias_ref, alog_ref, tri_ref, triu_ref):
    Q = x_ref.shape[0]
    x, Bm, Cm = x_ref[...], b_ref[...], c_ref[...]
    pre = dtr_ref[...] + bias_ref[...]
    dt = _softplus(pre)
    A = -jnp.exp(alog_ref[...])
    adt = dt * A
    ac4 = _dot(tri_ref[...], adt, NN, HI)
    ar4 = _dot(adt, triu_ref[...], TN, HI)
    cb = _dot(Cm, Bm, NT)
    ii = lax.broadcasted_iota(jnp.int32, (Q, Q), 0)
    jj = lax.broadcasted_iota(jnp.int32, (Q, Q), 1)
    atot4 = ac4[Q - 1:Q, :]
    return x, Bm, Cm, pre, dt, A, ac4, ar4, cb, ii >= jj, atot4


def _ssd_decay(k, ac4, ar4, causal):
    seg = ac4[:, k:k + 1] - ar4[k:k + 1, :]
    return jnp.where(causal, jnp.exp(jnp.where(causal, seg, 0.0)), 0.0)


def _per_head(c4, n):
    lane = lax.broadcasted_iota(jnp.int32, (c4.shape[0], n), 1)
    out = jnp.broadcast_to(c4[:, 0:1], (c4.shape[0], n))
    for k in range(1, SSD_HPG):
        out = jnp.where(lane >= SSD_HEADDIM * k, c4[:, k:k + 1], out)
    return out


def _head_sums(v):
    n = SSD_HPG * SSD_HEADDIM
    row = lax.broadcasted_iota(jnp.int32, (n, SSD_HPG), 0)
    col = lax.broadcasted_iota(jnp.int32, (n, SSD_HPG), 1)
    member = jnp.where((row >= SSD_HEADDIM * col) & (row < SSD_HEADDIM * (col + 1)), 1.0, 0.0)
    return _dot(v, member, NN, HI)


def _head_rows(c4):
    n = SSD_HPG * SSD_HEADDIM
    row = lax.broadcasted_iota(jnp.int32, (n, 1), 0)
    out = jnp.broadcast_to(c4[:, 0:1], (n, 1))
    for k in range(1, SSD_HPG):
        out = jnp.where(row >= SSD_HEADDIM * k, c4[:, k:k + 1], out)
    return out


def _ssd_specs(T, Q, rev):
    NC = T // Q
    cc = (lambda c: NC - 1 - c) if rev else (lambda c: c)
    W, N, S = SSD_HPG * SSD_HEADDIM, SSD_STATE, SSD_STEP_GROUPS
    x_spec = pl.BlockSpec((Q, S * W), lambda g, c: (cc(c), g))
    b_spec = pl.BlockSpec((Q, S * N), lambda g, c: (cc(c), D_INNER // (S * N) + g))
    c_spec = pl.BlockSpec((Q, S * N), lambda g, c: (cc(c), (D_INNER + SSD_GROUPS * N) // (S * N) + g))
    dt_spec = pl.BlockSpec((S, Q, SSD_HPG), lambda g, c: (g, cc(c), 0))
    p_spec = pl.BlockSpec((S, 1, SSD_HPG), lambda g, c: (g, 0, 0))
    tri_spec = pl.BlockSpec((Q, Q), lambda g, c: (0, 0))
    h_spec = pl.BlockSpec((S, None, W, N), lambda g, c: (g, cc(c), 0, 0))
    return x_spec, b_spec, c_spec, dt_spec, p_spec, tri_spec, h_spec


def _ssd_views(gi, wide, narrow, lead):
    W, N = SSD_HPG * SSD_HEADDIM, SSD_STATE
    return ([r.at[:, pl.ds(W * gi, W)] for r in wide] + [r.at[:, pl.ds(N * gi, N)] for r in narrow]
            + [r.at[gi] for r in lead])


def _tri(Q):
    tri = jnp.tril(jnp.ones((Q, Q), F32))
    return tri, tri.T


def _ssd_fwd(xc, dtr8, bias8, alog8, dsk8):
    T, Q = xc.shape[0], SSD_Q
    NC = T // Q
    P, W = SSD_HEADDIM, SSD_HPG * SSD_HEADDIM
    x_spec, b_spec, c_spec, dt_spec, p_spec, tri_spec, h_spec = _ssd_specs(T, Q, False)

    def body(x_ref, b_ref, c_ref, dtr_ref, bias_ref, alog_ref, dsk_ref, tri_ref, triu_ref, y_ref, hs_ref, h_scr):
        for gi in range(SSD_STEP_GROUPS):
            xv, yv, bv, cv, dtv, biasv, alogv, dskv, hsv, hv = _ssd_views(
                gi, (x_ref, y_ref), (b_ref, c_ref), (dtr_ref, bias_ref, alog_ref, dsk_ref, hs_ref, h_scr))
            one(xv, bv, cv, dtv, biasv, alogv, dskv, tri_ref, triu_ref, yv, hsv, hv)

    def one(x_ref, b_ref, c_ref, dtr_ref, bias_ref, alog_ref, dsk_ref, tri_ref, triu_ref, y_ref, hs_ref, h_scr):
        @pl.when(pl.program_id(1) == 0)
        def _():
            h_scr[...] = jnp.zeros_like(h_scr)

        hin = h_scr[...]
        hs_ref[...] = hin
        x, Bm, Cm, pre, dt, A, ac4, ar4, cb, causal, atot4 = _ssd_common(
            x_ref, b_ref, c_ref, dtr_ref, bias_ref, alog_ref, tri_ref, triu_ref)
        xdt = x * _per_head(dt, W)
        rest = _dot(Cm, hin, NT) * _per_head(jnp.exp(ac4), W) + _per_head(dsk_ref[...], W) * x
        for k in range(SSD_HPG):
            blk = slice(P * k, P * (k + 1))
            G = cb * _ssd_decay(k, ac4, ar4, causal)
            y_ref[:, blk] = _dot(G, xdt[:, blk], NN) + rest[:, blk]
        xw = xdt * _per_head(jnp.exp(atot4 - ac4), W)
        h_scr[...] = _head_rows(jnp.exp(atot4)) * hin + _dot(xw, Bm, TN)

    tri, triu = _tri(Q)
    return pl.pallas_call(
        body, name="ssd_fwd", grid=(SSD_GROUPS // SSD_STEP_GROUPS, NC),
        in_specs=[x_spec, b_spec, c_spec, dt_spec, p_spec, p_spec, p_spec, tri_spec, tri_spec],
        out_specs=[pl.BlockSpec((Q, SSD_STEP_GROUPS * W), lambda g, c: (c, g)), h_spec],
        out_shape=[jax.ShapeDtypeStruct((T, D_INNER), F32),
                   jax.ShapeDtypeStruct((SSD_GROUPS, NC, W, SSD_STATE), F32)],
        scratch_shapes=[pltpu.VMEM((SSD_STEP_GROUPS, W, SSD_STATE), F32)],
        compiler_params=_params(("parallel", "arbitrary")),
    )(xc, xc, xc, dtr8, bias8, alog8, dsk8, tri, triu)


def _ssd_bwd(xc, dtr8, bias8, alog8, dsk8, hs, dy):
    T, Q = xc.shape[0], SSD_Q
    NC = T // Q
    P, W = SSD_HEADDIM, SSD_HPG * SSD_HEADDIM
    x_spec, b_spec, c_spec, dt_spec, p_spec, tri_spec, h_spec = _ssd_specs(T, Q, True)
    dy_spec = pl.BlockSpec((Q, SSD_STEP_GROUPS * W), lambda g, c: (NC - 1 - c, g))
    dbc_spec = pl.BlockSpec((Q, SSD_STEP_GROUPS * SSD_STATE), lambda g, c: (NC - 1 - c, g))

    def body(x_ref, b_ref, c_ref, dtr_ref, bias_ref, alog_ref, dsk_ref, tri_ref, triu_ref, hs_ref, dy_ref,
             dx_ref, db_ref, dc_ref, ddtr_ref, dbias_ref, dalog_ref, ddsk_ref, dh_scr):
        for gi in range(SSD_STEP_GROUPS):
            (xv, dyv, dxv, bv, cv, dbv, dcv, dtv, biasv, alogv, dskv, hsv, ddtv, dbiasv, dalogv, ddskv, dhv) = _ssd_views(
                gi, (x_ref, dy_ref, dx_ref), (b_ref, c_ref, db_ref, dc_ref),
                (dtr_ref, bias_ref, alog_ref, dsk_ref, hs_ref, ddtr_ref, dbias_ref, dalog_ref, ddsk_ref, dh_scr))
            one(xv, bv, cv, dtv, biasv, alogv, dskv, tri_ref, triu_ref, hsv, dyv,
                dxv, dbv, dcv, ddtv, dbiasv, dalogv, ddskv, dhv)

    def one(x_ref, b_ref, c_ref, dtr_ref, bias_ref, alog_ref, dsk_ref, tri_ref, triu_ref, hs_ref, dy_ref,
            dx_ref, db_ref, dc_ref, ddtr_ref, dbias_ref, dalog_ref, ddsk_ref, dh_scr):
        @pl.when(pl.program_id(1) == 0)
        def _():
            dh_scr[...] = jnp.zeros_like(dh_scr)
            dbias_ref[...] = jnp.zeros_like(dbias_ref)
            dalog_ref[...] = jnp.zeros_like(dalog_ref)
            ddsk_ref[...] = jnp.zeros_like(ddsk_ref)

        x, Bm, Cm, pre, dt, A, ac4, ar4, cb, causal, atot4 = _ssd_common(
            x_ref, b_ref, c_ref, dtr_ref, bias_ref, alog_ref, tri_ref, triu_ref)
        dyv, hin, dho = dy_ref[...], hs_ref[...], dh_scr[...]
        dt_w = _per_head(dt, W)
        xdt = x * dt_w
        E4, F4, etot4 = jnp.exp(ac4), jnp.exp(atot4 - ac4), jnp.exp(atot4)
        F_w = _per_head(F4, W)
        ddsk_ref[...] += _head_sums(_colsum(dyv * x))
        Z = _dot(Cm, hin, NT)
        dZ = dyv * _per_head(E4, W)
        dac4 = _head_sums(dyv * Z) * E4
        dC = _dot(dZ, hin, NN)
        dh_scr[...] = _dot(dZ, Cm, TN) + _head_rows(etot4) * dho
        per_row = jnp.sum(dho * hin, axis=1, keepdims=True)
        lane4 = lax.broadcasted_iota(jnp.int32, (1, SSD_HPG), 1)
        datot4 = jnp.zeros((1, SSD_HPG), F32)
        for k in range(SSD_HPG):
            datot4 = jnp.where(lane4 == k, jnp.sum(per_row[P * k:P * (k + 1)], keepdims=True), datot4)
        datot4 = datot4 * etot4
        dxw = _dot(Bm, dho, NT)
        dB = _dot(xdt * F_w, dho, NN)
        dFa4 = _head_sums(dxw * xdt) * F4
        datot4 = datot4 + _colsum(dFa4)
        dac4 = dac4 - dFa4
        dcb = jnp.zeros((Q, Q), F32)
        lane_q4 = lax.broadcasted_iota(jnp.int32, (Q, SSD_HPG), 1)
        sub_4q = lax.broadcasted_iota(jnp.int32, (SSD_HPG, Q), 0)
        row_sums = jnp.zeros((Q, SSD_HPG), F32)
        col_sums = jnp.zeros((SSD_HPG, Q), F32)
        dxdt_heads = []
        for k in range(SSD_HPG):
            blk = slice(P * k, P * (k + 1))
            L = _ssd_decay(k, ac4, ar4, causal)
            G = cb * L
            dG = _dot(dyv[:, blk], xdt[:, blk], NT)
            dxdt_heads.append(_dot(G, dyv[:, blk], TN))
            dcb = dcb + dG * L
            Mseg = dG * G
            row_sums = jnp.where(lane_q4 == k, jnp.sum(Mseg, axis=1, keepdims=True), row_sums)
            col_sums = jnp.where(sub_4q == k, jnp.sum(Mseg, axis=0, keepdims=True), col_sums)
        dxdt = dxw * F_w + jnp.concatenate(dxdt_heads, axis=1)
        last = lax.broadcasted_iota(jnp.int32, (Q, SSD_HPG), 0) == Q - 1
        dac4 = dac4 + row_sums + jnp.where(last, datot4, 0.0)
        dadt4 = _dot(triu_ref[...], dac4, NN, HI) - _dot(triu_ref[...], col_sums, NT, HI)
        ddt4 = _head_sums(dxdt * x) + dadt4 * A
        dalog_ref[...] += _colsum(dadt4 * dt) * A
        ddtr4 = ddt4 * jax.nn.sigmoid(pre)
        dbias_ref[...] += _colsum(ddtr4)
        ddtr_ref[...] = ddtr4
        dx_ref[...] = (_per_head(dsk_ref[...], W) * dyv + dxdt * dt_w).astype(dx_ref.dtype)
        db_ref[...] = dB + _dot(dcb, Cm, TN)
        dc_ref[...] = dC + _dot(dcb, Bm, NN)

    tri, triu = _tri(Q)
    return pl.pallas_call(
        body, name="ssd_bwd", grid=(SSD_GROUPS // SSD_STEP_GROUPS, NC),
        in_specs=[x_spec, b_spec, c_spec, dt_spec, p_spec, p_spec, p_spec, tri_spec, tri_spec, h_spec, dy_spec],
        out_specs=[dy_spec, dbc_spec, dbc_spec, dt_spec, p_spec, p_spec, p_spec],
        out_shape=[jax.ShapeDtypeStruct((T, D_INNER), F32),
                   jax.ShapeDtypeStruct((T, SSD_GROUPS * SSD_STATE), F32),
                   jax.ShapeDtypeStruct((T, SSD_GROUPS * SSD_STATE), F32),
                   jax.ShapeDtypeStruct(dtr8.shape, F32)] + [jax.ShapeDtypeStruct(bias8.shape, F32)] * 3,
        scratch_shapes=[pltpu.VMEM((SSD_STEP_GROUPS, W, SSD_STATE), F32)],
        compiler_params=_params(("parallel", "arbitrary")),
    )(xc, xc, xc, dtr8, bias8, alog8, dsk8, tri, triu, hs, dy)


def _to_groups(v):
    return v.reshape(SSD_GROUPS, 1, SSD_HPG)


def _mixer_fwd(x, p, w):
    T, D = x.shape
    (hn,) = _rowwise(lambda xt, g: (_rms(xt, g),), [x], [p["mix_norm"]], [(D, BF16)], name="mix_norm")
    winT = w["w_in"]
    u, z, xbc, dtr, gl = [
        _matmul(hn, winT[IN_OFFS[i]:IN_OFFS[i + 1]], "nt", name=f"mix_in{i}") for i in range(5)]
    tiles = _s5_derive(*[p[k][0] for k in ("s5_A_re", "s5_A_im", "s5_log_dt", "s5_B_re", "s5_B_im", "s5_C_re", "s5_C_im")])
    u = _to_seg(u)
    ys = _s5_fwd(u, tiles)
    (g,) = _rowwise(lambda yt, ut, d: (jax.nn.gelu(yt + d * ut),), [ys, u], [p["s5_D"]], [(D, F32)], name="s5_gelu")
    y5 = _matmul(g, w["s5_w_glu"], "nn", name="s5_glu", out=(BF16,),
                 epilogue=lambda acc, gt, b: (gt * jax.nn.sigmoid(acc + b),), extras=[g, p["s5_b_glu"]])
    xc = _conv_fwd(xbc, w["conv_w"], p["conv_b"])
    dtr8 = dtr.reshape(T, SSD_GROUPS, SSD_HPG).transpose(1, 0, 2)
    ssd_p = [_to_groups(p[k]) for k in ("ssd_dt_bias", "ssd_A_log", "ssd_D")]
    yssd_raw, hs = _ssd_fwd(xc, dtr8, *ssd_p)
    (yssd,) = _rowwise(lambda yt, zt, nw: (_rms(yt * jax.nn.silu(zt), nw),), [yssd_raw, z], [p["ssd_norm"]],
                       [(D_INNER, BF16)], name="ssd_gate")
    p5 = _from_seg(_matmul(y5, w["w_proj_s5"], "nn", name="mix_p5"))
    pssd = _matmul(yssd, w["w_proj_ssd"], "nn", name="mix_pssd")

    def merge(glt, at, bt, bg):
        gates = jax.nn.sigmoid(glt + bg)
        return (gates[:, :D] * at + gates[:, D:] * bt,)

    (merged,) = _rowwise(merge, [gl, p5, pssd], [p["b_gate"]], [(D, BF16)], name="mix_merge")
    y = _matmul(merged, w["w_out"], "nn", name="mix_out", epilogue=lambda acc, xt: (xt + acc,), extras=[x])
    res = dict(hn=hn, u=u, z=z, xbc=xbc, gl=gl, tiles=tiles, ys=ys, g=g, y5=y5, xc=xc, dtr8=dtr8, ssd_p=ssd_p,
               yssd_raw=yssd_raw, hs=hs, yssd=yssd, p5=p5, pssd=pssd, merged=merged)
    return y, res


def _mixer_bwd(dy, dy16, x, p, w, r):
    T, D = x.shape
    gp = {}
    dmerged = _matmul(dy16, w["w_out"], "nt", name="mix_d_merged")
    gp["w_out"] = _matmul(r["merged"], dy16, "tn", name="mix_d_wout")

    def merge_bwd(glt, at, bt, dm, bg):
        def f(q, a_, b_):
            gates = jax.nn.sigmoid(q + bg)
            return gates[:, :D] * a_ + gates[:, D:] * b_
        _, vjp = jax.vjp(f, glt, at, bt)
        dq, da_, db_ = vjp(dm)
        return da_, db_, dq, _colsum(dq)

    dp5, dpssd, dgl, gp["b_gate"] = _rowwise(
        merge_bwd, [r["gl"], r["p5"], r["pssd"], dmerged], [p["b_gate"]],
        [(D, BF16), (D, BF16), (2 * D, BF16)], [((1, 2 * D), F32)], name="mix_d_merge")
    dyssd = _matmul(dpssd, w["w_proj_ssd"], "nt", name="mix_d_yssd")
    gp["w_proj_ssd"] = _matmul(r["yssd"], dpssd, "tn", name="mix_d_wpssd")

    def gate_bwd(yt, zt, dyt, nw):
        _, vjp = jax.vjp(lambda a_, b_, c_: _rms(a_ * jax.nn.silu(b_), c_), yt, zt, nw)
        return vjp(dyt)

    dyraw, dz, gp["ssd_norm"] = _rowwise(
        gate_bwd, [r["yssd_raw"], r["z"], dyssd], [p["ssd_norm"]],
        [(D_INNER, F32), (D_INNER, BF16)], [((1, D_INNER), F32)], name="ssd_d_gate")
    dxs, dBm, dCm, ddtr8, dbias8, dalog8, ddsk8 = _ssd_bwd(r["xc"], r["dtr8"], *r["ssd_p"], r["hs"], dyraw)
    gp["ssd_dt_bias"], gp["ssd_A_log"], gp["ssd_D"] = [v.reshape(1, SSD_HEADS) for v in (dbias8, dalog8, ddsk8)]
    ddtr = ddtr8.transpose(1, 0, 2).reshape(T, SSD_HEADS)
    conv = [_conv_bwd(r["xbc"], w["conv_w"], p["conv_b"], d, c0, f"conv_bwd{i}")
            for i, (d, c0) in enumerate(((dxs, 0), (dBm, D_INNER), (dCm, D_INNER + SSD_GROUPS * SSD_STATE)))]
    dxbc = [c[0] for c in conv]
    gp["conv_w"] = jnp.concatenate([c[1] for c in conv], axis=1)
    gp["conv_b"] = jnp.concatenate([c[2] for c in conv], axis=1)

    dp5 = _to_seg(dp5)
    dy5 = _matmul(dp5, w["w_proj_s5"], "nt", name="mix_d_y5")
    gp["w_proj_s5"] = _matmul(r["y5"], dp5, "tn", name="mix_d_wp5")
    g = r["g"]

    def glu_ep(acc, gt, b, dyt):
        _, vjp = jax.vjp(lambda g_, t_: g_ * jax.nn.sigmoid(t_ + b), gt, acc)
        return vjp(dyt)

    dg1, dt_ = _matmul(g, w["s5_w_glu"], "nn", name="s5_d_glu", out=(F32, BF16), epilogue=glu_ep,
                       extras=[g, p["s5_b_glu"], dy5])
    gp["s5_w_glu"] = _matmul(g, dt_, "tn", name="s5_d_wglu")
    dg = _matmul(dt_, w["s5_w_glu"], "nt", name="s5_d_g", epilogue=lambda acc, e: (acc + e,), extras=[dg1])

    def gelu_bwd(yt, ut, dgt, dtt, d):
        _, vjp = jax.vjp(lambda y_, d_: jax.nn.gelu(y_ + d_ * ut), yt, d)
        dys, dd = vjp(dgt)
        return dys, dys * d, dd, _colsum(dtt.astype(F32))

    dys, dusk, gp["s5_D"], gp["s5_b_glu"] = _rowwise(
        gelu_bwd, [r["ys"], r["u"], dg, dt_], [p["s5_D"]], [(D, F32), (D, F32)], [((1, D), F32), ((1, D), F32)],
        name="s5_d_gelu")
    du, *dtiles = _s5_bwd(r["u"], dys, dusk, r["tiles"])
    du = _from_seg(du)

    winT = w["w_in"]
    hn = r["hn"]
    d_rows, dhn = [], None
    cols = [(du, 0, 1024), (dz, 1024, 3072), (dxbc[0], 3072, 5120), (dxbc[1], 5120, 6144), (dxbc[2], 6144, 7168),
            (ddtr, 7168, 7200), (dgl, 7200, 9248)]
    for i, (d, lo, hi) in enumerate(cols):
        d_rows.append(_matmul(d, hn, "tn", name=f"mix_d_win{i}"))
        if dhn is None:
            dhn = _matmul(d, winT[lo:hi], "nn", name=f"mix_d_hn{i}")
        else:
            dhn = _matmul(d, winT[lo:hi], "nn", name=f"mix_d_hn{i}", epilogue=lambda acc, e: (acc + e,), extras=[dhn])
    gp["w_in"] = jnp.concatenate(d_rows, axis=0)

    def norm_bwd(xt, dh, dyt, gn):
        _, vjp = jax.vjp(_rms, xt, gn)
        dx, dgn = vjp(dh)
        return dyt + dx, dyt + dx, dgn

    dx, dx16, gp["mix_norm"] = _rowwise(norm_bwd, [x, dhn, dy], [p["mix_norm"]], [(D, F32), (D, BF16)],
                                        [((1, D), F32)], name="mix_d_norm")
    return dx, dx16, gp, dtiles


def _s5_param_grads(p, dtiles):
    keys = ("s5_A_re", "s5_A_im", "s5_log_dt", "s5_B_re", "s5_B_im", "s5_C_re", "s5_C_im")
    _, vjp = jax.vjp(_s5_derive, *[p[k][0] for k in keys])
    return {k: v[None] for k, v in zip(keys, vjp(tuple(dtiles)))}


def _local_step(x, target, p, w, late_weights, exchange):
    T, D = x.shape
    w = dict(w)
    x1, r1 = _ffn_fwd(x, p["ffn1_norm"], w["ffn1_w_gate"], w["ffn1_w_up"], w["ffn1_w_down"], "ffn1",
                      on_gate=lambda a_: w.update(late_weights("mixer", a_)))
    x2, rm = _mixer_fwd(x1, p, w)
    w.update(late_weights("ffn2", rm["p5"]))
    x3, r2 = _ffn_fwd(x2, p["ffn2_norm"], w["ffn2_w_gate"], w["ffn2_w_up"], w["ffn2_w_down"], "ffn2")

    def head(xt, tt, g):
        def f(x_, g_):
            e = _rms(x_, g_) - tt
            return 0.5 * jnp.sum(jnp.mean(e * e, axis=-1))
        l, (dx_, dg_) = jax.value_and_grad(f, argnums=(0, 1))(xt, g)
        return dx_, dx_, l.reshape(1, 1), dg_

    dx3, dx3_16, loss, d_final = _rowwise(head, [x3, target], [p["final_norm"]], [(D, F32), (D, BF16)],
                                          [((1, 1), F32), ((1, D), F32)], name="loss_head")
    gp = {"final_norm": d_final}

    def ffn_exchange(tag):
        return lambda d_wgT, d_wuT, d_wd, after: exchange(
            tag, {f"{tag}_w_gate": d_wgT, f"{tag}_w_up": d_wuT, f"{tag}_w_down": d_wd}, after)

    dx2, dx2_16, gp["ffn2_norm"] = _ffn_bwd(
        dx3, dx3_16, x2, p["ffn2_norm"], w["ffn2_w_gate"], w["ffn2_w_up"], w["ffn2_w_down"], r2, "ffn2", (),
        ffn_exchange("ffn2"))
    dx1, dx1_16, gm, dtiles = _mixer_bwd(dx2, dx2_16, x1, p, w, rm)
    conv_w_grad = gm.pop("conv_w")
    started = exchange("mixer", {k: gm.pop(k) for k in ("w_out", "w_proj_s5", "w_proj_ssd", "s5_w_glu", "w_in")}, dx1)
    gp.update(gm)
    gp.update(_s5_param_grads(p, dtiles))
    dx0, _, gp["ffn1_norm"] = _ffn_bwd(
        dx1, dx1_16, x, p["ffn1_norm"], w["ffn1_w_gate"], w["ffn1_w_up"], w["ffn1_w_down"], r1, "ffn1", started,
        ffn_exchange("ffn1"))
    return loss, dx0, gp, conv_w_grad


ADAMW_BLOCK = 1 << 19


def _adamw(w, m, v, gs, *, name, emit_g):
    _, R, C = w.shape
    n_g = len(gs)
    n_o = 4 if emit_g else 3
    tm = _rtile(R, max(8, ADAMW_BLOCK // C))
    lead = pl.BlockSpec((None, tm, C), lambda i: (0, i, 0))

    def body(w_ref, m_ref, v_ref, *refs):
        g = refs[0][...]
        for r in refs[1:n_g]:
            g = g + r[...]
        m2 = ADAM_B1 * m_ref[...] + (1.0 - ADAM_B1) * g
        v2 = ADAM_B2 * v_ref[...] + (1.0 - ADAM_B2) * (g * g)
        m_hat = m2 / (1.0 - ADAM_B1 ** ADAM_STEP)
        v_hat = v2 / (1.0 - ADAM_B2 ** ADAM_STEP)
        delta = -ADAM_LR * (m_hat / (jnp.sqrt(v_hat) + ADAM_EPS) + ADAM_WD * w_ref[...])
        for r, val in zip(refs[n_g:], ((g,) if emit_g else ()) + (delta, m2, v2)):
            r[...] = val

    return pl.pallas_call(
        body, name=name, grid=(R // tm,),
        in_specs=[lead] * 3 + [pl.BlockSpec((tm, C), lambda i: (i, 0))] * n_g, out_specs=[lead] * n_o,
        out_shape=[jax.ShapeDtypeStruct((1, R, C), F32)] * n_o,
        compiler_params=_params(("parallel",)),
    )(w, m, v, *gs)


def _rtile(r, cap=512):
    if r <= cap:
        return r
    best = None
    for t in range(8, cap + 1, 8):
        if r % t == 0:
            best = t
    assert best is not None, r
    return best


ANY = pl.BlockSpec(memory_space=pl.ANY)


def _where_am_i():
    x, y, c = lax.axis_index("x"), lax.axis_index("y"), lax.axis_index("c")
    ks = (2 * x + y, 2 * (1 - x) + y, 2 * x + (1 - y), 2 * (1 - x) + (1 - y))
    return x, y, c, ks


def _rc(src, dst, ssem, rsem, to):
    return pltpu.make_async_remote_copy(src_ref=src, dst_ref=dst, send_sem=ssem, recv_sem=rsem,
                                        device_id=to, device_id_type=MESH)


def _gather_shards(srcs):
    n = len(srcs)

    def body(*refs):
        _gather_body(refs[:n], refs[n:2 * n], *refs[2 * n:], extra=())

    return pl.pallas_call(
        body, name="gather_shards",
        in_specs=[ANY] * n, out_specs=[ANY] * n,
        out_shape=[jax.ShapeDtypeStruct((4,) + s.shape, s.dtype) for s in srcs],
        scratch_shapes=[pltpu.SemaphoreType.DMA((n, 6)), pltpu.SemaphoreType.DMA((n, 6)), pltpu.SemaphoreType.DMA((n,))],
        compiler_params=pltpu.CompilerParams(has_side_effects=True),
    )(*srcs)


def _handshake_all():
    x, y, c = lax.axis_index("x"), lax.axis_index("y"), lax.axis_index("c")
    barrier = pltpu.get_barrier_semaphore()
    for dx, dy, dc in itertools.product((0, 1), repeat=3):
        if (dx, dy, dc) != (0, 0, 0):
            to = (1 - x if dx else x, 1 - y if dy else y, 1 - c if dc else c)
            pl.semaphore_signal(barrier, inc=1, device_id=to, device_id_type=MESH)
    pl.semaphore_wait(barrier, 7)


SEQUENCER = dict(axis_name="seq", num_cores=1)


def _gather_shards_async(srcs, token, name, collective_id):
    n = len(srcs)

    def body(*refs):
        tok, tok_out = refs[n], refs[2 * n + 1]
        ssem, rsem, lsem = refs[2 * n + 2:]
        _handshake_all()
        _gather_body(refs[:n], refs[n + 1:2 * n + 1], ssem, rsem, lsem,
                     extra=(pltpu.make_async_copy(tok, tok_out, lsem.at[n]),))

    res = pl.kernel(
        body, name=name,
        out_type=[jax.ShapeDtypeStruct((4,) + s.shape, s.dtype) for s in srcs]
        + [jax.ShapeDtypeStruct(token.shape, token.dtype)],
        mesh=plsc.ScalarSubcoreMesh(**SEQUENCER),
        scratch_types=[pltpu.SemaphoreType.DMA((n, 6)), pltpu.SemaphoreType.DMA((n, 6)), pltpu.SemaphoreType.DMA((n + 1,))],
        compiler_params=pltpu.CompilerParams(collective_id=collective_id),
    )(*srcs, token)
    return res[:n], res[n]


def _gather_body(src, out, ssem, rsem, lsem, extra):
        n = len(src)
        x, y, c, (k_me, k_x, k_y, k_d) = _where_am_i()
        sib = (x, y, 1 - c)
        local = [pltpu.make_async_copy(src[i], out[i].at[k_me], lsem.at[i]) for i in range(n)] + list(extra)
        for cp in local:
            cp.start()

        def own(i, q, to):
            return _rc(src[i], out[i].at[k_me], ssem.at[i, q], rsem.at[i, q], to)

        def slab(i, q, k, to):
            return _rc(out[i].at[k], out[i].at[k], ssem.at[i, q], rsem.at[i, q], to)

        @pl.when(c == 1)
        def _():
            sends = [own(i, 0, (1 - x, y, 1)) for i in range(n)]
            for cp in sends:
                cp.start()
            fwd = [slab(i, 3, k_x, sib) for i in range(n)]
            for i in range(n):
                slab(i, 0, k_x, sib).wait_recv()
                fwd[i].start()
            for i in range(n):
                slab(i, 4, k_y, sib).wait_recv()
                slab(i, 5, k_d, sib).wait_recv()
            for cp in sends + fwd:
                cp.wait_send()

        @pl.when(c == 0)
        def _():
            sends = [own(i, 1, (x, 1 - y, 0)) for i in range(n)] + [own(i, 2, (1 - x, 1 - y, 0)) for i in range(n)]
            for cp in sends:
                cp.start()
            fwd = [slab(i, 4, k_y, sib) for i in range(n)] + [slab(i, 5, k_d, sib) for i in range(n)]
            for i in range(n):
                slab(i, 1, k_y, sib).wait_recv()
                fwd[i].start()
            for i in range(n):
                slab(i, 2, k_d, sib).wait_recv()
                fwd[n + i].start()
            for i in range(n):
                slab(i, 3, k_x, sib).wait_recv()
            for cp in sends + fwd:
                cp.wait_send()

        for cp in local:
            cp.wait()


def _swap_slabs(arrs, n_slabs, name, async_id, token):
    n = len(arrs)
    J = max(n_slabs, 1)

    def body(*refs):
        src, out = refs[:n], refs[n:2 * n]
        ssem, rsem = refs[2 * n:]
        x, y, c, (k_me, k_x, k_y, k_d) = _where_am_i()
        sib = (x, y, 1 - c)
        sel = (jnp.where(c == 1, k_y, k_me), jnp.where(c == 1, k_d, k_x))
        cps = []
        for i in range(n):
            for j in range(J):
                s = src[i].at[sel[j]] if n_slabs else src[i]
                d = out[i].at[j] if n_slabs else out[i]
                cps.append(_rc(s, d, ssem.at[i, j], rsem.at[i, j], sib))
        for cp in cps:
            cp.start()
        for cp in cps:
            cp.wait()

    return _exchange_call(
        body, arrs, [jax.ShapeDtypeStruct(((n_slabs,) if n_slabs else ()) + a.shape[-2:], a.dtype) for a in arrs],
        [pltpu.SemaphoreType.DMA((n, J)), pltpu.SemaphoreType.DMA((n, J))], name, async_id, token)


def _exchange_call(body, arrs, out_shape, sems, name, collective_id, token):
    n, m = len(arrs), len(out_shape)

    def seq_body(*refs):
        tok, tok_out, tok_sem = refs[n], refs[n + 1 + m], refs[-1]
        _handshake_all()
        cp = pltpu.make_async_copy(tok, tok_out, tok_sem)
        cp.start()
        body(*refs[:n], *refs[n + 1:n + 1 + m], *refs[n + 2 + m:-1])
        cp.wait()

    res = pl.kernel(
        seq_body, name=name, out_type=list(out_shape) + [jax.ShapeDtypeStruct(token.shape, token.dtype)],
        mesh=plsc.ScalarSubcoreMesh(**SEQUENCER), scratch_types=list(sems) + [pltpu.SemaphoreType.DMA],
        compiler_params=pltpu.CompilerParams(collective_id=collective_id))(*arrs, token)
    return res[:m], res[m]


def _send_chip_sums(arrs, name, async_id, token):
    n = len(arrs)

    def body(*refs):
        src, out = refs[:n], refs[n:2 * n]
        ssem, rsem = refs[2 * n:]
        x, y, c, _ = _where_am_i()

        @pl.when(c == 1)
        def _():
            cps = [_rc(src[i].at[1], out[i].at[0], ssem.at[i, 0], rsem.at[i, 0], (1 - x, y, 1)) for i in range(n)]
            for cp in cps:
                cp.start()
            for cp in cps:
                cp.wait()

        @pl.when(c == 0)
        def _():
            cps = [_rc(src[i].at[0], out[i].at[0], ssem.at[i, 0], rsem.at[i, 0], (x, 1 - y, 0)) for i in range(n)]
            cps += [_rc(src[i].at[1], out[i].at[1], ssem.at[i, 1], rsem.at[i, 1], (1 - x, 1 - y, 0)) for i in range(n)]
            for cp in cps:
                cp.start()
            for cp in cps:
                cp.wait()

    return _exchange_call(
        body, arrs, [jax.ShapeDtypeStruct(a.shape, a.dtype) for a in arrs],
        [pltpu.SemaphoreType.DMA((n, 2)), pltpu.SemaphoreType.DMA((n, 2))], name, async_id, token)


def _chip_sum(g, recv, sel, name):
    _, r, C = g.shape
    tr = _rtile(r)

    def body(sel_ref, g_ref, r_ref, o32_ref, o16_ref):
        s = g_ref[...] + r_ref[...]
        o32_ref[...] = s
        o16_ref[...] = s.astype(BF16)

    blk = pl.BlockSpec((None, tr, C), lambda j, t, sel_ref: (j, t, 0))
    return pl.pallas_call(
        body, name=name,
        grid_spec=pltpu.PrefetchScalarGridSpec(
            num_scalar_prefetch=1, grid=(2, r // tr),
            in_specs=[pl.BlockSpec((None, tr, C), lambda j, t, sel_ref: (sel_ref[j], t, 0)), blk],
            out_specs=[blk, blk]),
        out_shape=[jax.ShapeDtypeStruct((2, r, C), F32), jax.ShapeDtypeStruct((2, r, C), BF16)],
        compiler_params=_params(("arbitrary", "arbitrary")),
    )(sel, g, recv)


def _cross_sum(s32, got, north, after, name):
    _, r, C = s32.shape
    tr = _rtile(r)

    def body(f_ref, p_ref, a_ref, b_ref, after_ref, o_ref):
        a = a_ref[...].astype(F32)

        @pl.when(f_ref[0] == 1)
        def _():
            o_ref[...] = p_ref[...] + a

        @pl.when(f_ref[0] == 0)
        def _():
            o_ref[...] = a + b_ref[...].astype(F32)

    return pl.pallas_call(
        body, name=name,
        grid_spec=pltpu.PrefetchScalarGridSpec(
            num_scalar_prefetch=1, grid=(r // tr,),
            in_specs=[pl.BlockSpec((None, tr, C), lambda t, f: (0, t, 0)),
                      pl.BlockSpec((None, tr, C), lambda t, f: (0, t, 0)),
                      pl.BlockSpec((None, tr, C), lambda t, f: (1 - f[0], t, 0)), ANY],
            out_specs=pl.BlockSpec((tr, C), lambda t, f: (t, 0))),
        out_shape=jax.ShapeDtypeStruct((r, C), F32),
        compiler_params=_params(("arbitrary",)),
    )(north, s32, got, got, after)


class _Reducer:
    def __init__(self, token):
        self.token, self.pending, self.pairs = token, None, {}

    def begin(self, tag, ids, grads, after):
        names = list(grads)
        arrs = [grads[n].reshape(4, -1, D_MODEL) for n in names]
        x, y, c, (k_me, k_x, k_y, k_d) = _where_am_i()
        sel = jnp.stack([jnp.where(c == 1, k_me, k_y), jnp.where(c == 1, k_x, k_d)]).astype(jnp.int32)
        from_sib, self.token = _swap_slabs(arrs, 2, f"swap_grad_slabs_{tag}", ids[0], self.token)
        sums = [_chip_sum(g, r, sel, f"chip_sum_{tag}{i}") for i, (g, r) in enumerate(zip(arrs, from_sib))]
        self.finish(after)
        got, self.token = _send_chip_sums([s16 for _, s16 in sums], f"send_chip_sums_{tag}", ids[1], self.token)
        self.pending = (tag, ids[2], names, sums, got)
        return tuple(s16 for _, s16 in sums)

    def finish(self, after):
        if self.pending is None:
            return
        tag, cid, names, sums, got = self.pending
        north = jnp.reshape(lax.axis_index("c"), (1,)).astype(jnp.int32)
        parts = [_cross_sum(s32, b, north, after, f"cross_sum_{tag}{i}") for i, ((s32, _), b) in enumerate(zip(sums, got))]
        theirs, self.token = _swap_slabs(parts, 0, f"swap_parts_{tag}", cid, self.token)
        self.pairs.update(zip(names, zip(parts, theirs)))
        self.pending = None


def _all_reduce_small(v, token, collective_id):
    R, C = v.shape
    tr = R

    def body(v_ref, buf, ssem, rsem, lsem):
        x, y, c = lax.axis_index("x"), lax.axis_index("y"), lax.axis_index("c")
        me, sib = (x, y, c), (x, y, 1 - c)
        chips = [(1 - x, y), (x, 1 - y), (1 - x, 1 - y)]

        def slot(px, py, pc):
            return buf.at[4 * px + 2 * py + pc]

        def copy(k, block, to, src=None):
            return _rc(slot(*block) if src is None else src, slot(*block), ssem.at[k], rsem.at[k], to)

        mine = pltpu.make_async_copy(v_ref, slot(*me), lsem)
        mine.start()
        first = [copy(0, me, sib, src=v_ref)] + [copy(1 + j, me, (*chip, c), src=v_ref) for j, chip in enumerate(chips)]
        for cp in first:
            cp.start()
        passed = [copy(4 + j, (*chip, c), sib) for j, chip in enumerate(chips)]
        for j, chip in enumerate(chips):
            copy(1 + j, (*chip, c), me).wait_recv()
            passed[j].start()
        copy(0, sib, me).wait_recv()
        for j, chip in enumerate(chips):
            copy(4 + j, (*chip, 1 - c), me).wait_recv()
        for cp in first + passed:
            cp.wait_send()
        mine.wait()

    (gathered,), token = _exchange_call(
        body, [v], [jax.ShapeDtypeStruct((8, R, C), F32)],
        [pltpu.SemaphoreType.DMA((7,)), pltpu.SemaphoreType.DMA((7,)), pltpu.SemaphoreType.DMA],
        "gather_small", collective_id, token)

    def add(g_ref, o_ref):
        acc = g_ref[0]
        for k in range(1, 8):
            acc = acc + g_ref[k]
        o_ref[...] = acc

    total = pl.pallas_call(
        add, name="sum_small", grid=(R // tr,),
        in_specs=[pl.BlockSpec((8, tr, C), lambda i: (0, i, 0))], out_specs=pl.BlockSpec((tr, C), lambda i: (i, 0)),
        out_shape=jax.ShapeDtypeStruct((R, C), F32), compiler_params=_params(("parallel",)),
    )(gathered)
    return total, token


WEIGHTS = ['ffn1_norm', 'ffn1_w_gate', 'ffn1_w_up', 'ffn1_w_down', 'mix_norm', 'w_in', 'conv_w', 'conv_b', 's5_A_re',
           's5_A_im', 's5_log_dt', 's5_B_re', 's5_B_im', 's5_C_re', 's5_C_im', 's5_D', 's5_w_glu', 's5_b_glu',
           'ssd_A_log', 'ssd_dt_bias', 'ssd_D', 'ssd_norm', 'w_proj_s5', 'w_proj_ssd', 'b_gate', 'w_out', 'ffn2_norm',
           'ffn2_w_gate', 'ffn2_w_up', 'ffn2_w_down', 'final_norm']
ARGS = ['x'] + WEIGHTS + ['loss_target'] + ['m_' + n for n in WEIGHTS] + ['v_' + n for n in WEIGHTS]
COL_SHARDED = ('ffn1_w_gate', 'ffn1_w_up', 'w_in', 'ffn2_w_gate', 'ffn2_w_up')
ROW_SHARDED = ('ffn1_w_down', 's5_w_glu', 'w_proj_s5', 'w_proj_ssd', 'w_out', 'ffn2_w_down')
MATRICES = COL_SHARDED + ROW_SHARDED
GATHER_FIRST = ('ffn1_w_gate', 'ffn1_w_up')
GATHER_FIRST_LATER = ('ffn1_w_down',)
GATHER_SECOND = ('w_in', 'conv_w', 's5_w_glu', 'w_proj_s5', 'w_proj_ssd', 'w_out')
GATHER_THIRD = ('ffn2_w_gate', 'ffn2_w_up', 'ffn2_w_down')
SMALL = [n for n in WEIGHTS if n not in MATRICES]


def _pack(arrs, width=1024):
    flat = jnp.concatenate([a.reshape(-1).astype(F32) for a in arrs])
    rows = -(-flat.shape[0] // (8 * width)) * 8
    return jnp.pad(flat, (0, rows * width - flat.shape[0])).reshape(rows, width)


def _unpack(packed, shapes):
    flat, out, o = packed.reshape(-1), [], 0
    for s in shapes:
        n = math.prod(s)
        out.append(flat[o:o + n].reshape(s))
        o += n
    return out


def kernel(x, ffn1_norm, ffn1_w_gate, ffn1_w_up, ffn1_w_down, mix_norm, w_in, conv_w, conv_b, s5_A_re, s5_A_im, s5_log_dt, s5_B_re, s5_B_im, s5_C_re, s5_C_im, s5_D, s5_w_glu, s5_b_glu, ssd_A_log, ssd_dt_bias, ssd_D, ssd_norm, w_proj_s5, w_proj_ssd, b_gate, w_out, ffn2_norm, ffn2_w_gate, ffn2_w_up, ffn2_w_down, final_norm, loss_target, m_ffn1_norm, m_ffn1_w_gate, m_ffn1_w_up, m_ffn1_w_down, m_mix_norm, m_w_in, m_conv_w, m_conv_b, m_s5_A_re, m_s5_A_im, m_s5_log_dt, m_s5_B_re, m_s5_B_im, m_s5_C_re, m_s5_C_im, m_s5_D, m_s5_w_glu, m_s5_b_glu, m_ssd_A_log, m_ssd_dt_bias, m_ssd_D, m_ssd_norm, m_w_proj_s5, m_w_proj_ssd, m_b_gate, m_w_out, m_ffn2_norm, m_ffn2_w_gate, m_ffn2_w_up, m_ffn2_w_down, m_final_norm, v_ffn1_norm, v_ffn1_w_gate, v_ffn1_w_up, v_ffn1_w_down, v_mix_norm, v_w_in, v_conv_w, v_conv_b, v_s5_A_re, v_s5_A_im, v_s5_log_dt, v_s5_B_re, v_s5_B_im, v_s5_C_re, v_s5_C_im, v_s5_D, v_s5_w_glu, v_s5_b_glu, v_ssd_A_log, v_ssd_dt_bias, v_ssd_D, v_ssd_norm, v_w_proj_s5, v_w_proj_ssd, v_b_gate, v_w_out, v_ffn2_norm, v_ffn2_w_gate, v_ffn2_w_up, v_ffn2_w_down, v_final_norm):
    a = dict(locals())
    assert list(a) == ARGS
    x, target = a['x'][0], a['loss_target'][0]
    k_me = 2 * lax.axis_index("x") + lax.axis_index("y")

    src = {n: a[n][0].T.astype(BF16) for n in COL_SHARDED}
    src.update({n: a[n][0].astype(BF16) for n in ROW_SHARDED})
    src['conv_w'] = a['conv_w'][0]
    def flat(f):
        return f.reshape(-1, f.shape[-1])

    full = _gather_shards([src[n] for n in GATHER_FIRST])
    w = {n: flat(f) for n, f in zip(GATHER_FIRST, full)}
    got, token = _gather_shards_async(
        [src[n] for n in GATHER_FIRST_LATER], full[0][0, :8, :128].astype(F32), "gather_ffn1_down", 14)
    w.update({n: flat(f) for n, f in zip(GATHER_FIRST_LATER, got)})
    reducer = _Reducer(token)

    def gather_later(stage, after):
        token = reducer.token + after[:8, :128].astype(F32)
        calls = {"mixer": ((GATHER_SECOND[:2], "gather_mixer_in", 1), (GATHER_SECOND[2:], "gather_mixer", 12)),
                 "ffn2": ((GATHER_THIRD, "gather_ffn2", 2),)}[stage]
        out = {}
        for names, name, cid in calls:
            got, token = _gather_shards_async([src[n] for n in names], token, name, cid)
            out.update({n: flat(f) for n, f in zip(names, got)})
        reducer.token = token
        if 'conv_w' in out:
            out['conv_w'] = out['conv_w'].reshape(4, CONV_K, -1).transpose(1, 0, 2).reshape(CONV_K, CONV_DIM)
        return out

    p = {n: a[n] for n in SMALL if n != 'conv_w'}
    p['final_norm'] = a['final_norm'][None]
    ids = {"ffn2": (3, 4, 5), "mixer": (6, 7, 8), "ffn1": (9, 10, 11)}
    loss, grad_x, gp, conv_w_grad = _local_step(
        x, target, p, w, gather_later, lambda tag, grads, after: reducer.begin(tag, ids[tag], grads, after))
    gp['conv_w'] = conv_w_grad[None]
    small_shapes = [(1, 1)] + [gp[n].shape if n != 'final_norm' else (1, D_MODEL) for n in SMALL]
    summed, reducer.token = _all_reduce_small(_pack([loss] + [gp[n] for n in SMALL]), reducer.token, 13)
    red = _unpack(summed, small_shapes)
    loss_all = red[0].reshape(())
    gsmall = dict(zip(SMALL, red[1:]))
    gsmall['conv_w'] = lax.dynamic_slice_in_dim(gsmall['conv_w'], k_me * 1024, 1024, axis=2)
    gsmall = {n: g.reshape(a[n].shape) for n, g in gsmall.items()}

    grads, delta, new_m, new_v = {}, {}, {}, {}

    def update(names):
        for n in names:
            turn = (lambda t: jnp.swapaxes(t, 1, 2)) if n in COL_SHARDED else (lambda t: t)
            out = _adamw(turn(a[n]), turn(a['m_' + n]), turn(a['v_' + n]), list(reducer.pairs[n]),
                         name=f"adamw_{n}", emit_g=True)
            grads[n], delta[n], new_m[n], new_v[n] = [turn(t) for t in out]

    update([n for n in MATRICES if n not in GATHER_FIRST + GATHER_FIRST_LATER])
    reducer.finish(delta['w_in'])
    update(GATHER_FIRST + GATHER_FIRST_LATER)
    sw, sm, sv, sg = [_pack([t[n] for n in SMALL])[None] for t in (a, {n: a['m_' + n] for n in SMALL},
                                                                   {n: a['v_' + n] for n in SMALL}, gsmall)]
    d, m2, v2 = _adamw(sw, sm, sv, [sg[0]], name="adamw_small", emit_g=False)
    shapes = [a[n].shape for n in SMALL]
    for n, dd, mm, vv in zip(SMALL, _unpack(d[0], shapes), _unpack(m2[0], shapes), _unpack(v2[0], shapes)):
        grads[n], delta[n], new_m[n], new_v[n] = gsmall[n], dd, mm, vv
    return (loss_all, grad_x[None], *[grads[n] for n in WEIGHTS], *[delta[n] for n in WEIGHTS],
            *[new_m[n] for n in WEIGHTS], *[new_v[n] for n in WEIGHTS])
```

```python
import itertools
import math

import jax
import jax.numpy as jnp
from jax import lax
from jax.experimental import pallas as pl
from jax.experimental.pallas import tpu as pltpu
from jax.experimental.pallas import tpu_sc as plsc

F32 = jnp.float32
BF16 = jnp.bfloat16
MXU = BF16
HI = lax.Precision.HIGHEST

D_MODEL = 1024
D_FF = 2816
EPS = 1e-6
S5_GROUPS, S5_GROUP, S5_STATE = 64, 16, 64
D_INNER = 2048
SSD_HEADDIM, SSD_HEADS, SSD_GROUPS, SSD_HPG, SSD_STATE = 64, 32, 8, 4, 128
CONV_K, CONV_DIM = 4, 4096
IN_OFFS = (0, 1024, 3072, 7168, 7200, 9248)
SSD_Q = 256
SSD_STEP_GROUPS = 8
ADAM_LR, ADAM_B1, ADAM_B2, ADAM_EPS, ADAM_WD, ADAM_STEP = 0.001, 0.9, 0.999, 1e-08, 0.01, 10

VMEM_LIMIT = 56 * 1024 * 1024
MATMUL_VMEM = 32 * 1024 * 1024
MESH = pl.DeviceIdType.MESH

NN = ((1,), (0,))
NT = ((1,), (1,))
TN = ((0,), (0,))


def _dot(a, b, dims, precision=None):
    if precision is None:
        a, b = a.astype(MXU), b.astype(MXU)
    return lax.dot_general(a, b, (dims, ((), ())), precision=precision, preferred_element_type=F32)


def _tile(n, pref):
    if n <= pref:
        return n
    best = None
    for t in range(128, pref + 1, 128):
        if n % t == 0:
            best = t
    assert best is not None, (n, pref)
    return best


def _params(sem):
    return pltpu.CompilerParams(dimension_semantics=sem, vmem_limit_bytes=VMEM_LIMIT)


def _matmul(a, b, mode, *, name, out=((F32),), epilogue=None, extras=(), after=(), tm=1024, tn=1536):
    if mode == "nn":
        (M, K), (_, N) = a.shape, b.shape
    elif mode == "nt":
        (M, K), (N, _) = a.shape, b.shape
    else:
        (K, M), (_, N) = a.shape, b.shape
    tm, tn = _tile(M, tm), _tile(N, tn)

    def vmem(tm_, tn_):
        per = K * tm_ * a.dtype.itemsize + K * tn_ * b.dtype.itemsize
        per += sum(tm_ * tn_ * jnp.dtype(dt).itemsize for dt in out)
        per += sum((1 if e.shape[0] == 1 else tm_) * tn_ * e.dtype.itemsize for e in extras)
        return 2 * per

    while vmem(tm, tn) > MATMUL_VMEM and (tn > 128 or tm > 128):
        if tn >= tm and tn > 128:
            tn = _tile(N, tn - 128)
        else:
            tm = _tile(M, tm - 128)
    bytes_a, bytes_b = a.size * a.dtype.itemsize, b.size * b.dtype.itemsize
    rows_outer = bytes_a + (M // tm) * bytes_b <= (N // tn) * bytes_a + bytes_b
    grid = (M // tm, N // tn) if rows_outer else (N // tn, M // tm)

    def at(f):
        return (lambda i, j: f(i, j)) if rows_outer else (lambda j, i: f(i, j))

    a_spec = pl.BlockSpec((K, tm), at(lambda i, j: (0, i))) if mode == "tn" else pl.BlockSpec((tm, K), at(lambda i, j: (i, 0)))
    b_spec = pl.BlockSpec((tn, K), at(lambda i, j: (j, 0))) if mode == "nt" else pl.BlockSpec((K, tn), at(lambda i, j: (0, j)))
    dims = {"nn": NN, "nt": NT, "tn": TN}[mode]
    e_specs = []
    for e in extras:
        if e.shape[0] == 1:
            e_specs.append(pl.BlockSpec((1, tn), at(lambda i, j: (0, j))))
        else:
            assert e.shape == (M, N), (e.shape, M, N)
            e_specs.append(pl.BlockSpec((tm, tn), at(lambda i, j: (i, j))))
    n_e, n_o, n_a = len(extras), len(out), len(after)

    def body(a_ref, b_ref, *refs):
        acc = _dot(a_ref[...], b_ref[...], dims)
        res = (acc,) if epilogue is None else epilogue(acc, *[r[...] for r in refs[:n_e]])
        for r, v in zip(refs[n_e + n_a:], res):
            r[...] = v.astype(r.dtype)

    res = pl.pallas_call(
        body, name=name, grid=grid,
        in_specs=[a_spec, b_spec] + e_specs + [ANY] * n_a,
        out_specs=[pl.BlockSpec((tm, tn), at(lambda i, j: (i, j))) for _ in range(n_o)],
        out_shape=[jax.ShapeDtypeStruct((M, N), dt) for dt in out],
        compiler_params=_params(("parallel", "parallel")),
    )(a, b, *extras, *after)
    return res[0] if n_o == 1 else res


ROW_TILE = 256
ROW_TILE_WIDE_UP_TO = 1024


def _rowwise(fn, rows, fulls, outs, reds=(), *, name, after=(), tm=ROW_TILE):
    T = rows[0].shape[0]
    widest = max([r.shape[1] for r in rows] + [c for c, _ in outs])
    if tm == ROW_TILE and widest <= ROW_TILE_WIDE_UP_TO and T % (2 * tm) == 0:
        tm *= 2
    tm = min(tm, T)
    n_r, n_f, n_o, n_d, n_a = len(rows), len(fulls), len(outs), len(reds), len(after)

    def body(*refs):
        ins = [r[...] for r in refs[:n_r + n_f]]
        o_refs = refs[n_r + n_f + n_a:n_r + n_f + n_a + n_o]
        d_refs = refs[n_r + n_f + n_a + n_o:]
        res = fn(*ins)
        for r, v in zip(o_refs, res[:n_o]):
            r[...] = v.astype(r.dtype)
        if n_d:
            @pl.when(pl.program_id(0) == 0)
            def _():
                for r in d_refs:
                    r[...] = jnp.zeros_like(r)
            for r, v in zip(d_refs, res[n_o:]):
                r[...] += v.astype(r.dtype)

    res = pl.pallas_call(
        body, name=name, grid=(T // tm,),
        in_specs=[pl.BlockSpec((tm, r.shape[1]), lambda i: (i, 0)) for r in rows]
        + [pl.BlockSpec(f.shape, lambda i, nd=f.ndim: (0,) * nd) for f in fulls] + [ANY] * n_a,
        out_specs=[pl.BlockSpec((tm, c), lambda i: (i, 0)) for c, _ in outs]
        + [pl.BlockSpec(s, lambda i, nd=len(s): (0,) * nd) for s, _ in reds],
        out_shape=[jax.ShapeDtypeStruct((T, c), dt) for c, dt in outs]
        + [jax.ShapeDtypeStruct(s, dt) for s, dt in reds],
        compiler_params=_params(("arbitrary",)),
    )(*rows, *fulls, *after)
    return res


def _rms(x, g):
    return x * lax.rsqrt(jnp.mean(x * x, axis=-1, keepdims=True) + EPS) * g


def _colsum(v):
    return jnp.sum(v, axis=0, keepdims=True)


def _softplus(x):
    return jnp.maximum(x, 0.0) + jnp.log1p(jnp.exp(-jnp.abs(x)))


def _ffn_fwd(x, norm, wgT, wuT, wd, tag, on_gate=None):
    D = x.shape[1]
    (hn,) = _rowwise(lambda xt, g: (_rms(xt, g),), [x], [norm], [(D, BF16)], name=f"{tag}_norm")
    a = _matmul(hn, wgT, "nt", name=f"{tag}_gate", out=(BF16,))
    if on_gate is not None:
        on_gate(a)
    b, hid = _matmul(hn, wuT, "nt", name=f"{tag}_up", out=(BF16, BF16),
                     epilogue=lambda acc, at: (acc, jax.nn.silu(at.astype(F32)) * acc), extras=[a])
    y = _matmul(hid, wd, "nn", name=f"{tag}_down", epilogue=lambda acc, xt: (xt + 0.5 * acc,), extras=[x])
    return y, (hn, a, b, hid)


def _ffn_bwd(dy, dy16, x, norm, wgT, wuT, wd, res, tag, after, on_grads):
    hn, a, b, hid = res
    D, F = x.shape[1], a.shape[1]

    def act_bwd(acc, at, bt):
        a_, b_ = at.astype(F32), bt.astype(F32)
        sig = jax.nn.sigmoid(a_)
        s = a_ * sig
        dh = 0.5 * acc
        return dh * b_ * (sig + s * (1.0 - sig)), dh * s

    da, db = _matmul(dy16, wd, "nt", name=f"{tag}_d_hid", out=(BF16, BF16), epilogue=act_bwd, extras=[a, b])
    d_wd = _matmul(hid, dy16, "tn", name=f"{tag}_d_wd", epilogue=lambda acc: (0.5 * acc,))
    d_wgT = _matmul(da, hn, "tn", name=f"{tag}_d_wg", after=after)
    d_wuT = _matmul(db, hn, "tn", name=f"{tag}_d_wu")
    started = on_grads(d_wgT, d_wuT, d_wd, dy)
    dhn = _matmul(da, wgT, "nn", name=f"{tag}_d_hn1")
    dhn = _matmul(db, wuT, "nn", name=f"{tag}_d_hn2", epilogue=lambda acc, e: (acc + e,), extras=[dhn])

    def norm_bwd(xt, dh, dyt, g):
        _, vjp = jax.vjp(_rms, xt, g)
        dx, dg = vjp(dh)
        return dyt + dx, dyt + dx, dg

    dx, dx16, d_norm = _rowwise(norm_bwd, [x, dhn, dy], [norm], [(D, F32), (D, BF16)], [((1, D), F32)],
                                name=f"{tag}_d_norm", after=started)
    return dx, dx16, d_norm


S5_TILES = 8
S5_HALF = 256


def _s5_derive(A_re, A_im, log_dt, B_re, B_im, C_re, C_im):
    G, N, M = S5_GROUPS, S5_STATE, S5_GROUP
    dt = jnp.exp(log_dt)[:, None]
    mag = jnp.exp(A_re * dt)
    ar, ai = mag * jnp.cos(A_im * dt), mag * jnp.sin(A_im * dt)
    den = A_re * A_re + A_im * A_im
    cr = ((ar - 1.0) * A_re + ai * A_im) / den
    ci = (ai * A_re - (ar - 1.0) * A_im) / den
    bbr = cr[..., None] * B_re - ci[..., None] * B_im
    bbi = cr[..., None] * B_im + ci[..., None] * B_re
    eye = jnp.eye(8, dtype=F32)

    def tile_in(bb):
        t = bb.reshape(S5_TILES, 8, N, M).transpose(0, 1, 3, 2)
        return jnp.einsum("jamn,ab->jambn", t, eye).reshape(S5_TILES, 8 * M, 8 * N)

    def tile_out(c):
        t = c.reshape(S5_TILES, 8, M, N).transpose(0, 1, 3, 2)
        return jnp.einsum("janm,ab->janbm", t, eye).reshape(S5_TILES, 8 * N, 8 * M)

    return (tile_in(bbr), tile_in(bbi), tile_out(C_re), tile_out(C_im),
            ar.reshape(S5_TILES, 1, 8 * N), ai.reshape(S5_TILES, 1, 8 * N))


S5_NB = S5_HALF // 128
S5_SEG = 32


def _cmul(ar, ai, br, bi):
    return ar * br - ai * bi, ar * bi + ai * br


def _s5_scan(sr_ref, si_ref, ar, ai, T, reverse):
    L, V = T // S5_SEG, S5_SEG // 8
    assert L * S5_SEG == T and L & (L - 1) == 0, T
    sg = -1.0 if reverse else 1.0
    a_r = [jnp.broadcast_to(ar[:, 128 * b:128 * (b + 1)], (8, 128)) for b in range(S5_NB)]
    a_i = [jnp.broadcast_to(sg * ai[:, 128 * b:128 * (b + 1)], (8, 128)) for b in range(S5_NB)]

    def rows(k, v):
        return pl.ds(pl.multiple_of(((L - 1 - k) if reverse else k) * S5_SEG + 8 * v, 8), 8)

    def local(k, carry):
        out = []
        for b in range(S5_NB):
            for v in range(V):
                idx = rows(k, v)
                mr, mi = _cmul(a_r[b], a_i[b], *carry[b * V + v])
                nr, ni = mr + sr_ref[b, idx, :], mi + si_ref[b, idx, :]
                sr_ref[b, idx, :] = nr
                si_ref[b, idx, :] = ni
                out.append((nr, ni))
        return tuple(out)

    z = jnp.zeros((8, 128), F32)
    ends = lax.fori_loop(0, L, local, tuple((z, z) for _ in range(S5_NB * V)))

    carries = []
    for b in range(S5_NB):
        pr, pi = a_r[b][0:1], a_i[b][0:1]
        n = L
        while n > 1:
            pr, pi = _cmul(pr, pi, pr, pi)
            n //= 2
        cr, ci = jnp.zeros((1, 128), F32), jnp.zeros((1, 128), F32)
        into = [None] * S5_SEG
        for j in (reversed(range(S5_SEG)) if reverse else range(S5_SEG)):
            into[j] = (cr, ci)
            er, ei = ends[b * V + j // 8]
            mr, mi = _cmul(pr, pi, cr, ci)
            cr, ci = mr + er[j % 8:j % 8 + 1], mi + ei[j % 8:j % 8 + 1]
        carries.append([(jnp.concatenate([into[8 * v + s][0] for s in range(8)], axis=0),
                         jnp.concatenate([into[8 * v + s][1] for s in range(8)], axis=0)) for v in range(V)])

    def fix(k, powers):
        out = []
        for b in range(S5_NB):
            pr, pi = powers[b]
            for v in range(V):
                idx = rows(k, v)
                dr, di = _cmul(pr, pi, *carries[b][v])
                sr_ref[b, idx, :] += dr
                si_ref[b, idx, :] += di
            out.append(_cmul(pr, pi, a_r[b], a_i[b]))
        return tuple(out)

    lax.fori_loop(0, L, fix, tuple((a_r[b], a_i[b]) for b in range(S5_NB)))


def _s5_in(ut, b_ref, s_ref):
    for b in range(S5_NB):
        s_ref[b] = _dot(ut, b_ref[:, 128 * b:128 * (b + 1)], NN)


def _to_seg(v):
    T, C = v.shape
    return v.reshape(S5_SEG, T // S5_SEG, C).transpose(1, 0, 2).reshape(T, C)


def _from_seg(v):
    T, C = v.shape
    return v.reshape(T // S5_SEG, S5_SEG, C).transpose(1, 0, 2).reshape(T, C)


def _s5_specs(T):
    u_spec = pl.BlockSpec((T, 128), lambda j, h: (0, j))
    b_spec = pl.BlockSpec((None, 128, S5_HALF), lambda j, h: (j, 0, h))
    c_spec = pl.BlockSpec((None, S5_HALF, 128), lambda j, h: (j, h, 0))
    a_spec = pl.BlockSpec((None, 1, S5_HALF), lambda j, h: (j, 0, h))
    return u_spec, b_spec, c_spec, a_spec


def _s5_fwd(u, tiles):
    T = u.shape[0]
    u_spec, b_spec, c_spec, a_spec = _s5_specs(T)

    def body(u_ref, br_ref, bi_ref, cr_ref, ci_ref, ar_ref, ai_ref, y_ref, sr_ref, si_ref):
        ut = u_ref[...].astype(MXU)
        _s5_in(ut, br_ref, sr_ref)
        _s5_in(ut, bi_ref, si_ref)
        _s5_scan(sr_ref, si_ref, ar_ref[...], ai_ref[...], T, False)
        y = None
        for b in range(S5_NB):
            blk = slice(128 * b, 128 * (b + 1))
            yb = _dot(sr_ref[b], cr_ref[blk, :], NN) - _dot(si_ref[b], ci_ref[blk, :], NN)
            y = yb if y is None else y + yb

        @pl.when(pl.program_id(1) == 0)
        def _():
            y_ref[...] = y

        @pl.when(pl.program_id(1) == 1)
        def _():
            y_ref[...] += y

    scr = pltpu.VMEM((S5_NB, T, 128), F32)
    return pl.pallas_call(
        body, name="s5_fwd", grid=(S5_TILES, 2),
        in_specs=[u_spec, b_spec, b_spec, c_spec, c_spec, a_spec, a_spec],
        out_specs=u_spec, out_shape=jax.ShapeDtypeStruct(u.shape, F32),
        scratch_shapes=[scr, scr],
        compiler_params=_params(("parallel", "arbitrary")),
    )(u, *tiles)


def _s5_bwd(u, dys, du_skip, tiles):
    T = u.shape[0]
    u_spec, b_spec, c_spec, a_spec = _s5_specs(T)

    def body(u_ref, dy_ref, sk_ref, br_ref, bi_ref, cr_ref, ci_ref, ar_ref, ai_ref,
             du_ref, dbr_ref, dbi_ref, dcr_ref, dci_ref, dar_ref, dai_ref, sr_ref, si_ref, lr_ref, li_ref):
        ut, dy = u_ref[...].astype(MXU), dy_ref[...].astype(MXU)
        ar, ai = ar_ref[...], ai_ref[...]
        _s5_in(ut, br_ref, sr_ref)
        _s5_in(ut, bi_ref, si_ref)
        _s5_scan(sr_ref, si_ref, ar, ai, T, False)
        for b in range(S5_NB):
            blk = slice(128 * b, 128 * (b + 1))
            dcr_ref[blk, :] = _dot(sr_ref[b], dy, TN)
            dci_ref[blk, :] = -_dot(si_ref[b], dy, TN)
            lr_ref[b] = _dot(dy, cr_ref[blk, :], NT)
            li_ref[b] = -_dot(dy, ci_ref[blk, :], NT)
        _s5_scan(lr_ref, li_ref, ar, ai, T, True)
        du = None
        head = lax.broadcasted_iota(jnp.int32, (T, 128), 0) < S5_SEG
        seg0 = lax.broadcasted_iota(jnp.int32, (S5_SEG, 128), 0) == 0
        for b in range(S5_NB):
            blk = slice(128 * b, 128 * (b + 1))
            lr, li = lr_ref[b], li_ref[b]
            pr = jnp.where(head, 0.0, pltpu.roll(sr_ref[b], S5_SEG, 0))
            pi = jnp.where(head, 0.0, pltpu.roll(si_ref[b], S5_SEG, 0))
            er = jnp.where(seg0, 0.0, pltpu.roll(sr_ref[b, T - S5_SEG:T, :], 1, 0))
            ei = jnp.where(seg0, 0.0, pltpu.roll(si_ref[b, T - S5_SEG:T, :], 1, 0))
            hr, hi = lr[0:S5_SEG], li[0:S5_SEG]
            dar_ref[:, blk] = _colsum(lr * pr + li * pi) + _colsum(hr * er + hi * ei)
            dai_ref[:, blk] = _colsum(li * pr - lr * pi) + _colsum(hi * er - hr * ei)
            dbr_ref[:, blk] = _dot(ut, lr, TN)
            dbi_ref[:, blk] = _dot(ut, li, TN)
            dub = _dot(lr, br_ref[:, blk], NT) + _dot(li, bi_ref[:, blk], NT)
            du = dub if du is None else du + dub

        @pl.when(pl.program_id(1) == 0)
        def _():
            du_ref[...] = du + sk_ref[...]

        @pl.when(pl.program_id(1) == 1)
        def _():
            du_ref[...] += du

    scr = pltpu.VMEM((S5_NB, T, 128), F32)
    return pl.pallas_call(
        body, name="s5_bwd", grid=(S5_TILES, 2),
        in_specs=[u_spec, u_spec, u_spec, b_spec, b_spec, c_spec, c_spec, a_spec, a_spec],
        out_specs=[u_spec, b_spec, b_spec, c_spec, c_spec, a_spec, a_spec],
        out_shape=[jax.ShapeDtypeStruct(u.shape, F32)] + [jax.ShapeDtypeStruct(t.shape, F32) for t in tiles],
        scratch_shapes=[scr, scr, scr, scr],
        compiler_params=_params(("parallel", "arbitrary")),
    )(u, dys, du_skip, *tiles)


CONV_TILE = 256


def _conv_pre(x, w, b):
    T = x.shape[0]
    row = lax.broadcasted_iota(jnp.int32, x.shape, 0)
    acc = x * w[CONV_K - 1:CONV_K, :] + b
    for lag in range(1, CONV_K):
        acc = acc + jnp.where(row >= lag, pltpu.roll(x, lag, 0), 0.0) * w[CONV_K - 1 - lag:CONV_K - lag, :]
    return acc


def _conv_fwd(x, w, b):
    T, C = x.shape
    col = pl.BlockSpec((T, CONV_TILE), lambda j: (0, j))

    def body(x_ref, w_ref, b_ref, o_ref):
        o_ref[...] = jax.nn.silu(_conv_pre(x_ref[...], w_ref[...], b_ref[...]))

    return pl.pallas_call(
        body, name="conv_fwd", grid=(C // CONV_TILE,),
        in_specs=[col, pl.BlockSpec((CONV_K, CONV_TILE), lambda j: (0, j)), pl.BlockSpec((1, CONV_TILE), lambda j: (0, j))],
        out_specs=col, out_shape=jax.ShapeDtypeStruct((T, C), F32),
        compiler_params=_params(("parallel",)),
    )(x, w, b)


def _conv_bwd(x, w, b, dout, col0, name):
    T, C = dout.shape
    off = col0 // CONV_TILE
    xcol = pl.BlockSpec((T, CONV_TILE), lambda j: (0, j + off))
    dcol = pl.BlockSpec((T, CONV_TILE), lambda j: (0, j))

    def body(x_ref, w_ref, b_ref, d_ref, dx_ref, dw_ref, db_ref):
        x, w = x_ref[...], w_ref[...]
        _, vjp = jax.vjp(jax.nn.silu, _conv_pre(x, w, b_ref[...]))
        (dy,) = vjp(d_ref[...])
        row = lax.broadcasted_iota(jnp.int32, x.shape, 0)
        dx = dy * w[CONV_K - 1:CONV_K, :]
        dw_ref[CONV_K - 1:CONV_K, :] = _colsum(dy * x)
        for lag in range(1, CONV_K):
            ahead = jnp.where(row < T - lag, pltpu.roll(dy, T - lag, 0), 0.0)
            dx = dx + ahead * w[CONV_K - 1 - lag:CONV_K - lag, :]
            dw_ref[CONV_K - 1 - lag:CONV_K - lag, :] = _colsum(ahead * x)
        dx_ref[...] = dx.astype(dx_ref.dtype)
        db_ref[...] = _colsum(dy)

    return pl.pallas_call(
        body, name=name, grid=(C // CONV_TILE,),
        in_specs=[xcol, pl.BlockSpec((CONV_K, CONV_TILE), lambda j: (0, j + off)),
                  pl.BlockSpec((1, CONV_TILE), lambda j: (0, j + off)), dcol],
        out_specs=[dcol, pl.BlockSpec((CONV_K, CONV_TILE), lambda j: (0, j)), pl.BlockSpec((1, CONV_TILE), lambda j: (0, j))],
        out_shape=[jax.ShapeDtypeStruct((T, C), BF16), jax.ShapeDtypeStruct((CONV_K, C), F32), jax.ShapeDtypeStruct((1, C), F32)],
        compiler_params=_params(("parallel",)),
    )(x, w, b, dout)


def _ssd_common(x_ref, b_ref, c_ref, dtr_ref, bias_ref, alog_ref, tri_ref, triu_ref):
    Q = x_ref.shape[0]
    x, Bm, Cm = x_ref[...], b_ref[...], c_ref[...]
    pre = dtr_ref[...] + bias_ref[...]
    dt = _softplus(pre)
    A = -jnp.exp(alog_ref[...])
    adt = dt * A
    ac4 = _dot(tri_ref[...], adt, NN, HI)
    ar4 = _dot(adt, triu_ref[...], TN, HI)
    cb = _dot(Cm, Bm, NT)
    ii = lax.broadcasted_iota(jnp.int32, (Q, Q), 0)
    jj = lax.broadcasted_iota(jnp.int32, (Q, Q), 1)
    atot4 = ac4[Q - 1:Q, :]
    return x, Bm, Cm, pre, dt, A, ac4, ar4, cb, ii >= jj, atot4


def _ssd_decay(k, ac4, ar4, causal):
    seg = ac4[:, k:k + 1] - ar4[k:k + 1, :]
    return jnp.where(causal, jnp.exp(jnp.where(causal, seg, 0.0)), 0.0)


def _per_head(c4, n):
    lane = lax.broadcasted_iota(jnp.int32, (c4.shape[0], n), 1)
    out = jnp.broadcast_to(c4[:, 0:1], (c4.shape[0], n))
    for k in range(1, SSD_HPG):
        out = jnp.where(lane >= SSD_HEADDIM * k, c4[:, k:k + 1], out)
    return out


def _head_sums(v):
    n = SSD_HPG * SSD_HEADDIM
    row = lax.broadcasted_iota(jnp.int32, (n, SSD_HPG), 0)
    col = lax.broadcasted_iota(jnp.int32, (n, SSD_HPG), 1)
    member = jnp.where((row >= SSD_HEADDIM * col) & (row < SSD_HEADDIM * (col + 1)), 1.0, 0.0)
    return _dot(v, member, NN, HI)


def _head_rows(c4):
    n = SSD_HPG * SSD_HEADDIM
    row = lax.broadcasted_iota(jnp.int32, (n, 1), 0)
    out = jnp.broadcast_to(c4[:, 0:1], (n, 1))
    for k in range(1, SSD_HPG):
        out = jnp.where(row >= SSD_HEADDIM * k, c4[:, k:k + 1], out)
    return out


def _ssd_specs(T, Q, rev):
    NC = T // Q
    cc = (lambda c: NC - 1 - c) if rev else (lambda c: c)
    W, N, S = SSD_HPG * SSD_HEADDIM, SSD_STATE, SSD_STEP_GROUPS
    x_spec = pl.BlockSpec((Q, S * W), lambda g, c: (cc(c), g))
    b_spec = pl.BlockSpec((Q, S * N), lambda g, c: (cc(c), D_INNER // (S * N) + g))
    c_spec = pl.BlockSpec((Q, S * N), lambda g, c: (cc(c), (D_INNER + SSD_GROUPS * N) // (S * N) + g))
    dt_spec = pl.BlockSpec((S, Q, SSD_HPG), lambda g, c: (g, cc(c), 0))
    p_spec = pl.BlockSpec((S, 1, SSD_HPG), lambda g, c: (g, 0, 0))
    tri_spec = pl.BlockSpec((Q, Q), lambda g, c: (0, 0))
    h_spec = pl.BlockSpec((S, None, W, N), lambda g, c: (g, cc(c), 0, 0))
    return x_spec, b_spec, c_spec, dt_spec, p_spec, tri_spec, h_spec


def _ssd_views(gi, wide, narrow, lead):
    W, N = SSD_HPG * SSD_HEADDIM, SSD_STATE
    return ([r.at[:, pl.ds(W * gi, W)] for r in wide] + [r.at[:, pl.ds(N * gi, N)] for r in narrow]
            + [r.at[gi] for r in lead])


def _tri(Q):
    tri = jnp.tril(jnp.ones((Q, Q), F32))
    return tri, tri.T


def _ssd_fwd(xc, dtr8, bias8, alog8, dsk8):
    T, Q = xc.shape[0], SSD_Q
    NC = T // Q
    P, W = SSD_HEADDIM, SSD_HPG * SSD_HEADDIM
    x_spec, b_spec, c_spec, dt_spec, p_spec, tri_spec, h_spec = _ssd_specs(T, Q, False)

    def body(x_ref, b_ref, c_ref, dtr_ref, bias_ref, alog_ref, dsk_ref, tri_ref, triu_ref, y_ref, hs_ref, h_scr):
        for gi in range(SSD_STEP_GROUPS):
            xv, yv, bv, cv, dtv, biasv, alogv, dskv, hsv, hv = _ssd_views(
                gi, (x_ref, y_ref), (b_ref, c_ref), (dtr_ref, bias_ref, alog_ref, dsk_ref, hs_ref, h_scr))
            one(xv, bv, cv, dtv, biasv, alogv, dskv, tri_ref, triu_ref, yv, hsv, hv)

    def one(x_ref, b_ref, c_ref, dtr_ref, bias_ref, alog_ref, dsk_ref, tri_ref, triu_ref, y_ref, hs_ref, h_scr):
        @pl.when(pl.program_id(1) == 0)
        def _():
            h_scr[...] = jnp.zeros_like(h_scr)

        hin = h_scr[...]
        hs_ref[...] = hin
        x, Bm, Cm, pre, dt, A, ac4, ar4, cb, causal, atot4 = _ssd_common(
            x_ref, b_ref, c_ref, dtr_ref, bias_ref, alog_ref, tri_ref, triu_ref)
        xdt = x * _per_head(dt, W)
        rest = _dot(Cm, hin, NT) * _per_head(jnp.exp(ac4), W) + _per_head(dsk_ref[...], W) * x
        for k in range(SSD_HPG):
            blk = slice(P * k, P * (k + 1))
            G = cb * _ssd_decay(k, ac4, ar4, causal)
            y_ref[:, blk] = _dot(G, xdt[:, blk], NN) + rest[:, blk]
        xw = xdt * _per_head(jnp.exp(atot4 - ac4), W)
        h_scr[...] = _head_rows(jnp.exp(atot4)) * hin + _dot(xw, Bm, TN)

    tri, triu = _tri(Q)
    return pl.pallas_call(
        body, name="ssd_fwd", grid=(SSD_GROUPS // SSD_STEP_GROUPS, NC),
        in_specs=[x_spec, b_spec, c_spec, dt_spec, p_spec, p_spec, p_spec, tri_spec, tri_spec],
        out_specs=[pl.BlockSpec((Q, SSD_STEP_GROUPS * W), lambda g, c: (c, g)), h_spec],
        out_shape=[jax.ShapeDtypeStruct((T, D_INNER), F32),
                   jax.ShapeDtypeStruct((SSD_GROUPS, NC, W, SSD_STATE), F32)],
        scratch_shapes=[pltpu.VMEM((SSD_STEP_GROUPS, W, SSD_STATE), F32)],
        compiler_params=_params(("parallel", "arbitrary")),
    )(xc, xc, xc, dtr8, bias8, alog8, dsk8, tri, triu)


def _ssd_bwd(xc, dtr8, bias8, alog8, dsk8, hs, dy):
    T, Q = xc.shape[0], SSD_Q
    NC = T // Q
    P, W = SSD_HEADDIM, SSD_HPG * SSD_HEADDIM
    x_spec, b_spec, c_spec, dt_spec, p_spec, tri_spec, h_spec = _ssd_specs(T, Q, True)
    dy_spec = pl.BlockSpec((Q, SSD_STEP_GROUPS * W), lambda g, c: (NC - 1 - c, g))
    dbc_spec = pl.BlockSpec((Q, SSD_STEP_GROUPS * SSD_STATE), lambda g, c: (NC - 1 - c, g))

    def body(x_ref, b_ref, c_ref, dtr_ref, bias_ref, alog_ref, dsk_ref, tri_ref, triu_ref, hs_ref, dy_ref,
             dx_ref, db_ref, dc_ref, ddtr_ref, dbias_ref, dalog_ref, ddsk_ref, dh_scr):
        for gi in range(SSD_STEP_GROUPS):
            (xv, dyv, dxv, bv, cv, dbv, dcv, dtv, biasv, alogv, dskv, hsv, ddtv, dbiasv, dalogv, ddskv, dhv) = _ssd_views(
                gi, (x_ref, dy_ref, dx_ref), (b_ref, c_ref, db_ref, dc_ref),
                (dtr_ref, bias_ref, alog_ref, dsk_ref, hs_ref, ddtr_ref, dbias_ref, dalog_ref, ddsk_ref, dh_scr))
            one(xv, bv, cv, dtv, biasv, alogv, dskv, tri_ref, triu_ref, hsv, dyv,
                dxv, dbv, dcv, ddtv, dbiasv, dalogv, ddskv, dhv)

    def one(x_ref, b_ref, c_ref, dtr_ref, bias_ref, alog_ref, dsk_ref, tri_ref, triu_ref, hs_ref, dy_ref,
            dx_ref, db_ref, dc_ref, ddtr_ref, dbias_ref, dalog_ref, ddsk_ref, dh_scr):
        @pl.when(pl.program_id(1) == 0)
        def _():
            dh_scr[...] = jnp.zeros_like(dh_scr)
            dbias_ref[...] = jnp.zeros_like(dbias_ref)
            dalog_ref[...] = jnp.zeros_like(dalog_ref)
            ddsk_ref[...] = jnp.zeros_like(ddsk_ref)

        x, Bm, Cm, pre, dt, A, ac4, ar4, cb, causal, atot4 = _ssd_common(
            x_ref, b_ref, c_ref, dtr_ref, bias_ref, alog_ref, tri_ref, triu_ref)
        dyv, hin, dho = dy_ref[...], hs_ref[...], dh_scr[...]
        dt_w = _per_head(dt, W)
        xdt = x * dt_w
        E4, F4, etot4 = jnp.exp(ac4), jnp.exp(atot4 - ac4), jnp.exp(atot4)
        F_w = _per_head(F4, W)
        ddsk_ref[...] += _head_sums(_colsum(dyv * x))
        Z = _dot(Cm, hin, NT)
        dZ = dyv * _per_head(E4, W)
        dac4 = _head_sums(dyv * Z) * E4
        dC = _dot(dZ, hin, NN)
        dh_scr[...] = _dot(dZ, Cm, TN) + _head_rows(etot4) * dho
        per_row = jnp.sum(dho * hin, axis=1, keepdims=True)
        lane4 = lax.broadcasted_iota(jnp.int32, (1, SSD_HPG), 1)
        datot4 = jnp.zeros((1, SSD_HPG), F32)
        for k in range(SSD_HPG):
            datot4 = jnp.where(lane4 == k, jnp.sum(per_row[P * k:P * (k + 1)], keepdims=True), datot4)
        datot4 = datot4 * etot4
        dxw = _dot(Bm, dho, NT)
        dB = _dot(xdt * F_w, dho, NN)
        dFa4 = _head_sums(dxw * xdt) * F4
        datot4 = datot4 + _colsum(dFa4)
        dac4 = dac4 - dFa4
        dcb = jnp.zeros((Q, Q), F32)
        lane_q4 = lax.broadcasted_iota(jnp.int32, (Q, SSD_HPG), 1)
        sub_4q = lax.broadcasted_iota(jnp.int32, (SSD_HPG, Q), 0)
        row_sums = jnp.zeros((Q, SSD_HPG), F32)
        col_sums = jnp.zeros((SSD_HPG, Q), F32)
        dxdt_heads = []
        for k in range(SSD_HPG):
            blk = slice(P * k, P * (k + 1))
            L = _ssd_decay(k, ac4, ar4, causal)
            G = cb * L
            dG = _dot(dyv[:, blk], xdt[:, blk], NT)
            dxdt_heads.append(_dot(G, dyv[:, blk], TN))
            dcb = dcb + dG * L
            Mseg = dG * G
            row_sums = jnp.where(lane_q4 == k, jnp.sum(Mseg, axis=1, keepdims=True), row_sums)
            col_sums = jnp.where(sub_4q == k, jnp.sum(Mseg, axis=0, keepdims=True), col_sums)
        dxdt = dxw * F_w + jnp.concatenate(dxdt_heads, axis=1)
        last = lax.broadcasted_iota(jnp.int32, (Q, SSD_HPG), 0) == Q - 1
        dac4 = dac4 + row_sums + jnp.where(last, datot4, 0.0)
        dadt4 = _dot(triu_ref[...], dac4, NN, HI) - _dot(triu_ref[...], col_sums, NT, HI)
        ddt4 = _head_sums(dxdt * x) + dadt4 * A
        dalog_ref[...] += _colsum(dadt4 * dt) * A
        ddtr4 = ddt4 * jax.nn.sigmoid(pre)
        dbias_ref[...] += _colsum(ddtr4)
        ddtr_ref[...] = ddtr4
        dx_ref[...] = (_per_head(dsk_ref[...], W) * dyv + dxdt * dt_w).astype(dx_ref.dtype)
        db_ref[...] = dB + _dot(dcb, Cm, TN)
        dc_ref[...] = dC + _dot(dcb, Bm, NN)

    tri, triu = _tri(Q)
    return pl.pallas_call(
        body, name="ssd_bwd", grid=(SSD_GROUPS // SSD_STEP_GROUPS, NC),
        in_specs=[x_spec, b_spec, c_spec, dt_spec, p_spec, p_spec, p_spec, tri_spec, tri_spec, h_spec, dy_spec],
        out_specs=[dy_spec, dbc_spec, dbc_spec, dt_spec, p_spec, p_spec, p_spec],
        out_shape=[jax.ShapeDtypeStruct((T, D_INNER), F32),
                   jax.ShapeDtypeStruct((T, SSD_GROUPS * SSD_STATE), F32),
                   jax.ShapeDtypeStruct((T, SSD_GROUPS * SSD_STATE), F32),
                   jax.ShapeDtypeStruct(dtr8.shape, F32)] + [jax.ShapeDtypeStruct(bias8.shape, F32)] * 3,
        scratch_shapes=[pltpu.VMEM((SSD_STEP_GROUPS, W, SSD_STATE), F32)],
        compiler_params=_params(("parallel", "arbitrary")),
    )(xc, xc, xc, dtr8, bias8, alog8, dsk8, tri, triu, hs, dy)


def _to_groups(v):
    return v.reshape(SSD_GROUPS, 1, SSD_HPG)


def _mixer_fwd(x, p, w):
    T, D = x.shape
    (hn,) = _rowwise(lambda xt, g: (_rms(xt, g),), [x], [p["mix_norm"]], [(D, BF16)], name="mix_norm")
    winT = w["w_in"]
    u, z, xbc, dtr, gl = [
        _matmul(hn, winT[IN_OFFS[i]:IN_OFFS[i + 1]], "nt", name=f"mix_in{i}") for i in range(5)]
    tiles = _s5_derive(*[p[k][0] for k in ("s5_A_re", "s5_A_im", "s5_log_dt", "s5_B_re", "s5_B_im", "s5_C_re", "s5_C_im")])
    u = _to_seg(u)
    ys = _s5_fwd(u, tiles)
    (g,) = _rowwise(lambda yt, ut, d: (jax.nn.gelu(yt + d * ut),), [ys, u], [p["s5_D"]], [(D, F32)], name="s5_gelu")
    y5 = _matmul(g, w["s5_w_glu"], "nn", name="s5_glu", out=(BF16,),
                 epilogue=lambda acc, gt, b: (gt * jax.nn.sigmoid(acc + b),), extras=[g, p["s5_b_glu"]])
    xc = _conv_fwd(xbc, w["conv_w"], p["conv_b"])
    dtr8 = dtr.reshape(T, SSD_GROUPS, SSD_HPG).transpose(1, 0, 2)
    ssd_p = [_to_groups(p[k]) for k in ("ssd_dt_bias", "ssd_A_log", "ssd_D")]
    yssd_raw, hs = _ssd_fwd(xc, dtr8, *ssd_p)
    (yssd,) = _rowwise(lambda yt, zt, nw: (_rms(yt * jax.nn.silu(zt), nw),), [yssd_raw, z], [p["ssd_norm"]],
                       [(D_INNER, BF16)], name="ssd_gate")
    p5 = _from_seg(_matmul(y5, w["w_proj_s5"], "nn", name="mix_p5"))
    pssd = _matmul(yssd, w["w_proj_ssd"], "nn", name="mix_pssd")

    def merge(glt, at, bt, bg):
        gates = jax.nn.sigmoid(glt + bg)
        return (gates[:, :D] * at + gates[:, D:] * bt,)

    (merged,) = _rowwise(merge, [gl, p5, pssd], [p["b_gate"]], [(D, BF16)], name="mix_merge")
    y = _matmul(merged, w["w_out"], "nn", name="mix_out", epilogue=lambda acc, xt: (xt + acc,), extras=[x])
    res = dict(hn=hn, u=u, z=z, xbc=xbc, gl=gl, tiles=tiles, ys=ys, g=g, y5=y5, xc=xc, dtr8=dtr8, ssd_p=ssd_p,
               yssd_raw=yssd_raw, hs=hs, yssd=yssd, p5=p5, pssd=pssd, merged=merged)
    return y, res


def _mixer_bwd(dy, dy16, x, p, w, r):
    T, D = x.shape
    gp = {}
    dmerged = _matmul(dy16, w["w_out"], "nt", name="mix_d_merged")
    gp["w_out"] = _matmul(r["merged"], dy16, "tn", name="mix_d_wout")

    def merge_bwd(glt, at, bt, dm, bg):
        def f(q, a_, b_):
            gates = jax.nn.sigmoid(q + bg)
            return gates[:, :D] * a_ + gates[:, D:] * b_
        _, vjp = jax.vjp(f, glt, at, bt)
        dq, da_, db_ = vjp(dm)
        return da_, db_, dq, _colsum(dq)

    dp5, dpssd, dgl, gp["b_gate"] = _rowwise(
        merge_bwd, [r["gl"], r["p5"], r["pssd"], dmerged], [p["b_gate"]],
        [(D, BF16), (D, BF16), (2 * D, BF16)], [((1, 2 * D), F32)], name="mix_d_merge")
    dyssd = _matmul(dpssd, w["w_proj_ssd"], "nt", name="mix_d_yssd")
    gp["w_proj_ssd"] = _matmul(r["yssd"], dpssd, "tn", name="mix_d_wpssd")

    def gate_bwd(yt, zt, dyt, nw):
        _, vjp = jax.vjp(lambda a_, b_, c_: _rms(a_ * jax.nn.silu(b_), c_), yt, zt, nw)
        return vjp(dyt)

    dyraw, dz, gp["ssd_norm"] = _rowwise(
        gate_bwd, [r["yssd_raw"], r["z"], dyssd], [p["ssd_norm"]],
        [(D_INNER, F32), (D_INNER, BF16)], [((1, D_INNER), F32)], name="ssd_d_gate")
    dxs, dBm, dCm, ddtr8, dbias8, dalog8, ddsk8 = _ssd_bwd(r["xc"], r["dtr8"], *r["ssd_p"], r["hs"], dyraw)
    gp["ssd_dt_bias"], gp["ssd_A_log"], gp["ssd_D"] = [v.reshape(1, SSD_HEADS) for v in (dbias8, dalog8, ddsk8)]
    ddtr = ddtr8.transpose(1, 0, 2).reshape(T, SSD_HEADS)
    conv = [_conv_bwd(r["xbc"], w["conv_w"], p["conv_b"], d, c0, f"conv_bwd{i}")
            for i, (d, c0) in enumerate(((dxs, 0), (dBm, D_INNER), (dCm, D_INNER + SSD_GROUPS * SSD_STATE)))]
    dxbc = [c[0] for c in conv]
    gp["conv_w"] = jnp.concatenate([c[1] for c in conv], axis=1)
    gp["conv_b"] = jnp.concatenate([c[2] for c in conv], axis=1)

    dp5 = _to_seg(dp5)
    dy5 = _matmul(dp5, w["w_proj_s5"], "nt", name="mix_d_y5")
    gp["w_proj_s5"] = _matmul(r["y5"], dp5, "tn", name="mix_d_wp5")
    g = r["g"]

    def glu_ep(acc, gt, b, dyt):
        _, vjp = jax.vjp(lambda g_, t_: g_ * jax.nn.sigmoid(t_ + b), gt, acc)
        return vjp(dyt)

    dg1, dt_ = _matmul(g, w["s5_w_glu"], "nn", name="s5_d_glu", out=(F32, BF16), epilogue=glu_ep,
                       extras=[g, p["s5_b_glu"], dy5])
    gp["s5_w_glu"] = _matmul(g, dt_, "tn", name="s5_d_wglu")
    dg = _matmul(dt_, w["s5_w_glu"], "nt", name="s5_d_g", epilogue=lambda acc, e: (acc + e,), extras=[dg1])

    def gelu_bwd(yt, ut, dgt, dtt, d):
        _, vjp = jax.vjp(lambda y_, d_: jax.nn.gelu(y_ + d_ * ut), yt, d)
        dys, dd = vjp(dgt)
        return dys, dys * d, dd, _colsum(dtt.astype(F32))

    dys, dusk, gp["s5_D"], gp["s5_b_glu"] = _rowwise(
        gelu_bwd, [r["ys"], r["u"], dg, dt_], [p["s5_D"]], [(D, F32), (D, F32)], [((1, D), F32), ((1, D), F32)],
        name="s5_d_gelu")
    du, *dtiles = _s5_bwd(r["u"], dys, dusk, r["tiles"])
    du = _from_seg(du)

    winT = w["w_in"]
    hn = r["hn"]
    d_rows, dhn = [], None
    cols = [(du, 0, 1024), (dz, 1024, 3072), (dxbc[0], 3072, 5120), (dxbc[1], 5120, 6144), (dxbc[2], 6144, 7168),
            (ddtr, 7168, 7200), (dgl, 7200, 9248)]
    for i, (d, lo, hi) in enumerate(cols):
        d_rows.append(_matmul(d, hn, "tn", name=f"mix_d_win{i}"))
        if dhn is None:
            dhn = _matmul(d, winT[lo:hi], "nn", name=f"mix_d_hn{i}")
        else:
            dhn = _matmul(d, winT[lo:hi], "nn", name=f"mix_d_hn{i}", epilogue=lambda acc, e: (acc + e,), extras=[dhn])
    gp["w_in"] = jnp.concatenate(d_rows, axis=0)

    def norm_bwd(xt, dh, dyt, gn):
        _, vjp = jax.vjp(_rms, xt, gn)
        dx, dgn = vjp(dh)
        return dyt + dx, dyt + dx, dgn

    dx, dx16, gp["mix_norm"] = _rowwise(norm_bwd, [x, dhn, dy], [p["mix_norm"]], [(D, F32), (D, BF16)],
                                        [((1, D), F32)], name="mix_d_norm")
    return dx, dx16, gp, dtiles


def _s5_param_grads(p, dtiles):
    keys = ("s5_A_re", "s5_A_im", "s5_log_dt", "s5_B_re", "s5_B_im", "s5_C_re", "s5_C_im")
    _, vjp = jax.vjp(_s5_derive, *[p[k][0] for k in keys])
    return {k: v[None] for k, v in zip(keys, vjp(tuple(dtiles)))}


def _local_step(x, target, p, w, late_weights, exchange):
    T, D = x.shape
    w = dict(w)
    x1, r1 = _ffn_fwd(x, p["ffn1_norm"], w["ffn1_w_gate"], w["ffn1_w_up"], w["ffn1_w_down"], "ffn1",
                      on_gate=lambda a_: w.update(late_weights("mixer", a_)))
    x2, rm = _mixer_fwd(x1, p, w)
    w.update(late_weights("ffn2", rm["p5"]))
    x3, r2 = _ffn_fwd(x2, p["ffn2_norm"], w["ffn2_w_gate"], w["ffn2_w_up"], w["ffn2_w_down"], "ffn2")

    def head(xt, tt, g):
        def f(x_, g_):
            e = _rms(x_, g_) - tt
            return 0.5 * jnp.sum(jnp.mean(e * e, axis=-1))
        l, (dx_, dg_) = jax.value_and_grad(f, argnums=(0, 1))(xt, g)
        return dx_, dx_, l.reshape(1, 1), dg_

    dx3, dx3_16, loss, d_final = _rowwise(head, [x3, target], [p["final_norm"]], [(D, F32), (D, BF16)],
                                          [((1, 1), F32), ((1, D), F32)], name="loss_head")
    gp = {"final_norm": d_final}

    def ffn_exchange(tag):
        return lambda d_wgT, d_wuT, d_wd, after: exchange(
            tag, {f"{tag}_w_gate": d_wgT, f"{tag}_w_up": d_wuT, f"{tag}_w_down": d_wd}, after)

    dx2, dx2_16, gp["ffn2_norm"] = _ffn_bwd(
        dx3, dx3_16, x2, p["ffn2_norm"], w["ffn2_w_gate"], w["ffn2_w_up"], w["ffn2_w_down"], r2, "ffn2", (),
        ffn_exchange("ffn2"))
    dx1, dx1_16, gm, dtiles = _mixer_bwd(dx2, dx2_16, x1, p, w, rm)
    conv_w_grad = gm.pop("conv_w")
    started = exchange("mixer", {k: gm.pop(k) for k in ("w_out", "w_proj_s5", "w_proj_ssd", "s5_w_glu", "w_in")}, dx1)
    gp.update(gm)
    gp.update(_s5_param_grads(p, dtiles))
    dx0, _, gp["ffn1_norm"] = _ffn_bwd(
        dx1, dx1_16, x, p["ffn1_norm"], w["ffn1_w_gate"], w["ffn1_w_up"], w["ffn1_w_down"], r1, "ffn1", started,
        ffn_exchange("ffn1"))
    return loss, dx0, gp, conv_w_grad


ADAMW_BLOCK = 1 << 19


def _adamw(w, m, v, gs, *, name, emit_g):
    _, R, C = w.shape
    n_g = len(gs)
    n_o = 4 if emit_g else 3
    tm = _rtile(R, max(8, ADAMW_BLOCK // C))
    lead = pl.BlockSpec((None, tm, C), lambda i: (0, i, 0))

    def body(w_ref, m_ref, v_ref, *refs):
        g = refs[0][...]
        for r in refs[1:n_g]:
            g = g + r[...]
        m2 = ADAM_B1 * m_ref[...] + (1.0 - ADAM_B1) * g
        v2 = ADAM_B2 * v_ref[...] + (1.0 - ADAM_B2) * (g * g)
        m_hat = m2 / (1.0 - ADAM_B1 ** ADAM_STEP)
        v_hat = v2 / (1.0 - ADAM_B2 ** ADAM_STEP)
        delta = -ADAM_LR * (m_hat / (jnp.sqrt(v_hat) + ADAM_EPS) + ADAM_WD * w_ref[...])
        for r, val in zip(refs[n_g:], ((g,) if emit_g else ()) + (delta, m2, v2)):
            r[...] = val

    return pl.pallas_call(
        body, name=name, grid=(R // tm,),
        in_specs=[lead] * 3 + [pl.BlockSpec((tm, C), lambda i: (i, 0))] * n_g, out_specs=[lead] * n_o,
        out_shape=[jax.ShapeDtypeStruct((1, R, C), F32)] * n_o,
        compiler_params=_params(("parallel",)),
    )(w, m, v, *gs)


def _rtile(r, cap=512):
    if r <= cap:
        return r
    best = None
    for t in range(8, cap + 1, 8):
        if r % t == 0:
            best = t
    assert best is not None, r
    return best


ANY = pl.BlockSpec(memory_space=pl.ANY)


def _where_am_i():
    x, y, c = lax.axis_index("x"), lax.axis_index("y"), lax.axis_index("c")
    ks = (2 * x + y, 2 * (1 - x) + y, 2 * x + (1 - y), 2 * (1 - x) + (1 - y))
    return x, y, c, ks


def _rc(src, dst, ssem, rsem, to):
    return pltpu.make_async_remote_copy(src_ref=src, dst_ref=dst, send_sem=ssem, recv_sem=rsem,
                                        device_id=to, device_id_type=MESH)


def _gather_shards(srcs):
    n = len(srcs)

    def body(*refs):
        _gather_body(refs[:n], refs[n:2 * n], *refs[2 * n:], extra=())

    return pl.pallas_call(
        body, name="gather_shards",
        in_specs=[ANY] * n, out_specs=[ANY] * n,
        out_shape=[jax.ShapeDtypeStruct((4,) + s.shape, s.dtype) for s in srcs],
        scratch_shapes=[pltpu.SemaphoreType.DMA((n, 6)), pltpu.SemaphoreType.DMA((n, 6)), pltpu.SemaphoreType.DMA((n,))],
        compiler_params=pltpu.CompilerParams(has_side_effects=True),
    )(*srcs)


def _handshake_all():
    x, y, c = lax.axis_index("x"), lax.axis_index("y"), lax.axis_index("c")
    barrier = pltpu.get_barrier_semaphore()
    for dx, dy, dc in itertools.product((0, 1), repeat=3):
        if (dx, dy, dc) != (0, 0, 0):
            to = (1 - x if dx else x, 1 - y if dy else y, 1 - c if dc else c)
            pl.semaphore_signal(barrier, inc=1, device_id=to, device_id_type=MESH)
    pl.semaphore_wait(barrier, 7)


SEQUENCER = dict(axis_name="seq", num_cores=1)


def _gather_shards_async(srcs, token, name, collective_id):
    n = len(srcs)

    def body(*refs):
        tok, tok_out = refs[n], refs[2 * n + 1]
        ssem, rsem, lsem = refs[2 * n + 2:]
        _handshake_all()
        _gather_body(refs[:n], refs[n + 1:2 * n + 1], ssem, rsem, lsem,
                     extra=(pltpu.make_async_copy(tok, tok_out, lsem.at[n]),))

    res = pl.kernel(
        body, name=name,
        out_type=[jax.ShapeDtypeStruct((4,) + s.shape, s.dtype) for s in srcs]
        + [jax.ShapeDtypeStruct(token.shape, token.dtype)],
        mesh=plsc.ScalarSubcoreMesh(**SEQUENCER),
        scratch_types=[pltpu.SemaphoreType.DMA((n, 6)), pltpu.SemaphoreType.DMA((n, 6)), pltpu.SemaphoreType.DMA((n + 1,))],
        compiler_params=pltpu.CompilerParams(collective_id=collective_id),
    )(*srcs, token)
    return res[:n], res[n]


def _gather_body(src, out, ssem, rsem, lsem, extra):
        n = len(src)
        x, y, c, (k_me, k_x, k_y, k_d) = _where_am_i()
        sib = (x, y, 1 - c)
        local = [pltpu.make_async_copy(src[i], out[i].at[k_me], lsem.at[i]) for i in range(n)] + list(extra)
        for cp in local:
            cp.start()

        def own(i, q, to):
            return _rc(src[i], out[i].at[k_me], ssem.at[i, q], rsem.at[i, q], to)

        def slab(i, q, k, to):
            return _rc(out[i].at[k], out[i].at[k], ssem.at[i, q], rsem.at[i, q], to)

        @pl.when(c == 1)
        def _():
            sends = [own(i, 0, (1 - x, y, 1)) for i in range(n)]
            for cp in sends:
                cp.start()
            fwd = [slab(i, 3, k_x, sib) for i in range(n)]
            for i in range(n):
                slab(i, 0, k_x, sib).wait_recv()
                fwd[i].start()
            for i in range(n):
                slab(i, 4, k_y, sib).wait_recv()
                slab(i, 5, k_d, sib).wait_recv()
            for cp in sends + fwd:
                cp.wait_send()

        @pl.when(c == 0)
        def _():
            sends = [own(i, 1, (x, 1 - y, 0)) for i in range(n)] + [own(i, 2, (1 - x, 1 - y, 0)) for i in range(n)]
            for cp in sends:
                cp.start()
            fwd = [slab(i, 4, k_y, sib) for i in range(n)] + [slab(i, 5, k_d, sib) for i in range(n)]
            for i in range(n):
                slab(i, 1, k_y, sib).wait_recv()
                fwd[i].start()
            for i in range(n):
                slab(i, 2, k_d, sib).wait_recv()
                fwd[n + i].start()
            for i in range(n):
                slab(i, 3, k_x, sib).wait_recv()
            for cp in sends + fwd:
                cp.wait_send()

        for cp in local:
            cp.wait()


def _swap_slabs(arrs, n_slabs, name, async_id, token):
    n = len(arrs)
    J = max(n_slabs, 1)

    def body(*refs):
        src, out = refs[:n], refs[n:2 * n]
        ssem, rsem = refs[2 * n:]
        x, y, c, (k_me, k_x, k_y, k_d) = _where_am_i()
        sib = (x, y, 1 - c)
        sel = (jnp.where(c == 1, k_y, k_me), jnp.where(c == 1, k_d, k_x))
        cps = []
        for i in range(n):
            for j in range(J):
                s = src[i].at[sel[j]] if n_slabs else src[i]
                d = out[i].at[j] if n_slabs else out[i]
                cps.append(_rc(s, d, ssem.at[i, j], rsem.at[i, j], sib))
        for cp in cps:
            cp.start()
        for cp in cps:
            cp.wait()

    return _exchange_call(
        body, arrs, [jax.ShapeDtypeStruct(((n_slabs,) if n_slabs else ()) + a.shape[-2:], a.dtype) for a in arrs],
        [pltpu.SemaphoreType.DMA((n, J)), pltpu.SemaphoreType.DMA((n, J))], name, async_id, token)


def _exchange_call(body, arrs, out_shape, sems, name, collective_id, token):
    n, m = len(arrs), len(out_shape)

    def seq_body(*refs):
        tok, tok_out, tok_sem = refs[n], refs[n + 1 + m], refs[-1]
        _handshake_all()
        cp = pltpu.make_async_copy(tok, tok_out, tok_sem)
        cp.start()
        body(*refs[:n], *refs[n + 1:n + 1 + m], *refs[n + 2 + m:-1])
        cp.wait()

    res = pl.kernel(
        seq_body, name=name, out_type=list(out_shape) + [jax.ShapeDtypeStruct(token.shape, token.dtype)],
        mesh=plsc.ScalarSubcoreMesh(**SEQUENCER), scratch_types=list(sems) + [pltpu.SemaphoreType.DMA],
        compiler_params=pltpu.CompilerParams(collective_id=collective_id))(*arrs, token)
    return res[:m], res[m]


def _send_chip_sums(arrs, name, async_id, token):
    n = len(arrs)

    def body(*refs):
        src, out = refs[:n], refs[n:2 * n]
        ssem, rsem = refs[2 * n:]
        x, y, c, _ = _where_am_i()

        @pl.when(c == 1)
        def _():
            cps = [_rc(src[i].at[1], out[i].at[0], ssem.at[i, 0], rsem.at[i, 0], (1 - x, y, 1)) for i in range(n)]
            for cp in cps:
                cp.start()
            for cp in cps:
                cp.wait()

        @pl.when(c == 0)
        def _():
            cps = [_rc(src[i].at[0], out[i].at[0], ssem.at[i, 0], rsem.at[i, 0], (x, 1 - y, 0)) for i in range(n)]
            cps += [_rc(src[i].at[1], out[i].at[1], ssem.at[i, 1], rsem.at[i, 1], (1 - x, 1 - y, 0)) for i in range(n)]
            for cp in cps:
                cp.start()
            for cp in cps:
                cp.wait()

    return _exchange_call(
        body, arrs, [jax.ShapeDtypeStruct(a.shape, a.dtype) for a in arrs],
        [pltpu.SemaphoreType.DMA((n, 2)), pltpu.SemaphoreType.DMA((n, 2))], name, async_id, token)


def _chip_sum(g, recv, sel, name):
    _, r, C = g.shape
    tr = _rtile(r)

    def body(sel_ref, g_ref, r_ref, o32_ref, o16_ref):
        s = g_ref[...] + r_ref[...]
        o32_ref[...] = s
        o16_ref[...] = s.astype(BF16)

    blk = pl.BlockSpec((None, tr, C), lambda j, t, sel_ref: (j, t, 0))
    return pl.pallas_call(
        body, name=name,
        grid_spec=pltpu.PrefetchScalarGridSpec(
            num_scalar_prefetch=1, grid=(2, r // tr),
            in_specs=[pl.BlockSpec((None, tr, C), lambda j, t, sel_ref: (sel_ref[j], t, 0)), blk],
            out_specs=[blk, blk]),
        out_shape=[jax.ShapeDtypeStruct((2, r, C), F32), jax.ShapeDtypeStruct((2, r, C), BF16)],
        compiler_params=_params(("arbitrary", "arbitrary")),
    )(sel, g, recv)


def _cross_sum(s32, got, north, after, name):
    _, r, C = s32.shape
    tr = _rtile(r)

    def body(f_ref, p_ref, a_ref, b_ref, after_ref, o_ref):
        a = a_ref[...].astype(F32)

        @pl.when(f_ref[0] == 1)
        def _():
            o_ref[...] = p_ref[...] + a

        @pl.when(f_ref[0] == 0)
        def _():
            o_ref[...] = a + b_ref[...].astype(F32)

    return pl.pallas_call(
        body, name=name,
        grid_spec=pltpu.PrefetchScalarGridSpec(
            num_scalar_prefetch=1, grid=(r // tr,),
            in_specs=[pl.BlockSpec((None, tr, C), lambda t, f: (0, t, 0)),
                      pl.BlockSpec((None, tr, C), lambda t, f: (0, t, 0)),
                      pl.BlockSpec((None, tr, C), lambda t, f: (1 - f[0], t, 0)), ANY],
            out_specs=pl.BlockSpec((tr, C), lambda t, f: (t, 0))),
        out_shape=jax.ShapeDtypeStruct((r, C), F32),
        compiler_params=_params(("arbitrary",)),
    )(north, s32, got, got, after)


class _Reducer:
    def __init__(self, token):
        self.token, self.pending, self.pairs = token, None, {}

    def begin(self, tag, ids, grads, after):
        names = list(grads)
        arrs = [grads[n].reshape(4, -1, D_MODEL) for n in names]
        x, y, c, (k_me, k_x, k_y, k_d) = _where_am_i()
        sel = jnp.stack([jnp.where(c == 1, k_me, k_y), jnp.where(c == 1, k_x, k_d)]).astype(jnp.int32)
        from_sib, self.token = _swap_slabs(arrs, 2, f"swap_grad_slabs_{tag}", ids[0], self.token)
        sums = [_chip_sum(g, r, sel, f"chip_sum_{tag}{i}") for i, (g, r) in enumerate(zip(arrs, from_sib))]
        self.finish(after)
        got, self.token = _send_chip_sums([s16 for _, s16 in sums], f"send_chip_sums_{tag}", ids[1], self.token)
        self.pending = (tag, ids[2], names, sums, got)
        return tuple(s16 for _, s16 in sums)

    def finish(self, after):
        if self.pending is None:
            return
        tag, cid, names, sums, got = self.pending
        north = jnp.reshape(lax.axis_index("c"), (1,)).astype(jnp.int32)
        parts = [_cross_sum(s32, b, north, after, f"cross_sum_{tag}{i}") for i, ((s32, _), b) in enumerate(zip(sums, got))]
        theirs, self.token = _swap_slabs(parts, 0, f"swap_parts_{tag}", cid, self.token)
        self.pairs.update(zip(names, zip(parts, theirs)))
        self.pending = None


def _all_reduce_small(v, token, collective_id):
    R, C = v.shape
    tr = R

    def body(v_ref, buf, ssem, rsem, lsem):
        x, y, c = lax.axis_index("x"), lax.axis_index("y"), lax.axis_index("c")
        me, sib = (x, y, c), (x, y, 1 - c)
        chips = [(1 - x, y), (x, 1 - y), (1 - x, 1 - y)]

        def slot(px, py, pc):
            return buf.at[4 * px + 2 * py + pc]

        def copy(k, block, to, src=None):
            return _rc(slot(*block) if src is None else src, slot(*block), ssem.at[k], rsem.at[k], to)

        mine = pltpu.make_async_copy(v_ref, slot(*me), lsem)
        mine.start()
        first = [copy(0, me, sib, src=v_ref)] + [copy(1 + j, me, (*chip, c), src=v_ref) for j, chip in enumerate(chips)]
        for cp in first:
            cp.start()
        passed = [copy(4 + j, (*chip, c), sib) for j, chip in enumerate(chips)]
        for j, chip in enumerate(chips):
            copy(1 + j, (*chip, c), me).wait_recv()
            passed[j].start()
        copy(0, sib, me).wait_recv()
        for j, chip in enumerate(chips):
            copy(4 + j, (*chip, 1 - c), me).wait_recv()
        for cp in first + passed:
            cp.wait_send()
        mine.wait()

    (gathered,), token = _exchange_call(
        body, [v], [jax.ShapeDtypeStruct((8, R, C), F32)],
        [pltpu.SemaphoreType.DMA((7,)), pltpu.SemaphoreType.DMA((7,)), pltpu.SemaphoreType.DMA],
        "gather_small", collective_id, token)

    def add(g_ref, o_ref):
        acc = g_ref[0]
        for k in range(1, 8):
            acc = acc + g_ref[k]
        o_ref[...] = acc

    total = pl.pallas_call(
        add, name="sum_small", grid=(R // tr,),
        in_specs=[pl.BlockSpec((8, tr, C), lambda i: (0, i, 0))], out_specs=pl.BlockSpec((tr, C), lambda i: (i, 0)),
        out_shape=jax.ShapeDtypeStruct((R, C), F32), compiler_params=_params(("parallel",)),
    )(gathered)
    return total, token


WEIGHTS = ['ffn1_norm', 'ffn1_w_gate', 'ffn1_w_up', 'ffn1_w_down', 'mix_norm', 'w_in', 'conv_w', 'conv_b', 's5_A_re',
           's5_A_im', 's5_log_dt', 's5_B_re', 's5_B_im', 's5_C_re', 's5_C_im', 's5_D', 's5_w_glu', 's5_b_glu',
           'ssd_A_log', 'ssd_dt_bias', 'ssd_D', 'ssd_norm', 'w_proj_s5', 'w_proj_ssd', 'b_gate', 'w_out', 'ffn2_norm',
           'ffn2_w_gate', 'ffn2_w_up', 'ffn2_w_down', 'final_norm']
ARGS = ['x'] + WEIGHTS + ['loss_target'] + ['m_' + n for n in WEIGHTS] + ['v_' + n for n in WEIGHTS]
COL_SHARDED = ('ffn1_w_gate', 'ffn1_w_up', 'w_in', 'ffn2_w_gate', 'ffn2_w_up')
ROW_SHARDED = ('ffn1_w_down', 's5_w_glu', 'w_proj_s5', 'w_proj_ssd', 'w_out', 'ffn2_w_down')
MATRICES = COL_SHARDED + ROW_SHARDED
GATHER_FIRST = ('ffn1_w_gate', 'ffn1_w_up')
GATHER_FIRST_LATER = ('ffn1_w_down',)
GATHER_SECOND = ('w_in', 'conv_w', 's5_w_glu', 'w_proj_s5', 'w_proj_ssd', 'w_out')
GATHER_THIRD = ('ffn2_w_gate', 'ffn2_w_up', 'ffn2_w_down')
SMALL = [n for n in WEIGHTS if n not in MATRICES]


def _pack(arrs, width=1024):
    flat = jnp.concatenate([a.reshape(-1).astype(F32) for a in arrs])
    rows = -(-flat.shape[0] // (8 * width)) * 8
    return jnp.pad(flat, (0, rows * width - flat.shape[0])).reshape(rows, width)


def _unpack(packed, shapes):
    flat, out, o = packed.reshape(-1), [], 0
    for s in shapes:
        n = math.prod(s)
        out.append(flat[o:o + n].reshape(s))
        o += n
    return out


def kernel(x, ffn1_norm, ffn1_w_gate, ffn1_w_up, ffn1_w_down, mix_norm, w_in, conv_w, conv_b, s5_A_re, s5_A_im, s5_log_dt, s5_B_re, s5_B_im, s5_C_re, s5_C_im, s5_D, s5_w_glu, s5_b_glu, ssd_A_log, ssd_dt_bias, ssd_D, ssd_norm, w_proj_s5, w_proj_ssd, b_gate, w_out, ffn2_norm, ffn2_w_gate, ffn2_w_up, ffn2_w_down, final_norm, loss_target, m_ffn1_norm, m_ffn1_w_gate, m_ffn1_w_up, m_ffn1_w_down, m_mix_norm, m_w_in, m_conv_w, m_conv_b, m_s5_A_re, m_s5_A_im, m_s5_log_dt, m_s5_B_re, m_s5_B_im, m_s5_C_re, m_s5_C_im, m_s5_D, m_s5_w_glu, m_s5_b_glu, m_ssd_A_log, m_ssd_dt_bias, m_ssd_D, m_ssd_norm, m_w_proj_s5, m_w_proj_ssd, m_b_gate, m_w_out, m_ffn2_norm, m_ffn2_w_gate, m_ffn2_w_up, m_ffn2_w_down, m_final_norm, v_ffn1_norm, v_ffn1_w_gate, v_ffn1_w_up, v_ffn1_w_down, v_mix_norm, v_w_in, v_conv_w, v_conv_b, v_s5_A_re, v_s5_A_im, v_s5_log_dt, v_s5_B_re, v_s5_B_im, v_s5_C_re, v_s5_C_im, v_s5_D, v_s5_w_glu, v_s5_b_glu, v_ssd_A_log, v_ssd_dt_bias, v_ssd_D, v_ssd_norm, v_w_proj_s5, v_w_proj_ssd, v_b_gate, v_w_out, v_ffn2_norm, v_ffn2_w_gate, v_ffn2_w_up, v_ffn2_w_down, v_final_norm):
    a = dict(locals())
    assert list(a) == ARGS
    x, target = a['x'][0], a['loss_target'][0]
    k_me = 2 * lax.axis_index("x") + lax.axis_index("y")

    src = {n: a[n][0].T.astype(BF16) for n in COL_SHARDED}
    src.update({n: a[n][0].astype(BF16) for n in ROW_SHARDED})
    src['conv_w'] = a['conv_w'][0]
    def flat(f):
        return f.reshape(-1, f.shape[-1])

    full = _gather_shards([src[n] for n in GATHER_FIRST])
    w = {n: flat(f) for n, f in zip(GATHER_FIRST, full)}
    got, token = _gather_shards_async(
        [src[n] for n in GATHER_FIRST_LATER], full[0][0, :8, :128].astype(F32), "gather_ffn1_down", 14)
    w.update({n: flat(f) for n, f in zip(GATHER_FIRST_LATER, got)})
    reducer = _Reducer(token)

    def gather_later(stage, after):
        token = reducer.token + after[:8, :128].astype(F32)
        calls = {"mixer": ((GATHER_SECOND[:2], "gather_mixer_in", 1), (GATHER_SECOND[2:], "gather_mixer", 12)),
                 "ffn2": ((GATHER_THIRD, "gather_ffn2", 2),)}[stage]
        out = {}
        for names, name, cid in calls:
            got, token = _gather_shards_async([src[n] for n in names], token, name, cid)
            out.update({n: flat(f) for n, f in zip(names, got)})
        reducer.token = token
        if 'conv_w' in out:
            out['conv_w'] = out['conv_w'].reshape(4, CONV_K, -1).transpose(1, 0, 2).reshape(CONV_K, CONV_DIM)
        return out

    p = {n: a[n] for n in SMALL if n != 'conv_w'}
    p['final_norm'] = a['final_norm'][None]
    ids = {"ffn2": (3, 4, 5), "mixer": (6, 7, 8), "ffn1": (9, 10, 11)}
    loss, grad_x, gp, conv_w_grad = _local_step(
        x, target, p, w, gather_later, lambda tag, grads, after: reducer.begin(tag, ids[tag], grads, after))
    gp['conv_w'] = conv_w_grad[None]
    small_shapes = [(1, 1)] + [gp[n].shape if n != 'final_norm' else (1, D_MODEL) for n in SMALL]
    summed, reducer.token = _all_reduce_small(_pack([loss] + [gp[n] for n in SMALL]), reducer.token, 13)
    red = _unpack(summed, small_shapes)
    loss_all = red[0].reshape(())
    gsmall = dict(zip(SMALL, red[1:]))
    gsmall['conv_w'] = lax.dynamic_slice_in_dim(gsmall['conv_w'], k_me * 1024, 1024, axis=2)
    gsmall = {n: g.reshape(a[n].shape) for n, g in gsmall.items()}

    grads, delta, new_m, new_v = {}, {}, {}, {}

    def update(names):
        for n in names:
            turn = (lambda t: jnp.swapaxes(t, 1, 2)) if n in COL_SHARDED else (lambda t: t)
            out = _adamw(turn(a[n]), turn(a['m_' + n]), turn(a['v_' + n]), list(reducer.pairs[n]),
                         name=f"adamw_{n}", emit_g=True)
            grads[n], delta[n], new_m[n], new_v[n] = [turn(t) for t in out]

    update([n for n in MATRICES if n not in GATHER_FIRST + GATHER_FIRST_LATER])
    reducer.finish(delta['w_in'])
    update(GATHER_FIRST + GATHER_FIRST_LATER)
    sw, sm, sv, sg = [_pack([t[n] for n in SMALL])[None] for t in (a, {n: a['m_' + n] for n in SMALL},
                                                                   {n: a['v_' + n] for n in SMALL}, gsmall)]
    d, m2, v2 = _adamw(sw, sm, sv, [sg[0]], name="adamw_small", emit_g=False)
    shapes = [a[n].shape for n in SMALL]
    for n, dd, mm, vv in zip(SMALL, _unpack(d[0], shapes), _unpack(m2[0], shapes), _unpack(v2[0], shapes)):
        grads[n], delta[n], new_m[n], new_v[n] = gsmall[n], dd, mm, vv
    return (loss_all, grad_x[None], *[grads[n] for n in WEIGHTS], *[delta[n] for n in WEIGHTS],
            *[new_m[n] for n in WEIGHTS], *[new_v[n] for n in WEIGHTS])
```
